```python
import math
import jax, jax.numpy as jnp
from jax import lax
import numpy as np

D_MODEL = 1024
BATCH = 32
SEQ = 256
DEPTH = 4
DEC_BATCH = 8
DEC_SEQ = 1024
PAST_LEN = 512

GRID_W = 64
HEAD_DIM = 64
DIFF_HEADS = 4
SWA_HEADS = 8
SWA_KV_HEADS = 2
SWA_GROUP = SWA_HEADS // SWA_KV_HEADS
WINDOW = 128
MLA_HEADS = 16
MLA_Q_RANK = 384
MLA_KV_RANK = 256
MLA_NOPE_DIM = 64
MLA_ROPE_DIM = 32
MLA_V_DIM = 64
N_EXPERTS = 16
EXPERT_FF = 512
EC_FACTOR = 2
QBLK = 128
ROPE_BASE = 10000.0
EPS = 1e-6
NEG_INF = -1e30
N_EVEN = (DEPTH + 1) // 2
N_ODD = DEPTH // 2
DIFF_W = DIFF_HEADS * 2 * HEAD_DIM
SWA_QW = SWA_HEADS * HEAD_DIM
SWA_KW = SWA_KV_HEADS * HEAD_DIM
EVEN_IN = 3 * DIFF_W + SWA_QW + 2 * SWA_KW
EVEN_OUT = DIFF_W + SWA_QW
ODD_IN = MLA_Q_RANK + MLA_KV_RANK + MLA_ROPE_DIM
ODD_OUT = MLA_HEADS * MLA_V_DIM

kernel_name = "hybrid_diffusion_prefix_trunk_step"


def _rmsnorm(x, g):
    xf = x.astype(jnp.float32)
    y = xf * lax.rsqrt(jnp.mean(xf * xf, axis=-1, keepdims=True) + EPS)
    return (y * g.astype(jnp.float32)).astype(x.dtype)


def _modulate(x, g, shift, scale):
    return _rmsnorm(x, g) * (1 + scale) + shift


def _ada(cvec, w, b):
    mod = (jax.nn.silu(cvec) @ w + b)[:, None, :]
    return jnp.split(mod, 6, axis=-1)


def _rope_1d(x, pos):
    half = x.shape[-1] // 2
    inv = ROPE_BASE ** (-(jnp.arange(half, dtype=jnp.float32) / half))
    ang = pos.astype(jnp.float32)[:, None] * inv[None, :]
    cos = jnp.cos(ang)[:, None, :].astype(x.dtype)
    sin = jnp.sin(ang)[:, None, :].astype(x.dtype)
    x1, x2 = x[..., :half], x[..., half:]
    return jnp.concatenate([x1 * cos - x2 * sin, x1 * sin + x2 * cos], axis=-1)


def _rope_2d(x):
    s = x.shape[1]
    rows = s // GRID_W
    row = jnp.repeat(jnp.arange(rows), GRID_W)
    col = jnp.tile(jnp.arange(GRID_W), rows)
    h = x.shape[-1] // 2
    return jnp.concatenate([_rope_1d(x[..., :h], row), _rope_1d(x[..., h:], col)], axis=-1)


def _map_query_blocks(fn, q):
    b, s = q.shape[0], q.shape[1]
    nb = s // QBLK
    qb = jnp.moveaxis(q.reshape((b, nb, QBLK) + q.shape[2:]), 1, 0)
    out = lax.map(fn, qb)
    out = jnp.moveaxis(out, 0, 1)
    return out.reshape((b, s) + out.shape[3:])


def _lambda_init(layer):
    return 0.8 - 0.6 * math.exp(-0.3 * layer)


def _diff_attend(q, k, v, lam, subln, lam_init):
    b, s = q.shape[:2]
    lamf = lam.astype(jnp.float32)
    lam_full = jnp.exp(jnp.sum(lamf[0] * lamf[1])) - jnp.exp(jnp.sum(lamf[2] * lamf[3])) + lam_init
    scale = HEAD_DIM ** -0.5

    def blk(qb):
        sc = jnp.einsum('bqhcd,bkhcd->bhcqk', qb, k).astype(jnp.float32) * scale
        p = jax.nn.softmax(sc, axis=-1)
        a = p[:, :, 0] - lam_full * p[:, :, 1]
        return jnp.einsum('bhqk,bkhe->bqhe', a.astype(v.dtype), v)

    o = _map_query_blocks(blk, q)
    o = _rmsnorm(o, subln) * (1 - lam_init)
    return o.reshape(b, s, -1)


def _sink_softmax(sc, sink):
    sk = sink.astype(jnp.float32).reshape(SWA_KV_HEADS, SWA_GROUP, 1, 1)
    m = jnp.maximum(jnp.max(sc, axis=-1, keepdims=True), sk)
    e = jnp.exp(sc - m)
    return e / (jnp.sum(e, axis=-1, keepdims=True) + jnp.exp(sk - m))


def _sink_attend(q, k, v, sink):
    b, s = q.shape[:2]
    scale = HEAD_DIM ** -0.5

    def blk(qb):
        sc = jnp.einsum('bqhgd,bkhd->bhgqk', qb, k).astype(jnp.float32) * scale
        p = _sink_softmax(sc, sink)
        return jnp.einsum('bhgqk,bkhd->bqhgd', p.astype(v.dtype), v)

    return _map_query_blocks(blk, q).reshape(b, s, -1)


def _band_blocks(t):
    b, s = t.shape[:2]
    nb = s // WINDOW
    tp = jnp.pad(t, ((0, 0), (WINDOW, WINDOW), (0, 0), (0, 0))).reshape((b, nb + 2, WINDOW) + t.shape[2:])
    return jnp.concatenate([tp[:, :-2], tp[:, 1:-1], tp[:, 2:]], axis=2)


def _swa_latent(q, k, v, k_ctx, v_ctx, sink):
    b, s = q.shape[:2]
    nb = s // WINDOW
    lc = k_ctx.shape[1]
    scale = HEAD_DIM ** -0.5
    qb = q.reshape(b, nb, WINDOW, SWA_KV_HEADS, SWA_GROUP, HEAD_DIM)
    kb, vb = _band_blocks(k), _band_blocks(v)
    s_loc = jnp.einsum('bnqhgd,bnkhd->bnhgqk', qb, kb).astype(jnp.float32) * scale
    s_ctx = jnp.einsum('bnqhgd,bkhd->bnhgqk', qb, k_ctx).astype(jnp.float32) * scale
    start = jnp.arange(nb)[:, None, None] * WINDOW
    qpos = start + jnp.arange(WINDOW)[None, :, None]
    kpos = start - WINDOW + jnp.arange(3 * WINDOW)[None, None, :]
    valid = (jnp.abs(qpos - kpos) <= WINDOW) & (kpos >= 0) & (kpos < s)
    s_loc = jnp.where(valid[None, :, None, None], s_loc, NEG_INF)
    p = _sink_softmax(jnp.concatenate([s_ctx, s_loc], axis=-1), sink).astype(v.dtype)
    o = (jnp.einsum('bnhgqk,bkhd->bnqhgd', p[..., :lc], v_ctx)
         + jnp.einsum('bnhgqk,bnkhd->bnqhgd', p[..., lc:], vb))
    return o.reshape(b, s, -1)


def _even_project(h, w_in):
    b, s = h.shape[:2]
    qa, ka, va, qb, kb, vb = jnp.split(
        h @ w_in, [DIFF_W, 2 * DIFF_W, 3 * DIFF_W, 3 * DIFF_W + SWA_QW, 3 * DIFF_W + SWA_QW + SWA_KW], axis=-1)
    return (qa.reshape(b, s, 2 * DIFF_HEADS, HEAD_DIM), ka.reshape(b, s, 2 * DIFF_HEADS, HEAD_DIM),
            va.reshape(b, s, DIFF_HEADS, 2 * HEAD_DIM), qb.reshape(b, s, SWA_HEADS, HEAD_DIM),
            kb.reshape(b, s, SWA_KV_HEADS, HEAD_DIM), vb.reshape(b, s, SWA_KV_HEADS, HEAD_DIM))


def _even_ctx(h, w_in, w_out, lam, subln, sink, lam_init):
    b, l = h.shape[:2]
    qa, ka, va, qb, kb, vb = _even_project(h, w_in)
    o_a = _diff_attend(qa.reshape(b, l, DIFF_HEADS, 2, HEAD_DIM), ka.reshape(b, l, DIFF_HEADS, 2, HEAD_DIM),
                       va, lam, subln, lam_init)
    o_b = _sink_attend(qb.reshape(b, l, SWA_KV_HEADS, SWA_GROUP, HEAD_DIM), kb, vb, sink)
    out = jnp.concatenate([o_a, o_b], axis=-1) @ w_out
    return out, ka.reshape(b, l, DIFF_HEADS, 2 * HEAD_DIM), va, kb, vb


def _even_lat(h, kd_ctx, vd_ctx, ks_ctx, vs_ctx, w_in, w_out, lam, subln, sink, lam_init):
    b, s = h.shape[:2]
    lc = kd_ctx.shape[1]
    qa, ka, va, qb, kb, vb = _even_project(h, w_in)
    qa, ka, qb, kb = _rope_2d(qa), _rope_2d(ka), _rope_2d(qb), _rope_2d(kb)
    k_all = jnp.concatenate([kd_ctx.reshape(b, lc, DIFF_HEADS, 2, HEAD_DIM),
                             ka.reshape(b, s, DIFF_HEADS, 2, HEAD_DIM)], axis=1)
    v_all = jnp.concatenate([vd_ctx, va], axis=1)
    o_a = _diff_attend(qa.reshape(b, s, DIFF_HEADS, 2, HEAD_DIM), k_all, v_all, lam, subln, lam_init)
    o_b = _swa_latent(qb.reshape(b, s, SWA_KV_HEADS, SWA_GROUP, HEAD_DIM), kb, vb, ks_ctx, vs_ctx, sink)
    return jnp.concatenate([o_a, o_b], axis=-1) @ w_out


def _mla_project(h, w_in, q_norm, w_q_up, kv_norm):
    b, s = h.shape[:2]
    cq, ckv, kr = jnp.split(h @ w_in, [MLA_Q_RANK, MLA_Q_RANK + MLA_KV_RANK], axis=-1)
    q = (_rmsnorm(cq, q_norm) @ w_q_up).reshape(b, s, MLA_HEADS, MLA_NOPE_DIM + MLA_ROPE_DIM)
    return q, _rmsnorm(ckv, kv_norm), kr


def _mla_keys(ckv, kr, w_kv_up):
    b, n = ckv.shape[:2]
    kv = (ckv @ w_kv_up).reshape(b, n, MLA_HEADS, MLA_NOPE_DIM + MLA_V_DIM)
    k_nope, v = kv[..., :MLA_NOPE_DIM], kv[..., MLA_NOPE_DIM:]
    k_rope = jnp.broadcast_to(kr[:, :, None, :], (b, n, MLA_HEADS, MLA_ROPE_DIM))
    return jnp.concatenate([k_nope, k_rope], axis=-1), v


def _mla_attend(q, k, v):
    b, s = q.shape[:2]
    scale = (MLA_NOPE_DIM + MLA_ROPE_DIM) ** -0.5

    def blk(qb):
        sc = jnp.einsum('bqhd,bkhd->bhqk', qb, k).astype(jnp.float32) * scale
        p = jax.nn.softmax(sc, axis=-1)
        return jnp.einsum('bhqk,bkhd->bqhd', p.astype(v.dtype), v)

    return _map_query_blocks(blk, q).reshape(b, s, -1)


def _odd_ctx(h, w_in, q_norm, w_q_up, kv_norm, w_kv_up, w_out):
    q, ckv, kr = _mla_project(h, w_in, q_norm, w_q_up, kv_norm)
    k, v = _mla_keys(ckv, kr, w_kv_up)
    return _mla_attend(q, k, v) @ w_out, ckv, kr


def _odd_lat(h, ckv_ctx, kr_ctx, w_in, q_norm, w_q_up, kv_norm, w_kv_up, w_out):
    q, ckv, kr = _mla_project(h, w_in, q_norm, w_q_up, kv_norm)
    q = jnp.concatenate([q[..., :MLA_NOPE_DIM], _rope_2d(q[..., MLA_NOPE_DIM:])], axis=-1)
    kr = _rope_2d(kr[:, :, None, :])[:, :, 0, :]
    k, v = _mla_keys(jnp.concatenate([ckv_ctx, ckv], axis=1), jnp.concatenate([kr_ctx, kr], axis=1), w_kv_up)
    return _mla_attend(q, k, v) @ w_out


def _moe(h, w_router, w_gate, w_up, w_down):
    b, s, d = h.shape
    n = b * s
    cap = EC_FACTOR * n // N_EXPERTS
    xt = h.reshape(n, d)
    aff = jax.nn.softmax((xt @ w_router).astype(jnp.float32), axis=-1)
    gate, idx = lax.top_k(aff.T, cap)
    xe = xt[idx]
    hid = jax.nn.silu(jnp.einsum('ecd,edf->ecf', xe, w_gate)) * jnp.einsum('ecd,edf->ecf', xe, w_up)
    ye = jnp.einsum('ecf,efd->ecd', hid, w_down) * gate[..., None].astype(h.dtype)
    out = jnp.zeros((n, d), h.dtype).at[idx.reshape(-1)].add(ye.reshape(-1, d))
    return out.reshape(b, s, d)


def _normal(k, shape, scale):
    return jax.random.normal(k, shape, jnp.float32) * scale


def setup_inputs(seed: int = 0) -> dict:
    key = jax.random.key(seed)
    ks = jax.random.split(key, 32)
    D = D_MODEL
    gain = lambda k, shape: 1.0 + _normal(k, shape, 0.02)
    return {
        "x_prompt": _normal(ks[0], (BATCH, SEQ, D), 1.0),
        "x_sample": _normal(ks[1], (DEC_BATCH, DEC_SEQ, D), 1.0),
        "cache_diff_k": _normal(ks[2], (DEC_BATCH, N_EVEN, PAST_LEN, DIFF_HEADS, 2 * HEAD_DIM), 1.0),
        "cache_diff_v": _normal(ks[3], (DEC_BATCH, N_EVEN, PAST_LEN, DIFF_HEADS, 2 * HEAD_DIM), 1.0),
        "cache_swa_k": _normal(ks[4], (DEC_BATCH, N_EVEN, PAST_LEN, SWA_KV_HEADS, HEAD_DIM), 1.0),
        "cache_swa_v": _normal(ks[5], (DEC_BATCH, N_EVEN, PAST_LEN, SWA_KV_HEADS, HEAD_DIM), 1.0),
        "cache_mla_ckv": _normal(ks[6], (DEC_BATCH, N_ODD, PAST_LEN, MLA_KV_RANK), 1.0),
        "cache_mla_krope": _normal(ks[7], (DEC_BATCH, N_ODD, PAST_LEN, MLA_ROPE_DIM), 1.0),
        "c": _normal(ks[8], (DEC_BATCH, D), 1.0),
        "c_ctx": _normal(ks[9], (D,), 1.0),
        "w_ada": _normal(ks[10], (DEPTH, D, 6 * D), 0.5 * D ** -0.5),
        "b_ada": _normal(ks[11], (DEPTH, 6 * D), 0.02),
        "norm_mix": gain(ks[12], (DEPTH, D)),
        "norm_ffn": gain(ks[13], (DEPTH, D)),
        "w_in_even": _normal(ks[14], (N_EVEN, D, EVEN_IN), D ** -0.5),
        "w_out_even": _normal(ks[15], (N_EVEN, EVEN_OUT, D), EVEN_OUT ** -0.5),
        "diff_lambda": _normal(ks[16], (N_EVEN, 4, HEAD_DIM), 0.1),
        "diff_subln": gain(ks[17], (N_EVEN, 2 * HEAD_DIM)),
        "swa_sink": _normal(ks[18], (N_EVEN, SWA_HEADS), 0.5),
        "w_in_odd": _normal(ks[19], (N_ODD, D, ODD_IN), D ** -0.5),
        "mla_q_norm": gain(ks[20], (N_ODD, MLA_Q_RANK)),
        "w_q_up": _normal(ks[21], (N_ODD, MLA_Q_RANK, MLA_HEADS * (MLA_NOPE_DIM + MLA_ROPE_DIM)), MLA_Q_RANK ** -0.5),
        "mla_kv_norm": gain(ks[22], (N_ODD, MLA_KV_RANK)),
        "w_kv_up": _normal(ks[23], (N_ODD, MLA_KV_RANK, MLA_HEADS * (MLA_NOPE_DIM + MLA_V_DIM)), MLA_KV_RANK ** -0.5),
        "w_out_odd": _normal(ks[24], (N_ODD, ODD_OUT, D), ODD_OUT ** -0.5),
        "w_router": _normal(ks[25], (DEPTH, D, N_EXPERTS), D ** -0.5),
        "w_gate_exp": _normal(ks[26], (DEPTH, N_EXPERTS, D, EXPERT_FF), D ** -0.5),
        "w_up_exp": _normal(ks[27], (DEPTH, N_EXPERTS, D, EXPERT_FF), D ** -0.5),
        "w_down_exp": _normal(ks[28], (DEPTH, N_EXPERTS, EXPERT_FF, D), EXPERT_FF ** -0.5),
        "final_norm": gain(ks[29], (D,)),
    }


def reference(x_prompt, x_sample, cache_diff_k, cache_diff_v, cache_swa_k, cache_swa_v,
              cache_mla_ckv, cache_mla_krope, c, c_ctx, w_ada, b_ada, norm_mix, norm_ffn,
              w_in_even, w_out_even, diff_lambda, diff_subln, swa_sink, w_in_odd, mla_q_norm,
              w_q_up, mla_kv_norm, w_kv_up, w_out_odd, w_router, w_gate_exp, w_up_exp,
              w_down_exp, final_norm):
    xc = x_prompt
    diff_k, diff_v, swa_k, swa_v, mla_ckv, mla_kr = [], [], [], [], [], []
    for i in range(DEPTH):
        j = i // 2
        sh1, sc1, g1, sh2, sc2, g2 = _ada(c_ctx[None, :], w_ada[i], b_ada[i])
        h = _modulate(xc, norm_mix[i], sh1, sc1)
        if i % 2 == 0:
            out, kd, vd, ksw, vsw = _even_ctx(h, w_in_even[j], w_out_even[j], diff_lambda[j],
                                              diff_subln[j], swa_sink[j], _lambda_init(i))
            diff_k.append(kd); diff_v.append(vd); swa_k.append(ksw); swa_v.append(vsw)
        else:
            out, ckv, kr = _odd_ctx(h, w_in_odd[j], mla_q_norm[j], w_q_up[j], mla_kv_norm[j],
                                    w_kv_up[j], w_out_odd[j])
            mla_ckv.append(ckv); mla_kr.append(kr)
        xc = xc + g1 * out
        h = _modulate(xc, norm_ffn[i], sh2, sc2)
        xc = xc + g2 * _moe(h, w_router[i], w_gate_exp[i], w_up_exp[i], w_down_exp[i])
    y_prompt = _rmsnorm(xc, final_norm)
    new_diff_k = jnp.stack(diff_k, axis=1)
    new_diff_v = jnp.stack(diff_v, axis=1)
    new_swa_k = jnp.stack(swa_k, axis=1)
    new_swa_v = jnp.stack(swa_v, axis=1)
    new_mla_ckv = jnp.stack(mla_ckv, axis=1)
    new_mla_krope = jnp.stack(mla_kr, axis=1)

    xs = x_sample
    for i in range(DEPTH):
        j = i // 2
        sh1, sc1, g1, sh2, sc2, g2 = _ada(c, w_ada[i], b_ada[i])
        h = _modulate(xs, norm_mix[i], sh1, sc1)
        if i % 2 == 0:
            out = _even_lat(h, cache_diff_k[:, j], cache_diff_v[:, j], cache_swa_k[:, j], cache_swa_v[:, j],
                            w_in_even[j], w_out_even[j], diff_lambda[j], diff_subln[j], swa_sink[j],
                            _lambda_init(i))
        else:
            out = _odd_lat(h, cache_mla_ckv[:, j], cache_mla_krope[:, j], w_in_odd[j], mla_q_norm[j],
                           w_q_up[j], mla_kv_norm[j], w_kv_up[j], w_out_odd[j])
        xs = xs + g1 * out
        h = _modulate(xs, norm_ffn[i], sh2, sc2)
        xs = xs + g2 * _moe(h, w_router[i], w_gate_exp[i], w_up_exp[i], w_down_exp[i])
    y_sample = _rmsnorm(xs, final_norm)

    return (y_prompt, y_sample, new_diff_k, new_diff_v, new_swa_k, new_swa_v, new_mla_ckv, new_mla_krope)
```

```python
import functools
import math

import jax
import jax.numpy as jnp
from jax import lax
from jax.experimental import pallas as pl
from jax.experimental.pallas import tpu as pltpu

F32 = jnp.float32
BF16 = jnp.bfloat16
I32 = jnp.int32

D = 1024
HD = 64
GRID_W = 64
WINDOW = 128
DIFF_HEADS = 4
SWA_HEADS = 8
MLA_HEADS = 16
MLA_Q_RANK = 384
MLA_KV_RANK = 256
MLA_ROPE = 32
N_EXPERTS = 16
EXPERT_FF = 512
EC_FACTOR = 2
ROPE_BASE = 10000.0
EPS = 1e-6
NEG_INF = -1e30
EVEN_IN = 2304
LANES = 128
TM = 256
TQ = 128
TB = 256
SEG = 8
VMEM_LIMIT = 56 * 1024 * 1024


def _cparams(sem, vmem=VMEM_LIMIT):
    return pltpu.CompilerParams(dimension_semantics=sem, vmem_limit_bytes=vmem)


def _dot(a, b):
    return jnp.dot(a, b, preferred_element_type=F32)


def _dot_nt(a, b):
    return lax.dot_general(a, b, (((1,), (1,)), ((), ())), preferred_element_type=F32)


def _silu(x):
    return x / (1.0 + jnp.exp(-x))


def _rms(x, g):
    ms = jnp.mean(x * x, axis=-1, keepdims=True)
    return x * lax.rsqrt(ms + EPS) * g


def _modulate(x, g, shift, scale):
    return _rms(x, g) * (1.0 + scale) + shift


def _lambda_init(layer):
    return 0.8 - 0.6 * math.exp(-0.3 * layer)


def _ada_kernel(c_ref, w_ref, b_ref, o_ref):
    s = _silu(c_ref[...]).astype(BF16)
    o_ref[...] = _dot(s, w_ref[...].astype(BF16)) + b_ref[...]


def _ada_call(cvec, w_ada, b_ada):
    depth, _, n6 = w_ada.shape
    rc = cvec.shape[0]
    tn = 512
    return pl.pallas_call(
        _ada_kernel,
        grid=(depth, n6 // tn),
        in_specs=[pl.BlockSpec((rc, D), lambda i, n: (0, 0)),
                  pl.BlockSpec((None, D, tn), lambda i, n: (i, 0, n)),
                  pl.BlockSpec((None, 1, tn), lambda i, n: (i, 0, n))],
        out_specs=pl.BlockSpec((None, rc, tn), lambda i, n: (i, 0, n)),
        out_shape=jax.ShapeDtypeStruct((depth, rc, n6), F32),
        compiler_params=_cparams(("parallel", "parallel")),
        name="ada",
    )(cvec, w_ada, b_ada.reshape(depth, 1, n6))


def _norm_mod_kernel(x_ref, mod_ref, g_ref, h_ref):
    h_ref[...] = _modulate(x_ref[...], g_ref[...], mod_ref[0], mod_ref[1]).astype(BF16)


def _norm_mod_call(x, mod, g, modrow):
    t = x.shape[0]
    return pl.pallas_call(
        _norm_mod_kernel,
        grid=(t // TM,),
        in_specs=[pl.BlockSpec((TM, D), lambda r: (r, 0)),
                  pl.BlockSpec((None, 6, 1, D), lambda r: (modrow(r), 0, 0, 0)),
                  pl.BlockSpec((1, D), lambda r: (0, 0))],
        out_specs=pl.BlockSpec((TM, D), lambda r: (r, 0)),
        out_shape=jax.ShapeDtypeStruct((t, D), BF16),
        compiler_params=_cparams(("parallel",)),
        name="norm_mod",
    )(x, mod, g)


def _rope_tables(dec_seq, half, lane_of_dim):
    pos = jnp.arange(dec_seq)
    row = (pos // GRID_W).astype(F32)
    col = (pos % GRID_W).astype(F32)
    inv = ROPE_BASE ** (-(jnp.arange(half, dtype=F32) / half))
    ang = jnp.stack([row[:, None] * inv[None, :], col[:, None] * inv[None, :]])
    cos, sin = jnp.cos(ang), jnp.sin(ang)
    c_cols, s1_cols, s2_cols = [], [], []
    one, zero = jnp.ones((dec_seq,), F32), jnp.zeros((dec_seq,), F32)
    for lane in range(LANES):
        info = lane_of_dim(lane)
        if info is None:
            c_cols.append(one); s1_cols.append(zero); s2_cols.append(zero)
            continue
        axis, k, second = info
        c_cols.append(cos[axis, :, k])
        if second:
            s1_cols.append(zero); s2_cols.append(sin[axis, :, k])
        else:
            s1_cols.append(-sin[axis, :, k]); s2_cols.append(zero)
    return (jnp.stack(c_cols, axis=1), jnp.stack(s1_cols, axis=1), jnp.stack(s2_cols, axis=1))


def _even_lane(lane):
    j = lane % HD
    axis, jj = j // 32, j % 32
    return axis, jj % 16, jj >= 16


def _mla_lane(lane):
    if lane < 64 or lane >= 96:
        return None
    jj = lane - 64
    axis, k = jj // 16, jj % 16
    return axis, k % 8, k >= 8


def _rope(x, c, s1, s2, shift):
    return x * c + pltpu.roll(x, LANES - shift, 1) * s1 + pltpu.roll(x, shift, 1) * s2


_EVEN_ROPE_TILES = tuple(range(0, 8)) + tuple(range(12, 17))
_EVEN_Q_TILES = tuple(range(0, 4)) + tuple(range(12, 16))


def _proj_even_kernel(*refs, rope, caches):
    h_ref, w_ref = refs[0], refs[1]
    pos = 2
    if rope:
        c_ref, s1_ref, s2_ref = refs[2:5]
        pos = 5
    qkv_ref = refs[pos]
    res = _dot(h_ref[...], w_ref[...])
    scale = HD ** -0.5
    for t in range(EVEN_IN // LANES):
        x = res[:, t * LANES:(t + 1) * LANES]
        if rope and t in _EVEN_ROPE_TILES:
            x = _rope(x, c_ref[...], s1_ref[...], s2_ref[...], 16)
        if t in _EVEN_Q_TILES:
            x = x * scale
        qkv_ref[:, t * LANES:(t + 1) * LANES] = x.astype(BF16)
    if caches:
        kv_ref = refs[pos + 1]
        kv_ref[:, 0:1024] = res[:, 512:1536]
        kv_ref[:, 1024:1280] = res[:, 2048:2304]


def _proj_even_call(h, w, row0, nrows, tables, dec_seq):
    rope = tables is not None
    caches = not rope
    t0 = row0 // TM
    in_specs = [pl.BlockSpec((TM, D), lambda r: (t0 + r, 0)),
                pl.BlockSpec((D, EVEN_IN), lambda r: (0, 0))]
    args = [h, w]
    if rope:
        per = dec_seq // TM
        for _ in range(3):
            in_specs.append(pl.BlockSpec((TM, LANES), lambda r: (r % per, 0)))
        args += list(tables)
    out_specs = [pl.BlockSpec((TM, EVEN_IN), lambda r: (r, 0))]
    out_shape = [jax.ShapeDtypeStruct((nrows, EVEN_IN), BF16)]
    if caches:
        out_specs.append(pl.BlockSpec((TM, 1280), lambda r: (r, 0)))
        out_shape.append(jax.ShapeDtypeStruct((nrows, 1280), F32))
    return pl.pallas_call(
        functools.partial(_proj_even_kernel, rope=rope, caches=caches),
        grid=(nrows // TM,),
        in_specs=in_specs, out_specs=out_specs, out_shape=out_shape,
        compiler_params=_cparams(("parallel",)),
        name="proj_even_lat" if rope else "proj_even_ctx",
    )(*args)


ODD_IN_PAD = MLA_Q_RANK + MLA_KV_RANK + LANES
MLA_QW = MLA_HEADS * LANES


def _proj_odd_kernel(*refs, rope):
    h_ref, w_ref, qn_ref, wq_ref, kvn_ref = refs[:5]
    pos = 5
    if rope:
        c_ref, s1_ref, s2_ref = refs[5:8]
        pos = 8
    q_ref, ckv_ref, kr_ref = refs[pos:pos + 3]
    res = _dot(h_ref[...], w_ref[...])
    cq = _rms(res[:, :MLA_Q_RANK], qn_ref[...]).astype(BF16)
    ckv_ref[...] = _rms(res[:, MLA_Q_RANK:MLA_Q_RANK + MLA_KV_RANK], kvn_ref[...])
    kr = res[:, MLA_Q_RANK + MLA_KV_RANK:]
    if rope:
        kr = _rope(kr, c_ref[...], s1_ref[...], s2_ref[...], 8)
    kr_ref[...] = kr
    q = _dot(cq, wq_ref[...])
    scale = (HD + MLA_ROPE) ** -0.5
    for t in range(MLA_HEADS):
        x = q[:, t * LANES:(t + 1) * LANES]
        if rope:
            x = _rope(x, c_ref[...], s1_ref[...], s2_ref[...], 8)
        q_ref[:, t * LANES:(t + 1) * LANES] = (x * scale).astype(BF16)


def _proj_odd_call(h, w_in, qn, wq, kvn, row0, nrows, tables, dec_seq):
    rope = tables is not None
    t0 = row0 // TM
    in_specs = [pl.BlockSpec((TM, D), lambda r: (t0 + r, 0)),
                pl.BlockSpec((D, ODD_IN_PAD), lambda r: (0, 0)),
                pl.BlockSpec((1, MLA_Q_RANK), lambda r: (0, 0)),
                pl.BlockSpec((MLA_Q_RANK, MLA_QW), lambda r: (0, 0)),
                pl.BlockSpec((1, MLA_KV_RANK), lambda r: (0, 0))]
    args = [h, w_in, qn, wq, kvn]
    if rope:
        per = dec_seq // TM
        for _ in range(3):
            in_specs.append(pl.BlockSpec((TM, LANES), lambda r: (r % per, 0)))
        args += list(tables)
    return pl.pallas_call(
        functools.partial(_proj_odd_kernel, rope=rope),
        grid=(nrows // TM,),
        in_specs=in_specs,
        out_specs=[pl.BlockSpec((TM, MLA_QW), lambda r: (r, 0)),
                   pl.BlockSpec((TM, MLA_KV_RANK), lambda r: (r, 0)),
                   pl.BlockSpec((TM, LANES), lambda r: (r, 0))],
        out_shape=[jax.ShapeDtypeStruct((nrows, MLA_QW), BF16),
                   jax.ShapeDtypeStruct((nrows, MLA_KV_RANK), F32),
                   jax.ShapeDtypeStruct((nrows, LANES), F32)],
        compiler_params=_cparams(("parallel",)),
        name="proj_odd_lat" if rope else "proj_odd_ctx",
    )(*args)


def _attn_even_kernel(*refs, seq, past, lam_init):
    latent = past > 0
    n = past + seq
    if latent:
        (qkv_ref, ck_ref, cv_ref, sk_ref, sv_ref, lam_ref, subln_ref, sink_ref,
         o_ref, kd, vd, ka, vl, vh) = refs
    else:
        qkv_ref, lam_ref, subln_ref, sink_ref, o_ref, kd, vd, ka, vl, vh = refs
    qi = pl.program_id(1)
    lo = lax.broadcasted_iota(I32, (1, LANES), 1) < HD

    @pl.when(qi == 0)
    def _build():
        chunk = 256
        for c0 in range(0, n, chunk):
            rows = slice(c0, c0 + chunk)
            if c0 < past:
                prow = slice(c0, c0 + chunk)
                kd[rows, :] = ck_ref[prow, :].astype(BF16)
                vd[rows, :] = cv_ref[prow, :].astype(BF16)
                kt = sk_ref[prow, :]
                vt = sv_ref[prow, :]
            else:
                orow = slice(c0 - past, c0 - past + chunk)
                kd[rows, :] = qkv_ref[orow, 512:1024]
                vd[rows, :] = qkv_ref[orow, 1024:1536]
                kt = qkv_ref[orow, 2048:2176].astype(F32)
                vt = qkv_ref[orow, 2176:2304].astype(F32)
            kr = pltpu.roll(kt, HD, 1)
            vr = pltpu.roll(vt, HD, 1)
            ka[0, rows, :] = jnp.where(lo, kt, kr).astype(BF16)
            ka[1, rows, :] = jnp.where(lo, kr, kt).astype(BF16)
            vl[0, rows, :] = jnp.where(lo, vt, 0.0).astype(BF16)
            vh[0, rows, :] = jnp.where(lo, 0.0, vr).astype(BF16)
            vl[1, rows, :] = jnp.where(lo, vr, 0.0).astype(BF16)
            vh[1, rows, :] = jnp.where(lo, 0.0, vt).astype(BF16)

    r0 = pl.multiple_of(qi * TQ, TQ)
    lam = lam_ref[...]
    lam_full = (jnp.exp(jnp.sum(lam[0:1] * lam[1:2], axis=-1, keepdims=True))
                - jnp.exp(jnp.sum(lam[2:3] * lam[3:4], axis=-1, keepdims=True)) + lam_init)
    zero_b = jnp.zeros((), BF16)

    for h in range(DIFF_HEADS):
        cs = slice(h * LANES, (h + 1) * LANES)
        qt = qkv_ref[pl.ds(r0, TQ), cs]
        kh = kd[:, cs]
        es, rs = [], []
        for comp in range(2):
            qc = jnp.where(lo, qt, zero_b) if comp == 0 else jnp.where(lo, zero_b, qt)
            s = _dot_nt(qc, kh)
            m = jnp.max(s, axis=-1, keepdims=True)
            e = jnp.exp(s - m)
            es.append(e)
            rs.append(1.0 / jnp.sum(e, axis=-1, keepdims=True))
        a = es[0] * rs[0] - es[1] * (lam_full * rs[1])
        o = _dot(a.astype(BF16), vd[:, cs])
        o = _rms(o, subln_ref[...]) * (1.0 - lam_init)
        o_ref[:, cs] = o.astype(BF16)

    nblk = seq // WINDOW
    dense = past if latent else seq
    if latent:
        rr = lax.broadcasted_iota(I32, (TQ, WINDOW), 0)
        cc = lax.broadcasted_iota(I32, (TQ, WINDOW), 1)
        band = {-1: jnp.logical_and(rr <= cc, qi > 0), 0: None,
                1: jnp.logical_and(cc <= rr, qi < nblk - 1)}
        starts = {d: pl.multiple_of(past + jnp.clip(qi + d, 0, nblk - 1) * WINDOW, WINDOW)
                  for d in (-1, 0, 1)}
    for i in range(SWA_HEADS // 2):
        hk = i // 2
        cs = slice(1536 + i * LANES, 1536 + (i + 1) * LANES)
        qt = qkv_ref[pl.ds(r0, TQ), cs]
        acc = jnp.zeros((TQ, LANES), F32)
        for half in range(2):
            qc = jnp.where(lo, qt, zero_b) if half == 0 else jnp.where(lo, zero_b, qt)
            vsel = vl if half == 0 else vh
            sink = sink_ref[2 * i + half]
            parts = [_dot_nt(qc, ka[hk, 0:dense, :])]
            if latent:
                for d in (-1, 0, 1):
                    s = _dot_nt(qc, ka[hk, pl.ds(starts[d], WINDOW), :])
                    if band[d] is not None:
                        s = jnp.where(band[d], s, NEG_INF)
                    parts.append(s)
            s_all = jnp.concatenate(parts, axis=1) if len(parts) > 1 else parts[0]
            m = jnp.maximum(jnp.max(s_all, axis=-1, keepdims=True), sink)
            e = jnp.exp(s_all - m)
            r = 1.0 / (jnp.sum(e, axis=-1, keepdims=True) + jnp.exp(sink - m))
            eb = e.astype(BF16)
            o = _dot(eb[:, 0:dense], vsel[hk, 0:dense, :])
            if latent:
                for k, d in enumerate((-1, 0, 1)):
                    o += _dot(eb[:, dense + k * WINDOW:dense + (k + 1) * WINDOW],
                              vsel[hk, pl.ds(starts[d], WINDOW), :])
            acc += o * r
        o_ref[:, 512 + i * LANES:512 + (i + 1) * LANES] = acc.astype(BF16)


def _attn_even_call(qkv, caches, lam, subln, sink, nbatch, seq, past, lam_init):
    n = past + seq
    latent = past > 0
    in_specs = [pl.BlockSpec((seq, EVEN_IN), lambda b, q: (b, 0))]
    args = [qkv]
    if latent:
        ck, cv, sk, sv = caches
        in_specs += [pl.BlockSpec((None, past, 512), lambda b, q: (b, 0, 0)),
                     pl.BlockSpec((None, past, 512), lambda b, q: (b, 0, 0)),
                     pl.BlockSpec((None, past, LANES), lambda b, q: (b, 0, 0)),
                     pl.BlockSpec((None, past, LANES), lambda b, q: (b, 0, 0))]
        args += [ck, cv, sk, sv]
    in_specs += [pl.BlockSpec((4, HD), lambda b, q: (0, 0)),
                 pl.BlockSpec((1, 2 * HD), lambda b, q: (0, 0)),
                 pl.BlockSpec(memory_space=pltpu.SMEM)]
    args += [lam, subln, sink]
    return pl.pallas_call(
        functools.partial(_attn_even_kernel, seq=seq, past=past, lam_init=lam_init),
        grid=(nbatch, seq // TQ),
        in_specs=in_specs,
        out_specs=pl.BlockSpec((TQ, D), lambda b, q: (b * (seq // TQ) + q, 0)),
        out_shape=jax.ShapeDtypeStruct((nbatch * seq, D), BF16),
        scratch_shapes=[pltpu.VMEM((n, 512), BF16), pltpu.VMEM((n, 512), BF16),
                        pltpu.VMEM((2, n, LANES), BF16), pltpu.VMEM((2, n, LANES), BF16),
                        pltpu.VMEM((2, n, LANES), BF16)],
        compiler_params=_cparams(("arbitrary", "arbitrary")),
        name="attn_even_lat" if latent else "attn_even_ctx",
    )(*args)


def _attn_odd_kernel(*refs, seq, past):
    latent = past > 0
    n = past + seq
    if latent:
        q_ref, ckv_ref, kr_ref, cckv_ref, ckr_ref, wk_ref, wv_ref, o_ref, kf, vlo, vhi = refs
    else:
        q_ref, ckv_ref, kr_ref, wk_ref, wv_ref, o_ref, kf, vlo, vhi = refs
    qi = pl.program_id(1)
    lo = lax.broadcasted_iota(I32, (1, LANES), 1) < HD

    @pl.when(qi == 0)
    def _build():
        chunk = 256
        for c0 in range(0, n, chunk):
            rows = slice(c0, c0 + chunk)
            if c0 < past:
                ckv = cckv_ref[c0:c0 + chunk, :].astype(BF16)
                kr = ckr_ref[c0:c0 + chunk, :]
            else:
                ckv = ckv_ref[c0 - past:c0 - past + chunk, :].astype(BF16)
                kr = kr_ref[c0 - past:c0 - past + chunk, :]
            kk = _dot(ckv, wk_ref[...])
            for h in range(MLA_HEADS):
                cs = slice(h * LANES, (h + 1) * LANES)
                kf[rows, cs] = (kk[:, cs] + kr).astype(BF16)
            vv = _dot(ckv, wv_ref[...])
            for i in range(MLA_HEADS // 2):
                cs = slice(i * LANES, (i + 1) * LANES)
                vlo[rows, cs] = jnp.where(lo, vv[:, cs], 0.0).astype(BF16)
                vhi[rows, cs] = jnp.where(lo, 0.0, vv[:, cs]).astype(BF16)

    r0 = pl.multiple_of(qi * TQ, TQ)
    for i in range(MLA_HEADS // 2):
        acc = jnp.zeros((TQ, LANES), F32)
        for half in range(2):
            h = 2 * i + half
            cs = slice(h * LANES, (h + 1) * LANES)
            s = _dot_nt(q_ref[pl.ds(r0, TQ), cs], kf[:, cs])
            m = jnp.max(s, axis=-1, keepdims=True)
            e = jnp.exp(s - m)
            r = 1.0 / jnp.sum(e, axis=-1, keepdims=True)
            vsel = vlo if half == 0 else vhi
            acc += _dot(e.astype(BF16), vsel[:, i * LANES:(i + 1) * LANES]) * r
        o_ref[:, i * LANES:(i + 1) * LANES] = acc.astype(BF16)


def _attn_odd_call(q, ckv, kr, caches, wk, wv, nbatch, seq, past):
    n = past + seq
    latent = past > 0
    in_specs = [pl.BlockSpec((seq, MLA_QW), lambda b, qq: (b, 0)),
                pl.BlockSpec((seq, MLA_KV_RANK), lambda b, qq: (b, 0)),
                pl.BlockSpec((seq, LANES), lambda b, qq: (b, 0))]
    args = [q, ckv, kr]
    if latent:
        in_specs += [pl.BlockSpec((None, past, MLA_KV_RANK), lambda b, qq: (b, 0, 0)),
                     pl.BlockSpec((None, past, LANES), lambda b, qq: (b, 0, 0))]
        args += list(caches)
    in_specs += [pl.BlockSpec((MLA_KV_RANK, MLA_QW), lambda b, qq: (0, 0)),
                 pl.BlockSpec((MLA_KV_RANK, D), lambda b, qq: (0, 0))]
    args += [wk, wv]
    return pl.pallas_call(
        functools.partial(_attn_odd_kernel, seq=seq, past=past),
        grid=(nbatch, seq // TQ),
        in_specs=in_specs,
        out_specs=pl.BlockSpec((TQ, D), lambda b, qq: (b * (seq // TQ) + qq, 0)),
        out_shape=jax.ShapeDtypeStruct((nbatch * seq, D), BF16),
        scratch_shapes=[pltpu.VMEM((n, MLA_QW), BF16), pltpu.VMEM((n, D), BF16),
                        pltpu.VMEM((n, D), BF16)],
        compiler_params=_cparams(("arbitrary", "arbitrary")),
        name="attn_odd_lat" if latent else "attn_odd_ctx",
    )(*args)


def _outproj_kernel(x_ref, o_ref, w_ref, mod_ref, g_ref, xo_ref, h_ref):
    x = x_ref[...] + mod_ref[2] * _dot(o_ref[...], w_ref[...])
    xo_ref[...] = x
    h_ref[...] = _modulate(x, g_ref[...], mod_ref[3], mod_ref[4]).astype(BF16)


def _outproj_call(x, o, w, mod, g, modrow):
    t = x.shape[0]
    return pl.pallas_call(
        _outproj_kernel,
        grid=(t // TM,),
        in_specs=[pl.BlockSpec((TM, D), lambda r: (r, 0)),
                  pl.BlockSpec((TM, D), lambda r: (r, 0)),
                  pl.BlockSpec((D, D), lambda r: (0, 0)),
                  pl.BlockSpec((None, 6, 1, D), lambda r: (modrow(r), 0, 0, 0)),
                  pl.BlockSpec((1, D), lambda r: (0, 0))],
        out_specs=[pl.BlockSpec((TM, D), lambda r: (r, 0)),
                   pl.BlockSpec((TM, D), lambda r: (r, 0))],
        out_shape=[jax.ShapeDtypeStruct((t, D), F32), jax.ShapeDtypeStruct((t, D), BF16)],
        compiler_params=_cparams(("parallel",)),
        name="outproj",
    )(x, o, w, mod, g)


def _router_kernel(h_ref, w_ref, rowid_ref, gate_ref, tab_ref, aff_sc, *, nb, cap):
    b = pl.program_id(1)
    ne = N_EXPERTS
    logits = _dot(h_ref[...], w_ref[...])
    lane = lax.broadcasted_iota(I32, (TB, LANES), 1)
    lg = jnp.where(lane < ne, logits, -jnp.inf)
    e = jnp.exp(lg - jnp.max(lg, axis=-1, keepdims=True))
    aff = e / jnp.sum(e, axis=-1, keepdims=True)
    aff_sc[pl.ds(pl.multiple_of(b * ne, ne), ne), :] = aff.T[0:ne, :]

    @pl.when(b == nb - 1)
    def _select():
        nr = nb * ne
        a = aff_sc[...]
        bits = pltpu.bitcast(a, I32)
        ri = lax.broadcasted_iota(I32, (nr, nr), 0)
        ci = lax.broadcasted_iota(I32, (nr, nr), 1)
        same_e = (ri & (ne - 1)) == (ci & (ne - 1))
        same_b = (ri >> 4) == (ci >> 4)
        m_e = jnp.where(same_e, 1.0, 0.0).astype(BF16)
        m_b = jnp.where(same_b, 1.0, 0.0).astype(BF16)
        m_a = jnp.where(jnp.logical_and(same_e, ci < ri), 1.0, 0.0).astype(BF16)
        m_o = jnp.where(jnp.logical_and(same_b, ci < ri), 1.0, 0.0).astype(BF16)
        ui = lax.broadcasted_iota(I32, (TB, TB), 0)
        uj = lax.broadcasted_iota(I32, (TB, TB), 1)
        upper = jnp.where(ui < uj, 1.0, 0.0).astype(BF16)

        def rows_to_lanes(col):
            return jnp.broadcast_to(col, (nr, LANES)).astype(BF16)

        def count_ge(cand):
            c = jnp.sum(jnp.where(bits >= cand, 1.0, 0.0), axis=-1, keepdims=True)
            return _dot(m_e, rows_to_lanes(c))[:, 0:1]

        def bisect(i, v):
            cand = v | jnp.left_shift(jnp.int32(1), 30 - i)
            return jnp.where(count_ge(cand) >= cap, cand, v)

        thr = lax.fori_loop(0, 31, bisect, jnp.zeros((nr, 1), I32))
        gt = jnp.where(bits > thr, 1.0, 0.0)
        eq = jnp.where(bits == thr, 1.0, 0.0)
        n_gt = _dot(m_e, rows_to_lanes(jnp.sum(gt, axis=-1, keepdims=True)))[:, 0:1]
        need = cap - n_gt
        eq_before = (_dot(m_a, rows_to_lanes(jnp.sum(eq, axis=-1, keepdims=True)))[:, 0:1]
                     + _dot(eq.astype(BF16), upper))
        sel = jnp.where(jnp.logical_and(eq > 0.0, eq_before < need), 1.0, gt)
        local = _dot(sel.astype(BF16), upper)
        cnt = jnp.sum(sel, axis=-1, keepdims=True)
        seg = jnp.floor((cnt + (SEG - 1)) * (1.0 / SEG)) * SEG
        segb = rows_to_lanes(seg)
        off_stack = _dot(m_o, segb)[:, 0:1]
        off_buf = _dot(m_a, segb)[:, 0:1]
        rows_blk = _dot(m_b, segb)[:, 0:1]
        rows_exp = _dot(m_e, segb)[:, 0:1]
        rowid = jnp.where(sel > 0.0, off_stack + local, -1.0).astype(I32)
        gate = jnp.where(sel > 0.0, a, 0.0)
        for bb in range(nb):
            rowid_ref[bb] = rowid[bb * ne:(bb + 1) * ne, :]
            gate_ref[bb] = gate[bb * ne:(bb + 1) * ne, :]
        tl = lax.broadcasted_iota(I32, (nr, LANES), 1)
        tab = jnp.where(tl == 0, seg, jnp.where(tl == 1, off_stack, jnp.where(
            tl == 2, off_buf, jnp.where(tl == 3, rows_blk, rows_exp))))
        tab_ref[...] = tab.astype(I32)


def _router_call(h, w_router_pad, ngroups, ntok):
    nb = ntok // TB
    cap = EC_FACTOR * ntok // N_EXPERTS
    nr = nb * N_EXPERTS
    return pl.pallas_call(
        functools.partial(_router_kernel, nb=nb, cap=cap),
        grid=(ngroups, nb),
        in_specs=[pl.BlockSpec((TB, D), lambda g, b: (g * nb + b, 0)),
                  pl.BlockSpec((D, LANES), lambda g, b: (0, 0))],
        out_specs=[pl.BlockSpec((None, nb, N_EXPERTS, TB), lambda g, b: (g, 0, 0, 0)),
                   pl.BlockSpec((None, nb, N_EXPERTS, TB), lambda g, b: (g, 0, 0, 0)),
                   pl.BlockSpec((None, nr, LANES), lambda g, b: (g, 0, 0))],
        out_shape=[jax.ShapeDtypeStruct((ngroups, nb, N_EXPERTS, TB), I32),
                   jax.ShapeDtypeStruct((ngroups, nb, N_EXPERTS, TB), F32),
                   jax.ShapeDtypeStruct((ngroups, nr, LANES), I32)],
        scratch_shapes=[pltpu.VMEM((nr, TB), F32)],
        compiler_params=_cparams(("arbitrary", "arbitrary")),
        name="router",
    )(h, w_router_pad)


STACK_ROWS = N_EXPERTS * TB


def _onehot_rows(rid, base, values=None):
    ri = lax.broadcasted_iota(I32, (TB, TB), 0) + base
    out = jnp.zeros((TB, TB), F32)
    for e in range(N_EXPERTS):
        hit = rid[e:e + 1, :] == ri
        out = jnp.where(hit, 1.0 if values is None else values[e:e + 1, :], out)
    return out


def _dispatch_kernel(seg_s, offs_s, offb_s, rblk_s, rexp_s, h_ref, rowid_ref, xe_hbm,
                     stack, zbuf, sem, zsem, *, nb):
    g = pl.program_id(0)
    b = pl.program_id(1)
    gb = g * nb + b
    rows = rblk_s[gb]
    rid = rowid_ref[...]
    h = h_ref[...]
    xrows = xe_hbm.shape[2]

    @pl.when(b == 0)
    def _zero_last_tile():
        zbuf[...] = jnp.zeros_like(zbuf)
        for e in range(N_EXPERTS):
            pltpu.make_async_copy(zbuf, xe_hbm.at[g, e, pl.ds(xrows - TB, TB)], zsem).start()

    def chunk(c, carry):
        base = pl.multiple_of(c * TB, TB)
        stack[pl.ds(base, TB), :] = _dot(_onehot_rows(rid, base).astype(BF16), h)
        return carry

    lax.fori_loop(0, (rows + TB - 1) // TB, chunk, 0)

    for e in range(N_EXPERTS):
        k = gb * N_EXPERTS + e
        src0, dst0 = offs_s[k], offb_s[k]

        def start(i, carry, e=e, src0=src0, dst0=dst0):
            pltpu.make_async_copy(
                stack.at[pl.ds(pl.multiple_of(src0 + i * SEG, SEG), SEG)],
                xe_hbm.at[g, e, pl.ds(pl.multiple_of(dst0 + i * SEG, SEG), SEG)], sem).start()
            return carry

        lax.fori_loop(0, seg_s[k] // SEG, start, 0)

    @pl.when(b == 0)
    def _zero_last_tile_done():
        for e in range(N_EXPERTS):
            pltpu.make_async_copy(zbuf, xe_hbm.at[g, e, pl.ds(xrows - TB, TB)], zsem).wait()

    @pl.when(b == nb - 1)
    def _zero_tail():
        zbuf[...] = jnp.zeros_like(zbuf)
        for e in range(N_EXPERTS):
            pltpu.make_async_copy(
                zbuf, xe_hbm.at[g, e, pl.ds(pl.multiple_of(rexp_s[g * N_EXPERTS + e], SEG), TB)],
                zsem).start()
        for e in range(N_EXPERTS):
            pltpu.make_async_copy(zbuf, xe_hbm.at[g, e, pl.ds(0, TB)], zsem).wait()

    def wait(i, carry):
        pltpu.make_async_copy(stack.at[pl.ds(0, SEG)], xe_hbm.at[g, 0, pl.ds(0, SEG)], sem).wait()
        return carry

    lax.fori_loop(0, rows // SEG, wait, 0)


def _expert_rows(ntok):
    cap = EC_FACTOR * ntok // N_EXPERTS
    worst = cap + (ntok // TB) * (SEG - 1)
    tiles = -(-worst // TB)
    assert cap >= (tiles - 1) * TB
    return tiles, (tiles + 1) * TB


def _dispatch_call(tabs, h, rowid, ngroups, ntok):
    nb = ntok // TB
    _, xrows = _expert_rows(ntok)
    grid_spec = pltpu.PrefetchScalarGridSpec(
        num_scalar_prefetch=5,
        grid=(ngroups, nb),
        in_specs=[pl.BlockSpec((TB, D), lambda g, b, *_: (g * nb + b, 0)),
                  pl.BlockSpec((None, None, N_EXPERTS, TB), lambda g, b, *_: (g, b, 0, 0))],
        out_specs=pl.BlockSpec(memory_space=pl.ANY),
        scratch_shapes=[pltpu.VMEM((STACK_ROWS, D), F32), pltpu.VMEM((TB, D), F32),
                        pltpu.SemaphoreType.DMA, pltpu.SemaphoreType.DMA])
    return pl.pallas_call(
        functools.partial(_dispatch_kernel, nb=nb),
        grid_spec=grid_spec,
        out_shape=jax.ShapeDtypeStruct((ngroups, N_EXPERTS, xrows, D), F32),
        compiler_params=_cparams(("arbitrary", "arbitrary")),
        name="dispatch",
    )(*tabs, h, rowid)


def _ffn_kernel(nt_s, xe_ref, wg_ref, wu_ref, wd_ref, y_ref, wgb, wub, wdb):
    e = pl.program_id(0)
    g = pl.program_id(1)
    j = pl.program_id(2)

    @pl.when(jnp.logical_and(g == 0, j == 0))
    def _cast():
        wgb[...] = wg_ref[...].astype(BF16)
        wub[...] = wu_ref[...].astype(BF16)
        wdb[...] = wd_ref[...].astype(BF16)

    live = j < nt_s[g * N_EXPERTS + e]

    @pl.when(live)
    def _run():
        x = xe_ref[...].astype(BF16)
        hid = (_silu(_dot(x, wgb[...])) * _dot(x, wub[...])).astype(BF16)
        y_ref[...] = _dot(hid, wdb[...])

    @pl.when(jnp.logical_not(live))
    def _skip():
        y_ref[...] = jnp.zeros_like(y_ref)


def _ffn_call(ntiles, xe, wg, wu, wd, layer, ngroups, ntok):
    tiles, xrows = _expert_rows(ntok)

    def xmap(e, g, j, nt):
        return (g, e, jnp.minimum(j, nt[g * N_EXPERTS + e] - 1), 0)

    grid_spec = pltpu.PrefetchScalarGridSpec(
        num_scalar_prefetch=1,
        grid=(N_EXPERTS, ngroups, tiles),
        in_specs=[pl.BlockSpec((None, None, TB, D), xmap),
                  pl.BlockSpec((None, None, D, EXPERT_FF), lambda e, g, j, nt: (layer, e, 0, 0)),
                  pl.BlockSpec((None, None, D, EXPERT_FF), lambda e, g, j, nt: (layer, e, 0, 0)),
                  pl.BlockSpec((None, None, EXPERT_FF, D), lambda e, g, j, nt: (layer, e, 0, 0))],
        out_specs=pl.BlockSpec((None, None, TB, D), lambda e, g, j, nt: (g, e, j, 0)),
        scratch_shapes=[pltpu.VMEM((D, EXPERT_FF), BF16), pltpu.VMEM((D, EXPERT_FF), BF16),
                        pltpu.VMEM((EXPERT_FF, D), BF16)])
    return pl.pallas_call(
        _ffn_kernel,
        grid_spec=grid_spec,
        out_shape=jax.ShapeDtypeStruct((ngroups, N_EXPERTS, tiles * TB, D), F32),
        compiler_params=_cparams(("arbitrary", "arbitrary", "arbitrary")),
        name="ffn",
    )(ntiles, xe, wg, wu, wd)


def _combine_kernel(seg_s, offs_s, offb_s, rblk_s, y_hbm, rowid_ref, gate_ref, x_ref, mod_ref,
                    modn_ref, g_ref, xo_ref, h_ref, stack, acc, sem, *, nb, final):
    g = pl.program_id(0)
    b = pl.program_id(1)
    gb = g * nb + b
    rows = rblk_s[gb]

    @pl.when(jnp.logical_and(g == 0, b == 0))
    def _init():
        stack[...] = jnp.zeros_like(stack)

    for e in range(N_EXPERTS):
        k = gb * N_EXPERTS + e
        dst0, src0 = offs_s[k], offb_s[k]

        def start(i, carry, e=e, src0=src0, dst0=dst0):
            pltpu.make_async_copy(
                y_hbm.at[g, e, pl.ds(pl.multiple_of(src0 + i * SEG, SEG), SEG)],
                stack.at[pl.ds(pl.multiple_of(dst0 + i * SEG, SEG), SEG)], sem).start()
            return carry

        lax.fori_loop(0, seg_s[k] // SEG, start, 0)

    def wait(i, carry):
        pltpu.make_async_copy(y_hbm.at[g, 0, pl.ds(0, SEG)], stack.at[pl.ds(0, SEG)], sem).wait()
        return carry

    lax.fori_loop(0, rows // SEG, wait, 0)

    rid = rowid_ref[...]
    gate = gate_ref[...]
    acc[...] = jnp.zeros_like(acc)

    def chunk(c, carry):
        base = pl.multiple_of(c * TB, TB)
        w = _onehot_rows(rid, base, gate).T
        acc[...] += _dot(w.astype(BF16), stack[pl.ds(base, TB), :].astype(BF16))
        return carry

    lax.fori_loop(0, (rows + TB - 1) // TB, chunk, 0)

    x = x_ref[...] + mod_ref[5] * acc[...]
    xo_ref[...] = x
    if final:
        h_ref[...] = _rms(x, g_ref[...])
    else:
        h_ref[...] = _modulate(x, g_ref[...], modn_ref[0], modn_ref[1]).astype(BF16)


def _combine_call(tabs, y, rowid, gate, x, mod, modn, gvec, modrow, ngroups, ntok, final):
    nb = ntok // TB
    t = x.shape[0]
    grid_spec = pltpu.PrefetchScalarGridSpec(
        num_scalar_prefetch=4,
        grid=(ngroups, nb),
        in_specs=[pl.BlockSpec(memory_space=pl.ANY),
                  pl.BlockSpec((None, None, N_EXPERTS, TB), lambda g, b, *_: (g, b, 0, 0)),
                  pl.BlockSpec((None, None, N_EXPERTS, TB), lambda g, b, *_: (g, b, 0, 0)),
                  pl.BlockSpec((TB, D), lambda g, b, *_: (g * nb + b, 0)),
                  pl.BlockSpec((None, 6, 1, D), lambda g, b, *_: (modrow(g * nb + b), 0, 0, 0)),
                  pl.BlockSpec((None, 6, 1, D), lambda g, b, *_: (modrow(g * nb + b), 0, 0, 0)),
                  pl.BlockSpec((1, D), lambda g, b, *_: (0, 0))],
        out_specs=[pl.BlockSpec((TB, D), lambda g, b, *_: (g * nb + b, 0)),
                   pl.BlockSpec((TB, D), lambda g, b, *_: (g * nb + b, 0))],
        scratch_shapes=[pltpu.VMEM((STACK_ROWS, D), F32), pltpu.VMEM((TB, D), F32),
                        pltpu.SemaphoreType.DMA])
    return pl.pallas_call(
        functools.partial(_combine_kernel, nb=nb, final=final),
        grid_spec=grid_spec,
        out_shape=[jax.ShapeDtypeStruct((t, D), F32),
                   jax.ShapeDtypeStruct((t, D), F32 if final else BF16)],
        compiler_params=_cparams(("arbitrary", "arbitrary")),
        name="combine",
    )(*tabs, y, rowid, gate, x, mod, modn, gvec)


def kernel(x_prompt, x_sample, cache_diff_k, cache_diff_v, cache_swa_k, cache_swa_v, cache_mla_ckv, cache_mla_krope, c, c_ctx, w_ada, b_ada, norm_mix, norm_ffn, w_in_even, w_out_even, diff_lambda, diff_subln, swa_sink, w_in_odd, mla_q_norm, w_q_up, mla_kv_norm, w_kv_up, w_out_odd, w_router, w_gate_exp, w_up_exp, w_down_exp, final_norm):
    batch, seq, _ = x_prompt.shape
    dec_batch, dec_seq, _ = x_sample.shape
    past = cache_diff_k.shape[2]
    depth = w_ada.shape[0]
    n_even = w_in_even.shape[0]
    n_odd = w_in_odd.shape[0]
    nc, ns = batch * seq, dec_batch * dec_seq
    assert nc == ns, "the routed-expert kernels take two token groups of equal size"
    assert seq % TM == 0 and dec_seq % TM == 0 and past % 256 == 0 and dec_seq % GRID_W == 0
    ntok = nc

    def modrow(r):
        tok = r * TM
        return jnp.where(tok < nc, 0, 1 + jnp.maximum(tok - nc, 0) // dec_seq)

    rc = -(-(1 + dec_batch) // 16) * 16
    cvec = jnp.zeros((rc, D), F32).at[0].set(c_ctx).at[1:1 + dec_batch].set(c)
    mods = _ada_call(cvec, w_ada, b_ada).reshape(depth, rc, 6, 1, D)

    w_even_b = w_in_even.astype(BF16)
    w_oute_b = w_out_even.astype(BF16)
    w_outo_b = w_out_odd.astype(BF16)
    kr_pad = jnp.zeros((n_odd, D, LANES), F32).at[:, :, 64:96].set(w_in_odd[:, :, 640:672])
    w_odd_b = jnp.concatenate([w_in_odd[:, :, :640], kr_pad], axis=-1).astype(BF16)
    wq = w_q_up.reshape(n_odd, MLA_Q_RANK, MLA_HEADS, HD + MLA_ROPE)
    wq_b = jnp.pad(wq, ((0, 0), (0, 0), (0, 0), (0, LANES - HD - MLA_ROPE))).reshape(
        n_odd, MLA_Q_RANK, MLA_QW).astype(BF16)
    wkv = w_kv_up.reshape(n_odd, MLA_KV_RANK, MLA_HEADS, 2 * HD)
    wk_b = jnp.pad(wkv[..., :HD], ((0, 0), (0, 0), (0, 0), (0, LANES - HD))).reshape(
        n_odd, MLA_KV_RANK, MLA_QW).astype(BF16)
    wv_b = wkv[..., HD:].reshape(n_odd, MLA_KV_RANK, D).astype(BF16)
    w_router_b = jnp.pad(w_router, ((0, 0), (0, 0), (0, LANES - N_EXPERTS))).astype(BF16)
    even_tabs = _rope_tables(dec_seq, 16, _even_lane)
    mla_tabs = _rope_tables(dec_seq, 8, _mla_lane)
    cdk = cache_diff_k.reshape(dec_batch, n_even, past, 512)
    cdv = cache_diff_v.reshape(dec_batch, n_even, past, 512)
    csk = cache_swa_k.reshape(dec_batch, n_even, past, LANES)
    csv = cache_swa_v.reshape(dec_batch, n_even, past, LANES)
    ckr = jnp.zeros((dec_batch, n_odd, past, LANES), F32).at[..., 64:96].set(cache_mla_krope)

    x = jnp.concatenate([x_prompt.reshape(nc, D), x_sample.reshape(ns, D)], axis=0)
    h = _norm_mod_call(x, mods[0], norm_mix[0:1], modrow)
    tiles, _ = _expert_rows(ntok)
    kv_even, ckv_odd, kr_odd = [], [], []
    y_final = None
    for i in range(depth):
        j = i // 2
        if i % 2 == 0:
            qkv_c, kvc = _proj_even_call(h, w_even_b[j], 0, nc, None, dec_seq)
            (qkv_l,) = _proj_even_call(h, w_even_b[j], nc, ns, even_tabs, dec_seq)
            kv_even.append(kvc)
            li = _lambda_init(i)
            o_c = _attn_even_call(qkv_c, None, diff_lambda[j], diff_subln[j:j + 1], swa_sink[j],
                                  batch, seq, 0, li)
            o_l = _attn_even_call(qkv_l, (cdk[:, j], cdv[:, j], csk[:, j], csv[:, j]),
                                  diff_lambda[j], diff_subln[j:j + 1], swa_sink[j],
                                  dec_batch, dec_seq, past, li)
            w_out = w_oute_b[j]
        else:
            q_c, ckv_c, kr_c = _proj_odd_call(h, w_odd_b[j], mla_q_norm[j:j + 1], wq_b[j],
                                              mla_kv_norm[j:j + 1], 0, nc, None, dec_seq)
            q_l, ckv_l, kr_l = _proj_odd_call(h, w_odd_b[j], mla_q_norm[j:j + 1], wq_b[j],
                                              mla_kv_norm[j:j + 1], nc, ns, mla_tabs, dec_seq)
            ckv_odd.append(ckv_c)
            kr_odd.append(kr_c)
            o_c = _attn_odd_call(q_c, ckv_c, kr_c, None, wk_b[j], wv_b[j], batch, seq, 0)
            o_l = _attn_odd_call(q_l, ckv_l, kr_l, (cache_mla_ckv[:, j], ckr[:, j]),
                                 wk_b[j], wv_b[j], dec_batch, dec_seq, past)
            w_out = w_outo_b[j]
        o = jnp.concatenate([o_c, o_l], axis=0)
        x, h2 = _outproj_call(x, o, w_out, mods[i], norm_ffn[i:i + 1], modrow)

        rowid, gate, tab = _router_call(h2, w_router_b[i], 2, ntok)
        seg_t = tab[:, :, 0].reshape(-1)
        offs_t = tab[:, :, 1].reshape(-1)
        offb_t = tab[:, :, 2].reshape(-1)
        rblk_t = tab[:, ::N_EXPERTS, 3].reshape(-1)
        rexp_t = tab[:, :N_EXPERTS, 4].reshape(-1)
        xe = _dispatch_call((seg_t, offs_t, offb_t, rblk_t, rexp_t), h2, rowid, 2, ntok)
        ntile_t = (rexp_t + TB - 1) // TB
        y = _ffn_call(ntile_t, xe, w_gate_exp, w_up_exp, w_down_exp, i, 2, ntok)
        final = i == depth - 1
        nxt = i if final else i + 1
        gvec = final_norm.reshape(1, D) if final else norm_mix[nxt:nxt + 1]
        x, h = _combine_call((seg_t, offs_t, offb_t, rblk_t), y, rowid, gate, x, mods[i],
                             mods[nxt], gvec, modrow, 2, ntok, final)
        if final:
            y_final = h

    y_prompt = y_final[:nc].reshape(batch, seq, D)
    y_sample = y_final[nc:].reshape(dec_batch, dec_seq, D)
    kv = jnp.stack(kv_even, axis=1).reshape(batch, seq, n_even, 1280).transpose(0, 2, 1, 3)
    new_diff_k = kv[..., 0:512].reshape(batch, n_even, seq, DIFF_HEADS, 2 * HD)
    new_diff_v = kv[..., 512:1024].reshape(batch, n_even, seq, DIFF_HEADS, 2 * HD)
    new_swa_k = kv[..., 1024:1152].reshape(batch, n_even, seq, 2, HD)
    new_swa_v = kv[..., 1152:1280].reshape(batch, n_even, seq, 2, HD)
    new_mla_ckv = jnp.stack(ckv_odd, axis=1).reshape(batch, seq, n_odd, MLA_KV_RANK).transpose(0, 2, 1, 3)
    new_mla_krope = jnp.stack(kr_odd, axis=1).reshape(batch, seq, n_odd, LANES).transpose(
        0, 2, 1, 3)[..., 64:96]
    return (y_prompt, y_sample, new_diff_k, new_diff_v, new_swa_k, new_swa_v, new_mla_ckv,
            new_mla_krope)
```

```python
import functools
import math

import jax
import jax.numpy as jnp
from jax import lax
from jax.experimental import pallas as pl
from jax.experimental.pallas import tpu as pltpu

F32 = jnp.float32
BF16 = jnp.bfloat16
I32 = jnp.int32

D = 1024
HD = 64
GRID_W = 64
WINDOW = 128
DIFF_HEADS = 4
SWA_HEADS = 8
MLA_HEADS = 16
MLA_Q_RANK = 384
MLA_KV_RANK = 256
MLA_ROPE = 32
N_EXPERTS = 16
EXPERT_FF = 512
EC_FACTOR = 2
ROPE_BASE = 10000.0
EPS = 1e-6
NEG_INF = -1e30
EVEN_IN = 2304
LANES = 128
TM = 256
TQ = 256
TB = 256
SEG = 16
VMEM_LIMIT = 56 * 1024 * 1024


def _cparams(sem, vmem=VMEM_LIMIT):
    return pltpu.CompilerParams(dimension_semantics=sem, vmem_limit_bytes=vmem)


def _dot(a, b):
    return jnp.dot(a, b, preferred_element_type=F32)


def _dot_nt(a, b):
    return lax.dot_general(a, b, (((1,), (1,)), ((), ())), preferred_element_type=F32)


def _silu(x):
    return x / (1.0 + jnp.exp(-x))


def _rms(x, g):
    ms = jnp.mean(x * x, axis=-1, keepdims=True)
    return x * lax.rsqrt(ms + EPS) * g


def _modulate(x, g, shift, scale):
    return _rms(x, g) * (1.0 + scale) + shift


def _lambda_init(layer):
    return 0.8 - 0.6 * math.exp(-0.3 * layer)


def _ada_kernel(c_ref, w_ref, b_ref, o_ref):
    s = _silu(c_ref[...]).astype(BF16)
    o_ref[...] = _dot(s, w_ref[...].astype(BF16)) + b_ref[...]


def _ada_call(cvec, w_ada, b_ada):
    depth, _, n6 = w_ada.shape
    rc = cvec.shape[0]
    tn = 512
    return pl.pallas_call(
        _ada_kernel,
        grid=(depth, n6 // tn),
        in_specs=[pl.BlockSpec((rc, D), lambda i, n: (0, 0)),
                  pl.BlockSpec((None, D, tn), lambda i, n: (i, 0, n)),
                  pl.BlockSpec((None, 1, tn), lambda i, n: (i, 0, n))],
        out_specs=pl.BlockSpec((None, rc, tn), lambda i, n: (i, 0, n)),
        out_shape=jax.ShapeDtypeStruct((depth, rc, n6), F32),
        compiler_params=_cparams(("parallel", "parallel")),
        name="ada",
    )(cvec, w_ada, b_ada.reshape(depth, 1, n6))


def _norm_mod_kernel(x_ref, mod_ref, g_ref, h_ref):
    h_ref[...] = _modulate(x_ref[...], g_ref[...], mod_ref[0], mod_ref[1]).astype(BF16)


def _norm_mod_call(x, mod, g, modrow):
    t = x.shape[0]
    return pl.pallas_call(
        _norm_mod_kernel,
        grid=(t // TM,),
        in_specs=[pl.BlockSpec((TM, D), lambda r: (r, 0)),
                  pl.BlockSpec((None, 6, 1, D), lambda r: (modrow(r), 0, 0, 0)),
                  pl.BlockSpec((1, D), lambda r: (0, 0))],
        out_specs=pl.BlockSpec((TM, D), lambda r: (r, 0)),
        out_shape=jax.ShapeDtypeStruct((t, D), BF16),
        compiler_params=_cparams(("parallel",)),
        name="norm_mod",
    )(x, mod, g)


def _rope_tables(dec_seq, half, lane_of_dim):
    pos = jnp.arange(dec_seq)
    row = (pos // GRID_W).astype(F32)
    col = (pos % GRID_W).astype(F32)
    inv = ROPE_BASE ** (-(jnp.arange(half, dtype=F32) / half))
    ang = jnp.stack([row[:, None] * inv[None, :], col[:, None] * inv[None, :]])
    cos, sin = jnp.cos(ang), jnp.sin(ang)
    c_cols, s1_cols, s2_cols = [], [], []
    one, zero = jnp.ones((dec_seq,), F32), jnp.zeros((dec_seq,), F32)
    for lane in range(LANES):
        info = lane_of_dim(lane)
        if info is None:
            c_cols.append(one); s1_cols.append(zero); s2_cols.append(zero)
            continue
        axis, k, second = info
        c_cols.append(cos[axis, :, k])
        if second:
            s1_cols.append(zero); s2_cols.append(sin[axis, :, k])
        else:
            s1_cols.append(-sin[axis, :, k]); s2_cols.append(zero)
    return (jnp.stack(c_cols, axis=1), jnp.stack(s1_cols, axis=1), jnp.stack(s2_cols, axis=1))


def _even_lane(lane):
    j = lane % HD
    axis, jj = j // 32, j % 32
    return axis, jj % 16, jj >= 16


def _mla_lane(lane):
    if lane < 64 or lane >= 96:
        return None
    jj = lane - 64
    axis, k = jj // 16, jj % 16
    return axis, k % 8, k >= 8


def _rope(x, c, s1, s2, shift):
    return x * c + pltpu.roll(x, LANES - shift, 1) * s1 + pltpu.roll(x, shift, 1) * s2


_EVEN_ROPE_TILES = tuple(range(0, 8)) + tuple(range(12, 17))
_EVEN_Q_TILES = tuple(range(0, 4)) + tuple(range(12, 16))


def _proj_even_kernel(*refs, rope, caches):
    h_ref, w_ref = refs[0], refs[1]
    pos = 2
    if rope:
        c_ref, s1_ref, s2_ref = refs[2:5]
        pos = 5
    qkv_ref = refs[pos]
    res = _dot(h_ref[...], w_ref[...])
    scale = HD ** -0.5
    for t in range(EVEN_IN // LANES):
        x = res[:, t * LANES:(t + 1) * LANES]
        if rope and t in _EVEN_ROPE_TILES:
            x = _rope(x, c_ref[...], s1_ref[...], s2_ref[...], 16)
        if t in _EVEN_Q_TILES:
            x = x * scale
        qkv_ref[:, t * LANES:(t + 1) * LANES] = x.astype(BF16)
    if caches:
        kv_ref = refs[pos + 1]
        kv_ref[:, 0:1024] = res[:, 512:1536]
        kv_ref[:, 1024:1280] = res[:, 2048:2304]


def _proj_even_call(h, w, row0, nrows, tables, dec_seq):
    rope = tables is not None
    caches = not rope
    t0 = row0 // TM
    in_specs = [pl.BlockSpec((TM, D), lambda r: (t0 + r, 0)),
                pl.BlockSpec((D, EVEN_IN), lambda r: (0, 0))]
    args = [h, w]
    if rope:
        per = dec_seq // TM
        for _ in range(3):
            in_specs.append(pl.BlockSpec((TM, LANES), lambda r: (r % per, 0)))
        args += list(tables)
    out_specs = [pl.BlockSpec((TM, EVEN_IN), lambda r: (r, 0))]
    out_shape = [jax.ShapeDtypeStruct((nrows, EVEN_IN), BF16)]
    if caches:
        out_specs.append(pl.BlockSpec((TM, 1280), lambda r: (r, 0)))
        out_shape.append(jax.ShapeDtypeStruct((nrows, 1280), F32))
    return pl.pallas_call(
        functools.partial(_proj_even_kernel, rope=rope, caches=caches),
        grid=(nrows // TM,),
        in_specs=in_specs, out_specs=out_specs, out_shape=out_shape,
        compiler_params=_cparams(("parallel",)),
        name="proj_even_lat" if rope else "proj_even_ctx",
    )(*args)


ODD_IN_PAD = MLA_Q_RANK + MLA_KV_RANK + LANES
MLA_QW = MLA_HEADS * LANES


def _proj_odd_kernel(*refs, rope):
    h_ref, w_ref, qn_ref, wq_ref, kvn_ref = refs[:5]
    pos = 5
    if rope:
        c_ref, s1_ref, s2_ref = refs[5:8]
        pos = 8
    q_ref, ckv_ref, kr_ref = refs[pos:pos + 3]
    res = _dot(h_ref[...], w_ref[...])
    cq = _rms(res[:, :MLA_Q_RANK], qn_ref[...]).astype(BF16)
    ckv_ref[...] = _rms(res[:, MLA_Q_RANK:MLA_Q_RANK + MLA_KV_RANK], kvn_ref[...])
    kr = res[:, MLA_Q_RANK + MLA_KV_RANK:]
    if rope:
        kr = _rope(kr, c_ref[...], s1_ref[...], s2_ref[...], 8)
    kr_ref[...] = kr
    q = _dot(cq, wq_ref[...])
    scale = (HD + MLA_ROPE) ** -0.5
    for t in range(MLA_HEADS):
        x = q[:, t * LANES:(t + 1) * LANES]
        if rope:
            x = _rope(x, c_ref[...], s1_ref[...], s2_ref[...], 8)
        q_ref[:, t * LANES:(t + 1) * LANES] = (x * scale).astype(BF16)


def _proj_odd_call(h, w_in, qn, wq, kvn, row0, nrows, tables, dec_seq):
    rope = tables is not None
    t0 = row0 // TM
    in_specs = [pl.BlockSpec((TM, D), lambda r: (t0 + r, 0)),
                pl.BlockSpec((D, ODD_IN_PAD), lambda r: (0, 0)),
                pl.BlockSpec((1, MLA_Q_RANK), lambda r: (0, 0)),
                pl.BlockSpec((MLA_Q_RANK, MLA_QW), lambda r: (0, 0)),
                pl.BlockSpec((1, MLA_KV_RANK), lambda r: (0, 0))]
    args = [h, w_in, qn, wq, kvn]
    if rope:
        per = dec_seq // TM
        for _ in range(3):
            in_specs.append(pl.BlockSpec((TM, LANES), lambda r: (r % per, 0)))
        args += list(tables)
    return pl.pallas_call(
        functools.partial(_proj_odd_kernel, rope=rope),
        grid=(nrows // TM,),
        in_specs=in_specs,
        out_specs=[pl.BlockSpec((TM, MLA_QW), lambda r: (r, 0)),
                   pl.BlockSpec((TM, MLA_KV_RANK), lambda r: (r, 0)),
                   pl.BlockSpec((TM, LANES), lambda r: (r, 0))],
        out_shape=[jax.ShapeDtypeStruct((nrows, MLA_QW), BF16),
                   jax.ShapeDtypeStruct((nrows, MLA_KV_RANK), F32),
                   jax.ShapeDtypeStruct((nrows, LANES), F32)],
        compiler_params=_cparams(("parallel",)),
        name="proj_odd_lat" if rope else "proj_odd_ctx",
    )(*args)


def _attn_even_kernel(*refs, seq, past, lam_init):
    latent = past > 0
    n = past + seq
    if latent:
        (qkv_ref, ck_ref, cv_ref, sk_ref, sv_ref, lam_ref, subln_ref, sink_ref,
         o_ref, kd, vd, ka, vl, vh) = refs
    else:
        qkv_ref, lam_ref, subln_ref, sink_ref, o_ref, kd, vd, ka, vl, vh = refs
    qi = pl.program_id(1)
    lo = lax.broadcasted_iota(I32, (1, LANES), 1) < HD

    @pl.when(qi == 0)
    def _build():
        chunk = 256
        for c0 in range(0, n, chunk):
            rows = slice(c0, c0 + chunk)
            if c0 < past:
                prow = slice(c0, c0 + chunk)
                kd[rows, :] = ck_ref[prow, :].astype(BF16)
                vd[rows, :] = cv_ref[prow, :].astype(BF16)
                kt = sk_ref[prow, :]
                vt = sv_ref[prow, :]
            else:
                orow = slice(c0 - past, c0 - past + chunk)
                kd[rows, :] = qkv_ref[orow, 512:1024]
                vd[rows, :] = qkv_ref[orow, 1024:1536]
                kt = qkv_ref[orow, 2048:2176].astype(F32)
                vt = qkv_ref[orow, 2176:2304].astype(F32)
            kr = pltpu.roll(kt, HD, 1)
            vr = pltpu.roll(vt, HD, 1)
            ka[0, rows, :] = jnp.where(lo, kt, kr).astype(BF16)
            ka[1, rows, :] = jnp.where(lo, kr, kt).astype(BF16)
            vl[0, rows, :] = jnp.where(lo, vt, 0.0).astype(BF16)
            vh[0, rows, :] = jnp.where(lo, 0.0, vr).astype(BF16)
            vl[1, rows, :] = jnp.where(lo, vr, 0.0).astype(BF16)
            vh[1, rows, :] = jnp.where(lo, 0.0, vt).astype(BF16)

    r0 = pl.multiple_of(qi * TQ, TQ)
    lam = lam_ref[...]
    lam_full = (jnp.exp(jnp.sum(lam[0:1] * lam[1:2], axis=-1, keepdims=True))
                - jnp.exp(jnp.sum(lam[2:3] * lam[3:4], axis=-1, keepdims=True)) + lam_init)
    zero_b = jnp.zeros((), BF16)

    for h in range(DIFF_HEADS):
        cs = slice(h * LANES, (h + 1) * LANES)
        qt = qkv_ref[pl.ds(r0, TQ), cs]
        kh = kd[:, cs]
        es, rs = [], []
        for comp in range(2):
            qc = jnp.where(lo, qt, zero_b) if comp == 0 else jnp.where(lo, zero_b, qt)
            s = _dot_nt(qc, kh)
            m = jnp.max(s, axis=-1, keepdims=True)
            e = jnp.exp(s - m)
            es.append(e)
            rs.append(1.0 / jnp.sum(e, axis=-1, keepdims=True))
        a = es[0] * rs[0] - es[1] * (lam_full * rs[1])
        o = _dot(a.astype(BF16), vd[:, cs])
        o = _rms(o, subln_ref[...]) * (1.0 - lam_init)
        o_ref[:, cs] = o.astype(BF16)

    nblk = seq // WINDOW
    dense = past if latent else seq
    if latent:
        per = TQ // WINDOW
        offsets = tuple(range(-1, per + 1))
        rr = lax.broadcasted_iota(I32, (TQ, WINDOW), 0)
        cc = lax.broadcasted_iota(I32, (TQ, WINDOW), 1)
        band, starts = {}, {}
        for d in offsets:
            blk = qi * per + d
            inside = jnp.logical_and(blk >= 0, blk < nblk)
            band[d] = jnp.logical_and(jnp.abs(rr - cc - d * WINDOW) <= WINDOW, inside)
            starts[d] = pl.multiple_of(past + jnp.clip(blk, 0, nblk - 1) * WINDOW, WINDOW)
    for i in range(SWA_HEADS // 2):
        hk = i // 2
        cs = slice(1536 + i * LANES, 1536 + (i + 1) * LANES)
        qt = qkv_ref[pl.ds(r0, TQ), cs]
        acc = jnp.zeros((TQ, LANES), F32)
        for half in range(2):
            qc = jnp.where(lo, qt, zero_b) if half == 0 else jnp.where(lo, zero_b, qt)
            vsel = vl if half == 0 else vh
            sink = sink_ref[2 * i + half]
            parts = [_dot_nt(qc, ka[hk, 0:dense, :])]
            if latent:
                for d in offsets:
                    s = _dot_nt(qc, ka[hk, pl.ds(starts[d], WINDOW), :])
                    parts.append(jnp.where(band[d], s, NEG_INF))
            s_all = jnp.concatenate(parts, axis=1) if len(parts) > 1 else parts[0]
            m = jnp.maximum(jnp.max(s_all, axis=-1, keepdims=True), sink)
            e = jnp.exp(s_all - m)
            r = 1.0 / (jnp.sum(e, axis=-1, keepdims=True) + jnp.exp(sink - m))
            eb = e.astype(BF16)
            o = _dot(eb[:, 0:dense], vsel[hk, 0:dense, :])
            if latent:
                for k, d in enumerate(offsets):
                    o += _dot(eb[:, dense + k * WINDOW:dense + (k + 1) * WINDOW],
                              vsel[hk, pl.ds(starts[d], WINDOW), :])
            acc += o * r
        o_ref[:, 512 + i * LANES:512 + (i + 1) * LANES] = acc.astype(BF16)


def _attn_even_call(qkv, caches, lam, subln, sink, nbatch, seq, past, lam_init):
    n = past + seq
    latent = past > 0
    in_specs = [pl.BlockSpec((seq, EVEN_IN), lambda b, q: (b, 0))]
    args = [qkv]
    if latent:
        ck, cv, sk, sv = caches
        in_specs += [pl.BlockSpec((None, past, 512), lambda b, q: (b, 0, 0)),
                     pl.BlockSpec((None, past, 512), lambda b, q: (b, 0, 0)),
                     pl.BlockSpec((None, past, LANES), lambda b, q: (b, 0, 0)),
                     pl.BlockSpec((None, past, LANES), lambda b, q: (b, 0, 0))]
        args += [ck, cv, sk, sv]
    in_specs += [pl.BlockSpec((4, HD), lambda b, q: (0, 0)),
                 pl.BlockSpec((1, 2 * HD), lambda b, q: (0, 0)),
                 pl.BlockSpec(memory_space=pltpu.SMEM)]
    args += [lam, subln, sink]
    return pl.pallas_call(
        functools.partial(_attn_even_kernel, seq=seq, past=past, lam_init=lam_init),
        grid=(nbatch, seq // TQ),
        in_specs=in_specs,
        out_specs=pl.BlockSpec((TQ, D), lambda b, q: (b * (seq // TQ) + q, 0)),
        out_shape=jax.ShapeDtypeStruct((nbatch * seq, D), BF16),
        scratch_shapes=[pltpu.VMEM((n, 512), BF16), pltpu.VMEM((n, 512), BF16),
                        pltpu.VMEM((2, n, LANES), BF16), pltpu.VMEM((2, n, LANES), BF16),
                        pltpu.VMEM((2, n, LANES), BF16)],
        compiler_params=_cparams(("arbitrary", "arbitrary")),
        name="attn_even_lat" if latent else "attn_even_ctx",
    )(*args)


def _attn_odd_kernel(*refs, seq, past):
    latent = past > 0
    n = past + seq
    if latent:
        q_ref, ckv_ref, kr_ref, cckv_ref, ckr_ref, wk_ref, wv_ref, o_ref, kf, vlo, vhi = refs
    else:
        q_ref, ckv_ref, kr_ref, wk_ref, wv_ref, o_ref, kf, vlo, vhi = refs
    qi = pl.program_id(1)
    lo = lax.broadcasted_iota(I32, (1, LANES), 1) < HD

    @pl.when(qi == 0)
    def _build():
        chunk = 256
        for c0 in range(0, n, chunk):
            rows = slice(c0, c0 + chunk)
            if c0 < past:
                ckv = cckv_ref[c0:c0 + chunk, :].astype(BF16)
                kr = ckr_ref[c0:c0 + chunk, :]
            else:
                ckv = ckv_ref[c0 - past:c0 - past + chunk, :].astype(BF16)
                kr = kr_ref[c0 - past:c0 - past + chunk, :]
            kk = _dot(ckv, wk_ref[...])
            for h in range(MLA_HEADS):
                cs = slice(h * LANES, (h + 1) * LANES)
                kf[rows, cs] = (kk[:, cs] + kr).astype(BF16)
            vv = _dot(ckv, wv_ref[...])
            for i in range(MLA_HEADS // 2):
                cs = slice(i * LANES, (i + 1) * LANES)
                vlo[rows, cs] = jnp.where(lo, vv[:, cs], 0.0).astype(BF16)
                vhi[rows, cs] = jnp.where(lo, 0.0, vv[:, cs]).astype(BF16)

    r0 = pl.multiple_of(qi * TQ, TQ)
    for i in range(MLA_HEADS // 2):
        acc = jnp.zeros((TQ, LANES), F32)
        for half in range(2):
            h = 2 * i + half
            cs = slice(h * LANES, (h + 1) * LANES)
            s = _dot_nt(q_ref[pl.ds(r0, TQ), cs], kf[:, cs])
            m = jnp.max(s, axis=-1, keepdims=True)
            e = jnp.exp(s - m)
            r = 1.0 / jnp.sum(e, axis=-1, keepdims=True)
            vsel = vlo if half == 0 else vhi
            acc += _dot(e.astype(BF16), vsel[:, i * LANES:(i + 1) * LANES]) * r
        o_ref[:, i * LANES:(i + 1) * LANES] = acc.astype(BF16)


def _attn_odd_call(q, ckv, kr, caches, wk, wv, nbatch, seq, past):
    n = past + seq
    latent = past > 0
    in_specs = [pl.BlockSpec((seq, MLA_QW), lambda b, qq: (b, 0)),
                pl.BlockSpec((seq, MLA_KV_RANK), lambda b, qq: (b, 0)),
                pl.BlockSpec((seq, LANES), lambda b, qq: (b, 0))]
    args = [q, ckv, kr]
    if latent:
        in_specs += [pl.BlockSpec((None, past, MLA_KV_RANK), lambda b, qq: (b, 0, 0)),
                     pl.BlockSpec((None, past, LANES), lambda b, qq: (b, 0, 0))]
        args += list(caches)
    in_specs += [pl.BlockSpec((MLA_KV_RANK, MLA_QW), lambda b, qq: (0, 0)),
                 pl.BlockSpec((MLA_KV_RANK, D), lambda b, qq: (0, 0))]
    args += [wk, wv]
    return pl.pallas_call(
        functools.partial(_attn_odd_kernel, seq=seq, past=past),
        grid=(nbatch, seq // TQ),
        in_specs=in_specs,
        out_specs=pl.BlockSpec((TQ, D), lambda b, qq: (b * (seq // TQ) + qq, 0)),
        out_shape=jax.ShapeDtypeStruct((nbatch * seq, D), BF16),
        scratch_shapes=[pltpu.VMEM((n, MLA_QW), BF16), pltpu.VMEM((n, D), BF16),
                        pltpu.VMEM((n, D), BF16)],
        compiler_params=_cparams(("arbitrary", "arbitrary")),
        name="attn_odd_lat" if latent else "attn_odd_ctx",
    )(*args)


def _outproj_kernel(x_ref, o_ref, w_ref, mod_ref, g_ref, xo_ref, h_ref):
    x = x_ref[...] + mod_ref[2] * _dot(o_ref[...], w_ref[...])
    xo_ref[...] = x
    h_ref[...] = _modulate(x, g_ref[...], mod_ref[3], mod_ref[4]).astype(BF16)


def _outproj_call(x, o, w, mod, g, modrow):
    t = x.shape[0]
    return pl.pallas_call(
        _outproj_kernel,
        grid=(t // TM,),
        in_specs=[pl.BlockSpec((TM, D), lambda r: (r, 0)),
                  pl.BlockSpec((TM, D), lambda r: (r, 0)),
                  pl.BlockSpec((D, D), lambda r: (0, 0)),
                  pl.BlockSpec((None, 6, 1, D), lambda r: (modrow(r), 0, 0, 0)),
                  pl.BlockSpec((1, D), lambda r: (0, 0))],
        out_specs=[pl.BlockSpec((TM, D), lambda r: (r, 0)),
                   pl.BlockSpec((TM, D), lambda r: (r, 0))],
        out_shape=[jax.ShapeDtypeStruct((t, D), F32), jax.ShapeDtypeStruct((t, D), BF16)],
        compiler_params=_cparams(("parallel",)),
        name="outproj",
    )(x, o, w, mod, g)


def _router_kernel(h_ref, w_ref, rowid_ref, gate_ref, tab_ref, aff_sc, *, nb, cap):
    b = pl.program_id(1)
    ne = N_EXPERTS
    logits = _dot(h_ref[...], w_ref[...])
    lane = lax.broadcasted_iota(I32, (TB, LANES), 1)
    lg = jnp.where(lane < ne, logits, -jnp.inf)
    e = jnp.exp(lg - jnp.max(lg, axis=-1, keepdims=True))
    aff = e / jnp.sum(e, axis=-1, keepdims=True)
    aff_sc[pl.ds(pl.multiple_of(b * ne, ne), ne), :] = aff.T[0:ne, :]

    @pl.when(b == nb - 1)
    def _select():
        nr = nb * ne
        a = aff_sc[...]
        bits = pltpu.bitcast(a, I32)
        ri = lax.broadcasted_iota(I32, (nr, nr), 0)
        ci = lax.broadcasted_iota(I32, (nr, nr), 1)
        same_e = (ri & (ne - 1)) == (ci & (ne - 1))
        same_b = (ri >> 4) == (ci >> 4)
        m_e = jnp.where(same_e, 1.0, 0.0).astype(BF16)
        m_b = jnp.where(same_b, 1.0, 0.0).astype(BF16)
        m_a = jnp.where(jnp.logical_and(same_e, ci < ri), 1.0, 0.0).astype(BF16)
        m_o = jnp.where(jnp.logical_and(same_b, ci < ri), 1.0, 0.0).astype(BF16)
        ui = lax.broadcasted_iota(I32, (TB, TB), 0)
        uj = lax.broadcasted_iota(I32, (TB, TB), 1)
        upper = jnp.where(ui < uj, 1.0, 0.0).astype(BF16)

        def rows_to_lanes(col):
            return jnp.broadcast_to(col, (nr, LANES)).astype(BF16)

        def count_ge(cand):
            c = jnp.sum(jnp.where(bits >= cand, 1.0, 0.0), axis=-1, keepdims=True)
            return _dot(m_e, rows_to_lanes(c))[:, 0:1]

        def bisect(i, v):
            cand = v | jnp.left_shift(jnp.int32(1), 30 - i)
            return jnp.where(count_ge(cand) >= cap, cand, v)

        thr = lax.fori_loop(0, 31, bisect, jnp.zeros((nr, 1), I32))
        gt = jnp.where(bits > thr, 1.0, 0.0)
        eq = jnp.where(bits == thr, 1.0, 0.0)
        n_gt = _dot(m_e, rows_to_lanes(jnp.sum(gt, axis=-1, keepdims=True)))[:, 0:1]
        need = cap - n_gt
        eq_before = (_dot(m_a, rows_to_lanes(jnp.sum(eq, axis=-1, keepdims=True)))[:, 0:1]
                     + _dot(eq.astype(BF16), upper))
        sel = jnp.where(jnp.logical_and(eq > 0.0, eq_before < need), 1.0, gt)
        local = _dot(sel.astype(BF16), upper)
        cnt = jnp.sum(sel, axis=-1, keepdims=True)
        seg = jnp.floor((cnt + (SEG - 1)) * (1.0 / SEG)) * SEG
        segb = rows_to_lanes(seg)
        off_stack = _dot(m_o, segb)[:, 0:1]
        off_buf = _dot(m_a, segb)[:, 0:1]
        rows_blk = _dot(m_b, segb)[:, 0:1]
        rows_exp = _dot(m_e, segb)[:, 0:1]
        rowid = jnp.where(sel > 0.0, off_stack + local, -1.0).astype(I32)
        gate = jnp.where(sel > 0.0, a, 0.0)
        for bb in range(nb):
            rowid_ref[bb] = rowid[bb * ne:(bb + 1) * ne, :]
            gate_ref[bb] = gate[bb * ne:(bb + 1) * ne, :]
        tl = lax.broadcasted_iota(I32, (nr, LANES), 1)
        tab = jnp.where(tl == 0, seg, jnp.where(tl == 1, off_stack, jnp.where(
            tl == 2, off_buf, jnp.where(tl == 3, rows_blk, rows_exp))))
        tab_ref[...] = tab.astype(I32)


def _router_call(h, w_router_pad, ngroups, ntok):
    nb = ntok // TB
    cap = EC_FACTOR * ntok // N_EXPERTS
    nr = nb * N_EXPERTS
    return pl.pallas_call(
        functools.partial(_router_kernel, nb=nb, cap=cap),
        grid=(ngroups, nb),
        in_specs=[pl.BlockSpec((TB, D), lambda g, b: (g * nb + b, 0)),
                  pl.BlockSpec((D, LANES), lambda g, b: (0, 0))],
        out_specs=[pl.BlockSpec((None, nb, N_EXPERTS, TB), lambda g, b: (g, 0, 0, 0)),
                   pl.BlockSpec((None, nb, N_EXPERTS, TB), lambda g, b: (g, 0, 0, 0)),
                   pl.BlockSpec((None, nr, LANES), lambda g, b: (g, 0, 0))],
        out_shape=[jax.ShapeDtypeStruct((ngroups, nb, N_EXPERTS, TB), I32),
                   jax.ShapeDtypeStruct((ngroups, nb, N_EXPERTS, TB), F32),
                   jax.ShapeDtypeStruct((ngroups, nr, LANES), I32)],
        scratch_shapes=[pltpu.VMEM((nr, TB), F32)],
        compiler_params=_cparams(("arbitrary", "arbitrary")),
        name="router",
    )(h, w_router_pad)


STACK_ROWS = N_EXPERTS * TB


FIRST_ROWS = 3 * TB


def _segment_rows(seg_s, offs_s, step, rowid_ref, dst, value_ref=None):
    for e in range(N_EXPERTS):
        k = step * N_EXPERTS + e
        rid_e = rowid_ref[e:e + 1, :]
        val_e = None if value_ref is None else value_ref[e:e + 1, :]

        def group(i, carry, off=offs_s[k], rid_e=rid_e, val_e=val_e):
            r0 = pl.multiple_of(off + i * SEG, SEG)
            hit = rid_e == lax.broadcasted_iota(I32, (SEG, TB), 0) + r0
            dst[pl.ds(r0, SEG), :] = jnp.where(hit, 1.0 if val_e is None else val_e, 0.0).astype(dst.dtype)
            return carry

        lax.fori_loop(0, seg_s[k] // SEG, group, 0)


def _zero_rows_to(dst, rows, total):
    def zero(i, carry):
        dst[pl.ds(pl.multiple_of(rows + i * SEG, SEG), SEG), :] = jnp.zeros((SEG, dst.shape[1]), dst.dtype)
        return carry

    lax.fori_loop(0, (total - rows) // SEG, zero, 0)


def _padded_rows(rows):
    return jnp.maximum(FIRST_ROWS, ((rows + TB - 1) // TB) * TB)


def _dispatch_kernel(seg_s, offs_s, offb_s, rblk_s, h_ref, rowid_ref, xe_hbm,
                     onehot, stack, zbuf, sem, zsem, *, nb, nsteps, cap):
    g = pl.program_id(0)
    b = pl.program_id(1)
    step = g * nb + b
    slot = lax.rem(step, 2)
    rows = rblk_s[step]
    xrows = xe_hbm.shape[2]

    def copies(st, sl, wait_only):
        gg = st // nb
        for e in range(N_EXPERTS):
            k = st * N_EXPERTS + e

            def one(i, carry, e=e, src0=offs_s[k], dst0=offb_s[k]):
                cp = pltpu.make_async_copy(
                    stack.at[sl, pl.ds(pl.multiple_of(src0 + i * SEG, SEG), SEG)],
                    xe_hbm.at[gg, e, pl.ds(pl.multiple_of(dst0 + i * SEG, SEG), SEG)], sem.at[sl])
                if wait_only:
                    cp.wait()
                else:
                    cp.start()
                return carry

            lax.fori_loop(0, seg_s[k] // SEG, one, 0)

    @pl.when(b == 0)
    def _zero_unused():
        zbuf[...] = jnp.zeros_like(zbuf)
        for e in range(N_EXPERTS):
            pltpu.make_async_copy(zbuf, xe_hbm.at[g, e, pl.ds(cap, xrows - cap)], zsem).start()

    _segment_rows(seg_s, offs_s, step, rowid_ref, onehot)
    _zero_rows_to(onehot, rows, _padded_rows(rows))

    @pl.when(step >= 2)
    def _slot_free():
        copies(step - 2, slot, True)

    h = h_ref[...]
    stack[slot, 0:FIRST_ROWS, :] = _dot(onehot[0:FIRST_ROWS, :], h).astype(BF16)

    def chunk(c, carry):
        base = pl.multiple_of(c * TB, TB)
        stack[slot, pl.ds(base, TB), :] = _dot(onehot[pl.ds(base, TB), :], h).astype(BF16)
        return carry

    lax.fori_loop(FIRST_ROWS // TB, (rows + TB - 1) // TB, chunk, 0)
    copies(step, slot, False)

    @pl.when(b == 0)
    def _zero_unused_done():
        for e in range(N_EXPERTS):
            pltpu.make_async_copy(zbuf, xe_hbm.at[g, e, pl.ds(cap, xrows - cap)], zsem).wait()

    @pl.when(step == nsteps - 1)
    def _drain():
        if nsteps > 1:
            copies(step - 1, 1 - slot, True)
        copies(step, slot, True)


def _expert_rows(ntok):
    cap = EC_FACTOR * ntok // N_EXPERTS
    worst = cap + (ntok // TB) * (SEG - 1)
    tiles = -(-worst // TB)
    assert cap % SEG == 0 and tiles * TB > cap
    return cap, tiles


def _dispatch_call(tabs, h, rowid, ngroups, ntok):
    nb = ntok // TB
    cap, tiles = _expert_rows(ntok)
    xrows = tiles * TB
    grid_spec = pltpu.PrefetchScalarGridSpec(
        num_scalar_prefetch=4,
        grid=(ngroups, nb),
        in_specs=[pl.BlockSpec((TB, D), lambda g, b, *_: (g * nb + b, 0)),
                  pl.BlockSpec((None, None, N_EXPERTS, TB), lambda g, b, *_: (g, b, 0, 0))],
        out_specs=pl.BlockSpec(memory_space=pl.ANY),
        scratch_shapes=[pltpu.VMEM((STACK_ROWS, TB), BF16), pltpu.VMEM((2, STACK_ROWS, D), BF16),
                        pltpu.VMEM((xrows - cap, D), BF16),
                        pltpu.SemaphoreType.DMA((2,)), pltpu.SemaphoreType.DMA])
    return pl.pallas_call(
        functools.partial(_dispatch_kernel, nb=nb, nsteps=ngroups * nb, cap=cap),
        grid_spec=grid_spec,
        out_shape=jax.ShapeDtypeStruct((ngroups, N_EXPERTS, xrows, D), BF16),
        compiler_params=_cparams(("arbitrary", "arbitrary")),
        name="dispatch",
    )(*tabs, h, rowid)


def _ffn_kernel(nt_s, xe_ref, wg_ref, wu_ref, wd_ref, y_ref, wgb, wub, wdb):
    e = pl.program_id(0)
    g = pl.program_id(1)
    j = pl.program_id(2)

    @pl.when(jnp.logical_and(g == 0, j == 0))
    def _cast():
        wgb[...] = wg_ref[...].astype(BF16)
        wub[...] = wu_ref[...].astype(BF16)
        wdb[...] = wd_ref[...].astype(BF16)

    live = j < nt_s[g * N_EXPERTS + e]

    @pl.when(live)
    def _run():
        x = xe_ref[...]
        hid = (_silu(_dot(x, wgb[...])) * _dot(x, wub[...])).astype(BF16)
        y_ref[...] = _dot(hid, wdb[...]).astype(BF16)

    @pl.when(jnp.logical_not(live))
    def _skip():
        y_ref[...] = jnp.zeros_like(y_ref)


def _ffn_call(ntiles, xe, wg, wu, wd, layer, ngroups, ntok):
    _, tiles = _expert_rows(ntok)

    def xmap(e, g, j, nt):
        return (g, e, jnp.minimum(j, nt[g * N_EXPERTS + e] - 1), 0)

    grid_spec = pltpu.PrefetchScalarGridSpec(
        num_scalar_prefetch=1,
        grid=(N_EXPERTS, ngroups, tiles),
        in_specs=[pl.BlockSpec((None, None, TB, D), xmap),
                  pl.BlockSpec((None, None, D, EXPERT_FF), lambda e, g, j, nt: (layer, e, 0, 0)),
                  pl.BlockSpec((None, None, D, EXPERT_FF), lambda e, g, j, nt: (layer, e, 0, 0)),
                  pl.BlockSpec((None, None, EXPERT_FF, D), lambda e, g, j, nt: (layer, e, 0, 0))],
        out_specs=pl.BlockSpec((None, None, TB, D), lambda e, g, j, nt: (g, e, j, 0)),
        scratch_shapes=[pltpu.VMEM((D, EXPERT_FF), BF16), pltpu.VMEM((D, EXPERT_FF), BF16),
                        pltpu.VMEM((EXPERT_FF, D), BF16)])
    return pl.pallas_call(
        _ffn_kernel,
        grid_spec=grid_spec,
        out_shape=jax.ShapeDtypeStruct((ngroups, N_EXPERTS, tiles * TB, D), BF16),
        compiler_params=_cparams(("arbitrary", "arbitrary", "arbitrary")),
        name="ffn",
    )(ntiles, xe, wg, wu, wd)


def _combine_kernel(seg_s, offs_s, offb_s, rblk_s, y_hbm, rowid_ref, gate_ref, x_ref, mod_ref,
                    modn_ref, g_ref, xo_ref, h_ref, weights, stack, acc, sem, *, nb, nsteps, final):
    g = pl.program_id(0)
    b = pl.program_id(1)
    step = g * nb + b
    slot = lax.rem(step, 2)
    rows = rblk_s[step]

    def copies(st, sl, wait_only):
        gg = st // nb
        for e in range(N_EXPERTS):
            k = st * N_EXPERTS + e

            def one(i, carry, e=e, dst0=offs_s[k], src0=offb_s[k]):
                cp = pltpu.make_async_copy(
                    y_hbm.at[gg, e, pl.ds(pl.multiple_of(src0 + i * SEG, SEG), SEG)],
                    stack.at[sl, pl.ds(pl.multiple_of(dst0 + i * SEG, SEG), SEG)], sem.at[sl])
                if wait_only:
                    cp.wait()
                else:
                    cp.start()
                return carry

            lax.fori_loop(0, seg_s[k] // SEG, one, 0)

    @pl.when(step == 0)
    def _first():
        stack[...] = jnp.zeros_like(stack)
        copies(step, slot, False)

    if nsteps > 1:
        @pl.when(step + 1 < nsteps)
        def _prefetch():
            copies(step + 1, 1 - slot, False)

    _segment_rows(seg_s, offs_s, step, rowid_ref, weights, gate_ref)
    _zero_rows_to(weights, rows, _padded_rows(rows))
    copies(step, slot, True)

    def token_weights(base):
        return weights[pl.ds(base, TB), :].T.astype(BF16)

    w = jnp.concatenate([token_weights(c * TB) for c in range(FIRST_ROWS // TB)], axis=1)
    acc[...] = _dot(w, stack[slot, 0:FIRST_ROWS, :])

    def chunk(c, carry):
        base = pl.multiple_of(c * TB, TB)
        acc[...] += _dot(token_weights(base), stack[slot, pl.ds(base, TB), :])
        return carry

    lax.fori_loop(FIRST_ROWS // TB, (rows + TB - 1) // TB, chunk, 0)

    x = x_ref[...] + mod_ref[5] * acc[...]
    xo_ref[...] = x
    if final:
        h_ref[...] = _rms(x, g_ref[...])
    else:
        h_ref[...] = _modulate(x, g_ref[...], modn_ref[0], modn_ref[1]).astype(BF16)


def _combine_call(tabs, y, rowid, gate, x, mod, modn, gvec, modrow, ngroups, ntok, final):
    nb = ntok // TB
    t = x.shape[0]
    grid_spec = pltpu.PrefetchScalarGridSpec(
        num_scalar_prefetch=4,
        grid=(ngroups, nb),
        in_specs=[pl.BlockSpec(memory_space=pl.ANY),
                  pl.BlockSpec((None, None, N_EXPERTS, TB), lambda g, b, *_: (g, b, 0, 0)),
                  pl.BlockSpec((None, None, N_EXPERTS, TB), lambda g, b, *_: (g, b, 0, 0)),
                  pl.BlockSpec((TB, D), lambda g, b, *_: (g * nb + b, 0)),
                  pl.BlockSpec((None, 6, 1, D), lambda g, b, *_: (modrow(g * nb + b), 0, 0, 0)),
                  pl.BlockSpec((None, 6, 1, D), lambda g, b, *_: (modrow(g * nb + b), 0, 0, 0)),
                  pl.BlockSpec((1, D), lambda g, b, *_: (0, 0))],
        out_specs=[pl.BlockSpec((TB, D), lambda g, b, *_: (g * nb + b, 0)),
                   pl.BlockSpec((TB, D), lambda g, b, *_: (g * nb + b, 0))],
        scratch_shapes=[pltpu.VMEM((STACK_ROWS, TB), F32), pltpu.VMEM((2, STACK_ROWS, D), BF16),
                        pltpu.VMEM((TB, D), F32), pltpu.SemaphoreType.DMA((2,))])
    return pl.pallas_call(
        functools.partial(_combine_kernel, nb=nb, nsteps=ngroups * nb, final=final),
        grid_spec=grid_spec,
        out_shape=[jax.ShapeDtypeStruct((t, D), F32),
                   jax.ShapeDtypeStruct((t, D), F32 if final else BF16)],
        compiler_params=_cparams(("arbitrary", "arbitrary")),
        name="combine",
    )(*tabs, y, rowid, gate, x, mod, modn, gvec)


def kernel(x_prompt, x_sample, cache_diff_k, cache_diff_v, cache_swa_k, cache_swa_v, cache_mla_ckv, cache_mla_krope, c, c_ctx, w_ada, b_ada, norm_mix, norm_ffn, w_in_even, w_out_even, diff_lambda, diff_subln, swa_sink, w_in_odd, mla_q_norm, w_q_up, mla_kv_norm, w_kv_up, w_out_odd, w_router, w_gate_exp, w_up_exp, w_down_exp, final_norm):
    batch, seq, _ = x_prompt.shape
    dec_batch, dec_seq, _ = x_sample.shape
    past = cache_diff_k.shape[2]
    depth = w_ada.shape[0]
    n_even = w_in_even.shape[0]
    n_odd = w_in_odd.shape[0]
    nc, ns = batch * seq, dec_batch * dec_seq
    assert nc == ns, "the routed-expert kernels take two token groups of equal size"
    assert seq % TM == 0 and dec_seq % TM == 0 and past % 256 == 0 and dec_seq % GRID_W == 0
    ntok = nc

    def modrow(r):
        tok = r * TM
        return jnp.where(tok < nc, 0, 1 + jnp.maximum(tok - nc, 0) // dec_seq)

    rc = -(-(1 + dec_batch) // 16) * 16
    cvec = jnp.zeros((rc, D), F32).at[0].set(c_ctx).at[1:1 + dec_batch].set(c)
    mods = _ada_call(cvec, w_ada, b_ada).reshape(depth, rc, 6, 1, D)

    w_even_b = w_in_even.astype(BF16)
    w_oute_b = w_out_even.astype(BF16)
    w_outo_b = w_out_odd.astype(BF16)
    kr_pad = jnp.zeros((n_odd, D, LANES), F32).at[:, :, 64:96].set(w_in_odd[:, :, 640:672])
    w_odd_b = jnp.concatenate([w_in_odd[:, :, :640], kr_pad], axis=-1).astype(BF16)
    wq = w_q_up.reshape(n_odd, MLA_Q_RANK, MLA_HEADS, HD + MLA_ROPE)
    wq_b = jnp.pad(wq, ((0, 0), (0, 0), (0, 0), (0, LANES - HD - MLA_ROPE))).reshape(
        n_odd, MLA_Q_RANK, MLA_QW).astype(BF16)
    wkv = w_kv_up.reshape(n_odd, MLA_KV_RANK, MLA_HEADS, 2 * HD)
    wk_b = jnp.pad(wkv[..., :HD], ((0, 0), (0, 0), (0, 0), (0, LANES - HD))).reshape(
        n_odd, MLA_KV_RANK, MLA_QW).astype(BF16)
    wv_b = wkv[..., HD:].reshape(n_odd, MLA_KV_RANK, D).astype(BF16)
    w_router_b = jnp.pad(w_router, ((0, 0), (0, 0), (0, LANES - N_EXPERTS))).astype(BF16)
    even_tabs = _rope_tables(dec_seq, 16, _even_lane)
    mla_tabs = _rope_tables(dec_seq, 8, _mla_lane)
    cdk = cache_diff_k.reshape(dec_batch, n_even, past, 512)
    cdv = cache_diff_v.reshape(dec_batch, n_even, past, 512)
    csk = cache_swa_k.reshape(dec_batch, n_even, past, LANES)
    csv = cache_swa_v.reshape(dec_batch, n_even, past, LANES)
    ckr = jnp.zeros((dec_batch, n_odd, past, LANES), F32).at[..., 64:96].set(cache_mla_krope)

    x = jnp.concatenate([x_prompt.reshape(nc, D), x_sample.reshape(ns, D)], axis=0)
    h = _norm_mod_call(x, mods[0], norm_mix[0:1], modrow)
    kv_even, ckv_odd, kr_odd = [], [], []
    y_final = None
    for i in range(depth):
        j = i // 2
        if i % 2 == 0:
            qkv_c, kvc = _proj_even_call(h, w_even_b[j], 0, nc, None, dec_seq)
            (qkv_l,) = _proj_even_call(h, w_even_b[j], nc, ns, even_tabs, dec_seq)
            kv_even.append(kvc)
            li = _lambda_init(i)
            o_c = _attn_even_call(qkv_c, None, diff_lambda[j], diff_subln[j:j + 1], swa_sink[j],
                                  batch, seq, 0, li)
            o_l = _attn_even_call(qkv_l, (cdk[:, j], cdv[:, j], csk[:, j], csv[:, j]),
                                  diff_lambda[j], diff_subln[j:j + 1], swa_sink[j],
                                  dec_batch, dec_seq, past, li)
            w_out = w_oute_b[j]
        else:
            q_c, ckv_c, kr_c = _proj_odd_call(h, w_odd_b[j], mla_q_norm[j:j + 1], wq_b[j],
                                              mla_kv_norm[j:j + 1], 0, nc, None, dec_seq)
            q_l, ckv_l, kr_l = _proj_odd_call(h, w_odd_b[j], mla_q_norm[j:j + 1], wq_b[j],
                                              mla_kv_norm[j:j + 1], nc, ns, mla_tabs, dec_seq)
            ckv_odd.append(ckv_c)
            kr_odd.append(kr_c)
            o_c = _attn_odd_call(q_c, ckv_c, kr_c, None, wk_b[j], wv_b[j], batch, seq, 0)
            o_l = _attn_odd_call(q_l, ckv_l, kr_l, (cache_mla_ckv[:, j], ckr[:, j]),
                                 wk_b[j], wv_b[j], dec_batch, dec_seq, past)
            w_out = w_outo_b[j]
        o = jnp.concatenate([o_c, o_l], axis=0)
        x, h2 = _outproj_call(x, o, w_out, mods[i], norm_ffn[i:i + 1], modrow)

        rowid, gate, tab = _router_call(h2, w_router_b[i], 2, ntok)
        seg_t = tab[:, :, 0].reshape(-1)
        offs_t = tab[:, :, 1].reshape(-1)
        offb_t = tab[:, :, 2].reshape(-1)
        rblk_t = tab[:, ::N_EXPERTS, 3].reshape(-1)
        rexp_t = tab[:, :N_EXPERTS, 4].reshape(-1)
        xe = _dispatch_call((seg_t, offs_t, offb_t, rblk_t), h2, rowid, 2, ntok)
        ntile_t = (rexp_t + TB - 1) // TB
        y = _ffn_call(ntile_t, xe, w_gate_exp, w_up_exp, w_down_exp, i, 2, ntok)
        final = i == depth - 1
        nxt = i if final else i + 1
        gvec = final_norm.reshape(1, D) if final else norm_mix[nxt:nxt + 1]
        x, h = _combine_call((seg_t, offs_t, offb_t, rblk_t), y, rowid, gate, x, mods[i],
                             mods[nxt], gvec, modrow, 2, ntok, final)
        if final:
            y_final = h

    y_prompt = y_final[:nc].reshape(batch, seq, D)
    y_sample = y_final[nc:].reshape(dec_batch, dec_seq, D)
    kv = jnp.stack(kv_even, axis=1).reshape(batch, seq, n_even, 1280).transpose(0, 2, 1, 3)
    new_diff_k = kv[..., 0:512].reshape(batch, n_even, seq, DIFF_HEADS, 2 * HD)
    new_diff_v = kv[..., 512:1024].reshape(batch, n_even, seq, DIFF_HEADS, 2 * HD)
    new_swa_k = kv[..., 1024:1152].reshape(batch, n_even, seq, 2, HD)
    new_swa_v = kv[..., 1152:1280].reshape(batch, n_even, seq, 2, HD)
    new_mla_ckv = jnp.stack(ckv_odd, axis=1).reshape(batch, seq, n_odd, MLA_KV_RANK).transpose(0, 2, 1, 3)
    new_mla_krope = jnp.stack(kr_odd, axis=1).reshape(batch, seq, n_odd, LANES).transpose(
        0, 2, 1, 3)[..., 64:96]
    return (y_prompt, y_sample, new_diff_k, new_diff_v, new_swa_k, new_swa_v, new_mla_ckv,
            new_mla_krope)
```

```python
import functools
import math

import jax
import jax.numpy as jnp
from jax import lax
from jax.experimental import pallas as pl
from jax.experimental.pallas import tpu as pltpu

F32 = jnp.float32
BF16 = jnp.bfloat16
I32 = jnp.int32

D = 1024
HD = 64
GRID_W = 64
WINDOW = 128
DIFF_HEADS = 4
SWA_HEADS = 8
MLA_HEADS = 16
MLA_Q_RANK = 384
MLA_KV_RANK = 256
MLA_ROPE = 32
N_EXPERTS = 16
EXPERT_FF = 512
EC_FACTOR = 2
ROPE_BASE = 10000.0
EPS = 1e-6
NEG_INF = -1e30
EVEN_IN = 2304
LANES = 128
TM = 256
TQ = 256
TB = 256
SEG = 16
VMEM_LIMIT = 56 * 1024 * 1024


def _cparams(sem, vmem=VMEM_LIMIT):
    return pltpu.CompilerParams(dimension_semantics=sem, vmem_limit_bytes=vmem)


def _dot(a, b):
    return jnp.dot(a, b, preferred_element_type=F32)


def _dot_nt(a, b):
    return lax.dot_general(a, b, (((1,), (1,)), ((), ())), preferred_element_type=F32)


def _silu(x):
    return x / (1.0 + jnp.exp(-x))


def _rms(x, g):
    ms = jnp.mean(x * x, axis=-1, keepdims=True)
    return x * lax.rsqrt(ms + EPS) * g


def _modulate(x, g, shift, scale):
    return _rms(x, g) * (1.0 + scale) + shift


def _lambda_init(layer):
    return 0.8 - 0.6 * math.exp(-0.3 * layer)


def _ada_kernel(c_ref, w_ref, b_ref, o_ref):
    s = _silu(c_ref[...]).astype(BF16)
    o_ref[...] = _dot(s, w_ref[...].astype(BF16)) + b_ref[...]


def _ada_call(cvec, w_ada, b_ada):
    depth, _, n6 = w_ada.shape
    rc = cvec.shape[0]
    tn = 512
    return pl.pallas_call(
        _ada_kernel,
        grid=(depth, n6 // tn),
        in_specs=[pl.BlockSpec((rc, D), lambda i, n: (0, 0)),
                  pl.BlockSpec((None, D, tn), lambda i, n: (i, 0, n)),
                  pl.BlockSpec((None, 1, tn), lambda i, n: (i, 0, n))],
        out_specs=pl.BlockSpec((None, rc, tn), lambda i, n: (i, 0, n)),
        out_shape=jax.ShapeDtypeStruct((depth, rc, n6), F32),
        compiler_params=_cparams(("parallel", "parallel")),
        name="ada",
    )(cvec, w_ada, b_ada.reshape(depth, 1, n6))


def _norm_mod_kernel(x_ref, mod_ref, g_ref, h_ref):
    h_ref[...] = _modulate(x_ref[...], g_ref[...], mod_ref[0], mod_ref[1]).astype(BF16)


def _norm_mod_call(x, mod, g, modrow):
    t = x.shape[0]
    return pl.pallas_call(
        _norm_mod_kernel,
        grid=(t // TM,),
        in_specs=[pl.BlockSpec((TM, D), lambda r: (r, 0)),
                  pl.BlockSpec((None, 6, 1, D), lambda r: (modrow(r), 0, 0, 0)),
                  pl.BlockSpec((1, D), lambda r: (0, 0))],
        out_specs=pl.BlockSpec((TM, D), lambda r: (r, 0)),
        out_shape=jax.ShapeDtypeStruct((t, D), BF16),
        compiler_params=_cparams(("parallel",)),
        name="norm_mod",
    )(x, mod, g)


def _rope_tables(dec_seq, half, lane_of_dim):
    pos = jnp.arange(dec_seq)
    row = (pos // GRID_W).astype(F32)
    col = (pos % GRID_W).astype(F32)
    inv = ROPE_BASE ** (-(jnp.arange(half, dtype=F32) / half))
    ang = jnp.stack([row[:, None] * inv[None, :], col[:, None] * inv[None, :]])
    cos, sin = jnp.cos(ang), jnp.sin(ang)
    c_cols, s1_cols, s2_cols = [], [], []
    one, zero = jnp.ones((dec_seq,), F32), jnp.zeros((dec_seq,), F32)
    for lane in range(LANES):
        info = lane_of_dim(lane)
        if info is None:
            c_cols.append(one); s1_cols.append(zero); s2_cols.append(zero)
            continue
        axis, k, second = info
        c_cols.append(cos[axis, :, k])
        if second:
            s1_cols.append(zero); s2_cols.append(sin[axis, :, k])
        else:
            s1_cols.append(-sin[axis, :, k]); s2_cols.append(zero)
    return (jnp.stack(c_cols, axis=1), jnp.stack(s1_cols, axis=1), jnp.stack(s2_cols, axis=1))


def _even_lane(lane):
    j = lane % HD
    axis, jj = j // 32, j % 32
    return axis, jj % 16, jj >= 16


def _mla_lane(lane):
    if lane < 64 or lane >= 96:
        return None
    jj = lane - 64
    axis, k = jj // 16, jj % 16
    return axis, k % 8, k >= 8


def _rope(x, c, s1, s2, shift):
    return x * c + pltpu.roll(x, LANES - shift, 1) * s1 + pltpu.roll(x, shift, 1) * s2


_EVEN_ROPE_TILES = tuple(range(0, 8)) + tuple(range(12, 17))
_EVEN_Q_TILES = tuple(range(0, 4)) + tuple(range(12, 16))


def _proj_even_kernel(*refs, rope, caches):
    h_ref, w_ref = refs[0], refs[1]
    pos = 2
    if rope:
        c_ref, s1_ref, s2_ref = refs[2:5]
        pos = 5
    qkv_ref = refs[pos]
    res = _dot(h_ref[...], w_ref[...])
    scale = HD ** -0.5
    for t in range(EVEN_IN // LANES):
        x = res[:, t * LANES:(t + 1) * LANES]
        if rope and t in _EVEN_ROPE_TILES:
            x = _rope(x, c_ref[...], s1_ref[...], s2_ref[...], 16)
        if t in _EVEN_Q_TILES:
            x = x * scale
        qkv_ref[:, t * LANES:(t + 1) * LANES] = x.astype(BF16)
    if caches:
        kd_ref, vd_ref, ks_ref, vs_ref = refs[pos + 1:pos + 5]
        kd_ref[...] = res[:, 512:1024]
        vd_ref[...] = res[:, 1024:1536]
        ks_ref[...] = res[:, 2048:2176]
        vs_ref[...] = res[:, 2176:2304]


def _proj_even_call(h, w, row0, nrows, tables, dec_seq):
    rope = tables is not None
    caches = not rope
    t0 = row0 // TM
    in_specs = [pl.BlockSpec((TM, D), lambda r: (t0 + r, 0)),
                pl.BlockSpec((D, EVEN_IN), lambda r: (0, 0))]
    args = [h, w]
    if rope:
        per = dec_seq // TM
        for _ in range(3):
            in_specs.append(pl.BlockSpec((TM, LANES), lambda r: (r % per, 0)))
        args += list(tables)
    out_specs = [pl.BlockSpec((TM, EVEN_IN), lambda r: (r, 0))]
    out_shape = [jax.ShapeDtypeStruct((nrows, EVEN_IN), BF16)]
    if caches:
        for width in (512, 512, LANES, LANES):
            out_specs.append(pl.BlockSpec((TM, width), lambda r: (r, 0)))
            out_shape.append(jax.ShapeDtypeStruct((nrows, width), F32))
    return pl.pallas_call(
        functools.partial(_proj_even_kernel, rope=rope, caches=caches),
        grid=(nrows // TM,),
        in_specs=in_specs, out_specs=out_specs, out_shape=out_shape,
        compiler_params=_cparams(("parallel",)),
        name="proj_even_lat" if rope else "proj_even_ctx",
    )(*args)


ODD_IN_PAD = MLA_Q_RANK + MLA_KV_RANK + LANES
MLA_QW = MLA_HEADS * LANES


def _proj_odd_kernel(*refs, rope):
    h_ref, w_ref, qn_ref, wq_ref, kvn_ref = refs[:5]
    pos = 5
    if rope:
        c_ref, s1_ref, s2_ref = refs[5:8]
        pos = 8
    q_ref, ckv_ref, kr_ref = refs[pos:pos + 3]
    res = _dot(h_ref[...], w_ref[...])
    cq = _rms(res[:, :MLA_Q_RANK], qn_ref[...]).astype(BF16)
    ckv_ref[...] = _rms(res[:, MLA_Q_RANK:MLA_Q_RANK + MLA_KV_RANK], kvn_ref[...])
    kr = res[:, MLA_Q_RANK + MLA_KV_RANK:]
    if rope:
        kr = _rope(kr, c_ref[...], s1_ref[...], s2_ref[...], 8)
    kr_ref[...] = kr
    q = _dot(cq, wq_ref[...])
    scale = (HD + MLA_ROPE) ** -0.5
    for t in range(MLA_HEADS):
        x = q[:, t * LANES:(t + 1) * LANES]
        if rope:
            x = _rope(x, c_ref[...], s1_ref[...], s2_ref[...], 8)
        q_ref[:, t * LANES:(t + 1) * LANES] = (x * scale).astype(BF16)


def _proj_odd_call(h, w_in, qn, wq, kvn, row0, nrows, tables, dec_seq):
    rope = tables is not None
    t0 = row0 // TM
    in_specs = [pl.BlockSpec((TM, D), lambda r: (t0 + r, 0)),
                pl.BlockSpec((D, ODD_IN_PAD), lambda r: (0, 0)),
                pl.BlockSpec((1, MLA_Q_RANK), lambda r: (0, 0)),
                pl.BlockSpec((MLA_Q_RANK, MLA_QW), lambda r: (0, 0)),
                pl.BlockSpec((1, MLA_KV_RANK), lambda r: (0, 0))]
    args = [h, w_in, qn, wq, kvn]
    if rope:
        per = dec_seq // TM
        for _ in range(3):
            in_specs.append(pl.BlockSpec((TM, LANES), lambda r: (r % per, 0)))
        args += list(tables)
    return pl.pallas_call(
        functools.partial(_proj_odd_kernel, rope=rope),
        grid=(nrows // TM,),
        in_specs=in_specs,
        out_specs=[pl.BlockSpec((TM, MLA_QW), lambda r: (r, 0)),
                   pl.BlockSpec((TM, MLA_KV_RANK), lambda r: (r, 0)),
                   pl.BlockSpec((TM, LANES), lambda r: (r, 0))],
        out_shape=[jax.ShapeDtypeStruct((nrows, MLA_QW), BF16),
                   jax.ShapeDtypeStruct((nrows, MLA_KV_RANK), F32),
                   jax.ShapeDtypeStruct((nrows, LANES), F32)],
        compiler_params=_cparams(("parallel",)),
        name="proj_odd_lat" if rope else "proj_odd_ctx",
    )(*args)


def _attn_even_kernel(*refs, seq, past, lam_init):
    latent = past > 0
    n = past + seq
    if latent:
        (qkv_ref, ck_ref, cv_ref, sk_ref, sv_ref, lam_ref, subln_ref, sink_ref,
         o_ref, kd, vd, ka, vl, vh) = refs
    else:
        qkv_ref, lam_ref, subln_ref, sink_ref, o_ref, kd, vd, ka, vl, vh = refs
    qi = pl.program_id(1)
    lo = lax.broadcasted_iota(I32, (1, LANES), 1) < HD

    @pl.when(qi == 0)
    def _build():
        chunk = 256
        for c0 in range(0, n, chunk):
            rows = slice(c0, c0 + chunk)
            if c0 < past:
                prow = slice(c0, c0 + chunk)
                for h in range(DIFF_HEADS):
                    kd[rows, h * LANES:(h + 1) * LANES] = ck_ref[prow, h, :].astype(BF16)
                    vd[rows, h * LANES:(h + 1) * LANES] = cv_ref[prow, h, :].astype(BF16)
                kt = jnp.concatenate([sk_ref[prow, 0, :], sk_ref[prow, 1, :]], axis=1)
                vt = jnp.concatenate([sv_ref[prow, 0, :], sv_ref[prow, 1, :]], axis=1)
            else:
                orow = slice(c0 - past, c0 - past + chunk)
                kd[rows, :] = qkv_ref[orow, 512:1024]
                vd[rows, :] = qkv_ref[orow, 1024:1536]
                kt = qkv_ref[orow, 2048:2176].astype(F32)
                vt = qkv_ref[orow, 2176:2304].astype(F32)
            kr = pltpu.roll(kt, HD, 1)
            vr = pltpu.roll(vt, HD, 1)
            ka[0, rows, :] = jnp.where(lo, kt, kr).astype(BF16)
            ka[1, rows, :] = jnp.where(lo, kr, kt).astype(BF16)
            vl[0, rows, :] = jnp.where(lo, vt, 0.0).astype(BF16)
            vh[0, rows, :] = jnp.where(lo, 0.0, vr).astype(BF16)
            vl[1, rows, :] = jnp.where(lo, vr, 0.0).astype(BF16)
            vh[1, rows, :] = jnp.where(lo, 0.0, vt).astype(BF16)

    r0 = pl.multiple_of(qi * TQ, TQ)
    lam = lam_ref[...]
    lam_full = (jnp.exp(jnp.sum(lam[0:1] * lam[1:2], axis=-1, keepdims=True))
                - jnp.exp(jnp.sum(lam[2:3] * lam[3:4], axis=-1, keepdims=True)) + lam_init)
    zero_b = jnp.zeros((), BF16)

    for h in range(DIFF_HEADS):
        cs = slice(h * LANES, (h + 1) * LANES)
        qt = qkv_ref[pl.ds(r0, TQ), cs]
        kh = kd[:, cs]
        es, rs = [], []
        for comp in range(2):
            qc = jnp.where(lo, qt, zero_b) if comp == 0 else jnp.where(lo, zero_b, qt)
            s = _dot_nt(qc, kh)
            m = jnp.max(s, axis=-1, keepdims=True)
            e = jnp.exp(s - m)
            es.append(e)
            rs.append(1.0 / jnp.sum(e, axis=-1, keepdims=True))
        a = es[0] * rs[0] - es[1] * (lam_full * rs[1])
        o = _dot(a.astype(BF16), vd[:, cs])
        o = _rms(o, subln_ref[...]) * (1.0 - lam_init)
        o_ref[:, cs] = o.astype(BF16)

    nblk = seq // WINDOW
    dense = past if latent else seq
    if latent:
        per = TQ // WINDOW
        offsets = tuple(range(-1, per + 1))
        rr = lax.broadcasted_iota(I32, (TQ, WINDOW), 0)
        cc = lax.broadcasted_iota(I32, (TQ, WINDOW), 1)
        band, starts = {}, {}
        for d in offsets:
            blk = qi * per + d
            inside = jnp.logical_and(blk >= 0, blk < nblk)
            band[d] = jnp.logical_and(jnp.abs(rr - cc - d * WINDOW) <= WINDOW, inside)
            starts[d] = pl.multiple_of(past + jnp.clip(blk, 0, nblk - 1) * WINDOW, WINDOW)
    for i in range(SWA_HEADS // 2):
        hk = i // 2
        cs = slice(1536 + i * LANES, 1536 + (i + 1) * LANES)
        qt = qkv_ref[pl.ds(r0, TQ), cs]
        acc = jnp.zeros((TQ, LANES), F32)
        for half in range(2):
            qc = jnp.where(lo, qt, zero_b) if half == 0 else jnp.where(lo, zero_b, qt)
            vsel = vl if half == 0 else vh
            sink = sink_ref[2 * i + half]
            parts = [_dot_nt(qc, ka[hk, 0:dense, :])]
            if latent:
                for d in offsets:
                    s = _dot_nt(qc, ka[hk, pl.ds(starts[d], WINDOW), :])
                    parts.append(jnp.where(band[d], s, NEG_INF))
            s_all = jnp.concatenate(parts, axis=1) if len(parts) > 1 else parts[0]
            m = jnp.maximum(jnp.max(s_all, axis=-1, keepdims=True), sink)
            e = jnp.exp(s_all - m)
            r = 1.0 / (jnp.sum(e, axis=-1, keepdims=True) + jnp.exp(sink - m))
            eb = e.astype(BF16)
            o = _dot(eb[:, 0:dense], vsel[hk, 0:dense, :])
            if latent:
                for k, d in enumerate(offsets):
                    o += _dot(eb[:, dense + k * WINDOW:dense + (k + 1) * WINDOW],
                              vsel[hk, pl.ds(starts[d], WINDOW), :])
            acc += o * r
        o_ref[:, 512 + i * LANES:512 + (i + 1) * LANES] = acc.astype(BF16)


def _attn_even_call(qkv, caches, j, lam, subln, sink, nbatch, seq, past, lam_init):
    n = past + seq
    latent = past > 0
    in_specs = [pl.BlockSpec((seq, EVEN_IN), lambda b, q: (b, 0))]
    args = [qkv]
    if latent:
        ck, cv, sk, sv = caches
        in_specs += [pl.BlockSpec((None, None, past, DIFF_HEADS, 2 * HD), lambda b, q: (b, j, 0, 0, 0)),
                     pl.BlockSpec((None, None, past, DIFF_HEADS, 2 * HD), lambda b, q: (b, j, 0, 0, 0)),
                     pl.BlockSpec((None, None, past, 2, HD), lambda b, q: (b, j, 0, 0, 0)),
                     pl.BlockSpec((None, None, past, 2, HD), lambda b, q: (b, j, 0, 0, 0))]
        args += [ck, cv, sk, sv]
    in_specs += [pl.BlockSpec((4, HD), lambda b, q: (0, 0)),
                 pl.BlockSpec((1, 2 * HD), lambda b, q: (0, 0)),
                 pl.BlockSpec(memory_space=pltpu.SMEM)]
    args += [lam, subln, sink]
    return pl.pallas_call(
        functools.partial(_attn_even_kernel, seq=seq, past=past, lam_init=lam_init),
        grid=(nbatch, seq // TQ),
        in_specs=in_specs,
        out_specs=pl.BlockSpec((TQ, D), lambda b, q: (b * (seq // TQ) + q, 0)),
        out_shape=jax.ShapeDtypeStruct((nbatch * seq, D), BF16),
        scratch_shapes=[pltpu.VMEM((n, 512), BF16), pltpu.VMEM((n, 512), BF16),
                        pltpu.VMEM((2, n, LANES), BF16), pltpu.VMEM((2, n, LANES), BF16),
                        pltpu.VMEM((2, n, LANES), BF16)],
        compiler_params=_cparams(("arbitrary", "arbitrary")),
        name="attn_even_lat" if latent else "attn_even_ctx",
    )(*args)


def _attn_odd_kernel(*refs, seq, past):
    latent = past > 0
    n = past + seq
    if latent:
        q_ref, ckv_ref, kr_ref, cckv_ref, ckr_ref, wk_ref, wv_ref, o_ref, kf, vlo, vhi = refs
    else:
        q_ref, ckv_ref, kr_ref, wk_ref, wv_ref, o_ref, kf, vlo, vhi = refs
    qi = pl.program_id(1)
    lo = lax.broadcasted_iota(I32, (1, LANES), 1) < HD

    @pl.when(qi == 0)
    def _build():
        chunk = 256
        for c0 in range(0, n, chunk):
            rows = slice(c0, c0 + chunk)
            if c0 < past:
                ckv = cckv_ref[c0:c0 + chunk, :].astype(BF16)
                kr = ckr_ref[c0:c0 + chunk, :]
            else:
                ckv = ckv_ref[c0 - past:c0 - past + chunk, :].astype(BF16)
                kr = kr_ref[c0 - past:c0 - past + chunk, :]
            kk = _dot(ckv, wk_ref[...])
            for h in range(MLA_HEADS):
                cs = slice(h * LANES, (h + 1) * LANES)
                kf[rows, cs] = (kk[:, cs] + kr).astype(BF16)
            vv = _dot(ckv, wv_ref[...])
            for i in range(MLA_HEADS // 2):
                cs = slice(i * LANES, (i + 1) * LANES)
                vlo[rows, cs] = jnp.where(lo, vv[:, cs], 0.0).astype(BF16)
                vhi[rows, cs] = jnp.where(lo, 0.0, vv[:, cs]).astype(BF16)

    r0 = pl.multiple_of(qi * TQ, TQ)
    for i in range(MLA_HEADS // 2):
        acc = jnp.zeros((TQ, LANES), F32)
        for half in range(2):
            h = 2 * i + half
            cs = slice(h * LANES, (h + 1) * LANES)
            s = _dot_nt(q_ref[pl.ds(r0, TQ), cs], kf[:, cs])
            m = jnp.max(s, axis=-1, keepdims=True)
            e = jnp.exp(s - m)
            r = 1.0 / jnp.sum(e, axis=-1, keepdims=True)
            vsel = vlo if half == 0 else vhi
            acc += _dot(e.astype(BF16), vsel[:, i * LANES:(i + 1) * LANES]) * r
        o_ref[:, i * LANES:(i + 1) * LANES] = acc.astype(BF16)


def _attn_odd_call(q, ckv, kr, caches, j, wk, wv, nbatch, seq, past):
    n = past + seq
    latent = past > 0
    in_specs = [pl.BlockSpec((seq, MLA_QW), lambda b, qq: (b, 0)),
                pl.BlockSpec((seq, MLA_KV_RANK), lambda b, qq: (b, 0)),
                pl.BlockSpec((seq, LANES), lambda b, qq: (b, 0))]
    args = [q, ckv, kr]
    if latent:
        in_specs += [pl.BlockSpec((None, None, past, MLA_KV_RANK), lambda b, qq: (b, j, 0, 0)),
                     pl.BlockSpec((None, None, past, LANES), lambda b, qq: (b, j, 0, 0))]
        args += list(caches)
    in_specs += [pl.BlockSpec((MLA_KV_RANK, MLA_QW), lambda b, qq: (0, 0)),
                 pl.BlockSpec((MLA_KV_RANK, D), lambda b, qq: (0, 0))]
    args += [wk, wv]
    return pl.pallas_call(
        functools.partial(_attn_odd_kernel, seq=seq, past=past),
        grid=(nbatch, seq // TQ),
        in_specs=in_specs,
        out_specs=pl.BlockSpec((TQ, D), lambda b, qq: (b * (seq // TQ) + qq, 0)),
        out_shape=jax.ShapeDtypeStruct((nbatch * seq, D), BF16),
        scratch_shapes=[pltpu.VMEM((n, MLA_QW), BF16), pltpu.VMEM((n, D), BF16),
                        pltpu.VMEM((n, D), BF16)],
        compiler_params=_cparams(("arbitrary", "arbitrary")),
        name="attn_odd_lat" if latent else "attn_odd_ctx",
    )(*args)


def _outproj_kernel(x_ref, oa_ref, ob_ref, w_ref, mod_ref, g_ref, xo_ref, h_ref, *, ntile_a):
    def run(o_ref):
        x = x_ref[...] + mod_ref[2] * _dot(o_ref[...], w_ref[...])
        xo_ref[...] = x
        h_ref[...] = _modulate(x, g_ref[...], mod_ref[3], mod_ref[4]).astype(BF16)

    @pl.when(pl.program_id(0) < ntile_a)
    def _():
        run(oa_ref)

    @pl.when(pl.program_id(0) >= ntile_a)
    def _():
        run(ob_ref)


def _outproj_call(x, o_a, o_b, w, mod, g, modrow):
    t = x.shape[0]
    na, nb_ = o_a.shape[0] // TM, o_b.shape[0] // TM
    return pl.pallas_call(
        functools.partial(_outproj_kernel, ntile_a=na),
        grid=(t // TM,),
        in_specs=[pl.BlockSpec((TM, D), lambda r: (r, 0)),
                  pl.BlockSpec((TM, D), lambda r: (jnp.minimum(r, na - 1), 0)),
                  pl.BlockSpec((TM, D), lambda r: (jnp.clip(r - na, 0, nb_ - 1), 0)),
                  pl.BlockSpec((D, D), lambda r: (0, 0)),
                  pl.BlockSpec((None, 6, 1, D), lambda r: (modrow(r), 0, 0, 0)),
                  pl.BlockSpec((1, D), lambda r: (0, 0))],
        out_specs=[pl.BlockSpec((TM, D), lambda r: (r, 0)),
                   pl.BlockSpec((TM, D), lambda r: (r, 0))],
        out_shape=[jax.ShapeDtypeStruct((t, D), F32), jax.ShapeDtypeStruct((t, D), BF16)],
        compiler_params=_cparams(("parallel",)),
        name="outproj",
    )(x, o_a, o_b, w, mod, g)


def _router_kernel(h_ref, w_ref, rowid_ref, gate_ref, tab_ref, aff_sc, *, nb, cap):
    b = pl.program_id(1)
    ne = N_EXPERTS
    logits = _dot(h_ref[...], w_ref[...])
    lane = lax.broadcasted_iota(I32, (TB, LANES), 1)
    lg = jnp.where(lane < ne, logits, -jnp.inf)
    e = jnp.exp(lg - jnp.max(lg, axis=-1, keepdims=True))
    aff = e / jnp.sum(e, axis=-1, keepdims=True)
    aff_sc[pl.ds(pl.multiple_of(b * ne, ne), ne), :] = aff.T[0:ne, :]

    @pl.when(b == nb - 1)
    def _select():
        nr = nb * ne
        a = aff_sc[...]
        bits = pltpu.bitcast(a, I32)
        ri = lax.broadcasted_iota(I32, (nr, nr), 0)
        ci = lax.broadcasted_iota(I32, (nr, nr), 1)
        same_e = (ri & (ne - 1)) == (ci & (ne - 1))
        same_b = (ri >> 4) == (ci >> 4)
        m_e = jnp.where(same_e, 1.0, 0.0).astype(BF16)
        m_b = jnp.where(same_b, 1.0, 0.0).astype(BF16)
        m_a = jnp.where(jnp.logical_and(same_e, ci < ri), 1.0, 0.0).astype(BF16)
        m_o = jnp.where(jnp.logical_and(same_b, ci < ri), 1.0, 0.0).astype(BF16)
        ui = lax.broadcasted_iota(I32, (TB, TB), 0)
        uj = lax.broadcasted_iota(I32, (TB, TB), 1)
        upper = jnp.where(ui < uj, 1.0, 0.0).astype(BF16)

        def rows_to_lanes(col):
            return jnp.broadcast_to(col, (nr, LANES)).astype(BF16)

        def count_ge(cand):
            c = jnp.sum(jnp.where(bits >= cand, 1.0, 0.0), axis=-1, keepdims=True)
            return _dot(m_e, rows_to_lanes(c))[:, 0:1]

        def bisect(i, v):
            cand = v | jnp.left_shift(jnp.int32(1), 30 - i)
            return jnp.where(count_ge(cand) >= cap, cand, v)

        thr = lax.fori_loop(0, 31, bisect, jnp.zeros((nr, 1), I32))
        gt = jnp.where(bits > thr, 1.0, 0.0)
        eq = jnp.where(bits == thr, 1.0, 0.0)
        n_gt = _dot(m_e, rows_to_lanes(jnp.sum(gt, axis=-1, keepdims=True)))[:, 0:1]
        need = cap - n_gt
        eq_before = (_dot(m_a, rows_to_lanes(jnp.sum(eq, axis=-1, keepdims=True)))[:, 0:1]
                     + _dot(eq.astype(BF16), upper))
        sel = jnp.where(jnp.logical_and(eq > 0.0, eq_before < need), 1.0, gt)
        local = _dot(sel.astype(BF16), upper)
        cnt = jnp.sum(sel, axis=-1, keepdims=True)
        seg = jnp.floor((cnt + (SEG - 1)) * (1.0 / SEG)) * SEG
        segb = rows_to_lanes(seg)
        off_stack = _dot(m_o, segb)[:, 0:1]
        off_buf = _dot(m_a, segb)[:, 0:1]
        rows_blk = _dot(m_b, segb)[:, 0:1]
        rows_exp = _dot(m_e, segb)[:, 0:1]
        rowid = jnp.where(sel > 0.0, off_stack + local, -1.0).astype(I32)
        gate = jnp.where(sel > 0.0, a, 0.0)
        for bb in range(nb):
            rowid_ref[bb] = rowid[bb * ne:(bb + 1) * ne, :]
            gate_ref[bb] = gate[bb * ne:(bb + 1) * ne, :]
        tl = lax.broadcasted_iota(I32, (nr, LANES), 1)
        tab = jnp.where(tl == 0, seg, jnp.where(tl == 1, off_stack, jnp.where(
            tl == 2, off_buf, jnp.where(tl == 3, rows_blk, rows_exp))))
        tab_ref[...] = tab.astype(I32)


def _router_call(h, w_router_pad, ngroups, ntok):
    nb = ntok // TB
    cap = EC_FACTOR * ntok // N_EXPERTS
    nr = nb * N_EXPERTS
    return pl.pallas_call(
        functools.partial(_router_kernel, nb=nb, cap=cap),
        grid=(ngroups, nb),
        in_specs=[pl.BlockSpec((TB, D), lambda g, b: (g * nb + b, 0)),
                  pl.BlockSpec((D, LANES), lambda g, b: (0, 0))],
        out_specs=[pl.BlockSpec((None, nb, N_EXPERTS, TB), lambda g, b: (g, 0, 0, 0)),
                   pl.BlockSpec((None, nb, N_EXPERTS, TB), lambda g, b: (g, 0, 0, 0)),
                   pl.BlockSpec((None, nr, LANES), lambda g, b: (g, 0, 0))],
        out_shape=[jax.ShapeDtypeStruct((ngroups, nb, N_EXPERTS, TB), I32),
                   jax.ShapeDtypeStruct((ngroups, nb, N_EXPERTS, TB), F32),
                   jax.ShapeDtypeStruct((ngroups, nr, LANES), I32)],
        scratch_shapes=[pltpu.VMEM((nr, TB), F32)],
        compiler_params=_cparams(("arbitrary", "arbitrary")),
        name="router",
    )(h, w_router_pad)


STACK_ROWS = N_EXPERTS * TB


FIRST_ROWS = 3 * TB


WIN = 4 * SEG
PIECES = (16, 8, 4, 2, 1)
TF = 672


def _segment_rows(seg_s, offs_s, step, rows, rowid_ref, dst, value_ref=None):
    dst[0:FIRST_ROWS, :] = jnp.zeros((FIRST_ROWS, TB), dst.dtype)

    def zero(i, carry):
        r0 = pl.multiple_of(FIRST_ROWS + i * SEG, SEG)
        dst[pl.ds(r0, SEG), :] = jnp.zeros((SEG, TB), dst.dtype)
        return carry

    lax.fori_loop(0, (((rows + TB - 1) // TB) * TB - FIRST_ROWS) // SEG, zero, 0)

    for e in range(N_EXPERTS):
        k = step * N_EXPERTS + e
        off = offs_s[k]
        rid_e = rowid_ref[e:e + 1, :]
        val_e = 1.0 if value_ref is None else value_ref[e:e + 1, :]

        def put(r0, nrows, rid_e=rid_e, val_e=val_e):
            hit = rid_e == lax.broadcasted_iota(I32, (nrows, TB), 0) + r0
            dst[pl.ds(r0, nrows), :] = jnp.where(hit, val_e, 0.0).astype(dst.dtype)

        put(pl.multiple_of(off, SEG), WIN)

        def group(i, carry, off=off, put=put):
            put(pl.multiple_of(off + i * SEG, SEG), SEG)
            return carry

        lax.fori_loop(WIN // SEG, seg_s[k] // SEG, group, 0)


def _segment_copies(seg_s, k, make_copy):
    groups = seg_s[k] // SEG
    for p in PIECES:
        @pl.when((groups & p) != 0)
        def _(p=p):
            make_copy((groups & ~(2 * p - 1)) * SEG, p * SEG).start()


def _wait_rows(rows, make_copy):
    def big(i, carry):
        make_copy(TB).wait()
        return carry

    def small(i, carry):
        make_copy(SEG).wait()
        return carry

    lax.fori_loop(0, rows // TB, big, 0)
    lax.fori_loop(0, (rows % TB) // SEG, small, 0)


def _dispatch_kernel(seg_s, offs_s, offb_s, rblk_s, h_ref, rowid_ref, xe_hbm,
                     onehot, stack, zbuf, sem, zsem, *, nb, nsteps, cap):
    g = pl.program_id(0)
    b = pl.program_id(1)
    step = g * nb + b
    slot = lax.rem(step, 2)
    rows = rblk_s[step]
    xrows = xe_hbm.shape[2]

    def wait_slot(nrows, sl):
        _wait_rows(nrows, lambda n: pltpu.make_async_copy(
            stack.at[sl, pl.ds(0, n)], xe_hbm.at[0, 0, pl.ds(0, n)], sem.at[sl]))

    @pl.when(b == 0)
    def _zero_unused():
        zbuf[...] = jnp.zeros_like(zbuf)
        for e in range(N_EXPERTS):
            pltpu.make_async_copy(zbuf, xe_hbm.at[g, e, pl.ds(cap, xrows - cap)], zsem).start()

    _segment_rows(seg_s, offs_s, step, rows, rowid_ref, onehot)

    @pl.when(step >= 2)
    def _slot_free():
        wait_slot(rblk_s[step - 2], slot)

    h = h_ref[...]
    stack[slot, 0:FIRST_ROWS, :] = _dot(onehot[0:FIRST_ROWS, :], h).astype(BF16)

    def chunk(c, carry):
        base = pl.multiple_of(c * TB, TB)
        stack[slot, pl.ds(base, TB), :] = _dot(onehot[pl.ds(base, TB), :], h).astype(BF16)
        return carry

    lax.fori_loop(FIRST_ROWS // TB, (rows + TB - 1) // TB, chunk, 0)

    for e in range(N_EXPERTS):
        k = step * N_EXPERTS + e
        _segment_copies(seg_s, k, lambda r, n, e=e, k=k: pltpu.make_async_copy(
            stack.at[slot, pl.ds(pl.multiple_of(offs_s[k] + r, SEG), n)],
            xe_hbm.at[g, e, pl.ds(pl.multiple_of(offb_s[k] + r, SEG), n)], sem.at[slot]))

    @pl.when(b == 0)
    def _zero_unused_done():
        for e in range(N_EXPERTS):
            pltpu.make_async_copy(zbuf, xe_hbm.at[g, e, pl.ds(cap, xrows - cap)], zsem).wait()

    @pl.when(step == nsteps - 1)
    def _drain():
        if nsteps > 1:
            wait_slot(rblk_s[step - 1], 1 - slot)
        wait_slot(rows, slot)


def _expert_rows(ntok):
    cap = EC_FACTOR * ntok // N_EXPERTS
    worst = cap + (ntok // TB) * (SEG - 1)
    tiles = -(-worst // TF)
    assert cap % SEG == 0 and tiles * TF > cap
    return cap, tiles


def _dispatch_call(tabs, h, rowid, ngroups, ntok):
    nb = ntok // TB
    cap, tiles = _expert_rows(ntok)
    xrows = tiles * TF
    grid_spec = pltpu.PrefetchScalarGridSpec(
        num_scalar_prefetch=4,
        grid=(ngroups, nb),
        in_specs=[pl.BlockSpec((TB, D), lambda g, b, *_: (g * nb + b, 0)),
                  pl.BlockSpec((None, None, N_EXPERTS, TB), lambda g, b, *_: (g, b, 0, 0))],
        out_specs=pl.BlockSpec(memory_space=pl.ANY),
        scratch_shapes=[pltpu.VMEM((STACK_ROWS + WIN, TB), BF16), pltpu.VMEM((2, STACK_ROWS, D), BF16),
                        pltpu.VMEM((xrows - cap, D), BF16),
                        pltpu.SemaphoreType.DMA((2,)), pltpu.SemaphoreType.DMA])
    return pl.pallas_call(
        functools.partial(_dispatch_kernel, nb=nb, nsteps=ngroups * nb, cap=cap),
        grid_spec=grid_spec,
        out_shape=jax.ShapeDtypeStruct((ngroups, N_EXPERTS, xrows, D), BF16),
        compiler_params=_cparams(("arbitrary", "arbitrary")),
        name="dispatch",
    )(*tabs, h, rowid)


def _ffn_kernel(nt_s, xe_ref, wg_ref, wu_ref, wd_ref, y_ref, wgb, wub, wdb):
    e = pl.program_id(0)
    g = pl.program_id(1)
    j = pl.program_id(2)

    @pl.when(jnp.logical_and(g == 0, j == 0))
    def _cast():
        wgb[...] = wg_ref[...].astype(BF16)
        wub[...] = wu_ref[...].astype(BF16)
        wdb[...] = wd_ref[...].astype(BF16)

    live = j < nt_s[g * N_EXPERTS + e]

    @pl.when(live)
    def _run():
        x = xe_ref[...]
        hid = (_silu(_dot(x, wgb[...])) * _dot(x, wub[...])).astype(BF16)
        y_ref[...] = _dot(hid, wdb[...]).astype(BF16)

    @pl.when(jnp.logical_not(live))
    def _skip():
        y_ref[...] = jnp.zeros_like(y_ref)


def _ffn_call(ntiles, xe, wg, wu, wd, layer, ngroups, ntok):
    _, tiles = _expert_rows(ntok)

    def xmap(e, g, j, nt):
        return (g, e, jnp.minimum(j, nt[g * N_EXPERTS + e] - 1), 0)

    grid_spec = pltpu.PrefetchScalarGridSpec(
        num_scalar_prefetch=1,
        grid=(N_EXPERTS, ngroups, tiles),
        in_specs=[pl.BlockSpec((None, None, TF, D), xmap),
                  pl.BlockSpec((None, None, D, EXPERT_FF), lambda e, g, j, nt: (layer, e, 0, 0)),
                  pl.BlockSpec((None, None, D, EXPERT_FF), lambda e, g, j, nt: (layer, e, 0, 0)),
                  pl.BlockSpec((None, None, EXPERT_FF, D), lambda e, g, j, nt: (layer, e, 0, 0))],
        out_specs=pl.BlockSpec((None, None, TF, D), lambda e, g, j, nt: (g, e, j, 0)),
        scratch_shapes=[pltpu.VMEM((D, EXPERT_FF), BF16), pltpu.VMEM((D, EXPERT_FF), BF16),
                        pltpu.VMEM((EXPERT_FF, D), BF16)])
    return pl.pallas_call(
        _ffn_kernel,
        grid_spec=grid_spec,
        out_shape=jax.ShapeDtypeStruct((ngroups, N_EXPERTS, tiles * TF, D), BF16),
        compiler_params=_cparams(("arbitrary", "arbitrary", "arbitrary")),
        name="ffn",
    )(ntiles, xe, wg, wu, wd)


def _combine_kernel(seg_s, offs_s, offb_s, rblk_s, y_hbm, rowid_ref, gate_ref, x_ref, mod_ref,
                    modn_ref, g_ref, xo_ref, h_ref, weights, stack, acc, sem, *, nb, nsteps, final):
    g = pl.program_id(0)
    b = pl.program_id(1)
    step = g * nb + b
    slot = lax.rem(step, 2)
    rows = rblk_s[step]

    def fetch(st, sl):
        gg = st // nb
        for e in range(N_EXPERTS):
            k = st * N_EXPERTS + e
            _segment_copies(seg_s, k, lambda r, n, e=e, k=k: pltpu.make_async_copy(
                y_hbm.at[gg, e, pl.ds(pl.multiple_of(offb_s[k] + r, SEG), n)],
                stack.at[sl, pl.ds(pl.multiple_of(offs_s[k] + r, SEG), n)], sem.at[sl]))

    @pl.when(step == 0)
    def _first():
        stack[...] = jnp.zeros_like(stack)
        fetch(step, slot)

    if nsteps > 1:
        @pl.when(step + 1 < nsteps)
        def _prefetch():
            fetch(step + 1, 1 - slot)

    _segment_rows(seg_s, offs_s, step, rows, rowid_ref, weights, gate_ref)
    _wait_rows(rows, lambda n: pltpu.make_async_copy(
        y_hbm.at[0, 0, pl.ds(0, n)], stack.at[slot, pl.ds(0, n)], sem.at[slot]))

    def token_weights(base):
        return weights[pl.ds(base, TB), :].T.astype(BF16)

    w = jnp.concatenate([token_weights(c * TB) for c in range(FIRST_ROWS // TB)], axis=1)
    acc[...] = _dot(w, stack[slot, 0:FIRST_ROWS, :])

    def chunk(c, carry):
        base = pl.multiple_of(c * TB, TB)
        acc[...] += _dot(token_weights(base), stack[slot, pl.ds(base, TB), :])
        return carry

    lax.fori_loop(FIRST_ROWS // TB, (rows + TB - 1) // TB, chunk, 0)

    x = x_ref[...] + mod_ref[5] * acc[...]
    if final:
        y = _rms(x, g_ref[...])

        @pl.when(g == 0)
        def _():
            xo_ref[...] = y

        @pl.when(g != 0)
        def _():
            h_ref[...] = y
    else:
        xo_ref[...] = x
        h_ref[...] = _modulate(x, g_ref[...], modn_ref[0], modn_ref[1]).astype(BF16)


def _combine_call(tabs, y, rowid, gate, x, mod, modn, gvec, modrow, ngroups, ntok, final):
    nb = ntok // TB
    t = x.shape[0]
    if final:
        assert ngroups == 2
        out_specs = [pl.BlockSpec((TB, D), lambda g, b, *_: (jnp.where(g == 0, b, nb - 1), 0)),
                     pl.BlockSpec((TB, D), lambda g, b, *_: (jnp.where(g == 0, 0, b), 0))]
        out_shape = [jax.ShapeDtypeStruct((ntok, D), F32), jax.ShapeDtypeStruct((ntok, D), F32)]
    else:
        out_specs = [pl.BlockSpec((TB, D), lambda g, b, *_: (g * nb + b, 0)),
                     pl.BlockSpec((TB, D), lambda g, b, *_: (g * nb + b, 0))]
        out_shape = [jax.ShapeDtypeStruct((t, D), F32), jax.ShapeDtypeStruct((t, D), BF16)]
    grid_spec = pltpu.PrefetchScalarGridSpec(
        num_scalar_prefetch=4,
        grid=(ngroups, nb),
        in_specs=[pl.BlockSpec(memory_space=pl.ANY),
                  pl.BlockSpec((None, None, N_EXPERTS, TB), lambda g, b, *_: (g, b, 0, 0)),
                  pl.BlockSpec((None, None, N_EXPERTS, TB), lambda g, b, *_: (g, b, 0, 0)),
                  pl.BlockSpec((TB, D), lambda g, b, *_: (g * nb + b, 0)),
                  pl.BlockSpec((None, 6, 1, D), lambda g, b, *_: (modrow(g * nb + b), 0, 0, 0)),
                  pl.BlockSpec((None, 6, 1, D), lambda g, b, *_: (modrow(g * nb + b), 0, 0, 0)),
                  pl.BlockSpec((1, D), lambda g, b, *_: (0, 0))],
        out_specs=out_specs,
        scratch_shapes=[pltpu.VMEM((STACK_ROWS + WIN, TB), F32), pltpu.VMEM((2, STACK_ROWS, D), BF16),
                        pltpu.VMEM((TB, D), F32), pltpu.SemaphoreType.DMA((2,))])
    return pl.pallas_call(
        functools.partial(_combine_kernel, nb=nb, nsteps=ngroups * nb, final=final),
        grid_spec=grid_spec,
        out_shape=out_shape,
        compiler_params=_cparams(("arbitrary", "arbitrary")),
        name="combine",
    )(*tabs, y, rowid, gate, x, mod, modn, gvec)


def kernel(x_prompt, x_sample, cache_diff_k, cache_diff_v, cache_swa_k, cache_swa_v, cache_mla_ckv, cache_mla_krope, c, c_ctx, w_ada, b_ada, norm_mix, norm_ffn, w_in_even, w_out_even, diff_lambda, diff_subln, swa_sink, w_in_odd, mla_q_norm, w_q_up, mla_kv_norm, w_kv_up, w_out_odd, w_router, w_gate_exp, w_up_exp, w_down_exp, final_norm):
    batch, seq, _ = x_prompt.shape
    dec_batch, dec_seq, _ = x_sample.shape
    past = cache_diff_k.shape[2]
    depth = w_ada.shape[0]
    n_even = w_in_even.shape[0]
    n_odd = w_in_odd.shape[0]
    nc, ns = batch * seq, dec_batch * dec_seq
    assert nc == ns, "the routed-expert kernels take two token groups of equal size"
    assert seq % TM == 0 and dec_seq % TM == 0 and past % 256 == 0 and dec_seq % GRID_W == 0
    ntok = nc

    def modrow(r):
        tok = r * TM
        return jnp.where(tok < nc, 0, 1 + jnp.maximum(tok - nc, 0) // dec_seq)

    rc = -(-(1 + dec_batch) // 16) * 16
    cvec = jnp.zeros((rc, D), F32).at[0].set(c_ctx).at[1:1 + dec_batch].set(c)
    mods = _ada_call(cvec, w_ada, b_ada).reshape(depth, rc, 6, 1, D)

    w_even_b = w_in_even.astype(BF16)
    w_oute_b = w_out_even.astype(BF16)
    w_outo_b = w_out_odd.astype(BF16)
    kr_pad = jnp.zeros((n_odd, D, LANES), F32).at[:, :, 64:96].set(w_in_odd[:, :, 640:672])
    w_odd_b = jnp.concatenate([w_in_odd[:, :, :640], kr_pad], axis=-1).astype(BF16)
    wq = w_q_up.reshape(n_odd, MLA_Q_RANK, MLA_HEADS, HD + MLA_ROPE)
    wq_b = jnp.pad(wq, ((0, 0), (0, 0), (0, 0), (0, LANES - HD - MLA_ROPE))).reshape(
        n_odd, MLA_Q_RANK, MLA_QW).astype(BF16)
    wkv = w_kv_up.reshape(n_odd, MLA_KV_RANK, MLA_HEADS, 2 * HD)
    wk_b = jnp.pad(wkv[..., :HD], ((0, 0), (0, 0), (0, 0), (0, LANES - HD))).reshape(
        n_odd, MLA_KV_RANK, MLA_QW).astype(BF16)
    wv_b = wkv[..., HD:].reshape(n_odd, MLA_KV_RANK, D).astype(BF16)
    w_router_b = jnp.pad(w_router, ((0, 0), (0, 0), (0, LANES - N_EXPERTS))).astype(BF16)
    even_tabs = _rope_tables(dec_seq, 16, _even_lane)
    mla_tabs = _rope_tables(dec_seq, 8, _mla_lane)
    ckr = jnp.zeros((dec_batch, n_odd, past, LANES), F32).at[..., 64:96].set(cache_mla_krope)
    even_caches = (cache_diff_k, cache_diff_v, cache_swa_k, cache_swa_v)

    x = jnp.concatenate([x_prompt.reshape(nc, D), x_sample.reshape(ns, D)], axis=0)
    h = _norm_mod_call(x, mods[0], norm_mix[0:1], modrow)
    kd_new, vd_new, ks_new, vs_new, ckv_odd, kr_odd = [], [], [], [], [], []
    y_prompt = y_sample = None
    for i in range(depth):
        j = i // 2
        if i % 2 == 0:
            qkv_c, kd, vd, ks, vs = _proj_even_call(h, w_even_b[j], 0, nc, None, dec_seq)
            (qkv_l,) = _proj_even_call(h, w_even_b[j], nc, ns, even_tabs, dec_seq)
            kd_new.append(kd.reshape(batch, seq, DIFF_HEADS, 2 * HD))
            vd_new.append(vd.reshape(batch, seq, DIFF_HEADS, 2 * HD))
            ks_new.append(ks.reshape(batch, seq, 2, HD))
            vs_new.append(vs.reshape(batch, seq, 2, HD))
            li = _lambda_init(i)
            o_c = _attn_even_call(qkv_c, None, j, diff_lambda[j], diff_subln[j:j + 1], swa_sink[j],
                                  batch, seq, 0, li)
            o_l = _attn_even_call(qkv_l, even_caches, j, diff_lambda[j], diff_subln[j:j + 1],
                                  swa_sink[j], dec_batch, dec_seq, past, li)
            w_out = w_oute_b[j]
        else:
            q_c, ckv_c, kr_c = _proj_odd_call(h, w_odd_b[j], mla_q_norm[j:j + 1], wq_b[j],
                                              mla_kv_norm[j:j + 1], 0, nc, None, dec_seq)
            q_l, ckv_l, kr_l = _proj_odd_call(h, w_odd_b[j], mla_q_norm[j:j + 1], wq_b[j],
                                              mla_kv_norm[j:j + 1], nc, ns, mla_tabs, dec_seq)
            ckv_odd.append(ckv_c.reshape(batch, seq, MLA_KV_RANK))
            kr_odd.append(kr_c.reshape(batch, seq, LANES)[..., 64:96])
            o_c = _attn_odd_call(q_c, ckv_c, kr_c, None, j, wk_b[j], wv_b[j], batch, seq, 0)
            o_l = _attn_odd_call(q_l, ckv_l, kr_l, (cache_mla_ckv, ckr), j,
                                 wk_b[j], wv_b[j], dec_batch, dec_seq, past)
            w_out = w_outo_b[j]
        x, h2 = _outproj_call(x, o_c, o_l, w_out, mods[i], norm_ffn[i:i + 1], modrow)

        rowid, gate, tab = _router_call(h2, w_router_b[i], 2, ntok)
        seg_t = tab[:, :, 0].reshape(-1)
        offs_t = tab[:, :, 1].reshape(-1)
        offb_t = tab[:, :, 2].reshape(-1)
        rblk_t = tab[:, ::N_EXPERTS, 3].reshape(-1)
        rexp_t = tab[:, :N_EXPERTS, 4].reshape(-1)
        xe = _dispatch_call((seg_t, offs_t, offb_t, rblk_t), h2, rowid, 2, ntok)
        ntile_t = (rexp_t + TF - 1) // TF
        y = _ffn_call(ntile_t, xe, w_gate_exp, w_up_exp, w_down_exp, i, 2, ntok)
        final = i == depth - 1
        nxt = i if final else i + 1
        gvec = final_norm.reshape(1, D) if final else norm_mix[nxt:nxt + 1]
        out_a, out_b = _combine_call((seg_t, offs_t, offb_t, rblk_t), y, rowid, gate, x, mods[i],
                                     mods[nxt], gvec, modrow, 2, ntok, final)
        if final:
            y_prompt = out_a.reshape(batch, seq, D)
            y_sample = out_b.reshape(dec_batch, dec_seq, D)
        else:
            x, h = out_a, out_b

    return (y_prompt, y_sample, jnp.stack(kd_new, axis=1), jnp.stack(vd_new, axis=1),
            jnp.stack(ks_new, axis=1), jnp.stack(vs_new, axis=1), jnp.stack(ckv_odd, axis=1),
            jnp.stack(kr_odd, axis=1))
```

```python
import functools
import math

import jax
import jax.numpy as jnp
from jax import lax
from jax.experimental import pallas as pl
from jax.experimental.pallas import tpu as pltpu

F32 = jnp.float32
BF16 = jnp.bfloat16
I32 = jnp.int32

D = 1024
HD = 64
GRID_W = 64
WINDOW = 128
DIFF_HEADS = 4
SWA_HEADS = 8
MLA_HEADS = 16
MLA_Q_RANK = 384
MLA_KV_RANK = 256
MLA_ROPE = 32
N_EXPERTS = 16
EXPERT_FF = 512
EC_FACTOR = 2
ROPE_BASE = 10000.0
EPS = 1e-6
NEG_INF = -1e30
EVEN_IN = 2304
LANES = 128
TM = 256
TQ = 256
TB = 256
SEG = 16
VMEM_LIMIT = 56 * 1024 * 1024


def _cparams(sem, vmem=VMEM_LIMIT):
    return pltpu.CompilerParams(dimension_semantics=sem, vmem_limit_bytes=vmem)


def _dot(a, b):
    return jnp.dot(a, b, preferred_element_type=F32)


def _dot_nt(a, b):
    return lax.dot_general(a, b, (((1,), (1,)), ((), ())), preferred_element_type=F32)


def _silu(x):
    return x / (1.0 + jnp.exp(-x))


def _rms(x, g):
    ms = jnp.mean(x * x, axis=-1, keepdims=True)
    return x * lax.rsqrt(ms + EPS) * g


def _modulate(x, g, shift, scale):
    return _rms(x, g) * (1.0 + scale) + shift


def _lambda_init(layer):
    return 0.8 - 0.6 * math.exp(-0.3 * layer)


def _ada_kernel(c_ref, w_ref, b_ref, o_ref):
    s = _silu(c_ref[...]).astype(BF16)
    o_ref[...] = _dot(s, w_ref[...].astype(BF16)) + b_ref[...]


def _ada_call(cvec, w_ada, b_ada):
    depth, _, n6 = w_ada.shape
    rc = cvec.shape[0]
    tn = 512
    return pl.pallas_call(
        _ada_kernel,
        grid=(depth, n6 // tn),
        in_specs=[pl.BlockSpec((rc, D), lambda i, n: (0, 0)),
                  pl.BlockSpec((None, D, tn), lambda i, n: (i, 0, n)),
                  pl.BlockSpec((None, 1, tn), lambda i, n: (i, 0, n))],
        out_specs=pl.BlockSpec((None, rc, tn), lambda i, n: (i, 0, n)),
        out_shape=jax.ShapeDtypeStruct((depth, rc, n6), F32),
        compiler_params=_cparams(("parallel", "parallel")),
        name="ada",
    )(cvec, w_ada, b_ada.reshape(depth, 1, n6))


def _norm_mod_kernel(x_ref, mod_ref, g_ref, h_ref):
    h_ref[...] = _modulate(x_ref[...], g_ref[...], mod_ref[0], mod_ref[1]).astype(BF16)


def _norm_mod_call(x, mod, g, modrow):
    t = x.shape[0]
    return pl.pallas_call(
        _norm_mod_kernel,
        grid=(t // TM,),
        in_specs=[pl.BlockSpec((TM, D), lambda r: (r, 0)),
                  pl.BlockSpec((None, 6, 1, D), lambda r: (modrow(r), 0, 0, 0)),
                  pl.BlockSpec((1, D), lambda r: (0, 0))],
        out_specs=pl.BlockSpec((TM, D), lambda r: (r, 0)),
        out_shape=jax.ShapeDtypeStruct((t, D), BF16),
        compiler_params=_cparams(("parallel",)),
        name="norm_mod",
    )(x, mod, g)


def _rope_tables(dec_seq, half, lane_of_dim):
    pos = jnp.arange(dec_seq)
    row = (pos // GRID_W).astype(F32)
    col = (pos % GRID_W).astype(F32)
    inv = ROPE_BASE ** (-(jnp.arange(half, dtype=F32) / half))
    ang = jnp.stack([row[:, None] * inv[None, :], col[:, None] * inv[None, :]])
    cos, sin = jnp.cos(ang), jnp.sin(ang)
    c_cols, s1_cols, s2_cols = [], [], []
    one, zero = jnp.ones((dec_seq,), F32), jnp.zeros((dec_seq,), F32)
    for lane in range(LANES):
        info = lane_of_dim(lane)
        if info is None:
            c_cols.append(one); s1_cols.append(zero); s2_cols.append(zero)
            continue
        axis, k, second = info
        c_cols.append(cos[axis, :, k])
        if second:
            s1_cols.append(zero); s2_cols.append(sin[axis, :, k])
        else:
            s1_cols.append(-sin[axis, :, k]); s2_cols.append(zero)
    return (jnp.stack(c_cols, axis=1), jnp.stack(s1_cols, axis=1), jnp.stack(s2_cols, axis=1))


def _even_lane(lane):
    j = lane % HD
    axis, jj = j // 32, j % 32
    return axis, jj % 16, jj >= 16


def _mla_lane(lane):
    if lane < 64 or lane >= 96:
        return None
    jj = lane - 64
    axis, k = jj // 16, jj % 16
    return axis, k % 8, k >= 8


def _rope(x, c, s1, s2, shift):
    return x * c + pltpu.roll(x, LANES - shift, 1) * s1 + pltpu.roll(x, shift, 1) * s2


_EVEN_ROPE_TILES = tuple(range(0, 8)) + tuple(range(12, 17))
_EVEN_Q_TILES = tuple(range(0, 4)) + tuple(range(12, 16))


def _proj_even_kernel(*refs, rope, caches):
    h_ref, w_ref = refs[0], refs[1]
    pos = 2
    if rope:
        c_ref, s1_ref, s2_ref = refs[2:5]
        pos = 5
    qkv_ref = refs[pos]
    res = _dot(h_ref[...], w_ref[...])
    scale = HD ** -0.5
    for t in range(EVEN_IN // LANES):
        x = res[:, t * LANES:(t + 1) * LANES]
        if rope and t in _EVEN_ROPE_TILES:
            x = _rope(x, c_ref[...], s1_ref[...], s2_ref[...], 16)
        if t in _EVEN_Q_TILES:
            x = x * scale
        qkv_ref[:, t * LANES:(t + 1) * LANES] = x.astype(BF16)
    if caches:
        kd_ref, vd_ref, ks_ref, vs_ref = refs[pos + 1:pos + 5]
        kd_ref[...] = res[:, 512:1024]
        vd_ref[...] = res[:, 1024:1536]
        ks_ref[...] = res[:, 2048:2176]
        vs_ref[...] = res[:, 2176:2304]


def _proj_even_call(h, w, row0, nrows, tables, dec_seq):
    rope = tables is not None
    caches = not rope
    t0 = row0 // TM
    in_specs = [pl.BlockSpec((TM, D), lambda r: (t0 + r, 0)),
                pl.BlockSpec((D, EVEN_IN), lambda r: (0, 0))]
    args = [h, w]
    if rope:
        per = dec_seq // TM
        for _ in range(3):
            in_specs.append(pl.BlockSpec((TM, LANES), lambda r: (r % per, 0)))
        args += list(tables)
    out_specs = [pl.BlockSpec((TM, EVEN_IN), lambda r: (r, 0))]
    out_shape = [jax.ShapeDtypeStruct((nrows, EVEN_IN), BF16)]
    if caches:
        for width in (512, 512, LANES, LANES):
            out_specs.append(pl.BlockSpec((TM, width), lambda r: (r, 0)))
            out_shape.append(jax.ShapeDtypeStruct((nrows, width), F32))
    return pl.pallas_call(
        functools.partial(_proj_even_kernel, rope=rope, caches=caches),
        grid=(nrows // TM,),
        in_specs=in_specs, out_specs=out_specs, out_shape=out_shape,
        compiler_params=_cparams(("parallel",)),
        name="proj_even_lat" if rope else "proj_even_ctx",
    )(*args)


ODD_IN_PAD = MLA_Q_RANK + MLA_KV_RANK + LANES
MLA_QW = MLA_HEADS * LANES


def _proj_odd_kernel(*refs, rope):
    h_ref, w_ref, qn_ref, wq_ref, kvn_ref = refs[:5]
    pos = 5
    if rope:
        c_ref, s1_ref, s2_ref = refs[5:8]
        pos = 8
    q_ref, ckv_ref, kr_ref = refs[pos:pos + 3]
    res = _dot(h_ref[...], w_ref[...])
    cq = _rms(res[:, :MLA_Q_RANK], qn_ref[...]).astype(BF16)
    ckv_ref[...] = _rms(res[:, MLA_Q_RANK:MLA_Q_RANK + MLA_KV_RANK], kvn_ref[...])
    kr = res[:, MLA_Q_RANK + MLA_KV_RANK:]
    if rope:
        kr = _rope(kr, c_ref[...], s1_ref[...], s2_ref[...], 8)
    kr_ref[...] = kr
    q = _dot(cq, wq_ref[...])
    scale = (HD + MLA_ROPE) ** -0.5
    for t in range(MLA_HEADS):
        x = q[:, t * LANES:(t + 1) * LANES]
        if rope:
            x = _rope(x, c_ref[...], s1_ref[...], s2_ref[...], 8)
        q_ref[:, t * LANES:(t + 1) * LANES] = (x * scale).astype(BF16)


def _proj_odd_call(h, w_in, qn, wq, kvn, row0, nrows, tables, dec_seq):
    rope = tables is not None
    t0 = row0 // TM
    in_specs = [pl.BlockSpec((TM, D), lambda r: (t0 + r, 0)),
                pl.BlockSpec((D, ODD_IN_PAD), lambda r: (0, 0)),
                pl.BlockSpec((1, MLA_Q_RANK), lambda r: (0, 0)),
                pl.BlockSpec((MLA_Q_RANK, MLA_QW), lambda r: (0, 0)),
                pl.BlockSpec((1, MLA_KV_RANK), lambda r: (0, 0))]
    args = [h, w_in, qn, wq, kvn]
    if rope:
        per = dec_seq // TM
        for _ in range(3):
            in_specs.append(pl.BlockSpec((TM, LANES), lambda r: (r % per, 0)))
        args += list(tables)
    return pl.pallas_call(
        functools.partial(_proj_odd_kernel, rope=rope),
        grid=(nrows // TM,),
        in_specs=in_specs,
        out_specs=[pl.BlockSpec((TM, MLA_QW), lambda r: (r, 0)),
                   pl.BlockSpec((TM, MLA_KV_RANK), lambda r: (r, 0)),
                   pl.BlockSpec((TM, LANES), lambda r: (r, 0))],
        out_shape=[jax.ShapeDtypeStruct((nrows, MLA_QW), BF16),
                   jax.ShapeDtypeStruct((nrows, MLA_KV_RANK), F32),
                   jax.ShapeDtypeStruct((nrows, LANES), F32)],
        compiler_params=_cparams(("parallel",)),
        name="proj_odd_lat" if rope else "proj_odd_ctx",
    )(*args)


def _attn_even_kernel(*refs, seq, past, lam_init):
    latent = past > 0
    n = past + seq
    if latent:
        (qkv_ref, ck_ref, cv_ref, sk_ref, sv_ref, lam_ref, subln_ref, sink_ref,
         o_ref, kd, vd, ka, vl, vh) = refs
    else:
        qkv_ref, lam_ref, subln_ref, sink_ref, o_ref, kd, vd, ka, vl, vh = refs
    qi = pl.program_id(1)
    lo = lax.broadcasted_iota(I32, (1, LANES), 1) < HD

    @pl.when(qi == 0)
    def _build():
        chunk = 256
        for c0 in range(0, n, chunk):
            rows = slice(c0, c0 + chunk)
            if c0 < past:
                prow = slice(c0, c0 + chunk)
                for h in range(DIFF_HEADS):
                    kd[rows, h * LANES:(h + 1) * LANES] = ck_ref[prow, h, :].astype(BF16)
                    vd[rows, h * LANES:(h + 1) * LANES] = cv_ref[prow, h, :].astype(BF16)
                kt = jnp.concatenate([sk_ref[prow, 0, :], sk_ref[prow, 1, :]], axis=1)
                vt = jnp.concatenate([sv_ref[prow, 0, :], sv_ref[prow, 1, :]], axis=1)
            else:
                orow = slice(c0 - past, c0 - past + chunk)
                kd[rows, :] = qkv_ref[orow, 512:1024]
                vd[rows, :] = qkv_ref[orow, 1024:1536]
                kt = qkv_ref[orow, 2048:2176].astype(F32)
                vt = qkv_ref[orow, 2176:2304].astype(F32)
            kr = pltpu.roll(kt, HD, 1)
            vr = pltpu.roll(vt, HD, 1)
            ka[0, rows, :] = jnp.where(lo, kt, kr).astype(BF16)
            ka[1, rows, :] = jnp.where(lo, kr, kt).astype(BF16)
            vl[0, rows, :] = jnp.where(lo, vt, 0.0).astype(BF16)
            vh[0, rows, :] = jnp.where(lo, 0.0, vr).astype(BF16)
            vl[1, rows, :] = jnp.where(lo, vr, 0.0).astype(BF16)
            vh[1, rows, :] = jnp.where(lo, 0.0, vt).astype(BF16)

    r0 = pl.multiple_of(qi * TQ, TQ)
    lam = lam_ref[...]
    lam_full = (jnp.exp(jnp.sum(lam[0:1] * lam[1:2], axis=-1, keepdims=True))
                - jnp.exp(jnp.sum(lam[2:3] * lam[3:4], axis=-1, keepdims=True)) + lam_init)
    zero_b = jnp.zeros((), BF16)

    for h in range(DIFF_HEADS):
        cs = slice(h * LANES, (h + 1) * LANES)
        qt = qkv_ref[pl.ds(r0, TQ), cs]
        kh = kd[:, cs]
        es, rs = [], []
        for comp in range(2):
            qc = jnp.where(lo, qt, zero_b) if comp == 0 else jnp.where(lo, zero_b, qt)
            s = _dot_nt(qc, kh)
            m = jnp.max(s, axis=-1, keepdims=True)
            e = jnp.exp(s - m)
            es.append(e)
            rs.append(1.0 / jnp.sum(e, axis=-1, keepdims=True))
        a = es[0] * rs[0] - es[1] * (lam_full * rs[1])
        o = _dot(a.astype(BF16), vd[:, cs])
        o = _rms(o, subln_ref[...]) * (1.0 - lam_init)
        o_ref[:, cs] = o.astype(BF16)

    nblk = seq // WINDOW
    dense = past if latent else seq
    if latent:
        per = TQ // WINDOW
        offsets = tuple(range(-1, per + 1))
        rr = lax.broadcasted_iota(I32, (TQ, WINDOW), 0)
        cc = lax.broadcasted_iota(I32, (TQ, WINDOW), 1)
        band, starts = {}, {}
        for d in offsets:
            blk = qi * per + d
            inside = jnp.logical_and(blk >= 0, blk < nblk)
            band[d] = jnp.logical_and(jnp.abs(rr - cc - d * WINDOW) <= WINDOW, inside)
            starts[d] = pl.multiple_of(past + jnp.clip(blk, 0, nblk - 1) * WINDOW, WINDOW)
    for i in range(SWA_HEADS // 2):
        hk = i // 2
        cs = slice(1536 + i * LANES, 1536 + (i + 1) * LANES)
        qt = qkv_ref[pl.ds(r0, TQ), cs]
        acc = jnp.zeros((TQ, LANES), F32)
        for half in range(2):
            qc = jnp.where(lo, qt, zero_b) if half == 0 else jnp.where(lo, zero_b, qt)
            vsel = vl if half == 0 else vh
            sink = sink_ref[2 * i + half]
            parts = [_dot_nt(qc, ka[hk, 0:dense, :])]
            if latent:
                for d in offsets:
                    s = _dot_nt(qc, ka[hk, pl.ds(starts[d], WINDOW), :])
                    parts.append(jnp.where(band[d], s, NEG_INF))
            s_all = jnp.concatenate(parts, axis=1) if len(parts) > 1 else parts[0]
            m = jnp.maximum(jnp.max(s_all, axis=-1, keepdims=True), sink)
            e = jnp.exp(s_all - m)
            r = 1.0 / (jnp.sum(e, axis=-1, keepdims=True) + jnp.exp(sink - m))
            eb = e.astype(BF16)
            o = _dot(eb[:, 0:dense], vsel[hk, 0:dense, :])
            if latent:
                for k, d in enumerate(offsets):
                    o += _dot(eb[:, dense + k * WINDOW:dense + (k + 1) * WINDOW],
                              vsel[hk, pl.ds(starts[d], WINDOW), :])
            acc += o * r
        o_ref[:, 512 + i * LANES:512 + (i + 1) * LANES] = acc.astype(BF16)


def _attn_even_call(qkv, caches, j, lam, subln, sink, nbatch, seq, past, lam_init):
    n = past + seq
    latent = past > 0
    in_specs = [pl.BlockSpec((seq, EVEN_IN), lambda b, q: (b, 0))]
    args = [qkv]
    if latent:
        ck, cv, sk, sv = caches
        in_specs += [pl.BlockSpec((None, None, past, DIFF_HEADS, 2 * HD), lambda b, q: (b, j, 0, 0, 0)),
                     pl.BlockSpec((None, None, past, DIFF_HEADS, 2 * HD), lambda b, q: (b, j, 0, 0, 0)),
                     pl.BlockSpec((None, None, past, 2, HD), lambda b, q: (b, j, 0, 0, 0)),
                     pl.BlockSpec((None, None, past, 2, HD), lambda b, q: (b, j, 0, 0, 0))]
        args += [ck, cv, sk, sv]
    in_specs += [pl.BlockSpec((4, HD), lambda b, q: (0, 0)),
                 pl.BlockSpec((1, 2 * HD), lambda b, q: (0, 0)),
                 pl.BlockSpec(memory_space=pltpu.SMEM)]
    args += [lam, subln, sink]
    return pl.pallas_call(
        functools.partial(_attn_even_kernel, seq=seq, past=past, lam_init=lam_init),
        grid=(nbatch, seq // TQ),
        in_specs=in_specs,
        out_specs=pl.BlockSpec((TQ, D), lambda b, q: (b * (seq // TQ) + q, 0)),
        out_shape=jax.ShapeDtypeStruct((nbatch * seq, D), BF16),
        scratch_shapes=[pltpu.VMEM((n, 512), BF16), pltpu.VMEM((n, 512), BF16),
                        pltpu.VMEM((2, n, LANES), BF16), pltpu.VMEM((2, n, LANES), BF16),
                        pltpu.VMEM((2, n, LANES), BF16)],
        compiler_params=_cparams(("arbitrary", "arbitrary")),
        name="attn_even_lat" if latent else "attn_even_ctx",
    )(*args)


def _attn_odd_kernel(*refs, seq, past):
    latent = past > 0
    n = past + seq
    if latent:
        q_ref, ckv_ref, kr_ref, cckv_ref, ckr_ref, wk_ref, wv_ref, o_ref, kf, vlo, vhi = refs
    else:
        q_ref, ckv_ref, kr_ref, wk_ref, wv_ref, o_ref, kf, vlo, vhi = refs
    qi = pl.program_id(1)
    lo = lax.broadcasted_iota(I32, (1, LANES), 1) < HD

    @pl.when(qi == 0)
    def _build():
        chunk = 256
        for c0 in range(0, n, chunk):
            rows = slice(c0, c0 + chunk)
            if c0 < past:
                ckv = cckv_ref[c0:c0 + chunk, :].astype(BF16)
                kr = ckr_ref[c0:c0 + chunk, :]
            else:
                ckv = ckv_ref[c0 - past:c0 - past + chunk, :].astype(BF16)
                kr = kr_ref[c0 - past:c0 - past + chunk, :]
            kk = _dot(ckv, wk_ref[...])
            for h in range(MLA_HEADS):
                cs = slice(h * LANES, (h + 1) * LANES)
                kf[rows, cs] = (kk[:, cs] + kr).astype(BF16)
            vv = _dot(ckv, wv_ref[...])
            for i in range(MLA_HEADS // 2):
                cs = slice(i * LANES, (i + 1) * LANES)
                vlo[rows, cs] = jnp.where(lo, vv[:, cs], 0.0).astype(BF16)
                vhi[rows, cs] = jnp.where(lo, 0.0, vv[:, cs]).astype(BF16)

    r0 = pl.multiple_of(qi * TQ, TQ)
    for i in range(MLA_HEADS // 2):
        acc = jnp.zeros((TQ, LANES), F32)
        for half in range(2):
            h = 2 * i + half
            cs = slice(h * LANES, (h + 1) * LANES)
            s = _dot_nt(q_ref[pl.ds(r0, TQ), cs], kf[:, cs])
            m = jnp.max(s, axis=-1, keepdims=True)
            e = jnp.exp(s - m)
            r = 1.0 / jnp.sum(e, axis=-1, keepdims=True)
            vsel = vlo if half == 0 else vhi
            acc += _dot(e.astype(BF16), vsel[:, i * LANES:(i + 1) * LANES]) * r
        o_ref[:, i * LANES:(i + 1) * LANES] = acc.astype(BF16)


def _attn_odd_call(q, ckv, kr, caches, j, wk, wv, nbatch, seq, past):
    n = past + seq
    latent = past > 0
    in_specs = [pl.BlockSpec((seq, MLA_QW), lambda b, qq: (b, 0)),
                pl.BlockSpec((seq, MLA_KV_RANK), lambda b, qq: (b, 0)),
                pl.BlockSpec((seq, LANES), lambda b, qq: (b, 0))]
    args = [q, ckv, kr]
    if latent:
        in_specs += [pl.BlockSpec((None, None, past, MLA_KV_RANK), lambda b, qq: (b, j, 0, 0)),
                     pl.BlockSpec((None, None, past, LANES), lambda b, qq: (b, j, 0, 0))]
        args += list(caches)
    in_specs += [pl.BlockSpec((MLA_KV_RANK, MLA_QW), lambda b, qq: (0, 0)),
                 pl.BlockSpec((MLA_KV_RANK, D), lambda b, qq: (0, 0))]
    args += [wk, wv]
    return pl.pallas_call(
        functools.partial(_attn_odd_kernel, seq=seq, past=past),
        grid=(nbatch, seq // TQ),
        in_specs=in_specs,
        out_specs=pl.BlockSpec((TQ, D), lambda b, qq: (b * (seq // TQ) + qq, 0)),
        out_shape=jax.ShapeDtypeStruct((nbatch * seq, D), BF16),
        scratch_shapes=[pltpu.VMEM((n, MLA_QW), BF16), pltpu.VMEM((n, D), BF16),
                        pltpu.VMEM((n, D), BF16)],
        compiler_params=_cparams(("arbitrary", "arbitrary")),
        name="attn_odd_lat" if latent else "attn_odd_ctx",
    )(*args)


def _outproj_kernel(x_ref, oa_ref, ob_ref, w_ref, mod_ref, g_ref, xo_ref, h_ref, *, ntile_a):
    def run(o_ref):
        x = x_ref[...] + mod_ref[2] * _dot(o_ref[...], w_ref[...])
        xo_ref[...] = x
        h_ref[...] = _modulate(x, g_ref[...], mod_ref[3], mod_ref[4]).astype(BF16)

    @pl.when(pl.program_id(0) < ntile_a)
    def _():
        run(oa_ref)

    @pl.when(pl.program_id(0) >= ntile_a)
    def _():
        run(ob_ref)


def _outproj_call(x, o_a, o_b, w, mod, g, modrow):
    t = x.shape[0]
    na, nb_ = o_a.shape[0] // TM, o_b.shape[0] // TM
    return pl.pallas_call(
        functools.partial(_outproj_kernel, ntile_a=na),
        grid=(t // TM,),
        in_specs=[pl.BlockSpec((TM, D), lambda r: (r, 0)),
                  pl.BlockSpec((TM, D), lambda r: (jnp.minimum(r, na - 1), 0)),
                  pl.BlockSpec((TM, D), lambda r: (jnp.clip(r - na, 0, nb_ - 1), 0)),
                  pl.BlockSpec((D, D), lambda r: (0, 0)),
                  pl.BlockSpec((None, 6, 1, D), lambda r: (modrow(r), 0, 0, 0)),
                  pl.BlockSpec((1, D), lambda r: (0, 0))],
        out_specs=[pl.BlockSpec((TM, D), lambda r: (r, 0)),
                   pl.BlockSpec((TM, D), lambda r: (r, 0))],
        out_shape=[jax.ShapeDtypeStruct((t, D), F32), jax.ShapeDtypeStruct((t, D), BF16)],
        compiler_params=_cparams(("parallel",)),
        name="outproj",
    )(x, o_a, o_b, w, mod, g)


def _router_kernel(h_ref, w_ref, rowid_ref, gate_ref, tab_ref, aff_sc, *, nb, cap):
    b = pl.program_id(1)
    ne = N_EXPERTS
    logits = _dot(h_ref[...], w_ref[...])
    lane = lax.broadcasted_iota(I32, (TB, LANES), 1)
    lg = jnp.where(lane < ne, logits, -jnp.inf)
    e = jnp.exp(lg - jnp.max(lg, axis=-1, keepdims=True))
    aff = e / jnp.sum(e, axis=-1, keepdims=True)
    aff_sc[pl.ds(pl.multiple_of(b * ne, ne), ne), :] = aff.T[0:ne, :]

    @pl.when(b == nb - 1)
    def _select():
        nr = nb * ne
        a = aff_sc[...]
        bits = pltpu.bitcast(a, I32)
        ri = lax.broadcasted_iota(I32, (nr, nr), 0)
        ci = lax.broadcasted_iota(I32, (nr, nr), 1)
        same_e = (ri & (ne - 1)) == (ci & (ne - 1))
        same_b = (ri >> 4) == (ci >> 4)
        m_e = jnp.where(same_e, 1.0, 0.0).astype(BF16)
        m_b = jnp.where(same_b, 1.0, 0.0).astype(BF16)
        m_a = jnp.where(jnp.logical_and(same_e, ci < ri), 1.0, 0.0).astype(BF16)
        m_o = jnp.where(jnp.logical_and(same_b, ci < ri), 1.0, 0.0).astype(BF16)
        ui = lax.broadcasted_iota(I32, (TB, TB), 0)
        uj = lax.broadcasted_iota(I32, (TB, TB), 1)
        upper = jnp.where(ui < uj, 1.0, 0.0).astype(BF16)

        def rows_to_lanes(col):
            return jnp.broadcast_to(col, (nr, LANES)).astype(BF16)

        def count_ge(cand):
            c = jnp.sum(jnp.where(bits >= cand, 1.0, 0.0), axis=-1, keepdims=True)
            return _dot(m_e, rows_to_lanes(c))[:, 0:1]

        def bisect(i, v):
            cand = v | jnp.left_shift(jnp.int32(1), 30 - i)
            return jnp.where(count_ge(cand) >= cap, cand, v)

        thr = lax.fori_loop(0, 31, bisect, jnp.zeros((nr, 1), I32))
        gt = jnp.where(bits > thr, 1.0, 0.0)
        eq = jnp.where(bits == thr, 1.0, 0.0)
        n_gt = _dot(m_e, rows_to_lanes(jnp.sum(gt, axis=-1, keepdims=True)))[:, 0:1]
        need = cap - n_gt
        eq_before = (_dot(m_a, rows_to_lanes(jnp.sum(eq, axis=-1, keepdims=True)))[:, 0:1]
                     + _dot(eq.astype(BF16), upper))
        sel = jnp.where(jnp.logical_and(eq > 0.0, eq_before < need), 1.0, gt)
        local = _dot(sel.astype(BF16), upper)
        cnt = jnp.sum(sel, axis=-1, keepdims=True)
        seg = jnp.floor((cnt + (SEG - 1)) * (1.0 / SEG)) * SEG
        segb = rows_to_lanes(seg)
        over = jnp.maximum(seg - WINR, 0.0)
        overb = rows_to_lanes(over)
        off_over = FIRST_ROWS + _dot(m_o, overb)[:, 0:1]
        off_buf = _dot(m_a, segb)[:, 0:1]
        over_blk = _dot(m_b, overb)[:, 0:1]
        rows_exp = _dot(m_e, segb)[:, 0:1]
        expert = (lax.broadcasted_iota(I32, (nr, 1), 0) & (ne - 1)).astype(F32)
        row = jnp.where(local < WINR, expert * WINR + local, off_over + local - WINR)
        rowid = jnp.where(sel > 0.0, row, -1.0).astype(I32)
        gate = jnp.where(sel > 0.0, a, 0.0)
        for bb in range(nb):
            rowid_ref[bb] = rowid[bb * ne:(bb + 1) * ne, :]
            gate_ref[bb] = gate[bb * ne:(bb + 1) * ne, :]
        tl = lax.broadcasted_iota(I32, (nr, LANES), 1)
        tab = jnp.where(tl == 0, seg, jnp.where(tl == 1, off_over, jnp.where(
            tl == 2, off_buf, jnp.where(tl == 3, over_blk, rows_exp))))
        tab_ref[...] = tab.astype(I32)


def _router_call(h, w_router_pad, ngroups, ntok):
    nb = ntok // TB
    cap = EC_FACTOR * ntok // N_EXPERTS
    nr = nb * N_EXPERTS
    return pl.pallas_call(
        functools.partial(_router_kernel, nb=nb, cap=cap),
        grid=(ngroups, nb),
        in_specs=[pl.BlockSpec((TB, D), lambda g, b: (g * nb + b, 0)),
                  pl.BlockSpec((D, LANES), lambda g, b: (0, 0))],
        out_specs=[pl.BlockSpec((None, nb, N_EXPERTS, TB), lambda g, b: (g, 0, 0, 0)),
                   pl.BlockSpec((None, nb, N_EXPERTS, TB), lambda g, b: (g, 0, 0, 0)),
                   pl.BlockSpec((None, nr, LANES), lambda g, b: (g, 0, 0))],
        out_shape=[jax.ShapeDtypeStruct((ngroups, nb, N_EXPERTS, TB), I32),
                   jax.ShapeDtypeStruct((ngroups, nb, N_EXPERTS, TB), F32),
                   jax.ShapeDtypeStruct((ngroups, nr, LANES), I32)],
        scratch_shapes=[pltpu.VMEM((nr, TB), F32)],
        compiler_params=_cparams(("arbitrary", "arbitrary")),
        name="router",
    )(h, w_router_pad)


STACK_ROWS = N_EXPERTS * TB


FIRST_ROWS = 3 * TB


WINR = 3 * SEG
assert N_EXPERTS * WINR == FIRST_ROWS
TF = 672


def _window_rows(rowid_ref, dst, value_ref=None):
    for e in range(N_EXPERTS):
        rid_e = rowid_ref[e:e + 1, :]
        val_e = 1.0 if value_ref is None else value_ref[e:e + 1, :]
        hit = rid_e == lax.broadcasted_iota(I32, (WINR, TB), 0) + e * WINR
        dst[e * WINR:(e + 1) * WINR, :] = jnp.where(hit, val_e, 0.0).astype(dst.dtype)


def _overflow_groups(seg_s, k):
    return jnp.maximum(seg_s[k] - WINR, 0) // SEG


def _overflow_rows(seg_s, offo_s, step, over, rowid_ref, dst, value_ref=None):
    def zero(i, carry):
        r0 = pl.multiple_of(FIRST_ROWS + i * SEG, SEG)
        dst[pl.ds(r0, SEG), :] = jnp.zeros((SEG, TB), dst.dtype)
        return carry

    lax.fori_loop(0, ((over + TB - 1) // TB) * (TB // SEG), zero, 0)
    for e in range(N_EXPERTS):
        k = step * N_EXPERTS + e
        rid_e = rowid_ref[e:e + 1, :]
        val_e = 1.0 if value_ref is None else value_ref[e:e + 1, :]

        def group(i, carry, off=offo_s[k], rid_e=rid_e, val_e=val_e):
            r0 = pl.multiple_of(off + i * SEG, SEG)
            hit = rid_e == lax.broadcasted_iota(I32, (SEG, TB), 0) + r0
            dst[pl.ds(r0, SEG), :] = jnp.where(hit, val_e, 0.0).astype(dst.dtype)
            return carry

        lax.fori_loop(0, _overflow_groups(seg_s, k), group, 0)


def _wait_rows(rows, make_copy):
    def big(i, carry):
        make_copy(TB).wait()
        return carry

    def small(i, carry):
        make_copy(SEG).wait()
        return carry

    lax.fori_loop(0, rows // TB, big, 0)
    lax.fori_loop(0, (rows % TB) // SEG, small, 0)


def _dispatch_kernel(seg_s, offo_s, offb_s, over_s, h_ref, rowid_ref, xe_hbm,
                     onehot, stack, zbuf, sem, zsem, *, nb, nsteps, cap):
    g = pl.program_id(0)
    b = pl.program_id(1)
    step = g * nb + b
    slot = lax.rem(step, 2)
    over = over_s[step]
    xrows = xe_hbm.shape[2]

    def wait_slot(nrows, sl):
        _wait_rows(nrows, lambda n: pltpu.make_async_copy(
            stack.at[sl, pl.ds(0, n)], xe_hbm.at[0, 0, pl.ds(0, n)], sem.at[sl]))

    @pl.when(step == 0)
    def _init():
        stack[...] = jnp.zeros_like(stack)

    @pl.when(b == 0)
    def _zero_unused():
        zbuf[...] = jnp.zeros_like(zbuf)
        for e in range(N_EXPERTS):
            pltpu.make_async_copy(zbuf, xe_hbm.at[g, e, pl.ds(cap, xrows - cap)], zsem).start()

    _window_rows(rowid_ref, onehot)
    h = h_ref[...]
    stack[slot, 0:FIRST_ROWS, :] = _dot(onehot[0:FIRST_ROWS, :], h).astype(BF16)

    @pl.when(over > 0)
    def _overflow():
        _overflow_rows(seg_s, offo_s, step, over, rowid_ref, onehot)

        def chunk(c, carry):
            base = pl.multiple_of(FIRST_ROWS + c * TB, TB)
            stack[slot, pl.ds(base, TB), :] = _dot(onehot[pl.ds(base, TB), :], h).astype(BF16)
            return carry

        lax.fori_loop(0, (over + TB - 1) // TB, chunk, 0)

    @pl.when(b == 0)
    def _zero_unused_done():
        for e in range(N_EXPERTS):
            pltpu.make_async_copy(zbuf, xe_hbm.at[g, e, pl.ds(cap, xrows - cap)], zsem).wait()

    @pl.when(step >= 1)
    def _previous_landed():
        wait_slot(FIRST_ROWS + over_s[step - 1], 1 - slot)

    for e in range(N_EXPERTS):
        k = step * N_EXPERTS + e
        pltpu.make_async_copy(
            stack.at[slot, e * WINR:(e + 1) * WINR],
            xe_hbm.at[g, e, pl.ds(pl.multiple_of(offb_s[k], SEG), WINR)], sem.at[slot]).start()

    @pl.when(over > 0)
    def _overflow_copies():
        for e in range(N_EXPERTS):
            k = step * N_EXPERTS + e

            def one(i, carry, e=e, k=k):
                pltpu.make_async_copy(
                    stack.at[slot, pl.ds(pl.multiple_of(offo_s[k] + i * SEG, SEG), SEG)],
                    xe_hbm.at[g, e, pl.ds(pl.multiple_of(offb_s[k] + WINR + i * SEG, SEG), SEG)],
                    sem.at[slot]).start()
                return carry

            lax.fori_loop(0, _overflow_groups(seg_s, k), one, 0)

    @pl.when(step == nsteps - 1)
    def _drain():
        wait_slot(FIRST_ROWS + over, slot)


def _expert_rows(ntok):
    cap = EC_FACTOR * ntok // N_EXPERTS
    worst = cap + (ntok // TB) * (SEG - 1)
    tiles = -(-(worst + WINR) // TF)
    assert cap % SEG == 0 and cap >= WINR and tiles * TF > cap
    return cap, tiles


def _dispatch_call(tabs, h, rowid, ngroups, ntok):
    nb = ntok // TB
    cap, tiles = _expert_rows(ntok)
    xrows = tiles * TF
    grid_spec = pltpu.PrefetchScalarGridSpec(
        num_scalar_prefetch=4,
        grid=(ngroups, nb),
        in_specs=[pl.BlockSpec((TB, D), lambda g, b, *_: (g * nb + b, 0)),
                  pl.BlockSpec((None, None, N_EXPERTS, TB), lambda g, b, *_: (g, b, 0, 0))],
        out_specs=pl.BlockSpec(memory_space=pl.ANY),
        scratch_shapes=[pltpu.VMEM((STACK_ROWS, TB), BF16), pltpu.VMEM((2, STACK_ROWS, D), BF16),
                        pltpu.VMEM((xrows - cap, D), BF16),
                        pltpu.SemaphoreType.DMA((2,)), pltpu.SemaphoreType.DMA])
    return pl.pallas_call(
        functools.partial(_dispatch_kernel, nb=nb, nsteps=ngroups * nb, cap=cap),
        grid_spec=grid_spec,
        out_shape=jax.ShapeDtypeStruct((ngroups, N_EXPERTS, xrows, D), BF16),
        compiler_params=_cparams(("arbitrary", "arbitrary")),
        name="dispatch",
    )(*tabs, h, rowid)


def _ffn_kernel(nt_s, xe_ref, wg_hbm, wu_hbm, wd_hbm, y_ref, wg32, wu32, wd32, wgb, wub, wdb, sem,
                *, layer):
    e = pl.program_id(0)
    g = pl.program_id(1)
    j = pl.program_id(2)

    def weight_copies(ee, sl):
        return (pltpu.make_async_copy(wg_hbm.at[layer, ee], wg32.at[sl], sem.at[sl]),
                pltpu.make_async_copy(wu_hbm.at[layer, ee], wu32.at[sl], sem.at[sl]),
                pltpu.make_async_copy(wd_hbm.at[layer, ee], wd32.at[sl], sem.at[sl]))

    @pl.when(jnp.logical_and(g == 0, j == 0))
    def _weights():
        sl = lax.rem(e, 2)

        @pl.when(e == 0)
        def _():
            for cp in weight_copies(e, sl):
                cp.start()

        @pl.when(e + 1 < N_EXPERTS)
        def _():
            for cp in weight_copies(e + 1, 1 - sl):
                cp.start()

        for cp in weight_copies(e, sl):
            cp.wait()
        wgb[...] = wg32[sl].astype(BF16)
        wub[...] = wu32[sl].astype(BF16)
        wdb[...] = wd32[sl].astype(BF16)

    live = j < nt_s[g * N_EXPERTS + e]

    @pl.when(live)
    def _run():
        x = xe_ref[...]
        hid = (_silu(_dot(x, wgb[...])) * _dot(x, wub[...])).astype(BF16)
        y_ref[...] = _dot(hid, wdb[...]).astype(BF16)

    @pl.when(jnp.logical_not(live))
    def _skip():
        y_ref[...] = jnp.zeros_like(y_ref)


def _ffn_call(ntiles, xe, wg, wu, wd, layer, ngroups, ntok):
    _, tiles = _expert_rows(ntok)

    def xmap(e, g, j, nt):
        return (g, e, jnp.minimum(j, nt[g * N_EXPERTS + e] - 1), 0)

    grid_spec = pltpu.PrefetchScalarGridSpec(
        num_scalar_prefetch=1,
        grid=(N_EXPERTS, ngroups, tiles),
        in_specs=[pl.BlockSpec((None, None, TF, D), xmap),
                  pl.BlockSpec(memory_space=pl.ANY), pl.BlockSpec(memory_space=pl.ANY),
                  pl.BlockSpec(memory_space=pl.ANY)],
        out_specs=pl.BlockSpec((None, None, TF, D), lambda e, g, j, nt: (g, e, j, 0)),
        scratch_shapes=[pltpu.VMEM((2, D, EXPERT_FF), F32), pltpu.VMEM((2, D, EXPERT_FF), F32),
                        pltpu.VMEM((2, EXPERT_FF, D), F32),
                        pltpu.VMEM((D, EXPERT_FF), BF16), pltpu.VMEM((D, EXPERT_FF), BF16),
                        pltpu.VMEM((EXPERT_FF, D), BF16), pltpu.SemaphoreType.DMA((2,))])
    return pl.pallas_call(
        functools.partial(_ffn_kernel, layer=layer),
        grid_spec=grid_spec,
        out_shape=jax.ShapeDtypeStruct((ngroups, N_EXPERTS, tiles * TF, D), BF16),
        compiler_params=_cparams(("arbitrary", "arbitrary", "arbitrary")),
        name="ffn",
    )(ntiles, xe, wg, wu, wd)


def _combine_kernel(seg_s, offo_s, offb_s, over_s, y_hbm, rowid_ref, gate_ref, x_ref, mod_ref,
                    modn_ref, g_ref, xo_ref, h_ref, weights, stack, acc, sem, *, nb, nsteps, final):
    g = pl.program_id(0)
    b = pl.program_id(1)
    step = g * nb + b
    slot = lax.rem(step, 2)
    over = over_s[step]

    def fetch(st, sl):
        gg = st // nb
        for e in range(N_EXPERTS):
            k = st * N_EXPERTS + e
            pltpu.make_async_copy(
                y_hbm.at[gg, e, pl.ds(pl.multiple_of(offb_s[k], SEG), WINR)],
                stack.at[sl, e * WINR:(e + 1) * WINR], sem.at[sl]).start()

        @pl.when(over_s[st] > 0)
        def _():
            for e in range(N_EXPERTS):
                k = st * N_EXPERTS + e

                def one(i, carry, e=e, k=k):
                    pltpu.make_async_copy(
                        y_hbm.at[gg, e, pl.ds(pl.multiple_of(offb_s[k] + WINR + i * SEG, SEG), SEG)],
                        stack.at[sl, pl.ds(pl.multiple_of(offo_s[k] + i * SEG, SEG), SEG)],
                        sem.at[sl]).start()
                    return carry

                lax.fori_loop(0, _overflow_groups(seg_s, k), one, 0)

    @pl.when(step == 0)
    def _first():
        stack[...] = jnp.zeros_like(stack)
        fetch(step, slot)

    if nsteps > 1:
        @pl.when(step + 1 < nsteps)
        def _prefetch():
            fetch(step + 1, 1 - slot)

    _window_rows(rowid_ref, weights, gate_ref)

    @pl.when(over > 0)
    def _():
        _overflow_rows(seg_s, offo_s, step, over, rowid_ref, weights, gate_ref)

    _wait_rows(FIRST_ROWS + over, lambda n: pltpu.make_async_copy(
        y_hbm.at[0, 0, pl.ds(0, n)], stack.at[slot, pl.ds(0, n)], sem.at[slot]))

    def token_weights(base):
        return weights[pl.ds(base, TB), :].T.astype(BF16)

    w = jnp.concatenate([token_weights(c * TB) for c in range(FIRST_ROWS // TB)], axis=1)
    acc[...] = _dot(w, stack[slot, 0:FIRST_ROWS, :])

    @pl.when(over > 0)
    def _():
        def chunk(c, carry):
            base = pl.multiple_of(FIRST_ROWS + c * TB, TB)
            acc[...] += _dot(token_weights(base), stack[slot, pl.ds(base, TB), :])
            return carry

        lax.fori_loop(0, (over + TB - 1) // TB, chunk, 0)

    x = x_ref[...] + mod_ref[5] * acc[...]
    if final:
        y = _rms(x, g_ref[...])

        @pl.when(g == 0)
        def _():
            xo_ref[...] = y

        @pl.when(g != 0)
        def _():
            h_ref[...] = y
    else:
        xo_ref[...] = x
        h_ref[...] = _modulate(x, g_ref[...], modn_ref[0], modn_ref[1]).astype(BF16)


def _combine_call(tabs, y, rowid, gate, x, mod, modn, gvec, modrow, ngroups, ntok, final):
    nb = ntok // TB
    t = x.shape[0]
    if final:
        assert ngroups == 2
        out_specs = [pl.BlockSpec((TB, D), lambda g, b, *_: (jnp.where(g == 0, b, nb - 1), 0)),
                     pl.BlockSpec((TB, D), lambda g, b, *_: (jnp.where(g == 0, 0, b), 0))]
        out_shape = [jax.ShapeDtypeStruct((ntok, D), F32), jax.ShapeDtypeStruct((ntok, D), F32)]
    else:
        out_specs = [pl.BlockSpec((TB, D), lambda g, b, *_: (g * nb + b, 0)),
                     pl.BlockSpec((TB, D), lambda g, b, *_: (g * nb + b, 0))]
        out_shape = [jax.ShapeDtypeStruct((t, D), F32), jax.ShapeDtypeStruct((t, D), BF16)]
    grid_spec = pltpu.PrefetchScalarGridSpec(
        num_scalar_prefetch=4,
        grid=(ngroups, nb),
        in_specs=[pl.BlockSpec(memory_space=pl.ANY),
                  pl.BlockSpec((None, None, N_EXPERTS, TB), lambda g, b, *_: (g, b, 0, 0)),
                  pl.BlockSpec((None, None, N_EXPERTS, TB), lambda g, b, *_: (g, b, 0, 0)),
                  pl.BlockSpec((TB, D), lambda g, b, *_: (g * nb + b, 0)),
                  pl.BlockSpec((None, 6, 1, D), lambda g, b, *_: (modrow(g * nb + b), 0, 0, 0)),
                  pl.BlockSpec((None, 6, 1, D), lambda g, b, *_: (modrow(g * nb + b), 0, 0, 0)),
                  pl.BlockSpec((1, D), lambda g, b, *_: (0, 0))],
        out_specs=out_specs,
        scratch_shapes=[pltpu.VMEM((STACK_ROWS, TB), F32), pltpu.VMEM((2, STACK_ROWS, D), BF16),
                        pltpu.VMEM((TB, D), F32), pltpu.SemaphoreType.DMA((2,))])
    return pl.pallas_call(
        functools.partial(_combine_kernel, nb=nb, nsteps=ngroups * nb, final=final),
        grid_spec=grid_spec,
        out_shape=out_shape,
        compiler_params=_cparams(("arbitrary", "arbitrary")),
        name="combine",
    )(*tabs, y, rowid, gate, x, mod, modn, gvec)


def kernel(x_prompt, x_sample, cache_diff_k, cache_diff_v, cache_swa_k, cache_swa_v, cache_mla_ckv, cache_mla_krope, c, c_ctx, w_ada, b_ada, norm_mix, norm_ffn, w_in_even, w_out_even, diff_lambda, diff_subln, swa_sink, w_in_odd, mla_q_norm, w_q_up, mla_kv_norm, w_kv_up, w_out_odd, w_router, w_gate_exp, w_up_exp, w_down_exp, final_norm):
    batch, seq, _ = x_prompt.shape
    dec_batch, dec_seq, _ = x_sample.shape
    past = cache_diff_k.shape[2]
    depth = w_ada.shape[0]
    n_even = w_in_even.shape[0]
    n_odd = w_in_odd.shape[0]
    nc, ns = batch * seq, dec_batch * dec_seq
    assert nc == ns, "the routed-expert kernels take two token groups of equal size"
    assert seq % TM == 0 and dec_seq % TM == 0 and past % 256 == 0 and dec_seq % GRID_W == 0
    ntok = nc

    def modrow(r):
        tok = r * TM
        return jnp.where(tok < nc, 0, 1 + jnp.maximum(tok - nc, 0) // dec_seq)

    rc = -(-(1 + dec_batch) // 16) * 16
    cvec = jnp.zeros((rc, D), F32).at[0].set(c_ctx).at[1:1 + dec_batch].set(c)
    mods = _ada_call(cvec, w_ada, b_ada).reshape(depth, rc, 6, 1, D)

    w_even_b = w_in_even.astype(BF16)
    w_oute_b = w_out_even.astype(BF16)
    w_outo_b = w_out_odd.astype(BF16)
    kr_pad = jnp.zeros((n_odd, D, LANES), F32).at[:, :, 64:96].set(w_in_odd[:, :, 640:672])
    w_odd_b = jnp.concatenate([w_in_odd[:, :, :640], kr_pad], axis=-1).astype(BF16)
    wq = w_q_up.reshape(n_odd, MLA_Q_RANK, MLA_HEADS, HD + MLA_ROPE)
    wq_b = jnp.pad(wq, ((0, 0), (0, 0), (0, 0), (0, LANES - HD - MLA_ROPE))).reshape(
        n_odd, MLA_Q_RANK, MLA_QW).astype(BF16)
    wkv = w_kv_up.reshape(n_odd, MLA_KV_RANK, MLA_HEADS, 2 * HD)
    wk_b = jnp.pad(wkv[..., :HD], ((0, 0), (0, 0), (0, 0), (0, LANES - HD))).reshape(
        n_odd, MLA_KV_RANK, MLA_QW).astype(BF16)
    wv_b = wkv[..., HD:].reshape(n_odd, MLA_KV_RANK, D).astype(BF16)
    w_router_b = jnp.pad(w_router, ((0, 0), (0, 0), (0, LANES - N_EXPERTS))).astype(BF16)
    even_tabs = _rope_tables(dec_seq, 16, _even_lane)
    mla_tabs = _rope_tables(dec_seq, 8, _mla_lane)
    ckr = jnp.zeros((dec_batch, n_odd, past, LANES), F32).at[..., 64:96].set(cache_mla_krope)
    even_caches = (cache_diff_k, cache_diff_v, cache_swa_k, cache_swa_v)

    x = jnp.concatenate([x_prompt.reshape(nc, D), x_sample.reshape(ns, D)], axis=0)
    h = _norm_mod_call(x, mods[0], norm_mix[0:1], modrow)
    kd_new, vd_new, ks_new, vs_new, ckv_odd, kr_odd = [], [], [], [], [], []
    y_prompt = y_sample = None
    for i in range(depth):
        j = i // 2
        if i % 2 == 0:
            qkv_c, kd, vd, ks, vs = _proj_even_call(h, w_even_b[j], 0, nc, None, dec_seq)
            (qkv_l,) = _proj_even_call(h, w_even_b[j], nc, ns, even_tabs, dec_seq)
            kd_new.append(kd.reshape(batch, seq, DIFF_HEADS, 2 * HD))
            vd_new.append(vd.reshape(batch, seq, DIFF_HEADS, 2 * HD))
            ks_new.append(ks.reshape(batch, seq, 2, HD))
            vs_new.append(vs.reshape(batch, seq, 2, HD))
            li = _lambda_init(i)
            o_c = _attn_even_call(qkv_c, None, j, diff_lambda[j], diff_subln[j:j + 1], swa_sink[j],
                                  batch, seq, 0, li)
            o_l = _attn_even_call(qkv_l, even_caches, j, diff_lambda[j], diff_subln[j:j + 1],
                                  swa_sink[j], dec_batch, dec_seq, past, li)
            w_out = w_oute_b[j]
        else:
            q_c, ckv_c, kr_c = _proj_odd_call(h, w_odd_b[j], mla_q_norm[j:j + 1], wq_b[j],
                                              mla_kv_norm[j:j + 1], 0, nc, None, dec_seq)
            q_l, ckv_l, kr_l = _proj_odd_call(h, w_odd_b[j], mla_q_norm[j:j + 1], wq_b[j],
                                              mla_kv_norm[j:j + 1], nc, ns, mla_tabs, dec_seq)
            ckv_odd.append(ckv_c.reshape(batch, seq, MLA_KV_RANK))
            kr_odd.append(kr_c.reshape(batch, seq, LANES)[..., 64:96])
            o_c = _attn_odd_call(q_c, ckv_c, kr_c, None, j, wk_b[j], wv_b[j], batch, seq, 0)
            o_l = _attn_odd_call(q_l, ckv_l, kr_l, (cache_mla_ckv, ckr), j,
                                 wk_b[j], wv_b[j], dec_batch, dec_seq, past)
            w_out = w_outo_b[j]
        x, h2 = _outproj_call(x, o_c, o_l, w_out, mods[i], norm_ffn[i:i + 1], modrow)

        rowid, gate, tab = _router_call(h2, w_router_b[i], 2, ntok)
        seg_t = tab[:, :, 0].reshape(-1)
        offo_t = tab[:, :, 1].reshape(-1)
        offb_t = tab[:, :, 2].reshape(-1)
        over_t = tab[:, ::N_EXPERTS, 3].reshape(-1)
        rexp_t = tab[:, :N_EXPERTS, 4].reshape(-1)
        xe = _dispatch_call((seg_t, offo_t, offb_t, over_t), h2, rowid, 2, ntok)
        ntile_t = (rexp_t + TF - 1) // TF
        y = _ffn_call(ntile_t, xe, w_gate_exp, w_up_exp, w_down_exp, i, 2, ntok)
        final = i == depth - 1
        nxt = i if final else i + 1
        gvec = final_norm.reshape(1, D) if final else norm_mix[nxt:nxt + 1]
        out_a, out_b = _combine_call((seg_t, offo_t, offb_t, over_t), y, rowid, gate, x, mods[i],
                                     mods[nxt], gvec, modrow, 2, ntok, final)
        if final:
            y_prompt = out_a.reshape(batch, seq, D)
            y_sample = out_b.reshape(dec_batch, dec_seq, D)
        else:
            x, h = out_a, out_b

    return (y_prompt, y_sample, jnp.stack(kd_new, axis=1), jnp.stack(vd_new, axis=1),
            jnp.stack(ks_new, axis=1), jnp.stack(vs_new, axis=1), jnp.stack(ckv_odd, axis=1),
            jnp.stack(kr_odd, axis=1))
```

```python
import functools
import math

import jax
import jax.numpy as jnp
from jax import lax
from jax.experimental import pallas as pl
from jax.experimental.pallas import tpu as pltpu

F32 = jnp.float32
BF16 = jnp.bfloat16
I32 = jnp.int32

D = 1024
HD = 64
GRID_W = 64
WINDOW = 128
DIFF_HEADS = 4
SWA_HEADS = 8
MLA_HEADS = 16
MLA_Q_RANK = 384
MLA_KV_RANK = 256
MLA_ROPE = 32
N_EXPERTS = 16
EXPERT_FF = 512
EC_FACTOR = 2
ROPE_BASE = 10000.0
EPS = 1e-6
NEG_INF = -1e30
EVEN_IN = 2304
LANES = 128
TM = 256
TQ = 256
TB = 256
SEG = 16
VMEM_LIMIT = 56 * 1024 * 1024


def _cparams(sem, vmem=VMEM_LIMIT):
    return pltpu.CompilerParams(dimension_semantics=sem, vmem_limit_bytes=vmem)


def _dot(a, b):
    return jnp.dot(a, b, preferred_element_type=F32)


def _dot_nt(a, b):
    return lax.dot_general(a, b, (((1,), (1,)), ((), ())), preferred_element_type=F32)


def _silu(x):
    return x / (1.0 + jnp.exp(-x))


def _rms(x, g):
    ms = jnp.mean(x * x, axis=-1, keepdims=True)
    return x * lax.rsqrt(ms + EPS) * g


def _modulate(x, g, shift, scale):
    return _rms(x, g) * (1.0 + scale) + shift


def _lambda_init(layer):
    return 0.8 - 0.6 * math.exp(-0.3 * layer)


def _ada_kernel(c_ref, w_ref, b_ref, o_ref):
    s = _silu(c_ref[...]).astype(BF16)
    o_ref[...] = _dot(s, w_ref[...].astype(BF16)) + b_ref[...]


def _ada_call(cvec, w_ada, b_ada):
    depth, _, n6 = w_ada.shape
    rc = cvec.shape[0]
    tn = 512
    return pl.pallas_call(
        _ada_kernel,
        grid=(depth, n6 // tn),
        in_specs=[pl.BlockSpec((rc, D), lambda i, n: (0, 0)),
                  pl.BlockSpec((None, D, tn), lambda i, n: (i, 0, n)),
                  pl.BlockSpec((None, 1, tn), lambda i, n: (i, 0, n))],
        out_specs=pl.BlockSpec((None, rc, tn), lambda i, n: (i, 0, n)),
        out_shape=jax.ShapeDtypeStruct((depth, rc, n6), F32),
        compiler_params=_cparams(("parallel", "parallel")),
        name="ada",
    )(cvec, w_ada, b_ada.reshape(depth, 1, n6))


def _two_part_specs(rows_a, rows_b, width, off_b=0):
    na, nb_ = rows_a // TM, rows_b // TM
    return (pl.BlockSpec((TM, width), lambda r: (jnp.minimum(r, na - 1), 0)),
            pl.BlockSpec((TM, width), lambda r: (off_b + jnp.clip(r - na, 0, nb_ - 1), 0)))


def _norm_mod_kernel(xa_ref, xb_ref, mod_ref, g_ref, h_ref, *, ntile_a):
    def run(x_ref):
        h_ref[...] = _modulate(x_ref[...], g_ref[...], mod_ref[0], mod_ref[1]).astype(BF16)

    @pl.when(pl.program_id(0) < ntile_a)
    def _():
        run(xa_ref)

    @pl.when(pl.program_id(0) >= ntile_a)
    def _():
        run(xb_ref)


def _norm_mod_call(xa, xb, mods, layer, gains, modrow):
    na, nb_ = xa.shape[0], xb.shape[0]
    return pl.pallas_call(
        functools.partial(_norm_mod_kernel, ntile_a=na // TM),
        grid=((na + nb_) // TM,),
        in_specs=[*_two_part_specs(na, nb_, D),
                  pl.BlockSpec((None, None, 6, 1, D), lambda r: (layer, modrow(r), 0, 0, 0)),
                  pl.BlockSpec((None, 1, D), lambda r: (layer, 0, 0))],
        out_specs=pl.BlockSpec((TM, D), lambda r: (r, 0)),
        out_shape=jax.ShapeDtypeStruct((na + nb_, D), BF16),
        compiler_params=_cparams(("parallel",)),
        name="norm_mod",
    )(xa, xb, mods, gains)


def _rope_tables(dec_seq, half, lane_of_dim):
    pos = jnp.arange(dec_seq)
    row = (pos // GRID_W).astype(F32)
    col = (pos % GRID_W).astype(F32)
    inv = ROPE_BASE ** (-(jnp.arange(half, dtype=F32) / half))
    ang = jnp.stack([row[:, None] * inv[None, :], col[:, None] * inv[None, :]])
    cos, sin = jnp.cos(ang), jnp.sin(ang)
    c_cols, s1_cols, s2_cols = [], [], []
    one, zero = jnp.ones((dec_seq,), F32), jnp.zeros((dec_seq,), F32)
    for lane in range(LANES):
        info = lane_of_dim(lane)
        if info is None:
            c_cols.append(one); s1_cols.append(zero); s2_cols.append(zero)
            continue
        axis, k, second = info
        c_cols.append(cos[axis, :, k])
        if second:
            s1_cols.append(zero); s2_cols.append(sin[axis, :, k])
        else:
            s1_cols.append(-sin[axis, :, k]); s2_cols.append(zero)
    return (jnp.stack(c_cols, axis=1), jnp.stack(s1_cols, axis=1), jnp.stack(s2_cols, axis=1))


def _even_lane(lane):
    j = lane % HD
    axis, jj = j // 32, j % 32
    return axis, jj % 16, jj >= 16


def _mla_lane(lane):
    if lane < 64 or lane >= 96:
        return None
    jj = lane - 64
    axis, k = jj // 16, jj % 16
    return axis, k % 8, k >= 8


def _rope(x, c, s1, s2, shift):
    return x * c + pltpu.roll(x, LANES - shift, 1) * s1 + pltpu.roll(x, shift, 1) * s2


_EVEN_ROPE_TILES = tuple(range(0, 8)) + tuple(range(12, 17))
_EVEN_Q_TILES = tuple(range(0, 4)) + tuple(range(12, 16))


def _proj_even_kernel(*refs, rope, caches):
    h_ref, w_ref = refs[0], refs[1]
    pos = 2
    if rope:
        c_ref, s1_ref, s2_ref = refs[2:5]
        pos = 5
    qkv_ref = refs[pos]
    res = _dot(h_ref[...], w_ref[...])
    scale = HD ** -0.5
    for t in range(EVEN_IN // LANES):
        x = res[:, t * LANES:(t + 1) * LANES]
        if rope and t in _EVEN_ROPE_TILES:
            x = _rope(x, c_ref[...], s1_ref[...], s2_ref[...], 16)
        if t in _EVEN_Q_TILES:
            x = x * scale
        qkv_ref[:, t * LANES:(t + 1) * LANES] = x.astype(BF16)
    if caches:
        kd_ref, vd_ref, ks_ref, vs_ref = refs[pos + 1:pos + 5]
        kd_ref[...] = res[:, 512:1024]
        vd_ref[...] = res[:, 1024:1536]
        ks_ref[...] = res[:, 2048:2176]
        vs_ref[...] = res[:, 2176:2304]


def _proj_even_call(h, w, j, row0, nrows, tables, dec_seq):
    rope = tables is not None
    caches = not rope
    t0 = row0 // TM
    in_specs = [pl.BlockSpec((TM, D), lambda r: (t0 + r, 0)),
                pl.BlockSpec((None, D, EVEN_IN), lambda r: (j, 0, 0))]
    args = [h, w]
    if rope:
        per = dec_seq // TM
        for _ in range(3):
            in_specs.append(pl.BlockSpec((TM, LANES), lambda r: (r % per, 0)))
        args += list(tables)
    out_specs = [pl.BlockSpec((TM, EVEN_IN), lambda r: (r, 0))]
    out_shape = [jax.ShapeDtypeStruct((nrows, EVEN_IN), BF16)]
    if caches:
        for width in (512, 512, LANES, LANES):
            out_specs.append(pl.BlockSpec((TM, width), lambda r: (r, 0)))
            out_shape.append(jax.ShapeDtypeStruct((nrows, width), F32))
    return pl.pallas_call(
        functools.partial(_proj_even_kernel, rope=rope, caches=caches),
        grid=(nrows // TM,),
        in_specs=in_specs, out_specs=out_specs, out_shape=out_shape,
        compiler_params=_cparams(("parallel",)),
        name="proj_even_lat" if rope else "proj_even_ctx",
    )(*args)


ODD_IN_PAD = MLA_Q_RANK + MLA_KV_RANK + LANES
MLA_QW = MLA_HEADS * LANES


def _proj_odd_kernel(*refs, rope):
    h_ref, w_ref, qn_ref, wq_ref, kvn_ref = refs[:5]
    pos = 5
    if rope:
        c_ref, s1_ref, s2_ref = refs[5:8]
        pos = 8
    q_ref, ckv_ref, kr_ref = refs[pos:pos + 3]
    res = _dot(h_ref[...], w_ref[...])
    cq = _rms(res[:, :MLA_Q_RANK], qn_ref[...]).astype(BF16)
    ckv_ref[...] = _rms(res[:, MLA_Q_RANK:MLA_Q_RANK + MLA_KV_RANK], kvn_ref[...])
    kr = res[:, MLA_Q_RANK + MLA_KV_RANK:]
    if rope:
        kr = _rope(kr, c_ref[...], s1_ref[...], s2_ref[...], 8)
    kr_ref[...] = kr
    q = _dot(cq, wq_ref[...])
    scale = (HD + MLA_ROPE) ** -0.5
    for t in range(MLA_HEADS):
        x = q[:, t * LANES:(t + 1) * LANES]
        if rope:
            x = _rope(x, c_ref[...], s1_ref[...], s2_ref[...], 8)
        q_ref[:, t * LANES:(t + 1) * LANES] = (x * scale).astype(BF16)


def _proj_odd_call(h, w_in, qn, wq, kvn, j, row0, nrows, tables, dec_seq):
    rope = tables is not None
    t0 = row0 // TM
    in_specs = [pl.BlockSpec((TM, D), lambda r: (t0 + r, 0)),
                pl.BlockSpec((None, D, ODD_IN_PAD), lambda r: (j, 0, 0)),
                pl.BlockSpec((None, 1, MLA_Q_RANK), lambda r: (j, 0, 0)),
                pl.BlockSpec((None, MLA_Q_RANK, MLA_QW), lambda r: (j, 0, 0)),
                pl.BlockSpec((None, 1, MLA_KV_RANK), lambda r: (j, 0, 0))]
    args = [h, w_in, qn, wq, kvn]
    if rope:
        per = dec_seq // TM
        for _ in range(3):
            in_specs.append(pl.BlockSpec((TM, LANES), lambda r: (r % per, 0)))
        args += list(tables)
    return pl.pallas_call(
        functools.partial(_proj_odd_kernel, rope=rope),
        grid=(nrows // TM,),
        in_specs=in_specs,
        out_specs=[pl.BlockSpec((TM, MLA_QW), lambda r: (r, 0)),
                   pl.BlockSpec((TM, MLA_KV_RANK), lambda r: (r, 0)),
                   pl.BlockSpec((TM, LANES), lambda r: (r, 0))],
        out_shape=[jax.ShapeDtypeStruct((nrows, MLA_QW), BF16),
                   jax.ShapeDtypeStruct((nrows, MLA_KV_RANK), F32),
                   jax.ShapeDtypeStruct((nrows, LANES), F32)],
        compiler_params=_cparams(("parallel",)),
        name="proj_odd_lat" if rope else "proj_odd_ctx",
    )(*args)


def _attn_even_kernel(*refs, seq, past, lam_init, layer):
    latent = past > 0
    n = past + seq
    if latent:
        (qkv_ref, ck_ref, cv_ref, sk_ref, sv_ref, lam_ref, subln_ref, sink_ref,
         o_ref, kd, vd, ka, vl, vh) = refs
    else:
        qkv_ref, lam_ref, subln_ref, sink_ref, o_ref, kd, vd, ka, vl, vh = refs
    qi = pl.program_id(1)
    lo = lax.broadcasted_iota(I32, (1, LANES), 1) < HD

    @pl.when(qi == 0)
    def _build():
        chunk = 256
        for c0 in range(0, n, chunk):
            rows = slice(c0, c0 + chunk)
            if c0 < past:
                prow = slice(c0, c0 + chunk)
                for h in range(DIFF_HEADS):
                    kd[rows, h * LANES:(h + 1) * LANES] = ck_ref[prow, h, :].astype(BF16)
                    vd[rows, h * LANES:(h + 1) * LANES] = cv_ref[prow, h, :].astype(BF16)
                kt = jnp.concatenate([sk_ref[prow, 0, :], sk_ref[prow, 1, :]], axis=1)
                vt = jnp.concatenate([sv_ref[prow, 0, :], sv_ref[prow, 1, :]], axis=1)
            else:
                orow = slice(c0 - past, c0 - past + chunk)
                kd[rows, :] = qkv_ref[orow, 512:1024]
                vd[rows, :] = qkv_ref[orow, 1024:1536]
                kt = qkv_ref[orow, 2048:2176].astype(F32)
                vt = qkv_ref[orow, 2176:2304].astype(F32)
            kr = pltpu.roll(kt, HD, 1)
            vr = pltpu.roll(vt, HD, 1)
            ka[0, rows, :] = jnp.where(lo, kt, kr).astype(BF16)
            ka[1, rows, :] = jnp.where(lo, kr, kt).astype(BF16)
            vl[0, rows, :] = jnp.where(lo, vt, 0.0).astype(BF16)
            vh[0, rows, :] = jnp.where(lo, 0.0, vr).astype(BF16)
            vl[1, rows, :] = jnp.where(lo, vr, 0.0).astype(BF16)
            vh[1, rows, :] = jnp.where(lo, 0.0, vt).astype(BF16)

    r0 = pl.multiple_of(qi * TQ, TQ)
    lam = lam_ref[...]
    lam_full = (jnp.exp(jnp.sum(lam[0:1] * lam[1:2], axis=-1, keepdims=True))
                - jnp.exp(jnp.sum(lam[2:3] * lam[3:4], axis=-1, keepdims=True)) + lam_init)
    zero_b = jnp.zeros((), BF16)

    for h in range(DIFF_HEADS):
        cs = slice(h * LANES, (h + 1) * LANES)
        qt = qkv_ref[pl.ds(r0, TQ), cs]
        kh = kd[:, cs]
        es, rs = [], []
        for comp in range(2):
            qc = jnp.where(lo, qt, zero_b) if comp == 0 else jnp.where(lo, zero_b, qt)
            s = _dot_nt(qc, kh)
            m = jnp.max(s, axis=-1, keepdims=True)
            e = jnp.exp(s - m)
            es.append(e)
            rs.append(1.0 / jnp.sum(e, axis=-1, keepdims=True))
        a = es[0] * rs[0] - es[1] * (lam_full * rs[1])
        o = _dot(a.astype(BF16), vd[:, cs])
        o = _rms(o, subln_ref[...]) * (1.0 - lam_init)
        o_ref[:, cs] = o.astype(BF16)

    nblk = seq // WINDOW
    dense = past if latent else seq
    if latent:
        per = TQ // WINDOW
        offsets = tuple(range(-1, per + 1))
        rr = lax.broadcasted_iota(I32, (TQ, WINDOW), 0)
        cc = lax.broadcasted_iota(I32, (TQ, WINDOW), 1)
        band, starts = {}, {}
        for d in offsets:
            blk = qi * per + d
            inside = jnp.logical_and(blk >= 0, blk < nblk)
            band[d] = jnp.logical_and(jnp.abs(rr - cc - d * WINDOW) <= WINDOW, inside)
            starts[d] = pl.multiple_of(past + jnp.clip(blk, 0, nblk - 1) * WINDOW, WINDOW)
    for i in range(SWA_HEADS // 2):
        hk = i // 2
        cs = slice(1536 + i * LANES, 1536 + (i + 1) * LANES)
        qt = qkv_ref[pl.ds(r0, TQ), cs]
        acc = jnp.zeros((TQ, LANES), F32)
        for half in range(2):
            qc = jnp.where(lo, qt, zero_b) if half == 0 else jnp.where(lo, zero_b, qt)
            vsel = vl if half == 0 else vh
            sink = sink_ref[layer, 2 * i + half]
            parts = [_dot_nt(qc, ka[hk, 0:dense, :])]
            if latent:
                for d in offsets:
                    s = _dot_nt(qc, ka[hk, pl.ds(starts[d], WINDOW), :])
                    parts.append(jnp.where(band[d], s, NEG_INF))
            s_all = jnp.concatenate(parts, axis=1) if len(parts) > 1 else parts[0]
            m = jnp.maximum(jnp.max(s_all, axis=-1, keepdims=True), sink)
            e = jnp.exp(s_all - m)
            r = 1.0 / (jnp.sum(e, axis=-1, keepdims=True) + jnp.exp(sink - m))
            eb = e.astype(BF16)
            o = _dot(eb[:, 0:dense], vsel[hk, 0:dense, :])
            if latent:
                for k, d in enumerate(offsets):
                    o += _dot(eb[:, dense + k * WINDOW:dense + (k + 1) * WINDOW],
                              vsel[hk, pl.ds(starts[d], WINDOW), :])
            acc += o * r
        o_ref[:, 512 + i * LANES:512 + (i + 1) * LANES] = acc.astype(BF16)


def _attn_even_call(qkv, caches, j, lam, subln, sink, nbatch, seq, past, lam_init):
    n = past + seq
    latent = past > 0
    in_specs = [pl.BlockSpec((seq, EVEN_IN), lambda b, q: (b, 0))]
    args = [qkv]
    if latent:
        ck, cv, sk, sv = caches
        in_specs += [pl.BlockSpec((None, None, past, DIFF_HEADS, 2 * HD), lambda b, q: (b, j, 0, 0, 0)),
                     pl.BlockSpec((None, None, past, DIFF_HEADS, 2 * HD), lambda b, q: (b, j, 0, 0, 0)),
                     pl.BlockSpec((None, None, past, 2, HD), lambda b, q: (b, j, 0, 0, 0)),
                     pl.BlockSpec((None, None, past, 2, HD), lambda b, q: (b, j, 0, 0, 0))]
        args += [ck, cv, sk, sv]
    in_specs += [pl.BlockSpec((None, 4, HD), lambda b, q: (j, 0, 0)),
                 pl.BlockSpec((None, 1, 2 * HD), lambda b, q: (j, 0, 0)),
                 pl.BlockSpec(memory_space=pltpu.SMEM)]
    args += [lam, subln, sink]
    return pl.pallas_call(
        functools.partial(_attn_even_kernel, seq=seq, past=past, lam_init=lam_init, layer=j),
        grid=(nbatch, seq // TQ),
        in_specs=in_specs,
        out_specs=pl.BlockSpec((TQ, D), lambda b, q: (b * (seq // TQ) + q, 0)),
        out_shape=jax.ShapeDtypeStruct((nbatch * seq, D), BF16),
        scratch_shapes=[pltpu.VMEM((n, 512), BF16), pltpu.VMEM((n, 512), BF16),
                        pltpu.VMEM((2, n, LANES), BF16), pltpu.VMEM((2, n, LANES), BF16),
                        pltpu.VMEM((2, n, LANES), BF16)],
        compiler_params=_cparams(("arbitrary", "arbitrary")),
        name="attn_even_lat" if latent else "attn_even_ctx",
    )(*args)


def _attn_odd_kernel(*refs, seq, past):
    latent = past > 0
    n = past + seq
    if latent:
        q_ref, ckv_ref, kr_ref, cckv_ref, ckr_ref, wk_ref, wv_ref, o_ref, kf, vlo, vhi = refs
    else:
        q_ref, ckv_ref, kr_ref, wk_ref, wv_ref, o_ref, kf, vlo, vhi = refs
    qi = pl.program_id(1)
    lo = lax.broadcasted_iota(I32, (1, LANES), 1) < HD

    @pl.when(qi == 0)
    def _build():
        chunk = 256
        for c0 in range(0, n, chunk):
            rows = slice(c0, c0 + chunk)
            if c0 < past:
                ckv = cckv_ref[c0:c0 + chunk, :].astype(BF16)
                kr = ckr_ref[c0:c0 + chunk, :]
            else:
                ckv = ckv_ref[c0 - past:c0 - past + chunk, :].astype(BF16)
                kr = kr_ref[c0 - past:c0 - past + chunk, :]
            kk = _dot(ckv, wk_ref[...])
            for h in range(MLA_HEADS):
                cs = slice(h * LANES, (h + 1) * LANES)
                kf[rows, cs] = (kk[:, cs] + kr).astype(BF16)
            vv = _dot(ckv, wv_ref[...])
            for i in range(MLA_HEADS // 2):
                cs = slice(i * LANES, (i + 1) * LANES)
                vlo[rows, cs] = jnp.where(lo, vv[:, cs], 0.0).astype(BF16)
                vhi[rows, cs] = jnp.where(lo, 0.0, vv[:, cs]).astype(BF16)

    r0 = pl.multiple_of(qi * TQ, TQ)
    for i in range(MLA_HEADS // 2):
        acc = jnp.zeros((TQ, LANES), F32)
        for half in range(2):
            h = 2 * i + half
            cs = slice(h * LANES, (h + 1) * LANES)
            s = _dot_nt(q_ref[pl.ds(r0, TQ), cs], kf[:, cs])
            m = jnp.max(s, axis=-1, keepdims=True)
            e = jnp.exp(s - m)
            r = 1.0 / jnp.sum(e, axis=-1, keepdims=True)
            vsel = vlo if half == 0 else vhi
            acc += _dot(e.astype(BF16), vsel[:, i * LANES:(i + 1) * LANES]) * r
        o_ref[:, i * LANES:(i + 1) * LANES] = acc.astype(BF16)


def _attn_odd_call(q, ckv, kr, caches, j, wk, wv, nbatch, seq, past):
    n = past + seq
    latent = past > 0
    in_specs = [pl.BlockSpec((seq, MLA_QW), lambda b, qq: (b, 0)),
                pl.BlockSpec((seq, MLA_KV_RANK), lambda b, qq: (b, 0)),
                pl.BlockSpec((seq, LANES), lambda b, qq: (b, 0))]
    args = [q, ckv, kr]
    if latent:
        in_specs += [pl.BlockSpec((None, None, past, MLA_KV_RANK), lambda b, qq: (b, j, 0, 0)),
                     pl.BlockSpec((None, None, past, LANES), lambda b, qq: (b, j, 0, 0))]
        args += list(caches)
    in_specs += [pl.BlockSpec((None, MLA_KV_RANK, MLA_QW), lambda b, qq: (j, 0, 0)),
                 pl.BlockSpec((None, MLA_KV_RANK, D), lambda b, qq: (j, 0, 0))]
    args += [wk, wv]
    return pl.pallas_call(
        functools.partial(_attn_odd_kernel, seq=seq, past=past),
        grid=(nbatch, seq // TQ),
        in_specs=in_specs,
        out_specs=pl.BlockSpec((TQ, D), lambda b, qq: (b * (seq // TQ) + qq, 0)),
        out_shape=jax.ShapeDtypeStruct((nbatch * seq, D), BF16),
        scratch_shapes=[pltpu.VMEM((n, MLA_QW), BF16), pltpu.VMEM((n, D), BF16),
                        pltpu.VMEM((n, D), BF16)],
        compiler_params=_cparams(("arbitrary", "arbitrary")),
        name="attn_odd_lat" if latent else "attn_odd_ctx",
    )(*args)


def _outproj_kernel(xa_ref, xb_ref, oa_ref, ob_ref, w_ref, mod_ref, g_ref, wr_ref,
                    xo_ref, h_ref, aff_ref, *, ntile_a):
    def run(x_ref, o_ref):
        x = x_ref[...] + mod_ref[2] * _dot(o_ref[...], w_ref[...])
        xo_ref[...] = x
        h = _modulate(x, g_ref[...], mod_ref[3], mod_ref[4]).astype(BF16)
        h_ref[...] = h
        logits = _dot(h, wr_ref[...])
        lane = lax.broadcasted_iota(I32, (TM, LANES), 1)
        lg = jnp.where(lane < N_EXPERTS, logits, -jnp.inf)
        e = jnp.exp(lg - jnp.max(lg, axis=-1, keepdims=True))
        aff = e / jnp.sum(e, axis=-1, keepdims=True)
        aff_ref[...] = aff.T[0:N_EXPERTS, :]

    @pl.when(pl.program_id(0) < ntile_a)
    def _():
        run(xa_ref, oa_ref)

    @pl.when(pl.program_id(0) >= ntile_a)
    def _():
        run(xb_ref, ob_ref)


def _outproj_call(xa, xb, xb_off, o_a, o_b, w, widx, mods, layer, gains, w_router, modrow):
    na, nb_ = o_a.shape[0], o_b.shape[0]
    t = na + nb_
    assert TM == TB
    return pl.pallas_call(
        functools.partial(_outproj_kernel, ntile_a=na // TM),
        grid=(t // TM,),
        in_specs=[*_two_part_specs(na, nb_, D, xb_off), *_two_part_specs(na, nb_, D),
                  pl.BlockSpec((None, D, D), lambda r: (widx, 0, 0)),
                  pl.BlockSpec((None, None, 6, 1, D), lambda r: (layer, modrow(r), 0, 0, 0)),
                  pl.BlockSpec((None, 1, D), lambda r: (layer, 0, 0)),
                  pl.BlockSpec((None, D, LANES), lambda r: (layer, 0, 0))],
        out_specs=[pl.BlockSpec((TM, D), lambda r: (r, 0)),
                   pl.BlockSpec((TM, D), lambda r: (r, 0)),
                   pl.BlockSpec((None, N_EXPERTS, TB), lambda r: (r, 0, 0))],
        out_shape=[jax.ShapeDtypeStruct((t, D), F32), jax.ShapeDtypeStruct((t, D), BF16),
                   jax.ShapeDtypeStruct((t // TB, N_EXPERTS, TB), F32)],
        compiler_params=_cparams(("parallel",)),
        name="outproj",
    )(xa, xb, o_a, o_b, w, mods, gains, w_router)


def _select_kernel(aff_ref, rowid_ref, gate_ref, tab_ref, *, nb, cap):
    ne = N_EXPERTS
    nr = nb * ne
    a = aff_ref[...].reshape(nr, TB)
    bits = pltpu.bitcast(a, I32)
    ri = lax.broadcasted_iota(I32, (nr, nr), 0)
    ci = lax.broadcasted_iota(I32, (nr, nr), 1)
    same_e = (ri & (ne - 1)) == (ci & (ne - 1))
    same_b = (ri >> 4) == (ci >> 4)
    m_e = jnp.where(same_e, 1.0, 0.0).astype(BF16)
    m_b = jnp.where(same_b, 1.0, 0.0).astype(BF16)
    m_a = jnp.where(jnp.logical_and(same_e, ci < ri), 1.0, 0.0).astype(BF16)
    m_o = jnp.where(jnp.logical_and(same_b, ci < ri), 1.0, 0.0).astype(BF16)
    ui = lax.broadcasted_iota(I32, (TB, TB), 0)
    uj = lax.broadcasted_iota(I32, (TB, TB), 1)
    upper = jnp.where(ui < uj, 1.0, 0.0).astype(BF16)

    def rows_to_lanes(col):
        return jnp.broadcast_to(col, (nr, LANES)).astype(BF16)

    def count_ge(cand):
        c = jnp.sum(jnp.where(bits >= cand, 1.0, 0.0), axis=-1, keepdims=True)
        return _dot(m_e, rows_to_lanes(c))[:, 0:1]

    def bisect(i, v):
        cand = v | jnp.left_shift(jnp.int32(1), 30 - i)
        return jnp.where(count_ge(cand) >= cap, cand, v)

    thr = lax.fori_loop(0, 31, bisect, jnp.zeros((nr, 1), I32))
    gt = jnp.where(bits > thr, 1.0, 0.0)
    eq = jnp.where(bits == thr, 1.0, 0.0)
    n_gt = _dot(m_e, rows_to_lanes(jnp.sum(gt, axis=-1, keepdims=True)))[:, 0:1]
    need = cap - n_gt
    eq_before = (_dot(m_a, rows_to_lanes(jnp.sum(eq, axis=-1, keepdims=True)))[:, 0:1]
                 + _dot(eq.astype(BF16), upper))
    sel = jnp.where(jnp.logical_and(eq > 0.0, eq_before < need), 1.0, gt)
    local = _dot(sel.astype(BF16), upper)
    cnt = jnp.sum(sel, axis=-1, keepdims=True)
    seg = jnp.floor((cnt + (SEG - 1)) * (1.0 / SEG)) * SEG
    segb = rows_to_lanes(seg)
    over = jnp.maximum(seg - WINR, 0.0)
    overb = rows_to_lanes(over)
    off_over = FIRST_ROWS + _dot(m_o, overb)[:, 0:1]
    off_buf = _dot(m_a, segb)[:, 0:1]
    over_blk = _dot(m_b, overb)[:, 0:1]
    rows_exp = _dot(m_e, segb)[:, 0:1]
    tiles_exp = jnp.floor((rows_exp + (TF - 1)) / TF)
    expert = (lax.broadcasted_iota(I32, (nr, 1), 0) & (ne - 1)).astype(F32)
    row = jnp.where(local < WINR, expert * WINR + local, off_over + local - WINR)
    rowid_ref[...] = jnp.where(sel > 0.0, row, -1.0).astype(I32).reshape(nb, ne, TB)
    gate_ref[...] = jnp.where(sel > 0.0, a, 0.0).reshape(nb, ne, TB)
    tl = lax.broadcasted_iota(I32, (nr, LANES), 1)
    tab = jnp.where(tl == 0, seg, jnp.where(tl == 1, off_over, jnp.where(
        tl == 2, off_buf, jnp.where(tl == 3, over_blk, jnp.where(tl == 4, rows_exp, tiles_exp)))))
    tab_ref[...] = tab.T[0:8, :].astype(I32)


def _select_call(aff, ngroups, ntok):
    nb = ntok // TB
    cap = EC_FACTOR * ntok // N_EXPERTS
    nr = nb * N_EXPERTS
    return pl.pallas_call(
        functools.partial(_select_kernel, nb=nb, cap=cap),
        grid=(ngroups,),
        in_specs=[pl.BlockSpec((nb, N_EXPERTS, TB), lambda g: (g, 0, 0))],
        out_specs=[pl.BlockSpec((None, nb, N_EXPERTS, TB), lambda g: (g, 0, 0, 0)),
                   pl.BlockSpec((None, nb, N_EXPERTS, TB), lambda g: (g, 0, 0, 0)),
                   pl.BlockSpec((None, 8, nr), lambda g: (g, 0, 0))],
        out_shape=[jax.ShapeDtypeStruct((ngroups, nb, N_EXPERTS, TB), I32),
                   jax.ShapeDtypeStruct((ngroups, nb, N_EXPERTS, TB), F32),
                   jax.ShapeDtypeStruct((ngroups, 8, nr), I32)],
        compiler_params=_cparams(("arbitrary",)),
        name="select",
    )(aff)


STACK_ROWS = N_EXPERTS * TB


FIRST_ROWS = 3 * TB


WINR = 3 * SEG
assert N_EXPERTS * WINR == FIRST_ROWS
TF = 672


def _window_rows(rowid_ref, dst, value_ref=None):
    for e in range(N_EXPERTS):
        rid_e = rowid_ref[e:e + 1, :]
        val_e = 1.0 if value_ref is None else value_ref[e:e + 1, :]
        hit = rid_e == lax.broadcasted_iota(I32, (WINR, TB), 0) + e * WINR
        dst[e * WINR:(e + 1) * WINR, :] = jnp.where(hit, val_e, 0.0).astype(dst.dtype)


class _Table:
    def __init__(self, tab_s, row, nb, per_block=False):
        self.tab_s, self.row, self.per = tab_s, row, nb if per_block else nb * N_EXPERTS
        self.scale = N_EXPERTS if per_block else 1

    def __getitem__(self, k):
        return self.tab_s[lax.div(k, self.per), self.row, lax.rem(k, self.per) * self.scale]


def _tables(tab_s, nb):
    return (_Table(tab_s, 0, nb), _Table(tab_s, 1, nb), _Table(tab_s, 2, nb),
            _Table(tab_s, 3, nb, per_block=True))


def _overflow_groups(seg_s, k):
    return jnp.maximum(seg_s[k] - WINR, 0) // SEG


def _overflow_rows(seg_s, offo_s, step, over, rowid_ref, dst, value_ref=None):
    def zero(i, carry):
        r0 = pl.multiple_of(FIRST_ROWS + i * SEG, SEG)
        dst[pl.ds(r0, SEG), :] = jnp.zeros((SEG, TB), dst.dtype)
        return carry

    lax.fori_loop(0, ((over + TB - 1) // TB) * (TB // SEG), zero, 0)
    for e in range(N_EXPERTS):
        k = step * N_EXPERTS + e
        rid_e = rowid_ref[e:e + 1, :]
        val_e = 1.0 if value_ref is None else value_ref[e:e + 1, :]

        def group(i, carry, off=offo_s[k], rid_e=rid_e, val_e=val_e):
            r0 = pl.multiple_of(off + i * SEG, SEG)
            hit = rid_e == lax.broadcasted_iota(I32, (SEG, TB), 0) + r0
            dst[pl.ds(r0, SEG), :] = jnp.where(hit, val_e, 0.0).astype(dst.dtype)
            return carry

        lax.fori_loop(0, _overflow_groups(seg_s, k), group, 0)


def _wait_rows(rows, make_copy):
    def big(i, carry):
        make_copy(TB).wait()
        return carry

    def small(i, carry):
        make_copy(SEG).wait()
        return carry

    lax.fori_loop(0, rows // TB, big, 0)
    lax.fori_loop(0, (rows % TB) // SEG, small, 0)


def _dispatch_kernel(tab_s, h_ref, rowid_ref, xe_hbm,
                     onehot, stack, zbuf, sem, zsem, *, nb, nsteps, cap):
    g = pl.program_id(0)
    b = pl.program_id(1)
    step = g * nb + b
    slot = lax.rem(step, 2)
    seg_s, offo_s, offb_s, over_s = _tables(tab_s, nb)
    over = over_s[step]
    xrows = xe_hbm.shape[2]

    def wait_slot(nrows, sl):
        _wait_rows(nrows, lambda n: pltpu.make_async_copy(
            stack.at[sl, pl.ds(0, n)], xe_hbm.at[0, 0, pl.ds(0, n)], sem.at[sl]))

    @pl.when(step == 0)
    def _init():
        stack[...] = jnp.zeros_like(stack)

    @pl.when(b == 0)
    def _zero_unused():
        zbuf[...] = jnp.zeros_like(zbuf)
        for e in range(N_EXPERTS):
            pltpu.make_async_copy(zbuf, xe_hbm.at[g, e, pl.ds(cap, xrows - cap)], zsem).start()

    _window_rows(rowid_ref, onehot)
    h = h_ref[...]
    stack[slot, 0:FIRST_ROWS, :] = _dot(onehot[0:FIRST_ROWS, :], h).astype(BF16)

    @pl.when(over > 0)
    def _overflow():
        _overflow_rows(seg_s, offo_s, step, over, rowid_ref, onehot)

        def chunk(c, carry):
            base = pl.multiple_of(FIRST_ROWS + c * TB, TB)
            stack[slot, pl.ds(base, TB), :] = _dot(onehot[pl.ds(base, TB), :], h).astype(BF16)
            return carry

        lax.fori_loop(0, (over + TB - 1) // TB, chunk, 0)

    @pl.when(b == 0)
    def _zero_unused_done():
        for e in range(N_EXPERTS):
            pltpu.make_async_copy(zbuf, xe_hbm.at[g, e, pl.ds(cap, xrows - cap)], zsem).wait()

    @pl.when(step >= 1)
    def _previous_landed():
        wait_slot(FIRST_ROWS + over_s[step - 1], 1 - slot)

    for e in range(N_EXPERTS):
        k = step * N_EXPERTS + e
        pltpu.make_async_copy(
            stack.at[slot, e * WINR:(e + 1) * WINR],
            xe_hbm.at[g, e, pl.ds(pl.multiple_of(offb_s[k], SEG), WINR)], sem.at[slot]).start()

    @pl.when(over > 0)
    def _overflow_copies():
        for e in range(N_EXPERTS):
            k = step * N_EXPERTS + e

            def one(i, carry, e=e, k=k):
                pltpu.make_async_copy(
                    stack.at[slot, pl.ds(pl.multiple_of(offo_s[k] + i * SEG, SEG), SEG)],
                    xe_hbm.at[g, e, pl.ds(pl.multiple_of(offb_s[k] + WINR + i * SEG, SEG), SEG)],
                    sem.at[slot]).start()
                return carry

            lax.fori_loop(0, _overflow_groups(seg_s, k), one, 0)

    @pl.when(step == nsteps - 1)
    def _drain():
        wait_slot(FIRST_ROWS + over, slot)


def _expert_rows(ntok):
    cap = EC_FACTOR * ntok // N_EXPERTS
    worst = cap + (ntok // TB) * (SEG - 1)
    tiles = -(-(worst + WINR) // TF)
    assert cap % SEG == 0 and cap >= WINR and tiles * TF > cap
    return cap, tiles


def _dispatch_call(tab, h, rowid, ngroups, ntok):
    nb = ntok // TB
    cap, tiles = _expert_rows(ntok)
    xrows = tiles * TF
    grid_spec = pltpu.PrefetchScalarGridSpec(
        num_scalar_prefetch=1,
        grid=(ngroups, nb),
        in_specs=[pl.BlockSpec((TB, D), lambda g, b, *_: (g * nb + b, 0)),
                  pl.BlockSpec((None, None, N_EXPERTS, TB), lambda g, b, *_: (g, b, 0, 0))],
        out_specs=pl.BlockSpec(memory_space=pl.ANY),
        scratch_shapes=[pltpu.VMEM((STACK_ROWS, TB), BF16), pltpu.VMEM((2, STACK_ROWS, D), BF16),
                        pltpu.VMEM((xrows - cap, D), BF16),
                        pltpu.SemaphoreType.DMA((2,)), pltpu.SemaphoreType.DMA])
    return pl.pallas_call(
        functools.partial(_dispatch_kernel, nb=nb, nsteps=ngroups * nb, cap=cap),
        grid_spec=grid_spec,
        out_shape=jax.ShapeDtypeStruct((ngroups, N_EXPERTS, xrows, D), BF16),
        compiler_params=_cparams(("arbitrary", "arbitrary")),
        name="dispatch",
    )(tab, h, rowid)


def _ffn_kernel(tab_s, xe_ref, wg_hbm, wu_hbm, wd_hbm, y_ref, wg32, wu32, wd32, wgb, wub, wdb, sem,
                *, layer):
    e = pl.program_id(0)
    g = pl.program_id(1)
    j = pl.program_id(2)

    def weight_copies(ee, sl):
        return (pltpu.make_async_copy(wg_hbm.at[layer, ee], wg32.at[sl], sem.at[sl]),
                pltpu.make_async_copy(wu_hbm.at[layer, ee], wu32.at[sl], sem.at[sl]),
                pltpu.make_async_copy(wd_hbm.at[layer, ee], wd32.at[sl], sem.at[sl]))

    @pl.when(jnp.logical_and(g == 0, j == 0))
    def _weights():
        sl = lax.rem(e, 2)

        @pl.when(e == 0)
        def _():
            for cp in weight_copies(e, sl):
                cp.start()

        @pl.when(e + 1 < N_EXPERTS)
        def _():
            for cp in weight_copies(e + 1, 1 - sl):
                cp.start()

        for cp in weight_copies(e, sl):
            cp.wait()
        wgb[...] = wg32[sl].astype(BF16)
        wub[...] = wu32[sl].astype(BF16)
        wdb[...] = wd32[sl].astype(BF16)

    live = j < tab_s[g, 5, e]

    @pl.when(live)
    def _run():
        x = xe_ref[...]
        hid = (_silu(_dot(x, wgb[...])) * _dot(x, wub[...])).astype(BF16)
        y_ref[...] = _dot(hid, wdb[...]).astype(BF16)

    @pl.when(jnp.logical_not(live))
    def _skip():
        y_ref[...] = jnp.zeros_like(y_ref)


def _ffn_call(tab, xe, wg, wu, wd, layer, ngroups, ntok):
    _, tiles = _expert_rows(ntok)

    def xmap(e, g, j, tab_s):
        return (g, e, jnp.minimum(j, tab_s[g, 5, e] - 1), 0)

    grid_spec = pltpu.PrefetchScalarGridSpec(
        num_scalar_prefetch=1,
        grid=(N_EXPERTS, ngroups, tiles),
        in_specs=[pl.BlockSpec((None, None, TF, D), xmap),
                  pl.BlockSpec(memory_space=pl.ANY), pl.BlockSpec(memory_space=pl.ANY),
                  pl.BlockSpec(memory_space=pl.ANY)],
        out_specs=pl.BlockSpec((None, None, TF, D), lambda e, g, j, nt: (g, e, j, 0)),
        scratch_shapes=[pltpu.VMEM((2, D, EXPERT_FF), F32), pltpu.VMEM((2, D, EXPERT_FF), F32),
                        pltpu.VMEM((2, EXPERT_FF, D), F32),
                        pltpu.VMEM((D, EXPERT_FF), BF16), pltpu.VMEM((D, EXPERT_FF), BF16),
                        pltpu.VMEM((EXPERT_FF, D), BF16), pltpu.SemaphoreType.DMA((2,))])
    return pl.pallas_call(
        functools.partial(_ffn_kernel, layer=layer),
        grid_spec=grid_spec,
        out_shape=jax.ShapeDtypeStruct((ngroups, N_EXPERTS, tiles * TF, D), BF16),
        compiler_params=_cparams(("arbitrary", "arbitrary", "arbitrary")),
        name="ffn",
    )(tab, xe, wg, wu, wd)


def _combine_kernel(tab_s, y_hbm, rowid_ref, gate_ref, x_ref, mod_ref,
                    modn_ref, g_ref, xo_ref, h_ref, weights, stack, acc, sem, *, nb, nsteps, final):
    g = pl.program_id(0)
    b = pl.program_id(1)
    step = g * nb + b
    slot = lax.rem(step, 2)
    seg_s, offo_s, offb_s, over_s = _tables(tab_s, nb)
    over = over_s[step]

    def fetch(st, sl):
        gg = st // nb
        for e in range(N_EXPERTS):
            k = st * N_EXPERTS + e
            pltpu.make_async_copy(
                y_hbm.at[gg, e, pl.ds(pl.multiple_of(offb_s[k], SEG), WINR)],
                stack.at[sl, e * WINR:(e + 1) * WINR], sem.at[sl]).start()

        @pl.when(over_s[st] > 0)
        def _():
            for e in range(N_EXPERTS):
                k = st * N_EXPERTS + e

                def one(i, carry, e=e, k=k):
                    pltpu.make_async_copy(
                        y_hbm.at[gg, e, pl.ds(pl.multiple_of(offb_s[k] + WINR + i * SEG, SEG), SEG)],
                        stack.at[sl, pl.ds(pl.multiple_of(offo_s[k] + i * SEG, SEG), SEG)],
                        sem.at[sl]).start()
                    return carry

                lax.fori_loop(0, _overflow_groups(seg_s, k), one, 0)

    @pl.when(step == 0)
    def _first():
        stack[...] = jnp.zeros_like(stack)
        fetch(step, slot)

    if nsteps > 1:
        @pl.when(step + 1 < nsteps)
        def _prefetch():
            fetch(step + 1, 1 - slot)

    _window_rows(rowid_ref, weights, gate_ref)

    @pl.when(over > 0)
    def _():
        _overflow_rows(seg_s, offo_s, step, over, rowid_ref, weights, gate_ref)

    _wait_rows(FIRST_ROWS + over, lambda n: pltpu.make_async_copy(
        y_hbm.at[0, 0, pl.ds(0, n)], stack.at[slot, pl.ds(0, n)], sem.at[slot]))

    def token_weights(base):
        return weights[pl.ds(base, TB), :].T.astype(BF16)

    w = jnp.concatenate([token_weights(c * TB) for c in range(FIRST_ROWS // TB)], axis=1)
    acc[...] = _dot(w, stack[slot, 0:FIRST_ROWS, :])

    @pl.when(over > 0)
    def _():
        def chunk(c, carry):
            base = pl.multiple_of(FIRST_ROWS + c * TB, TB)
            acc[...] += _dot(token_weights(base), stack[slot, pl.ds(base, TB), :])
            return carry

        lax.fori_loop(0, (over + TB - 1) // TB, chunk, 0)

    x = x_ref[...] + mod_ref[5] * acc[...]
    if final:
        y = _rms(x, g_ref[...])

        @pl.when(g == 0)
        def _():
            xo_ref[...] = y

        @pl.when(g != 0)
        def _():
            h_ref[...] = y
    else:
        xo_ref[...] = x
        h_ref[...] = _modulate(x, g_ref[...], modn_ref[0], modn_ref[1]).astype(BF16)


def _combine_call(tab, y, rowid, gate, x, mods, layer, nxt, gains, gidx, modrow, ngroups, ntok, final):
    nb = ntok // TB
    t = x.shape[0]
    if final:
        assert ngroups == 2
        out_specs = [pl.BlockSpec((TB, D), lambda g, b, *_: (jnp.where(g == 0, b, nb - 1), 0)),
                     pl.BlockSpec((TB, D), lambda g, b, *_: (jnp.where(g == 0, 0, b), 0))]
        out_shape = [jax.ShapeDtypeStruct((ntok, D), F32), jax.ShapeDtypeStruct((ntok, D), F32)]
    else:
        out_specs = [pl.BlockSpec((TB, D), lambda g, b, *_: (g * nb + b, 0)),
                     pl.BlockSpec((TB, D), lambda g, b, *_: (g * nb + b, 0))]
        out_shape = [jax.ShapeDtypeStruct((t, D), F32), jax.ShapeDtypeStruct((t, D), BF16)]
    grid_spec = pltpu.PrefetchScalarGridSpec(
        num_scalar_prefetch=1,
        grid=(ngroups, nb),
        in_specs=[pl.BlockSpec(memory_space=pl.ANY),
                  pl.BlockSpec((None, None, N_EXPERTS, TB), lambda g, b, *_: (g, b, 0, 0)),
                  pl.BlockSpec((None, None, N_EXPERTS, TB), lambda g, b, *_: (g, b, 0, 0)),
                  pl.BlockSpec((TB, D), lambda g, b, *_: (g * nb + b, 0)),
                  pl.BlockSpec((None, None, 6, 1, D),
                               lambda g, b, *_: (layer, modrow(g * nb + b), 0, 0, 0)),
                  pl.BlockSpec((None, None, 6, 1, D),
                               lambda g, b, *_: (nxt, modrow(g * nb + b), 0, 0, 0)),
                  pl.BlockSpec((None, 1, D), lambda g, b, *_: (gidx, 0, 0))],
        out_specs=out_specs,
        scratch_shapes=[pltpu.VMEM((STACK_ROWS, TB), F32), pltpu.VMEM((2, STACK_ROWS, D), BF16),
                        pltpu.VMEM((TB, D), F32), pltpu.SemaphoreType.DMA((2,))])
    return pl.pallas_call(
        functools.partial(_combine_kernel, nb=nb, nsteps=ngroups * nb, final=final),
        grid_spec=grid_spec,
        out_shape=out_shape,
        compiler_params=_cparams(("arbitrary", "arbitrary")),
        name="combine",
    )(tab, y, rowid, gate, x, mods, mods, gains)


def kernel(x_prompt, x_sample, cache_diff_k, cache_diff_v, cache_swa_k, cache_swa_v, cache_mla_ckv, cache_mla_krope, c, c_ctx, w_ada, b_ada, norm_mix, norm_ffn, w_in_even, w_out_even, diff_lambda, diff_subln, swa_sink, w_in_odd, mla_q_norm, w_q_up, mla_kv_norm, w_kv_up, w_out_odd, w_router, w_gate_exp, w_up_exp, w_down_exp, final_norm):
    batch, seq, _ = x_prompt.shape
    dec_batch, dec_seq, _ = x_sample.shape
    past = cache_diff_k.shape[2]
    depth = w_ada.shape[0]
    n_even = w_in_even.shape[0]
    n_odd = w_in_odd.shape[0]
    nc, ns = batch * seq, dec_batch * dec_seq
    assert nc == ns, "the routed-expert kernels take two token groups of equal size"
    assert seq % TM == 0 and dec_seq % TM == 0 and past % 256 == 0 and dec_seq % GRID_W == 0
    ntok = nc

    def modrow(r):
        tok = r * TM
        return jnp.where(tok < nc, 0, 1 + jnp.maximum(tok - nc, 0) // dec_seq)

    rc = -(-(1 + dec_batch) // 16) * 16
    cvec = jnp.zeros((rc, D), F32).at[0].set(c_ctx).at[1:1 + dec_batch].set(c)
    mods = _ada_call(cvec, w_ada, b_ada).reshape(depth, rc, 6, 1, D)

    w_even_b = w_in_even.astype(BF16)
    w_oute_b = w_out_even.astype(BF16)
    w_outo_b = w_out_odd.astype(BF16)
    kr_pad = jnp.zeros((n_odd, D, LANES), F32).at[:, :, 64:96].set(w_in_odd[:, :, 640:672])
    w_odd_b = jnp.concatenate([w_in_odd[:, :, :640], kr_pad], axis=-1).astype(BF16)
    wq = w_q_up.reshape(n_odd, MLA_Q_RANK, MLA_HEADS, HD + MLA_ROPE)
    wq_b = jnp.pad(wq, ((0, 0), (0, 0), (0, 0), (0, LANES - HD - MLA_ROPE))).reshape(
        n_odd, MLA_Q_RANK, MLA_QW).astype(BF16)
    wkv = w_kv_up.reshape(n_odd, MLA_KV_RANK, MLA_HEADS, 2 * HD)
    wk_b = jnp.pad(wkv[..., :HD], ((0, 0), (0, 0), (0, 0), (0, LANES - HD))).reshape(
        n_odd, MLA_KV_RANK, MLA_QW).astype(BF16)
    wv_b = wkv[..., HD:].reshape(n_odd, MLA_KV_RANK, D).astype(BF16)
    w_router_b = jnp.pad(w_router, ((0, 0), (0, 0), (0, LANES - N_EXPERTS))).astype(BF16)
    even_tabs = _rope_tables(dec_seq, 16, _even_lane)
    mla_tabs = _rope_tables(dec_seq, 8, _mla_lane)
    ckr = jnp.zeros((dec_batch, n_odd, past, LANES), F32).at[..., 64:96].set(cache_mla_krope)
    even_caches = (cache_diff_k, cache_diff_v, cache_swa_k, cache_swa_v)

    mix_gains = jnp.concatenate([norm_mix, final_norm[None]], axis=0).reshape(depth + 1, 1, D)
    ffn_gains = norm_ffn.reshape(depth, 1, D)
    q_gains = mla_q_norm.reshape(n_odd, 1, MLA_Q_RANK)
    kv_gains = mla_kv_norm.reshape(n_odd, 1, MLA_KV_RANK)
    sublns = diff_subln.reshape(n_even, 1, 2 * HD)

    xa, xb, xb_off = x_prompt.reshape(nc, D), x_sample.reshape(ns, D), 0
    h = _norm_mod_call(xa, xb, mods, 0, mix_gains, modrow)
    kd_new, vd_new, ks_new, vs_new, ckv_odd, kr_odd = [], [], [], [], [], []
    y_prompt = y_sample = None
    for i in range(depth):
        j = i // 2
        if i % 2 == 0:
            qkv_c, kd, vd, ks, vs = _proj_even_call(h, w_even_b, j, 0, nc, None, dec_seq)
            (qkv_l,) = _proj_even_call(h, w_even_b, j, nc, ns, even_tabs, dec_seq)
            kd_new.append(kd.reshape(batch, seq, DIFF_HEADS, 2 * HD))
            vd_new.append(vd.reshape(batch, seq, DIFF_HEADS, 2 * HD))
            ks_new.append(ks.reshape(batch, seq, 2, HD))
            vs_new.append(vs.reshape(batch, seq, 2, HD))
            li = _lambda_init(i)
            o_c = _attn_even_call(qkv_c, None, j, diff_lambda, sublns, swa_sink, batch, seq, 0, li)
            o_l = _attn_even_call(qkv_l, even_caches, j, diff_lambda, sublns, swa_sink,
                                  dec_batch, dec_seq, past, li)
            w_out = w_oute_b
        else:
            q_c, ckv_c, kr_c = _proj_odd_call(h, w_odd_b, q_gains, wq_b, kv_gains, j, 0, nc, None,
                                              dec_seq)
            q_l, ckv_l, kr_l = _proj_odd_call(h, w_odd_b, q_gains, wq_b, kv_gains, j, nc, ns,
                                              mla_tabs, dec_seq)
            ckv_odd.append(ckv_c.reshape(batch, seq, MLA_KV_RANK))
            kr_odd.append(kr_c.reshape(batch, seq, LANES)[..., 64:96])
            o_c = _attn_odd_call(q_c, ckv_c, kr_c, None, j, wk_b, wv_b, batch, seq, 0)
            o_l = _attn_odd_call(q_l, ckv_l, kr_l, (cache_mla_ckv, ckr), j, wk_b, wv_b,
                                 dec_batch, dec_seq, past)
            w_out = w_outo_b
        x, h2, aff = _outproj_call(xa, xb, xb_off, o_c, o_l, w_out, j, mods, i, ffn_gains,
                                   w_router_b, modrow)
        rowid, gate, tab = _select_call(aff, 2, ntok)
        xe = _dispatch_call(tab, h2, rowid, 2, ntok)
        y = _ffn_call(tab, xe, w_gate_exp, w_up_exp, w_down_exp, i, 2, ntok)
        final = i == depth - 1
        nxt = i if final else i + 1
        out_a, out_b = _combine_call(tab, y, rowid, gate, x, mods, i, nxt, mix_gains,
                                     depth if final else nxt, modrow, 2, ntok, final)
        if final:
            y_prompt = out_a.reshape(batch, seq, D)
            y_sample = out_b.reshape(dec_batch, dec_seq, D)
        else:
            x, h = out_a, out_b
            xa, xb, xb_off = x, x, nc // TM

    return (y_prompt, y_sample, jnp.stack(kd_new, axis=1), jnp.stack(vd_new, axis=1),
            jnp.stack(ks_new, axis=1), jnp.stack(vs_new, axis=1), jnp.stack(ckv_odd, axis=1),
            jnp.stack(kr_odd, axis=1))
```

```python
import functools
import math

import jax
import jax.numpy as jnp
from jax import lax
from jax.experimental import pallas as pl
from jax.experimental.pallas import tpu as pltpu

F32 = jnp.float32
BF16 = jnp.bfloat16
I32 = jnp.int32

D = 1024
HD = 64
GRID_W = 64
WINDOW = 128
DIFF_HEADS = 4
SWA_HEADS = 8
MLA_HEADS = 16
MLA_Q_RANK = 384
MLA_KV_RANK = 256
MLA_ROPE = 32
N_EXPERTS = 16
EXPERT_FF = 512
EC_FACTOR = 2
ROPE_BASE = 10000.0
EPS = 1e-6
NEG_INF = -1e30
LOG2E = math.log2(math.e)
EVEN_IN = 2304
LANES = 128
TM = 256
TQ = 256
TB = 256
SEG = 16
VMEM_LIMIT = 56 * 1024 * 1024


def _cparams(sem, vmem=VMEM_LIMIT):
    return pltpu.CompilerParams(dimension_semantics=sem, vmem_limit_bytes=vmem)


def _dot(a, b):
    return jnp.dot(a, b, preferred_element_type=F32)


def _dot_nt(a, b):
    return lax.dot_general(a, b, (((1,), (1,)), ((), ())), preferred_element_type=F32)


def _silu(x):
    return x / (1.0 + jnp.exp(-x))


def _rms(x, g):
    ms = jnp.mean(x * x, axis=-1, keepdims=True)
    return x * lax.rsqrt(ms + EPS) * g


def _modulate(x, g, shift, scale):
    return _rms(x, g) * (1.0 + scale) + shift


def _lambda_init(layer):
    return 0.8 - 0.6 * math.exp(-0.3 * layer)


def _ada_kernel(c_ref, w_ref, b_ref, o_ref):
    s = _silu(c_ref[...]).astype(BF16)
    o_ref[...] = _dot(s, w_ref[...].astype(BF16)) + b_ref[...]


def _ada_call(cvec, w_ada, b_ada):
    depth, _, n6 = w_ada.shape
    rc = cvec.shape[0]
    tn = 512
    return pl.pallas_call(
        _ada_kernel,
        grid=(depth, n6 // tn),
        in_specs=[pl.BlockSpec((rc, D), lambda i, n: (0, 0)),
                  pl.BlockSpec((None, D, tn), lambda i, n: (i, 0, n)),
                  pl.BlockSpec((None, 1, tn), lambda i, n: (i, 0, n))],
        out_specs=pl.BlockSpec((None, rc, tn), lambda i, n: (i, 0, n)),
        out_shape=jax.ShapeDtypeStruct((depth, rc, n6), F32),
        compiler_params=_cparams(("parallel", "parallel")),
        name="ada",
    )(cvec, w_ada, b_ada.reshape(depth, 1, n6))


def _two_part_specs(rows_a, rows_b, width, off_b=0):
    na, nb_ = rows_a // TM, rows_b // TM
    return (pl.BlockSpec((TM, width), lambda r: (jnp.minimum(r, na - 1), 0)),
            pl.BlockSpec((TM, width), lambda r: (off_b + jnp.clip(r - na, 0, nb_ - 1), 0)))


def _norm_mod_kernel(xa_ref, xb_ref, mod_ref, g_ref, h_ref, *, ntile_a):
    def run(x_ref):
        h_ref[...] = _modulate(x_ref[...], g_ref[...], mod_ref[0], mod_ref[1]).astype(BF16)

    @pl.when(pl.program_id(0) < ntile_a)
    def _():
        run(xa_ref)

    @pl.when(pl.program_id(0) >= ntile_a)
    def _():
        run(xb_ref)


def _norm_mod_call(xa, xb, mods, layer, gains, modrow):
    na, nb_ = xa.shape[0], xb.shape[0]
    return pl.pallas_call(
        functools.partial(_norm_mod_kernel, ntile_a=na // TM),
        grid=((na + nb_) // TM,),
        in_specs=[*_two_part_specs(na, nb_, D),
                  pl.BlockSpec((None, None, 6, 1, D), lambda r: (layer, modrow(r), 0, 0, 0)),
                  pl.BlockSpec((None, 1, D), lambda r: (layer, 0, 0))],
        out_specs=pl.BlockSpec((TM, D), lambda r: (r, 0)),
        out_shape=jax.ShapeDtypeStruct((na + nb_, D), BF16),
        compiler_params=_cparams(("parallel",)),
        name="norm_mod",
    )(xa, xb, mods, gains)


def _rope_tables(dec_seq, half, lane_of_dim):
    pos = jnp.arange(dec_seq)
    row = (pos // GRID_W).astype(F32)
    col = (pos % GRID_W).astype(F32)
    inv = ROPE_BASE ** (-(jnp.arange(half, dtype=F32) / half))
    ang = jnp.stack([row[:, None] * inv[None, :], col[:, None] * inv[None, :]])
    cos, sin = jnp.cos(ang), jnp.sin(ang)
    c_cols, s1_cols, s2_cols = [], [], []
    one, zero = jnp.ones((dec_seq,), F32), jnp.zeros((dec_seq,), F32)
    for lane in range(LANES):
        info = lane_of_dim(lane)
        if info is None:
            c_cols.append(one); s1_cols.append(zero); s2_cols.append(zero)
            continue
        axis, k, second = info
        c_cols.append(cos[axis, :, k])
        if second:
            s1_cols.append(zero); s2_cols.append(sin[axis, :, k])
        else:
            s1_cols.append(-sin[axis, :, k]); s2_cols.append(zero)
    return (jnp.stack(c_cols, axis=1), jnp.stack(s1_cols, axis=1), jnp.stack(s2_cols, axis=1))


def _even_lane(lane):
    j = lane % HD
    axis, jj = j // 32, j % 32
    return axis, jj % 16, jj >= 16


def _mla_lane(lane):
    if lane < 64 or lane >= 96:
        return None
    jj = lane - 64
    axis, k = jj // 16, jj % 16
    return axis, k % 8, k >= 8


def _rope(x, c, s1, s2, shift):
    return x * c + pltpu.roll(x, LANES - shift, 1) * s1 + pltpu.roll(x, shift, 1) * s2


_EVEN_ROPE_TILES = tuple(range(0, 8)) + tuple(range(12, 17))
_EVEN_Q_TILES = tuple(range(0, 4)) + tuple(range(12, 16))


def _proj_even_kernel(*refs, rope, caches):
    h_ref, w_ref = refs[0], refs[1]
    pos = 2
    if rope:
        c_ref, s1_ref, s2_ref = refs[2:5]
        pos = 5
    qkv_ref = refs[pos]
    res = _dot(h_ref[...], w_ref[...])
    scale = HD ** -0.5 * LOG2E
    for t in range(EVEN_IN // LANES):
        x = res[:, t * LANES:(t + 1) * LANES]
        if rope and t in _EVEN_ROPE_TILES:
            x = _rope(x, c_ref[...], s1_ref[...], s2_ref[...], 16)
        if t in _EVEN_Q_TILES:
            x = x * scale
        qkv_ref[:, t * LANES:(t + 1) * LANES] = x.astype(BF16)
    if caches:
        kd_ref, vd_ref, ks_ref, vs_ref = refs[pos + 1:pos + 5]
        for h in range(DIFF_HEADS):
            kd_ref[:, h, :] = res[:, 512 + h * LANES:512 + (h + 1) * LANES]
            vd_ref[:, h, :] = res[:, 1024 + h * LANES:1024 + (h + 1) * LANES]
        for hk in range(2):
            ks_ref[:, hk, :] = res[:, 2048 + hk * HD:2048 + (hk + 1) * HD]
            vs_ref[:, hk, :] = res[:, 2176 + hk * HD:2176 + (hk + 1) * HD]


def _proj_even_call(h, w, j, row0, nrows, tables, dec_seq):
    rope = tables is not None
    caches = not rope
    t0 = row0 // TM
    in_specs = [pl.BlockSpec((TM, D), lambda r: (t0 + r, 0)),
                pl.BlockSpec((None, D, EVEN_IN), lambda r: (j, 0, 0))]
    args = [h, w]
    if rope:
        per = dec_seq // TM
        for _ in range(3):
            in_specs.append(pl.BlockSpec((TM, LANES), lambda r: (r % per, 0)))
        args += list(tables)
    out_specs = [pl.BlockSpec((TM, EVEN_IN), lambda r: (r, 0))]
    out_shape = [jax.ShapeDtypeStruct((nrows, EVEN_IN), BF16)]
    if caches:
        for heads, dim in ((DIFF_HEADS, 2 * HD), (DIFF_HEADS, 2 * HD), (2, HD), (2, HD)):
            out_specs.append(pl.BlockSpec((TM, heads, dim), lambda r: (r, 0, 0)))
            out_shape.append(jax.ShapeDtypeStruct((nrows, heads, dim), F32))
    return pl.pallas_call(
        functools.partial(_proj_even_kernel, rope=rope, caches=caches),
        grid=(nrows // TM,),
        in_specs=in_specs, out_specs=out_specs, out_shape=out_shape,
        compiler_params=_cparams(("parallel",)),
        name="proj_even_lat" if rope else "proj_even_ctx",
    )(*args)


ODD_IN_PAD = MLA_Q_RANK + MLA_KV_RANK + LANES
MLA_QW = MLA_HEADS * LANES


def _proj_odd_kernel(*refs, rope):
    h_ref, w_ref, qn_ref, wq_ref, kvn_ref = refs[:5]
    pos = 5
    if rope:
        c_ref, s1_ref, s2_ref = refs[5:8]
        pos = 8
    q_ref, ckv_ref, kr_ref = refs[pos:pos + 3]
    res = _dot(h_ref[...], w_ref[...])
    cq = _rms(res[:, :MLA_Q_RANK], qn_ref[...]).astype(BF16)
    ckv_ref[...] = _rms(res[:, MLA_Q_RANK:MLA_Q_RANK + MLA_KV_RANK], kvn_ref[...])
    kr = res[:, MLA_Q_RANK + MLA_KV_RANK:]
    if rope:
        kr = _rope(kr, c_ref[...], s1_ref[...], s2_ref[...], 8)
    kr_ref[...] = kr
    q = _dot(cq, wq_ref[...])
    scale = (HD + MLA_ROPE) ** -0.5 * LOG2E
    for t in range(MLA_HEADS):
        x = q[:, t * LANES:(t + 1) * LANES]
        if rope:
            x = _rope(x, c_ref[...], s1_ref[...], s2_ref[...], 8)
        q_ref[:, t * LANES:(t + 1) * LANES] = (x * scale).astype(BF16)


def _proj_odd_call(h, w_in, qn, wq, kvn, j, row0, nrows, tables, dec_seq):
    rope = tables is not None
    t0 = row0 // TM
    in_specs = [pl.BlockSpec((TM, D), lambda r: (t0 + r, 0)),
                pl.BlockSpec((None, D, ODD_IN_PAD), lambda r: (j, 0, 0)),
                pl.BlockSpec((None, 1, MLA_Q_RANK), lambda r: (j, 0, 0)),
                pl.BlockSpec((None, MLA_Q_RANK, MLA_QW), lambda r: (j, 0, 0)),
                pl.BlockSpec((None, 1, MLA_KV_RANK), lambda r: (j, 0, 0))]
    args = [h, w_in, qn, wq, kvn]
    if rope:
        per = dec_seq // TM
        for _ in range(3):
            in_specs.append(pl.BlockSpec((TM, LANES), lambda r: (r % per, 0)))
        args += list(tables)
    return pl.pallas_call(
        functools.partial(_proj_odd_kernel, rope=rope),
        grid=(nrows // TM,),
        in_specs=in_specs,
        out_specs=[pl.BlockSpec((TM, MLA_QW), lambda r: (r, 0)),
                   pl.BlockSpec((TM, MLA_KV_RANK), lambda r: (r, 0)),
                   pl.BlockSpec((TM, LANES), lambda r: (r, 0))],
        out_shape=[jax.ShapeDtypeStruct((nrows, MLA_QW), BF16),
                   jax.ShapeDtypeStruct((nrows, MLA_KV_RANK), F32),
                   jax.ShapeDtypeStruct((nrows, LANES), F32)],
        compiler_params=_cparams(("parallel",)),
        name="proj_odd_lat" if rope else "proj_odd_ctx",
    )(*args)


def _ones_lane(half):
    return HD if half == 0 else 0


def _half_values(v, half):
    lane = lax.broadcasted_iota(I32, (1, LANES), 1)
    keep = (lane < HD) if half == 0 else (lane >= HD)
    return jnp.where(keep, v, jnp.where(lane == _ones_lane(half), 1.0, 0.0)).astype(BF16)


def _attn_even_kernel(*refs, seq, past, lam_init, layer):
    latent = past > 0
    n = past + seq
    if latent:
        (qkv_ref, ck_ref, cv_ref, sk_ref, sv_ref, lam_ref, subln_ref, sink_ref,
         o_ref, kd, vd, ka, vl, vh) = refs
    else:
        qkv_ref, lam_ref, subln_ref, sink_ref, o_ref, kd, vd, ka, vl, vh = refs
    qi = pl.program_id(1)
    lo = lax.broadcasted_iota(I32, (1, LANES), 1) < HD

    @pl.when(qi == 0)
    def _build():
        chunk = 256
        for c0 in range(0, n, chunk):
            rows = slice(c0, c0 + chunk)
            if c0 < past:
                prow = slice(c0, c0 + chunk)
                for h in range(DIFF_HEADS):
                    kd[rows, h * LANES:(h + 1) * LANES] = ck_ref[prow, h, :].astype(BF16)
                    vd[rows, h * LANES:(h + 1) * LANES] = cv_ref[prow, h, :].astype(BF16)
                kt = jnp.concatenate([sk_ref[prow, 0, :], sk_ref[prow, 1, :]], axis=1)
                vt = jnp.concatenate([sv_ref[prow, 0, :], sv_ref[prow, 1, :]], axis=1)
            else:
                orow = slice(c0 - past, c0 - past + chunk)
                kd[rows, :] = qkv_ref[orow, 512:1024]
                vd[rows, :] = qkv_ref[orow, 1024:1536]
                kt = qkv_ref[orow, 2048:2176].astype(F32)
                vt = qkv_ref[orow, 2176:2304].astype(F32)
            kr = pltpu.roll(kt, HD, 1)
            vr = pltpu.roll(vt, HD, 1)
            ka[0, rows, :] = jnp.where(lo, kt, kr).astype(BF16)
            ka[1, rows, :] = jnp.where(lo, kr, kt).astype(BF16)
            vl[0, rows, :] = _half_values(vt, 0)
            vh[0, rows, :] = _half_values(vr, 1)
            vl[1, rows, :] = _half_values(vr, 0)
            vh[1, rows, :] = _half_values(vt, 1)

    r0 = pl.multiple_of(qi * TQ, TQ)
    lam = lam_ref[...]
    lam_full = (jnp.exp(jnp.sum(lam[0:1] * lam[1:2], axis=-1, keepdims=True))
                - jnp.exp(jnp.sum(lam[2:3] * lam[3:4], axis=-1, keepdims=True)) + lam_init)
    zero_b = jnp.zeros((), BF16)

    for h in range(DIFF_HEADS):
        cs = slice(h * LANES, (h + 1) * LANES)
        qt = qkv_ref[pl.ds(r0, TQ), cs]
        kh = kd[:, cs]
        es, rs = [], []
        for comp in range(2):
            qc = jnp.where(lo, qt, zero_b) if comp == 0 else jnp.where(lo, zero_b, qt)
            s = _dot_nt(qc, kh)
            m = jnp.max(s, axis=-1, keepdims=True)
            e = jnp.exp2(s - m)
            es.append(e)
            rs.append(1.0 / jnp.sum(e, axis=-1, keepdims=True))
        a = es[0] * rs[0] - es[1] * (lam_full * rs[1])
        o = _dot(a.astype(BF16), vd[:, cs])
        o = _rms(o, subln_ref[...]) * (1.0 - lam_init)
        o_ref[:, cs] = o.astype(BF16)

    nblk = seq // WINDOW
    dense = past if latent else seq
    if latent:
        per = TQ // WINDOW
        offsets = tuple(range(-1, per + 1))
        rr = lax.broadcasted_iota(I32, (TQ, WINDOW), 0)
        cc = lax.broadcasted_iota(I32, (TQ, WINDOW), 1)
        band, starts = {}, {}
        for d in offsets:
            blk = qi * per + d
            inside = jnp.logical_and(blk >= 0, blk < nblk)
            band[d] = jnp.logical_and(jnp.abs(rr - cc - d * WINDOW) <= WINDOW, inside)
            starts[d] = pl.multiple_of(past + jnp.clip(blk, 0, nblk - 1) * WINDOW, WINDOW)
    for i in range(SWA_HEADS // 2):
        hk = i // 2
        cs = slice(1536 + i * LANES, 1536 + (i + 1) * LANES)
        qt = qkv_ref[pl.ds(r0, TQ), cs]
        halves = []
        for half in range(2):
            qc = jnp.where(lo, qt, zero_b) if half == 0 else jnp.where(lo, zero_b, qt)
            vsel = vl if half == 0 else vh
            sink = sink_ref[layer, 2 * i + half] * LOG2E
            parts = [_dot_nt(qc, ka[hk, 0:dense, :])]
            if latent:
                for d in offsets:
                    s = _dot_nt(qc, ka[hk, pl.ds(starts[d], WINDOW), :])
                    parts.append(jnp.where(band[d], s, NEG_INF))
            s_all = jnp.concatenate(parts, axis=1) if len(parts) > 1 else parts[0]
            m = jnp.maximum(jnp.max(s_all, axis=-1, keepdims=True), sink)
            eb = jnp.exp2(s_all - m).astype(BF16)
            o = _dot(eb[:, 0:dense], vsel[hk, 0:dense, :])
            if latent:
                for k, d in enumerate(offsets):
                    o += _dot(eb[:, dense + k * WINDOW:dense + (k + 1) * WINDOW],
                              vsel[hk, pl.ds(starts[d], WINDOW), :])
            one = _ones_lane(half)
            halves.append(o * (1.0 / (o[:, one:one + 1] + jnp.exp2(sink - m))))
        o_ref[:, 512 + i * LANES:512 + (i + 1) * LANES] = jnp.where(lo, halves[0], halves[1]).astype(BF16)


def _attn_even_call(qkv, caches, j, lam, subln, sink, nbatch, seq, past, lam_init):
    n = past + seq
    latent = past > 0
    in_specs = [pl.BlockSpec((seq, EVEN_IN), lambda b, q: (b, 0))]
    args = [qkv]
    if latent:
        ck, cv, sk, sv = caches
        in_specs += [pl.BlockSpec((None, None, past, DIFF_HEADS, 2 * HD), lambda b, q: (b, j, 0, 0, 0)),
                     pl.BlockSpec((None, None, past, DIFF_HEADS, 2 * HD), lambda b, q: (b, j, 0, 0, 0)),
                     pl.BlockSpec((None, None, past, 2, HD), lambda b, q: (b, j, 0, 0, 0)),
                     pl.BlockSpec((None, None, past, 2, HD), lambda b, q: (b, j, 0, 0, 0))]
        args += [ck, cv, sk, sv]
    in_specs += [pl.BlockSpec((None, 4, HD), lambda b, q: (j, 0, 0)),
                 pl.BlockSpec((None, 1, 2 * HD), lambda b, q: (j, 0, 0)),
                 pl.BlockSpec(memory_space=pltpu.SMEM)]
    args += [lam, subln, sink]
    return pl.pallas_call(
        functools.partial(_attn_even_kernel, seq=seq, past=past, lam_init=lam_init, layer=j),
        grid=(nbatch, seq // TQ),
        in_specs=in_specs,
        out_specs=pl.BlockSpec((TQ, D), lambda b, q: (b * (seq // TQ) + q, 0)),
        out_shape=jax.ShapeDtypeStruct((nbatch * seq, D), BF16),
        scratch_shapes=[pltpu.VMEM((n, 512), BF16), pltpu.VMEM((n, 512), BF16),
                        pltpu.VMEM((2, n, LANES), BF16), pltpu.VMEM((2, n, LANES), BF16),
                        pltpu.VMEM((2, n, LANES), BF16)],
        compiler_params=_cparams(("arbitrary", "arbitrary")),
        name="attn_even_lat" if latent else "attn_even_ctx",
    )(*args)


def _attn_odd_kernel(*refs, seq, past):
    latent = past > 0
    n = past + seq
    if latent:
        q_ref, ckv_ref, kr_ref, cckv_ref, ckr_ref, wk_ref, wv_ref, o_ref, kf, vlo, vhi = refs
    else:
        q_ref, ckv_ref, kr_ref, wk_ref, wv_ref, o_ref, kf, vlo, vhi = refs
    qi = pl.program_id(1)
    lo = lax.broadcasted_iota(I32, (1, LANES), 1) < HD

    @pl.when(qi == 0)
    def _build():
        chunk = 256
        for c0 in range(0, n, chunk):
            rows = slice(c0, c0 + chunk)
            if c0 < past:
                ckv = cckv_ref[c0:c0 + chunk, :].astype(BF16)
                kr = ckr_ref[c0:c0 + chunk, :]
            else:
                ckv = ckv_ref[c0 - past:c0 - past + chunk, :].astype(BF16)
                kr = kr_ref[c0 - past:c0 - past + chunk, :]
            kk = _dot(ckv, wk_ref[...])
            for h in range(MLA_HEADS):
                cs = slice(h * LANES, (h + 1) * LANES)
                kf[rows, cs] = (kk[:, cs] + kr).astype(BF16)
            vv = _dot(ckv, wv_ref[...])
            for i in range(MLA_HEADS // 2):
                cs = slice(i * LANES, (i + 1) * LANES)
                vlo[rows, cs] = _half_values(vv[:, cs], 0)
                vhi[rows, cs] = _half_values(vv[:, cs], 1)

    r0 = pl.multiple_of(qi * TQ, TQ)
    lo = lax.broadcasted_iota(I32, (1, LANES), 1) < HD
    for i in range(MLA_HEADS // 2):
        halves = []
        for half in range(2):
            h = 2 * i + half
            cs = slice(h * LANES, (h + 1) * LANES)
            s = _dot_nt(q_ref[pl.ds(r0, TQ), cs], kf[:, cs])
            m = jnp.max(s, axis=-1, keepdims=True)
            vsel = vlo if half == 0 else vhi
            o = _dot(jnp.exp2(s - m).astype(BF16), vsel[:, i * LANES:(i + 1) * LANES])
            one = _ones_lane(half)
            halves.append(o * (1.0 / o[:, one:one + 1]))
        o_ref[:, i * LANES:(i + 1) * LANES] = jnp.where(lo, halves[0], halves[1]).astype(BF16)


def _attn_odd_call(q, ckv, kr, caches, j, wk, wv, nbatch, seq, past):
    n = past + seq
    latent = past > 0
    in_specs = [pl.BlockSpec((seq, MLA_QW), lambda b, qq: (b, 0)),
                pl.BlockSpec((seq, MLA_KV_RANK), lambda b, qq: (b, 0)),
                pl.BlockSpec((seq, LANES), lambda b, qq: (b, 0))]
    args = [q, ckv, kr]
    if latent:
        in_specs += [pl.BlockSpec((None, None, past, MLA_KV_RANK), lambda b, qq: (b, j, 0, 0)),
                     pl.BlockSpec((None, None, past, LANES), lambda b, qq: (b, j, 0, 0))]
        args += list(caches)
    in_specs += [pl.BlockSpec((None, MLA_KV_RANK, MLA_QW), lambda b, qq: (j, 0, 0)),
                 pl.BlockSpec((None, MLA_KV_RANK, D), lambda b, qq: (j, 0, 0))]
    args += [wk, wv]
    return pl.pallas_call(
        functools.partial(_attn_odd_kernel, seq=seq, past=past),
        grid=(nbatch, seq // TQ),
        in_specs=in_specs,
        out_specs=pl.BlockSpec((TQ, D), lambda b, qq: (b * (seq // TQ) + qq, 0)),
        out_shape=jax.ShapeDtypeStruct((nbatch * seq, D), BF16),
        scratch_shapes=[pltpu.VMEM((n, MLA_QW), BF16), pltpu.VMEM((n, D), BF16),
                        pltpu.VMEM((n, D), BF16)],
        compiler_params=_cparams(("arbitrary", "arbitrary")),
        name="attn_odd_lat" if latent else "attn_odd_ctx",
    )(*args)


def _outproj_kernel(xa_ref, xb_ref, oa_ref, ob_ref, w_ref, mod_ref, g_ref, wr_ref,
                    xo_ref, h_ref, aff_ref, *, ntile_a):
    def run(x_ref, o_ref):
        x = x_ref[...] + mod_ref[2] * _dot(o_ref[...], w_ref[...])
        xo_ref[...] = x
        h = _modulate(x, g_ref[...], mod_ref[3], mod_ref[4]).astype(BF16)
        h_ref[...] = h
        logits = _dot(h, wr_ref[...])
        lane = lax.broadcasted_iota(I32, (TM, LANES), 1)
        lg = jnp.where(lane < N_EXPERTS, logits, -jnp.inf)
        e = jnp.exp(lg - jnp.max(lg, axis=-1, keepdims=True))
        aff = e / jnp.sum(e, axis=-1, keepdims=True)
        aff_ref[...] = aff.T[0:N_EXPERTS, :]

    @pl.when(pl.program_id(0) < ntile_a)
    def _():
        run(xa_ref, oa_ref)

    @pl.when(pl.program_id(0) >= ntile_a)
    def _():
        run(xb_ref, ob_ref)


def _outproj_call(xa, xb, xb_off, o_a, o_b, w, widx, mods, layer, gains, w_router, modrow):
    na, nb_ = o_a.shape[0], o_b.shape[0]
    t = na + nb_
    assert TM == TB
    return pl.pallas_call(
        functools.partial(_outproj_kernel, ntile_a=na // TM),
        grid=(t // TM,),
        in_specs=[*_two_part_specs(na, nb_, D, xb_off), *_two_part_specs(na, nb_, D),
                  pl.BlockSpec((None, D, D), lambda r: (widx, 0, 0)),
                  pl.BlockSpec((None, None, 6, 1, D), lambda r: (layer, modrow(r), 0, 0, 0)),
                  pl.BlockSpec((None, 1, D), lambda r: (layer, 0, 0)),
                  pl.BlockSpec((None, D, LANES), lambda r: (layer, 0, 0))],
        out_specs=[pl.BlockSpec((TM, D), lambda r: (r, 0)),
                   pl.BlockSpec((TM, D), lambda r: (r, 0)),
                   pl.BlockSpec((None, N_EXPERTS, TB), lambda r: (r, 0, 0))],
        out_shape=[jax.ShapeDtypeStruct((t, D), F32), jax.ShapeDtypeStruct((t, D), BF16),
                   jax.ShapeDtypeStruct((t // TB, N_EXPERTS, TB), F32)],
        compiler_params=_cparams(("parallel",)),
        name="outproj",
    )(xa, xb, o_a, o_b, w, mods, gains, w_router)


def _select_kernel(aff_ref, rowid_ref, gate_ref, tab_ref, *, nb, cap):
    ne = N_EXPERTS
    nr = nb * ne
    a = aff_ref[...].reshape(nr, TB)
    bits = pltpu.bitcast(a, I32)
    ri = lax.broadcasted_iota(I32, (nr, nr), 0)
    ci = lax.broadcasted_iota(I32, (nr, nr), 1)
    same_e = (ri & (ne - 1)) == (ci & (ne - 1))
    same_b = (ri >> 4) == (ci >> 4)
    m_e = jnp.where(same_e, 1.0, 0.0).astype(BF16)
    m_b = jnp.where(same_b, 1.0, 0.0).astype(BF16)
    m_a = jnp.where(jnp.logical_and(same_e, ci < ri), 1.0, 0.0).astype(BF16)
    m_o = jnp.where(jnp.logical_and(same_b, ci < ri), 1.0, 0.0).astype(BF16)
    ui = lax.broadcasted_iota(I32, (TB, TB), 0)
    uj = lax.broadcasted_iota(I32, (TB, TB), 1)
    upper = jnp.where(ui < uj, 1.0, 0.0).astype(BF16)

    def rows_to_lanes(col):
        return jnp.broadcast_to(col, (nr, LANES)).astype(BF16)

    def count_ge(cand):
        c = jnp.sum(jnp.where(bits >= cand, 1.0, 0.0), axis=-1, keepdims=True)
        return _dot(m_e, rows_to_lanes(c))[:, 0:1]

    def bisect(i, v):
        cand = v | jnp.left_shift(jnp.int32(1), 30 - i)
        return jnp.where(count_ge(cand) >= cap, cand, v)

    thr = lax.fori_loop(0, 31, bisect, jnp.zeros((nr, 1), I32))
    gt = jnp.where(bits > thr, 1.0, 0.0)
    eq = jnp.where(bits == thr, 1.0, 0.0)
    n_gt = _dot(m_e, rows_to_lanes(jnp.sum(gt, axis=-1, keepdims=True)))[:, 0:1]
    need = cap - n_gt
    eq_before = (_dot(m_a, rows_to_lanes(jnp.sum(eq, axis=-1, keepdims=True)))[:, 0:1]
                 + _dot(eq.astype(BF16), upper))
    sel = jnp.where(jnp.logical_and(eq > 0.0, eq_before < need), 1.0, gt)
    local = _dot(sel.astype(BF16), upper)
    cnt = jnp.sum(sel, axis=-1, keepdims=True)
    seg = jnp.floor((cnt + (SEG - 1)) * (1.0 / SEG)) * SEG
    segb = rows_to_lanes(seg)
    over = jnp.maximum(seg - WINR, 0.0)
    overb = rows_to_lanes(over)
    off_over = FIRST_ROWS + _dot(m_o, overb)[:, 0:1]
    off_buf = _dot(m_a, segb)[:, 0:1]
    over_blk = _dot(m_b, overb)[:, 0:1]
    rows_exp = _dot(m_e, segb)[:, 0:1]
    tiles_exp = jnp.floor((rows_exp + (TF - 1)) / TF)
    expert = (lax.broadcasted_iota(I32, (nr, 1), 0) & (ne - 1)).astype(F32)
    row = jnp.where(local < WINR, expert * WINR + local, off_over + local - WINR)
    rowid_ref[...] = jnp.where(sel > 0.0, row, -1.0).astype(I32).reshape(nb, ne, TB)
    gate_ref[...] = jnp.where(sel > 0.0, a, 0.0).reshape(nb, ne, TB)
    tl = lax.broadcasted_iota(I32, (nr, LANES), 1)
    tab = jnp.where(tl == 0, seg, jnp.where(tl == 1, off_over, jnp.where(
        tl == 2, off_buf, jnp.where(tl == 3, over_blk, jnp.where(tl == 4, rows_exp, tiles_exp)))))
    tab_ref[...] = tab.T[0:8, :].astype(I32)


def _select_call(aff, ngroups, ntok):
    nb = ntok // TB
    cap = EC_FACTOR * ntok // N_EXPERTS
    nr = nb * N_EXPERTS
    return pl.pallas_call(
        functools.partial(_select_kernel, nb=nb, cap=cap),
        grid=(ngroups,),
        in_specs=[pl.BlockSpec((nb, N_EXPERTS, TB), lambda g: (g, 0, 0))],
        out_specs=[pl.BlockSpec((None, nb, N_EXPERTS, TB), lambda g: (g, 0, 0, 0)),
                   pl.BlockSpec((None, nb, N_EXPERTS, TB), lambda g: (g, 0, 0, 0)),
                   pl.BlockSpec((None, 8, nr), lambda g: (g, 0, 0))],
        out_shape=[jax.ShapeDtypeStruct((ngroups, nb, N_EXPERTS, TB), I32),
                   jax.ShapeDtypeStruct((ngroups, nb, N_EXPERTS, TB), F32),
                   jax.ShapeDtypeStruct((ngroups, 8, nr), I32)],
        compiler_params=_cparams(("arbitrary",)),
        name="select",
    )(aff)


STACK_ROWS = N_EXPERTS * TB


FIRST_ROWS = 3 * TB


WINR = 3 * SEG
assert N_EXPERTS * WINR == FIRST_ROWS
TF = 672


def _window_rows(rowid_ref, dst, value_ref=None):
    for e in range(N_EXPERTS):
        rid_e = rowid_ref[e:e + 1, :]
        val_e = 1.0 if value_ref is None else value_ref[e:e + 1, :]
        hit = rid_e == lax.broadcasted_iota(I32, (WINR, TB), 0) + e * WINR
        dst[e * WINR:(e + 1) * WINR, :] = jnp.where(hit, val_e, 0.0).astype(dst.dtype)


class _Table:
    def __init__(self, tab_s, row, nb, per_block=False):
        self.tab_s, self.row, self.per = tab_s, row, nb if per_block else nb * N_EXPERTS
        self.scale = N_EXPERTS if per_block else 1

    def __getitem__(self, k):
        if self.per & (self.per - 1) == 0:
            hi, low = lax.shift_right_logical(k, self.per.bit_length() - 1), k & (self.per - 1)
        else:
            hi, low = lax.div(k, self.per), lax.rem(k, self.per)
        return self.tab_s[hi, self.row, low * self.scale]


def _tables(tab_s, nb):
    return (_Table(tab_s, 0, nb), _Table(tab_s, 1, nb), _Table(tab_s, 2, nb),
            _Table(tab_s, 3, nb, per_block=True))


def _overflow_groups(seg_s, k):
    return lax.div(jnp.maximum(seg_s[k] - WINR, 0), SEG)


def _overflow_rows(seg_s, offo_s, step, over, rowid_ref, dst, value_ref=None):
    def zero(i, carry):
        r0 = pl.multiple_of(FIRST_ROWS + i * SEG, SEG)
        dst[pl.ds(r0, SEG), :] = jnp.zeros((SEG, TB), dst.dtype)
        return carry

    lax.fori_loop(0, ((over + TB - 1) // TB) * (TB // SEG), zero, 0)
    for e in range(N_EXPERTS):
        k = step * N_EXPERTS + e
        rid_e = rowid_ref[e:e + 1, :]
        val_e = 1.0 if value_ref is None else value_ref[e:e + 1, :]

        def group(i, carry, off=offo_s[k], rid_e=rid_e, val_e=val_e):
            r0 = pl.multiple_of(off + i * SEG, SEG)
            hit = rid_e == lax.broadcasted_iota(I32, (SEG, TB), 0) + r0
            dst[pl.ds(r0, SEG), :] = jnp.where(hit, val_e, 0.0).astype(dst.dtype)
            return carry

        lax.fori_loop(0, _overflow_groups(seg_s, k), group, 0)


def _wait_rows(rows, make_copy):
    def big(i, carry):
        make_copy(TB).wait()
        return carry

    def small(i, carry):
        make_copy(SEG).wait()
        return carry

    lax.fori_loop(0, lax.div(rows, TB), big, 0)
    lax.fori_loop(0, lax.div(lax.rem(rows, TB), SEG), small, 0)


def _dispatch_kernel(tab_s, h_ref, rowid_ref, xe_hbm,
                     onehot, stack, zbuf, sem, zsem, *, nb, nsteps, cap):
    g = pl.program_id(0)
    b = pl.program_id(1)
    step = g * nb + b
    slot = lax.rem(step, 2)
    seg_s, offo_s, offb_s, over_s = _tables(tab_s, nb)
    over = over_s[step]
    xrows = xe_hbm.shape[2]

    def wait_slot(nrows, sl):
        _wait_rows(nrows, lambda n: pltpu.make_async_copy(
            stack.at[sl, pl.ds(0, n)], xe_hbm.at[0, 0, pl.ds(0, n)], sem.at[sl]))

    @pl.when(step == 0)
    def _init():
        stack[...] = jnp.zeros_like(stack)

    @pl.when(b == 0)
    def _zero_unused():
        zbuf[...] = jnp.zeros_like(zbuf)
        for e in range(N_EXPERTS):
            pltpu.make_async_copy(zbuf, xe_hbm.at[g, e, pl.ds(cap, xrows - cap)], zsem).start()

    _window_rows(rowid_ref, onehot)
    h = h_ref[...]
    stack[slot, 0:FIRST_ROWS, :] = _dot(onehot[0:FIRST_ROWS, :], h).astype(BF16)

    @pl.when(over > 0)
    def _overflow():
        _overflow_rows(seg_s, offo_s, step, over, rowid_ref, onehot)

        def chunk(c, carry):
            base = pl.multiple_of(FIRST_ROWS + c * TB, TB)
            stack[slot, pl.ds(base, TB), :] = _dot(onehot[pl.ds(base, TB), :], h).astype(BF16)
            return carry

        lax.fori_loop(0, (over + TB - 1) // TB, chunk, 0)

    @pl.when(b == 0)
    def _zero_unused_done():
        for e in range(N_EXPERTS):
            pltpu.make_async_copy(zbuf, xe_hbm.at[g, e, pl.ds(cap, xrows - cap)], zsem).wait()

    @pl.when(step >= 1)
    def _previous_landed():
        wait_slot(FIRST_ROWS + over_s[step - 1], 1 - slot)

    for e in range(N_EXPERTS):
        k = step * N_EXPERTS + e
        pltpu.make_async_copy(
            stack.at[slot, e * WINR:(e + 1) * WINR],
            xe_hbm.at[g, e, pl.ds(pl.multiple_of(offb_s[k], SEG), WINR)], sem.at[slot]).start()

    @pl.when(over > 0)
    def _overflow_copies():
        for e in range(N_EXPERTS):
            k = step * N_EXPERTS + e

            def one(i, carry, e=e, k=k):
                pltpu.make_async_copy(
                    stack.at[slot, pl.ds(pl.multiple_of(offo_s[k] + i * SEG, SEG), SEG)],
                    xe_hbm.at[g, e, pl.ds(pl.multiple_of(offb_s[k] + WINR + i * SEG, SEG), SEG)],
                    sem.at[slot]).start()
                return carry

            lax.fori_loop(0, _overflow_groups(seg_s, k), one, 0)

    @pl.when(step == nsteps - 1)
    def _drain():
        wait_slot(FIRST_ROWS + over, slot)


def _expert_rows(ntok):
    cap = EC_FACTOR * ntok // N_EXPERTS
    worst = cap + (ntok // TB) * (SEG - 1)
    tiles = -(-(worst + WINR) // TF)
    assert cap % SEG == 0 and cap >= WINR and tiles * TF > cap
    return cap, tiles


def _dispatch_call(tab, h, rowid, ngroups, ntok):
    nb = ntok // TB
    cap, tiles = _expert_rows(ntok)
    xrows = tiles * TF
    grid_spec = pltpu.PrefetchScalarGridSpec(
        num_scalar_prefetch=1,
        grid=(ngroups, nb),
        in_specs=[pl.BlockSpec((TB, D), lambda g, b, *_: (g * nb + b, 0)),
                  pl.BlockSpec((None, None, N_EXPERTS, TB), lambda g, b, *_: (g, b, 0, 0))],
        out_specs=pl.BlockSpec(memory_space=pl.ANY),
        scratch_shapes=[pltpu.VMEM((STACK_ROWS, TB), BF16), pltpu.VMEM((2, STACK_ROWS, D), BF16),
                        pltpu.VMEM((xrows - cap, D), BF16),
                        pltpu.SemaphoreType.DMA((2,)), pltpu.SemaphoreType.DMA])
    return pl.pallas_call(
        functools.partial(_dispatch_kernel, nb=nb, nsteps=ngroups * nb, cap=cap),
        grid_spec=grid_spec,
        out_shape=jax.ShapeDtypeStruct((ngroups, N_EXPERTS, xrows, D), BF16),
        compiler_params=_cparams(("arbitrary", "arbitrary")),
        name="dispatch",
    )(tab, h, rowid)


def _ffn_kernel(tab_s, xe_ref, wg_hbm, wu_hbm, wd_hbm, y_ref, wg32, wu32, wd32, wgb, wub, wdb, sem,
                *, layer):
    e = pl.program_id(0)
    g = pl.program_id(1)
    j = pl.program_id(2)

    def weight_copies(ee, sl):
        return (pltpu.make_async_copy(wg_hbm.at[layer, ee], wg32.at[sl], sem.at[sl]),
                pltpu.make_async_copy(wu_hbm.at[layer, ee], wu32.at[sl], sem.at[sl]),
                pltpu.make_async_copy(wd_hbm.at[layer, ee], wd32.at[sl], sem.at[sl]))

    @pl.when(jnp.logical_and(g == 0, j == 0))
    def _weights():
        sl = lax.rem(e, 2)

        @pl.when(e == 0)
        def _():
            for cp in weight_copies(e, sl):
                cp.start()

        @pl.when(e + 1 < N_EXPERTS)
        def _():
            for cp in weight_copies(e + 1, 1 - sl):
                cp.start()

        for cp in weight_copies(e, sl):
            cp.wait()
        wgb[...] = wg32[sl].astype(BF16)
        wub[...] = wu32[sl].astype(BF16)
        wdb[...] = wd32[sl].astype(BF16)

    live = j < tab_s[g, 5, e]

    @pl.when(live)
    def _run():
        x = xe_ref[...]
        hid = (_silu(_dot(x, wgb[...])) * _dot(x, wub[...])).astype(BF16)
        y_ref[...] = _dot(hid, wdb[...]).astype(BF16)

    @pl.when(jnp.logical_not(live))
    def _skip():
        y_ref[...] = jnp.zeros_like(y_ref)


def _ffn_call(tab, xe, wg, wu, wd, layer, ngroups, ntok):
    _, tiles = _expert_rows(ntok)

    def xmap(e, g, j, tab_s):
        return (g, e, jnp.minimum(j, tab_s[g, 5, e] - 1), 0)

    grid_spec = pltpu.PrefetchScalarGridSpec(
        num_scalar_prefetch=1,
        grid=(N_EXPERTS, ngroups, tiles),
        in_specs=[pl.BlockSpec((None, None, TF, D), xmap),
                  pl.BlockSpec(memory_space=pl.ANY), pl.BlockSpec(memory_space=pl.ANY),
                  pl.BlockSpec(memory_space=pl.ANY)],
        out_specs=pl.BlockSpec((None, None, TF, D), lambda e, g, j, nt: (g, e, j, 0)),
        scratch_shapes=[pltpu.VMEM((2, D, EXPERT_FF), F32), pltpu.VMEM((2, D, EXPERT_FF), F32),
                        pltpu.VMEM((2, EXPERT_FF, D), F32),
                        pltpu.VMEM((D, EXPERT_FF), BF16), pltpu.VMEM((D, EXPERT_FF), BF16),
                        pltpu.VMEM((EXPERT_FF, D), BF16), pltpu.SemaphoreType.DMA((2,))])
    return pl.pallas_call(
        functools.partial(_ffn_kernel, layer=layer),
        grid_spec=grid_spec,
        out_shape=jax.ShapeDtypeStruct((ngroups, N_EXPERTS, tiles * TF, D), BF16),
        compiler_params=_cparams(("arbitrary", "arbitrary", "arbitrary")),
        name="ffn",
    )(tab, xe, wg, wu, wd)


def _combine_kernel(tab_s, y_hbm, rowid_ref, gate_ref, x_ref, mod_ref,
                    modn_ref, g_ref, xo_ref, h_ref, weights, stack, acc, sem, *, nb, nsteps, final):
    g = pl.program_id(0)
    b = pl.program_id(1)
    step = g * nb + b
    slot = lax.rem(step, 2)
    seg_s, offo_s, offb_s, over_s = _tables(tab_s, nb)
    over = over_s[step]

    def fetch(st, sl):
        gg = lax.div(st, nb)
        for e in range(N_EXPERTS):
            k = st * N_EXPERTS + e
            pltpu.make_async_copy(
                y_hbm.at[gg, e, pl.ds(pl.multiple_of(offb_s[k], SEG), WINR)],
                stack.at[sl, e * WINR:(e + 1) * WINR], sem.at[sl]).start()

        @pl.when(over_s[st] > 0)
        def _():
            for e in range(N_EXPERTS):
                k = st * N_EXPERTS + e

                def one(i, carry, e=e, k=k):
                    pltpu.make_async_copy(
                        y_hbm.at[gg, e, pl.ds(pl.multiple_of(offb_s[k] + WINR + i * SEG, SEG), SEG)],
                        stack.at[sl, pl.ds(pl.multiple_of(offo_s[k] + i * SEG, SEG), SEG)],
                        sem.at[sl]).start()
                    return carry

                lax.fori_loop(0, _overflow_groups(seg_s, k), one, 0)

    @pl.when(step == 0)
    def _first():
        stack[...] = jnp.zeros_like(stack)
        fetch(step, slot)

    if nsteps > 1:
        @pl.when(step + 1 < nsteps)
        def _prefetch():
            fetch(step + 1, 1 - slot)

    _window_rows(rowid_ref, weights, gate_ref)

    @pl.when(over > 0)
    def _():
        _overflow_rows(seg_s, offo_s, step, over, rowid_ref, weights, gate_ref)

    _wait_rows(FIRST_ROWS + over, lambda n: pltpu.make_async_copy(
        y_hbm.at[0, 0, pl.ds(0, n)], stack.at[slot, pl.ds(0, n)], sem.at[slot]))

    def token_weights(base):
        return weights[pl.ds(base, TB), :].T.astype(BF16)

    w = jnp.concatenate([token_weights(c * TB) for c in range(FIRST_ROWS // TB)], axis=1)
    acc[...] = _dot(w, stack[slot, 0:FIRST_ROWS, :])

    @pl.when(over > 0)
    def _():
        def chunk(c, carry):
            base = pl.multiple_of(FIRST_ROWS + c * TB, TB)
            acc[...] += _dot(token_weights(base), stack[slot, pl.ds(base, TB), :])
            return carry

        lax.fori_loop(0, (over + TB - 1) // TB, chunk, 0)

    x = x_ref[...] + mod_ref[5] * acc[...]
    if final:
        y = _rms(x, g_ref[...])

        @pl.when(g == 0)
        def _():
            xo_ref[...] = y

        @pl.when(g != 0)
        def _():
            h_ref[...] = y
    else:
        xo_ref[...] = x
        h_ref[...] = _modulate(x, g_ref[...], modn_ref[0], modn_ref[1]).astype(BF16)


def _combine_call(tab, y, rowid, gate, x, mods, layer, nxt, gains, gidx, modrow, ngroups, ntok, final):
    nb = ntok // TB
    t = x.shape[0]
    if final:
        assert ngroups == 2
        out_specs = [pl.BlockSpec((TB, D), lambda g, b, *_: (jnp.where(g == 0, b, nb - 1), 0)),
                     pl.BlockSpec((TB, D), lambda g, b, *_: (jnp.where(g == 0, 0, b), 0))]
        out_shape = [jax.ShapeDtypeStruct((ntok, D), F32), jax.ShapeDtypeStruct((ntok, D), F32)]
    else:
        out_specs = [pl.BlockSpec((TB, D), lambda g, b, *_: (g * nb + b, 0)),
                     pl.BlockSpec((TB, D), lambda g, b, *_: (g * nb + b, 0))]
        out_shape = [jax.ShapeDtypeStruct((t, D), F32), jax.ShapeDtypeStruct((t, D), BF16)]
    grid_spec = pltpu.PrefetchScalarGridSpec(
        num_scalar_prefetch=1,
        grid=(ngroups, nb),
        in_specs=[pl.BlockSpec(memory_space=pl.ANY),
                  pl.BlockSpec((None, None, N_EXPERTS, TB), lambda g, b, *_: (g, b, 0, 0)),
                  pl.BlockSpec((None, None, N_EXPERTS, TB), lambda g, b, *_: (g, b, 0, 0)),
                  pl.BlockSpec((TB, D), lambda g, b, *_: (g * nb + b, 0)),
                  pl.BlockSpec((None, None, 6, 1, D),
                               lambda g, b, *_: (layer, modrow(g * nb + b), 0, 0, 0)),
                  pl.BlockSpec((None, None, 6, 1, D),
                               lambda g, b, *_: (nxt, modrow(g * nb + b), 0, 0, 0)),
                  pl.BlockSpec((None, 1, D), lambda g, b, *_: (gidx, 0, 0))],
        out_specs=out_specs,
        scratch_shapes=[pltpu.VMEM((STACK_ROWS, TB), F32), pltpu.VMEM((2, STACK_ROWS, D), BF16),
                        pltpu.VMEM((TB, D), F32), pltpu.SemaphoreType.DMA((2,))])
    return pl.pallas_call(
        functools.partial(_combine_kernel, nb=nb, nsteps=ngroups * nb, final=final),
        grid_spec=grid_spec,
        out_shape=out_shape,
        compiler_params=_cparams(("arbitrary", "arbitrary")),
        name="combine",
    )(tab, y, rowid, gate, x, mods, mods, gains)


def kernel(x_prompt, x_sample, cache_diff_k, cache_diff_v, cache_swa_k, cache_swa_v, cache_mla_ckv, cache_mla_krope, c, c_ctx, w_ada, b_ada, norm_mix, norm_ffn, w_in_even, w_out_even, diff_lambda, diff_subln, swa_sink, w_in_odd, mla_q_norm, w_q_up, mla_kv_norm, w_kv_up, w_out_odd, w_router, w_gate_exp, w_up_exp, w_down_exp, final_norm):
    batch, seq, _ = x_prompt.shape
    dec_batch, dec_seq, _ = x_sample.shape
    past = cache_diff_k.shape[2]
    depth = w_ada.shape[0]
    n_even = w_in_even.shape[0]
    n_odd = w_in_odd.shape[0]
    nc, ns = batch * seq, dec_batch * dec_seq
    assert nc == ns, "the routed-expert kernels take two token groups of equal size"
    assert seq % TM == 0 and dec_seq % TM == 0 and past % 256 == 0 and dec_seq % GRID_W == 0
    ntok = nc

    def modrow(r):
        tok = r * TM
        return jnp.where(tok < nc, 0, 1 + jnp.maximum(tok - nc, 0) // dec_seq)

    rc = -(-(1 + dec_batch) // 16) * 16
    cvec = jnp.zeros((rc, D), F32).at[0].set(c_ctx).at[1:1 + dec_batch].set(c)
    mods = _ada_call(cvec, w_ada, b_ada).reshape(depth, rc, 6, 1, D)

    w_even_b = w_in_even.astype(BF16)
    w_oute_b = w_out_even.astype(BF16)
    w_outo_b = w_out_odd.astype(BF16)
    kr_pad = jnp.zeros((n_odd, D, LANES), F32).at[:, :, 64:96].set(w_in_odd[:, :, 640:672])
    w_odd_b = jnp.concatenate([w_in_odd[:, :, :640], kr_pad], axis=-1).astype(BF16)
    wq = w_q_up.reshape(n_odd, MLA_Q_RANK, MLA_HEADS, HD + MLA_ROPE)
    wq_b = jnp.pad(wq, ((0, 0), (0, 0), (0, 0), (0, LANES - HD - MLA_ROPE))).reshape(
        n_odd, MLA_Q_RANK, MLA_QW).astype(BF16)
    wkv = w_kv_up.reshape(n_odd, MLA_KV_RANK, MLA_HEADS, 2 * HD)
    wk_b = jnp.pad(wkv[..., :HD], ((0, 0), (0, 0), (0, 0), (0, LANES - HD))).reshape(
        n_odd, MLA_KV_RANK, MLA_QW).astype(BF16)
    wv_b = wkv[..., HD:].reshape(n_odd, MLA_KV_RANK, D).astype(BF16)
    w_router_b = jnp.pad(w_router, ((0, 0), (0, 0), (0, LANES - N_EXPERTS))).astype(BF16)
    even_tabs = _rope_tables(dec_seq, 16, _even_lane)
    mla_tabs = _rope_tables(dec_seq, 8, _mla_lane)
    ckr = jnp.zeros((dec_batch, n_odd, past, LANES), F32).at[..., 64:96].set(cache_mla_krope)
    even_caches = (cache_diff_k, cache_diff_v, cache_swa_k, cache_swa_v)

    mix_gains = jnp.concatenate([norm_mix, final_norm[None]], axis=0).reshape(depth + 1, 1, D)
    ffn_gains = norm_ffn.reshape(depth, 1, D)
    q_gains = mla_q_norm.reshape(n_odd, 1, MLA_Q_RANK)
    kv_gains = mla_kv_norm.reshape(n_odd, 1, MLA_KV_RANK)
    sublns = diff_subln.reshape(n_even, 1, 2 * HD)

    xa, xb, xb_off = x_prompt.reshape(nc, D), x_sample.reshape(ns, D), 0
    h = _norm_mod_call(xa, xb, mods, 0, mix_gains, modrow)
    kd_new, vd_new, ks_new, vs_new, ckv_odd, kr_odd = [], [], [], [], [], []
    y_prompt = y_sample = None
    for i in range(depth):
        j = i // 2
        if i % 2 == 0:
            qkv_c, kd, vd, ks, vs = _proj_even_call(h, w_even_b, j, 0, nc, None, dec_seq)
            (qkv_l,) = _proj_even_call(h, w_even_b, j, nc, ns, even_tabs, dec_seq)
            kd_new.append(kd.reshape(batch, seq, DIFF_HEADS, 2 * HD))
            vd_new.append(vd.reshape(batch, seq, DIFF_HEADS, 2 * HD))
            ks_new.append(ks.reshape(batch, seq, 2, HD))
            vs_new.append(vs.reshape(batch, seq, 2, HD))
            li = _lambda_init(i)
            o_c = _attn_even_call(qkv_c, None, j, diff_lambda, sublns, swa_sink, batch, seq, 0, li)
            o_l = _attn_even_call(qkv_l, even_caches, j, diff_lambda, sublns, swa_sink,
                                  dec_batch, dec_seq, past, li)
            w_out = w_oute_b
        else:
            q_c, ckv_c, kr_c = _proj_odd_call(h, w_odd_b, q_gains, wq_b, kv_gains, j, 0, nc, None,
                                              dec_seq)
            q_l, ckv_l, kr_l = _proj_odd_call(h, w_odd_b, q_gains, wq_b, kv_gains, j, nc, ns,
                                              mla_tabs, dec_seq)
            ckv_odd.append(ckv_c.reshape(batch, seq, MLA_KV_RANK))
            kr_odd.append(kr_c.reshape(batch, seq, LANES)[..., 64:96])
            o_c = _attn_odd_call(q_c, ckv_c, kr_c, None, j, wk_b, wv_b, batch, seq, 0)
            o_l = _attn_odd_call(q_l, ckv_l, kr_l, (cache_mla_ckv, ckr), j, wk_b, wv_b,
                                 dec_batch, dec_seq, past)
            w_out = w_outo_b
        x, h2, aff = _outproj_call(xa, xb, xb_off, o_c, o_l, w_out, j, mods, i, ffn_gains,
                                   w_router_b, modrow)
        rowid, gate, tab = _select_call(aff, 2, ntok)
        xe = _dispatch_call(tab, h2, rowid, 2, ntok)
        y = _ffn_call(tab, xe, w_gate_exp, w_up_exp, w_down_exp, i, 2, ntok)
        final = i == depth - 1
        nxt = i if final else i + 1
        out_a, out_b = _combine_call(tab, y, rowid, gate, x, mods, i, nxt, mix_gains,
                                     depth if final else nxt, modrow, 2, ntok, final)
        if final:
            y_prompt = out_a.reshape(batch, seq, D)
            y_sample = out_b.reshape(dec_batch, dec_seq, D)
        else:
            x, h = out_a, out_b
            xa, xb, xb_off = x, x, nc // TM

    return (y_prompt, y_sample, jnp.stack(kd_new, axis=1), jnp.stack(vd_new, axis=1),
            jnp.stack(ks_new, axis=1), jnp.stack(vs_new, axis=1), jnp.stack(ckv_odd, axis=1),
            jnp.stack(kr_odd, axis=1))
```

```python
import functools
import math

import jax
import jax.numpy as jnp
from jax import lax
from jax.experimental import pallas as pl
from jax.experimental.pallas import tpu as pltpu

F32 = jnp.float32
BF16 = jnp.bfloat16
I32 = jnp.int32

D = 1024
HD = 64
GRID_W = 64
WINDOW = 128
DIFF_HEADS = 4
SWA_HEADS = 8
MLA_HEADS = 16
MLA_Q_RANK = 384
MLA_KV_RANK = 256
MLA_ROPE = 32
N_EXPERTS = 16
EXPERT_FF = 512
EC_FACTOR = 2
ROPE_BASE = 10000.0
EPS = 1e-6
NEG_INF = -1e30
LOG2E = math.log2(math.e)
EVEN_IN = 2304
LANES = 128
TM = 512
TQ = 256
TB = 256
SEG = 16
VMEM_LIMIT = 56 * 1024 * 1024


def _cparams(sem, vmem=VMEM_LIMIT):
    return pltpu.CompilerParams(dimension_semantics=sem, vmem_limit_bytes=vmem)


def _dot(a, b):
    return jnp.dot(a, b, preferred_element_type=F32)


def _dot_nt(a, b):
    return lax.dot_general(a, b, (((1,), (1,)), ((), ())), preferred_element_type=F32)


def _silu(x):
    return x / (1.0 + jnp.exp(-x))


def _rms(x, g):
    ms = jnp.mean(x * x, axis=-1, keepdims=True)
    return x * lax.rsqrt(ms + EPS) * g


def _modulate(x, g, shift, scale):
    return _rms(x, g) * (1.0 + scale) + shift


def _lambda_init(layer):
    return 0.8 - 0.6 * math.exp(-0.3 * layer)


def _ada_kernel(c_ref, w_ref, b_ref, o_ref):
    s = _silu(c_ref[...]).astype(BF16)
    o_ref[...] = _dot(s, w_ref[...].astype(BF16)) + b_ref[...]


def _ada_call(cvec, w_ada, b_ada):
    depth, _, n6 = w_ada.shape
    rc = cvec.shape[0]
    tn = 512
    return pl.pallas_call(
        _ada_kernel,
        grid=(depth, n6 // tn),
        in_specs=[pl.BlockSpec((rc, D), lambda i, n: (0, 0)),
                  pl.BlockSpec((None, D, tn), lambda i, n: (i, 0, n)),
                  pl.BlockSpec((None, 1, tn), lambda i, n: (i, 0, n))],
        out_specs=pl.BlockSpec((None, rc, tn), lambda i, n: (i, 0, n)),
        out_shape=jax.ShapeDtypeStruct((depth, rc, n6), F32),
        compiler_params=_cparams(("parallel", "parallel")),
        name="ada",
    )(cvec, w_ada, b_ada.reshape(depth, 1, n6))


def _two_part_specs(rows_a, rows_b, width, off_b=0):
    na, nb_ = rows_a // TM, rows_b // TM
    return (pl.BlockSpec((TM, width), lambda r: (jnp.minimum(r, na - 1), 0)),
            pl.BlockSpec((TM, width), lambda r: (off_b + jnp.clip(r - na, 0, nb_ - 1), 0)))


def _norm_mod_kernel(xa_ref, xb_ref, mod_ref, g_ref, h_ref, *, ntile_a):
    def run(x_ref):
        h_ref[...] = _modulate(x_ref[...], g_ref[...], mod_ref[0], mod_ref[1]).astype(BF16)

    @pl.when(pl.program_id(0) < ntile_a)
    def _():
        run(xa_ref)

    @pl.when(pl.program_id(0) >= ntile_a)
    def _():
        run(xb_ref)


def _norm_mod_call(xa, xb, mods, layer, gains, modrow):
    na, nb_ = xa.shape[0], xb.shape[0]
    return pl.pallas_call(
        functools.partial(_norm_mod_kernel, ntile_a=na // TM),
        grid=((na + nb_) // TM,),
        in_specs=[*_two_part_specs(na, nb_, D),
                  pl.BlockSpec((None, None, 6, 1, D), lambda r: (layer, modrow(r), 0, 0, 0)),
                  pl.BlockSpec((None, 1, D), lambda r: (layer, 0, 0))],
        out_specs=pl.BlockSpec((TM, D), lambda r: (r, 0)),
        out_shape=jax.ShapeDtypeStruct((na + nb_, D), BF16),
        compiler_params=_cparams(("parallel",)),
        name="norm_mod",
    )(xa, xb, mods, gains)


def _rope_tables(dec_seq, half, lane_of_dim):
    pos = jnp.arange(dec_seq)
    row = (pos // GRID_W).astype(F32)
    col = (pos % GRID_W).astype(F32)
    inv = ROPE_BASE ** (-(jnp.arange(half, dtype=F32) / half))
    ang = jnp.stack([row[:, None] * inv[None, :], col[:, None] * inv[None, :]])
    cos, sin = jnp.cos(ang), jnp.sin(ang)
    c_cols, s1_cols, s2_cols = [], [], []
    one, zero = jnp.ones((dec_seq,), F32), jnp.zeros((dec_seq,), F32)
    for lane in range(LANES):
        info = lane_of_dim(lane)
        if info is None:
            c_cols.append(one); s1_cols.append(zero); s2_cols.append(zero)
            continue
        axis, k, second = info
        c_cols.append(cos[axis, :, k])
        if second:
            s1_cols.append(zero); s2_cols.append(sin[axis, :, k])
        else:
            s1_cols.append(-sin[axis, :, k]); s2_cols.append(zero)
    return (jnp.stack(c_cols, axis=1), jnp.stack(s1_cols, axis=1), jnp.stack(s2_cols, axis=1))


def _even_lane(lane):
    j = lane % HD
    axis, jj = j // 32, j % 32
    return axis, jj % 16, jj >= 16


def _mla_lane(lane):
    if lane < 64 or lane >= 96:
        return None
    jj = lane - 64
    axis, k = jj // 16, jj % 16
    return axis, k % 8, k >= 8


def _rope(x, c, s1, s2, shift):
    return x * c + pltpu.roll(x, LANES - shift, 1) * s1 + pltpu.roll(x, shift, 1) * s2


_EVEN_ROPE_TILES = tuple(range(0, 8)) + tuple(range(12, 17))
_EVEN_Q_TILES = tuple(range(0, 4)) + tuple(range(12, 16))


def _proj_even_kernel(*refs, rope, caches):
    h_ref, w_ref = refs[0], refs[1]
    pos = 2
    if rope:
        c_ref, s1_ref, s2_ref = refs[2:5]
        pos = 5
    qkv_ref = refs[pos]
    res = _dot(h_ref[...], w_ref[...])
    scale = HD ** -0.5 * LOG2E
    for t in range(EVEN_IN // LANES):
        x = res[:, t * LANES:(t + 1) * LANES]
        if rope and t in _EVEN_ROPE_TILES:
            x = _rope(x, c_ref[...], s1_ref[...], s2_ref[...], 16)
        if t in _EVEN_Q_TILES:
            x = x * scale
        qkv_ref[:, t * LANES:(t + 1) * LANES] = x.astype(BF16)
    if caches:
        kd_ref, vd_ref, ks_ref, vs_ref = refs[pos + 1:pos + 5]
        kd_ref[...] = res[:, 512:1024]
        vd_ref[...] = res[:, 1024:1536]
        ks_ref[...] = res[:, 2048:2176]
        vs_ref[...] = res[:, 2176:2304]


def _proj_even_call(h, w, j, row0, nrows, tables, dec_seq):
    rope = tables is not None
    caches = not rope
    t0 = row0 // TM
    in_specs = [pl.BlockSpec((TM, D), lambda r: (t0 + r, 0)),
                pl.BlockSpec((None, D, EVEN_IN), lambda r: (j, 0, 0))]
    args = [h, w]
    if rope:
        per = dec_seq // TM
        for _ in range(3):
            in_specs.append(pl.BlockSpec((TM, LANES), lambda r: (r % per, 0)))
        args += list(tables)
    out_specs = [pl.BlockSpec((TM, EVEN_IN), lambda r: (r, 0))]
    out_shape = [jax.ShapeDtypeStruct((nrows, EVEN_IN), BF16)]
    if caches:
        for width in (512, 512, LANES, LANES):
            out_specs.append(pl.BlockSpec((TM, width), lambda r: (r, 0)))
            out_shape.append(jax.ShapeDtypeStruct((nrows, width), F32))
    return pl.pallas_call(
        functools.partial(_proj_even_kernel, rope=rope, caches=caches),
        grid=(nrows // TM,),
        in_specs=in_specs, out_specs=out_specs, out_shape=out_shape,
        compiler_params=_cparams(("parallel",)),
        name="proj_even_lat" if rope else "proj_even_ctx",
    )(*args)


ODD_IN_PAD = MLA_Q_RANK + MLA_KV_RANK + LANES
MLA_QW = MLA_HEADS * LANES


def _proj_odd_kernel(*refs, rope):
    h_ref, w_ref, qn_ref, wq_ref, kvn_ref = refs[:5]
    pos = 5
    if rope:
        c_ref, s1_ref, s2_ref = refs[5:8]
        pos = 8
    q_ref, ckv_ref, kr_ref = refs[pos:pos + 3]
    res = _dot(h_ref[...], w_ref[...])
    cq = _rms(res[:, :MLA_Q_RANK], qn_ref[...]).astype(BF16)
    ckv_ref[...] = _rms(res[:, MLA_Q_RANK:MLA_Q_RANK + MLA_KV_RANK], kvn_ref[...])
    kr = res[:, MLA_Q_RANK + MLA_KV_RANK:]
    if rope:
        kr = _rope(kr, c_ref[...], s1_ref[...], s2_ref[...], 8)
    kr_ref[...] = kr
    q = _dot(cq, wq_ref[...])
    scale = (HD + MLA_ROPE) ** -0.5 * LOG2E
    for t in range(MLA_HEADS):
        x = q[:, t * LANES:(t + 1) * LANES]
        if rope:
            x = _rope(x, c_ref[...], s1_ref[...], s2_ref[...], 8)
        q_ref[:, t * LANES:(t + 1) * LANES] = (x * scale).astype(BF16)


def _proj_odd_call(h, w_in, qn, wq, kvn, j, row0, nrows, tables, dec_seq):
    rope = tables is not None
    t0 = row0 // TM
    in_specs = [pl.BlockSpec((TM, D), lambda r: (t0 + r, 0)),
                pl.BlockSpec((None, D, ODD_IN_PAD), lambda r: (j, 0, 0)),
                pl.BlockSpec((None, 1, MLA_Q_RANK), lambda r: (j, 0, 0)),
                pl.BlockSpec((None, MLA_Q_RANK, MLA_QW), lambda r: (j, 0, 0)),
                pl.BlockSpec((None, 1, MLA_KV_RANK), lambda r: (j, 0, 0))]
    args = [h, w_in, qn, wq, kvn]
    if rope:
        per = dec_seq // TM
        for _ in range(3):
            in_specs.append(pl.BlockSpec((TM, LANES), lambda r: (r % per, 0)))
        args += list(tables)
    return pl.pallas_call(
        functools.partial(_proj_odd_kernel, rope=rope),
        grid=(nrows // TM,),
        in_specs=in_specs,
        out_specs=[pl.BlockSpec((TM, MLA_QW), lambda r: (r, 0)),
                   pl.BlockSpec((TM, MLA_KV_RANK), lambda r: (r, 0)),
                   pl.BlockSpec((TM, LANES), lambda r: (r, 0))],
        out_shape=[jax.ShapeDtypeStruct((nrows, MLA_QW), BF16),
                   jax.ShapeDtypeStruct((nrows, MLA_KV_RANK), F32),
                   jax.ShapeDtypeStruct((nrows, LANES), F32)],
        compiler_params=_cparams(("parallel",)),
        name="proj_odd_lat" if rope else "proj_odd_ctx",
    )(*args)


def _ones_lane(half):
    return HD if half == 0 else 0


def _row_sum(e, o, half, from_matmul):
    if from_matmul:
        one = _ones_lane(half)
        return o[:, one:one + 1]
    return jnp.sum(e, axis=-1, keepdims=True)


def _half_values(v, half):
    lane = lax.broadcasted_iota(I32, (1, LANES), 1)
    keep = (lane < HD) if half == 0 else (lane >= HD)
    return jnp.where(keep, v, jnp.where(lane == _ones_lane(half), 1.0, 0.0)).astype(BF16)


def _attn_even_kernel(*refs, seq, past, lam_init, layer):
    latent = past > 0
    n = past + seq
    if latent:
        (qkv_ref, ck_ref, cv_ref, sk_ref, sv_ref, lam_ref, subln_ref, sink_ref,
         o_ref, kd, vd, ka, vl, vh) = refs
    else:
        qkv_ref, lam_ref, subln_ref, sink_ref, o_ref, kd, vd, ka, vl, vh = refs
    qi = pl.program_id(1)
    lo = lax.broadcasted_iota(I32, (1, LANES), 1) < HD

    @pl.when(qi == 0)
    def _build():
        chunk = 256
        for c0 in range(0, n, chunk):
            rows = slice(c0, c0 + chunk)
            if c0 < past:
                prow = slice(c0, c0 + chunk)
                for h in range(DIFF_HEADS):
                    kd[rows, h * LANES:(h + 1) * LANES] = ck_ref[prow, h, :].astype(BF16)
                    vd[rows, h * LANES:(h + 1) * LANES] = cv_ref[prow, h, :].astype(BF16)
                kt = jnp.concatenate([sk_ref[prow, 0, :], sk_ref[prow, 1, :]], axis=1)
                vt = jnp.concatenate([sv_ref[prow, 0, :], sv_ref[prow, 1, :]], axis=1)
            else:
                orow = slice(c0 - past, c0 - past + chunk)
                kd[rows, :] = qkv_ref[orow, 512:1024]
                vd[rows, :] = qkv_ref[orow, 1024:1536]
                kt = qkv_ref[orow, 2048:2176].astype(F32)
                vt = qkv_ref[orow, 2176:2304].astype(F32)
            kr = pltpu.roll(kt, HD, 1)
            vr = pltpu.roll(vt, HD, 1)
            ka[0, rows, :] = jnp.where(lo, kt, kr).astype(BF16)
            ka[1, rows, :] = jnp.where(lo, kr, kt).astype(BF16)
            vl[0, rows, :] = _half_values(vt, 0)
            vh[0, rows, :] = _half_values(vr, 1)
            vl[1, rows, :] = _half_values(vr, 0)
            vh[1, rows, :] = _half_values(vt, 1)

    r0 = pl.multiple_of(qi * TQ, TQ)
    lam = lam_ref[...]
    lam_full = (jnp.exp(jnp.sum(lam[0:1] * lam[1:2], axis=-1, keepdims=True))
                - jnp.exp(jnp.sum(lam[2:3] * lam[3:4], axis=-1, keepdims=True)) + lam_init)
    zero_b = jnp.zeros((), BF16)

    for h in range(DIFF_HEADS):
        cs = slice(h * LANES, (h + 1) * LANES)
        qt = qkv_ref[pl.ds(r0, TQ), cs]
        kh = kd[:, cs]
        es, rs = [], []
        for comp in range(2):
            qc = jnp.where(lo, qt, zero_b) if comp == 0 else jnp.where(lo, zero_b, qt)
            s = _dot_nt(qc, kh)
            m = jnp.max(s, axis=-1, keepdims=True)
            e = jnp.exp2(s - m)
            es.append(e)
            rs.append(1.0 / jnp.sum(e, axis=-1, keepdims=True))
        a = es[0] * rs[0] - es[1] * (lam_full * rs[1])
        o = _dot(a.astype(BF16), vd[:, cs])
        o = _rms(o, subln_ref[...]) * (1.0 - lam_init)
        o_ref[:, cs] = o.astype(BF16)

    nblk = seq // WINDOW
    dense = past if latent else seq
    if latent:
        per = TQ // WINDOW
        offsets = tuple(range(-1, per + 1))
        rr = lax.broadcasted_iota(I32, (TQ, WINDOW), 0)
        cc = lax.broadcasted_iota(I32, (TQ, WINDOW), 1)
        band, starts = {}, {}
        for d in offsets:
            blk = qi * per + d
            inside = jnp.logical_and(blk >= 0, blk < nblk)
            band[d] = jnp.logical_and(jnp.abs(rr - cc - d * WINDOW) <= WINDOW, inside)
            starts[d] = pl.multiple_of(past + jnp.clip(blk, 0, nblk - 1) * WINDOW, WINDOW)
    for i in range(SWA_HEADS // 2):
        hk = i // 2
        cs = slice(1536 + i * LANES, 1536 + (i + 1) * LANES)
        qt = qkv_ref[pl.ds(r0, TQ), cs]
        halves = []
        for half in range(2):
            qc = jnp.where(lo, qt, zero_b) if half == 0 else jnp.where(lo, zero_b, qt)
            vsel = vl if half == 0 else vh
            sink = sink_ref[layer, 2 * i + half] * LOG2E
            parts = [_dot_nt(qc, ka[hk, 0:dense, :])]
            if latent:
                for d in offsets:
                    s = _dot_nt(qc, ka[hk, pl.ds(starts[d], WINDOW), :])
                    parts.append(jnp.where(band[d], s, NEG_INF))
            s_all = jnp.concatenate(parts, axis=1) if len(parts) > 1 else parts[0]
            m = jnp.maximum(jnp.max(s_all, axis=-1, keepdims=True), sink)
            e = jnp.exp2(s_all - m)
            eb = e.astype(BF16)
            o = _dot(eb[:, 0:dense], vsel[hk, 0:dense, :])
            if latent:
                for k, d in enumerate(offsets):
                    o += _dot(eb[:, dense + k * WINDOW:dense + (k + 1) * WINDOW],
                              vsel[hk, pl.ds(starts[d], WINDOW), :])
            den = _row_sum(e, o, half, from_matmul=latent) + jnp.exp2(sink - m)
            halves.append(o * (1.0 / den))
        o_ref[:, 512 + i * LANES:512 + (i + 1) * LANES] = jnp.where(lo, halves[0], halves[1]).astype(BF16)


def _attn_even_call(qkv, caches, j, lam, subln, sink, nbatch, seq, past, lam_init):
    n = past + seq
    latent = past > 0
    in_specs = [pl.BlockSpec((seq, EVEN_IN), lambda b, q: (b, 0))]
    args = [qkv]
    if latent:
        ck, cv, sk, sv = caches
        in_specs += [pl.BlockSpec((None, None, past, DIFF_HEADS, 2 * HD), lambda b, q: (b, j, 0, 0, 0)),
                     pl.BlockSpec((None, None, past, DIFF_HEADS, 2 * HD), lambda b, q: (b, j, 0, 0, 0)),
                     pl.BlockSpec((None, None, past, 2, HD), lambda b, q: (b, j, 0, 0, 0)),
                     pl.BlockSpec((None, None, past, 2, HD), lambda b, q: (b, j, 0, 0, 0))]
        args += [ck, cv, sk, sv]
    in_specs += [pl.BlockSpec((None, 4, HD), lambda b, q: (j, 0, 0)),
                 pl.BlockSpec((None, 1, 2 * HD), lambda b, q: (j, 0, 0)),
                 pl.BlockSpec(memory_space=pltpu.SMEM)]
    args += [lam, subln, sink]
    return pl.pallas_call(
        functools.partial(_attn_even_kernel, seq=seq, past=past, lam_init=lam_init, layer=j),
        grid=(nbatch, seq // TQ),
        in_specs=in_specs,
        out_specs=pl.BlockSpec((TQ, D), lambda b, q: (b * (seq // TQ) + q, 0)),
        out_shape=jax.ShapeDtypeStruct((nbatch * seq, D), BF16),
        scratch_shapes=[pltpu.VMEM((n, 512), BF16), pltpu.VMEM((n, 512), BF16),
                        pltpu.VMEM((2, n, LANES), BF16), pltpu.VMEM((2, n, LANES), BF16),
                        pltpu.VMEM((2, n, LANES), BF16)],
        compiler_params=_cparams(("arbitrary", "arbitrary")),
        name="attn_even_lat" if latent else "attn_even_ctx",
    )(*args)


def _attn_odd_kernel(*refs, seq, past):
    latent = past > 0
    n = past + seq
    if latent:
        q_ref, ckv_ref, kr_ref, cckv_ref, ckr_ref, wk_ref, wv_ref, o_ref, kf, vlo, vhi = refs
    else:
        q_ref, ckv_ref, kr_ref, wk_ref, wv_ref, o_ref, kf, vlo, vhi = refs
    qi = pl.program_id(1)
    lo = lax.broadcasted_iota(I32, (1, LANES), 1) < HD

    @pl.when(qi == 0)
    def _build():
        chunk = 256
        for c0 in range(0, n, chunk):
            rows = slice(c0, c0 + chunk)
            if c0 < past:
                ckv = cckv_ref[c0:c0 + chunk, :].astype(BF16)
                kr = ckr_ref[c0:c0 + chunk, :]
            else:
                ckv = ckv_ref[c0 - past:c0 - past + chunk, :].astype(BF16)
                kr = kr_ref[c0 - past:c0 - past + chunk, :]
            kk = _dot(ckv, wk_ref[...])
            for h in range(MLA_HEADS):
                cs = slice(h * LANES, (h + 1) * LANES)
                kf[rows, cs] = (kk[:, cs] + kr).astype(BF16)
            vv = _dot(ckv, wv_ref[...])
            for i in range(MLA_HEADS // 2):
                cs = slice(i * LANES, (i + 1) * LANES)
                vlo[rows, cs] = _half_values(vv[:, cs], 0)
                vhi[rows, cs] = _half_values(vv[:, cs], 1)

    r0 = pl.multiple_of(qi * TQ, TQ)
    lo = lax.broadcasted_iota(I32, (1, LANES), 1) < HD
    for i in range(MLA_HEADS // 2):
        halves = []
        for half in range(2):
            h = 2 * i + half
            cs = slice(h * LANES, (h + 1) * LANES)
            s = _dot_nt(q_ref[pl.ds(r0, TQ), cs], kf[:, cs])
            m = jnp.max(s, axis=-1, keepdims=True)
            vsel = vlo if half == 0 else vhi
            e = jnp.exp2(s - m)
            o = _dot(e.astype(BF16), vsel[:, i * LANES:(i + 1) * LANES])
            halves.append(o * (1.0 / _row_sum(e, o, half, from_matmul=latent)))
        o_ref[:, i * LANES:(i + 1) * LANES] = jnp.where(lo, halves[0], halves[1]).astype(BF16)


def _attn_odd_call(q, ckv, kr, caches, j, wk, wv, nbatch, seq, past):
    n = past + seq
    latent = past > 0
    in_specs = [pl.BlockSpec((seq, MLA_QW), lambda b, qq: (b, 0)),
                pl.BlockSpec((seq, MLA_KV_RANK), lambda b, qq: (b, 0)),
                pl.BlockSpec((seq, LANES), lambda b, qq: (b, 0))]
    args = [q, ckv, kr]
    if latent:
        in_specs += [pl.BlockSpec((None, None, past, MLA_KV_RANK), lambda b, qq: (b, j, 0, 0)),
                     pl.BlockSpec((None, None, past, LANES), lambda b, qq: (b, j, 0, 0))]
        args += list(caches)
    in_specs += [pl.BlockSpec((None, MLA_KV_RANK, MLA_QW), lambda b, qq: (j, 0, 0)),
                 pl.BlockSpec((None, MLA_KV_RANK, D), lambda b, qq: (j, 0, 0))]
    args += [wk, wv]
    return pl.pallas_call(
        functools.partial(_attn_odd_kernel, seq=seq, past=past),
        grid=(nbatch, seq // TQ),
        in_specs=in_specs,
        out_specs=pl.BlockSpec((TQ, D), lambda b, qq: (b * (seq // TQ) + qq, 0)),
        out_shape=jax.ShapeDtypeStruct((nbatch * seq, D), BF16),
        scratch_shapes=[pltpu.VMEM((n, MLA_QW), BF16), pltpu.VMEM((n, D), BF16),
                        pltpu.VMEM((n, D), BF16)],
        compiler_params=_cparams(("arbitrary", "arbitrary")),
        name="attn_odd_lat" if latent else "attn_odd_ctx",
    )(*args)


def _outproj_kernel(xa_ref, xb_ref, oa_ref, ob_ref, w_ref, mod_ref, g_ref, wr_ref,
                    xo_ref, h_ref, aff_ref, *, ntile_a):
    def run(x_ref, o_ref):
        x = x_ref[...] + mod_ref[2] * _dot(o_ref[...], w_ref[...])
        xo_ref[...] = x
        h = _modulate(x, g_ref[...], mod_ref[3], mod_ref[4]).astype(BF16)
        h_ref[...] = h
        logits = _dot(h, wr_ref[...])
        lane = lax.broadcasted_iota(I32, (TM, LANES), 1)
        lg = jnp.where(lane < N_EXPERTS, logits, -jnp.inf)
        e = jnp.exp(lg - jnp.max(lg, axis=-1, keepdims=True))
        aff = e / jnp.sum(e, axis=-1, keepdims=True)
        for c in range(TM // TB):
            aff_ref[c] = aff[c * TB:(c + 1) * TB, :].T[0:N_EXPERTS, :]

    @pl.when(pl.program_id(0) < ntile_a)
    def _():
        run(xa_ref, oa_ref)

    @pl.when(pl.program_id(0) >= ntile_a)
    def _():
        run(xb_ref, ob_ref)


def _outproj_call(xa, xb, xb_off, o_a, o_b, w, widx, mods, layer, gains, w_router, modrow):
    na, nb_ = o_a.shape[0], o_b.shape[0]
    t = na + nb_
    assert TM % TB == 0
    return pl.pallas_call(
        functools.partial(_outproj_kernel, ntile_a=na // TM),
        grid=(t // TM,),
        in_specs=[*_two_part_specs(na, nb_, D, xb_off), *_two_part_specs(na, nb_, D),
                  pl.BlockSpec((None, D, D), lambda r: (widx, 0, 0)),
                  pl.BlockSpec((None, None, 6, 1, D), lambda r: (layer, modrow(r), 0, 0, 0)),
                  pl.BlockSpec((None, 1, D), lambda r: (layer, 0, 0)),
                  pl.BlockSpec((None, D, LANES), lambda r: (layer, 0, 0))],
        out_specs=[pl.BlockSpec((TM, D), lambda r: (r, 0)),
                   pl.BlockSpec((TM, D), lambda r: (r, 0)),
                   pl.BlockSpec((TM // TB, N_EXPERTS, TB), lambda r: (r, 0, 0))],
        out_shape=[jax.ShapeDtypeStruct((t, D), F32), jax.ShapeDtypeStruct((t, D), BF16),
                   jax.ShapeDtypeStruct((t // TB, N_EXPERTS, TB), F32)],
        compiler_params=_cparams(("parallel",)),
        name="outproj",
    )(xa, xb, o_a, o_b, w, mods, gains, w_router)


def _select_kernel(aff_ref, rowid_ref, gate_ref, tab_ref, *, nb, cap):
    ne = N_EXPERTS
    nr = nb * ne
    a = aff_ref[...].reshape(nr, TB)
    ri = lax.broadcasted_iota(I32, (nr, nr), 0)
    ci = lax.broadcasted_iota(I32, (nr, nr), 1)
    same_e = (ri & (ne - 1)) == (ci & (ne - 1))
    same_b = (ri >> 4) == (ci >> 4)
    m_e = jnp.where(same_e, 1.0, 0.0).astype(BF16)
    m_b = jnp.where(same_b, 1.0, 0.0).astype(BF16)
    m_a = jnp.where(jnp.logical_and(same_e, ci < ri), 1.0, 0.0).astype(BF16)
    m_o = jnp.where(jnp.logical_and(same_b, ci < ri), 1.0, 0.0).astype(BF16)
    ui = lax.broadcasted_iota(I32, (TB, TB), 0)
    uj = lax.broadcasted_iota(I32, (TB, TB), 1)
    upper = jnp.where(ui < uj, 1.0, 0.0).astype(BF16)

    def rows_to_lanes(col):
        return jnp.broadcast_to(col, (nr, LANES)).astype(BF16)

    def count_ge(value):
        c = jnp.sum(jnp.where(a >= value, 1.0, 0.0), axis=-1, keepdims=True)
        return _dot(m_e, rows_to_lanes(c))[:, 0:1]

    def bisect(i, v):
        cand = v | jnp.left_shift(jnp.int32(1), 30 - i)
        return jnp.where(count_ge(pltpu.bitcast(cand, F32)) >= cap, cand, v)

    thr = lax.fori_loop(0, 31, bisect, jnp.zeros((nr, 1), I32))

    def refine(i, lohi):
        lo_v, hi_v = lohi
        mid = 0.5 * (lo_v + hi_v)
        take = count_ge(mid) >= cap
        return jnp.where(take, mid, lo_v), jnp.where(take, hi_v, mid)

    lo_v, hi_v = lax.fori_loop(0, TIE_STEPS, refine,
                               (pltpu.bitcast(thr, F32), pltpu.bitcast(thr + 1, F32)))
    gt = jnp.where(a >= hi_v, 1.0, 0.0)
    eq = jnp.where(jnp.logical_and(a >= lo_v, a < hi_v), 1.0, 0.0)
    n_gt = _dot(m_e, rows_to_lanes(jnp.sum(gt, axis=-1, keepdims=True)))[:, 0:1]
    need = cap - n_gt
    eq_before = (_dot(m_a, rows_to_lanes(jnp.sum(eq, axis=-1, keepdims=True)))[:, 0:1]
                 + _dot(eq.astype(BF16), upper))
    sel = jnp.where(jnp.logical_and(eq > 0.0, eq_before < need), 1.0, gt)
    local = _dot(sel.astype(BF16), upper)
    cnt = jnp.sum(sel, axis=-1, keepdims=True)
    seg = jnp.floor((cnt + (SEG - 1)) * (1.0 / SEG)) * SEG
    segb = rows_to_lanes(seg)
    over = jnp.maximum(seg - WINR, 0.0)
    overb = rows_to_lanes(over)
    off_over = FIRST_ROWS + _dot(m_o, overb)[:, 0:1]
    off_buf = _dot(m_a, segb)[:, 0:1]
    over_blk = _dot(m_b, overb)[:, 0:1]
    rows_exp = _dot(m_e, segb)[:, 0:1]
    tiles_exp = jnp.floor((rows_exp + (TF - 1)) / TF)
    expert = (lax.broadcasted_iota(I32, (nr, 1), 0) & (ne - 1)).astype(F32)
    row = jnp.where(local < WINR, expert * WINR + local, off_over + local - WINR)
    rowid_ref[...] = jnp.where(sel > 0.0, row, -1.0).astype(I32).reshape(nb, ne, TB)
    gate_ref[...] = jnp.where(sel > 0.0, a, 0.0).reshape(nb, ne, TB)
    tl = lax.broadcasted_iota(I32, (nr, LANES), 1)
    tab = jnp.where(tl == 0, seg, jnp.where(tl == 1, off_over, jnp.where(
        tl == 2, off_buf, jnp.where(tl == 3, over_blk, jnp.where(tl == 4, rows_exp, tiles_exp)))))
    tab_ref[...] = tab.T[0:8, :].astype(I32)


def _select_call(aff, ngroups, ntok):
    nb = ntok // TB
    cap = EC_FACTOR * ntok // N_EXPERTS
    nr = nb * N_EXPERTS
    return pl.pallas_call(
        functools.partial(_select_kernel, nb=nb, cap=cap),
        grid=(ngroups,),
        in_specs=[pl.BlockSpec((nb, N_EXPERTS, TB), lambda g: (g, 0, 0))],
        out_specs=[pl.BlockSpec((None, nb, N_EXPERTS, TB), lambda g: (g, 0, 0, 0)),
                   pl.BlockSpec((None, nb, N_EXPERTS, TB), lambda g: (g, 0, 0, 0)),
                   pl.BlockSpec((None, 8, nr), lambda g: (g, 0, 0))],
        out_shape=[jax.ShapeDtypeStruct((ngroups, nb, N_EXPERTS, TB), I32),
                   jax.ShapeDtypeStruct((ngroups, nb, N_EXPERTS, TB), F32),
                   jax.ShapeDtypeStruct((ngroups, 8, nr), I32)],
        compiler_params=_cparams(("arbitrary",)),
        name="select",
    )(aff)


STACK_ROWS = N_EXPERTS * TB


FIRST_ROWS = 3 * TB


WINR = 3 * SEG
assert N_EXPERTS * WINR == FIRST_ROWS
TF = 672
TIE_STEPS = 12


def _window_rows(rowid_ref, dst, value_ref=None):
    for e in range(N_EXPERTS):
        rid_e = rowid_ref[e:e + 1, :]
        val_e = 1.0 if value_ref is None else value_ref[e:e + 1, :]
        hit = rid_e == lax.broadcasted_iota(I32, (WINR, TB), 0) + e * WINR
        dst[e * WINR:(e + 1) * WINR, :] = jnp.where(hit, val_e, 0.0).astype(dst.dtype)


class _Table:
    def __init__(self, tab_s, row, nb, per_block=False):
        self.tab_s, self.row, self.per = tab_s, row, nb if per_block else nb * N_EXPERTS
        self.scale = N_EXPERTS if per_block else 1

    def __getitem__(self, k):
        if self.per & (self.per - 1) == 0:
            hi, low = lax.shift_right_logical(k, self.per.bit_length() - 1), k & (self.per - 1)
        else:
            hi, low = lax.div(k, self.per), lax.rem(k, self.per)
        return self.tab_s[hi, self.row, low * self.scale]


def _tables(tab_s, nb):
    return (_Table(tab_s, 0, nb), _Table(tab_s, 1, nb), _Table(tab_s, 2, nb),
            _Table(tab_s, 3, nb, per_block=True))


def _overflow_groups(seg_s, k):
    return lax.div(jnp.maximum(seg_s[k] - WINR, 0), SEG)


def _overflow_rows(seg_s, offo_s, step, over, rowid_ref, dst, value_ref=None):
    def zero(i, carry):
        r0 = pl.multiple_of(FIRST_ROWS + i * SEG, SEG)
        dst[pl.ds(r0, SEG), :] = jnp.zeros((SEG, TB), dst.dtype)
        return carry

    lax.fori_loop(0, ((over + TB - 1) // TB) * (TB // SEG), zero, 0)
    for e in range(N_EXPERTS):
        k = step * N_EXPERTS + e
        rid_e = rowid_ref[e:e + 1, :]
        val_e = 1.0 if value_ref is None else value_ref[e:e + 1, :]

        def group(i, carry, off=offo_s[k], rid_e=rid_e, val_e=val_e):
            r0 = pl.multiple_of(off + i * SEG, SEG)
            hit = rid_e == lax.broadcasted_iota(I32, (SEG, TB), 0) + r0
            dst[pl.ds(r0, SEG), :] = jnp.where(hit, val_e, 0.0).astype(dst.dtype)
            return carry

        lax.fori_loop(0, _overflow_groups(seg_s, k), group, 0)


def _wait_rows(rows, make_copy):
    def big(i, carry):
        make_copy(TB).wait()
        return carry

    def small(i, carry):
        make_copy(SEG).wait()
        return carry

    lax.fori_loop(0, lax.div(rows, TB), big, 0)
    lax.fori_loop(0, lax.div(lax.rem(rows, TB), SEG), small, 0)


def _dispatch_kernel(tab_s, h_ref, rowid_ref, xe_hbm,
                     onehot, stack, zbuf, sem, zsem, *, nb, nsteps, cap):
    g = pl.program_id(0)
    b = pl.program_id(1)
    step = g * nb + b
    slot = lax.rem(step, 2)
    seg_s, offo_s, offb_s, over_s = _tables(tab_s, nb)
    over = over_s[step]
    xrows = xe_hbm.shape[2]

    def wait_slot(nrows, sl):
        _wait_rows(nrows, lambda n: pltpu.make_async_copy(
            stack.at[sl, pl.ds(0, n)], xe_hbm.at[0, 0, pl.ds(0, n)], sem.at[sl]))

    @pl.when(step == 0)
    def _init():
        stack[...] = jnp.zeros_like(stack)

    @pl.when(b == 0)
    def _zero_unused():
        zbuf[...] = jnp.zeros_like(zbuf)
        for e in range(N_EXPERTS):
            pltpu.make_async_copy(zbuf, xe_hbm.at[g, e, pl.ds(cap, xrows - cap)], zsem).start()

    _window_rows(rowid_ref, onehot)
    h = h_ref[...]
    stack[slot, 0:FIRST_ROWS, :] = _dot(onehot[0:FIRST_ROWS, :], h).astype(BF16)

    @pl.when(over > 0)
    def _overflow():
        _overflow_rows(seg_s, offo_s, step, over, rowid_ref, onehot)

        def chunk(c, carry):
            base = pl.multiple_of(FIRST_ROWS + c * TB, TB)
            stack[slot, pl.ds(base, TB), :] = _dot(onehot[pl.ds(base, TB), :], h).astype(BF16)
            return carry

        lax.fori_loop(0, (over + TB - 1) // TB, chunk, 0)

    @pl.when(b == 0)
    def _zero_unused_done():
        for e in range(N_EXPERTS):
            pltpu.make_async_copy(zbuf, xe_hbm.at[g, e, pl.ds(cap, xrows - cap)], zsem).wait()

    @pl.when(step >= 1)
    def _previous_landed():
        wait_slot(FIRST_ROWS + over_s[step - 1], 1 - slot)

    for e in range(N_EXPERTS):
        k = step * N_EXPERTS + e
        pltpu.make_async_copy(
            stack.at[slot, e * WINR:(e + 1) * WINR],
            xe_hbm.at[g, e, pl.ds(pl.multiple_of(offb_s[k], SEG), WINR)], sem.at[slot]).start()

    @pl.when(over > 0)
    def _overflow_copies():
        for e in range(N_EXPERTS):
            k = step * N_EXPERTS + e

            def one(i, carry, e=e, k=k):
                pltpu.make_async_copy(
                    stack.at[slot, pl.ds(pl.multiple_of(offo_s[k] + i * SEG, SEG), SEG)],
                    xe_hbm.at[g, e, pl.ds(pl.multiple_of(offb_s[k] + WINR + i * SEG, SEG), SEG)],
                    sem.at[slot]).start()
                return carry

            lax.fori_loop(0, _overflow_groups(seg_s, k), one, 0)

    @pl.when(step == nsteps - 1)
    def _drain():
        wait_slot(FIRST_ROWS + over, slot)


def _expert_rows(ntok):
    cap = EC_FACTOR * ntok // N_EXPERTS
    worst = cap + (ntok // TB) * (SEG - 1)
    tiles = -(-(worst + WINR) // TF)
    assert cap % SEG == 0 and cap >= WINR and tiles * TF > cap
    return cap, tiles


def _dispatch_call(tab, h, rowid, ngroups, ntok):
    nb = ntok // TB
    cap, tiles = _expert_rows(ntok)
    xrows = tiles * TF
    grid_spec = pltpu.PrefetchScalarGridSpec(
        num_scalar_prefetch=1,
        grid=(ngroups, nb),
        in_specs=[pl.BlockSpec((TB, D), lambda g, b, *_: (g * nb + b, 0)),
                  pl.BlockSpec((None, None, N_EXPERTS, TB), lambda g, b, *_: (g, b, 0, 0))],
        out_specs=pl.BlockSpec(memory_space=pl.ANY),
        scratch_shapes=[pltpu.VMEM((STACK_ROWS, TB), BF16), pltpu.VMEM((2, STACK_ROWS, D), BF16),
                        pltpu.VMEM((xrows - cap, D), BF16),
                        pltpu.SemaphoreType.DMA((2,)), pltpu.SemaphoreType.DMA])
    return pl.pallas_call(
        functools.partial(_dispatch_kernel, nb=nb, nsteps=ngroups * nb, cap=cap),
        grid_spec=grid_spec,
        out_shape=jax.ShapeDtypeStruct((ngroups, N_EXPERTS, xrows, D), BF16),
        compiler_params=_cparams(("arbitrary", "arbitrary")),
        name="dispatch",
    )(tab, h, rowid)


def _ffn_kernel(tab_s, xe_ref, wg_hbm, wu_hbm, wd_hbm, y_ref, wg32, wu32, wd32, wgb, wub, wdb, sem,
                *, layer):
    e = pl.program_id(0)
    g = pl.program_id(1)
    j = pl.program_id(2)

    def weight_copies(ee, sl):
        return (pltpu.make_async_copy(wg_hbm.at[layer, ee], wg32.at[sl], sem.at[sl]),
                pltpu.make_async_copy(wu_hbm.at[layer, ee], wu32.at[sl], sem.at[sl]),
                pltpu.make_async_copy(wd_hbm.at[layer, ee], wd32.at[sl], sem.at[sl]))

    @pl.when(jnp.logical_and(g == 0, j == 0))
    def _weights():
        sl = lax.rem(e, 2)

        @pl.when(e == 0)
        def _():
            for cp in weight_copies(e, sl):
                cp.start()

        @pl.when(e + 1 < N_EXPERTS)
        def _():
            for cp in weight_copies(e + 1, 1 - sl):
                cp.start()

        for cp in weight_copies(e, sl):
            cp.wait()
        wgb[...] = wg32[sl].astype(BF16)
        wub[...] = wu32[sl].astype(BF16)
        wdb[...] = wd32[sl].astype(BF16)

    live = j < tab_s[g, 5, e]

    @pl.when(live)
    def _run():
        x = xe_ref[...]
        hid = (_silu(_dot(x, wgb[...])) * _dot(x, wub[...])).astype(BF16)
        y_ref[...] = _dot(hid, wdb[...]).astype(BF16)

    @pl.when(jnp.logical_not(live))
    def _skip():
        y_ref[...] = jnp.zeros_like(y_ref)


def _ffn_call(tab, xe, wg, wu, wd, layer, ngroups, ntok):
    _, tiles = _expert_rows(ntok)

    def xmap(e, g, j, tab_s):
        return (g, e, jnp.minimum(j, tab_s[g, 5, e] - 1), 0)

    grid_spec = pltpu.PrefetchScalarGridSpec(
        num_scalar_prefetch=1,
        grid=(N_EXPERTS, ngroups, tiles),
        in_specs=[pl.BlockSpec((None, None, TF, D), xmap),
                  pl.BlockSpec(memory_space=pl.ANY), pl.BlockSpec(memory_space=pl.ANY),
                  pl.BlockSpec(memory_space=pl.ANY)],
        out_specs=pl.BlockSpec((None, None, TF, D), lambda e, g, j, nt: (g, e, j, 0)),
        scratch_shapes=[pltpu.VMEM((2, D, EXPERT_FF), F32), pltpu.VMEM((2, D, EXPERT_FF), F32),
                        pltpu.VMEM((2, EXPERT_FF, D), F32),
                        pltpu.VMEM((D, EXPERT_FF), BF16), pltpu.VMEM((D, EXPERT_FF), BF16),
                        pltpu.VMEM((EXPERT_FF, D), BF16), pltpu.SemaphoreType.DMA((2,))])
    return pl.pallas_call(
        functools.partial(_ffn_kernel, layer=layer),
        grid_spec=grid_spec,
        out_shape=jax.ShapeDtypeStruct((ngroups, N_EXPERTS, tiles * TF, D), BF16),
        compiler_params=_cparams(("arbitrary", "arbitrary", "arbitrary")),
        name="ffn",
    )(tab, xe, wg, wu, wd)


def _combine_kernel(tab_s, y_hbm, rowid_ref, gate_ref, x_ref, mod_ref,
                    modn_ref, g_ref, xo_ref, h_ref, weights, stack, acc, sem, *, nb, nsteps, final):
    g = pl.program_id(0)
    b = pl.program_id(1)
    step = g * nb + b
    slot = lax.rem(step, 2)
    seg_s, offo_s, offb_s, over_s = _tables(tab_s, nb)
    over = over_s[step]

    def fetch(st, sl):
        gg = lax.div(st, nb)
        for e in range(N_EXPERTS):
            k = st * N_EXPERTS + e
            pltpu.make_async_copy(
                y_hbm.at[gg, e, pl.ds(pl.multiple_of(offb_s[k], SEG), WINR)],
                stack.at[sl, e * WINR:(e + 1) * WINR], sem.at[sl]).start()

        @pl.when(over_s[st] > 0)
        def _():
            for e in range(N_EXPERTS):
                k = st * N_EXPERTS + e

                def one(i, carry, e=e, k=k):
                    pltpu.make_async_copy(
                        y_hbm.at[gg, e, pl.ds(pl.multiple_of(offb_s[k] + WINR + i * SEG, SEG), SEG)],
                        stack.at[sl, pl.ds(pl.multiple_of(offo_s[k] + i * SEG, SEG), SEG)],
                        sem.at[sl]).start()
                    return carry

                lax.fori_loop(0, _overflow_groups(seg_s, k), one, 0)

    @pl.when(step == 0)
    def _first():
        stack[...] = jnp.zeros_like(stack)
        fetch(step, slot)

    if nsteps > 1:
        @pl.when(step + 1 < nsteps)
        def _prefetch():
            fetch(step + 1, 1 - slot)

    _window_rows(rowid_ref, weights, gate_ref)

    @pl.when(over > 0)
    def _():
        _overflow_rows(seg_s, offo_s, step, over, rowid_ref, weights, gate_ref)

    _wait_rows(FIRST_ROWS + over, lambda n: pltpu.make_async_copy(
        y_hbm.at[0, 0, pl.ds(0, n)], stack.at[slot, pl.ds(0, n)], sem.at[slot]))

    def token_weights(base):
        return weights[pl.ds(base, TB), :].T.astype(BF16)

    w = jnp.concatenate([token_weights(c * TB) for c in range(FIRST_ROWS // TB)], axis=1)
    acc[...] = _dot(w, stack[slot, 0:FIRST_ROWS, :])

    @pl.when(over > 0)
    def _():
        def chunk(c, carry):
            base = pl.multiple_of(FIRST_ROWS + c * TB, TB)
            acc[...] += _dot(token_weights(base), stack[slot, pl.ds(base, TB), :])
            return carry

        lax.fori_loop(0, (over + TB - 1) // TB, chunk, 0)

    x = x_ref[...] + mod_ref[5] * acc[...]
    if final:
        y = _rms(x, g_ref[...])

        @pl.when(g == 0)
        def _():
            xo_ref[...] = y

        @pl.when(g != 0)
        def _():
            h_ref[...] = y
    else:
        xo_ref[...] = x
        h_ref[...] = _modulate(x, g_ref[...], modn_ref[0], modn_ref[1]).astype(BF16)


def _combine_call(tab, y, rowid, gate, x, mods, layer, nxt, gains, gidx, modrow, ngroups, ntok, final):
    nb = ntok // TB
    t = x.shape[0]
    if final:
        assert ngroups == 2
        out_specs = [pl.BlockSpec((TB, D), lambda g, b, *_: (jnp.where(g == 0, b, nb - 1), 0)),
                     pl.BlockSpec((TB, D), lambda g, b, *_: (jnp.where(g == 0, 0, b), 0))]
        out_shape = [jax.ShapeDtypeStruct((ntok, D), F32), jax.ShapeDtypeStruct((ntok, D), F32)]
    else:
        out_specs = [pl.BlockSpec((TB, D), lambda g, b, *_: (g * nb + b, 0)),
                     pl.BlockSpec((TB, D), lambda g, b, *_: (g * nb + b, 0))]
        out_shape = [jax.ShapeDtypeStruct((t, D), F32), jax.ShapeDtypeStruct((t, D), BF16)]
    grid_spec = pltpu.PrefetchScalarGridSpec(
        num_scalar_prefetch=1,
        grid=(ngroups, nb),
        in_specs=[pl.BlockSpec(memory_space=pl.ANY),
                  pl.BlockSpec((None, None, N_EXPERTS, TB), lambda g, b, *_: (g, b, 0, 0)),
                  pl.BlockSpec((None, None, N_EXPERTS, TB), lambda g, b, *_: (g, b, 0, 0)),
                  pl.BlockSpec((TB, D), lambda g, b, *_: (g * nb + b, 0)),
                  pl.BlockSpec((None, None, 6, 1, D),
                               lambda g, b, *_: (layer, modrow(g * nb + b), 0, 0, 0)),
                  pl.BlockSpec((None, None, 6, 1, D),
                               lambda g, b, *_: (nxt, modrow(g * nb + b), 0, 0, 0)),
                  pl.BlockSpec((None, 1, D), lambda g, b, *_: (gidx, 0, 0))],
        out_specs=out_specs,
        scratch_shapes=[pltpu.VMEM((STACK_ROWS, TB), F32), pltpu.VMEM((2, STACK_ROWS, D), BF16),
                        pltpu.VMEM((TB, D), F32), pltpu.SemaphoreType.DMA((2,))])
    return pl.pallas_call(
        functools.partial(_combine_kernel, nb=nb, nsteps=ngroups * nb, final=final),
        grid_spec=grid_spec,
        out_shape=out_shape,
        compiler_params=_cparams(("arbitrary", "arbitrary")),
        name="combine",
    )(tab, y, rowid, gate, x, mods, mods, gains)


def kernel(x_prompt, x_sample, cache_diff_k, cache_diff_v, cache_swa_k, cache_swa_v, cache_mla_ckv, cache_mla_krope, c, c_ctx, w_ada, b_ada, norm_mix, norm_ffn, w_in_even, w_out_even, diff_lambda, diff_subln, swa_sink, w_in_odd, mla_q_norm, w_q_up, mla_kv_norm, w_kv_up, w_out_odd, w_router, w_gate_exp, w_up_exp, w_down_exp, final_norm):
    batch, seq, _ = x_prompt.shape
    dec_batch, dec_seq, _ = x_sample.shape
    past = cache_diff_k.shape[2]
    depth = w_ada.shape[0]
    n_even = w_in_even.shape[0]
    n_odd = w_in_odd.shape[0]
    nc, ns = batch * seq, dec_batch * dec_seq
    assert nc == ns, "the routed-expert kernels take two token groups of equal size"
    assert nc % TM == 0 and dec_seq % TM == 0 and past % 256 == 0 and dec_seq % GRID_W == 0
    assert seq % TQ == 0 and dec_seq % TQ == 0
    ntok = nc

    def modrow_of(tile):
        def modrow(r):
            tok = r * tile
            return jnp.where(tok < nc, 0, 1 + jnp.maximum(tok - nc, 0) // dec_seq)
        return modrow

    modrow, modrow_tb = modrow_of(TM), modrow_of(TB)

    rc = -(-(1 + dec_batch) // 16) * 16
    cvec = jnp.zeros((rc, D), F32).at[0].set(c_ctx).at[1:1 + dec_batch].set(c)
    mods = _ada_call(cvec, w_ada, b_ada).reshape(depth, rc, 6, 1, D)

    w_even_b = w_in_even.astype(BF16)
    w_oute_b = w_out_even.astype(BF16)
    w_outo_b = w_out_odd.astype(BF16)
    kr_pad = jnp.zeros((n_odd, D, LANES), F32).at[:, :, 64:96].set(w_in_odd[:, :, 640:672])
    w_odd_b = jnp.concatenate([w_in_odd[:, :, :640], kr_pad], axis=-1).astype(BF16)
    wq = w_q_up.reshape(n_odd, MLA_Q_RANK, MLA_HEADS, HD + MLA_ROPE)
    wq_b = jnp.pad(wq, ((0, 0), (0, 0), (0, 0), (0, LANES - HD - MLA_ROPE))).reshape(
        n_odd, MLA_Q_RANK, MLA_QW).astype(BF16)
    wkv = w_kv_up.reshape(n_odd, MLA_KV_RANK, MLA_HEADS, 2 * HD)
    wk_b = jnp.pad(wkv[..., :HD], ((0, 0), (0, 0), (0, 0), (0, LANES - HD))).reshape(
        n_odd, MLA_KV_RANK, MLA_QW).astype(BF16)
    wv_b = wkv[..., HD:].reshape(n_odd, MLA_KV_RANK, D).astype(BF16)
    w_router_b = jnp.pad(w_router, ((0, 0), (0, 0), (0, LANES - N_EXPERTS))).astype(BF16)
    even_tabs = _rope_tables(dec_seq, 16, _even_lane)
    mla_tabs = _rope_tables(dec_seq, 8, _mla_lane)
    ckr = jnp.zeros((dec_batch, n_odd, past, LANES), F32).at[..., 64:96].set(cache_mla_krope)
    even_caches = (cache_diff_k, cache_diff_v, cache_swa_k, cache_swa_v)

    mix_gains = jnp.concatenate([norm_mix, final_norm[None]], axis=0).reshape(depth + 1, 1, D)
    ffn_gains = norm_ffn.reshape(depth, 1, D)
    q_gains = mla_q_norm.reshape(n_odd, 1, MLA_Q_RANK)
    kv_gains = mla_kv_norm.reshape(n_odd, 1, MLA_KV_RANK)
    sublns = diff_subln.reshape(n_even, 1, 2 * HD)

    xa, xb, xb_off = x_prompt.reshape(nc, D), x_sample.reshape(ns, D), 0
    h = _norm_mod_call(xa, xb, mods, 0, mix_gains, modrow)
    kd_new, vd_new, ks_new, vs_new, ckv_odd, kr_odd = [], [], [], [], [], []
    y_prompt = y_sample = None
    for i in range(depth):
        j = i // 2
        if i % 2 == 0:
            qkv_c, kd, vd, ks, vs = _proj_even_call(h, w_even_b, j, 0, nc, None, dec_seq)
            (qkv_l,) = _proj_even_call(h, w_even_b, j, nc, ns, even_tabs, dec_seq)
            kd_new.append(kd.reshape(batch, seq, DIFF_HEADS, 2 * HD))
            vd_new.append(vd.reshape(batch, seq, DIFF_HEADS, 2 * HD))
            ks_new.append(ks.reshape(batch, seq, 2, HD))
            vs_new.append(vs.reshape(batch, seq, 2, HD))
            li = _lambda_init(i)
            o_c = _attn_even_call(qkv_c, None, j, diff_lambda, sublns, swa_sink, batch, seq, 0, li)
            o_l = _attn_even_call(qkv_l, even_caches, j, diff_lambda, sublns, swa_sink,
                                  dec_batch, dec_seq, past, li)
            w_out = w_oute_b
        else:
            q_c, ckv_c, kr_c = _proj_odd_call(h, w_odd_b, q_gains, wq_b, kv_gains, j, 0, nc, None,
                                              dec_seq)
            q_l, ckv_l, kr_l = _proj_odd_call(h, w_odd_b, q_gains, wq_b, kv_gains, j, nc, ns,
                                              mla_tabs, dec_seq)
            ckv_odd.append(ckv_c.reshape(batch, seq, MLA_KV_RANK))
            kr_odd.append(kr_c.reshape(batch, seq, LANES)[..., 64:96])
            o_c = _attn_odd_call(q_c, ckv_c, kr_c, None, j, wk_b, wv_b, batch, seq, 0)
            o_l = _attn_odd_call(q_l, ckv_l, kr_l, (cache_mla_ckv, ckr), j, wk_b, wv_b,
                                 dec_batch, dec_seq, past)
            w_out = w_outo_b
        x, h2, aff = _outproj_call(xa, xb, xb_off, o_c, o_l, w_out, j, mods, i, ffn_gains,
                                   w_router_b, modrow)
        rowid, gate, tab = _select_call(aff, 2, ntok)
        xe = _dispatch_call(tab, h2, rowid, 2, ntok)
        y = _ffn_call(tab, xe, w_gate_exp, w_up_exp, w_down_exp, i, 2, ntok)
        final = i == depth - 1
        nxt = i if final else i + 1
        out_a, out_b = _combine_call(tab, y, rowid, gate, x, mods, i, nxt, mix_gains,
                                     depth if final else nxt, modrow_tb, 2, ntok, final)
        if final:
            y_prompt = out_a.reshape(batch, seq, D)
            y_sample = out_b.reshape(dec_batch, dec_seq, D)
        else:
            x, h = out_a, out_b
            xa, xb, xb_off = x, x, nc // TM

    return (y_prompt, y_sample, jnp.stack(kd_new, axis=1), jnp.stack(vd_new, axis=1),
            jnp.stack(ks_new, axis=1), jnp.stack(vs_new, axis=1), jnp.stack(ckv_odd, axis=1),
            jnp.stack(kr_odd, axis=1))
```

```python
import functools
import math

import jax
import jax.numpy as jnp
from jax import lax
from jax.experimental import pallas as pl
from jax.experimental.pallas import tpu as pltpu

F32 = jnp.float32
BF16 = jnp.bfloat16
I32 = jnp.int32

D = 1024
HD = 64
GRID_W = 64
WINDOW = 128
DIFF_HEADS = 4
SWA_HEADS = 8
MLA_HEADS = 16
MLA_Q_RANK = 384
MLA_KV_RANK = 256
MLA_ROPE = 32
N_EXPERTS = 16
EXPERT_FF = 512
EC_FACTOR = 2
ROPE_BASE = 10000.0
EPS = 1e-6
NEG_INF = -1e30
LOG2E = math.log2(math.e)
EVEN_IN = 2304
LANES = 128
TM = 512
TQ = 256
TB = 256
SEG = 16
VMEM_LIMIT = 56 * 1024 * 1024


def _cparams(sem, vmem=VMEM_LIMIT):
    return pltpu.CompilerParams(dimension_semantics=sem, vmem_limit_bytes=vmem)


def _dot(a, b):
    return jnp.dot(a, b, preferred_element_type=F32)


def _dot_nt(a, b):
    return lax.dot_general(a, b, (((1,), (1,)), ((), ())), preferred_element_type=F32)


def _silu(x):
    return x / (1.0 + jnp.exp(-x))


def _rms(x, g):
    ms = jnp.mean(x * x, axis=-1, keepdims=True)
    return x * lax.rsqrt(ms + EPS) * g


def _modulate(x, g, shift, scale):
    return _rms(x, g) * (1.0 + scale) + shift


def _lambda_init(layer):
    return 0.8 - 0.6 * math.exp(-0.3 * layer)


def _ada_kernel(c_ref, w_ref, b_ref, o_ref):
    s = _silu(c_ref[...]).astype(BF16)
    o_ref[...] = _dot(s, w_ref[...].astype(BF16)) + b_ref[...]


def _ada_call(cvec, w_ada, b_ada):
    depth, _, n6 = w_ada.shape
    rc = cvec.shape[0]
    tn = 512
    return pl.pallas_call(
        _ada_kernel,
        grid=(depth, n6 // tn),
        in_specs=[pl.BlockSpec((rc, D), lambda i, n: (0, 0)),
                  pl.BlockSpec((None, D, tn), lambda i, n: (i, 0, n)),
                  pl.BlockSpec((None, 1, tn), lambda i, n: (i, 0, n))],
        out_specs=pl.BlockSpec((None, rc, tn), lambda i, n: (i, 0, n)),
        out_shape=jax.ShapeDtypeStruct((depth, rc, n6), F32),
        compiler_params=_cparams(("parallel", "parallel")),
        name="ada",
    )(cvec, w_ada, b_ada.reshape(depth, 1, n6))


def _two_part_specs(rows_a, rows_b, width, off_b=0):
    na, nb_ = rows_a // TM, rows_b // TM
    return (pl.BlockSpec((TM, width), lambda r: (jnp.minimum(r, na - 1), 0)),
            pl.BlockSpec((TM, width), lambda r: (off_b + jnp.clip(r - na, 0, nb_ - 1), 0)))


def _norm_mod_kernel(xa_ref, xb_ref, mod_ref, g_ref, h_ref, *, ntile_a):
    def run(x_ref):
        h_ref[...] = _modulate(x_ref[...], g_ref[...], mod_ref[0], mod_ref[1]).astype(BF16)

    @pl.when(pl.program_id(0) < ntile_a)
    def _():
        run(xa_ref)

    @pl.when(pl.program_id(0) >= ntile_a)
    def _():
        run(xb_ref)


def _norm_mod_call(xa, xb, mods, layer, gains, modrow):
    na, nb_ = xa.shape[0], xb.shape[0]
    return pl.pallas_call(
        functools.partial(_norm_mod_kernel, ntile_a=na // TM),
        grid=((na + nb_) // TM,),
        in_specs=[*_two_part_specs(na, nb_, D),
                  pl.BlockSpec((None, None, 6, 1, D), lambda r: (layer, modrow(r), 0, 0, 0)),
                  pl.BlockSpec((None, 1, D), lambda r: (layer, 0, 0))],
        out_specs=pl.BlockSpec((TM, D), lambda r: (r, 0)),
        out_shape=jax.ShapeDtypeStruct((na + nb_, D), BF16),
        compiler_params=_cparams(("parallel",)),
        name="norm_mod",
    )(xa, xb, mods, gains)


def _rope_tables(dec_seq, half, lane_of_dim):
    pos = jnp.arange(dec_seq)
    row = (pos // GRID_W).astype(F32)
    col = (pos % GRID_W).astype(F32)
    inv = ROPE_BASE ** (-(jnp.arange(half, dtype=F32) / half))
    ang = jnp.stack([row[:, None] * inv[None, :], col[:, None] * inv[None, :]])
    cos, sin = jnp.cos(ang), jnp.sin(ang)
    c_cols, s1_cols, s2_cols = [], [], []
    one, zero = jnp.ones((dec_seq,), F32), jnp.zeros((dec_seq,), F32)
    for lane in range(LANES):
        info = lane_of_dim(lane)
        if info is None:
            c_cols.append(one); s1_cols.append(zero); s2_cols.append(zero)
            continue
        axis, k, second = info
        c_cols.append(cos[axis, :, k])
        if second:
            s1_cols.append(zero); s2_cols.append(sin[axis, :, k])
        else:
            s1_cols.append(-sin[axis, :, k]); s2_cols.append(zero)
    return (jnp.stack(c_cols, axis=1), jnp.stack(s1_cols, axis=1), jnp.stack(s2_cols, axis=1))


def _even_lane(lane):
    j = lane % HD
    axis, jj = j // 32, j % 32
    return axis, jj % 16, jj >= 16


def _mla_lane(lane):
    if lane < 64 or lane >= 96:
        return None
    jj = lane - 64
    axis, k = jj // 16, jj % 16
    return axis, k % 8, k >= 8


def _rope(x, c, s1, s2, shift):
    return x * c + pltpu.roll(x, LANES - shift, 1) * s1 + pltpu.roll(x, shift, 1) * s2


_EVEN_ROPE_TILES = tuple(range(0, 8)) + tuple(range(12, 17))
_EVEN_Q_TILES = tuple(range(0, 4)) + tuple(range(12, 16))


def _append_layer(prev_refs, out_refs, new_values):
    for i, (out_ref, new) in enumerate(zip(out_refs, new_values)):
        nbatch, nlayers, seq, width = out_ref.shape
        if prev_refs:
            out_ref[:, 0:nlayers - 1] = prev_refs[i][...]
        out_ref[:, nlayers - 1] = new.reshape(nbatch, seq, width)


def _state_specs(prev, widths, seq):
    nlayers = 1 if prev is None else prev[0].shape[1] + 1
    per = TM // seq
    ins = [] if prev is None else [pl.BlockSpec((per, nlayers - 1, seq, w), lambda r: (r, 0, 0, 0))
                                   for w in widths]
    outs = [pl.BlockSpec((per, nlayers, seq, w), lambda r: (r, 0, 0, 0)) for w in widths]
    return ins, outs, nlayers


def _proj_even_kernel(*refs, rope, caches, nprev):
    h_ref, w_ref = refs[0], refs[1]
    pos = 2
    if rope:
        c_ref, s1_ref, s2_ref = refs[2:5]
        pos = 5
    prev_refs = refs[pos:pos + nprev]
    pos += nprev
    qkv_ref = refs[pos]
    res = _dot(h_ref[...], w_ref[...])
    scale = HD ** -0.5 * LOG2E
    for t in range(EVEN_IN // LANES):
        x = res[:, t * LANES:(t + 1) * LANES]
        if rope and t in _EVEN_ROPE_TILES:
            x = _rope(x, c_ref[...], s1_ref[...], s2_ref[...], 16)
        if t in _EVEN_Q_TILES:
            x = x * scale
        qkv_ref[:, t * LANES:(t + 1) * LANES] = x.astype(BF16)
    if caches:
        _append_layer(prev_refs, refs[pos + 1:pos + 5],
                      (res[:, 512:1024], res[:, 1024:1536], res[:, 2048:2176], res[:, 2176:2304]))


def _proj_even_call(h, w, j, row0, nrows, tables, dec_seq, prev=None, seq=None):
    rope = tables is not None
    caches = not rope
    t0 = row0 // TM
    in_specs = [pl.BlockSpec((TM, D), lambda r: (t0 + r, 0)),
                pl.BlockSpec((None, D, EVEN_IN), lambda r: (j, 0, 0))]
    args = [h, w]
    if rope:
        per = dec_seq // TM
        for _ in range(3):
            in_specs.append(pl.BlockSpec((TM, LANES), lambda r: (r % per, 0)))
        args += list(tables)
    out_specs = [pl.BlockSpec((TM, EVEN_IN), lambda r: (r, 0))]
    out_shape = [jax.ShapeDtypeStruct((nrows, EVEN_IN), BF16)]
    nprev = 0
    if caches:
        widths = (512, 512, LANES, LANES)
        ins, outs, nlayers = _state_specs(prev, widths, seq)
        nprev = len(ins)
        in_specs += ins
        args += [] if prev is None else list(prev)
        out_specs += outs
        out_shape += [jax.ShapeDtypeStruct((nrows // seq, nlayers, seq, wd), F32) for wd in widths]
    return pl.pallas_call(
        functools.partial(_proj_even_kernel, rope=rope, caches=caches, nprev=nprev),
        grid=(nrows // TM,),
        in_specs=in_specs, out_specs=out_specs, out_shape=out_shape,
        compiler_params=_cparams(("parallel",)),
        name="proj_even_lat" if rope else "proj_even_ctx",
    )(*args)


ODD_IN_PAD = MLA_Q_RANK + MLA_KV_RANK + LANES
MLA_QW = MLA_HEADS * LANES


def _proj_odd_kernel(*refs, rope, nprev):
    h_ref, w_ref, qn_ref, wq_ref, kvn_ref = refs[:5]
    pos = 5
    if rope:
        c_ref, s1_ref, s2_ref = refs[5:8]
        pos = 8
    prev_refs = refs[pos:pos + nprev]
    pos += nprev
    q_ref, ckv_ref, kr_ref = refs[pos:pos + 3]
    res = _dot(h_ref[...], w_ref[...])
    cq = _rms(res[:, :MLA_Q_RANK], qn_ref[...]).astype(BF16)
    ckv = _rms(res[:, MLA_Q_RANK:MLA_Q_RANK + MLA_KV_RANK], kvn_ref[...])
    kr = res[:, MLA_Q_RANK + MLA_KV_RANK:]
    if rope:
        kr = _rope(kr, c_ref[...], s1_ref[...], s2_ref[...], 8)
        ckv_ref[...] = ckv
        kr_ref[...] = kr
    else:
        _append_layer(prev_refs, (ckv_ref, kr_ref), (ckv, kr))
    q = _dot(cq, wq_ref[...])
    scale = (HD + MLA_ROPE) ** -0.5 * LOG2E
    for t in range(MLA_HEADS):
        x = q[:, t * LANES:(t + 1) * LANES]
        if rope:
            x = _rope(x, c_ref[...], s1_ref[...], s2_ref[...], 8)
        q_ref[:, t * LANES:(t + 1) * LANES] = (x * scale).astype(BF16)


def _proj_odd_call(h, w_in, qn, wq, kvn, j, row0, nrows, tables, dec_seq, prev=None, seq=None):
    rope = tables is not None
    t0 = row0 // TM
    in_specs = [pl.BlockSpec((TM, D), lambda r: (t0 + r, 0)),
                pl.BlockSpec((None, D, ODD_IN_PAD), lambda r: (j, 0, 0)),
                pl.BlockSpec((None, 1, MLA_Q_RANK), lambda r: (j, 0, 0)),
                pl.BlockSpec((None, MLA_Q_RANK, MLA_QW), lambda r: (j, 0, 0)),
                pl.BlockSpec((None, 1, MLA_KV_RANK), lambda r: (j, 0, 0))]
    args = [h, w_in, qn, wq, kvn]
    if rope:
        per = dec_seq // TM
        for _ in range(3):
            in_specs.append(pl.BlockSpec((TM, LANES), lambda r: (r % per, 0)))
        args += list(tables)
    out_specs = [pl.BlockSpec((TM, MLA_QW), lambda r: (r, 0))]
    out_shape = [jax.ShapeDtypeStruct((nrows, MLA_QW), BF16)]
    widths = (MLA_KV_RANK, LANES)
    nprev = 0
    if rope:
        out_specs += [pl.BlockSpec((TM, wd), lambda r: (r, 0)) for wd in widths]
        out_shape += [jax.ShapeDtypeStruct((nrows, wd), F32) for wd in widths]
    else:
        ins, outs, nlayers = _state_specs(prev, widths, seq)
        nprev = len(ins)
        in_specs += ins
        args += [] if prev is None else list(prev)
        out_specs += outs
        out_shape += [jax.ShapeDtypeStruct((nrows // seq, nlayers, seq, wd), F32) for wd in widths]
    return pl.pallas_call(
        functools.partial(_proj_odd_kernel, rope=rope, nprev=nprev),
        grid=(nrows // TM,),
        in_specs=in_specs,
        out_specs=out_specs,
        out_shape=out_shape,
        compiler_params=_cparams(("parallel",)),
        name="proj_odd_lat" if rope else "proj_odd_ctx",
    )(*args)


def _ones_lane(half):
    return HD if half == 0 else 0


def _row_sum(e, o, half, from_matmul):
    if from_matmul:
        one = _ones_lane(half)
        return o[:, one:one + 1]
    return jnp.sum(e, axis=-1, keepdims=True)


def _half_values(v, half):
    lane = lax.broadcasted_iota(I32, (1, LANES), 1)
    keep = (lane < HD) if half == 0 else (lane >= HD)
    return jnp.where(keep, v, jnp.where(lane == _ones_lane(half), 1.0, 0.0)).astype(BF16)


def _attn_even_kernel(*refs, seq, past, lam_init, layer):
    latent = past > 0
    n = past + seq
    if latent:
        (qkv_ref, ck_ref, cv_ref, sk_ref, sv_ref, lam_ref, subln_ref, sink_ref,
         o_ref, kd, vd, ka, vl, vh) = refs
    else:
        qkv_ref, lam_ref, subln_ref, sink_ref, o_ref, kd, vd, ka, vl, vh = refs
    qi = pl.program_id(1)
    lo = lax.broadcasted_iota(I32, (1, LANES), 1) < HD

    @pl.when(qi == 0)
    def _build():
        chunk = 256
        for c0 in range(0, n, chunk):
            rows = slice(c0, c0 + chunk)
            if c0 < past:
                prow = slice(c0, c0 + chunk)
                for h in range(DIFF_HEADS):
                    kd[rows, h * LANES:(h + 1) * LANES] = ck_ref[prow, h, :].astype(BF16)
                    vd[rows, h * LANES:(h + 1) * LANES] = cv_ref[prow, h, :].astype(BF16)
                kt = jnp.concatenate([sk_ref[prow, 0, :], sk_ref[prow, 1, :]], axis=1)
                vt = jnp.concatenate([sv_ref[prow, 0, :], sv_ref[prow, 1, :]], axis=1)
            else:
                orow = slice(c0 - past, c0 - past + chunk)
                kd[rows, :] = qkv_ref[orow, 512:1024]
                vd[rows, :] = qkv_ref[orow, 1024:1536]
                kt = qkv_ref[orow, 2048:2176].astype(F32)
                vt = qkv_ref[orow, 2176:2304].astype(F32)
            kr = pltpu.roll(kt, HD, 1)
            vr = pltpu.roll(vt, HD, 1)
            ka[0, rows, :] = jnp.where(lo, kt, kr).astype(BF16)
            ka[1, rows, :] = jnp.where(lo, kr, kt).astype(BF16)
            vl[0, rows, :] = _half_values(vt, 0)
            vh[0, rows, :] = _half_values(vr, 1)
            vl[1, rows, :] = _half_values(vr, 0)
            vh[1, rows, :] = _half_values(vt, 1)

    r0 = pl.multiple_of(qi * TQ, TQ)
    lam = lam_ref[...]
    lam_full = (jnp.exp(jnp.sum(lam[0:1] * lam[1:2], axis=-1, keepdims=True))
                - jnp.exp(jnp.sum(lam[2:3] * lam[3:4], axis=-1, keepdims=True)) + lam_init)
    zero_b = jnp.zeros((), BF16)

    for h in range(DIFF_HEADS):
        cs = slice(h * LANES, (h + 1) * LANES)
        qt = qkv_ref[pl.ds(r0, TQ), cs]
        kh = kd[:, cs]
        es, rs = [], []
        for comp in range(2):
            qc = jnp.where(lo, qt, zero_b) if comp == 0 else jnp.where(lo, zero_b, qt)
            s = _dot_nt(qc, kh)
            m = jnp.max(s, axis=-1, keepdims=True)
            e = jnp.exp2(s - m)
            es.append(e)
            rs.append(1.0 / jnp.sum(e, axis=-1, keepdims=True))
        a = es[0] * rs[0] - es[1] * (lam_full * rs[1])
        o = _dot(a.astype(BF16), vd[:, cs])
        o = _rms(o, subln_ref[...]) * (1.0 - lam_init)
        o_ref[:, cs] = o.astype(BF16)

    nblk = seq // WINDOW
    dense = past if latent else seq
    if latent:
        per = TQ // WINDOW
        offsets = tuple(range(-1, per + 1))
        rr = lax.broadcasted_iota(I32, (TQ, WINDOW), 0)
        cc = lax.broadcasted_iota(I32, (TQ, WINDOW), 1)
        band, starts = {}, {}
        for d in offsets:
            blk = qi * per + d
            inside = jnp.logical_and(blk >= 0, blk < nblk)
            band[d] = jnp.logical_and(jnp.abs(rr - cc - d * WINDOW) <= WINDOW, inside)
            starts[d] = pl.multiple_of(past + jnp.clip(blk, 0, nblk - 1) * WINDOW, WINDOW)
    for i in range(SWA_HEADS // 2):
        hk = i // 2
        cs = slice(1536 + i * LANES, 1536 + (i + 1) * LANES)
        qt = qkv_ref[pl.ds(r0, TQ), cs]
        halves = []
        for half in range(2):
            qc = jnp.where(lo, qt, zero_b) if half == 0 else jnp.where(lo, zero_b, qt)
            vsel = vl if half == 0 else vh
            sink = sink_ref[layer, 2 * i + half] * LOG2E
            parts = [_dot_nt(qc, ka[hk, 0:dense, :])]
            if latent:
                for d in offsets:
                    s = _dot_nt(qc, ka[hk, pl.ds(starts[d], WINDOW), :])
                    parts.append(jnp.where(band[d], s, NEG_INF))
            s_all = jnp.concatenate(parts, axis=1) if len(parts) > 1 else parts[0]
            m = jnp.maximum(jnp.max(s_all, axis=-1, keepdims=True), sink)
            e = jnp.exp2(s_all - m)
            eb = e.astype(BF16)
            o = _dot(eb[:, 0:dense], vsel[hk, 0:dense, :])
            if latent:
                for k, d in enumerate(offsets):
                    o += _dot(eb[:, dense + k * WINDOW:dense + (k + 1) * WINDOW],
                              vsel[hk, pl.ds(starts[d], WINDOW), :])
            den = _row_sum(e, o, half, from_matmul=latent) + jnp.exp2(sink - m)
            halves.append(o * (1.0 / den))
        o_ref[:, 512 + i * LANES:512 + (i + 1) * LANES] = jnp.where(lo, halves[0], halves[1]).astype(BF16)


def _attn_even_call(qkv, caches, j, lam, subln, sink, nbatch, seq, past, lam_init):
    n = past + seq
    latent = past > 0
    in_specs = [pl.BlockSpec((seq, EVEN_IN), lambda b, q: (b, 0))]
    args = [qkv]
    if latent:
        ck, cv, sk, sv = caches
        in_specs += [pl.BlockSpec((None, None, past, DIFF_HEADS, 2 * HD), lambda b, q: (b, j, 0, 0, 0)),
                     pl.BlockSpec((None, None, past, DIFF_HEADS, 2 * HD), lambda b, q: (b, j, 0, 0, 0)),
                     pl.BlockSpec((None, None, past, 2, HD), lambda b, q: (b, j, 0, 0, 0)),
                     pl.BlockSpec((None, None, past, 2, HD), lambda b, q: (b, j, 0, 0, 0))]
        args += [ck, cv, sk, sv]
    in_specs += [pl.BlockSpec((None, 4, HD), lambda b, q: (j, 0, 0)),
                 pl.BlockSpec((None, 1, 2 * HD), lambda b, q: (j, 0, 0)),
                 pl.BlockSpec(memory_space=pltpu.SMEM)]
    args += [lam, subln, sink]
    return pl.pallas_call(
        functools.partial(_attn_even_kernel, seq=seq, past=past, lam_init=lam_init, layer=j),
        grid=(nbatch, seq // TQ),
        in_specs=in_specs,
        out_specs=pl.BlockSpec((TQ, D), lambda b, q: (b * (seq // TQ) + q, 0)),
        out_shape=jax.ShapeDtypeStruct((nbatch * seq, D), BF16),
        scratch_shapes=[pltpu.VMEM((n, 512), BF16), pltpu.VMEM((n, 512), BF16),
                        pltpu.VMEM((2, n, LANES), BF16), pltpu.VMEM((2, n, LANES), BF16),
                        pltpu.VMEM((2, n, LANES), BF16)],
        compiler_params=_cparams(("arbitrary", "arbitrary")),
        name="attn_even_lat" if latent else "attn_even_ctx",
    )(*args)


def _attn_odd_kernel(*refs, seq, past):
    latent = past > 0
    n = past + seq
    if latent:
        q_ref, ckv_ref, kr_ref, cckv_ref, ckr_ref, wk_ref, wv_ref, o_ref, kf, vlo, vhi = refs
    else:
        q_ref, ckv_ref, kr_ref, wk_ref, wv_ref, o_ref, kf, vlo, vhi = refs
    qi = pl.program_id(1)
    lo = lax.broadcasted_iota(I32, (1, LANES), 1) < HD

    @pl.when(qi == 0)
    def _build():
        chunk = 256
        for c0 in range(0, n, chunk):
            rows = slice(c0, c0 + chunk)
            if c0 < past:
                ckv = cckv_ref[c0:c0 + chunk, :].astype(BF16)
                kr = ckr_ref[c0:c0 + chunk, :]
            else:
                ckv = ckv_ref[c0 - past:c0 - past + chunk, :].astype(BF16)
                kr = kr_ref[c0 - past:c0 - past + chunk, :]
            kk = _dot(ckv, wk_ref[...])
            for h in range(MLA_HEADS):
                cs = slice(h * LANES, (h + 1) * LANES)
                kf[rows, cs] = (kk[:, cs] + kr).astype(BF16)
            vv = _dot(ckv, wv_ref[...])
            for i in range(MLA_HEADS // 2):
                cs = slice(i * LANES, (i + 1) * LANES)
                vlo[rows, cs] = _half_values(vv[:, cs], 0)
                vhi[rows, cs] = _half_values(vv[:, cs], 1)

    r0 = pl.multiple_of(qi * TQ, TQ)
    lo = lax.broadcasted_iota(I32, (1, LANES), 1) < HD
    for i in range(MLA_HEADS // 2):
        halves = []
        for half in range(2):
            h = 2 * i + half
            cs = slice(h * LANES, (h + 1) * LANES)
            s = _dot_nt(q_ref[pl.ds(r0, TQ), cs], kf[:, cs])
            m = jnp.max(s, axis=-1, keepdims=True)
            vsel = vlo if half == 0 else vhi
            e = jnp.exp2(s - m)
            o = _dot(e.astype(BF16), vsel[:, i * LANES:(i + 1) * LANES])
            halves.append(o * (1.0 / _row_sum(e, o, half, from_matmul=latent)))
        o_ref[:, i * LANES:(i + 1) * LANES] = jnp.where(lo, halves[0], halves[1]).astype(BF16)


def _attn_odd_call(q, ckv, kr, caches, j, wk, wv, nbatch, seq, past):
    n = past + seq
    latent = past > 0
    in_specs = [pl.BlockSpec((seq, MLA_QW), lambda b, qq: (b, 0))]
    if latent:
        in_specs += [pl.BlockSpec((seq, MLA_KV_RANK), lambda b, qq: (b, 0)),
                     pl.BlockSpec((seq, LANES), lambda b, qq: (b, 0))]
    else:
        last = ckv.shape[1] - 1
        in_specs += [pl.BlockSpec((None, None, seq, MLA_KV_RANK), lambda b, qq: (b, last, 0, 0)),
                     pl.BlockSpec((None, None, seq, LANES), lambda b, qq: (b, last, 0, 0))]
    args = [q, ckv, kr]
    if latent:
        in_specs += [pl.BlockSpec((None, None, past, MLA_KV_RANK), lambda b, qq: (b, j, 0, 0)),
                     pl.BlockSpec((None, None, past, LANES), lambda b, qq: (b, j, 0, 0))]
        args += list(caches)
    in_specs += [pl.BlockSpec((None, MLA_KV_RANK, MLA_QW), lambda b, qq: (j, 0, 0)),
                 pl.BlockSpec((None, MLA_KV_RANK, D), lambda b, qq: (j, 0, 0))]
    args += [wk, wv]
    return pl.pallas_call(
        functools.partial(_attn_odd_kernel, seq=seq, past=past),
        grid=(nbatch, seq // TQ),
        in_specs=in_specs,
        out_specs=pl.BlockSpec((TQ, D), lambda b, qq: (b * (seq // TQ) + qq, 0)),
        out_shape=jax.ShapeDtypeStruct((nbatch * seq, D), BF16),
        scratch_shapes=[pltpu.VMEM((n, MLA_QW), BF16), pltpu.VMEM((n, D), BF16),
                        pltpu.VMEM((n, D), BF16)],
        compiler_params=_cparams(("arbitrary", "arbitrary")),
        name="attn_odd_lat" if latent else "attn_odd_ctx",
    )(*args)


def _outproj_kernel(xa_ref, xb_ref, oa_ref, ob_ref, w_ref, mod_ref, g_ref, wr_ref,
                    xo_ref, h_ref, aff_ref, *, ntile_a):
    def run(x_ref, o_ref):
        x = x_ref[...] + mod_ref[2] * _dot(o_ref[...], w_ref[...])
        xo_ref[...] = x
        h = _modulate(x, g_ref[...], mod_ref[3], mod_ref[4]).astype(BF16)
        h_ref[...] = h
        logits = _dot(h, wr_ref[...])
        lane = lax.broadcasted_iota(I32, (TM, LANES), 1)
        lg = jnp.where(lane < N_EXPERTS, logits, -jnp.inf)
        e = jnp.exp(lg - jnp.max(lg, axis=-1, keepdims=True))
        aff = e / jnp.sum(e, axis=-1, keepdims=True)
        for c in range(TM // TB):
            aff_ref[c] = aff[c * TB:(c + 1) * TB, :].T[0:N_EXPERTS, :]

    @pl.when(pl.program_id(0) < ntile_a)
    def _():
        run(xa_ref, oa_ref)

    @pl.when(pl.program_id(0) >= ntile_a)
    def _():
        run(xb_ref, ob_ref)


def _outproj_call(xa, xb, xb_off, o_a, o_b, w, widx, mods, layer, gains, w_router, modrow):
    na, nb_ = o_a.shape[0], o_b.shape[0]
    t = na + nb_
    assert TM % TB == 0
    return pl.pallas_call(
        functools.partial(_outproj_kernel, ntile_a=na // TM),
        grid=(t // TM,),
        in_specs=[*_two_part_specs(na, nb_, D, xb_off), *_two_part_specs(na, nb_, D),
                  pl.BlockSpec((None, D, D), lambda r: (widx, 0, 0)),
                  pl.BlockSpec((None, None, 6, 1, D), lambda r: (layer, modrow(r), 0, 0, 0)),
                  pl.BlockSpec((None, 1, D), lambda r: (layer, 0, 0)),
                  pl.BlockSpec((None, D, LANES), lambda r: (layer, 0, 0))],
        out_specs=[pl.BlockSpec((TM, D), lambda r: (r, 0)),
                   pl.BlockSpec((TM, D), lambda r: (r, 0)),
                   pl.BlockSpec((TM // TB, N_EXPERTS, TB), lambda r: (r, 0, 0))],
        out_shape=[jax.ShapeDtypeStruct((t, D), F32), jax.ShapeDtypeStruct((t, D), BF16),
                   jax.ShapeDtypeStruct((t // TB, N_EXPERTS, TB), F32)],
        compiler_params=_cparams(("parallel",)),
        name="outproj",
    )(xa, xb, o_a, o_b, w, mods, gains, w_router)


def _select_kernel(aff_ref, rowid_ref, gate_ref, tab_ref, *, nb, cap):
    ne = N_EXPERTS
    nr = nb * ne
    a = aff_ref[...].reshape(nr, TB)
    ri = lax.broadcasted_iota(I32, (nr, nr), 0)
    ci = lax.broadcasted_iota(I32, (nr, nr), 1)
    same_e = (ri & (ne - 1)) == (ci & (ne - 1))
    same_b = (ri >> 4) == (ci >> 4)
    m_e = jnp.where(same_e, 1.0, 0.0).astype(BF16)
    m_b = jnp.where(same_b, 1.0, 0.0).astype(BF16)
    m_a = jnp.where(jnp.logical_and(same_e, ci < ri), 1.0, 0.0).astype(BF16)
    m_o = jnp.where(jnp.logical_and(same_b, ci < ri), 1.0, 0.0).astype(BF16)
    ui = lax.broadcasted_iota(I32, (TB, TB), 0)
    uj = lax.broadcasted_iota(I32, (TB, TB), 1)
    upper = jnp.where(ui < uj, 1.0, 0.0).astype(BF16)

    def rows_to_lanes(col):
        return jnp.broadcast_to(col, (nr, LANES)).astype(BF16)

    wide = jnp.concatenate([aff_ref[b] for b in range(nb)], axis=1)

    def count_ge(value):
        return jnp.sum(jnp.where(wide >= value, 1.0, 0.0), axis=-1, keepdims=True)

    def bisect(i, v):
        cand = v | jnp.left_shift(jnp.int32(1), 30 - i)
        return jnp.where(count_ge(pltpu.bitcast(cand, F32)) >= cap, cand, v)

    thr = lax.fori_loop(0, 31, bisect, jnp.zeros((ne, 1), I32))

    def refine(i, lohi):
        lo_e, hi_e = lohi
        mid = 0.5 * (lo_e + hi_e)
        take = count_ge(mid) >= cap
        return jnp.where(take, mid, lo_e), jnp.where(take, hi_e, mid)

    lo_e, hi_e = lax.fori_loop(0, TIE_STEPS, refine,
                               (pltpu.bitcast(thr, F32), pltpu.bitcast(thr + 1, F32)))
    lo_v = jnp.concatenate([lo_e] * nb, axis=0)
    hi_v = jnp.concatenate([hi_e] * nb, axis=0)
    gt = jnp.where(a >= hi_v, 1.0, 0.0)
    eq = jnp.where(jnp.logical_and(a >= lo_v, a < hi_v), 1.0, 0.0)
    n_gt = _dot(m_e, rows_to_lanes(jnp.sum(gt, axis=-1, keepdims=True)))[:, 0:1]
    need = cap - n_gt
    eq_before = (_dot(m_a, rows_to_lanes(jnp.sum(eq, axis=-1, keepdims=True)))[:, 0:1]
                 + _dot(eq.astype(BF16), upper))
    sel = jnp.where(jnp.logical_and(eq > 0.0, eq_before < need), 1.0, gt)
    local = _dot(sel.astype(BF16), upper)
    cnt = jnp.sum(sel, axis=-1, keepdims=True)
    seg = jnp.floor((cnt + (SEG - 1)) * (1.0 / SEG)) * SEG
    segb = rows_to_lanes(seg)
    over = jnp.maximum(seg - WINR, 0.0)
    overb = rows_to_lanes(over)
    off_over = FIRST_ROWS + _dot(m_o, overb)[:, 0:1]
    off_buf = _dot(m_a, segb)[:, 0:1]
    over_blk = _dot(m_b, overb)[:, 0:1]
    rows_exp = _dot(m_e, segb)[:, 0:1]
    tiles_exp = jnp.floor((rows_exp + (TF - 1)) / TF)
    expert = (lax.broadcasted_iota(I32, (nr, 1), 0) & (ne - 1)).astype(F32)
    row = jnp.where(local < WINR, expert * WINR + local, off_over + local - WINR)
    rowid_ref[...] = jnp.where(sel > 0.0, row, -1.0).astype(I32).reshape(nb, ne, TB)
    gate_ref[...] = jnp.where(sel > 0.0, a, 0.0).reshape(nb, ne, TB)
    tl = lax.broadcasted_iota(I32, (nr, LANES), 1)
    tab = jnp.where(tl == 0, seg, jnp.where(tl == 1, off_over, jnp.where(
        tl == 2, off_buf, jnp.where(tl == 3, over_blk, jnp.where(tl == 4, rows_exp, tiles_exp)))))
    tab_ref[...] = tab.T[0:8, :].astype(I32)


def _select_call(aff, ngroups, ntok):
    nb = ntok // TB
    cap = EC_FACTOR * ntok // N_EXPERTS
    nr = nb * N_EXPERTS
    return pl.pallas_call(
        functools.partial(_select_kernel, nb=nb, cap=cap),
        grid=(ngroups,),
        in_specs=[pl.BlockSpec((nb, N_EXPERTS, TB), lambda g: (g, 0, 0))],
        out_specs=[pl.BlockSpec((None, nb, N_EXPERTS, TB), lambda g: (g, 0, 0, 0)),
                   pl.BlockSpec((None, nb, N_EXPERTS, TB), lambda g: (g, 0, 0, 0)),
                   pl.BlockSpec((None, 8, nr), lambda g: (g, 0, 0))],
        out_shape=[jax.ShapeDtypeStruct((ngroups, nb, N_EXPERTS, TB), I32),
                   jax.ShapeDtypeStruct((ngroups, nb, N_EXPERTS, TB), F32),
                   jax.ShapeDtypeStruct((ngroups, 8, nr), I32)],
        compiler_params=_cparams(("arbitrary",)),
        name="select",
    )(aff)


STACK_ROWS = N_EXPERTS * TB


FIRST_ROWS = 3 * TB


WINR = 3 * SEG
assert N_EXPERTS * WINR == FIRST_ROWS
TF = 672
TIE_STEPS = 12


def _window_rows(rowid_ref, dst, value_ref=None):
    for e in range(N_EXPERTS):
        rid_e = rowid_ref[e:e + 1, :]
        val_e = 1.0 if value_ref is None else value_ref[e:e + 1, :]
        hit = rid_e == lax.broadcasted_iota(I32, (WINR, TB), 0) + e * WINR
        dst[e * WINR:(e + 1) * WINR, :] = jnp.where(hit, val_e, 0.0).astype(dst.dtype)


class _Table:
    def __init__(self, tab_s, row, nb, per_block=False):
        self.tab_s, self.row, self.per = tab_s, row, nb if per_block else nb * N_EXPERTS
        self.scale = N_EXPERTS if per_block else 1

    def __getitem__(self, k):
        if self.per & (self.per - 1) == 0:
            hi, low = lax.shift_right_logical(k, self.per.bit_length() - 1), k & (self.per - 1)
        else:
            hi, low = lax.div(k, self.per), lax.rem(k, self.per)
        return self.tab_s[hi, self.row, low * self.scale]


def _tables(tab_s, nb):
    return (_Table(tab_s, 0, nb), _Table(tab_s, 1, nb), _Table(tab_s, 2, nb),
            _Table(tab_s, 3, nb, per_block=True))


def _overflow_groups(seg_s, k):
    return lax.div(jnp.maximum(seg_s[k] - WINR, 0), SEG)


def _overflow_rows(seg_s, offo_s, step, over, rowid_ref, dst, value_ref=None):
    def zero(i, carry):
        r0 = pl.multiple_of(FIRST_ROWS + i * SEG, SEG)
        dst[pl.ds(r0, SEG), :] = jnp.zeros((SEG, TB), dst.dtype)
        return carry

    lax.fori_loop(0, ((over + TB - 1) // TB) * (TB // SEG), zero, 0)
    for e in range(N_EXPERTS):
        k = step * N_EXPERTS + e
        rid_e = rowid_ref[e:e + 1, :]
        val_e = 1.0 if value_ref is None else value_ref[e:e + 1, :]

        def group(i, carry, off=offo_s[k], rid_e=rid_e, val_e=val_e):
            r0 = pl.multiple_of(off + i * SEG, SEG)
            hit = rid_e == lax.broadcasted_iota(I32, (SEG, TB), 0) + r0
            dst[pl.ds(r0, SEG), :] = jnp.where(hit, val_e, 0.0).astype(dst.dtype)
            return carry

        lax.fori_loop(0, _overflow_groups(seg_s, k), group, 0)


def _wait_rows(rows, make_copy):
    def big(i, carry):
        make_copy(TB).wait()
        return carry

    def small(i, carry):
        make_copy(SEG).wait()
        return carry

    lax.fori_loop(0, lax.div(rows, TB), big, 0)
    lax.fori_loop(0, lax.div(lax.rem(rows, TB), SEG), small, 0)


def _dispatch_kernel(tab_s, h_ref, rowid_ref, xe_hbm,
                     onehot, stack, zbuf, sem, zsem, *, nb, nsteps, cap):
    g = pl.program_id(0)
    b = pl.program_id(1)
    step = g * nb + b
    slot = lax.rem(step, 2)
    seg_s, offo_s, offb_s, over_s = _tables(tab_s, nb)
    over = over_s[step]
    xrows = xe_hbm.shape[2]

    def wait_slot(nrows, sl):
        _wait_rows(nrows, lambda n: pltpu.make_async_copy(
            stack.at[sl, pl.ds(0, n)], xe_hbm.at[0, 0, pl.ds(0, n)], sem.at[sl]))

    @pl.when(step == 0)
    def _init():
        stack[...] = jnp.zeros_like(stack)

    @pl.when(b == 0)
    def _zero_unused():
        zbuf[...] = jnp.zeros_like(zbuf)
        for e in range(N_EXPERTS):
            pltpu.make_async_copy(zbuf, xe_hbm.at[g, e, pl.ds(cap, xrows - cap)], zsem).start()

    _window_rows(rowid_ref, onehot)
    h = h_ref[...]
    stack[slot, 0:FIRST_ROWS, :] = _dot(onehot[0:FIRST_ROWS, :], h).astype(BF16)

    @pl.when(over > 0)
    def _overflow():
        _overflow_rows(seg_s, offo_s, step, over, rowid_ref, onehot)

        def chunk(c, carry):
            base = pl.multiple_of(FIRST_ROWS + c * TB, TB)
            stack[slot, pl.ds(base, TB), :] = _dot(onehot[pl.ds(base, TB), :], h).astype(BF16)
            return carry

        lax.fori_loop(0, (over + TB - 1) // TB, chunk, 0)

    @pl.when(b == 0)
    def _zero_unused_done():
        for e in range(N_EXPERTS):
            pltpu.make_async_copy(zbuf, xe_hbm.at[g, e, pl.ds(cap, xrows - cap)], zsem).wait()

    @pl.when(step >= 1)
    def _previous_landed():
        wait_slot(FIRST_ROWS + over_s[step - 1], 1 - slot)

    for e in range(N_EXPERTS):
        k = step * N_EXPERTS + e
        pltpu.make_async_copy(
            stack.at[slot, e * WINR:(e + 1) * WINR],
            xe_hbm.at[g, e, pl.ds(pl.multiple_of(offb_s[k], SEG), WINR)], sem.at[slot]).start()

    @pl.when(over > 0)
    def _overflow_copies():
        for e in range(N_EXPERTS):
            k = step * N_EXPERTS + e

            def one(i, carry, e=e, k=k):
                pltpu.make_async_copy(
                    stack.at[slot, pl.ds(pl.multiple_of(offo_s[k] + i * SEG, SEG), SEG)],
                    xe_hbm.at[g, e, pl.ds(pl.multiple_of(offb_s[k] + WINR + i * SEG, SEG), SEG)],
                    sem.at[slot]).start()
                return carry

            lax.fori_loop(0, _overflow_groups(seg_s, k), one, 0)

    @pl.when(step == nsteps - 1)
    def _drain():
        wait_slot(FIRST_ROWS + over, slot)


def _expert_rows(ntok):
    cap = EC_FACTOR * ntok // N_EXPERTS
    worst = cap + (ntok // TB) * (SEG - 1)
    tiles = -(-(worst + WINR) // TF)
    assert cap % SEG == 0 and cap >= WINR and tiles * TF > cap
    return cap, tiles


def _dispatch_call(tab, h, rowid, ngroups, ntok):
    nb = ntok // TB
    cap, tiles = _expert_rows(ntok)
    xrows = tiles * TF
    grid_spec = pltpu.PrefetchScalarGridSpec(
        num_scalar_prefetch=1,
        grid=(ngroups, nb),
        in_specs=[pl.BlockSpec((TB, D), lambda g, b, *_: (g * nb + b, 0)),
                  pl.BlockSpec((None, None, N_EXPERTS, TB), lambda g, b, *_: (g, b, 0, 0))],
        out_specs=pl.BlockSpec(memory_space=pl.ANY),
        scratch_shapes=[pltpu.VMEM((STACK_ROWS, TB), BF16), pltpu.VMEM((2, STACK_ROWS, D), BF16),
                        pltpu.VMEM((xrows - cap, D), BF16),
                        pltpu.SemaphoreType.DMA((2,)), pltpu.SemaphoreType.DMA])
    return pl.pallas_call(
        functools.partial(_dispatch_kernel, nb=nb, nsteps=ngroups * nb, cap=cap),
        grid_spec=grid_spec,
        out_shape=jax.ShapeDtypeStruct((ngroups, N_EXPERTS, xrows, D), BF16),
        compiler_params=_cparams(("arbitrary", "arbitrary")),
        name="dispatch",
    )(tab, h, rowid)


def _ffn_kernel(tab_s, xe_ref, wg_hbm, wu_hbm, wd_hbm, y_ref, wg32, wu32, wd32, wgb, wub, wdb, sem,
                *, layer):
    e = pl.program_id(0)
    g = pl.program_id(1)
    j = pl.program_id(2)

    def weight_copies(ee, sl):
        return (pltpu.make_async_copy(wg_hbm.at[layer, ee], wg32.at[sl], sem.at[sl]),
                pltpu.make_async_copy(wu_hbm.at[layer, ee], wu32.at[sl], sem.at[sl]),
                pltpu.make_async_copy(wd_hbm.at[layer, ee], wd32.at[sl], sem.at[sl]))

    @pl.when(jnp.logical_and(g == 0, j == 0))
    def _weights():
        sl = lax.rem(e, 2)

        @pl.when(e == 0)
        def _():
            for cp in weight_copies(e, sl):
                cp.start()

        @pl.when(e + 1 < N_EXPERTS)
        def _():
            for cp in weight_copies(e + 1, 1 - sl):
                cp.start()

        for cp in weight_copies(e, sl):
            cp.wait()
        wgb[...] = wg32[sl].astype(BF16)
        wub[...] = wu32[sl].astype(BF16)
        wdb[...] = wd32[sl].astype(BF16)

    live = j < tab_s[g, 5, e]

    @pl.when(live)
    def _run():
        x = xe_ref[...]
        hid = (_silu(_dot(x, wgb[...])) * _dot(x, wub[...])).astype(BF16)
        y_ref[...] = _dot(hid, wdb[...]).astype(BF16)

    @pl.when(jnp.logical_not(live))
    def _skip():
        y_ref[...] = jnp.zeros_like(y_ref)


def _ffn_call(tab, xe, wg, wu, wd, layer, ngroups, ntok):
    _, tiles = _expert_rows(ntok)

    def xmap(e, g, j, tab_s):
        return (g, e, jnp.minimum(j, tab_s[g, 5, e] - 1), 0)

    grid_spec = pltpu.PrefetchScalarGridSpec(
        num_scalar_prefetch=1,
        grid=(N_EXPERTS, ngroups, tiles),
        in_specs=[pl.BlockSpec((None, None, TF, D), xmap),
                  pl.BlockSpec(memory_space=pl.ANY), pl.BlockSpec(memory_space=pl.ANY),
                  pl.BlockSpec(memory_space=pl.ANY)],
        out_specs=pl.BlockSpec((None, None, TF, D), lambda e, g, j, nt: (g, e, j, 0)),
        scratch_shapes=[pltpu.VMEM((2, D, EXPERT_FF), F32), pltpu.VMEM((2, D, EXPERT_FF), F32),
                        pltpu.VMEM((2, EXPERT_FF, D), F32),
                        pltpu.VMEM((D, EXPERT_FF), BF16), pltpu.VMEM((D, EXPERT_FF), BF16),
                        pltpu.VMEM((EXPERT_FF, D), BF16), pltpu.SemaphoreType.DMA((2,))])
    return pl.pallas_call(
        functools.partial(_ffn_kernel, layer=layer),
        grid_spec=grid_spec,
        out_shape=jax.ShapeDtypeStruct((ngroups, N_EXPERTS, tiles * TF, D), BF16),
        compiler_params=_cparams(("arbitrary", "arbitrary", "arbitrary")),
        name="ffn",
    )(tab, xe, wg, wu, wd)


def _combine_kernel(tab_s, y_hbm, rowid_ref, gate_ref, x_ref, mod_ref,
                    modn_ref, g_ref, xo_ref, h_ref, weights, stack, acc, sem, *, nb, nsteps, final):
    g = pl.program_id(0)
    b = pl.program_id(1)
    step = g * nb + b
    slot = lax.rem(step, 2)
    seg_s, offo_s, offb_s, over_s = _tables(tab_s, nb)
    over = over_s[step]

    def fetch(st, sl):
        gg = lax.div(st, nb)
        for e in range(N_EXPERTS):
            k = st * N_EXPERTS + e
            pltpu.make_async_copy(
                y_hbm.at[gg, e, pl.ds(pl.multiple_of(offb_s[k], SEG), WINR)],
                stack.at[sl, e * WINR:(e + 1) * WINR], sem.at[sl]).start()

        @pl.when(over_s[st] > 0)
        def _():
            for e in range(N_EXPERTS):
                k = st * N_EXPERTS + e

                def one(i, carry, e=e, k=k):
                    pltpu.make_async_copy(
                        y_hbm.at[gg, e, pl.ds(pl.multiple_of(offb_s[k] + WINR + i * SEG, SEG), SEG)],
                        stack.at[sl, pl.ds(pl.multiple_of(offo_s[k] + i * SEG, SEG), SEG)],
                        sem.at[sl]).start()
                    return carry

                lax.fori_loop(0, _overflow_groups(seg_s, k), one, 0)

    @pl.when(step == 0)
    def _first():
        stack[...] = jnp.zeros_like(stack)
        fetch(step, slot)

    if nsteps > 1:
        @pl.when(step + 1 < nsteps)
        def _prefetch():
            fetch(step + 1, 1 - slot)

    _window_rows(rowid_ref, weights, gate_ref)

    @pl.when(over > 0)
    def _():
        _overflow_rows(seg_s, offo_s, step, over, rowid_ref, weights, gate_ref)

    _wait_rows(FIRST_ROWS + over, lambda n: pltpu.make_async_copy(
        y_hbm.at[0, 0, pl.ds(0, n)], stack.at[slot, pl.ds(0, n)], sem.at[slot]))

    def token_weights(base):
        return weights[pl.ds(base, TB), :].T.astype(BF16)

    w = jnp.concatenate([token_weights(c * TB) for c in range(FIRST_ROWS // TB)], axis=1)
    acc[...] = _dot(w, stack[slot, 0:FIRST_ROWS, :])

    @pl.when(over > 0)
    def _():
        def chunk(c, carry):
            base = pl.multiple_of(FIRST_ROWS + c * TB, TB)
            acc[...] += _dot(token_weights(base), stack[slot, pl.ds(base, TB), :])
            return carry

        lax.fori_loop(0, (over + TB - 1) // TB, chunk, 0)

    x = x_ref[...] + mod_ref[5] * acc[...]
    if final:
        y = _rms(x, g_ref[...])

        @pl.when(g == 0)
        def _():
            xo_ref[...] = y

        @pl.when(g != 0)
        def _():
            h_ref[...] = y
    else:
        xo_ref[...] = x
        h_ref[...] = _modulate(x, g_ref[...], modn_ref[0], modn_ref[1]).astype(BF16)


def _combine_call(tab, y, rowid, gate, x, mods, layer, nxt, gains, gidx, modrow, ngroups, ntok, final):
    nb = ntok // TB
    t = x.shape[0]
    if final:
        assert ngroups == 2
        out_specs = [pl.BlockSpec((TB, D), lambda g, b, *_: (jnp.where(g == 0, b, nb - 1), 0)),
                     pl.BlockSpec((TB, D), lambda g, b, *_: (jnp.where(g == 0, 0, b), 0))]
        out_shape = [jax.ShapeDtypeStruct((ntok, D), F32), jax.ShapeDtypeStruct((ntok, D), F32)]
    else:
        out_specs = [pl.BlockSpec((TB, D), lambda g, b, *_: (g * nb + b, 0)),
                     pl.BlockSpec((TB, D), lambda g, b, *_: (g * nb + b, 0))]
        out_shape = [jax.ShapeDtypeStruct((t, D), F32), jax.ShapeDtypeStruct((t, D), BF16)]
    grid_spec = pltpu.PrefetchScalarGridSpec(
        num_scalar_prefetch=1,
        grid=(ngroups, nb),
        in_specs=[pl.BlockSpec(memory_space=pl.ANY),
                  pl.BlockSpec((None, None, N_EXPERTS, TB), lambda g, b, *_: (g, b, 0, 0)),
                  pl.BlockSpec((None, None, N_EXPERTS, TB), lambda g, b, *_: (g, b, 0, 0)),
                  pl.BlockSpec((TB, D), lambda g, b, *_: (g * nb + b, 0)),
                  pl.BlockSpec((None, None, 6, 1, D),
                               lambda g, b, *_: (layer, modrow(g * nb + b), 0, 0, 0)),
                  pl.BlockSpec((None, None, 6, 1, D),
                               lambda g, b, *_: (nxt, modrow(g * nb + b), 0, 0, 0)),
                  pl.BlockSpec((None, 1, D), lambda g, b, *_: (gidx, 0, 0))],
        out_specs=out_specs,
        scratch_shapes=[pltpu.VMEM((STACK_ROWS, TB), F32), pltpu.VMEM((2, STACK_ROWS, D), BF16),
                        pltpu.VMEM((TB, D), F32), pltpu.SemaphoreType.DMA((2,))])
    return pl.pallas_call(
        functools.partial(_combine_kernel, nb=nb, nsteps=ngroups * nb, final=final),
        grid_spec=grid_spec,
        out_shape=out_shape,
        compiler_params=_cparams(("arbitrary", "arbitrary")),
        name="combine",
    )(tab, y, rowid, gate, x, mods, mods, gains)


def kernel(x_prompt, x_sample, cache_diff_k, cache_diff_v, cache_swa_k, cache_swa_v, cache_mla_ckv, cache_mla_krope, c, c_ctx, w_ada, b_ada, norm_mix, norm_ffn, w_in_even, w_out_even, diff_lambda, diff_subln, swa_sink, w_in_odd, mla_q_norm, w_q_up, mla_kv_norm, w_kv_up, w_out_odd, w_router, w_gate_exp, w_up_exp, w_down_exp, final_norm):
    batch, seq, _ = x_prompt.shape
    dec_batch, dec_seq, _ = x_sample.shape
    past = cache_diff_k.shape[2]
    depth = w_ada.shape[0]
    n_even = w_in_even.shape[0]
    n_odd = w_in_odd.shape[0]
    nc, ns = batch * seq, dec_batch * dec_seq
    assert nc == ns, "the routed-expert kernels take two token groups of equal size"
    assert nc % TM == 0 and dec_seq % TM == 0 and past % 256 == 0 and dec_seq % GRID_W == 0
    assert seq % TQ == 0 and dec_seq % TQ == 0
    ntok = nc

    def modrow_of(tile):
        def modrow(r):
            tok = r * tile
            return jnp.where(tok < nc, 0, 1 + jnp.maximum(tok - nc, 0) // dec_seq)
        return modrow

    modrow, modrow_tb = modrow_of(TM), modrow_of(TB)

    rc = -(-(1 + dec_batch) // 16) * 16
    cvec = jnp.zeros((rc, D), F32).at[0].set(c_ctx).at[1:1 + dec_batch].set(c)
    mods = _ada_call(cvec, w_ada, b_ada).reshape(depth, rc, 6, 1, D)

    w_even_b = w_in_even.astype(BF16)
    w_oute_b = w_out_even.astype(BF16)
    w_outo_b = w_out_odd.astype(BF16)
    kr_pad = jnp.zeros((n_odd, D, LANES), F32).at[:, :, 64:96].set(w_in_odd[:, :, 640:672])
    w_odd_b = jnp.concatenate([w_in_odd[:, :, :640], kr_pad], axis=-1).astype(BF16)
    wq = w_q_up.reshape(n_odd, MLA_Q_RANK, MLA_HEADS, HD + MLA_ROPE)
    wq_b = jnp.pad(wq, ((0, 0), (0, 0), (0, 0), (0, LANES - HD - MLA_ROPE))).reshape(
        n_odd, MLA_Q_RANK, MLA_QW).astype(BF16)
    wkv = w_kv_up.reshape(n_odd, MLA_KV_RANK, MLA_HEADS, 2 * HD)
    wk_b = jnp.pad(wkv[..., :HD], ((0, 0), (0, 0), (0, 0), (0, LANES - HD))).reshape(
        n_odd, MLA_KV_RANK, MLA_QW).astype(BF16)
    wv_b = wkv[..., HD:].reshape(n_odd, MLA_KV_RANK, D).astype(BF16)
    w_router_b = jnp.pad(w_router, ((0, 0), (0, 0), (0, LANES - N_EXPERTS))).astype(BF16)
    even_tabs = _rope_tables(dec_seq, 16, _even_lane)
    mla_tabs = _rope_tables(dec_seq, 8, _mla_lane)
    ckr = jnp.zeros((dec_batch, n_odd, past, LANES), F32).at[..., 64:96].set(cache_mla_krope)
    even_caches = (cache_diff_k, cache_diff_v, cache_swa_k, cache_swa_v)

    mix_gains = jnp.concatenate([norm_mix, final_norm[None]], axis=0).reshape(depth + 1, 1, D)
    ffn_gains = norm_ffn.reshape(depth, 1, D)
    q_gains = mla_q_norm.reshape(n_odd, 1, MLA_Q_RANK)
    kv_gains = mla_kv_norm.reshape(n_odd, 1, MLA_KV_RANK)
    sublns = diff_subln.reshape(n_even, 1, 2 * HD)

    xa, xb, xb_off = x_prompt.reshape(nc, D), x_sample.reshape(ns, D), 0
    h = _norm_mod_call(xa, xb, mods, 0, mix_gains, modrow)
    even_state = odd_state = None
    y_prompt = y_sample = None
    for i in range(depth):
        j = i // 2
        if i % 2 == 0:
            qkv_c, *even_state = _proj_even_call(h, w_even_b, j, 0, nc, None, dec_seq,
                                                 prev=even_state, seq=seq)
            (qkv_l,) = _proj_even_call(h, w_even_b, j, nc, ns, even_tabs, dec_seq)
            li = _lambda_init(i)
            o_c = _attn_even_call(qkv_c, None, j, diff_lambda, sublns, swa_sink, batch, seq, 0, li)
            o_l = _attn_even_call(qkv_l, even_caches, j, diff_lambda, sublns, swa_sink,
                                  dec_batch, dec_seq, past, li)
            w_out = w_oute_b
        else:
            q_c, *odd_state = _proj_odd_call(h, w_odd_b, q_gains, wq_b, kv_gains, j, 0, nc, None,
                                             dec_seq, prev=odd_state, seq=seq)
            q_l, ckv_l, kr_l = _proj_odd_call(h, w_odd_b, q_gains, wq_b, kv_gains, j, nc, ns,
                                              mla_tabs, dec_seq)
            o_c = _attn_odd_call(q_c, odd_state[0], odd_state[1], None, j, wk_b, wv_b, batch, seq, 0)
            o_l = _attn_odd_call(q_l, ckv_l, kr_l, (cache_mla_ckv, ckr), j, wk_b, wv_b,
                                 dec_batch, dec_seq, past)
            w_out = w_outo_b
        x, h2, aff = _outproj_call(xa, xb, xb_off, o_c, o_l, w_out, j, mods, i, ffn_gains,
                                   w_router_b, modrow)
        rowid, gate, tab = _select_call(aff, 2, ntok)
        xe = _dispatch_call(tab, h2, rowid, 2, ntok)
        y = _ffn_call(tab, xe, w_gate_exp, w_up_exp, w_down_exp, i, 2, ntok)
        final = i == depth - 1
        nxt = i if final else i + 1
        out_a, out_b = _combine_call(tab, y, rowid, gate, x, mods, i, nxt, mix_gains,
                                     depth if final else nxt, modrow_tb, 2, ntok, final)
        if final:
            y_prompt = out_a.reshape(batch, seq, D)
            y_sample = out_b.reshape(dec_batch, dec_seq, D)
        else:
            x, h = out_a, out_b
            xa, xb, xb_off = x, x, nc // TM

    kd, vd, ks, vs = even_state
    ckv_new, kr_new = odd_state
    return (y_prompt, y_sample,
            kd.reshape(batch, n_even, seq, DIFF_HEADS, 2 * HD),
            vd.reshape(batch, n_even, seq, DIFF_HEADS, 2 * HD),
            ks.reshape(batch, n_even, seq, 2, HD), vs.reshape(batch, n_even, seq, 2, HD),
            ckv_new, kr_new[..., 64:96])
```

```python
import functools
import math

import jax
import jax.numpy as jnp
import numpy as np
from jax import lax
from jax.experimental import pallas as pl
from jax.experimental.pallas import tpu as pltpu

F32 = jnp.float32
BF16 = jnp.bfloat16
I32 = jnp.int32

D = 1024
HD = 64
GRID_W = 64
WINDOW = 128
DIFF_HEADS = 4
SWA_HEADS = 8
MLA_HEADS = 16
MLA_Q_RANK = 384
MLA_KV_RANK = 256
MLA_ROPE = 32
N_EXPERTS = 16
EXPERT_FF = 512
EC_FACTOR = 2
ROPE_BASE = 10000.0
EPS = 1e-6
NEG_INF = -1e30
LOG2E = math.log2(math.e)
EVEN_IN = 2304
LANES = 128
TM = 512
TQ = 256
TB = 256
SEG = 16
VMEM_LIMIT = 56 * 1024 * 1024


def _cparams(sem, vmem=VMEM_LIMIT):
    return pltpu.CompilerParams(dimension_semantics=sem, vmem_limit_bytes=vmem)


def _dot(a, b):
    return jnp.dot(a, b, preferred_element_type=F32)


def _dot_nt(a, b):
    return lax.dot_general(a, b, (((1,), (1,)), ((), ())), preferred_element_type=F32)


def _silu(x):
    return x / (1.0 + jnp.exp(-x))


def _rms(x, g):
    ms = jnp.mean(x * x, axis=-1, keepdims=True)
    return x * lax.rsqrt(ms + EPS) * g


def _modulate(x, g, shift, scale):
    return _rms(x, g) * (1.0 + scale) + shift


def _lambda_init(layer):
    return 0.8 - 0.6 * math.exp(-0.3 * layer)


def _ada_kernel(c_ref, w_ref, b_ref, o_ref):
    s = _silu(c_ref[...]).astype(BF16)
    o_ref[...] = _dot(s, w_ref[...].astype(BF16)) + b_ref[...]


def _ada_call(cvec, w_ada, b_ada):
    depth, _, n6 = w_ada.shape
    rc = cvec.shape[0]
    tn = 512
    return pl.pallas_call(
        _ada_kernel,
        grid=(depth, n6 // tn),
        in_specs=[pl.BlockSpec((rc, D), lambda i, n: (0, 0)),
                  pl.BlockSpec((None, D, tn), lambda i, n: (i, 0, n)),
                  pl.BlockSpec((None, 1, tn), lambda i, n: (i, 0, n))],
        out_specs=pl.BlockSpec((None, rc, tn), lambda i, n: (i, 0, n)),
        out_shape=jax.ShapeDtypeStruct((depth, rc, n6), F32),
        compiler_params=_cparams(("parallel", "parallel")),
        name="ada",
    )(cvec, w_ada, b_ada.reshape(depth, 1, n6))


def _two_part_specs(rows_a, rows_b, width, off_b=0):
    na, nb_ = rows_a // TM, rows_b // TM
    return (pl.BlockSpec((TM, width), lambda r: (jnp.minimum(r, na - 1), 0)),
            pl.BlockSpec((TM, width), lambda r: (off_b + jnp.clip(r - na, 0, nb_ - 1), 0)))


def _norm_mod_kernel(xa_ref, xb_ref, mod_ref, g_ref, h_ref, *, ntile_a):
    def run(x_ref):
        h_ref[...] = _modulate(x_ref[...], g_ref[...], mod_ref[0], mod_ref[1]).astype(BF16)

    @pl.when(pl.program_id(0) < ntile_a)
    def _():
        run(xa_ref)

    @pl.when(pl.program_id(0) >= ntile_a)
    def _():
        run(xb_ref)


def _norm_mod_call(xa, xb, mods, layer, gains, modrow):
    na, nb_ = xa.shape[0], xb.shape[0]
    return pl.pallas_call(
        functools.partial(_norm_mod_kernel, ntile_a=na // TM),
        grid=((na + nb_) // TM,),
        in_specs=[*_two_part_specs(na, nb_, D),
                  pl.BlockSpec((None, None, 6, 1, D), lambda r: (layer, modrow(r), 0, 0, 0)),
                  pl.BlockSpec((None, 1, D), lambda r: (layer, 0, 0))],
        out_specs=pl.BlockSpec((TM, D), lambda r: (r, 0)),
        out_shape=jax.ShapeDtypeStruct((na + nb_, D), BF16),
        compiler_params=_cparams(("parallel",)),
        name="norm_mod",
    )(xa, xb, mods, gains)


def _rope_tables(dec_seq, half, lane_of_dim):
    pos = jnp.arange(dec_seq)
    row = (pos // GRID_W).astype(F32)
    col = (pos % GRID_W).astype(F32)
    inv = ROPE_BASE ** (-(jnp.arange(half, dtype=F32) / half))
    ang = jnp.stack([row[:, None] * inv[None, :], col[:, None] * inv[None, :]])
    info = [lane_of_dim(lane) for lane in range(LANES)]
    axis = np.array([0 if i is None else i[0] for i in info])
    freq = np.array([0 if i is None else i[1] for i in info])
    first = np.array([i is not None and not i[2] for i in info])[None, :]
    second = np.array([i is not None and i[2] for i in info])[None, :]
    lane_ang = ang[axis, :, freq].T
    cos, sin = jnp.cos(lane_ang), jnp.sin(lane_ang)
    return (jnp.where(first | second, cos, 1.0), jnp.where(first, -sin, 0.0),
            jnp.where(second, sin, 0.0))


def _even_lane(lane):
    j = lane % HD
    axis, jj = j // 32, j % 32
    return axis, jj % 16, jj >= 16


def _mla_lane(lane):
    if lane < 64 or lane >= 96:
        return None
    jj = lane - 64
    axis, k = jj // 16, jj % 16
    return axis, k % 8, k >= 8


def _rope(x, c, s1, s2, shift):
    return x * c + pltpu.roll(x, LANES - shift, 1) * s1 + pltpu.roll(x, shift, 1) * s2


_EVEN_ROPE_TILES = tuple(range(0, 8)) + tuple(range(12, 17))
_EVEN_Q_TILES = tuple(range(0, 4)) + tuple(range(12, 16))


def _append_layer(prev_refs, out_refs, new_values):
    for i, (out_ref, new) in enumerate(zip(out_refs, new_values)):
        nbatch, nlayers, seq, width = out_ref.shape
        if prev_refs:
            out_ref[:, 0:nlayers - 1] = prev_refs[i][...]
        out_ref[:, nlayers - 1] = new.reshape(nbatch, seq, width)


def _state_specs(prev, widths, seq):
    nlayers = 1 if prev is None else prev[0].shape[1] + 1
    per = TM // seq
    ins = [] if prev is None else [pl.BlockSpec((per, nlayers - 1, seq, w), lambda r: (r, 0, 0, 0))
                                   for w in widths]
    outs = [pl.BlockSpec((per, nlayers, seq, w), lambda r: (r, 0, 0, 0)) for w in widths]
    return ins, outs, nlayers


def _proj_even_kernel(*refs, rope, caches, nprev):
    h_ref, w_ref = refs[0], refs[1]
    pos = 2
    if rope:
        c_ref, s1_ref, s2_ref = refs[2:5]
        pos = 5
    prev_refs = refs[pos:pos + nprev]
    pos += nprev
    qkv_ref = refs[pos]
    res = _dot(h_ref[...], w_ref[...])
    scale = HD ** -0.5 * LOG2E
    for t in range(EVEN_IN // LANES):
        x = res[:, t * LANES:(t + 1) * LANES]
        if rope and t in _EVEN_ROPE_TILES:
            x = _rope(x, c_ref[...], s1_ref[...], s2_ref[...], 16)
        if t in _EVEN_Q_TILES:
            x = x * scale
        qkv_ref[:, t * LANES:(t + 1) * LANES] = x.astype(BF16)
    if caches:
        _append_layer(prev_refs, refs[pos + 1:pos + 5],
                      (res[:, 512:1024], res[:, 1024:1536], res[:, 2048:2176], res[:, 2176:2304]))


def _proj_even_call(h, w, j, row0, nrows, tables, dec_seq, prev=None, seq=None):
    rope = tables is not None
    caches = not rope
    t0 = row0 // TM
    in_specs = [pl.BlockSpec((TM, D), lambda r: (t0 + r, 0)),
                pl.BlockSpec((None, D, EVEN_IN), lambda r: (j, 0, 0))]
    args = [h, w]
    if rope:
        per = dec_seq // TM
        for _ in range(3):
            in_specs.append(pl.BlockSpec((TM, LANES), lambda r: (r % per, 0)))
        args += list(tables)
    out_specs = [pl.BlockSpec((TM, EVEN_IN), lambda r: (r, 0))]
    out_shape = [jax.ShapeDtypeStruct((nrows, EVEN_IN), BF16)]
    nprev = 0
    if caches:
        widths = (512, 512, LANES, LANES)
        ins, outs, nlayers = _state_specs(prev, widths, seq)
        nprev = len(ins)
        in_specs += ins
        args += [] if prev is None else list(prev)
        out_specs += outs
        out_shape += [jax.ShapeDtypeStruct((nrows // seq, nlayers, seq, wd), F32) for wd in widths]
    return pl.pallas_call(
        functools.partial(_proj_even_kernel, rope=rope, caches=caches, nprev=nprev),
        grid=(nrows // TM,),
        in_specs=in_specs, out_specs=out_specs, out_shape=out_shape,
        compiler_params=_cparams(("parallel",)),
        name="proj_even_lat" if rope else "proj_even_ctx",
    )(*args)


ODD_IN_PAD = MLA_Q_RANK + MLA_KV_RANK + LANES
MLA_QW = MLA_HEADS * LANES


def _proj_odd_kernel(*refs, rope, nprev):
    h_ref, w_ref, qn_ref, wq_ref, kvn_ref = refs[:5]
    pos = 5
    if rope:
        c_ref, s1_ref, s2_ref = refs[5:8]
        pos = 8
    prev_refs = refs[pos:pos + nprev]
    pos += nprev
    q_ref, ckv_ref, kr_ref = refs[pos:pos + 3]
    res = _dot(h_ref[...], w_ref[...])
    cq = _rms(res[:, :MLA_Q_RANK], qn_ref[...]).astype(BF16)
    ckv = _rms(res[:, MLA_Q_RANK:MLA_Q_RANK + MLA_KV_RANK], kvn_ref[...])
    kr = res[:, MLA_Q_RANK + MLA_KV_RANK:]
    if rope:
        kr = _rope(kr, c_ref[...], s1_ref[...], s2_ref[...], 8)
        ckv_ref[...] = ckv
        kr_ref[...] = kr
    else:
        _append_layer(prev_refs, (ckv_ref, kr_ref), (ckv, kr))
    q = _dot(cq, wq_ref[...])
    scale = (HD + MLA_ROPE) ** -0.5 * LOG2E
    for t in range(MLA_HEADS):
        x = q[:, t * LANES:(t + 1) * LANES]
        if rope:
            x = _rope(x, c_ref[...], s1_ref[...], s2_ref[...], 8)
        q_ref[:, t * LANES:(t + 1) * LANES] = (x * scale).astype(BF16)


def _proj_odd_call(h, w_in, qn, wq, kvn, j, row0, nrows, tables, dec_seq, prev=None, seq=None):
    rope = tables is not None
    t0 = row0 // TM
    in_specs = [pl.BlockSpec((TM, D), lambda r: (t0 + r, 0)),
                pl.BlockSpec((None, D, ODD_IN_PAD), lambda r: (j, 0, 0)),
                pl.BlockSpec((None, 1, MLA_Q_RANK), lambda r: (j, 0, 0)),
                pl.BlockSpec((None, MLA_Q_RANK, MLA_QW), lambda r: (j, 0, 0)),
                pl.BlockSpec((None, 1, MLA_KV_RANK), lambda r: (j, 0, 0))]
    args = [h, w_in, qn, wq, kvn]
    if rope:
        per = dec_seq // TM
        for _ in range(3):
            in_specs.append(pl.BlockSpec((TM, LANES), lambda r: (r % per, 0)))
        args += list(tables)
    out_specs = [pl.BlockSpec((TM, MLA_QW), lambda r: (r, 0))]
    out_shape = [jax.ShapeDtypeStruct((nrows, MLA_QW), BF16)]
    widths = (MLA_KV_RANK, LANES)
    nprev = 0
    if rope:
        out_specs += [pl.BlockSpec((TM, wd), lambda r: (r, 0)) for wd in widths]
        out_shape += [jax.ShapeDtypeStruct((nrows, wd), F32) for wd in widths]
    else:
        ins, outs, nlayers = _state_specs(prev, widths, seq)
        nprev = len(ins)
        in_specs += ins
        args += [] if prev is None else list(prev)
        out_specs += outs
        out_shape += [jax.ShapeDtypeStruct((nrows // seq, nlayers, seq, wd), F32) for wd in widths]
    return pl.pallas_call(
        functools.partial(_proj_odd_kernel, rope=rope, nprev=nprev),
        grid=(nrows // TM,),
        in_specs=in_specs,
        out_specs=out_specs,
        out_shape=out_shape,
        compiler_params=_cparams(("parallel",)),
        name="proj_odd_lat" if rope else "proj_odd_ctx",
    )(*args)


def _ones_lane(half):
    return HD if half == 0 else 0


def _row_sum(e, o, half, from_matmul):
    if from_matmul:
        one = _ones_lane(half)
        return o[:, one:one + 1]
    return jnp.sum(e, axis=-1, keepdims=True)


def _half_values(v, half):
    lane = lax.broadcasted_iota(I32, (1, LANES), 1)
    keep = (lane < HD) if half == 0 else (lane >= HD)
    return jnp.where(keep, v, jnp.where(lane == _ones_lane(half), 1.0, 0.0)).astype(BF16)


def _attn_even_kernel(*refs, seq, past, lam_init, layer):
    latent = past > 0
    n = past + seq
    if latent:
        (qkv_ref, ck_ref, cv_ref, sk_ref, sv_ref, lam_ref, subln_ref, sink_ref,
         o_ref, kd, vd, ka, vl, vh) = refs
    else:
        qkv_ref, lam_ref, subln_ref, sink_ref, o_ref, kd, vd, ka, vl, vh = refs
    qi = pl.program_id(1)
    lo = lax.broadcasted_iota(I32, (1, LANES), 1) < HD

    @pl.when(qi == 0)
    def _build():
        chunk = 256
        for c0 in range(0, n, chunk):
            rows = slice(c0, c0 + chunk)
            if c0 < past:
                prow = slice(c0, c0 + chunk)
                for h in range(DIFF_HEADS):
                    kd[rows, h * LANES:(h + 1) * LANES] = ck_ref[prow, h, :].astype(BF16)
                    vd[rows, h * LANES:(h + 1) * LANES] = cv_ref[prow, h, :].astype(BF16)
                kt = jnp.concatenate([sk_ref[prow, 0, :], sk_ref[prow, 1, :]], axis=1)
                vt = jnp.concatenate([sv_ref[prow, 0, :], sv_ref[prow, 1, :]], axis=1)
            else:
                orow = slice(c0 - past, c0 - past + chunk)
                kd[rows, :] = qkv_ref[orow, 512:1024]
                vd[rows, :] = qkv_ref[orow, 1024:1536]
                kt = qkv_ref[orow, 2048:2176].astype(F32)
                vt = qkv_ref[orow, 2176:2304].astype(F32)
            kr = pltpu.roll(kt, HD, 1)
            vr = pltpu.roll(vt, HD, 1)
            ka[0, rows, :] = jnp.where(lo, kt, kr).astype(BF16)
            ka[1, rows, :] = jnp.where(lo, kr, kt).astype(BF16)
            vl[0, rows, :] = _half_values(vt, 0)
            vh[0, rows, :] = _half_values(vr, 1)
            vl[1, rows, :] = _half_values(vr, 0)
            vh[1, rows, :] = _half_values(vt, 1)

    r0 = pl.multiple_of(qi * TQ, TQ)
    lam = lam_ref[...]
    lam_full = (jnp.exp(jnp.sum(lam[0:1] * lam[1:2], axis=-1, keepdims=True))
                - jnp.exp(jnp.sum(lam[2:3] * lam[3:4], axis=-1, keepdims=True)) + lam_init)
    zero_b = jnp.zeros((), BF16)

    for h in range(DIFF_HEADS):
        cs = slice(h * LANES, (h + 1) * LANES)
        qt = qkv_ref[pl.ds(r0, TQ), cs]
        kh = kd[:, cs]
        es, rs = [], []
        for comp in range(2):
            qc = jnp.where(lo, qt, zero_b) if comp == 0 else jnp.where(lo, zero_b, qt)
            s = _dot_nt(qc, kh)
            m = jnp.max(s, axis=-1, keepdims=True)
            e = jnp.exp2(s - m)
            es.append(e)
            rs.append(1.0 / jnp.sum(e, axis=-1, keepdims=True))
        a = es[0] * rs[0] - es[1] * (lam_full * rs[1])
        o = _dot(a.astype(BF16), vd[:, cs])
        o = _rms(o, subln_ref[...]) * (1.0 - lam_init)
        o_ref[:, cs] = o.astype(BF16)

    nblk = seq // WINDOW
    dense = past if latent else seq
    if latent:
        per = TQ // WINDOW
        offsets = tuple(range(-1, per + 1))
        rr = lax.broadcasted_iota(I32, (TQ, WINDOW), 0)
        cc = lax.broadcasted_iota(I32, (TQ, WINDOW), 1)
        band, starts = {}, {}
        for d in offsets:
            blk = qi * per + d
            inside = jnp.logical_and(blk >= 0, blk < nblk)
            band[d] = jnp.logical_and(jnp.abs(rr - cc - d * WINDOW) <= WINDOW, inside)
            starts[d] = pl.multiple_of(past + jnp.clip(blk, 0, nblk - 1) * WINDOW, WINDOW)
    for i in range(SWA_HEADS // 2):
        hk = i // 2
        cs = slice(1536 + i * LANES, 1536 + (i + 1) * LANES)
        qt = qkv_ref[pl.ds(r0, TQ), cs]
        halves = []
        for half in range(2):
            qc = jnp.where(lo, qt, zero_b) if half == 0 else jnp.where(lo, zero_b, qt)
            vsel = vl if half == 0 else vh
            sink = sink_ref[layer, 2 * i + half] * LOG2E
            parts = [_dot_nt(qc, ka[hk, 0:dense, :])]
            if latent:
                for d in offsets:
                    s = _dot_nt(qc, ka[hk, pl.ds(starts[d], WINDOW), :])
                    parts.append(jnp.where(band[d], s, NEG_INF))
            s_all = jnp.concatenate(parts, axis=1) if len(parts) > 1 else parts[0]
            m = jnp.maximum(jnp.max(s_all, axis=-1, keepdims=True), sink)
            e = jnp.exp2(s_all - m)
            eb = e.astype(BF16)
            o = _dot(eb[:, 0:dense], vsel[hk, 0:dense, :])
            if latent:
                for k, d in enumerate(offsets):
                    o += _dot(eb[:, dense + k * WINDOW:dense + (k + 1) * WINDOW],
                              vsel[hk, pl.ds(starts[d], WINDOW), :])
            den = _row_sum(e, o, half, from_matmul=latent) + jnp.exp2(sink - m)
            halves.append(o * (1.0 / den))
        o_ref[:, 512 + i * LANES:512 + (i + 1) * LANES] = jnp.where(lo, halves[0], halves[1]).astype(BF16)


def _attn_even_call(qkv, caches, j, lam, subln, sink, nbatch, seq, past, lam_init):
    n = past + seq
    latent = past > 0
    in_specs = [pl.BlockSpec((seq, EVEN_IN), lambda b, q: (b, 0))]
    args = [qkv]
    if latent:
        ck, cv, sk, sv = caches
        in_specs += [pl.BlockSpec((None, None, past, DIFF_HEADS, 2 * HD), lambda b, q: (b, j, 0, 0, 0)),
                     pl.BlockSpec((None, None, past, DIFF_HEADS, 2 * HD), lambda b, q: (b, j, 0, 0, 0)),
                     pl.BlockSpec((None, None, past, 2, HD), lambda b, q: (b, j, 0, 0, 0)),
                     pl.BlockSpec((None, None, past, 2, HD), lambda b, q: (b, j, 0, 0, 0))]
        args += [ck, cv, sk, sv]
    in_specs += [pl.BlockSpec((None, 4, HD), lambda b, q: (j, 0, 0)),
                 pl.BlockSpec((None, 1, 2 * HD), lambda b, q: (j, 0, 0)),
                 pl.BlockSpec(memory_space=pltpu.SMEM)]
    args += [lam, subln, sink]
    return pl.pallas_call(
        functools.partial(_attn_even_kernel, seq=seq, past=past, lam_init=lam_init, layer=j),
        grid=(nbatch, seq // TQ),
        in_specs=in_specs,
        out_specs=pl.BlockSpec((TQ, D), lambda b, q: (b * (seq // TQ) + q, 0)),
        out_shape=jax.ShapeDtypeStruct((nbatch * seq, D), BF16),
        scratch_shapes=[pltpu.VMEM((n, 512), BF16), pltpu.VMEM((n, 512), BF16),
                        pltpu.VMEM((2, n, LANES), BF16), pltpu.VMEM((2, n, LANES), BF16),
                        pltpu.VMEM((2, n, LANES), BF16)],
        compiler_params=_cparams(("arbitrary", "arbitrary")),
        name="attn_even_lat" if latent else "attn_even_ctx",
    )(*args)


def _attn_odd_kernel(*refs, seq, past):
    latent = past > 0
    n = past + seq
    if latent:
        q_ref, ckv_ref, kr_ref, cckv_ref, ckr_ref, wk_ref, wv_ref, o_ref, kf, vlo, vhi = refs
    else:
        q_ref, ckv_ref, kr_ref, wk_ref, wv_ref, o_ref, kf, vlo, vhi = refs
    qi = pl.program_id(1)
    lo = lax.broadcasted_iota(I32, (1, LANES), 1) < HD

    @pl.when(qi == 0)
    def _build():
        chunk = 256
        for c0 in range(0, n, chunk):
            rows = slice(c0, c0 + chunk)
            if c0 < past:
                ckv = cckv_ref[c0:c0 + chunk, :].astype(BF16)
                kr = ckr_ref[c0:c0 + chunk, :]
            else:
                ckv = ckv_ref[c0 - past:c0 - past + chunk, :].astype(BF16)
                kr = kr_ref[c0 - past:c0 - past + chunk, :]
            kk = _dot(ckv, wk_ref[...])
            for h in range(MLA_HEADS):
                cs = slice(h * LANES, (h + 1) * LANES)
                kf[rows, cs] = (kk[:, cs] + kr).astype(BF16)
            vv = _dot(ckv, wv_ref[...])
            for i in range(MLA_HEADS // 2):
                cs = slice(i * LANES, (i + 1) * LANES)
                vlo[rows, cs] = _half_values(vv[:, cs], 0)
                vhi[rows, cs] = _half_values(vv[:, cs], 1)

    r0 = pl.multiple_of(qi * TQ, TQ)
    lo = lax.broadcasted_iota(I32, (1, LANES), 1) < HD
    for i in range(MLA_HEADS // 2):
        halves = []
        for half in range(2):
            h = 2 * i + half
            cs = slice(h * LANES, (h + 1) * LANES)
            s = _dot_nt(q_ref[pl.ds(r0, TQ), cs], kf[:, cs])
            m = jnp.max(s, axis=-1, keepdims=True)
            vsel = vlo if half == 0 else vhi
            e = jnp.exp2(s - m)
            o = _dot(e.astype(BF16), vsel[:, i * LANES:(i + 1) * LANES])
            halves.append(o * (1.0 / _row_sum(e, o, half, from_matmul=latent)))
        o_ref[:, i * LANES:(i + 1) * LANES] = jnp.where(lo, halves[0], halves[1]).astype(BF16)


def _attn_odd_call(q, ckv, kr, caches, j, wk, wv, nbatch, seq, past):
    n = past + seq
    latent = past > 0
    in_specs = [pl.BlockSpec((seq, MLA_QW), lambda b, qq: (b, 0))]
    if latent:
        in_specs += [pl.BlockSpec((seq, MLA_KV_RANK), lambda b, qq: (b, 0)),
                     pl.BlockSpec((seq, LANES), lambda b, qq: (b, 0))]
    else:
        last = ckv.shape[1] - 1
        in_specs += [pl.BlockSpec((None, None, seq, MLA_KV_RANK), lambda b, qq: (b, last, 0, 0)),
                     pl.BlockSpec((None, None, seq, LANES), lambda b, qq: (b, last, 0, 0))]
    args = [q, ckv, kr]
    if latent:
        in_specs += [pl.BlockSpec((None, None, past, MLA_KV_RANK), lambda b, qq: (b, j, 0, 0)),
                     pl.BlockSpec((None, None, past, LANES), lambda b, qq: (b, j, 0, 0))]
        args += list(caches)
    in_specs += [pl.BlockSpec((None, MLA_KV_RANK, MLA_QW), lambda b, qq: (j, 0, 0)),
                 pl.BlockSpec((None, MLA_KV_RANK, D), lambda b, qq: (j, 0, 0))]
    args += [wk, wv]
    return pl.pallas_call(
        functools.partial(_attn_odd_kernel, seq=seq, past=past),
        grid=(nbatch, seq // TQ),
        in_specs=in_specs,
        out_specs=pl.BlockSpec((TQ, D), lambda b, qq: (b * (seq // TQ) + qq, 0)),
        out_shape=jax.ShapeDtypeStruct((nbatch * seq, D), BF16),
        scratch_shapes=[pltpu.VMEM((n, MLA_QW), BF16), pltpu.VMEM((n, D), BF16),
                        pltpu.VMEM((n, D), BF16)],
        compiler_params=_cparams(("arbitrary", "arbitrary")),
        name="attn_odd_lat" if latent else "attn_odd_ctx",
    )(*args)


def _outproj_kernel(xa_ref, xb_ref, oa_ref, ob_ref, w_ref, mod_ref, g_ref, wr_ref,
                    xo_ref, h_ref, aff_ref, *, ntile_a):
    def run(x_ref, o_ref):
        x = x_ref[...] + mod_ref[2] * _dot(o_ref[...], w_ref[...])
        xo_ref[...] = x
        h = _modulate(x, g_ref[...], mod_ref[3], mod_ref[4]).astype(BF16)
        h_ref[...] = h
        logits = _dot(h, wr_ref[...])
        lane = lax.broadcasted_iota(I32, (TM, LANES), 1)
        lg = jnp.where(lane < N_EXPERTS, logits, -jnp.inf)
        e = jnp.exp(lg - jnp.max(lg, axis=-1, keepdims=True))
        aff = e / jnp.sum(e, axis=-1, keepdims=True)
        for c in range(TM // TB):
            aff_ref[c] = aff[c * TB:(c + 1) * TB, :].T[0:N_EXPERTS, :]

    @pl.when(pl.program_id(0) < ntile_a)
    def _():
        run(xa_ref, oa_ref)

    @pl.when(pl.program_id(0) >= ntile_a)
    def _():
        run(xb_ref, ob_ref)


def _outproj_call(xa, xb, xb_off, o_a, o_b, w, widx, mods, layer, gains, w_router, modrow):
    na, nb_ = o_a.shape[0], o_b.shape[0]
    t = na + nb_
    assert TM % TB == 0
    return pl.pallas_call(
        functools.partial(_outproj_kernel, ntile_a=na // TM),
        grid=(t // TM,),
        in_specs=[*_two_part_specs(na, nb_, D, xb_off), *_two_part_specs(na, nb_, D),
                  pl.BlockSpec((None, D, D), lambda r: (widx, 0, 0)),
                  pl.BlockSpec((None, None, 6, 1, D), lambda r: (layer, modrow(r), 0, 0, 0)),
                  pl.BlockSpec((None, 1, D), lambda r: (layer, 0, 0)),
                  pl.BlockSpec((None, D, LANES), lambda r: (layer, 0, 0))],
        out_specs=[pl.BlockSpec((TM, D), lambda r: (r, 0)),
                   pl.BlockSpec((TM, D), lambda r: (r, 0)),
                   pl.BlockSpec((TM // TB, N_EXPERTS, TB), lambda r: (r, 0, 0))],
        out_shape=[jax.ShapeDtypeStruct((t, D), F32), jax.ShapeDtypeStruct((t, D), BF16),
                   jax.ShapeDtypeStruct((t // TB, N_EXPERTS, TB), F32)],
        compiler_params=_cparams(("parallel",)),
        name="outproj",
    )(xa, xb, o_a, o_b, w, mods, gains, w_router)


def _select_kernel(aff_ref, rowid_ref, gate_ref, tab_ref, *, nb, cap):
    ne = N_EXPERTS
    nr = nb * ne
    a = aff_ref[...].reshape(nr, TB)
    ri = lax.broadcasted_iota(I32, (nr, nr), 0)
    ci = lax.broadcasted_iota(I32, (nr, nr), 1)
    same_e = (ri & (ne - 1)) == (ci & (ne - 1))
    same_b = (ri >> 4) == (ci >> 4)
    m_e = jnp.where(same_e, 1.0, 0.0).astype(BF16)
    m_b = jnp.where(same_b, 1.0, 0.0).astype(BF16)
    m_a = jnp.where(jnp.logical_and(same_e, ci < ri), 1.0, 0.0).astype(BF16)
    m_o = jnp.where(jnp.logical_and(same_b, ci < ri), 1.0, 0.0).astype(BF16)
    ui = lax.broadcasted_iota(I32, (TB, TB), 0)
    uj = lax.broadcasted_iota(I32, (TB, TB), 1)
    upper = jnp.where(ui < uj, 1.0, 0.0).astype(BF16)

    def rows_to_lanes(col):
        return jnp.broadcast_to(col, (nr, LANES)).astype(BF16)

    wide = jnp.concatenate([aff_ref[b] for b in range(nb)], axis=1)

    def count_ge(value):
        return jnp.sum(jnp.where(wide >= value, 1.0, 0.0), axis=-1, keepdims=True)

    def bisect(i, v):
        cand = v | jnp.left_shift(jnp.int32(1), 30 - i)
        return jnp.where(count_ge(pltpu.bitcast(cand, F32)) >= cap, cand, v)

    thr = lax.fori_loop(0, 31, bisect, jnp.zeros((ne, 1), I32))

    def refine(i, lohi):
        lo_e, hi_e = lohi
        mid = 0.5 * (lo_e + hi_e)
        take = count_ge(mid) >= cap
        return jnp.where(take, mid, lo_e), jnp.where(take, hi_e, mid)

    lo_e, hi_e = lax.fori_loop(0, TIE_STEPS, refine,
                               (pltpu.bitcast(thr, F32), pltpu.bitcast(thr + 1, F32)))
    lo_v = jnp.concatenate([lo_e] * nb, axis=0)
    hi_v = jnp.concatenate([hi_e] * nb, axis=0)
    gt = jnp.where(a >= hi_v, 1.0, 0.0)
    eq = jnp.where(jnp.logical_and(a >= lo_v, a < hi_v), 1.0, 0.0)
    n_gt = _dot(m_e, rows_to_lanes(jnp.sum(gt, axis=-1, keepdims=True)))[:, 0:1]
    need = cap - n_gt
    eq_before = (_dot(m_a, rows_to_lanes(jnp.sum(eq, axis=-1, keepdims=True)))[:, 0:1]
                 + _dot(eq.astype(BF16), upper))
    sel = jnp.where(jnp.logical_and(eq > 0.0, eq_before < need), 1.0, gt)
    local = _dot(sel.astype(BF16), upper)
    cnt = jnp.sum(sel, axis=-1, keepdims=True)
    seg = jnp.floor((cnt + (SEG - 1)) * (1.0 / SEG)) * SEG
    segb = rows_to_lanes(seg)
    over = jnp.maximum(seg - WINR, 0.0)
    overb = rows_to_lanes(over)
    off_over = FIRST_ROWS + _dot(m_o, overb)[:, 0:1]
    off_buf = _dot(m_a, segb)[:, 0:1]
    over_blk = _dot(m_b, overb)[:, 0:1]
    rows_exp = _dot(m_e, segb)[:, 0:1]
    tiles_exp = jnp.floor((rows_exp + (TF - 1)) / TF)
    expert = (lax.broadcasted_iota(I32, (nr, 1), 0) & (ne - 1)).astype(F32)
    row = jnp.where(local < WINR, expert * WINR + local, off_over + local - WINR)
    rowid_ref[...] = jnp.where(sel > 0.0, row, -1.0).astype(I32).reshape(nb, ne, TB)
    gate_ref[...] = jnp.where(sel > 0.0, a, 0.0).reshape(nb, ne, TB)
    tl = lax.broadcasted_iota(I32, (nr, LANES), 1)
    tab = jnp.where(tl == 0, seg, jnp.where(tl == 1, off_over, jnp.where(
        tl == 2, off_buf, jnp.where(tl == 3, over_blk, jnp.where(tl == 4, rows_exp, tiles_exp)))))
    tab_ref[...] = tab.T[0:8, :].astype(I32)


def _select_call(aff, ngroups, ntok):
    nb = ntok // TB
    cap = EC_FACTOR * ntok // N_EXPERTS
    nr = nb * N_EXPERTS
    return pl.pallas_call(
        functools.partial(_select_kernel, nb=nb, cap=cap),
        grid=(ngroups,),
        in_specs=[pl.BlockSpec((nb, N_EXPERTS, TB), lambda g: (g, 0, 0))],
        out_specs=[pl.BlockSpec((None, nb, N_EXPERTS, TB), lambda g: (g, 0, 0, 0)),
                   pl.BlockSpec((None, nb, N_EXPERTS, TB), lambda g: (g, 0, 0, 0)),
                   pl.BlockSpec((None, 8, nr), lambda g: (g, 0, 0))],
        out_shape=[jax.ShapeDtypeStruct((ngroups, nb, N_EXPERTS, TB), I32),
                   jax.ShapeDtypeStruct((ngroups, nb, N_EXPERTS, TB), F32),
                   jax.ShapeDtypeStruct((ngroups, 8, nr), I32)],
        compiler_params=_cparams(("arbitrary",)),
        name="select",
    )(aff)


STACK_ROWS = N_EXPERTS * TB


FIRST_ROWS = 3 * TB


WINR = 3 * SEG
assert N_EXPERTS * WINR == FIRST_ROWS
TF = 672
TIE_STEPS = 12


def _window_rows(rowid_ref, dst, value_ref=None):
    for e in range(N_EXPERTS):
        rid_e = rowid_ref[e:e + 1, :]
        val_e = 1.0 if value_ref is None else value_ref[e:e + 1, :]
        hit = rid_e == lax.broadcasted_iota(I32, (WINR, TB), 0) + e * WINR
        dst[e * WINR:(e + 1) * WINR, :] = jnp.where(hit, val_e, 0.0).astype(dst.dtype)


class _Table:
    def __init__(self, tab_s, row, nb, per_block=False):
        self.tab_s, self.row, self.per = tab_s, row, nb if per_block else nb * N_EXPERTS
        self.scale = N_EXPERTS if per_block else 1

    def __getitem__(self, k):
        if self.per & (self.per - 1) == 0:
            hi, low = lax.shift_right_logical(k, self.per.bit_length() - 1), k & (self.per - 1)
        else:
            hi, low = lax.div(k, self.per), lax.rem(k, self.per)
        return self.tab_s[hi, self.row, low * self.scale]


def _tables(tab_s, nb):
    return (_Table(tab_s, 0, nb), _Table(tab_s, 1, nb), _Table(tab_s, 2, nb),
            _Table(tab_s, 3, nb, per_block=True))


def _overflow_groups(seg_s, k):
    return lax.div(jnp.maximum(seg_s[k] - WINR, 0), SEG)


def _overflow_rows(seg_s, offo_s, step, over, rowid_ref, dst, value_ref=None):
    def zero(i, carry):
        r0 = pl.multiple_of(FIRST_ROWS + i * SEG, SEG)
        dst[pl.ds(r0, SEG), :] = jnp.zeros((SEG, TB), dst.dtype)
        return carry

    lax.fori_loop(0, ((over + TB - 1) // TB) * (TB // SEG), zero, 0)
    for e in range(N_EXPERTS):
        k = step * N_EXPERTS + e
        rid_e = rowid_ref[e:e + 1, :]
        val_e = 1.0 if value_ref is None else value_ref[e:e + 1, :]

        def group(i, carry, off=offo_s[k], rid_e=rid_e, val_e=val_e):
            r0 = pl.multiple_of(off + i * SEG, SEG)
            hit = rid_e == lax.broadcasted_iota(I32, (SEG, TB), 0) + r0
            dst[pl.ds(r0, SEG), :] = jnp.where(hit, val_e, 0.0).astype(dst.dtype)
            return carry

        lax.fori_loop(0, _overflow_groups(seg_s, k), group, 0)


def _wait_rows(rows, make_copy):
    def big(i, carry):
        make_copy(TB).wait()
        return carry

    def small(i, carry):
        make_copy(SEG).wait()
        return carry

    lax.fori_loop(0, lax.div(rows, TB), big, 0)
    lax.fori_loop(0, lax.div(lax.rem(rows, TB), SEG), small, 0)


def _dispatch_kernel(tab_s, h_ref, rowid_ref, xe_hbm,
                     onehot, stack, zbuf, sem, zsem, *, nb, nsteps, cap):
    g = pl.program_id(0)
    b = pl.program_id(1)
    step = g * nb + b
    slot = lax.rem(step, 2)
    seg_s, offo_s, offb_s, over_s = _tables(tab_s, nb)
    over = over_s[step]
    xrows = xe_hbm.shape[2]

    def wait_slot(nrows, sl):
        _wait_rows(nrows, lambda n: pltpu.make_async_copy(
            stack.at[sl, pl.ds(0, n)], xe_hbm.at[0, 0, pl.ds(0, n)], sem.at[sl]))

    @pl.when(step == 0)
    def _init():
        stack[...] = jnp.zeros_like(stack)

    @pl.when(b == 0)
    def _zero_unused():
        zbuf[...] = jnp.zeros_like(zbuf)
        for e in range(N_EXPERTS):
            pltpu.make_async_copy(zbuf, xe_hbm.at[g, e, pl.ds(cap, xrows - cap)], zsem).start()

    _window_rows(rowid_ref, onehot)
    h = h_ref[...]
    stack[slot, 0:FIRST_ROWS, :] = _dot(onehot[0:FIRST_ROWS, :], h).astype(BF16)

    @pl.when(over > 0)
    def _overflow():
        _overflow_rows(seg_s, offo_s, step, over, rowid_ref, onehot)

        def chunk(c, carry):
            base = pl.multiple_of(FIRST_ROWS + c * TB, TB)
            stack[slot, pl.ds(base, TB), :] = _dot(onehot[pl.ds(base, TB), :], h).astype(BF16)
            return carry

        lax.fori_loop(0, (over + TB - 1) // TB, chunk, 0)

    @pl.when(b == 0)
    def _zero_unused_done():
        for e in range(N_EXPERTS):
            pltpu.make_async_copy(zbuf, xe_hbm.at[g, e, pl.ds(cap, xrows - cap)], zsem).wait()

    @pl.when(step >= 1)
    def _previous_landed():
        wait_slot(FIRST_ROWS + over_s[step - 1], 1 - slot)

    for e in range(N_EXPERTS):
        k = step * N_EXPERTS + e
        pltpu.make_async_copy(
            stack.at[slot, e * WINR:(e + 1) * WINR],
            xe_hbm.at[g, e, pl.ds(pl.multiple_of(offb_s[k], SEG), WINR)], sem.at[slot]).start()

    @pl.when(over > 0)
    def _overflow_copies():
        for e in range(N_EXPERTS):
            k = step * N_EXPERTS + e

            def one(i, carry, e=e, k=k):
                pltpu.make_async_copy(
                    stack.at[slot, pl.ds(pl.multiple_of(offo_s[k] + i * SEG, SEG), SEG)],
                    xe_hbm.at[g, e, pl.ds(pl.multiple_of(offb_s[k] + WINR + i * SEG, SEG), SEG)],
                    sem.at[slot]).start()
                return carry

            lax.fori_loop(0, _overflow_groups(seg_s, k), one, 0)

    @pl.when(step == nsteps - 1)
    def _drain():
        wait_slot(FIRST_ROWS + over, slot)


def _expert_rows(ntok):
    cap = EC_FACTOR * ntok // N_EXPERTS
    worst = cap + (ntok // TB) * (SEG - 1)
    tiles = -(-(worst + WINR) // TF)
    assert cap % SEG == 0 and cap >= WINR and tiles * TF > cap
    return cap, tiles


def _dispatch_call(tab, h, rowid, ngroups, ntok):
    nb = ntok // TB
    cap, tiles = _expert_rows(ntok)
    xrows = tiles * TF
    grid_spec = pltpu.PrefetchScalarGridSpec(
        num_scalar_prefetch=1,
        grid=(ngroups, nb),
        in_specs=[pl.BlockSpec((TB, D), lambda g, b, *_: (g * nb + b, 0)),
                  pl.BlockSpec((None, None, N_EXPERTS, TB), lambda g, b, *_: (g, b, 0, 0))],
        out_specs=pl.BlockSpec(memory_space=pl.ANY),
        scratch_shapes=[pltpu.VMEM((STACK_ROWS, TB), BF16), pltpu.VMEM((2, STACK_ROWS, D), BF16),
                        pltpu.VMEM((xrows - cap, D), BF16),
                        pltpu.SemaphoreType.DMA((2,)), pltpu.SemaphoreType.DMA])
    return pl.pallas_call(
        functools.partial(_dispatch_kernel, nb=nb, nsteps=ngroups * nb, cap=cap),
        grid_spec=grid_spec,
        out_shape=jax.ShapeDtypeStruct((ngroups, N_EXPERTS, xrows, D), BF16),
        compiler_params=_cparams(("arbitrary", "arbitrary")),
        name="dispatch",
    )(tab, h, rowid)


def _ffn_kernel(tab_s, xe_ref, wg_hbm, wu_hbm, wd_hbm, y_ref, wg32, wu32, wd32, wgb, wub, wdb, sem,
                *, layer):
    e = pl.program_id(0)
    g = pl.program_id(1)
    j = pl.program_id(2)

    def weight_copies(ee, sl):
        return (pltpu.make_async_copy(wg_hbm.at[layer, ee], wg32.at[sl], sem.at[sl]),
                pltpu.make_async_copy(wu_hbm.at[layer, ee], wu32.at[sl], sem.at[sl]),
                pltpu.make_async_copy(wd_hbm.at[layer, ee], wd32.at[sl], sem.at[sl]))

    @pl.when(jnp.logical_and(g == 0, j == 0))
    def _weights():
        sl = lax.rem(e, 2)

        @pl.when(e == 0)
        def _():
            for cp in weight_copies(e, sl):
                cp.start()

        @pl.when(e + 1 < N_EXPERTS)
        def _():
            for cp in weight_copies(e + 1, 1 - sl):
                cp.start()

        for cp in weight_copies(e, sl):
            cp.wait()
        wgb[...] = wg32[sl].astype(BF16)
        wub[...] = wu32[sl].astype(BF16)
        wdb[...] = wd32[sl].astype(BF16)

    live = j < tab_s[g, 5, e]

    @pl.when(live)
    def _run():
        x = xe_ref[...]
        hid = (_silu(_dot(x, wgb[...])) * _dot(x, wub[...])).astype(BF16)
        y_ref[...] = _dot(hid, wdb[...]).astype(BF16)

    @pl.when(jnp.logical_not(live))
    def _skip():
        y_ref[...] = jnp.zeros_like(y_ref)


def _ffn_call(tab, xe, wg, wu, wd, layer, ngroups, ntok):
    _, tiles = _expert_rows(ntok)

    def xmap(e, g, j, tab_s):
        return (g, e, jnp.minimum(j, tab_s[g, 5, e] - 1), 0)

    grid_spec = pltpu.PrefetchScalarGridSpec(
        num_scalar_prefetch=1,
        grid=(N_EXPERTS, ngroups, tiles),
        in_specs=[pl.BlockSpec((None, None, TF, D), xmap),
                  pl.BlockSpec(memory_space=pl.ANY), pl.BlockSpec(memory_space=pl.ANY),
                  pl.BlockSpec(memory_space=pl.ANY)],
        out_specs=pl.BlockSpec((None, None, TF, D), lambda e, g, j, nt: (g, e, j, 0)),
        scratch_shapes=[pltpu.VMEM((2, D, EXPERT_FF), F32), pltpu.VMEM((2, D, EXPERT_FF), F32),
                        pltpu.VMEM((2, EXPERT_FF, D), F32),
                        pltpu.VMEM((D, EXPERT_FF), BF16), pltpu.VMEM((D, EXPERT_FF), BF16),
                        pltpu.VMEM((EXPERT_FF, D), BF16), pltpu.SemaphoreType.DMA((2,))])
    return pl.pallas_call(
        functools.partial(_ffn_kernel, layer=layer),
        grid_spec=grid_spec,
        out_shape=jax.ShapeDtypeStruct((ngroups, N_EXPERTS, tiles * TF, D), BF16),
        compiler_params=_cparams(("arbitrary", "arbitrary", "arbitrary")),
        name="ffn",
    )(tab, xe, wg, wu, wd)


def _combine_kernel(tab_s, y_hbm, rowid_ref, gate_ref, x_ref, mod_ref,
                    modn_ref, g_ref, xo_ref, h_ref, weights, stack, acc, sem, *, nb, nsteps, final):
    g = pl.program_id(0)
    b = pl.program_id(1)
    step = g * nb + b
    slot = lax.rem(step, 2)
    seg_s, offo_s, offb_s, over_s = _tables(tab_s, nb)
    over = over_s[step]

    def fetch(st, sl):
        gg = lax.div(st, nb)
        for e in range(N_EXPERTS):
            k = st * N_EXPERTS + e
            pltpu.make_async_copy(
                y_hbm.at[gg, e, pl.ds(pl.multiple_of(offb_s[k], SEG), WINR)],
                stack.at[sl, e * WINR:(e + 1) * WINR], sem.at[sl]).start()

        @pl.when(over_s[st] > 0)
        def _():
            for e in range(N_EXPERTS):
                k = st * N_EXPERTS + e

                def one(i, carry, e=e, k=k):
                    pltpu.make_async_copy(
                        y_hbm.at[gg, e, pl.ds(pl.multiple_of(offb_s[k] + WINR + i * SEG, SEG), SEG)],
                        stack.at[sl, pl.ds(pl.multiple_of(offo_s[k] + i * SEG, SEG), SEG)],
                        sem.at[sl]).start()
                    return carry

                lax.fori_loop(0, _overflow_groups(seg_s, k), one, 0)

    @pl.when(step == 0)
    def _first():
        stack[...] = jnp.zeros_like(stack)
        fetch(step, slot)

    if nsteps > 1:
        @pl.when(step + 1 < nsteps)
        def _prefetch():
            fetch(step + 1, 1 - slot)

    _window_rows(rowid_ref, weights, gate_ref)

    @pl.when(over > 0)
    def _():
        _overflow_rows(seg_s, offo_s, step, over, rowid_ref, weights, gate_ref)

    _wait_rows(FIRST_ROWS + over, lambda n: pltpu.make_async_copy(
        y_hbm.at[0, 0, pl.ds(0, n)], stack.at[slot, pl.ds(0, n)], sem.at[slot]))

    def token_weights(base):
        return weights[pl.ds(base, TB), :].T.astype(BF16)

    w = jnp.concatenate([token_weights(c * TB) for c in range(FIRST_ROWS // TB)], axis=1)
    acc[...] = _dot(w, stack[slot, 0:FIRST_ROWS, :])

    @pl.when(over > 0)
    def _():
        def chunk(c, carry):
            base = pl.multiple_of(FIRST_ROWS + c * TB, TB)
            acc[...] += _dot(token_weights(base), stack[slot, pl.ds(base, TB), :])
            return carry

        lax.fori_loop(0, (over + TB - 1) // TB, chunk, 0)

    x = x_ref[...] + mod_ref[5] * acc[...]
    if final:
        y = _rms(x, g_ref[...])

        @pl.when(g == 0)
        def _():
            xo_ref[...] = y

        @pl.when(g != 0)
        def _():
            h_ref[...] = y
    else:
        xo_ref[...] = x
        h_ref[...] = _modulate(x, g_ref[...], modn_ref[0], modn_ref[1]).astype(BF16)


def _combine_call(tab, y, rowid, gate, x, mods, layer, nxt, gains, gidx, modrow, ngroups, ntok, final):
    nb = ntok // TB
    t = x.shape[0]
    if final:
        assert ngroups == 2
        out_specs = [pl.BlockSpec((TB, D), lambda g, b, *_: (jnp.where(g == 0, b, nb - 1), 0)),
                     pl.BlockSpec((TB, D), lambda g, b, *_: (jnp.where(g == 0, 0, b), 0))]
        out_shape = [jax.ShapeDtypeStruct((ntok, D), F32), jax.ShapeDtypeStruct((ntok, D), F32)]
    else:
        out_specs = [pl.BlockSpec((TB, D), lambda g, b, *_: (g * nb + b, 0)),
                     pl.BlockSpec((TB, D), lambda g, b, *_: (g * nb + b, 0))]
        out_shape = [jax.ShapeDtypeStruct((t, D), F32), jax.ShapeDtypeStruct((t, D), BF16)]
    grid_spec = pltpu.PrefetchScalarGridSpec(
        num_scalar_prefetch=1,
        grid=(ngroups, nb),
        in_specs=[pl.BlockSpec(memory_space=pl.ANY),
                  pl.BlockSpec((None, None, N_EXPERTS, TB), lambda g, b, *_: (g, b, 0, 0)),
                  pl.BlockSpec((None, None, N_EXPERTS, TB), lambda g, b, *_: (g, b, 0, 0)),
                  pl.BlockSpec((TB, D), lambda g, b, *_: (g * nb + b, 0)),
                  pl.BlockSpec((None, None, 6, 1, D),
                               lambda g, b, *_: (layer, modrow(g * nb + b), 0, 0, 0)),
                  pl.BlockSpec((None, None, 6, 1, D),
                               lambda g, b, *_: (nxt, modrow(g * nb + b), 0, 0, 0)),
                  pl.BlockSpec((None, 1, D), lambda g, b, *_: (gidx, 0, 0))],
        out_specs=out_specs,
        scratch_shapes=[pltpu.VMEM((STACK_ROWS, TB), F32), pltpu.VMEM((2, STACK_ROWS, D), BF16),
                        pltpu.VMEM((TB, D), F32), pltpu.SemaphoreType.DMA((2,))])
    return pl.pallas_call(
        functools.partial(_combine_kernel, nb=nb, nsteps=ngroups * nb, final=final),
        grid_spec=grid_spec,
        out_shape=out_shape,
        compiler_params=_cparams(("arbitrary", "arbitrary")),
        name="combine",
    )(tab, y, rowid, gate, x, mods, mods, gains)


def kernel(x_prompt, x_sample, cache_diff_k, cache_diff_v, cache_swa_k, cache_swa_v, cache_mla_ckv, cache_mla_krope, c, c_ctx, w_ada, b_ada, norm_mix, norm_ffn, w_in_even, w_out_even, diff_lambda, diff_subln, swa_sink, w_in_odd, mla_q_norm, w_q_up, mla_kv_norm, w_kv_up, w_out_odd, w_router, w_gate_exp, w_up_exp, w_down_exp, final_norm):
    batch, seq, _ = x_prompt.shape
    dec_batch, dec_seq, _ = x_sample.shape
    past = cache_diff_k.shape[2]
    depth = w_ada.shape[0]
    n_even = w_in_even.shape[0]
    n_odd = w_in_odd.shape[0]
    nc, ns = batch * seq, dec_batch * dec_seq
    assert nc == ns, "the routed-expert kernels take two token groups of equal size"
    assert nc % TM == 0 and dec_seq % TM == 0 and past % 256 == 0 and dec_seq % GRID_W == 0
    assert seq % TQ == 0 and dec_seq % TQ == 0
    ntok = nc

    def modrow_of(tile):
        def modrow(r):
            tok = r * tile
            return jnp.where(tok < nc, 0, 1 + jnp.maximum(tok - nc, 0) // dec_seq)
        return modrow

    modrow, modrow_tb = modrow_of(TM), modrow_of(TB)

    rc = -(-(1 + dec_batch) // 16) * 16
    cvec = jnp.zeros((rc, D), F32).at[0].set(c_ctx).at[1:1 + dec_batch].set(c)
    mods = _ada_call(cvec, w_ada, b_ada).reshape(depth, rc, 6, 1, D)

    w_even_b = w_in_even.astype(BF16)
    w_oute_b = w_out_even.astype(BF16)
    w_outo_b = w_out_odd.astype(BF16)
    kr_pad = jnp.zeros((n_odd, D, LANES), F32).at[:, :, 64:96].set(w_in_odd[:, :, 640:672])
    w_odd_b = jnp.concatenate([w_in_odd[:, :, :640], kr_pad], axis=-1).astype(BF16)
    wq = w_q_up.reshape(n_odd, MLA_Q_RANK, MLA_HEADS, HD + MLA_ROPE)
    wq_b = jnp.pad(wq, ((0, 0), (0, 0), (0, 0), (0, LANES - HD - MLA_ROPE))).reshape(
        n_odd, MLA_Q_RANK, MLA_QW).astype(BF16)
    wkv = w_kv_up.reshape(n_odd, MLA_KV_RANK, MLA_HEADS, 2 * HD)
    wk_b = jnp.pad(wkv[..., :HD], ((0, 0), (0, 0), (0, 0), (0, LANES - HD))).reshape(
        n_odd, MLA_KV_RANK, MLA_QW).astype(BF16)
    wv_b = wkv[..., HD:].reshape(n_odd, MLA_KV_RANK, D).astype(BF16)
    w_router_b = jnp.pad(w_router, ((0, 0), (0, 0), (0, LANES - N_EXPERTS))).astype(BF16)
    even_tabs = _rope_tables(dec_seq, 16, _even_lane)
    mla_tabs = _rope_tables(dec_seq, 8, _mla_lane)
    ckr = jnp.zeros((dec_batch, n_odd, past, LANES), F32).at[..., 64:96].set(cache_mla_krope)
    even_caches = (cache_diff_k, cache_diff_v, cache_swa_k, cache_swa_v)

    mix_gains = jnp.concatenate([norm_mix, final_norm[None]], axis=0).reshape(depth + 1, 1, D)
    ffn_gains = norm_ffn.reshape(depth, 1, D)
    q_gains = mla_q_norm.reshape(n_odd, 1, MLA_Q_RANK)
    kv_gains = mla_kv_norm.reshape(n_odd, 1, MLA_KV_RANK)
    sublns = diff_subln.reshape(n_even, 1, 2 * HD)

    xa, xb, xb_off = x_prompt.reshape(nc, D), x_sample.reshape(ns, D), 0
    h = _norm_mod_call(xa, xb, mods, 0, mix_gains, modrow)
    even_state = odd_state = None
    y_prompt = y_sample = None
    for i in range(depth):
        j = i // 2
        if i % 2 == 0:
            qkv_c, *even_state = _proj_even_call(h, w_even_b, j, 0, nc, None, dec_seq,
                                                 prev=even_state, seq=seq)
            (qkv_l,) = _proj_even_call(h, w_even_b, j, nc, ns, even_tabs, dec_seq)
            li = _lambda_init(i)
            o_c = _attn_even_call(qkv_c, None, j, diff_lambda, sublns, swa_sink, batch, seq, 0, li)
            o_l = _attn_even_call(qkv_l, even_caches, j, diff_lambda, sublns, swa_sink,
                                  dec_batch, dec_seq, past, li)
            w_out = w_oute_b
        else:
            q_c, *odd_state = _proj_odd_call(h, w_odd_b, q_gains, wq_b, kv_gains, j, 0, nc, None,
                                             dec_seq, prev=odd_state, seq=seq)
            q_l, ckv_l, kr_l = _proj_odd_call(h, w_odd_b, q_gains, wq_b, kv_gains, j, nc, ns,
                                              mla_tabs, dec_seq)
            o_c = _attn_odd_call(q_c, odd_state[0], odd_state[1], None, j, wk_b, wv_b, batch, seq, 0)
            o_l = _attn_odd_call(q_l, ckv_l, kr_l, (cache_mla_ckv, ckr), j, wk_b, wv_b,
                                 dec_batch, dec_seq, past)
            w_out = w_outo_b
        x, h2, aff = _outproj_call(xa, xb, xb_off, o_c, o_l, w_out, j, mods, i, ffn_gains,
                                   w_router_b, modrow)
        rowid, gate, tab = _select_call(aff, 2, ntok)
        xe = _dispatch_call(tab, h2, rowid, 2, ntok)
        y = _ffn_call(tab, xe, w_gate_exp, w_up_exp, w_down_exp, i, 2, ntok)
        final = i == depth - 1
        nxt = i if final else i + 1
        out_a, out_b = _combine_call(tab, y, rowid, gate, x, mods, i, nxt, mix_gains,
                                     depth if final else nxt, modrow_tb, 2, ntok, final)
        if final:
            y_prompt = out_a.reshape(batch, seq, D)
            y_sample = out_b.reshape(dec_batch, dec_seq, D)
        else:
            x, h = out_a, out_b
            xa, xb, xb_off = x, x, nc // TM

    kd, vd, ks, vs = even_state
    ckv_new, kr_new = odd_state
    return (y_prompt, y_sample,
            kd.reshape(batch, n_even, seq, DIFF_HEADS, 2 * HD),
            vd.reshape(batch, n_even, seq, DIFF_HEADS, 2 * HD),
            ks.reshape(batch, n_even, seq, 2, HD), vs.reshape(batch, n_even, seq, 2, HD),
            ckv_new, kr_new[..., 64:96])
```

```python
import functools
import math

import jax
import jax.numpy as jnp
import numpy as np
from jax import lax
from jax.experimental import pallas as pl
from jax.experimental.pallas import tpu as pltpu

F32 = jnp.float32
BF16 = jnp.bfloat16
I32 = jnp.int32

D = 1024
HD = 64
GRID_W = 64
WINDOW = 128
DIFF_HEADS = 4
SWA_HEADS = 8
MLA_HEADS = 16
MLA_Q_RANK = 384
MLA_KV_RANK = 256
MLA_ROPE = 32
N_EXPERTS = 16
EXPERT_FF = 512
EC_FACTOR = 2
ROPE_BASE = 10000.0
EPS = 1e-6
NEG_INF = -1e30
LOG2E = math.log2(math.e)
EVEN_IN = 2304
LANES = 128
TM = 512
TQ = 256
TB = 256
SEG = 16
VMEM_LIMIT = 56 * 1024 * 1024


def _cparams(sem, vmem=VMEM_LIMIT):
    return pltpu.CompilerParams(dimension_semantics=sem, vmem_limit_bytes=vmem)


def _dot(a, b):
    return jnp.dot(a, b, preferred_element_type=F32)


def _dot_nt(a, b):
    return lax.dot_general(a, b, (((1,), (1,)), ((), ())), preferred_element_type=F32)


def _silu(x):
    return x / (1.0 + jnp.exp(-x))


def _rms(x, g):
    ms = jnp.mean(x * x, axis=-1, keepdims=True)
    return x * lax.rsqrt(ms + EPS) * g


def _modulate(x, g, shift, scale):
    return _rms(x, g) * (1.0 + scale) + shift


def _lambda_init(layer):
    return 0.8 - 0.6 * math.exp(-0.3 * layer)


def _ada_kernel(c_ref, w_ref, b_ref, o_ref):
    s = _silu(c_ref[...]).astype(BF16)
    o_ref[...] = _dot(s, w_ref[...].astype(BF16)) + b_ref[...]


def _ada_call(cvec, w_ada, b_ada):
    depth, _, n6 = w_ada.shape
    rc = cvec.shape[0]
    tn = 512
    return pl.pallas_call(
        _ada_kernel,
        grid=(depth, n6 // tn),
        in_specs=[pl.BlockSpec((rc, D), lambda i, n: (0, 0)),
                  pl.BlockSpec((None, D, tn), lambda i, n: (i, 0, n)),
                  pl.BlockSpec((None, 1, tn), lambda i, n: (i, 0, n))],
        out_specs=pl.BlockSpec((None, rc, tn), lambda i, n: (i, 0, n)),
        out_shape=jax.ShapeDtypeStruct((depth, rc, n6), F32),
        compiler_params=_cparams(("parallel", "parallel")),
        name="ada",
    )(cvec, w_ada, b_ada.reshape(depth, 1, n6))


def _two_part_specs(rows_a, rows_b, width, off_b=0):
    na, nb_ = rows_a // TM, rows_b // TM
    return (pl.BlockSpec((TM, width), lambda r: (jnp.minimum(r, na - 1), 0)),
            pl.BlockSpec((TM, width), lambda r: (off_b + jnp.clip(r - na, 0, nb_ - 1), 0)))


def _norm_mod_kernel(xa_ref, xb_ref, mod_ref, g_ref, h_ref, *, ntile_a):
    def run(x_ref):
        h_ref[...] = _modulate(x_ref[...], g_ref[...], mod_ref[0], mod_ref[1]).astype(BF16)

    @pl.when(pl.program_id(0) < ntile_a)
    def _():
        run(xa_ref)

    @pl.when(pl.program_id(0) >= ntile_a)
    def _():
        run(xb_ref)


def _norm_mod_call(xa, xb, mods, layer, gains, modrow):
    na, nb_ = xa.shape[0], xb.shape[0]
    return pl.pallas_call(
        functools.partial(_norm_mod_kernel, ntile_a=na // TM),
        grid=((na + nb_) // TM,),
        in_specs=[*_two_part_specs(na, nb_, D),
                  pl.BlockSpec((None, None, 6, 1, D), lambda r: (layer, modrow(r), 0, 0, 0)),
                  pl.BlockSpec((None, 1, D), lambda r: (layer, 0, 0))],
        out_specs=pl.BlockSpec((TM, D), lambda r: (r, 0)),
        out_shape=jax.ShapeDtypeStruct((na + nb_, D), BF16),
        compiler_params=_cparams(("parallel",)),
        name="norm_mod",
    )(xa, xb, mods, gains)


def _rope_tables(dec_seq, half, lane_of_dim):
    pos = jnp.arange(dec_seq)
    row = (pos // GRID_W).astype(F32)
    col = (pos % GRID_W).astype(F32)
    inv = ROPE_BASE ** (-(jnp.arange(half, dtype=F32) / half))
    ang = jnp.stack([row[:, None] * inv[None, :], col[:, None] * inv[None, :]])
    info = [lane_of_dim(lane) for lane in range(LANES)]
    axis = np.array([0 if i is None else i[0] for i in info])
    freq = np.array([0 if i is None else i[1] for i in info])
    first = np.array([i is not None and not i[2] for i in info])[None, :]
    second = np.array([i is not None and i[2] for i in info])[None, :]
    lane_ang = ang[axis, :, freq].T
    cos, sin = jnp.cos(lane_ang), jnp.sin(lane_ang)
    return (jnp.where(first | second, cos, 1.0), jnp.where(first, -sin, 0.0),
            jnp.where(second, sin, 0.0))


def _even_lane(lane):
    j = lane % HD
    axis, jj = j // 32, j % 32
    return axis, jj % 16, jj >= 16


def _mla_lane(lane):
    if lane < 64 or lane >= 96:
        return None
    jj = lane - 64
    axis, k = jj // 16, jj % 16
    return axis, k % 8, k >= 8


def _rope(x, c, s1, s2, shift):
    return x * c + pltpu.roll(x, LANES - shift, 1) * s1 + pltpu.roll(x, shift, 1) * s2


_EVEN_ROPE_TILES = tuple(range(0, 8)) + tuple(range(12, 17))
_EVEN_Q_TILES = tuple(range(0, 4)) + tuple(range(12, 16))


def _append_layer(prev_refs, out_refs, new_values):
    for i, (out_ref, new) in enumerate(zip(out_refs, new_values)):
        nbatch, nlayers, seq, width = out_ref.shape
        if prev_refs:
            out_ref[:, 0:nlayers - 1] = prev_refs[i][...]
        out_ref[:, nlayers - 1] = new.reshape(nbatch, seq, width)


def _state_specs(prev, widths, seq):
    nlayers = 1 if prev is None else prev[0].shape[1] + 1
    per = TM // seq
    ins = [] if prev is None else [pl.BlockSpec((per, nlayers - 1, seq, w), lambda r: (r, 0, 0, 0))
                                   for w in widths]
    outs = [pl.BlockSpec((per, nlayers, seq, w), lambda r: (r, 0, 0, 0)) for w in widths]
    return ins, outs, nlayers


def _proj_even_kernel(*refs, rope, caches, nprev):
    h_ref, w_ref = refs[0], refs[1]
    pos = 2
    if rope:
        c_ref, s1_ref, s2_ref = refs[2:5]
        pos = 5
    prev_refs = refs[pos:pos + nprev]
    pos += nprev
    qkv_ref = refs[pos]
    res = _dot(h_ref[...], w_ref[...])
    scale = HD ** -0.5 * LOG2E
    for t in range(EVEN_IN // LANES):
        x = res[:, t * LANES:(t + 1) * LANES]
        if rope and t in _EVEN_ROPE_TILES:
            x = _rope(x, c_ref[...], s1_ref[...], s2_ref[...], 16)
        if t in _EVEN_Q_TILES:
            x = x * scale
        qkv_ref[:, t * LANES:(t + 1) * LANES] = x.astype(BF16)
    if caches:
        _append_layer(prev_refs, refs[pos + 1:pos + 5],
                      (res[:, 512:1024], res[:, 1024:1536], res[:, 2048:2176], res[:, 2176:2304]))


def _proj_even_call(h, w, j, row0, nrows, tables, dec_seq, prev=None, seq=None):
    rope = tables is not None
    caches = not rope
    t0 = row0 // TM
    in_specs = [pl.BlockSpec((TM, D), lambda r: (t0 + r, 0)),
                pl.BlockSpec((None, D, EVEN_IN), lambda r: (j, 0, 0))]
    args = [h, w]
    if rope:
        per = dec_seq // TM
        for _ in range(3):
            in_specs.append(pl.BlockSpec((TM, LANES), lambda r: (r % per, 0)))
        args += list(tables)
    out_specs = [pl.BlockSpec((TM, EVEN_IN), lambda r: (r, 0))]
    out_shape = [jax.ShapeDtypeStruct((nrows, EVEN_IN), BF16)]
    nprev = 0
    if caches:
        widths = (512, 512, LANES, LANES)
        ins, outs, nlayers = _state_specs(prev, widths, seq)
        nprev = len(ins)
        in_specs += ins
        args += [] if prev is None else list(prev)
        out_specs += outs
        out_shape += [jax.ShapeDtypeStruct((nrows // seq, nlayers, seq, wd), F32) for wd in widths]
    return pl.pallas_call(
        functools.partial(_proj_even_kernel, rope=rope, caches=caches, nprev=nprev),
        grid=(nrows // TM,),
        in_specs=in_specs, out_specs=out_specs, out_shape=out_shape,
        compiler_params=_cparams(("parallel",)),
        name="proj_even_lat" if rope else "proj_even_ctx",
    )(*args)


ODD_IN_PAD = MLA_Q_RANK + MLA_KV_RANK + LANES
MLA_QW = MLA_HEADS * LANES


def _proj_odd_kernel(*refs, rope, nprev):
    h_ref, w_ref, qn_ref, wq_ref, kvn_ref = refs[:5]
    pos = 5
    if rope:
        c_ref, s1_ref, s2_ref = refs[5:8]
        pos = 8
    prev_refs = refs[pos:pos + nprev]
    pos += nprev
    q_ref, ckv_ref, kr_ref = refs[pos:pos + 3]
    res = _dot(h_ref[...], w_ref[...])
    cq = _rms(res[:, :MLA_Q_RANK], qn_ref[...]).astype(BF16)
    ckv = _rms(res[:, MLA_Q_RANK:MLA_Q_RANK + MLA_KV_RANK], kvn_ref[...])
    kr = res[:, MLA_Q_RANK + MLA_KV_RANK:]
    if rope:
        kr = _rope(kr, c_ref[...], s1_ref[...], s2_ref[...], 8)
        ckv_ref[...] = ckv
        kr_ref[...] = kr
    else:
        _append_layer(prev_refs, (ckv_ref, kr_ref), (ckv, kr))
    q = _dot(cq, wq_ref[...])
    scale = (HD + MLA_ROPE) ** -0.5 * LOG2E
    for t in range(MLA_HEADS):
        x = q[:, t * LANES:(t + 1) * LANES]
        if rope:
            x = _rope(x, c_ref[...], s1_ref[...], s2_ref[...], 8)
        q_ref[:, t * LANES:(t + 1) * LANES] = (x * scale).astype(BF16)


def _proj_odd_call(h, w_in, qn, wq, kvn, j, row0, nrows, tables, dec_seq, prev=None, seq=None):
    rope = tables is not None
    t0 = row0 // TM
    in_specs = [pl.BlockSpec((TM, D), lambda r: (t0 + r, 0)),
                pl.BlockSpec((None, D, ODD_IN_PAD), lambda r: (j, 0, 0)),
                pl.BlockSpec((None, 1, MLA_Q_RANK), lambda r: (j, 0, 0)),
                pl.BlockSpec((None, MLA_Q_RANK, MLA_QW), lambda r: (j, 0, 0)),
                pl.BlockSpec((None, 1, MLA_KV_RANK), lambda r: (j, 0, 0))]
    args = [h, w_in, qn, wq, kvn]
    if rope:
        per = dec_seq // TM
        for _ in range(3):
            in_specs.append(pl.BlockSpec((TM, LANES), lambda r: (r % per, 0)))
        args += list(tables)
    out_specs = [pl.BlockSpec((TM, MLA_QW), lambda r: (r, 0))]
    out_shape = [jax.ShapeDtypeStruct((nrows, MLA_QW), BF16)]
    widths = (MLA_KV_RANK, LANES)
    nprev = 0
    if rope:
        out_specs += [pl.BlockSpec((TM, wd), lambda r: (r, 0)) for wd in widths]
        out_shape += [jax.ShapeDtypeStruct((nrows, wd), F32) for wd in widths]
    else:
        ins, outs, nlayers = _state_specs(prev, widths, seq)
        nprev = len(ins)
        in_specs += ins
        args += [] if prev is None else list(prev)
        out_specs += outs
        out_shape += [jax.ShapeDtypeStruct((nrows // seq, nlayers, seq, wd), F32) for wd in widths]
    return pl.pallas_call(
        functools.partial(_proj_odd_kernel, rope=rope, nprev=nprev),
        grid=(nrows // TM,),
        in_specs=in_specs,
        out_specs=out_specs,
        out_shape=out_shape,
        compiler_params=_cparams(("parallel",)),
        name="proj_odd_lat" if rope else "proj_odd_ctx",
    )(*args)


def _ones_lane(half):
    return HD if half == 0 else 0


def _row_sum(e, o, half, from_matmul):
    if from_matmul:
        one = _ones_lane(half)
        return o[:, one:one + 1]
    return jnp.sum(e, axis=-1, keepdims=True)


def _half_values(v, half):
    lane = lax.broadcasted_iota(I32, (1, LANES), 1)
    keep = (lane < HD) if half == 0 else (lane >= HD)
    return jnp.where(keep, v, jnp.where(lane == _ones_lane(half), 1.0, 0.0)).astype(BF16)


def _attn_even_kernel(*refs, seq, past, lam_init, layer):
    latent = past > 0
    n = past + seq
    if latent:
        (qkv_ref, ck_ref, cv_ref, sk_ref, sv_ref, lam_ref, subln_ref, sink_ref,
         o_ref, kd, vd, ka, vl, vh) = refs
    else:
        qkv_ref, lam_ref, subln_ref, sink_ref, o_ref, kd, vd, ka, vl, vh = refs
    qi = pl.program_id(1)
    lo = lax.broadcasted_iota(I32, (1, LANES), 1) < HD

    @pl.when(qi == 0)
    def _build():
        chunk = 256
        for c0 in range(0, n, chunk):
            rows = slice(c0, c0 + chunk)
            if c0 < past:
                prow = slice(c0, c0 + chunk)
                for h in range(DIFF_HEADS):
                    kd[rows, h * LANES:(h + 1) * LANES] = ck_ref[prow, h, :].astype(BF16)
                    vd[rows, h * LANES:(h + 1) * LANES] = cv_ref[prow, h, :].astype(BF16)
                kt = jnp.concatenate([sk_ref[prow, 0, :], sk_ref[prow, 1, :]], axis=1)
                vt = jnp.concatenate([sv_ref[prow, 0, :], sv_ref[prow, 1, :]], axis=1)
            else:
                orow = slice(c0 - past, c0 - past + chunk)
                kd[rows, :] = qkv_ref[orow, 512:1024]
                vd[rows, :] = qkv_ref[orow, 1024:1536]
                kt = qkv_ref[orow, 2048:2176].astype(F32)
                vt = qkv_ref[orow, 2176:2304].astype(F32)
            kr = pltpu.roll(kt, HD, 1)
            vr = pltpu.roll(vt, HD, 1)
            ka[0, rows, :] = jnp.where(lo, kt, kr).astype(BF16)
            ka[1, rows, :] = jnp.where(lo, kr, kt).astype(BF16)
            vl[0, rows, :] = _half_values(vt, 0)
            vh[0, rows, :] = _half_values(vr, 1)
            vl[1, rows, :] = _half_values(vr, 0)
            vh[1, rows, :] = _half_values(vt, 1)

    r0 = pl.multiple_of(qi * TQ, TQ)
    lam = lam_ref[...]
    lam_full = (jnp.exp(jnp.sum(lam[0:1] * lam[1:2], axis=-1, keepdims=True))
                - jnp.exp(jnp.sum(lam[2:3] * lam[3:4], axis=-1, keepdims=True)) + lam_init)
    zero_b = jnp.zeros((), BF16)

    for h in range(DIFF_HEADS):
        cs = slice(h * LANES, (h + 1) * LANES)
        qt = qkv_ref[pl.ds(r0, TQ), cs]
        kh = kd[:, cs]
        es, rs = [], []
        for comp in range(2):
            qc = jnp.where(lo, qt, zero_b) if comp == 0 else jnp.where(lo, zero_b, qt)
            s = _dot_nt(qc, kh)
            m = jnp.max(s, axis=-1, keepdims=True)
            e = jnp.exp2(s - m)
            es.append(e)
            rs.append(1.0 / jnp.sum(e, axis=-1, keepdims=True))
        a = es[0] * rs[0] - es[1] * (lam_full * rs[1])
        o = _dot(a.astype(BF16), vd[:, cs])
        o = _rms(o, subln_ref[...]) * (1.0 - lam_init)
        o_ref[:, cs] = o.astype(BF16)

    nblk = seq // WINDOW
    dense = past if latent else seq
    if latent:
        per = TQ // WINDOW
        offsets = tuple(range(-1, per + 1))
        rr = lax.broadcasted_iota(I32, (TQ, WINDOW), 0)
        cc = lax.broadcasted_iota(I32, (TQ, WINDOW), 1)
        band, starts = {}, {}
        for d in offsets:
            blk = qi * per + d
            inside = jnp.logical_and(blk >= 0, blk < nblk)
            band[d] = jnp.logical_and(jnp.abs(rr - cc - d * WINDOW) <= WINDOW, inside)
            starts[d] = pl.multiple_of(past + jnp.clip(blk, 0, nblk - 1) * WINDOW, WINDOW)
    for i in range(SWA_HEADS // 2):
        hk = i // 2
        cs = slice(1536 + i * LANES, 1536 + (i + 1) * LANES)
        qt = qkv_ref[pl.ds(r0, TQ), cs]
        halves = []
        for half in range(2):
            qc = jnp.where(lo, qt, zero_b) if half == 0 else jnp.where(lo, zero_b, qt)
            vsel = vl if half == 0 else vh
            sink = sink_ref[layer, 2 * i + half] * LOG2E
            parts = [_dot_nt(qc, ka[hk, 0:dense, :])]
            if latent:
                for d in offsets:
                    s = _dot_nt(qc, ka[hk, pl.ds(starts[d], WINDOW), :])
                    parts.append(jnp.where(band[d], s, NEG_INF))
            s_all = jnp.concatenate(parts, axis=1) if len(parts) > 1 else parts[0]
            m = jnp.maximum(jnp.max(s_all, axis=-1, keepdims=True), sink)
            e = jnp.exp2(s_all - m)
            eb = e.astype(BF16)
            o = _dot(eb[:, 0:dense], vsel[hk, 0:dense, :])
            if latent:
                for k, d in enumerate(offsets):
                    o += _dot(eb[:, dense + k * WINDOW:dense + (k + 1) * WINDOW],
                              vsel[hk, pl.ds(starts[d], WINDOW), :])
            den = _row_sum(e, o, half, from_matmul=latent) + jnp.exp2(sink - m)
            halves.append(o * (1.0 / den))
        o_ref[:, 512 + i * LANES:512 + (i + 1) * LANES] = jnp.where(lo, halves[0], halves[1]).astype(BF16)


def _attn_even_call(qkv, caches, j, lam, subln, sink, nbatch, seq, past, lam_init):
    n = past + seq
    latent = past > 0
    in_specs = [pl.BlockSpec((seq, EVEN_IN), lambda b, q: (b, 0))]
    args = [qkv]
    if latent:
        ck, cv, sk, sv = caches
        in_specs += [pl.BlockSpec((None, None, past, DIFF_HEADS, 2 * HD), lambda b, q: (b, j, 0, 0, 0)),
                     pl.BlockSpec((None, None, past, DIFF_HEADS, 2 * HD), lambda b, q: (b, j, 0, 0, 0)),
                     pl.BlockSpec((None, None, past, 2, HD), lambda b, q: (b, j, 0, 0, 0)),
                     pl.BlockSpec((None, None, past, 2, HD), lambda b, q: (b, j, 0, 0, 0))]
        args += [ck, cv, sk, sv]
    in_specs += [pl.BlockSpec((None, 4, HD), lambda b, q: (j, 0, 0)),
                 pl.BlockSpec((None, 1, 2 * HD), lambda b, q: (j, 0, 0)),
                 pl.BlockSpec(memory_space=pltpu.SMEM)]
    args += [lam, subln, sink]
    return pl.pallas_call(
        functools.partial(_attn_even_kernel, seq=seq, past=past, lam_init=lam_init, layer=j),
        grid=(nbatch, seq // TQ),
        in_specs=in_specs,
        out_specs=pl.BlockSpec((TQ, D), lambda b, q: (b * (seq // TQ) + q, 0)),
        out_shape=jax.ShapeDtypeStruct((nbatch * seq, D), BF16),
        scratch_shapes=[pltpu.VMEM((n, 512), BF16), pltpu.VMEM((n, 512), BF16),
                        pltpu.VMEM((2, n, LANES), BF16), pltpu.VMEM((2, n, LANES), BF16),
                        pltpu.VMEM((2, n, LANES), BF16)],
        compiler_params=_cparams(("arbitrary", "arbitrary")),
        name="attn_even_lat" if latent else "attn_even_ctx",
    )(*args)


VT_ROWS = HD + SEG


def _attn_odd_kernel(*refs, seq, past):
    latent = past > 0
    n = past + seq
    if latent:
        q_ref, ckv_ref, kr_ref, cckv_ref, ckr_ref, wk_ref, wv_ref, o_ref, kf, vt = refs
    else:
        q_ref, ckv_ref, kr_ref, wk_ref, wv_ref, o_ref, kf, vlo, vhi = refs
    qi = pl.program_id(1)

    @pl.when(qi == 0)
    def _build():
        chunk = 256
        for c0 in range(0, n, chunk):
            rows = slice(c0, c0 + chunk)
            if c0 < past:
                ckv = cckv_ref[c0:c0 + chunk, :].astype(BF16)
                kr = ckr_ref[c0:c0 + chunk, :]
            else:
                ckv = ckv_ref[c0 - past:c0 - past + chunk, :].astype(BF16)
                kr = kr_ref[c0 - past:c0 - past + chunk, :]
            kk = _dot(ckv, wk_ref[...])
            for h in range(MLA_HEADS):
                cs = slice(h * LANES, (h + 1) * LANES)
                kf[rows, cs] = (kk[:, cs] + kr).astype(BF16)
            vv = _dot(ckv, wv_ref[...])
            for i in range(MLA_HEADS // 2):
                cs = slice(i * LANES, (i + 1) * LANES)
                if latent:
                    pair = vv[:, cs].T.astype(BF16)
                    for half in range(2):
                        r = (2 * i + half) * VT_ROWS
                        vt[r:r + HD, rows] = pair[half * HD:(half + 1) * HD, :]
                else:
                    vlo[rows, cs] = _half_values(vv[:, cs], 0)
                    vhi[rows, cs] = _half_values(vv[:, cs], 1)
        if latent:
            for h in range(MLA_HEADS):
                vt[h * VT_ROWS + HD:(h + 1) * VT_ROWS, :] = jnp.ones((SEG, n), BF16)

    r0 = pl.multiple_of(qi * TQ, TQ)
    lo = lax.broadcasted_iota(I32, (1, LANES), 1) < HD
    for i in range(MLA_HEADS // 2):
        halves = []
        for half in range(2):
            h = 2 * i + half
            cs = slice(h * LANES, (h + 1) * LANES)
            s = _dot_nt(q_ref[pl.ds(r0, TQ), cs], kf[:, cs])
            m = jnp.max(s, axis=-1, keepdims=True)
            e = jnp.exp2(s - m)
            if latent:
                ot = _dot_nt(vt[h * VT_ROWS:(h + 1) * VT_ROWS, :], e.astype(BF16))
                halves.append(ot[0:HD, :] * (1.0 / ot[HD:HD + 1, :]))
            else:
                vsel = vlo if half == 0 else vhi
                o = _dot(e.astype(BF16), vsel[:, i * LANES:(i + 1) * LANES])
                halves.append(o * (1.0 / jnp.sum(e, axis=-1, keepdims=True)))
        if latent:
            tile = jnp.concatenate(halves, axis=0).T
        else:
            tile = jnp.where(lo, halves[0], halves[1])
        o_ref[:, i * LANES:(i + 1) * LANES] = tile.astype(BF16)


def _attn_odd_call(q, ckv, kr, caches, j, wk, wv, nbatch, seq, past):
    n = past + seq
    latent = past > 0
    in_specs = [pl.BlockSpec((seq, MLA_QW), lambda b, qq: (b, 0))]
    if latent:
        in_specs += [pl.BlockSpec((seq, MLA_KV_RANK), lambda b, qq: (b, 0)),
                     pl.BlockSpec((seq, LANES), lambda b, qq: (b, 0))]
    else:
        last = ckv.shape[1] - 1
        in_specs += [pl.BlockSpec((None, None, seq, MLA_KV_RANK), lambda b, qq: (b, last, 0, 0)),
                     pl.BlockSpec((None, None, seq, LANES), lambda b, qq: (b, last, 0, 0))]
    args = [q, ckv, kr]
    if latent:
        in_specs += [pl.BlockSpec((None, None, past, MLA_KV_RANK), lambda b, qq: (b, j, 0, 0)),
                     pl.BlockSpec((None, None, past, LANES), lambda b, qq: (b, j, 0, 0))]
        args += list(caches)
    in_specs += [pl.BlockSpec((None, MLA_KV_RANK, MLA_QW), lambda b, qq: (j, 0, 0)),
                 pl.BlockSpec((None, MLA_KV_RANK, D), lambda b, qq: (j, 0, 0))]
    args += [wk, wv]
    return pl.pallas_call(
        functools.partial(_attn_odd_kernel, seq=seq, past=past),
        grid=(nbatch, seq // TQ),
        in_specs=in_specs,
        out_specs=pl.BlockSpec((TQ, D), lambda b, qq: (b * (seq // TQ) + qq, 0)),
        out_shape=jax.ShapeDtypeStruct((nbatch * seq, D), BF16),
        scratch_shapes=([pltpu.VMEM((n, MLA_QW), BF16), pltpu.VMEM((MLA_HEADS * VT_ROWS, n), BF16)]
                        if latent else
                        [pltpu.VMEM((n, MLA_QW), BF16), pltpu.VMEM((n, D), BF16),
                         pltpu.VMEM((n, D), BF16)]),
        compiler_params=_cparams(("arbitrary", "arbitrary")),
        name="attn_odd_lat" if latent else "attn_odd_ctx",
    )(*args)


def _outproj_kernel(xa_ref, xb_ref, oa_ref, ob_ref, w_ref, mod_ref, g_ref, wr_ref,
                    xo_ref, h_ref, aff_ref, *, ntile_a):
    def run(x_ref, o_ref):
        x = x_ref[...] + mod_ref[2] * _dot(o_ref[...], w_ref[...])
        xo_ref[...] = x
        h = _modulate(x, g_ref[...], mod_ref[3], mod_ref[4]).astype(BF16)
        h_ref[...] = h
        logits = _dot(h, wr_ref[...])
        lane = lax.broadcasted_iota(I32, (TM, LANES), 1)
        lg = jnp.where(lane < N_EXPERTS, logits, -jnp.inf)
        e = jnp.exp(lg - jnp.max(lg, axis=-1, keepdims=True))
        aff = e / jnp.sum(e, axis=-1, keepdims=True)
        for c in range(TM // TB):
            aff_ref[c] = aff[c * TB:(c + 1) * TB, :].T[0:N_EXPERTS, :]

    @pl.when(pl.program_id(0) < ntile_a)
    def _():
        run(xa_ref, oa_ref)

    @pl.when(pl.program_id(0) >= ntile_a)
    def _():
        run(xb_ref, ob_ref)


def _outproj_call(xa, xb, xb_off, o_a, o_b, w, widx, mods, layer, gains, w_router, modrow):
    na, nb_ = o_a.shape[0], o_b.shape[0]
    t = na + nb_
    assert TM % TB == 0
    return pl.pallas_call(
        functools.partial(_outproj_kernel, ntile_a=na // TM),
        grid=(t // TM,),
        in_specs=[*_two_part_specs(na, nb_, D, xb_off), *_two_part_specs(na, nb_, D),
                  pl.BlockSpec((None, D, D), lambda r: (widx, 0, 0)),
                  pl.BlockSpec((None, None, 6, 1, D), lambda r: (layer, modrow(r), 0, 0, 0)),
                  pl.BlockSpec((None, 1, D), lambda r: (layer, 0, 0)),
                  pl.BlockSpec((None, D, LANES), lambda r: (layer, 0, 0))],
        out_specs=[pl.BlockSpec((TM, D), lambda r: (r, 0)),
                   pl.BlockSpec((TM, D), lambda r: (r, 0)),
                   pl.BlockSpec((TM // TB, N_EXPERTS, TB), lambda r: (r, 0, 0))],
        out_shape=[jax.ShapeDtypeStruct((t, D), F32), jax.ShapeDtypeStruct((t, D), BF16),
                   jax.ShapeDtypeStruct((t // TB, N_EXPERTS, TB), F32)],
        compiler_params=_cparams(("parallel",)),
        name="outproj",
    )(xa, xb, o_a, o_b, w, mods, gains, w_router)


def _select_kernel(aff_ref, rowid_ref, gate_ref, tab_ref, *, nb, cap):
    ne = N_EXPERTS
    nr = nb * ne
    a = aff_ref[...].reshape(nr, TB)
    ri = lax.broadcasted_iota(I32, (nr, nr), 0)
    ci = lax.broadcasted_iota(I32, (nr, nr), 1)
    same_e = (ri & (ne - 1)) == (ci & (ne - 1))
    same_b = (ri >> 4) == (ci >> 4)
    m_e = jnp.where(same_e, 1.0, 0.0).astype(BF16)
    m_b = jnp.where(same_b, 1.0, 0.0).astype(BF16)
    m_a = jnp.where(jnp.logical_and(same_e, ci < ri), 1.0, 0.0).astype(BF16)
    m_o = jnp.where(jnp.logical_and(same_b, ci < ri), 1.0, 0.0).astype(BF16)
    ui = lax.broadcasted_iota(I32, (TB, TB), 0)
    uj = lax.broadcasted_iota(I32, (TB, TB), 1)
    upper = jnp.where(ui < uj, 1.0, 0.0).astype(BF16)

    def rows_to_lanes(col):
        return jnp.broadcast_to(col, (nr, LANES)).astype(BF16)

    wide = jnp.concatenate([aff_ref[b] for b in range(nb)], axis=1)

    def count_ge(value):
        return jnp.sum(jnp.where(wide >= value, 1.0, 0.0), axis=-1, keepdims=True)

    def bisect(i, v):
        cand = v | jnp.left_shift(jnp.int32(1), 30 - i)
        return jnp.where(count_ge(pltpu.bitcast(cand, F32)) >= cap, cand, v)

    thr = lax.fori_loop(0, 31, bisect, jnp.zeros((ne, 1), I32))

    def refine(i, lohi):
        lo_e, hi_e = lohi
        mid = 0.5 * (lo_e + hi_e)
        take = count_ge(mid) >= cap
        return jnp.where(take, mid, lo_e), jnp.where(take, hi_e, mid)

    lo_e, hi_e = lax.fori_loop(0, TIE_STEPS, refine,
                               (pltpu.bitcast(thr, F32), pltpu.bitcast(thr + 1, F32)))
    lo_v = jnp.concatenate([lo_e] * nb, axis=0)
    hi_v = jnp.concatenate([hi_e] * nb, axis=0)
    gt = jnp.where(a >= hi_v, 1.0, 0.0)
    eq = jnp.where(jnp.logical_and(a >= lo_v, a < hi_v), 1.0, 0.0)
    n_gt = _dot(m_e, rows_to_lanes(jnp.sum(gt, axis=-1, keepdims=True)))[:, 0:1]
    need = cap - n_gt
    eq_before = (_dot(m_a, rows_to_lanes(jnp.sum(eq, axis=-1, keepdims=True)))[:, 0:1]
                 + _dot(eq.astype(BF16), upper))
    sel = jnp.where(jnp.logical_and(eq > 0.0, eq_before < need), 1.0, gt)
    local = _dot(sel.astype(BF16), upper)
    cnt = jnp.sum(sel, axis=-1, keepdims=True)
    seg = jnp.floor((cnt + (SEG - 1)) * (1.0 / SEG)) * SEG
    segb = rows_to_lanes(seg)
    over = jnp.maximum(seg - WINR, 0.0)
    overb = rows_to_lanes(over)
    off_over = FIRST_ROWS + _dot(m_o, overb)[:, 0:1]
    off_buf = _dot(m_a, segb)[:, 0:1]
    over_blk = _dot(m_b, overb)[:, 0:1]
    rows_exp = _dot(m_e, segb)[:, 0:1]
    tiles_exp = jnp.floor((rows_exp + (TF - 1)) / TF)
    expert = (lax.broadcasted_iota(I32, (nr, 1), 0) & (ne - 1)).astype(F32)
    row = jnp.where(local < WINR, expert * WINR + local, off_over + local - WINR)
    rowid_ref[...] = jnp.where(sel > 0.0, row, -1.0).astype(I32).reshape(nb, ne, TB)
    gate_ref[...] = jnp.where(sel > 0.0, a, 0.0).reshape(nb, ne, TB)
    tl = lax.broadcasted_iota(I32, (nr, LANES), 1)
    tab = jnp.where(tl == 0, seg, jnp.where(tl == 1, off_over, jnp.where(
        tl == 2, off_buf, jnp.where(tl == 3, over_blk, jnp.where(tl == 4, rows_exp, tiles_exp)))))
    tab_ref[...] = tab.T[0:8, :].astype(I32)


def _select_call(aff, ngroups, ntok):
    nb = ntok // TB
    cap = EC_FACTOR * ntok // N_EXPERTS
    nr = nb * N_EXPERTS
    return pl.pallas_call(
        functools.partial(_select_kernel, nb=nb, cap=cap),
        grid=(ngroups,),
        in_specs=[pl.BlockSpec((nb, N_EXPERTS, TB), lambda g: (g, 0, 0))],
        out_specs=[pl.BlockSpec((None, nb, N_EXPERTS, TB), lambda g: (g, 0, 0, 0)),
                   pl.BlockSpec((None, nb, N_EXPERTS, TB), lambda g: (g, 0, 0, 0)),
                   pl.BlockSpec((None, 8, nr), lambda g: (g, 0, 0))],
        out_shape=[jax.ShapeDtypeStruct((ngroups, nb, N_EXPERTS, TB), I32),
                   jax.ShapeDtypeStruct((ngroups, nb, N_EXPERTS, TB), F32),
                   jax.ShapeDtypeStruct((ngroups, 8, nr), I32)],
        compiler_params=_cparams(("arbitrary",)),
        name="select",
    )(aff)


STACK_ROWS = N_EXPERTS * TB


FIRST_ROWS = 3 * TB


WINR = 3 * SEG
assert N_EXPERTS * WINR == FIRST_ROWS
TF = 672
TIE_STEPS = 12


def _window_rows(rowid_ref, dst, value_ref=None):
    for e in range(N_EXPERTS):
        rid_e = rowid_ref[e:e + 1, :]
        val_e = 1.0 if value_ref is None else value_ref[e:e + 1, :]
        hit = rid_e == lax.broadcasted_iota(I32, (WINR, TB), 0) + e * WINR
        dst[e * WINR:(e + 1) * WINR, :] = jnp.where(hit, val_e, 0.0).astype(dst.dtype)


class _Table:
    def __init__(self, tab_s, row, nb, per_block=False):
        self.tab_s, self.row, self.per = tab_s, row, nb if per_block else nb * N_EXPERTS
        self.scale = N_EXPERTS if per_block else 1

    def __getitem__(self, k):
        if self.per & (self.per - 1) == 0:
            hi, low = lax.shift_right_logical(k, self.per.bit_length() - 1), k & (self.per - 1)
        else:
            hi, low = lax.div(k, self.per), lax.rem(k, self.per)
        return self.tab_s[hi, self.row, low * self.scale]


def _tables(tab_s, nb):
    return (_Table(tab_s, 0, nb), _Table(tab_s, 1, nb), _Table(tab_s, 2, nb),
            _Table(tab_s, 3, nb, per_block=True))


def _overflow_groups(seg_s, k):
    return lax.div(jnp.maximum(seg_s[k] - WINR, 0), SEG)


def _overflow_rows(seg_s, offo_s, step, over, rowid_ref, dst, value_ref=None):
    def zero(i, carry):
        r0 = pl.multiple_of(FIRST_ROWS + i * SEG, SEG)
        dst[pl.ds(r0, SEG), :] = jnp.zeros((SEG, TB), dst.dtype)
        return carry

    lax.fori_loop(0, ((over + TB - 1) // TB) * (TB // SEG), zero, 0)
    for e in range(N_EXPERTS):
        k = step * N_EXPERTS + e
        rid_e = rowid_ref[e:e + 1, :]
        val_e = 1.0 if value_ref is None else value_ref[e:e + 1, :]

        def group(i, carry, off=offo_s[k], rid_e=rid_e, val_e=val_e):
            r0 = pl.multiple_of(off + i * SEG, SEG)
            hit = rid_e == lax.broadcasted_iota(I32, (SEG, TB), 0) + r0
            dst[pl.ds(r0, SEG), :] = jnp.where(hit, val_e, 0.0).astype(dst.dtype)
            return carry

        lax.fori_loop(0, _overflow_groups(seg_s, k), group, 0)


def _wait_rows(rows, make_copy):
    def big(i, carry):
        make_copy(TB).wait()
        return carry

    def small(i, carry):
        make_copy(SEG).wait()
        return carry

    lax.fori_loop(0, lax.div(rows, TB), big, 0)
    lax.fori_loop(0, lax.div(lax.rem(rows, TB), SEG), small, 0)


def _dispatch_kernel(tab_s, h_ref, rowid_ref, xe_hbm,
                     onehot, stack, zbuf, sem, zsem, *, nb, nsteps, cap):
    g = pl.program_id(0)
    b = pl.program_id(1)
    step = g * nb + b
    slot = lax.rem(step, 2)
    seg_s, offo_s, offb_s, over_s = _tables(tab_s, nb)
    over = over_s[step]
    xrows = xe_hbm.shape[2]

    def wait_slot(nrows, sl):
        _wait_rows(nrows, lambda n: pltpu.make_async_copy(
            stack.at[sl, pl.ds(0, n)], xe_hbm.at[0, 0, pl.ds(0, n)], sem.at[sl]))

    @pl.when(step == 0)
    def _init():
        stack[...] = jnp.zeros_like(stack)

    @pl.when(b == 0)
    def _zero_unused():
        zbuf[...] = jnp.zeros_like(zbuf)
        for e in range(N_EXPERTS):
            pltpu.make_async_copy(zbuf, xe_hbm.at[g, e, pl.ds(cap, xrows - cap)], zsem).start()

    _window_rows(rowid_ref, onehot)
    h = h_ref[...]
    stack[slot, 0:FIRST_ROWS, :] = _dot(onehot[0:FIRST_ROWS, :], h).astype(BF16)

    @pl.when(over > 0)
    def _overflow():
        _overflow_rows(seg_s, offo_s, step, over, rowid_ref, onehot)

        def chunk(c, carry):
            base = pl.multiple_of(FIRST_ROWS + c * TB, TB)
            stack[slot, pl.ds(base, TB), :] = _dot(onehot[pl.ds(base, TB), :], h).astype(BF16)
            return carry

        lax.fori_loop(0, (over + TB - 1) // TB, chunk, 0)

    @pl.when(b == 0)
    def _zero_unused_done():
        for e in range(N_EXPERTS):
            pltpu.make_async_copy(zbuf, xe_hbm.at[g, e, pl.ds(cap, xrows - cap)], zsem).wait()

    @pl.when(step >= 1)
    def _previous_landed():
        wait_slot(FIRST_ROWS + over_s[step - 1], 1 - slot)

    for e in range(N_EXPERTS):
        k = step * N_EXPERTS + e
        pltpu.make_async_copy(
            stack.at[slot, e * WINR:(e + 1) * WINR],
            xe_hbm.at[g, e, pl.ds(pl.multiple_of(offb_s[k], SEG), WINR)], sem.at[slot]).start()

    @pl.when(over > 0)
    def _overflow_copies():
        for e in range(N_EXPERTS):
            k = step * N_EXPERTS + e

            def one(i, carry, e=e, k=k):
                pltpu.make_async_copy(
                    stack.at[slot, pl.ds(pl.multiple_of(offo_s[k] + i * SEG, SEG), SEG)],
                    xe_hbm.at[g, e, pl.ds(pl.multiple_of(offb_s[k] + WINR + i * SEG, SEG), SEG)],
                    sem.at[slot]).start()
                return carry

            lax.fori_loop(0, _overflow_groups(seg_s, k), one, 0)

    @pl.when(step == nsteps - 1)
    def _drain():
        wait_slot(FIRST_ROWS + over, slot)


def _expert_rows(ntok):
    cap = EC_FACTOR * ntok // N_EXPERTS
    worst = cap + (ntok // TB) * (SEG - 1)
    tiles = -(-(worst + WINR) // TF)
    assert cap % SEG == 0 and cap >= WINR and tiles * TF > cap
    return cap, tiles


def _dispatch_call(tab, h, rowid, ngroups, ntok):
    nb = ntok // TB
    cap, tiles = _expert_rows(ntok)
    xrows = tiles * TF
    grid_spec = pltpu.PrefetchScalarGridSpec(
        num_scalar_prefetch=1,
        grid=(ngroups, nb),
        in_specs=[pl.BlockSpec((TB, D), lambda g, b, *_: (g * nb + b, 0)),
                  pl.BlockSpec((None, None, N_EXPERTS, TB), lambda g, b, *_: (g, b, 0, 0))],
        out_specs=pl.BlockSpec(memory_space=pl.ANY),
        scratch_shapes=[pltpu.VMEM((STACK_ROWS, TB), BF16), pltpu.VMEM((2, STACK_ROWS, D), BF16),
                        pltpu.VMEM((xrows - cap, D), BF16),
                        pltpu.SemaphoreType.DMA((2,)), pltpu.SemaphoreType.DMA])
    return pl.pallas_call(
        functools.partial(_dispatch_kernel, nb=nb, nsteps=ngroups * nb, cap=cap),
        grid_spec=grid_spec,
        out_shape=jax.ShapeDtypeStruct((ngroups, N_EXPERTS, xrows, D), BF16),
        compiler_params=_cparams(("arbitrary", "arbitrary")),
        name="dispatch",
    )(tab, h, rowid)


def _ffn_kernel(tab_s, xe_ref, wg_hbm, wu_hbm, wd_hbm, y_ref, wg32, wu32, wd32, wgb, wub, wdb, sem,
                *, layer):
    e = pl.program_id(0)
    g = pl.program_id(1)
    j = pl.program_id(2)

    def weight_copies(ee, sl):
        return (pltpu.make_async_copy(wg_hbm.at[layer, ee], wg32.at[sl], sem.at[sl]),
                pltpu.make_async_copy(wu_hbm.at[layer, ee], wu32.at[sl], sem.at[sl]),
                pltpu.make_async_copy(wd_hbm.at[layer, ee], wd32.at[sl], sem.at[sl]))

    @pl.when(jnp.logical_and(g == 0, j == 0))
    def _weights():
        sl = lax.rem(e, 2)

        @pl.when(e == 0)
        def _():
            for cp in weight_copies(e, sl):
                cp.start()

        @pl.when(e + 1 < N_EXPERTS)
        def _():
            for cp in weight_copies(e + 1, 1 - sl):
                cp.start()

        for cp in weight_copies(e, sl):
            cp.wait()
        wgb[...] = wg32[sl].astype(BF16)
        wub[...] = wu32[sl].astype(BF16)
        wdb[...] = wd32[sl].astype(BF16)

    live = j < tab_s[g, 5, e]

    @pl.when(live)
    def _run():
        x = xe_ref[...]
        hid = (_silu(_dot(x, wgb[...])) * _dot(x, wub[...])).astype(BF16)
        y_ref[...] = _dot(hid, wdb[...]).astype(BF16)

    @pl.when(jnp.logical_not(live))
    def _skip():
        y_ref[...] = jnp.zeros_like(y_ref)


def _ffn_call(tab, xe, wg, wu, wd, layer, ngroups, ntok):
    _, tiles = _expert_rows(ntok)

    def xmap(e, g, j, tab_s):
        return (g, e, jnp.minimum(j, tab_s[g, 5, e] - 1), 0)

    grid_spec = pltpu.PrefetchScalarGridSpec(
        num_scalar_prefetch=1,
        grid=(N_EXPERTS, ngroups, tiles),
        in_specs=[pl.BlockSpec((None, None, TF, D), xmap),
                  pl.BlockSpec(memory_space=pl.ANY), pl.BlockSpec(memory_space=pl.ANY),
                  pl.BlockSpec(memory_space=pl.ANY)],
        out_specs=pl.BlockSpec((None, None, TF, D), lambda e, g, j, nt: (g, e, j, 0)),
        scratch_shapes=[pltpu.VMEM((2, D, EXPERT_FF), F32), pltpu.VMEM((2, D, EXPERT_FF), F32),
                        pltpu.VMEM((2, EXPERT_FF, D), F32),
                        pltpu.VMEM((D, EXPERT_FF), BF16), pltpu.VMEM((D, EXPERT_FF), BF16),
                        pltpu.VMEM((EXPERT_FF, D), BF16), pltpu.SemaphoreType.DMA((2,))])
    return pl.pallas_call(
        functools.partial(_ffn_kernel, layer=layer),
        grid_spec=grid_spec,
        out_shape=jax.ShapeDtypeStruct((ngroups, N_EXPERTS, tiles * TF, D), BF16),
        compiler_params=_cparams(("arbitrary", "arbitrary", "arbitrary")),
        name="ffn",
    )(tab, xe, wg, wu, wd)


def _combine_kernel(tab_s, y_hbm, rowid_ref, gate_ref, x_ref, mod_ref,
                    modn_ref, g_ref, xo_ref, h_ref, weights, stack, acc, sem, *, nb, nsteps, final):
    g = pl.program_id(0)
    b = pl.program_id(1)
    step = g * nb + b
    slot = lax.rem(step, 2)
    seg_s, offo_s, offb_s, over_s = _tables(tab_s, nb)
    over = over_s[step]

    def fetch(st, sl):
        gg = lax.div(st, nb)
        for e in range(N_EXPERTS):
            k = st * N_EXPERTS + e
            pltpu.make_async_copy(
                y_hbm.at[gg, e, pl.ds(pl.multiple_of(offb_s[k], SEG), WINR)],
                stack.at[sl, e * WINR:(e + 1) * WINR], sem.at[sl]).start()

        @pl.when(over_s[st] > 0)
        def _():
            for e in range(N_EXPERTS):
                k = st * N_EXPERTS + e

                def one(i, carry, e=e, k=k):
                    pltpu.make_async_copy(
                        y_hbm.at[gg, e, pl.ds(pl.multiple_of(offb_s[k] + WINR + i * SEG, SEG), SEG)],
                        stack.at[sl, pl.ds(pl.multiple_of(offo_s[k] + i * SEG, SEG), SEG)],
                        sem.at[sl]).start()
                    return carry

                lax.fori_loop(0, _overflow_groups(seg_s, k), one, 0)

    @pl.when(step == 0)
    def _first():
        stack[...] = jnp.zeros_like(stack)
        fetch(step, slot)

    if nsteps > 1:
        @pl.when(step + 1 < nsteps)
        def _prefetch():
            fetch(step + 1, 1 - slot)

    _window_rows(rowid_ref, weights, gate_ref)

    @pl.when(over > 0)
    def _():
        _overflow_rows(seg_s, offo_s, step, over, rowid_ref, weights, gate_ref)

    _wait_rows(FIRST_ROWS + over, lambda n: pltpu.make_async_copy(
        y_hbm.at[0, 0, pl.ds(0, n)], stack.at[slot, pl.ds(0, n)], sem.at[slot]))

    def token_weights(base):
        return weights[pl.ds(base, TB), :].T.astype(BF16)

    w = jnp.concatenate([token_weights(c * TB) for c in range(FIRST_ROWS // TB)], axis=1)
    acc[...] = _dot(w, stack[slot, 0:FIRST_ROWS, :])

    @pl.when(over > 0)
    def _():
        def chunk(c, carry):
            base = pl.multiple_of(FIRST_ROWS + c * TB, TB)
            acc[...] += _dot(token_weights(base), stack[slot, pl.ds(base, TB), :])
            return carry

        lax.fori_loop(0, (over + TB - 1) // TB, chunk, 0)

    x = x_ref[...] + mod_ref[5] * acc[...]
    if final:
        y = _rms(x, g_ref[...])

        @pl.when(g == 0)
        def _():
            xo_ref[...] = y

        @pl.when(g != 0)
        def _():
            h_ref[...] = y
    else:
        xo_ref[...] = x
        h_ref[...] = _modulate(x, g_ref[...], modn_ref[0], modn_ref[1]).astype(BF16)


def _combine_call(tab, y, rowid, gate, x, mods, layer, nxt, gains, gidx, modrow, ngroups, ntok, final):
    nb = ntok // TB
    t = x.shape[0]
    if final:
        assert ngroups == 2
        out_specs = [pl.BlockSpec((TB, D), lambda g, b, *_: (jnp.where(g == 0, b, nb - 1), 0)),
                     pl.BlockSpec((TB, D), lambda g, b, *_: (jnp.where(g == 0, 0, b), 0))]
        out_shape = [jax.ShapeDtypeStruct((ntok, D), F32), jax.ShapeDtypeStruct((ntok, D), F32)]
    else:
        out_specs = [pl.BlockSpec((TB, D), lambda g, b, *_: (g * nb + b, 0)),
                     pl.BlockSpec((TB, D), lambda g, b, *_: (g * nb + b, 0))]
        out_shape = [jax.ShapeDtypeStruct((t, D), F32), jax.ShapeDtypeStruct((t, D), BF16)]
    grid_spec = pltpu.PrefetchScalarGridSpec(
        num_scalar_prefetch=1,
        grid=(ngroups, nb),
        in_specs=[pl.BlockSpec(memory_space=pl.ANY),
                  pl.BlockSpec((None, None, N_EXPERTS, TB), lambda g, b, *_: (g, b, 0, 0)),
                  pl.BlockSpec((None, None, N_EXPERTS, TB), lambda g, b, *_: (g, b, 0, 0)),
                  pl.BlockSpec((TB, D), lambda g, b, *_: (g * nb + b, 0)),
                  pl.BlockSpec((None, None, 6, 1, D),
                               lambda g, b, *_: (layer, modrow(g * nb + b), 0, 0, 0)),
                  pl.BlockSpec((None, None, 6, 1, D),
                               lambda g, b, *_: (nxt, modrow(g * nb + b), 0, 0, 0)),
                  pl.BlockSpec((None, 1, D), lambda g, b, *_: (gidx, 0, 0))],
        out_specs=out_specs,
        scratch_shapes=[pltpu.VMEM((STACK_ROWS, TB), F32), pltpu.VMEM((2, STACK_ROWS, D), BF16),
                        pltpu.VMEM((TB, D), F32), pltpu.SemaphoreType.DMA((2,))])
    return pl.pallas_call(
        functools.partial(_combine_kernel, nb=nb, nsteps=ngroups * nb, final=final),
        grid_spec=grid_spec,
        out_shape=out_shape,
        compiler_params=_cparams(("arbitrary", "arbitrary")),
        name="combine",
    )(tab, y, rowid, gate, x, mods, mods, gains)


def kernel(x_prompt, x_sample, cache_diff_k, cache_diff_v, cache_swa_k, cache_swa_v, cache_mla_ckv, cache_mla_krope, c, c_ctx, w_ada, b_ada, norm_mix, norm_ffn, w_in_even, w_out_even, diff_lambda, diff_subln, swa_sink, w_in_odd, mla_q_norm, w_q_up, mla_kv_norm, w_kv_up, w_out_odd, w_router, w_gate_exp, w_up_exp, w_down_exp, final_norm):
    batch, seq, _ = x_prompt.shape
    dec_batch, dec_seq, _ = x_sample.shape
    past = cache_diff_k.shape[2]
    depth = w_ada.shape[0]
    n_even = w_in_even.shape[0]
    n_odd = w_in_odd.shape[0]
    nc, ns = batch * seq, dec_batch * dec_seq
    assert nc == ns, "the routed-expert kernels take two token groups of equal size"
    assert nc % TM == 0 and dec_seq % TM == 0 and past % 256 == 0 and dec_seq % GRID_W == 0
    assert seq % TQ == 0 and dec_seq % TQ == 0
    ntok = nc

    def modrow_of(tile):
        def modrow(r):
            tok = r * tile
            return jnp.where(tok < nc, 0, 1 + jnp.maximum(tok - nc, 0) // dec_seq)
        return modrow

    modrow, modrow_tb = modrow_of(TM), modrow_of(TB)

    rc = -(-(1 + dec_batch) // 16) * 16
    cvec = jnp.zeros((rc, D), F32).at[0].set(c_ctx).at[1:1 + dec_batch].set(c)
    mods = _ada_call(cvec, w_ada, b_ada).reshape(depth, rc, 6, 1, D)

    w_even_b = w_in_even.astype(BF16)
    w_oute_b = w_out_even.astype(BF16)
    w_outo_b = w_out_odd.astype(BF16)
    kr_pad = jnp.zeros((n_odd, D, LANES), F32).at[:, :, 64:96].set(w_in_odd[:, :, 640:672])
    w_odd_b = jnp.concatenate([w_in_odd[:, :, :640], kr_pad], axis=-1).astype(BF16)
    wq = w_q_up.reshape(n_odd, MLA_Q_RANK, MLA_HEADS, HD + MLA_ROPE)
    wq_b = jnp.pad(wq, ((0, 0), (0, 0), (0, 0), (0, LANES - HD - MLA_ROPE))).reshape(
        n_odd, MLA_Q_RANK, MLA_QW).astype(BF16)
    wkv = w_kv_up.reshape(n_odd, MLA_KV_RANK, MLA_HEADS, 2 * HD)
    wk_b = jnp.pad(wkv[..., :HD], ((0, 0), (0, 0), (0, 0), (0, LANES - HD))).reshape(
        n_odd, MLA_KV_RANK, MLA_QW).astype(BF16)
    wv_b = wkv[..., HD:].reshape(n_odd, MLA_KV_RANK, D).astype(BF16)
    w_router_b = jnp.pad(w_router, ((0, 0), (0, 0), (0, LANES - N_EXPERTS))).astype(BF16)
    even_tabs = _rope_tables(dec_seq, 16, _even_lane)
    mla_tabs = _rope_tables(dec_seq, 8, _mla_lane)
    ckr = jnp.zeros((dec_batch, n_odd, past, LANES), F32).at[..., 64:96].set(cache_mla_krope)
    even_caches = (cache_diff_k, cache_diff_v, cache_swa_k, cache_swa_v)

    mix_gains = jnp.concatenate([norm_mix, final_norm[None]], axis=0).reshape(depth + 1, 1, D)
    ffn_gains = norm_ffn.reshape(depth, 1, D)
    q_gains = mla_q_norm.reshape(n_odd, 1, MLA_Q_RANK)
    kv_gains = mla_kv_norm.reshape(n_odd, 1, MLA_KV_RANK)
    sublns = diff_subln.reshape(n_even, 1, 2 * HD)

    xa, xb, xb_off = x_prompt.reshape(nc, D), x_sample.reshape(ns, D), 0
    h = _norm_mod_call(xa, xb, mods, 0, mix_gains, modrow)
    even_state = odd_state = None
    y_prompt = y_sample = None
    for i in range(depth):
        j = i // 2
        if i % 2 == 0:
            qkv_c, *even_state = _proj_even_call(h, w_even_b, j, 0, nc, None, dec_seq,
                                                 prev=even_state, seq=seq)
            (qkv_l,) = _proj_even_call(h, w_even_b, j, nc, ns, even_tabs, dec_seq)
            li = _lambda_init(i)
            o_c = _attn_even_call(qkv_c, None, j, diff_lambda, sublns, swa_sink, batch, seq, 0, li)
            o_l = _attn_even_call(qkv_l, even_caches, j, diff_lambda, sublns, swa_sink,
                                  dec_batch, dec_seq, past, li)
            w_out = w_oute_b
        else:
            q_c, *odd_state = _proj_odd_call(h, w_odd_b, q_gains, wq_b, kv_gains, j, 0, nc, None,
                                             dec_seq, prev=odd_state, seq=seq)
            q_l, ckv_l, kr_l = _proj_odd_call(h, w_odd_b, q_gains, wq_b, kv_gains, j, nc, ns,
                                              mla_tabs, dec_seq)
            o_c = _attn_odd_call(q_c, odd_state[0], odd_state[1], None, j, wk_b, wv_b, batch, seq, 0)
            o_l = _attn_odd_call(q_l, ckv_l, kr_l, (cache_mla_ckv, ckr), j, wk_b, wv_b,
                                 dec_batch, dec_seq, past)
            w_out = w_outo_b
        x, h2, aff = _outproj_call(xa, xb, xb_off, o_c, o_l, w_out, j, mods, i, ffn_gains,
                                   w_router_b, modrow)
        rowid, gate, tab = _select_call(aff, 2, ntok)
        xe = _dispatch_call(tab, h2, rowid, 2, ntok)
        y = _ffn_call(tab, xe, w_gate_exp, w_up_exp, w_down_exp, i, 2, ntok)
        final = i == depth - 1
        nxt = i if final else i + 1
        out_a, out_b = _combine_call(tab, y, rowid, gate, x, mods, i, nxt, mix_gains,
                                     depth if final else nxt, modrow_tb, 2, ntok, final)
        if final:
            y_prompt = out_a.reshape(batch, seq, D)
            y_sample = out_b.reshape(dec_batch, dec_seq, D)
        else:
            x, h = out_a, out_b
            xa, xb, xb_off = x, x, nc // TM

    kd, vd, ks, vs = even_state
    ckv_new, kr_new = odd_state
    return (y_prompt, y_sample,
            kd.reshape(batch, n_even, seq, DIFF_HEADS, 2 * HD),
            vd.reshape(batch, n_even, seq, DIFF_HEADS, 2 * HD),
            ks.reshape(batch, n_even, seq, 2, HD), vs.reshape(batch, n_even, seq, 2, HD),
            ckv_new, kr_new[..., 64:96])
```

```python
import functools
import math

import jax
import jax.numpy as jnp
import numpy as np
from jax import lax
from jax.experimental import pallas as pl
from jax.experimental.pallas import tpu as pltpu

F32 = jnp.float32
BF16 = jnp.bfloat16
I32 = jnp.int32

D = 1024
HD = 64
GRID_W = 64
WINDOW = 128
DIFF_HEADS = 4
SWA_HEADS = 8
MLA_HEADS = 16
MLA_Q_RANK = 384
MLA_KV_RANK = 256
MLA_ROPE = 32
N_EXPERTS = 16
EXPERT_FF = 512
EC_FACTOR = 2
ROPE_BASE = 10000.0
EPS = 1e-6
NEG_INF = -1e30
LOG2E = math.log2(math.e)
EVEN_IN = 2304
LANES = 128
TM = 512
TQ = 256
TB = 256
SEG = 16
VMEM_LIMIT = 56 * 1024 * 1024


def _cparams(sem, vmem=VMEM_LIMIT):
    return pltpu.CompilerParams(dimension_semantics=sem, vmem_limit_bytes=vmem)


def _dot(a, b):
    return jnp.dot(a, b, preferred_element_type=F32)


def _dot_nt(a, b):
    return lax.dot_general(a, b, (((1,), (1,)), ((), ())), preferred_element_type=F32)


def _silu(x):
    return x / (1.0 + jnp.exp(-x))


def _rms(x, g):
    ms = jnp.mean(x * x, axis=-1, keepdims=True)
    return x * lax.rsqrt(ms + EPS) * g


def _modulate(x, g, shift, scale):
    return _rms(x, g) * (1.0 + scale) + shift


def _lambda_init(layer):
    return 0.8 - 0.6 * math.exp(-0.3 * layer)


def _ada_kernel(c_ref, w_ref, b_ref, o_ref):
    s = _silu(c_ref[...]).astype(BF16)
    o_ref[...] = _dot(s, w_ref[...].astype(BF16)) + b_ref[...]


def _ada_call(cvec, w_ada, b_ada):
    depth, _, n6 = w_ada.shape
    rc = cvec.shape[0]
    tn = 512
    return pl.pallas_call(
        _ada_kernel,
        grid=(depth, n6 // tn),
        in_specs=[pl.BlockSpec((rc, D), lambda i, n: (0, 0)),
                  pl.BlockSpec((None, D, tn), lambda i, n: (i, 0, n)),
                  pl.BlockSpec((None, 1, tn), lambda i, n: (i, 0, n))],
        out_specs=pl.BlockSpec((None, rc, tn), lambda i, n: (i, 0, n)),
        out_shape=jax.ShapeDtypeStruct((depth, rc, n6), F32),
        compiler_params=_cparams(("parallel", "parallel")),
        name="ada",
    )(cvec, w_ada, b_ada.reshape(depth, 1, n6))


def _two_part_specs(rows_a, rows_b, width, off_b=0):
    na, nb_ = rows_a // TM, rows_b // TM
    return (pl.BlockSpec((TM, width), lambda r: (jnp.minimum(r, na - 1), 0)),
            pl.BlockSpec((TM, width), lambda r: (off_b + jnp.clip(r - na, 0, nb_ - 1), 0)))


def _norm_mod_kernel(xa_ref, xb_ref, mod_ref, g_ref, h_ref, *, ntile_a):
    def run(x_ref):
        h_ref[...] = _modulate(x_ref[...], g_ref[...], mod_ref[0], mod_ref[1]).astype(BF16)

    @pl.when(pl.program_id(0) < ntile_a)
    def _():
        run(xa_ref)

    @pl.when(pl.program_id(0) >= ntile_a)
    def _():
        run(xb_ref)


def _norm_mod_call(xa, xb, mods, layer, gains, modrow):
    na, nb_ = xa.shape[0], xb.shape[0]
    return pl.pallas_call(
        functools.partial(_norm_mod_kernel, ntile_a=na // TM),
        grid=((na + nb_) // TM,),
        in_specs=[*_two_part_specs(na, nb_, D),
                  pl.BlockSpec((None, None, 6, 1, D), lambda r: (layer, modrow(r), 0, 0, 0)),
                  pl.BlockSpec((None, 1, D), lambda r: (layer, 0, 0))],
        out_specs=pl.BlockSpec((TM, D), lambda r: (r, 0)),
        out_shape=jax.ShapeDtypeStruct((na + nb_, D), BF16),
        compiler_params=_cparams(("parallel",)),
        name="norm_mod",
    )(xa, xb, mods, gains)


def _rope_tables(dec_seq, half, lane_of_dim):
    pos = jnp.arange(dec_seq)
    row = (pos // GRID_W).astype(F32)
    col = (pos % GRID_W).astype(F32)
    inv = ROPE_BASE ** (-(jnp.arange(half, dtype=F32) / half))
    ang = jnp.stack([row[:, None] * inv[None, :], col[:, None] * inv[None, :]])
    info = [lane_of_dim(lane) for lane in range(LANES)]
    axis = np.array([0 if i is None else i[0] for i in info])
    freq = np.array([0 if i is None else i[1] for i in info])
    first = np.array([i is not None and not i[2] for i in info])[None, :]
    second = np.array([i is not None and i[2] for i in info])[None, :]
    lane_ang = ang[axis, :, freq].T
    cos, sin = jnp.cos(lane_ang), jnp.sin(lane_ang)
    return (jnp.where(first | second, cos, 1.0), jnp.where(first, -sin, 0.0),
            jnp.where(second, sin, 0.0))


def _even_lane(lane):
    j = lane % HD
    axis, jj = j // 32, j % 32
    return axis, jj % 16, jj >= 16


def _mla_lane(lane):
    if lane < 64 or lane >= 96:
        return None
    jj = lane - 64
    axis, k = jj // 16, jj % 16
    return axis, k % 8, k >= 8


def _rope(x, c, s1, s2, shift):
    return x * c + pltpu.roll(x, LANES - shift, 1) * s1 + pltpu.roll(x, shift, 1) * s2


_EVEN_ROPE_TILES = tuple(range(0, 8)) + tuple(range(12, 17))
_EVEN_Q_TILES = tuple(range(0, 4)) + tuple(range(12, 16))


def _append_layer(prev_refs, out_refs, new_values):
    for i, (out_ref, new) in enumerate(zip(out_refs, new_values)):
        nbatch, nlayers, seq, width = out_ref.shape
        if prev_refs:
            out_ref[:, 0:nlayers - 1] = prev_refs[i][...]
        out_ref[:, nlayers - 1] = new.reshape(nbatch, seq, width)


def _state_specs(prev, widths, seq):
    nlayers = 1 if prev is None else prev[0].shape[1] + 1
    per = TM // seq
    ins = [] if prev is None else [pl.BlockSpec((per, nlayers - 1, seq, w), lambda r: (r, 0, 0, 0))
                                   for w in widths]
    outs = [pl.BlockSpec((per, nlayers, seq, w), lambda r: (r, 0, 0, 0)) for w in widths]
    return ins, outs, nlayers


def _proj_even_kernel(*refs, rope, caches, nprev):
    h_ref, w_ref = refs[0], refs[1]
    pos = 2
    if rope:
        c_ref, s1_ref, s2_ref = refs[2:5]
        pos = 5
    prev_refs = refs[pos:pos + nprev]
    pos += nprev
    qkv_ref = refs[pos]
    res = _dot(h_ref[...], w_ref[...])
    scale = HD ** -0.5 * LOG2E
    for t in range(EVEN_IN // LANES):
        x = res[:, t * LANES:(t + 1) * LANES]
        if rope and t in _EVEN_ROPE_TILES:
            x = _rope(x, c_ref[...], s1_ref[...], s2_ref[...], 16)
        if t in _EVEN_Q_TILES:
            x = x * scale
        qkv_ref[:, t * LANES:(t + 1) * LANES] = x.astype(BF16)
    if caches:
        _append_layer(prev_refs, refs[pos + 1:pos + 5],
                      (res[:, 512:1024], res[:, 1024:1536], res[:, 2048:2176], res[:, 2176:2304]))


def _proj_even_call(h, w, j, row0, nrows, tables, dec_seq, prev=None, seq=None):
    rope = tables is not None
    caches = not rope
    t0 = row0 // TM
    in_specs = [pl.BlockSpec((TM, D), lambda r: (t0 + r, 0)),
                pl.BlockSpec((None, D, EVEN_IN), lambda r: (j, 0, 0))]
    args = [h, w]
    if rope:
        per = dec_seq // TM
        for _ in range(3):
            in_specs.append(pl.BlockSpec((TM, LANES), lambda r: (r % per, 0)))
        args += list(tables)
    out_specs = [pl.BlockSpec((TM, EVEN_IN), lambda r: (r, 0))]
    out_shape = [jax.ShapeDtypeStruct((nrows, EVEN_IN), BF16)]
    nprev = 0
    if caches:
        widths = (512, 512, LANES, LANES)
        ins, outs, nlayers = _state_specs(prev, widths, seq)
        nprev = len(ins)
        in_specs += ins
        args += [] if prev is None else list(prev)
        out_specs += outs
        out_shape += [jax.ShapeDtypeStruct((nrows // seq, nlayers, seq, wd), F32) for wd in widths]
    return pl.pallas_call(
        functools.partial(_proj_even_kernel, rope=rope, caches=caches, nprev=nprev),
        grid=(nrows // TM,),
        in_specs=in_specs, out_specs=out_specs, out_shape=out_shape,
        compiler_params=_cparams(("parallel",)),
        name="proj_even_lat" if rope else "proj_even_ctx",
    )(*args)


ODD_IN_PAD = MLA_Q_RANK + MLA_KV_RANK + LANES
MLA_QW = MLA_HEADS * LANES


def _proj_odd_kernel(*refs, rope, nprev):
    h_ref, w_ref, qn_ref, wq_ref, kvn_ref = refs[:5]
    pos = 5
    if rope:
        c_ref, s1_ref, s2_ref = refs[5:8]
        pos = 8
    prev_refs = refs[pos:pos + nprev]
    pos += nprev
    q_ref, ckv_ref, kr_ref = refs[pos:pos + 3]
    res = _dot(h_ref[...], w_ref[...])
    cq = _rms(res[:, :MLA_Q_RANK], qn_ref[...]).astype(BF16)
    ckv = _rms(res[:, MLA_Q_RANK:MLA_Q_RANK + MLA_KV_RANK], kvn_ref[...])
    kr = res[:, MLA_Q_RANK + MLA_KV_RANK:]
    if rope:
        kr = _rope(kr, c_ref[...], s1_ref[...], s2_ref[...], 8)
        ckv_ref[...] = ckv
        kr_ref[...] = kr
    else:
        _append_layer(prev_refs, (ckv_ref, kr_ref), (ckv, kr))
    q = _dot(cq, wq_ref[...])
    scale = (HD + MLA_ROPE) ** -0.5 * LOG2E
    for t in range(MLA_HEADS):
        x = q[:, t * LANES:(t + 1) * LANES]
        if rope:
            x = _rope(x, c_ref[...], s1_ref[...], s2_ref[...], 8)
        q_ref[:, t * LANES:(t + 1) * LANES] = (x * scale).astype(BF16)


def _proj_odd_call(h, w_in, qn, wq, kvn, j, row0, nrows, tables, dec_seq, prev=None, seq=None):
    rope = tables is not None
    t0 = row0 // TM
    in_specs = [pl.BlockSpec((TM, D), lambda r: (t0 + r, 0)),
                pl.BlockSpec((None, D, ODD_IN_PAD), lambda r: (j, 0, 0)),
                pl.BlockSpec((None, 1, MLA_Q_RANK), lambda r: (j, 0, 0)),
                pl.BlockSpec((None, MLA_Q_RANK, MLA_QW), lambda r: (j, 0, 0)),
                pl.BlockSpec((None, 1, MLA_KV_RANK), lambda r: (j, 0, 0))]
    args = [h, w_in, qn, wq, kvn]
    if rope:
        per = dec_seq // TM
        for _ in range(3):
            in_specs.append(pl.BlockSpec((TM, LANES), lambda r: (r % per, 0)))
        args += list(tables)
    out_specs = [pl.BlockSpec((TM, MLA_QW), lambda r: (r, 0))]
    out_shape = [jax.ShapeDtypeStruct((nrows, MLA_QW), BF16)]
    widths = (MLA_KV_RANK, LANES)
    nprev = 0
    if rope:
        out_specs += [pl.BlockSpec((TM, wd), lambda r: (r, 0)) for wd in widths]
        out_shape += [jax.ShapeDtypeStruct((nrows, wd), F32) for wd in widths]
    else:
        ins, outs, nlayers = _state_specs(prev, widths, seq)
        nprev = len(ins)
        in_specs += ins
        args += [] if prev is None else list(prev)
        out_specs += outs
        out_shape += [jax.ShapeDtypeStruct((nrows // seq, nlayers, seq, wd), F32) for wd in widths]
    return pl.pallas_call(
        functools.partial(_proj_odd_kernel, rope=rope, nprev=nprev),
        grid=(nrows // TM,),
        in_specs=in_specs,
        out_specs=out_specs,
        out_shape=out_shape,
        compiler_params=_cparams(("parallel",)),
        name="proj_odd_lat" if rope else "proj_odd_ctx",
    )(*args)


def _ones_lane(half):
    return HD if half == 0 else 0


def _row_sum(e, o, half, from_matmul):
    if from_matmul:
        one = _ones_lane(half)
        return o[:, one:one + 1]
    return jnp.sum(e, axis=-1, keepdims=True)


def _half_values(v, half):
    lane = lax.broadcasted_iota(I32, (1, LANES), 1)
    keep = (lane < HD) if half == 0 else (lane >= HD)
    return jnp.where(keep, v, jnp.where(lane == _ones_lane(half), 1.0, 0.0)).astype(BF16)


VDT_ROWS = 2 * HD + SEG


def _attn_even_kernel(*refs, seq, past, lam_init, layer):
    latent = past > 0
    n = past + seq
    if latent:
        (qkv_ref, ck_ref, cv_ref, sk_ref, sv_ref, lam_ref, subln_ref, sink_ref,
         o_ref, kd, vd, ka, vl, vh) = refs
    else:
        qkv_ref, lam_ref, subln_ref, sink_ref, o_ref, kd, vd, ka, vl, vh = refs
    qi = pl.program_id(1)
    lo = lax.broadcasted_iota(I32, (1, LANES), 1) < HD

    @pl.when(qi == 0)
    def _build():
        chunk = 256
        for c0 in range(0, n, chunk):
            rows = slice(c0, c0 + chunk)
            if c0 < past:
                prow = slice(c0, c0 + chunk)
                for h in range(DIFF_HEADS):
                    kd[rows, h * LANES:(h + 1) * LANES] = ck_ref[prow, h, :].astype(BF16)
                    vd[h * VDT_ROWS:h * VDT_ROWS + LANES, rows] = cv_ref[prow, h, :].T.astype(BF16)
                kt = jnp.concatenate([sk_ref[prow, 0, :], sk_ref[prow, 1, :]], axis=1)
                vt = jnp.concatenate([sv_ref[prow, 0, :], sv_ref[prow, 1, :]], axis=1)
            else:
                orow = slice(c0 - past, c0 - past + chunk)
                kd[rows, :] = qkv_ref[orow, 512:1024]
                if latent:
                    for h in range(DIFF_HEADS):
                        vown = qkv_ref[orow, 1024 + h * LANES:1024 + (h + 1) * LANES].astype(F32)
                        vd[h * VDT_ROWS:h * VDT_ROWS + LANES, rows] = vown.T.astype(BF16)
                else:
                    vd[rows, :] = qkv_ref[orow, 1024:1536]
                kt = qkv_ref[orow, 2048:2176].astype(F32)
                vt = qkv_ref[orow, 2176:2304].astype(F32)
            kr = pltpu.roll(kt, HD, 1)
            vr = pltpu.roll(vt, HD, 1)
            ka[0, rows, :] = jnp.where(lo, kt, kr).astype(BF16)
            ka[1, rows, :] = jnp.where(lo, kr, kt).astype(BF16)
            vl[0, rows, :] = _half_values(vt, 0)
            vh[0, rows, :] = _half_values(vr, 1)
            vl[1, rows, :] = _half_values(vr, 0)
            vh[1, rows, :] = _half_values(vt, 1)
        if latent:
            for h in range(DIFF_HEADS):
                vd[h * VDT_ROWS + LANES:(h + 1) * VDT_ROWS, :] = jnp.ones((SEG, n), BF16)

    r0 = pl.multiple_of(qi * TQ, TQ)
    lam = lam_ref[...]
    lam_full = (jnp.exp(jnp.sum(lam[0:1] * lam[1:2], axis=-1, keepdims=True))
                - jnp.exp(jnp.sum(lam[2:3] * lam[3:4], axis=-1, keepdims=True)) + lam_init)
    zero_b = jnp.zeros((), BF16)

    for h in range(DIFF_HEADS):
        cs = slice(h * LANES, (h + 1) * LANES)
        qt = qkv_ref[pl.ds(r0, TQ), cs]
        kh = kd[:, cs]
        es, rs, ots = [], [], []
        for comp in range(2):
            qc = jnp.where(lo, qt, zero_b) if comp == 0 else jnp.where(lo, zero_b, qt)
            s = _dot_nt(qc, kh)
            m = jnp.max(s, axis=-1, keepdims=True)
            e = jnp.exp2(s - m)
            if latent:
                ots.append(_dot_nt(vd[h * VDT_ROWS:(h + 1) * VDT_ROWS, :], e.astype(BF16)))
            else:
                es.append(e)
                rs.append(1.0 / jnp.sum(e, axis=-1, keepdims=True))
        if latent:
            ot = (ots[0][0:LANES, :] * (1.0 / ots[0][LANES:LANES + 1, :])
                  - ots[1][0:LANES, :] * (lam_full / ots[1][LANES:LANES + 1, :]))
            o = ot.T
        else:
            a = es[0] * rs[0] - es[1] * (lam_full * rs[1])
            o = _dot(a.astype(BF16), vd[:, cs])
        o = _rms(o, subln_ref[...]) * (1.0 - lam_init)
        o_ref[:, cs] = o.astype(BF16)

    nblk = seq // WINDOW
    dense = past if latent else seq
    if latent:
        per = TQ // WINDOW
        offsets = tuple(range(-1, per + 1))
        rr = lax.broadcasted_iota(I32, (TQ, WINDOW), 0)
        cc = lax.broadcasted_iota(I32, (TQ, WINDOW), 1)
        band, starts = {}, {}
        for d in offsets:
            blk = qi * per + d
            inside = jnp.logical_and(blk >= 0, blk < nblk)
            band[d] = jnp.logical_and(jnp.abs(rr - cc - d * WINDOW) <= WINDOW, inside)
            starts[d] = pl.multiple_of(past + jnp.clip(blk, 0, nblk - 1) * WINDOW, WINDOW)
    for i in range(SWA_HEADS // 2):
        hk = i // 2
        cs = slice(1536 + i * LANES, 1536 + (i + 1) * LANES)
        qt = qkv_ref[pl.ds(r0, TQ), cs]
        halves = []
        for half in range(2):
            qc = jnp.where(lo, qt, zero_b) if half == 0 else jnp.where(lo, zero_b, qt)
            vsel = vl if half == 0 else vh
            sink = sink_ref[layer, 2 * i + half] * LOG2E
            parts = [_dot_nt(qc, ka[hk, 0:dense, :])]
            if latent:
                for d in offsets:
                    s = _dot_nt(qc, ka[hk, pl.ds(starts[d], WINDOW), :])
                    parts.append(jnp.where(band[d], s, NEG_INF))
            s_all = jnp.concatenate(parts, axis=1) if len(parts) > 1 else parts[0]
            m = jnp.maximum(jnp.max(s_all, axis=-1, keepdims=True), sink)
            e = jnp.exp2(s_all - m)
            eb = e.astype(BF16)
            o = _dot(eb[:, 0:dense], vsel[hk, 0:dense, :])
            if latent:
                for k, d in enumerate(offsets):
                    o += _dot(eb[:, dense + k * WINDOW:dense + (k + 1) * WINDOW],
                              vsel[hk, pl.ds(starts[d], WINDOW), :])
            den = _row_sum(e, o, half, from_matmul=latent) + jnp.exp2(sink - m)
            halves.append(o * (1.0 / den))
        o_ref[:, 512 + i * LANES:512 + (i + 1) * LANES] = jnp.where(lo, halves[0], halves[1]).astype(BF16)


def _attn_even_call(qkv, caches, j, lam, subln, sink, nbatch, seq, past, lam_init):
    n = past + seq
    latent = past > 0
    in_specs = [pl.BlockSpec((seq, EVEN_IN), lambda b, q: (b, 0))]
    args = [qkv]
    if latent:
        ck, cv, sk, sv = caches
        in_specs += [pl.BlockSpec((None, None, past, DIFF_HEADS, 2 * HD), lambda b, q: (b, j, 0, 0, 0)),
                     pl.BlockSpec((None, None, past, DIFF_HEADS, 2 * HD), lambda b, q: (b, j, 0, 0, 0)),
                     pl.BlockSpec((None, None, past, 2, HD), lambda b, q: (b, j, 0, 0, 0)),
                     pl.BlockSpec((None, None, past, 2, HD), lambda b, q: (b, j, 0, 0, 0))]
        args += [ck, cv, sk, sv]
    in_specs += [pl.BlockSpec((None, 4, HD), lambda b, q: (j, 0, 0)),
                 pl.BlockSpec((None, 1, 2 * HD), lambda b, q: (j, 0, 0)),
                 pl.BlockSpec(memory_space=pltpu.SMEM)]
    args += [lam, subln, sink]
    return pl.pallas_call(
        functools.partial(_attn_even_kernel, seq=seq, past=past, lam_init=lam_init, layer=j),
        grid=(nbatch, seq // TQ),
        in_specs=in_specs,
        out_specs=pl.BlockSpec((TQ, D), lambda b, q: (b * (seq // TQ) + q, 0)),
        out_shape=jax.ShapeDtypeStruct((nbatch * seq, D), BF16),
        scratch_shapes=[pltpu.VMEM((n, 512), BF16),
                        pltpu.VMEM((DIFF_HEADS * VDT_ROWS, n) if latent else (n, 512), BF16),
                        pltpu.VMEM((2, n, LANES), BF16), pltpu.VMEM((2, n, LANES), BF16),
                        pltpu.VMEM((2, n, LANES), BF16)],
        compiler_params=_cparams(("arbitrary", "arbitrary")),
        name="attn_even_lat" if latent else "attn_even_ctx",
    )(*args)


VT_ROWS = HD + SEG


def _attn_odd_kernel(*refs, seq, past):
    latent = past > 0
    n = past + seq
    if latent:
        q_ref, ckv_ref, kr_ref, cckv_ref, ckr_ref, wk_ref, wv_ref, o_ref, kf, vt = refs
    else:
        q_ref, ckv_ref, kr_ref, wk_ref, wv_ref, o_ref, kf, vlo, vhi = refs
    qi = pl.program_id(1)

    @pl.when(qi == 0)
    def _build():
        chunk = 256
        for c0 in range(0, n, chunk):
            rows = slice(c0, c0 + chunk)
            if c0 < past:
                ckv = cckv_ref[c0:c0 + chunk, :].astype(BF16)
                kr = ckr_ref[c0:c0 + chunk, :]
            else:
                ckv = ckv_ref[c0 - past:c0 - past + chunk, :].astype(BF16)
                kr = kr_ref[c0 - past:c0 - past + chunk, :]
            kk = _dot(ckv, wk_ref[...])
            for h in range(MLA_HEADS):
                cs = slice(h * LANES, (h + 1) * LANES)
                kf[rows, cs] = (kk[:, cs] + kr).astype(BF16)
            vv = _dot(ckv, wv_ref[...])
            for i in range(MLA_HEADS // 2):
                cs = slice(i * LANES, (i + 1) * LANES)
                if latent:
                    pair = vv[:, cs].T.astype(BF16)
                    for half in range(2):
                        r = (2 * i + half) * VT_ROWS
                        vt[r:r + HD, rows] = pair[half * HD:(half + 1) * HD, :]
                else:
                    vlo[rows, cs] = _half_values(vv[:, cs], 0)
                    vhi[rows, cs] = _half_values(vv[:, cs], 1)
        if latent:
            for h in range(MLA_HEADS):
                vt[h * VT_ROWS + HD:(h + 1) * VT_ROWS, :] = jnp.ones((SEG, n), BF16)

    r0 = pl.multiple_of(qi * TQ, TQ)
    lo = lax.broadcasted_iota(I32, (1, LANES), 1) < HD
    for i in range(MLA_HEADS // 2):
        halves = []
        for half in range(2):
            h = 2 * i + half
            cs = slice(h * LANES, (h + 1) * LANES)
            s = _dot_nt(q_ref[pl.ds(r0, TQ), cs], kf[:, cs])
            m = jnp.max(s, axis=-1, keepdims=True)
            e = jnp.exp2(s - m)
            if latent:
                ot = _dot_nt(vt[h * VT_ROWS:(h + 1) * VT_ROWS, :], e.astype(BF16))
                halves.append(ot[0:HD, :] * (1.0 / ot[HD:HD + 1, :]))
            else:
                vsel = vlo if half == 0 else vhi
                o = _dot(e.astype(BF16), vsel[:, i * LANES:(i + 1) * LANES])
                halves.append(o * (1.0 / jnp.sum(e, axis=-1, keepdims=True)))
        if latent:
            tile = jnp.concatenate(halves, axis=0).T
        else:
            tile = jnp.where(lo, halves[0], halves[1])
        o_ref[:, i * LANES:(i + 1) * LANES] = tile.astype(BF16)


def _attn_odd_call(q, ckv, kr, caches, j, wk, wv, nbatch, seq, past):
    n = past + seq
    latent = past > 0
    in_specs = [pl.BlockSpec((seq, MLA_QW), lambda b, qq: (b, 0))]
    if latent:
        in_specs += [pl.BlockSpec((seq, MLA_KV_RANK), lambda b, qq: (b, 0)),
                     pl.BlockSpec((seq, LANES), lambda b, qq: (b, 0))]
    else:
        last = ckv.shape[1] - 1
        in_specs += [pl.BlockSpec((None, None, seq, MLA_KV_RANK), lambda b, qq: (b, last, 0, 0)),
                     pl.BlockSpec((None, None, seq, LANES), lambda b, qq: (b, last, 0, 0))]
    args = [q, ckv, kr]
    if latent:
        in_specs += [pl.BlockSpec((None, None, past, MLA_KV_RANK), lambda b, qq: (b, j, 0, 0)),
                     pl.BlockSpec((None, None, past, LANES), lambda b, qq: (b, j, 0, 0))]
        args += list(caches)
    in_specs += [pl.BlockSpec((None, MLA_KV_RANK, MLA_QW), lambda b, qq: (j, 0, 0)),
                 pl.BlockSpec((None, MLA_KV_RANK, D), lambda b, qq: (j, 0, 0))]
    args += [wk, wv]
    return pl.pallas_call(
        functools.partial(_attn_odd_kernel, seq=seq, past=past),
        grid=(nbatch, seq // TQ),
        in_specs=in_specs,
        out_specs=pl.BlockSpec((TQ, D), lambda b, qq: (b * (seq // TQ) + qq, 0)),
        out_shape=jax.ShapeDtypeStruct((nbatch * seq, D), BF16),
        scratch_shapes=([pltpu.VMEM((n, MLA_QW), BF16), pltpu.VMEM((MLA_HEADS * VT_ROWS, n), BF16)]
                        if latent else
                        [pltpu.VMEM((n, MLA_QW), BF16), pltpu.VMEM((n, D), BF16),
                         pltpu.VMEM((n, D), BF16)]),
        compiler_params=_cparams(("arbitrary", "arbitrary")),
        name="attn_odd_lat" if latent else "attn_odd_ctx",
    )(*args)


def _outproj_kernel(xa_ref, xb_ref, oa_ref, ob_ref, w_ref, mod_ref, g_ref, wr_ref,
                    xo_ref, h_ref, aff_ref, *, ntile_a):
    def run(x_ref, o_ref):
        x = x_ref[...] + mod_ref[2] * _dot(o_ref[...], w_ref[...])
        xo_ref[...] = x
        h = _modulate(x, g_ref[...], mod_ref[3], mod_ref[4]).astype(BF16)
        h_ref[...] = h
        logits = _dot(h, wr_ref[...])
        lane = lax.broadcasted_iota(I32, (TM, LANES), 1)
        lg = jnp.where(lane < N_EXPERTS, logits, -jnp.inf)
        e = jnp.exp(lg - jnp.max(lg, axis=-1, keepdims=True))
        aff = e / jnp.sum(e, axis=-1, keepdims=True)
        for c in range(TM // TB):
            aff_ref[c] = aff[c * TB:(c + 1) * TB, :].T[0:N_EXPERTS, :]

    @pl.when(pl.program_id(0) < ntile_a)
    def _():
        run(xa_ref, oa_ref)

    @pl.when(pl.program_id(0) >= ntile_a)
    def _():
        run(xb_ref, ob_ref)


def _outproj_call(xa, xb, xb_off, o_a, o_b, w, widx, mods, layer, gains, w_router, modrow):
    na, nb_ = o_a.shape[0], o_b.shape[0]
    t = na + nb_
    assert TM % TB == 0
    return pl.pallas_call(
        functools.partial(_outproj_kernel, ntile_a=na // TM),
        grid=(t // TM,),
        in_specs=[*_two_part_specs(na, nb_, D, xb_off), *_two_part_specs(na, nb_, D),
                  pl.BlockSpec((None, D, D), lambda r: (widx, 0, 0)),
                  pl.BlockSpec((None, None, 6, 1, D), lambda r: (layer, modrow(r), 0, 0, 0)),
                  pl.BlockSpec((None, 1, D), lambda r: (layer, 0, 0)),
                  pl.BlockSpec((None, D, LANES), lambda r: (layer, 0, 0))],
        out_specs=[pl.BlockSpec((TM, D), lambda r: (r, 0)),
                   pl.BlockSpec((TM, D), lambda r: (r, 0)),
                   pl.BlockSpec((TM // TB, N_EXPERTS, TB), lambda r: (r, 0, 0))],
        out_shape=[jax.ShapeDtypeStruct((t, D), F32), jax.ShapeDtypeStruct((t, D), BF16),
                   jax.ShapeDtypeStruct((t // TB, N_EXPERTS, TB), F32)],
        compiler_params=_cparams(("parallel",)),
        name="outproj",
    )(xa, xb, o_a, o_b, w, mods, gains, w_router)


def _select_kernel(aff_ref, rowid_ref, gate_ref, tab_ref, *, nb, cap):
    ne = N_EXPERTS
    nr = nb * ne
    a = aff_ref[...].reshape(nr, TB)
    ri = lax.broadcasted_iota(I32, (nr, nr), 0)
    ci = lax.broadcasted_iota(I32, (nr, nr), 1)
    same_e = (ri & (ne - 1)) == (ci & (ne - 1))
    same_b = (ri >> 4) == (ci >> 4)
    m_e = jnp.where(same_e, 1.0, 0.0).astype(BF16)
    m_b = jnp.where(same_b, 1.0, 0.0).astype(BF16)
    m_a = jnp.where(jnp.logical_and(same_e, ci < ri), 1.0, 0.0).astype(BF16)
    m_o = jnp.where(jnp.logical_and(same_b, ci < ri), 1.0, 0.0).astype(BF16)
    ui = lax.broadcasted_iota(I32, (TB, TB), 0)
    uj = lax.broadcasted_iota(I32, (TB, TB), 1)
    upper = jnp.where(ui < uj, 1.0, 0.0).astype(BF16)

    def rows_to_lanes(col):
        return jnp.broadcast_to(col, (nr, LANES)).astype(BF16)

    wide = jnp.concatenate([aff_ref[b] for b in range(nb)], axis=1)

    def count_ge(value):
        return jnp.sum(jnp.where(wide >= value, 1.0, 0.0), axis=-1, keepdims=True)

    def bisect(i, v):
        cand = v | jnp.left_shift(jnp.int32(1), 30 - i)
        return jnp.where(count_ge(pltpu.bitcast(cand, F32)) >= cap, cand, v)

    thr = lax.fori_loop(0, 31, bisect, jnp.zeros((ne, 1), I32))

    def refine(i, lohi):
        lo_e, hi_e = lohi
        mid = 0.5 * (lo_e + hi_e)
        take = count_ge(mid) >= cap
        return jnp.where(take, mid, lo_e), jnp.where(take, hi_e, mid)

    lo_e, hi_e = lax.fori_loop(0, TIE_STEPS, refine,
                               (pltpu.bitcast(thr, F32), pltpu.bitcast(thr + 1, F32)))
    lo_v = jnp.concatenate([lo_e] * nb, axis=0)
    hi_v = jnp.concatenate([hi_e] * nb, axis=0)
    gt = jnp.where(a >= hi_v, 1.0, 0.0)
    eq = jnp.where(jnp.logical_and(a >= lo_v, a < hi_v), 1.0, 0.0)
    n_gt = _dot(m_e, rows_to_lanes(jnp.sum(gt, axis=-1, keepdims=True)))[:, 0:1]
    need = cap - n_gt
    eq_before = (_dot(m_a, rows_to_lanes(jnp.sum(eq, axis=-1, keepdims=True)))[:, 0:1]
                 + _dot(eq.astype(BF16), upper))
    sel = jnp.where(jnp.logical_and(eq > 0.0, eq_before < need), 1.0, gt)
    local = _dot(sel.astype(BF16), upper)
    cnt = jnp.sum(sel, axis=-1, keepdims=True)
    seg = jnp.floor((cnt + (SEG - 1)) * (1.0 / SEG)) * SEG
    segb = rows_to_lanes(seg)
    over = jnp.maximum(seg - WINR, 0.0)
    overb = rows_to_lanes(over)
    off_over = FIRST_ROWS + _dot(m_o, overb)[:, 0:1]
    off_buf = _dot(m_a, segb)[:, 0:1]
    over_blk = _dot(m_b, overb)[:, 0:1]
    rows_exp = _dot(m_e, segb)[:, 0:1]
    tiles_exp = jnp.floor((rows_exp + (TF - 1)) / TF)
    expert = (lax.broadcasted_iota(I32, (nr, 1), 0) & (ne - 1)).astype(F32)
    row = jnp.where(local < WINR, expert * WINR + local, off_over + local - WINR)
    rowid_ref[...] = jnp.where(sel > 0.0, row, -1.0).astype(I32).reshape(nb, ne, TB)
    gate_ref[...] = jnp.where(sel > 0.0, a, 0.0).reshape(nb, ne, TB)
    tl = lax.broadcasted_iota(I32, (nr, LANES), 1)
    tab = jnp.where(tl == 0, seg, jnp.where(tl == 1, off_over, jnp.where(
        tl == 2, off_buf, jnp.where(tl == 3, over_blk, jnp.where(tl == 4, rows_exp, tiles_exp)))))
    tab_ref[...] = tab.T[0:8, :].astype(I32)


def _select_call(aff, ngroups, ntok):
    nb = ntok // TB
    cap = EC_FACTOR * ntok // N_EXPERTS
    nr = nb * N_EXPERTS
    return pl.pallas_call(
        functools.partial(_select_kernel, nb=nb, cap=cap),
        grid=(ngroups,),
        in_specs=[pl.BlockSpec((nb, N_EXPERTS, TB), lambda g: (g, 0, 0))],
        out_specs=[pl.BlockSpec((None, nb, N_EXPERTS, TB), lambda g: (g, 0, 0, 0)),
                   pl.BlockSpec((None, nb, N_EXPERTS, TB), lambda g: (g, 0, 0, 0)),
                   pl.BlockSpec((None, 8, nr), lambda g: (g, 0, 0))],
        out_shape=[jax.ShapeDtypeStruct((ngroups, nb, N_EXPERTS, TB), I32),
                   jax.ShapeDtypeStruct((ngroups, nb, N_EXPERTS, TB), F32),
                   jax.ShapeDtypeStruct((ngroups, 8, nr), I32)],
        compiler_params=_cparams(("arbitrary",)),
        name="select",
    )(aff)


STACK_ROWS = N_EXPERTS * TB


FIRST_ROWS = 3 * TB


WINR = 3 * SEG
assert N_EXPERTS * WINR == FIRST_ROWS
TF = 672
TIE_STEPS = 12


def _window_rows(rowid_ref, dst, value_ref=None):
    for e in range(N_EXPERTS):
        rid_e = rowid_ref[e:e + 1, :]
        val_e = 1.0 if value_ref is None else value_ref[e:e + 1, :]
        hit = rid_e == lax.broadcasted_iota(I32, (WINR, TB), 0) + e * WINR
        dst[e * WINR:(e + 1) * WINR, :] = jnp.where(hit, val_e, 0.0).astype(dst.dtype)


class _Table:
    def __init__(self, tab_s, row, nb, per_block=False):
        self.tab_s, self.row, self.per = tab_s, row, nb if per_block else nb * N_EXPERTS
        self.scale = N_EXPERTS if per_block else 1

    def __getitem__(self, k):
        if self.per & (self.per - 1) == 0:
            hi, low = lax.shift_right_logical(k, self.per.bit_length() - 1), k & (self.per - 1)
        else:
            hi, low = lax.div(k, self.per), lax.rem(k, self.per)
        return self.tab_s[hi, self.row, low * self.scale]


def _tables(tab_s, nb):
    return (_Table(tab_s, 0, nb), _Table(tab_s, 1, nb), _Table(tab_s, 2, nb),
            _Table(tab_s, 3, nb, per_block=True))


def _overflow_groups(seg_s, k):
    return lax.div(jnp.maximum(seg_s[k] - WINR, 0), SEG)


def _overflow_rows(seg_s, offo_s, step, over, rowid_ref, dst, value_ref=None):
    def zero(i, carry):
        r0 = pl.multiple_of(FIRST_ROWS + i * SEG, SEG)
        dst[pl.ds(r0, SEG), :] = jnp.zeros((SEG, TB), dst.dtype)
        return carry

    lax.fori_loop(0, ((over + TB - 1) // TB) * (TB // SEG), zero, 0)
    for e in range(N_EXPERTS):
        k = step * N_EXPERTS + e
        rid_e = rowid_ref[e:e + 1, :]
        val_e = 1.0 if value_ref is None else value_ref[e:e + 1, :]

        def group(i, carry, off=offo_s[k], rid_e=rid_e, val_e=val_e):
            r0 = pl.multiple_of(off + i * SEG, SEG)
            hit = rid_e == lax.broadcasted_iota(I32, (SEG, TB), 0) + r0
            dst[pl.ds(r0, SEG), :] = jnp.where(hit, val_e, 0.0).astype(dst.dtype)
            return carry

        lax.fori_loop(0, _overflow_groups(seg_s, k), group, 0)


def _wait_rows(rows, make_copy):
    def big(i, carry):
        make_copy(TB).wait()
        return carry

    def small(i, carry):
        make_copy(SEG).wait()
        return carry

    lax.fori_loop(0, lax.div(rows, TB), big, 0)
    lax.fori_loop(0, lax.div(lax.rem(rows, TB), SEG), small, 0)


def _dispatch_kernel(tab_s, h_ref, rowid_ref, xe_hbm,
                     onehot, stack, zbuf, sem, zsem, *, nb, nsteps, cap):
    g = pl.program_id(0)
    b = pl.program_id(1)
    step = g * nb + b
    slot = lax.rem(step, 2)
    seg_s, offo_s, offb_s, over_s = _tables(tab_s, nb)
    over = over_s[step]
    xrows = xe_hbm.shape[2]

    def wait_slot(nrows, sl):
        _wait_rows(nrows, lambda n: pltpu.make_async_copy(
            stack.at[sl, pl.ds(0, n)], xe_hbm.at[0, 0, pl.ds(0, n)], sem.at[sl]))

    @pl.when(step == 0)
    def _init():
        stack[...] = jnp.zeros_like(stack)

    @pl.when(b == 0)
    def _zero_unused():
        zbuf[...] = jnp.zeros_like(zbuf)
        for e in range(N_EXPERTS):
            pltpu.make_async_copy(zbuf, xe_hbm.at[g, e, pl.ds(cap, xrows - cap)], zsem).start()

    _window_rows(rowid_ref, onehot)
    h = h_ref[...]
    stack[slot, 0:FIRST_ROWS, :] = _dot(onehot[0:FIRST_ROWS, :], h).astype(BF16)

    @pl.when(over > 0)
    def _overflow():
        _overflow_rows(seg_s, offo_s, step, over, rowid_ref, onehot)

        def chunk(c, carry):
            base = pl.multiple_of(FIRST_ROWS + c * TB, TB)
            stack[slot, pl.ds(base, TB), :] = _dot(onehot[pl.ds(base, TB), :], h).astype(BF16)
            return carry

        lax.fori_loop(0, (over + TB - 1) // TB, chunk, 0)

    @pl.when(b == 0)
    def _zero_unused_done():
        for e in range(N_EXPERTS):
            pltpu.make_async_copy(zbuf, xe_hbm.at[g, e, pl.ds(cap, xrows - cap)], zsem).wait()

    @pl.when(step >= 1)
    def _previous_landed():
        wait_slot(FIRST_ROWS + over_s[step - 1], 1 - slot)

    for e in range(N_EXPERTS):
        k = step * N_EXPERTS + e
        pltpu.make_async_copy(
            stack.at[slot, e * WINR:(e + 1) * WINR],
            xe_hbm.at[g, e, pl.ds(pl.multiple_of(offb_s[k], SEG), WINR)], sem.at[slot]).start()

    @pl.when(over > 0)
    def _overflow_copies():
        for e in range(N_EXPERTS):
            k = step * N_EXPERTS + e

            def one(i, carry, e=e, k=k):
                pltpu.make_async_copy(
                    stack.at[slot, pl.ds(pl.multiple_of(offo_s[k] + i * SEG, SEG), SEG)],
                    xe_hbm.at[g, e, pl.ds(pl.multiple_of(offb_s[k] + WINR + i * SEG, SEG), SEG)],
                    sem.at[slot]).start()
                return carry

            lax.fori_loop(0, _overflow_groups(seg_s, k), one, 0)

    @pl.when(step == nsteps - 1)
    def _drain():
        wait_slot(FIRST_ROWS + over, slot)


def _expert_rows(ntok):
    cap = EC_FACTOR * ntok // N_EXPERTS
    worst = cap + (ntok // TB) * (SEG - 1)
    tiles = -(-(worst + WINR) // TF)
    assert cap % SEG == 0 and cap >= WINR and tiles * TF > cap
    return cap, tiles


def _dispatch_call(tab, h, rowid, ngroups, ntok):
    nb = ntok // TB
    cap, tiles = _expert_rows(ntok)
    xrows = tiles * TF
    grid_spec = pltpu.PrefetchScalarGridSpec(
        num_scalar_prefetch=1,
        grid=(ngroups, nb),
        in_specs=[pl.BlockSpec((TB, D), lambda g, b, *_: (g * nb + b, 0)),
                  pl.BlockSpec((None, None, N_EXPERTS, TB), lambda g, b, *_: (g, b, 0, 0))],
        out_specs=pl.BlockSpec(memory_space=pl.ANY),
        scratch_shapes=[pltpu.VMEM((STACK_ROWS, TB), BF16), pltpu.VMEM((2, STACK_ROWS, D), BF16),
                        pltpu.VMEM((xrows - cap, D), BF16),
                        pltpu.SemaphoreType.DMA((2,)), pltpu.SemaphoreType.DMA])
    return pl.pallas_call(
        functools.partial(_dispatch_kernel, nb=nb, nsteps=ngroups * nb, cap=cap),
        grid_spec=grid_spec,
        out_shape=jax.ShapeDtypeStruct((ngroups, N_EXPERTS, xrows, D), BF16),
        compiler_params=_cparams(("arbitrary", "arbitrary")),
        name="dispatch",
    )(tab, h, rowid)


def _ffn_kernel(tab_s, xe_ref, wg_hbm, wu_hbm, wd_hbm, y_ref, wg32, wu32, wd32, wgb, wub, wdb, sem,
                *, layer):
    e = pl.program_id(0)
    g = pl.program_id(1)
    j = pl.program_id(2)

    def weight_copies(ee, sl):
        return (pltpu.make_async_copy(wg_hbm.at[layer, ee], wg32.at[sl], sem.at[sl]),
                pltpu.make_async_copy(wu_hbm.at[layer, ee], wu32.at[sl], sem.at[sl]),
                pltpu.make_async_copy(wd_hbm.at[layer, ee], wd32.at[sl], sem.at[sl]))

    @pl.when(jnp.logical_and(g == 0, j == 0))
    def _weights():
        sl = lax.rem(e, 2)

        @pl.when(e == 0)
        def _():
            for cp in weight_copies(e, sl):
                cp.start()

        @pl.when(e + 1 < N_EXPERTS)
        def _():
            for cp in weight_copies(e + 1, 1 - sl):
                cp.start()

        for cp in weight_copies(e, sl):
            cp.wait()
        wgb[...] = wg32[sl].astype(BF16)
        wub[...] = wu32[sl].astype(BF16)
        wdb[...] = wd32[sl].astype(BF16)

    live = j < tab_s[g, 5, e]

    @pl.when(live)
    def _run():
        x = xe_ref[...]
        hid = (_silu(_dot(x, wgb[...])) * _dot(x, wub[...])).astype(BF16)
        y_ref[...] = _dot(hid, wdb[...]).astype(BF16)

    @pl.when(jnp.logical_not(live))
    def _skip():
        y_ref[...] = jnp.zeros_like(y_ref)


def _ffn_call(tab, xe, wg, wu, wd, layer, ngroups, ntok):
    _, tiles = _expert_rows(ntok)

    def xmap(e, g, j, tab_s):
        return (g, e, jnp.minimum(j, tab_s[g, 5, e] - 1), 0)

    grid_spec = pltpu.PrefetchScalarGridSpec(
        num_scalar_prefetch=1,
        grid=(N_EXPERTS, ngroups, tiles),
        in_specs=[pl.BlockSpec((None, None, TF, D), xmap),
                  pl.BlockSpec(memory_space=pl.ANY), pl.BlockSpec(memory_space=pl.ANY),
                  pl.BlockSpec(memory_space=pl.ANY)],
        out_specs=pl.BlockSpec((None, None, TF, D), lambda e, g, j, nt: (g, e, j, 0)),
        scratch_shapes=[pltpu.VMEM((2, D, EXPERT_FF), F32), pltpu.VMEM((2, D, EXPERT_FF), F32),
                        pltpu.VMEM((2, EXPERT_FF, D), F32),
                        pltpu.VMEM((D, EXPERT_FF), BF16), pltpu.VMEM((D, EXPERT_FF), BF16),
                        pltpu.VMEM((EXPERT_FF, D), BF16), pltpu.SemaphoreType.DMA((2,))])
    return pl.pallas_call(
        functools.partial(_ffn_kernel, layer=layer),
        grid_spec=grid_spec,
        out_shape=jax.ShapeDtypeStruct((ngroups, N_EXPERTS, tiles * TF, D), BF16),
        compiler_params=_cparams(("arbitrary", "arbitrary", "arbitrary")),
        name="ffn",
    )(tab, xe, wg, wu, wd)


def _combine_kernel(tab_s, y_hbm, rowid_ref, gate_ref, x_ref, mod_ref,
                    modn_ref, g_ref, xo_ref, h_ref, weights, stack, acc, sem, *, nb, nsteps, final):
    g = pl.program_id(0)
    b = pl.program_id(1)
    step = g * nb + b
    slot = lax.rem(step, 2)
    seg_s, offo_s, offb_s, over_s = _tables(tab_s, nb)
    over = over_s[step]

    def fetch(st, sl):
        gg = lax.div(st, nb)
        for e in range(N_EXPERTS):
            k = st * N_EXPERTS + e
            pltpu.make_async_copy(
                y_hbm.at[gg, e, pl.ds(pl.multiple_of(offb_s[k], SEG), WINR)],
                stack.at[sl, e * WINR:(e + 1) * WINR], sem.at[sl]).start()

        @pl.when(over_s[st] > 0)
        def _():
            for e in range(N_EXPERTS):
                k = st * N_EXPERTS + e

                def one(i, carry, e=e, k=k):
                    pltpu.make_async_copy(
                        y_hbm.at[gg, e, pl.ds(pl.multiple_of(offb_s[k] + WINR + i * SEG, SEG), SEG)],
                        stack.at[sl, pl.ds(pl.multiple_of(offo_s[k] + i * SEG, SEG), SEG)],
                        sem.at[sl]).start()
                    return carry

                lax.fori_loop(0, _overflow_groups(seg_s, k), one, 0)

    @pl.when(step == 0)
    def _first():
        stack[...] = jnp.zeros_like(stack)
        fetch(step, slot)

    if nsteps > 1:
        @pl.when(step + 1 < nsteps)
        def _prefetch():
            fetch(step + 1, 1 - slot)

    _window_rows(rowid_ref, weights, gate_ref)

    @pl.when(over > 0)
    def _():
        _overflow_rows(seg_s, offo_s, step, over, rowid_ref, weights, gate_ref)

    _wait_rows(FIRST_ROWS + over, lambda n: pltpu.make_async_copy(
        y_hbm.at[0, 0, pl.ds(0, n)], stack.at[slot, pl.ds(0, n)], sem.at[slot]))

    def token_weights(base):
        return weights[pl.ds(base, TB), :].T.astype(BF16)

    w = jnp.concatenate([token_weights(c * TB) for c in range(FIRST_ROWS // TB)], axis=1)
    acc[...] = _dot(w, stack[slot, 0:FIRST_ROWS, :])

    @pl.when(over > 0)
    def _():
        def chunk(c, carry):
            base = pl.multiple_of(FIRST_ROWS + c * TB, TB)
            acc[...] += _dot(token_weights(base), stack[slot, pl.ds(base, TB), :])
            return carry

        lax.fori_loop(0, (over + TB - 1) // TB, chunk, 0)

    x = x_ref[...] + mod_ref[5] * acc[...]
    if final:
        y = _rms(x, g_ref[...])

        @pl.when(g == 0)
        def _():
            xo_ref[...] = y

        @pl.when(g != 0)
        def _():
            h_ref[...] = y
    else:
        xo_ref[...] = x
        h_ref[...] = _modulate(x, g_ref[...], modn_ref[0], modn_ref[1]).astype(BF16)


def _combine_call(tab, y, rowid, gate, x, mods, layer, nxt, gains, gidx, modrow, ngroups, ntok, final):
    nb = ntok // TB
    t = x.shape[0]
    if final:
        assert ngroups == 2
        out_specs = [pl.BlockSpec((TB, D), lambda g, b, *_: (jnp.where(g == 0, b, nb - 1), 0)),
                     pl.BlockSpec((TB, D), lambda g, b, *_: (jnp.where(g == 0, 0, b), 0))]
        out_shape = [jax.ShapeDtypeStruct((ntok, D), F32), jax.ShapeDtypeStruct((ntok, D), F32)]
    else:
        out_specs = [pl.BlockSpec((TB, D), lambda g, b, *_: (g * nb + b, 0)),
                     pl.BlockSpec((TB, D), lambda g, b, *_: (g * nb + b, 0))]
        out_shape = [jax.ShapeDtypeStruct((t, D), F32), jax.ShapeDtypeStruct((t, D), BF16)]
    grid_spec = pltpu.PrefetchScalarGridSpec(
        num_scalar_prefetch=1,
        grid=(ngroups, nb),
        in_specs=[pl.BlockSpec(memory_space=pl.ANY),
                  pl.BlockSpec((None, None, N_EXPERTS, TB), lambda g, b, *_: (g, b, 0, 0)),
                  pl.BlockSpec((None, None, N_EXPERTS, TB), lambda g, b, *_: (g, b, 0, 0)),
                  pl.BlockSpec((TB, D), lambda g, b, *_: (g * nb + b, 0)),
                  pl.BlockSpec((None, None, 6, 1, D),
                               lambda g, b, *_: (layer, modrow(g * nb + b), 0, 0, 0)),
                  pl.BlockSpec((None, None, 6, 1, D),
                               lambda g, b, *_: (nxt, modrow(g * nb + b), 0, 0, 0)),
                  pl.BlockSpec((None, 1, D), lambda g, b, *_: (gidx, 0, 0))],
        out_specs=out_specs,
        scratch_shapes=[pltpu.VMEM((STACK_ROWS, TB), F32), pltpu.VMEM((2, STACK_ROWS, D), BF16),
                        pltpu.VMEM((TB, D), F32), pltpu.SemaphoreType.DMA((2,))])
    return pl.pallas_call(
        functools.partial(_combine_kernel, nb=nb, nsteps=ngroups * nb, final=final),
        grid_spec=grid_spec,
        out_shape=out_shape,
        compiler_params=_cparams(("arbitrary", "arbitrary")),
        name="combine",
    )(tab, y, rowid, gate, x, mods, mods, gains)


def kernel(x_prompt, x_sample, cache_diff_k, cache_diff_v, cache_swa_k, cache_swa_v, cache_mla_ckv, cache_mla_krope, c, c_ctx, w_ada, b_ada, norm_mix, norm_ffn, w_in_even, w_out_even, diff_lambda, diff_subln, swa_sink, w_in_odd, mla_q_norm, w_q_up, mla_kv_norm, w_kv_up, w_out_odd, w_router, w_gate_exp, w_up_exp, w_down_exp, final_norm):
    batch, seq, _ = x_prompt.shape
    dec_batch, dec_seq, _ = x_sample.shape
    past = cache_diff_k.shape[2]
    depth = w_ada.shape[0]
    n_even = w_in_even.shape[0]
    n_odd = w_in_odd.shape[0]
    nc, ns = batch * seq, dec_batch * dec_seq
    assert nc == ns, "the routed-expert kernels take two token groups of equal size"
    assert nc % TM == 0 and dec_seq % TM == 0 and past % 256 == 0 and dec_seq % GRID_W == 0
    assert seq % TQ == 0 and dec_seq % TQ == 0
    ntok = nc

    def modrow_of(tile):
        def modrow(r):
            tok = r * tile
            return jnp.where(tok < nc, 0, 1 + jnp.maximum(tok - nc, 0) // dec_seq)
        return modrow

    modrow, modrow_tb = modrow_of(TM), modrow_of(TB)

    rc = -(-(1 + dec_batch) // 16) * 16
    cvec = jnp.zeros((rc, D), F32).at[0].set(c_ctx).at[1:1 + dec_batch].set(c)
    mods = _ada_call(cvec, w_ada, b_ada).reshape(depth, rc, 6, 1, D)

    w_even_b = w_in_even.astype(BF16)
    w_oute_b = w_out_even.astype(BF16)
    w_outo_b = w_out_odd.astype(BF16)
    kr_pad = jnp.zeros((n_odd, D, LANES), F32).at[:, :, 64:96].set(w_in_odd[:, :, 640:672])
    w_odd_b = jnp.concatenate([w_in_odd[:, :, :640], kr_pad], axis=-1).astype(BF16)
    wq = w_q_up.reshape(n_odd, MLA_Q_RANK, MLA_HEADS, HD + MLA_ROPE)
    wq_b = jnp.pad(wq, ((0, 0), (0, 0), (0, 0), (0, LANES - HD - MLA_ROPE))).reshape(
        n_odd, MLA_Q_RANK, MLA_QW).astype(BF16)
    wkv = w_kv_up.reshape(n_odd, MLA_KV_RANK, MLA_HEADS, 2 * HD)
    wk_b = jnp.pad(wkv[..., :HD], ((0, 0), (0, 0), (0, 0), (0, LANES - HD))).reshape(
        n_odd, MLA_KV_RANK, MLA_QW).astype(BF16)
    wv_b = wkv[..., HD:].reshape(n_odd, MLA_KV_RANK, D).astype(BF16)
    w_router_b = jnp.pad(w_router, ((0, 0), (0, 0), (0, LANES - N_EXPERTS))).astype(BF16)
    even_tabs = _rope_tables(dec_seq, 16, _even_lane)
    mla_tabs = _rope_tables(dec_seq, 8, _mla_lane)
    ckr = jnp.zeros((dec_batch, n_odd, past, LANES), F32).at[..., 64:96].set(cache_mla_krope)
    even_caches = (cache_diff_k, cache_diff_v, cache_swa_k, cache_swa_v)

    mix_gains = jnp.concatenate([norm_mix, final_norm[None]], axis=0).reshape(depth + 1, 1, D)
    ffn_gains = norm_ffn.reshape(depth, 1, D)
    q_gains = mla_q_norm.reshape(n_odd, 1, MLA_Q_RANK)
    kv_gains = mla_kv_norm.reshape(n_odd, 1, MLA_KV_RANK)
    sublns = diff_subln.reshape(n_even, 1, 2 * HD)

    xa, xb, xb_off = x_prompt.reshape(nc, D), x_sample.reshape(ns, D), 0
    h = _norm_mod_call(xa, xb, mods, 0, mix_gains, modrow)
    even_state = odd_state = None
    y_prompt = y_sample = None
    for i in range(depth):
        j = i // 2
        if i % 2 == 0:
            qkv_c, *even_state = _proj_even_call(h, w_even_b, j, 0, nc, None, dec_seq,
                                                 prev=even_state, seq=seq)
            (qkv_l,) = _proj_even_call(h, w_even_b, j, nc, ns, even_tabs, dec_seq)
            li = _lambda_init(i)
            o_c = _attn_even_call(qkv_c, None, j, diff_lambda, sublns, swa_sink, batch, seq, 0, li)
            o_l = _attn_even_call(qkv_l, even_caches, j, diff_lambda, sublns, swa_sink,
                                  dec_batch, dec_seq, past, li)
            w_out = w_oute_b
        else:
            q_c, *odd_state = _proj_odd_call(h, w_odd_b, q_gains, wq_b, kv_gains, j, 0, nc, None,
                                             dec_seq, prev=odd_state, seq=seq)
            q_l, ckv_l, kr_l = _proj_odd_call(h, w_odd_b, q_gains, wq_b, kv_gains, j, nc, ns,
                                              mla_tabs, dec_seq)
            o_c = _attn_odd_call(q_c, odd_state[0], odd_state[1], None, j, wk_b, wv_b, batch, seq, 0)
            o_l = _attn_odd_call(q_l, ckv_l, kr_l, (cache_mla_ckv, ckr), j, wk_b, wv_b,
                                 dec_batch, dec_seq, past)
            w_out = w_outo_b
        x, h2, aff = _outproj_call(xa, xb, xb_off, o_c, o_l, w_out, j, mods, i, ffn_gains,
                                   w_router_b, modrow)
        rowid, gate, tab = _select_call(aff, 2, ntok)
        xe = _dispatch_call(tab, h2, rowid, 2, ntok)
        y = _ffn_call(tab, xe, w_gate_exp, w_up_exp, w_down_exp, i, 2, ntok)
        final = i == depth - 1
        nxt = i if final else i + 1
        out_a, out_b = _combine_call(tab, y, rowid, gate, x, mods, i, nxt, mix_gains,
                                     depth if final else nxt, modrow_tb, 2, ntok, final)
        if final:
            y_prompt = out_a.reshape(batch, seq, D)
            y_sample = out_b.reshape(dec_batch, dec_seq, D)
        else:
            x, h = out_a, out_b
            xa, xb, xb_off = x, x, nc // TM

    kd, vd, ks, vs = even_state
    ckv_new, kr_new = odd_state
    return (y_prompt, y_sample,
            kd.reshape(batch, n_even, seq, DIFF_HEADS, 2 * HD),
            vd.reshape(batch, n_even, seq, DIFF_HEADS, 2 * HD),
            ks.reshape(batch, n_even, seq, 2, HD), vs.reshape(batch, n_even, seq, 2, HD),
            ckv_new, kr_new[..., 64:96])
```

```python
import functools
import math

import jax
import jax.numpy as jnp
import numpy as np
from jax import lax
from jax.experimental import pallas as pl
from jax.experimental.pallas import tpu as pltpu

F32 = jnp.float32
BF16 = jnp.bfloat16
I32 = jnp.int32

D = 1024
HD = 64
GRID_W = 64
WINDOW = 128
DIFF_HEADS = 4
SWA_HEADS = 8
MLA_HEADS = 16
MLA_Q_RANK = 384
MLA_KV_RANK = 256
MLA_ROPE = 32
N_EXPERTS = 16
EXPERT_FF = 512
EC_FACTOR = 2
ROPE_BASE = 10000.0
EPS = 1e-6
NEG_INF = -1e30
LOG2E = math.log2(math.e)
EVEN_IN = 2304
LANES = 128
TM = 512
TQ = 256
TQ_LATENT = 512
TB = 256
SEG = 16
VMEM_LIMIT = 56 * 1024 * 1024


def _cparams(sem, vmem=VMEM_LIMIT):
    return pltpu.CompilerParams(dimension_semantics=sem, vmem_limit_bytes=vmem)


def _dot(a, b):
    return jnp.dot(a, b, preferred_element_type=F32)


def _dot_nt(a, b):
    return lax.dot_general(a, b, (((1,), (1,)), ((), ())), preferred_element_type=F32)


def _silu(x):
    return x / (1.0 + jnp.exp(-x))


def _rms(x, g):
    ms = jnp.mean(x * x, axis=-1, keepdims=True)
    return x * lax.rsqrt(ms + EPS) * g


def _modulate(x, g, shift, scale):
    return _rms(x, g) * (1.0 + scale) + shift


def _lambda_init(layer):
    return 0.8 - 0.6 * math.exp(-0.3 * layer)


def _ada_kernel(c_ref, w_ref, b_ref, o_ref):
    s = _silu(c_ref[...]).astype(BF16)
    o_ref[...] = _dot(s, w_ref[...].astype(BF16)) + b_ref[...]


def _ada_call(cvec, w_ada, b_ada):
    depth, _, n6 = w_ada.shape
    rc = cvec.shape[0]
    tn = 512
    return pl.pallas_call(
        _ada_kernel,
        grid=(depth, n6 // tn),
        in_specs=[pl.BlockSpec((rc, D), lambda i, n: (0, 0)),
                  pl.BlockSpec((None, D, tn), lambda i, n: (i, 0, n)),
                  pl.BlockSpec((None, 1, tn), lambda i, n: (i, 0, n))],
        out_specs=pl.BlockSpec((None, rc, tn), lambda i, n: (i, 0, n)),
        out_shape=jax.ShapeDtypeStruct((depth, rc, n6), F32),
        compiler_params=_cparams(("parallel", "parallel")),
        name="ada",
    )(cvec, w_ada, b_ada.reshape(depth, 1, n6))


def _two_part_specs(rows_a, rows_b, width, off_b=0):
    na, nb_ = rows_a // TM, rows_b // TM
    return (pl.BlockSpec((TM, width), lambda r: (jnp.minimum(r, na - 1), 0)),
            pl.BlockSpec((TM, width), lambda r: (off_b + jnp.clip(r - na, 0, nb_ - 1), 0)))


def _norm_mod_kernel(xa_ref, xb_ref, mod_ref, g_ref, h_ref, *, ntile_a):
    def run(x_ref):
        h_ref[...] = _modulate(x_ref[...], g_ref[...], mod_ref[0], mod_ref[1]).astype(BF16)

    @pl.when(pl.program_id(0) < ntile_a)
    def _():
        run(xa_ref)

    @pl.when(pl.program_id(0) >= ntile_a)
    def _():
        run(xb_ref)


def _norm_mod_call(xa, xb, mods, layer, gains, modrow):
    na, nb_ = xa.shape[0], xb.shape[0]
    return pl.pallas_call(
        functools.partial(_norm_mod_kernel, ntile_a=na // TM),
        grid=((na + nb_) // TM,),
        in_specs=[*_two_part_specs(na, nb_, D),
                  pl.BlockSpec((None, None, 6, 1, D), lambda r: (layer, modrow(r), 0, 0, 0)),
                  pl.BlockSpec((None, 1, D), lambda r: (layer, 0, 0))],
        out_specs=pl.BlockSpec((TM, D), lambda r: (r, 0)),
        out_shape=jax.ShapeDtypeStruct((na + nb_, D), BF16),
        compiler_params=_cparams(("parallel",)),
        name="norm_mod",
    )(xa, xb, mods, gains)


def _rope_tables(dec_seq, half, lane_of_dim):
    pos = jnp.arange(dec_seq)
    row = (pos // GRID_W).astype(F32)
    col = (pos % GRID_W).astype(F32)
    inv = ROPE_BASE ** (-(jnp.arange(half, dtype=F32) / half))
    ang = jnp.stack([row[:, None] * inv[None, :], col[:, None] * inv[None, :]])
    info = [lane_of_dim(lane) for lane in range(LANES)]
    axis = np.array([0 if i is None else i[0] for i in info])
    freq = np.array([0 if i is None else i[1] for i in info])
    first = np.array([i is not None and not i[2] for i in info])[None, :]
    second = np.array([i is not None and i[2] for i in info])[None, :]
    lane_ang = ang[axis, :, freq].T
    cos, sin = jnp.cos(lane_ang), jnp.sin(lane_ang)
    return (jnp.where(first | second, cos, 1.0), jnp.where(first, -sin, 0.0),
            jnp.where(second, sin, 0.0))


def _even_lane(lane):
    j = lane % HD
    axis, jj = j // 32, j % 32
    return axis, jj % 16, jj >= 16


def _mla_lane(lane):
    if lane < 64 or lane >= 96:
        return None
    jj = lane - 64
    axis, k = jj // 16, jj % 16
    return axis, k % 8, k >= 8


def _rope(x, c, s1, s2, shift):
    return x * c + pltpu.roll(x, LANES - shift, 1) * s1 + pltpu.roll(x, shift, 1) * s2


_EVEN_ROPE_TILES = tuple(range(0, 8)) + tuple(range(12, 17))
_EVEN_Q_TILES = tuple(range(0, 4)) + tuple(range(12, 16))


def _append_layer(prev_refs, out_refs, new_values):
    for i, (out_ref, new) in enumerate(zip(out_refs, new_values)):
        nbatch, nlayers, seq, width = out_ref.shape
        if prev_refs:
            out_ref[:, 0:nlayers - 1] = prev_refs[i][...]
        out_ref[:, nlayers - 1] = new.reshape(nbatch, seq, width)


def _state_specs(prev, widths, seq):
    nlayers = 1 if prev is None else prev[0].shape[1] + 1
    per = TM // seq
    ins = [] if prev is None else [pl.BlockSpec((per, nlayers - 1, seq, w), lambda r: (r, 0, 0, 0))
                                   for w in widths]
    outs = [pl.BlockSpec((per, nlayers, seq, w), lambda r: (r, 0, 0, 0)) for w in widths]
    return ins, outs, nlayers


def _proj_even_kernel(*refs, rope, caches, nprev):
    h_ref, w_ref = refs[0], refs[1]
    pos = 2
    if rope:
        c_ref, s1_ref, s2_ref = refs[2:5]
        pos = 5
    prev_refs = refs[pos:pos + nprev]
    pos += nprev
    qkv_ref = refs[pos]
    res = _dot(h_ref[...], w_ref[...])
    scale = HD ** -0.5 * LOG2E
    for t in range(EVEN_IN // LANES):
        x = res[:, t * LANES:(t + 1) * LANES]
        if rope and t in _EVEN_ROPE_TILES:
            x = _rope(x, c_ref[...], s1_ref[...], s2_ref[...], 16)
        if t in _EVEN_Q_TILES:
            x = x * scale
        qkv_ref[:, t * LANES:(t + 1) * LANES] = x.astype(BF16)
    if caches:
        _append_layer(prev_refs, refs[pos + 1:pos + 5],
                      (res[:, 512:1024], res[:, 1024:1536], res[:, 2048:2176], res[:, 2176:2304]))


def _proj_even_call(h, w, j, row0, nrows, tables, dec_seq, prev=None, seq=None):
    rope = tables is not None
    caches = not rope
    t0 = row0 // TM
    in_specs = [pl.BlockSpec((TM, D), lambda r: (t0 + r, 0)),
                pl.BlockSpec((None, D, EVEN_IN), lambda r: (j, 0, 0))]
    args = [h, w]
    if rope:
        per = dec_seq // TM
        for _ in range(3):
            in_specs.append(pl.BlockSpec((TM, LANES), lambda r: (r % per, 0)))
        args += list(tables)
    out_specs = [pl.BlockSpec((TM, EVEN_IN), lambda r: (r, 0))]
    out_shape = [jax.ShapeDtypeStruct((nrows, EVEN_IN), BF16)]
    nprev = 0
    if caches:
        widths = (512, 512, LANES, LANES)
        ins, outs, nlayers = _state_specs(prev, widths, seq)
        nprev = len(ins)
        in_specs += ins
        args += [] if prev is None else list(prev)
        out_specs += outs
        out_shape += [jax.ShapeDtypeStruct((nrows // seq, nlayers, seq, wd), F32) for wd in widths]
    return pl.pallas_call(
        functools.partial(_proj_even_kernel, rope=rope, caches=caches, nprev=nprev),
        grid=(nrows // TM,),
        in_specs=in_specs, out_specs=out_specs, out_shape=out_shape,
        compiler_params=_cparams(("parallel",)),
        name="proj_even_lat" if rope else "proj_even_ctx",
    )(*args)


ODD_IN_PAD = MLA_Q_RANK + MLA_KV_RANK + LANES
MLA_QW = MLA_HEADS * LANES


def _proj_odd_kernel(*refs, rope, nprev):
    h_ref, w_ref, qn_ref, wq_ref, kvn_ref = refs[:5]
    pos = 5
    if rope:
        c_ref, s1_ref, s2_ref = refs[5:8]
        pos = 8
    prev_refs = refs[pos:pos + nprev]
    pos += nprev
    q_ref, ckv_ref, kr_ref = refs[pos:pos + 3]
    res = _dot(h_ref[...], w_ref[...])
    cq = _rms(res[:, :MLA_Q_RANK], qn_ref[...]).astype(BF16)
    ckv = _rms(res[:, MLA_Q_RANK:MLA_Q_RANK + MLA_KV_RANK], kvn_ref[...])
    kr = res[:, MLA_Q_RANK + MLA_KV_RANK:]
    if rope:
        kr = _rope(kr, c_ref[...], s1_ref[...], s2_ref[...], 8)
        ckv_ref[...] = ckv
        kr_ref[...] = kr
    else:
        _append_layer(prev_refs, (ckv_ref, kr_ref), (ckv, kr))
    q = _dot(cq, wq_ref[...])
    scale = (HD + MLA_ROPE) ** -0.5 * LOG2E
    for t in range(MLA_HEADS):
        x = q[:, t * LANES:(t + 1) * LANES]
        if rope:
            x = _rope(x, c_ref[...], s1_ref[...], s2_ref[...], 8)
        q_ref[:, t * LANES:(t + 1) * LANES] = (x * scale).astype(BF16)


def _proj_odd_call(h, w_in, qn, wq, kvn, j, row0, nrows, tables, dec_seq, prev=None, seq=None):
    rope = tables is not None
    t0 = row0 // TM
    in_specs = [pl.BlockSpec((TM, D), lambda r: (t0 + r, 0)),
                pl.BlockSpec((None, D, ODD_IN_PAD), lambda r: (j, 0, 0)),
                pl.BlockSpec((None, 1, MLA_Q_RANK), lambda r: (j, 0, 0)),
                pl.BlockSpec((None, MLA_Q_RANK, MLA_QW), lambda r: (j, 0, 0)),
                pl.BlockSpec((None, 1, MLA_KV_RANK), lambda r: (j, 0, 0))]
    args = [h, w_in, qn, wq, kvn]
    if rope:
        per = dec_seq // TM
        for _ in range(3):
            in_specs.append(pl.BlockSpec((TM, LANES), lambda r: (r % per, 0)))
        args += list(tables)
    out_specs = [pl.BlockSpec((TM, MLA_QW), lambda r: (r, 0))]
    out_shape = [jax.ShapeDtypeStruct((nrows, MLA_QW), BF16)]
    widths = (MLA_KV_RANK, LANES)
    nprev = 0
    if rope:
        out_specs += [pl.BlockSpec((TM, wd), lambda r: (r, 0)) for wd in widths]
        out_shape += [jax.ShapeDtypeStruct((nrows, wd), F32) for wd in widths]
    else:
        ins, outs, nlayers = _state_specs(prev, widths, seq)
        nprev = len(ins)
        in_specs += ins
        args += [] if prev is None else list(prev)
        out_specs += outs
        out_shape += [jax.ShapeDtypeStruct((nrows // seq, nlayers, seq, wd), F32) for wd in widths]
    return pl.pallas_call(
        functools.partial(_proj_odd_kernel, rope=rope, nprev=nprev),
        grid=(nrows // TM,),
        in_specs=in_specs,
        out_specs=out_specs,
        out_shape=out_shape,
        compiler_params=_cparams(("parallel",)),
        name="proj_odd_lat" if rope else "proj_odd_ctx",
    )(*args)


def _query_tile(past):
    return TQ_LATENT if past > 0 else TQ


def _ones_lane(half):
    return HD if half == 0 else 0


def _row_sum(e, o, half, from_matmul):
    if from_matmul:
        one = _ones_lane(half)
        return o[:, one:one + 1]
    return jnp.sum(e, axis=-1, keepdims=True)


def _half_values(v, half):
    lane = lax.broadcasted_iota(I32, (1, LANES), 1)
    keep = (lane < HD) if half == 0 else (lane >= HD)
    return jnp.where(keep, v, jnp.where(lane == _ones_lane(half), 1.0, 0.0)).astype(BF16)


VDT_ROWS = 2 * HD + SEG


def _attn_even_kernel(*refs, seq, past, lam_init, layer):
    latent = past > 0
    tq = _query_tile(past)
    n = past + seq
    if latent:
        (qkv_ref, ck_ref, cv_ref, sk_ref, sv_ref, lam_ref, subln_ref, sink_ref,
         o_ref, kd, vd, ka, vl, vh) = refs
    else:
        qkv_ref, lam_ref, subln_ref, sink_ref, o_ref, kd, vd, ka, vl, vh = refs
    qi = pl.program_id(1)
    lo = lax.broadcasted_iota(I32, (1, LANES), 1) < HD

    @pl.when(qi == 0)
    def _build():
        chunk = 256
        for c0 in range(0, n, chunk):
            rows = slice(c0, c0 + chunk)
            if c0 < past:
                prow = slice(c0, c0 + chunk)
                for h in range(DIFF_HEADS):
                    kd[rows, h * LANES:(h + 1) * LANES] = ck_ref[prow, h, :].astype(BF16)
                    vd[h * VDT_ROWS:h * VDT_ROWS + LANES, rows] = cv_ref[prow, h, :].T.astype(BF16)
                kt = jnp.concatenate([sk_ref[prow, 0, :], sk_ref[prow, 1, :]], axis=1)
                vt = jnp.concatenate([sv_ref[prow, 0, :], sv_ref[prow, 1, :]], axis=1)
            else:
                orow = slice(c0 - past, c0 - past + chunk)
                kd[rows, :] = qkv_ref[orow, 512:1024]
                for h in range(DIFF_HEADS):
                    vown = qkv_ref[orow, 1024 + h * LANES:1024 + (h + 1) * LANES].astype(F32)
                    vd[h * VDT_ROWS:h * VDT_ROWS + LANES, rows] = vown.T.astype(BF16)
                kt = qkv_ref[orow, 2048:2176].astype(F32)
                vt = qkv_ref[orow, 2176:2304].astype(F32)
            kr = pltpu.roll(kt, HD, 1)
            vr = pltpu.roll(vt, HD, 1)
            ka[0, rows, :] = jnp.where(lo, kt, kr).astype(BF16)
            ka[1, rows, :] = jnp.where(lo, kr, kt).astype(BF16)
            vl[0, rows, :] = _half_values(vt, 0)
            vh[0, rows, :] = _half_values(vr, 1)
            vl[1, rows, :] = _half_values(vr, 0)
            vh[1, rows, :] = _half_values(vt, 1)
        for h in range(DIFF_HEADS):
            vd[h * VDT_ROWS + LANES:(h + 1) * VDT_ROWS, :] = jnp.ones((SEG, n), BF16)

    r0 = pl.multiple_of(qi * tq, tq)
    lam = lam_ref[...]
    lam_full = (jnp.exp(jnp.sum(lam[0:1] * lam[1:2], axis=-1, keepdims=True))
                - jnp.exp(jnp.sum(lam[2:3] * lam[3:4], axis=-1, keepdims=True)) + lam_init)
    zero_b = jnp.zeros((), BF16)

    for h in range(DIFF_HEADS):
        cs = slice(h * LANES, (h + 1) * LANES)
        qt = qkv_ref[pl.ds(r0, tq), cs]
        kh = kd[:, cs]
        ots = []
        for comp in range(2):
            qc = jnp.where(lo, qt, zero_b) if comp == 0 else jnp.where(lo, zero_b, qt)
            s = _dot_nt(qc, kh)
            m = jnp.max(s, axis=-1, keepdims=True)
            e = jnp.exp2(s - m).astype(BF16)
            ots.append(_dot_nt(vd[h * VDT_ROWS:(h + 1) * VDT_ROWS, :], e))
        ot = (ots[0][0:LANES, :] * (1.0 / ots[0][LANES:LANES + 1, :])
              - ots[1][0:LANES, :] * (lam_full / ots[1][LANES:LANES + 1, :]))
        o = _rms(ot.T, subln_ref[...]) * (1.0 - lam_init)
        o_ref[:, cs] = o.astype(BF16)

    nblk = seq // WINDOW
    dense = past if latent else seq
    if latent:
        per = tq // WINDOW
        offsets = tuple(range(-1, per + 1))
        rr = lax.broadcasted_iota(I32, (tq, WINDOW), 0)
        cc = lax.broadcasted_iota(I32, (tq, WINDOW), 1)
        band, starts = {}, {}
        for d in offsets:
            blk = qi * per + d
            inside = jnp.logical_and(blk >= 0, blk < nblk)
            band[d] = jnp.logical_and(jnp.abs(rr - cc - d * WINDOW) <= WINDOW, inside)
            starts[d] = pl.multiple_of(past + jnp.clip(blk, 0, nblk - 1) * WINDOW, WINDOW)
    for i in range(SWA_HEADS // 2):
        hk = i // 2
        cs = slice(1536 + i * LANES, 1536 + (i + 1) * LANES)
        qt = qkv_ref[pl.ds(r0, tq), cs]
        halves = []
        for half in range(2):
            qc = jnp.where(lo, qt, zero_b) if half == 0 else jnp.where(lo, zero_b, qt)
            vsel = vl if half == 0 else vh
            sink = sink_ref[layer, 2 * i + half] * LOG2E
            parts = [_dot_nt(qc, ka[hk, 0:dense, :])]
            if latent:
                for d in offsets:
                    s = _dot_nt(qc, ka[hk, pl.ds(starts[d], WINDOW), :])
                    parts.append(jnp.where(band[d], s, NEG_INF))
            s_all = jnp.concatenate(parts, axis=1) if len(parts) > 1 else parts[0]
            m = jnp.maximum(jnp.max(s_all, axis=-1, keepdims=True), sink)
            e = jnp.exp2(s_all - m)
            eb = e.astype(BF16)
            o = _dot(eb[:, 0:dense], vsel[hk, 0:dense, :])
            if latent:
                for k, d in enumerate(offsets):
                    o += _dot(eb[:, dense + k * WINDOW:dense + (k + 1) * WINDOW],
                              vsel[hk, pl.ds(starts[d], WINDOW), :])
            den = _row_sum(e, o, half, from_matmul=latent) + jnp.exp2(sink - m)
            halves.append(o * (1.0 / den))
        o_ref[:, 512 + i * LANES:512 + (i + 1) * LANES] = jnp.where(lo, halves[0], halves[1]).astype(BF16)


def _attn_even_call(qkv, caches, j, lam, subln, sink, nbatch, seq, past, lam_init):
    n = past + seq
    latent = past > 0
    tq = _query_tile(past)
    in_specs = [pl.BlockSpec((seq, EVEN_IN), lambda b, q: (b, 0))]
    args = [qkv]
    if latent:
        ck, cv, sk, sv = caches
        in_specs += [pl.BlockSpec((None, None, past, DIFF_HEADS, 2 * HD), lambda b, q: (b, j, 0, 0, 0)),
                     pl.BlockSpec((None, None, past, DIFF_HEADS, 2 * HD), lambda b, q: (b, j, 0, 0, 0)),
                     pl.BlockSpec((None, None, past, 2, HD), lambda b, q: (b, j, 0, 0, 0)),
                     pl.BlockSpec((None, None, past, 2, HD), lambda b, q: (b, j, 0, 0, 0))]
        args += [ck, cv, sk, sv]
    in_specs += [pl.BlockSpec((None, 4, HD), lambda b, q: (j, 0, 0)),
                 pl.BlockSpec((None, 1, 2 * HD), lambda b, q: (j, 0, 0)),
                 pl.BlockSpec(memory_space=pltpu.SMEM)]
    args += [lam, subln, sink]
    return pl.pallas_call(
        functools.partial(_attn_even_kernel, seq=seq, past=past, lam_init=lam_init, layer=j),
        grid=(nbatch, seq // tq),
        in_specs=in_specs,
        out_specs=pl.BlockSpec((tq, D), lambda b, q: (b * (seq // tq) + q, 0)),
        out_shape=jax.ShapeDtypeStruct((nbatch * seq, D), BF16),
        scratch_shapes=[pltpu.VMEM((n, 512), BF16),
                        pltpu.VMEM((DIFF_HEADS * VDT_ROWS, n), BF16),
                        pltpu.VMEM((2, n, LANES), BF16), pltpu.VMEM((2, n, LANES), BF16),
                        pltpu.VMEM((2, n, LANES), BF16)],
        compiler_params=_cparams(("arbitrary", "arbitrary")),
        name="attn_even_lat" if latent else "attn_even_ctx",
    )(*args)


VT_ROWS = HD + SEG


def _attn_odd_kernel(*refs, seq, past):
    latent = past > 0
    tq = _query_tile(past)
    n = past + seq
    if latent:
        q_ref, ckv_ref, kr_ref, cckv_ref, ckr_ref, wk_ref, wv_ref, o_ref, kf, vt = refs
    else:
        q_ref, ckv_ref, kr_ref, wk_ref, wv_ref, o_ref, kf, vt = refs
    qi = pl.program_id(1)

    @pl.when(qi == 0)
    def _build():
        chunk = 256
        for c0 in range(0, n, chunk):
            rows = slice(c0, c0 + chunk)
            if c0 < past:
                ckv = cckv_ref[c0:c0 + chunk, :].astype(BF16)
                kr = ckr_ref[c0:c0 + chunk, :]
            else:
                ckv = ckv_ref[c0 - past:c0 - past + chunk, :].astype(BF16)
                kr = kr_ref[c0 - past:c0 - past + chunk, :]
            kk = _dot(ckv, wk_ref[...])
            for h in range(MLA_HEADS):
                cs = slice(h * LANES, (h + 1) * LANES)
                kf[rows, cs] = (kk[:, cs] + kr).astype(BF16)
            vv = _dot(ckv, wv_ref[...])
            for i in range(MLA_HEADS // 2):
                cs = slice(i * LANES, (i + 1) * LANES)
                pair = vv[:, cs].T.astype(BF16)
                for half in range(2):
                    r = (2 * i + half) * VT_ROWS
                    vt[r:r + HD, rows] = pair[half * HD:(half + 1) * HD, :]
        for h in range(MLA_HEADS):
            vt[h * VT_ROWS + HD:(h + 1) * VT_ROWS, :] = jnp.ones((SEG, n), BF16)

    r0 = pl.multiple_of(qi * tq, tq)
    for i in range(MLA_HEADS // 2):
        halves = []
        for half in range(2):
            h = 2 * i + half
            cs = slice(h * LANES, (h + 1) * LANES)
            s = _dot_nt(q_ref[pl.ds(r0, tq), cs], kf[:, cs])
            m = jnp.max(s, axis=-1, keepdims=True)
            e = jnp.exp2(s - m).astype(BF16)
            ot = _dot_nt(vt[h * VT_ROWS:(h + 1) * VT_ROWS, :], e)
            halves.append(ot[0:HD, :] * (1.0 / ot[HD:HD + 1, :]))
        tile = jnp.concatenate(halves, axis=0).T
        o_ref[:, i * LANES:(i + 1) * LANES] = tile.astype(BF16)


def _attn_odd_call(q, ckv, kr, caches, j, wk, wv, nbatch, seq, past):
    n = past + seq
    latent = past > 0
    tq = _query_tile(past)
    in_specs = [pl.BlockSpec((seq, MLA_QW), lambda b, qq: (b, 0))]
    if latent:
        in_specs += [pl.BlockSpec((seq, MLA_KV_RANK), lambda b, qq: (b, 0)),
                     pl.BlockSpec((seq, LANES), lambda b, qq: (b, 0))]
    else:
        last = ckv.shape[1] - 1
        in_specs += [pl.BlockSpec((None, None, seq, MLA_KV_RANK), lambda b, qq: (b, last, 0, 0)),
                     pl.BlockSpec((None, None, seq, LANES), lambda b, qq: (b, last, 0, 0))]
    args = [q, ckv, kr]
    if latent:
        in_specs += [pl.BlockSpec((None, None, past, MLA_KV_RANK), lambda b, qq: (b, j, 0, 0)),
                     pl.BlockSpec((None, None, past, LANES), lambda b, qq: (b, j, 0, 0))]
        args += list(caches)
    in_specs += [pl.BlockSpec((None, MLA_KV_RANK, MLA_QW), lambda b, qq: (j, 0, 0)),
                 pl.BlockSpec((None, MLA_KV_RANK, D), lambda b, qq: (j, 0, 0))]
    args += [wk, wv]
    return pl.pallas_call(
        functools.partial(_attn_odd_kernel, seq=seq, past=past),
        grid=(nbatch, seq // tq),
        in_specs=in_specs,
        out_specs=pl.BlockSpec((tq, D), lambda b, qq: (b * (seq // tq) + qq, 0)),
        out_shape=jax.ShapeDtypeStruct((nbatch * seq, D), BF16),
        scratch_shapes=[pltpu.VMEM((n, MLA_QW), BF16), pltpu.VMEM((MLA_HEADS * VT_ROWS, n), BF16)],
        compiler_params=_cparams(("arbitrary", "arbitrary")),
        name="attn_odd_lat" if latent else "attn_odd_ctx",
    )(*args)


def _outproj_kernel(xa_ref, xb_ref, oa_ref, ob_ref, w_ref, mod_ref, g_ref, wr_ref,
                    xo_ref, h_ref, aff_ref, *, ntile_a):
    def run(x_ref, o_ref):
        x = x_ref[...] + mod_ref[2] * _dot(o_ref[...], w_ref[...])
        xo_ref[...] = x
        h = _modulate(x, g_ref[...], mod_ref[3], mod_ref[4]).astype(BF16)
        h_ref[...] = h
        logits = _dot(h, wr_ref[...])
        lane = lax.broadcasted_iota(I32, (TM, LANES), 1)
        lg = jnp.where(lane < N_EXPERTS, logits, -jnp.inf)
        e = jnp.exp(lg - jnp.max(lg, axis=-1, keepdims=True))
        aff = e / jnp.sum(e, axis=-1, keepdims=True)
        for c in range(TM // TB):
            aff_ref[c] = aff[c * TB:(c + 1) * TB, :].T[0:N_EXPERTS, :]

    @pl.when(pl.program_id(0) < ntile_a)
    def _():
        run(xa_ref, oa_ref)

    @pl.when(pl.program_id(0) >= ntile_a)
    def _():
        run(xb_ref, ob_ref)


def _outproj_call(xa, xb, xb_off, o_a, o_b, w, widx, mods, layer, gains, w_router, modrow):
    na, nb_ = o_a.shape[0], o_b.shape[0]
    t = na + nb_
    assert TM % TB == 0
    return pl.pallas_call(
        functools.partial(_outproj_kernel, ntile_a=na // TM),
        grid=(t // TM,),
        in_specs=[*_two_part_specs(na, nb_, D, xb_off), *_two_part_specs(na, nb_, D),
                  pl.BlockSpec((None, D, D), lambda r: (widx, 0, 0)),
                  pl.BlockSpec((None, None, 6, 1, D), lambda r: (layer, modrow(r), 0, 0, 0)),
                  pl.BlockSpec((None, 1, D), lambda r: (layer, 0, 0)),
                  pl.BlockSpec((None, D, LANES), lambda r: (layer, 0, 0))],
        out_specs=[pl.BlockSpec((TM, D), lambda r: (r, 0)),
                   pl.BlockSpec((TM, D), lambda r: (r, 0)),
                   pl.BlockSpec((TM // TB, N_EXPERTS, TB), lambda r: (r, 0, 0))],
        out_shape=[jax.ShapeDtypeStruct((t, D), F32), jax.ShapeDtypeStruct((t, D), BF16),
                   jax.ShapeDtypeStruct((t // TB, N_EXPERTS, TB), F32)],
        compiler_params=_cparams(("parallel",)),
        name="outproj",
    )(xa, xb, o_a, o_b, w, mods, gains, w_router)


def _select_kernel(aff_ref, rowid_ref, gate_ref, tab_ref, *, nb, cap):
    ne = N_EXPERTS
    nr = nb * ne
    a = aff_ref[...].reshape(nr, TB)
    ri = lax.broadcasted_iota(I32, (nr, nr), 0)
    ci = lax.broadcasted_iota(I32, (nr, nr), 1)
    same_e = (ri & (ne - 1)) == (ci & (ne - 1))
    same_b = (ri >> 4) == (ci >> 4)
    m_e = jnp.where(same_e, 1.0, 0.0).astype(BF16)
    m_b = jnp.where(same_b, 1.0, 0.0).astype(BF16)
    m_a = jnp.where(jnp.logical_and(same_e, ci < ri), 1.0, 0.0).astype(BF16)
    m_o = jnp.where(jnp.logical_and(same_b, ci < ri), 1.0, 0.0).astype(BF16)
    ui = lax.broadcasted_iota(I32, (TB, TB), 0)
    uj = lax.broadcasted_iota(I32, (TB, TB), 1)
    upper = jnp.where(ui < uj, 1.0, 0.0).astype(BF16)

    def rows_to_lanes(col):
        return jnp.broadcast_to(col, (nr, LANES)).astype(BF16)

    wide = jnp.concatenate([aff_ref[b] for b in range(nb)], axis=1)

    def count_ge(value):
        return jnp.sum(jnp.where(wide >= value, 1.0, 0.0), axis=-1, keepdims=True)

    def bisect(i, v):
        cand = v | jnp.left_shift(jnp.int32(1), 30 - i)
        return jnp.where(count_ge(pltpu.bitcast(cand, F32)) >= cap, cand, v)

    thr = lax.fori_loop(0, 31, bisect, jnp.zeros((ne, 1), I32))

    def refine(i, lohi):
        lo_e, hi_e = lohi
        mid = 0.5 * (lo_e + hi_e)
        take = count_ge(mid) >= cap
        return jnp.where(take, mid, lo_e), jnp.where(take, hi_e, mid)

    lo_e, hi_e = lax.fori_loop(0, TIE_STEPS, refine,
                               (pltpu.bitcast(thr, F32), pltpu.bitcast(thr + 1, F32)))
    lo_v = jnp.concatenate([lo_e] * nb, axis=0)
    hi_v = jnp.concatenate([hi_e] * nb, axis=0)
    gt = jnp.where(a >= hi_v, 1.0, 0.0)
    eq = jnp.where(jnp.logical_and(a >= lo_v, a < hi_v), 1.0, 0.0)
    n_gt = _dot(m_e, rows_to_lanes(jnp.sum(gt, axis=-1, keepdims=True)))[:, 0:1]
    need = cap - n_gt
    eq_before = (_dot(m_a, rows_to_lanes(jnp.sum(eq, axis=-1, keepdims=True)))[:, 0:1]
                 + _dot(eq.astype(BF16), upper))
    sel = jnp.where(jnp.logical_and(eq > 0.0, eq_before < need), 1.0, gt)
    local = _dot(sel.astype(BF16), upper)
    cnt = jnp.sum(sel, axis=-1, keepdims=True)
    seg = jnp.floor((cnt + (SEG - 1)) * (1.0 / SEG)) * SEG
    segb = rows_to_lanes(seg)
    over = jnp.maximum(seg - WINR, 0.0)
    overb = rows_to_lanes(over)
    off_over = FIRST_ROWS + _dot(m_o, overb)[:, 0:1]
    off_buf = _dot(m_a, segb)[:, 0:1]
    over_blk = _dot(m_b, overb)[:, 0:1]
    rows_exp = _dot(m_e, segb)[:, 0:1]
    tiles_exp = jnp.floor((rows_exp + (TF - 1)) / TF)
    expert = (lax.broadcasted_iota(I32, (nr, 1), 0) & (ne - 1)).astype(F32)
    row = jnp.where(local < WINR, expert * WINR + local, off_over + local - WINR)
    rowid_ref[...] = jnp.where(sel > 0.0, row, -1.0).astype(I32).reshape(nb, ne, TB)
    gate_ref[...] = jnp.where(sel > 0.0, a, 0.0).reshape(nb, ne, TB)
    tl = lax.broadcasted_iota(I32, (nr, LANES), 1)
    tab = jnp.where(tl == 0, seg, jnp.where(tl == 1, off_over, jnp.where(
        tl == 2, off_buf, jnp.where(tl == 3, over_blk, jnp.where(tl == 4, rows_exp, tiles_exp)))))
    tab_ref[...] = tab.T[0:8, :].astype(I32)


def _select_call(aff, ngroups, ntok):
    nb = ntok // TB
    cap = EC_FACTOR * ntok // N_EXPERTS
    nr = nb * N_EXPERTS
    return pl.pallas_call(
        functools.partial(_select_kernel, nb=nb, cap=cap),
        grid=(ngroups,),
        in_specs=[pl.BlockSpec((nb, N_EXPERTS, TB), lambda g: (g, 0, 0))],
        out_specs=[pl.BlockSpec((None, nb, N_EXPERTS, TB), lambda g: (g, 0, 0, 0)),
                   pl.BlockSpec((None, nb, N_EXPERTS, TB), lambda g: (g, 0, 0, 0)),
                   pl.BlockSpec((None, 8, nr), lambda g: (g, 0, 0))],
        out_shape=[jax.ShapeDtypeStruct((ngroups, nb, N_EXPERTS, TB), I32),
                   jax.ShapeDtypeStruct((ngroups, nb, N_EXPERTS, TB), F32),
                   jax.ShapeDtypeStruct((ngroups, 8, nr), I32)],
        compiler_params=_cparams(("arbitrary",)),
        name="select",
    )(aff)


STACK_ROWS = N_EXPERTS * TB


FIRST_ROWS = 3 * TB


WINR = 3 * SEG
assert N_EXPERTS * WINR == FIRST_ROWS
TF = 672
TIE_STEPS = 12


def _window_rows(rowid_ref, dst, value_ref=None):
    for e in range(N_EXPERTS):
        rid_e = rowid_ref[e:e + 1, :]
        val_e = 1.0 if value_ref is None else value_ref[e:e + 1, :]
        hit = rid_e == lax.broadcasted_iota(I32, (WINR, TB), 0) + e * WINR
        dst[e * WINR:(e + 1) * WINR, :] = jnp.where(hit, val_e, 0.0).astype(dst.dtype)


class _Table:
    def __init__(self, tab_s, row, nb, per_block=False):
        self.tab_s, self.row, self.per = tab_s, row, nb if per_block else nb * N_EXPERTS
        self.scale = N_EXPERTS if per_block else 1

    def __getitem__(self, k):
        if self.per & (self.per - 1) == 0:
            hi, low = lax.shift_right_logical(k, self.per.bit_length() - 1), k & (self.per - 1)
        else:
            hi, low = lax.div(k, self.per), lax.rem(k, self.per)
        return self.tab_s[hi, self.row, low * self.scale]


def _tables(tab_s, nb):
    return (_Table(tab_s, 0, nb), _Table(tab_s, 1, nb), _Table(tab_s, 2, nb),
            _Table(tab_s, 3, nb, per_block=True))


def _overflow_groups(seg_s, k):
    return lax.div(jnp.maximum(seg_s[k] - WINR, 0), SEG)


def _overflow_rows(seg_s, offo_s, step, over, rowid_ref, dst, value_ref=None):
    def zero(i, carry):
        r0 = pl.multiple_of(FIRST_ROWS + i * SEG, SEG)
        dst[pl.ds(r0, SEG), :] = jnp.zeros((SEG, TB), dst.dtype)
        return carry

    lax.fori_loop(0, ((over + TB - 1) // TB) * (TB // SEG), zero, 0)
    for e in range(N_EXPERTS):
        k = step * N_EXPERTS + e
        rid_e = rowid_ref[e:e + 1, :]
        val_e = 1.0 if value_ref is None else value_ref[e:e + 1, :]

        def group(i, carry, off=offo_s[k], rid_e=rid_e, val_e=val_e):
            r0 = pl.multiple_of(off + i * SEG, SEG)
            hit = rid_e == lax.broadcasted_iota(I32, (SEG, TB), 0) + r0
            dst[pl.ds(r0, SEG), :] = jnp.where(hit, val_e, 0.0).astype(dst.dtype)
            return carry

        lax.fori_loop(0, _overflow_groups(seg_s, k), group, 0)


def _wait_rows(rows, make_copy):
    def big(i, carry):
        make_copy(TB).wait()
        return carry

    def small(i, carry):
        make_copy(SEG).wait()
        return carry

    lax.fori_loop(0, lax.div(rows, TB), big, 0)
    lax.fori_loop(0, lax.div(lax.rem(rows, TB), SEG), small, 0)


def _dispatch_kernel(tab_s, h_ref, rowid_ref, xe_hbm,
                     onehot, stack, zbuf, sem, zsem, *, nb, nsteps, cap):
    g = pl.program_id(0)
    b = pl.program_id(1)
    step = g * nb + b
    slot = lax.rem(step, 2)
    seg_s, offo_s, offb_s, over_s = _tables(tab_s, nb)
    over = over_s[step]
    xrows = xe_hbm.shape[2]

    def wait_slot(nrows, sl):
        _wait_rows(nrows, lambda n: pltpu.make_async_copy(
            stack.at[sl, pl.ds(0, n)], xe_hbm.at[0, 0, pl.ds(0, n)], sem.at[sl]))

    @pl.when(step == 0)
    def _init():
        stack[...] = jnp.zeros_like(stack)

    @pl.when(b == 0)
    def _zero_unused():
        zbuf[...] = jnp.zeros_like(zbuf)
        for e in range(N_EXPERTS):
            pltpu.make_async_copy(zbuf, xe_hbm.at[g, e, pl.ds(cap, xrows - cap)], zsem).start()

    _window_rows(rowid_ref, onehot)
    h = h_ref[...]
    stack[slot, 0:FIRST_ROWS, :] = _dot(onehot[0:FIRST_ROWS, :], h).astype(BF16)

    @pl.when(over > 0)
    def _overflow():
        _overflow_rows(seg_s, offo_s, step, over, rowid_ref, onehot)

        def chunk(c, carry):
            base = pl.multiple_of(FIRST_ROWS + c * TB, TB)
            stack[slot, pl.ds(base, TB), :] = _dot(onehot[pl.ds(base, TB), :], h).astype(BF16)
            return carry

        lax.fori_loop(0, (over + TB - 1) // TB, chunk, 0)

    @pl.when(b == 0)
    def _zero_unused_done():
        for e in range(N_EXPERTS):
            pltpu.make_async_copy(zbuf, xe_hbm.at[g, e, pl.ds(cap, xrows - cap)], zsem).wait()

    @pl.when(step >= 1)
    def _previous_landed():
        wait_slot(FIRST_ROWS + over_s[step - 1], 1 - slot)

    for e in range(N_EXPERTS):
        k = step * N_EXPERTS + e
        pltpu.make_async_copy(
            stack.at[slot, e * WINR:(e + 1) * WINR],
            xe_hbm.at[g, e, pl.ds(pl.multiple_of(offb_s[k], SEG), WINR)], sem.at[slot]).start()

    @pl.when(over > 0)
    def _overflow_copies():
        for e in range(N_EXPERTS):
            k = step * N_EXPERTS + e

            def one(i, carry, e=e, k=k):
                pltpu.make_async_copy(
                    stack.at[slot, pl.ds(pl.multiple_of(offo_s[k] + i * SEG, SEG), SEG)],
                    xe_hbm.at[g, e, pl.ds(pl.multiple_of(offb_s[k] + WINR + i * SEG, SEG), SEG)],
                    sem.at[slot]).start()
                return carry

            lax.fori_loop(0, _overflow_groups(seg_s, k), one, 0)

    @pl.when(step == nsteps - 1)
    def _drain():
        wait_slot(FIRST_ROWS + over, slot)


def _expert_rows(ntok):
    cap = EC_FACTOR * ntok // N_EXPERTS
    worst = cap + (ntok // TB) * (SEG - 1)
    tiles = -(-(worst + WINR) // TF)
    assert cap % SEG == 0 and cap >= WINR and tiles * TF > cap
    return cap, tiles


def _dispatch_call(tab, h, rowid, ngroups, ntok):
    nb = ntok // TB
    cap, tiles = _expert_rows(ntok)
    xrows = tiles * TF
    grid_spec = pltpu.PrefetchScalarGridSpec(
        num_scalar_prefetch=1,
        grid=(ngroups, nb),
        in_specs=[pl.BlockSpec((TB, D), lambda g, b, *_: (g * nb + b, 0)),
                  pl.BlockSpec((None, None, N_EXPERTS, TB), lambda g, b, *_: (g, b, 0, 0))],
        out_specs=pl.BlockSpec(memory_space=pl.ANY),
        scratch_shapes=[pltpu.VMEM((STACK_ROWS, TB), BF16), pltpu.VMEM((2, STACK_ROWS, D), BF16),
                        pltpu.VMEM((xrows - cap, D), BF16),
                        pltpu.SemaphoreType.DMA((2,)), pltpu.SemaphoreType.DMA])
    return pl.pallas_call(
        functools.partial(_dispatch_kernel, nb=nb, nsteps=ngroups * nb, cap=cap),
        grid_spec=grid_spec,
        out_shape=jax.ShapeDtypeStruct((ngroups, N_EXPERTS, xrows, D), BF16),
        compiler_params=_cparams(("arbitrary", "arbitrary")),
        name="dispatch",
    )(tab, h, rowid)


def _ffn_kernel(tab_s, xe_ref, wg_hbm, wu_hbm, wd_hbm, y_ref, wg32, wu32, wd32, wgb, wub, wdb, sem,
                *, layer):
    e = pl.program_id(0)
    g = pl.program_id(1)
    j = pl.program_id(2)

    def weight_copies(ee, sl):
        return (pltpu.make_async_copy(wg_hbm.at[layer, ee], wg32.at[sl], sem.at[sl]),
                pltpu.make_async_copy(wu_hbm.at[layer, ee], wu32.at[sl], sem.at[sl]),
                pltpu.make_async_copy(wd_hbm.at[layer, ee], wd32.at[sl], sem.at[sl]))

    @pl.when(jnp.logical_and(g == 0, j == 0))
    def _weights():
        sl = lax.rem(e, 2)

        @pl.when(e == 0)
        def _():
            for cp in weight_copies(e, sl):
                cp.start()

        @pl.when(e + 1 < N_EXPERTS)
        def _():
            for cp in weight_copies(e + 1, 1 - sl):
                cp.start()

        for cp in weight_copies(e, sl):
            cp.wait()
        wgb[...] = wg32[sl].astype(BF16)
        wub[...] = wu32[sl].astype(BF16)
        wdb[...] = wd32[sl].astype(BF16)

    live = j < tab_s[g, 5, e]

    @pl.when(live)
    def _run():
        x = xe_ref[...]
        hid = (_silu(_dot(x, wgb[...])) * _dot(x, wub[...])).astype(BF16)
        y_ref[...] = _dot(hid, wdb[...]).astype(BF16)

    @pl.when(jnp.logical_not(live))
    def _skip():
        y_ref[...] = jnp.zeros_like(y_ref)


def _ffn_call(tab, xe, wg, wu, wd, layer, ngroups, ntok):
    _, tiles = _expert_rows(ntok)

    def xmap(e, g, j, tab_s):
        return (g, e, jnp.minimum(j, tab_s[g, 5, e] - 1), 0)

    grid_spec = pltpu.PrefetchScalarGridSpec(
        num_scalar_prefetch=1,
        grid=(N_EXPERTS, ngroups, tiles),
        in_specs=[pl.BlockSpec((None, None, TF, D), xmap),
                  pl.BlockSpec(memory_space=pl.ANY), pl.BlockSpec(memory_space=pl.ANY),
                  pl.BlockSpec(memory_space=pl.ANY)],
        out_specs=pl.BlockSpec((None, None, TF, D), lambda e, g, j, nt: (g, e, j, 0)),
        scratch_shapes=[pltpu.VMEM((2, D, EXPERT_FF), F32), pltpu.VMEM((2, D, EXPERT_FF), F32),
                        pltpu.VMEM((2, EXPERT_FF, D), F32),
                        pltpu.VMEM((D, EXPERT_FF), BF16), pltpu.VMEM((D, EXPERT_FF), BF16),
                        pltpu.VMEM((EXPERT_FF, D), BF16), pltpu.SemaphoreType.DMA((2,))])
    return pl.pallas_call(
        functools.partial(_ffn_kernel, layer=layer),
        grid_spec=grid_spec,
        out_shape=jax.ShapeDtypeStruct((ngroups, N_EXPERTS, tiles * TF, D), BF16),
        compiler_params=_cparams(("arbitrary", "arbitrary", "arbitrary")),
        name="ffn",
    )(tab, xe, wg, wu, wd)


def _combine_kernel(tab_s, y_hbm, rowid_ref, gate_ref, x_ref, mod_ref,
                    modn_ref, g_ref, xo_ref, h_ref, weights, stack, acc, sem, *, nb, nsteps, final):
    g = pl.program_id(0)
    b = pl.program_id(1)
    step = g * nb + b
    slot = lax.rem(step, 2)
    seg_s, offo_s, offb_s, over_s = _tables(tab_s, nb)
    over = over_s[step]

    def fetch(st, sl):
        gg = lax.div(st, nb)
        for e in range(N_EXPERTS):
            k = st * N_EXPERTS + e
            pltpu.make_async_copy(
                y_hbm.at[gg, e, pl.ds(pl.multiple_of(offb_s[k], SEG), WINR)],
                stack.at[sl, e * WINR:(e + 1) * WINR], sem.at[sl]).start()

        @pl.when(over_s[st] > 0)
        def _():
            for e in range(N_EXPERTS):
                k = st * N_EXPERTS + e

                def one(i, carry, e=e, k=k):
                    pltpu.make_async_copy(
                        y_hbm.at[gg, e, pl.ds(pl.multiple_of(offb_s[k] + WINR + i * SEG, SEG), SEG)],
                        stack.at[sl, pl.ds(pl.multiple_of(offo_s[k] + i * SEG, SEG), SEG)],
                        sem.at[sl]).start()
                    return carry

                lax.fori_loop(0, _overflow_groups(seg_s, k), one, 0)

    @pl.when(step == 0)
    def _first():
        stack[...] = jnp.zeros_like(stack)
        fetch(step, slot)

    if nsteps > 1:
        @pl.when(step + 1 < nsteps)
        def _prefetch():
            fetch(step + 1, 1 - slot)

    _window_rows(rowid_ref, weights, gate_ref)

    @pl.when(over > 0)
    def _():
        _overflow_rows(seg_s, offo_s, step, over, rowid_ref, weights, gate_ref)

    _wait_rows(FIRST_ROWS + over, lambda n: pltpu.make_async_copy(
        y_hbm.at[0, 0, pl.ds(0, n)], stack.at[slot, pl.ds(0, n)], sem.at[slot]))

    def token_weights(base):
        return weights[pl.ds(base, TB), :].T.astype(BF16)

    w = jnp.concatenate([token_weights(c * TB) for c in range(FIRST_ROWS // TB)], axis=1)
    acc[...] = _dot(w, stack[slot, 0:FIRST_ROWS, :])

    @pl.when(over > 0)
    def _():
        def chunk(c, carry):
            base = pl.multiple_of(FIRST_ROWS + c * TB, TB)
            acc[...] += _dot(token_weights(base), stack[slot, pl.ds(base, TB), :])
            return carry

        lax.fori_loop(0, (over + TB - 1) // TB, chunk, 0)

    x = x_ref[...] + mod_ref[5] * acc[...]
    if final:
        y = _rms(x, g_ref[...])

        @pl.when(g == 0)
        def _():
            xo_ref[...] = y

        @pl.when(g != 0)
        def _():
            h_ref[...] = y
    else:
        xo_ref[...] = x
        h_ref[...] = _modulate(x, g_ref[...], modn_ref[0], modn_ref[1]).astype(BF16)


def _combine_call(tab, y, rowid, gate, x, mods, layer, nxt, gains, gidx, modrow, ngroups, ntok, final):
    nb = ntok // TB
    t = x.shape[0]
    if final:
        assert ngroups == 2
        out_specs = [pl.BlockSpec((TB, D), lambda g, b, *_: (jnp.where(g == 0, b, nb - 1), 0)),
                     pl.BlockSpec((TB, D), lambda g, b, *_: (jnp.where(g == 0, 0, b), 0))]
        out_shape = [jax.ShapeDtypeStruct((ntok, D), F32), jax.ShapeDtypeStruct((ntok, D), F32)]
    else:
        out_specs = [pl.BlockSpec((TB, D), lambda g, b, *_: (g * nb + b, 0)),
                     pl.BlockSpec((TB, D), lambda g, b, *_: (g * nb + b, 0))]
        out_shape = [jax.ShapeDtypeStruct((t, D), F32), jax.ShapeDtypeStruct((t, D), BF16)]
    grid_spec = pltpu.PrefetchScalarGridSpec(
        num_scalar_prefetch=1,
        grid=(ngroups, nb),
        in_specs=[pl.BlockSpec(memory_space=pl.ANY),
                  pl.BlockSpec((None, None, N_EXPERTS, TB), lambda g, b, *_: (g, b, 0, 0)),
                  pl.BlockSpec((None, None, N_EXPERTS, TB), lambda g, b, *_: (g, b, 0, 0)),
                  pl.BlockSpec((TB, D), lambda g, b, *_: (g * nb + b, 0)),
                  pl.BlockSpec((None, None, 6, 1, D),
                               lambda g, b, *_: (layer, modrow(g * nb + b), 0, 0, 0)),
                  pl.BlockSpec((None, None, 6, 1, D),
                               lambda g, b, *_: (nxt, modrow(g * nb + b), 0, 0, 0)),
                  pl.BlockSpec((None, 1, D), lambda g, b, *_: (gidx, 0, 0))],
        out_specs=out_specs,
        scratch_shapes=[pltpu.VMEM((STACK_ROWS, TB), F32), pltpu.VMEM((2, STACK_ROWS, D), BF16),
                        pltpu.VMEM((TB, D), F32), pltpu.SemaphoreType.DMA((2,))])
    return pl.pallas_call(
        functools.partial(_combine_kernel, nb=nb, nsteps=ngroups * nb, final=final),
        grid_spec=grid_spec,
        out_shape=out_shape,
        compiler_params=_cparams(("arbitrary", "arbitrary")),
        name="combine",
    )(tab, y, rowid, gate, x, mods, mods, gains)


def kernel(x_prompt, x_sample, cache_diff_k, cache_diff_v, cache_swa_k, cache_swa_v, cache_mla_ckv, cache_mla_krope, c, c_ctx, w_ada, b_ada, norm_mix, norm_ffn, w_in_even, w_out_even, diff_lambda, diff_subln, swa_sink, w_in_odd, mla_q_norm, w_q_up, mla_kv_norm, w_kv_up, w_out_odd, w_router, w_gate_exp, w_up_exp, w_down_exp, final_norm):
    batch, seq, _ = x_prompt.shape
    dec_batch, dec_seq, _ = x_sample.shape
    past = cache_diff_k.shape[2]
    depth = w_ada.shape[0]
    n_even = w_in_even.shape[0]
    n_odd = w_in_odd.shape[0]
    nc, ns = batch * seq, dec_batch * dec_seq
    assert nc == ns, "the routed-expert kernels take two token groups of equal size"
    assert nc % TM == 0 and dec_seq % TM == 0 and past % 256 == 0 and dec_seq % GRID_W == 0
    assert seq % TQ == 0 and dec_seq % TQ_LATENT == 0
    ntok = nc

    def modrow_of(tile):
        def modrow(r):
            tok = r * tile
            return jnp.where(tok < nc, 0, 1 + jnp.maximum(tok - nc, 0) // dec_seq)
        return modrow

    modrow, modrow_tb = modrow_of(TM), modrow_of(TB)

    rc = -(-(1 + dec_batch) // 16) * 16
    cvec = jnp.zeros((rc, D), F32).at[0].set(c_ctx).at[1:1 + dec_batch].set(c)
    mods = _ada_call(cvec, w_ada, b_ada).reshape(depth, rc, 6, 1, D)

    w_even_b = w_in_even.astype(BF16)
    w_oute_b = w_out_even.astype(BF16)
    w_outo_b = w_out_odd.astype(BF16)
    kr_pad = jnp.zeros((n_odd, D, LANES), F32).at[:, :, 64:96].set(w_in_odd[:, :, 640:672])
    w_odd_b = jnp.concatenate([w_in_odd[:, :, :640], kr_pad], axis=-1).astype(BF16)
    wq = w_q_up.reshape(n_odd, MLA_Q_RANK, MLA_HEADS, HD + MLA_ROPE)
    wq_b = jnp.pad(wq, ((0, 0), (0, 0), (0, 0), (0, LANES - HD - MLA_ROPE))).reshape(
        n_odd, MLA_Q_RANK, MLA_QW).astype(BF16)
    wkv = w_kv_up.reshape(n_odd, MLA_KV_RANK, MLA_HEADS, 2 * HD)
    wk_b = jnp.pad(wkv[..., :HD], ((0, 0), (0, 0), (0, 0), (0, LANES - HD))).reshape(
        n_odd, MLA_KV_RANK, MLA_QW).astype(BF16)
    wv_b = wkv[..., HD:].reshape(n_odd, MLA_KV_RANK, D).astype(BF16)
    w_router_b = jnp.pad(w_router, ((0, 0), (0, 0), (0, LANES - N_EXPERTS))).astype(BF16)
    even_tabs = _rope_tables(dec_seq, 16, _even_lane)
    mla_tabs = _rope_tables(dec_seq, 8, _mla_lane)
    ckr = jnp.zeros((dec_batch, n_odd, past, LANES), F32).at[..., 64:96].set(cache_mla_krope)
    even_caches = (cache_diff_k, cache_diff_v, cache_swa_k, cache_swa_v)

    mix_gains = jnp.concatenate([norm_mix, final_norm[None]], axis=0).reshape(depth + 1, 1, D)
    ffn_gains = norm_ffn.reshape(depth, 1, D)
    q_gains = mla_q_norm.reshape(n_odd, 1, MLA_Q_RANK)
    kv_gains = mla_kv_norm.reshape(n_odd, 1, MLA_KV_RANK)
    sublns = diff_subln.reshape(n_even, 1, 2 * HD)

    xa, xb, xb_off = x_prompt.reshape(nc, D), x_sample.reshape(ns, D), 0
    h = _norm_mod_call(xa, xb, mods, 0, mix_gains, modrow)
    even_state = odd_state = None
    y_prompt = y_sample = None
    for i in range(depth):
        j = i // 2
        if i % 2 == 0:
            qkv_c, *even_state = _proj_even_call(h, w_even_b, j, 0, nc, None, dec_seq,
                                                 prev=even_state, seq=seq)
            (qkv_l,) = _proj_even_call(h, w_even_b, j, nc, ns, even_tabs, dec_seq)
            li = _lambda_init(i)
            o_c = _attn_even_call(qkv_c, None, j, diff_lambda, sublns, swa_sink, batch, seq, 0, li)
            o_l = _attn_even_call(qkv_l, even_caches, j, diff_lambda, sublns, swa_sink,
                                  dec_batch, dec_seq, past, li)
            w_out = w_oute_b
        else:
            q_c, *odd_state = _proj_odd_call(h, w_odd_b, q_gains, wq_b, kv_gains, j, 0, nc, None,
                                             dec_seq, prev=odd_state, seq=seq)
            q_l, ckv_l, kr_l = _proj_odd_call(h, w_odd_b, q_gains, wq_b, kv_gains, j, nc, ns,
                                              mla_tabs, dec_seq)
            o_c = _attn_odd_call(q_c, odd_state[0], odd_state[1], None, j, wk_b, wv_b, batch, seq, 0)
            o_l = _attn_odd_call(q_l, ckv_l, kr_l, (cache_mla_ckv, ckr), j, wk_b, wv_b,
                                 dec_batch, dec_seq, past)
            w_out = w_outo_b
        x, h2, aff = _outproj_call(xa, xb, xb_off, o_c, o_l, w_out, j, mods, i, ffn_gains,
                                   w_router_b, modrow)
        rowid, gate, tab = _select_call(aff, 2, ntok)
        xe = _dispatch_call(tab, h2, rowid, 2, ntok)
        y = _ffn_call(tab, xe, w_gate_exp, w_up_exp, w_down_exp, i, 2, ntok)
        final = i == depth - 1
        nxt = i if final else i + 1
        out_a, out_b = _combine_call(tab, y, rowid, gate, x, mods, i, nxt, mix_gains,
                                     depth if final else nxt, modrow_tb, 2, ntok, final)
        if final:
            y_prompt = out_a.reshape(batch, seq, D)
            y_sample = out_b.reshape(dec_batch, dec_seq, D)
        else:
            x, h = out_a, out_b
            xa, xb, xb_off = x, x, nc // TM

    kd, vd, ks, vs = even_state
    ckv_new, kr_new = odd_state
    return (y_prompt, y_sample,
            kd.reshape(batch, n_even, seq, DIFF_HEADS, 2 * HD),
            vd.reshape(batch, n_even, seq, DIFF_HEADS, 2 * HD),
            ks.reshape(batch, n_even, seq, 2, HD), vs.reshape(batch, n_even, seq, 2, HD),
            ckv_new, kr_new[..., 64:96])
```

```python
import functools
import math

import jax
import jax.numpy as jnp
import numpy as np
from jax import lax
from jax.experimental import pallas as pl
from jax.experimental.pallas import tpu as pltpu

F32 = jnp.float32
BF16 = jnp.bfloat16
I32 = jnp.int32

D = 1024
HD = 64
GRID_W = 64
WINDOW = 128
DIFF_HEADS = 4
SWA_HEADS = 8
MLA_HEADS = 16
MLA_Q_RANK = 384
MLA_KV_RANK = 256
MLA_ROPE = 32
N_EXPERTS = 16
EXPERT_FF = 512
EC_FACTOR = 2
ROPE_BASE = 10000.0
EPS = 1e-6
NEG_INF = -1e30
LOG2E = math.log2(math.e)
EVEN_IN = 2304
LANES = 128
TM = 512
TQ = 256
TQ_LATENT = 256
TB = 256
SEG = 16
VMEM_LIMIT = 56 * 1024 * 1024


def _cparams(sem, vmem=VMEM_LIMIT):
    return pltpu.CompilerParams(dimension_semantics=sem, vmem_limit_bytes=vmem)


def _dot(a, b):
    return jnp.dot(a, b, preferred_element_type=F32)


def _dot_nt(a, b):
    return lax.dot_general(a, b, (((1,), (1,)), ((), ())), preferred_element_type=F32)


def _silu(x):
    return x / (1.0 + jnp.exp(-x))


def _rms(x, g):
    ms = jnp.mean(x * x, axis=-1, keepdims=True)
    return x * lax.rsqrt(ms + EPS) * g


def _modulate(x, g, shift, scale):
    return _rms(x, g) * (1.0 + scale) + shift


def _lambda_init(layer):
    return 0.8 - 0.6 * math.exp(-0.3 * layer)


def _ada_kernel(c_ref, w_ref, b_ref, o_ref):
    s = _silu(c_ref[...]).astype(BF16)
    o_ref[...] = _dot(s, w_ref[...].astype(BF16)) + b_ref[...]


def _ada_call(cvec, w_ada, b_ada):
    depth, _, n6 = w_ada.shape
    rc = cvec.shape[0]
    tn = 512
    return pl.pallas_call(
        _ada_kernel,
        grid=(depth, n6 // tn),
        in_specs=[pl.BlockSpec((rc, D), lambda i, n: (0, 0)),
                  pl.BlockSpec((None, D, tn), lambda i, n: (i, 0, n)),
                  pl.BlockSpec((None, 1, tn), lambda i, n: (i, 0, n))],
        out_specs=pl.BlockSpec((None, rc, tn), lambda i, n: (i, 0, n)),
        out_shape=jax.ShapeDtypeStruct((depth, rc, n6), F32),
        compiler_params=_cparams(("parallel", "parallel")),
        name="ada",
    )(cvec, w_ada, b_ada.reshape(depth, 1, n6))


def _two_part_specs(rows_a, rows_b, width, off_b=0):
    na, nb_ = rows_a // TM, rows_b // TM
    return (pl.BlockSpec((TM, width), lambda r: (jnp.minimum(r, na - 1), 0)),
            pl.BlockSpec((TM, width), lambda r: (off_b + jnp.clip(r - na, 0, nb_ - 1), 0)))


def _norm_mod_kernel(xa_ref, xb_ref, mod_ref, g_ref, h_ref, *, ntile_a):
    def run(x_ref):
        h_ref[...] = _modulate(x_ref[...], g_ref[...], mod_ref[0], mod_ref[1]).astype(BF16)

    @pl.when(pl.program_id(0) < ntile_a)
    def _():
        run(xa_ref)

    @pl.when(pl.program_id(0) >= ntile_a)
    def _():
        run(xb_ref)


def _norm_mod_call(xa, xb, mods, layer, gains, modrow):
    na, nb_ = xa.shape[0], xb.shape[0]
    return pl.pallas_call(
        functools.partial(_norm_mod_kernel, ntile_a=na // TM),
        grid=((na + nb_) // TM,),
        in_specs=[*_two_part_specs(na, nb_, D),
                  pl.BlockSpec((None, None, 6, 1, D), lambda r: (layer, modrow(r), 0, 0, 0)),
                  pl.BlockSpec((None, 1, D), lambda r: (layer, 0, 0))],
        out_specs=pl.BlockSpec((TM, D), lambda r: (r, 0)),
        out_shape=jax.ShapeDtypeStruct((na + nb_, D), BF16),
        compiler_params=_cparams(("parallel",)),
        name="norm_mod",
    )(xa, xb, mods, gains)


def _rope_tables(dec_seq, half, lane_of_dim):
    pos = jnp.arange(dec_seq)
    row = (pos // GRID_W).astype(F32)
    col = (pos % GRID_W).astype(F32)
    inv = ROPE_BASE ** (-(jnp.arange(half, dtype=F32) / half))
    ang = jnp.stack([row[:, None] * inv[None, :], col[:, None] * inv[None, :]])
    info = [lane_of_dim(lane) for lane in range(LANES)]
    axis = np.array([0 if i is None else i[0] for i in info])
    freq = np.array([0 if i is None else i[1] for i in info])
    first = np.array([i is not None and not i[2] for i in info])[None, :]
    second = np.array([i is not None and i[2] for i in info])[None, :]
    lane_ang = ang[axis, :, freq].T
    cos, sin = jnp.cos(lane_ang), jnp.sin(lane_ang)
    return (jnp.where(first | second, cos, 1.0), jnp.where(first, -sin, 0.0),
            jnp.where(second, sin, 0.0))


def _even_lane(lane):
    j = lane % HD
    axis, jj = j // 32, j % 32
    return axis, jj % 16, jj >= 16


def _mla_lane(lane):
    if lane < 64 or lane >= 96:
        return None
    jj = lane - 64
    axis, k = jj // 16, jj % 16
    return axis, k % 8, k >= 8


def _rope(x, c, s1, s2, shift):
    return x * c + pltpu.roll(x, LANES - shift, 1) * s1 + pltpu.roll(x, shift, 1) * s2


_EVEN_ROPE_TILES = tuple(range(0, 8)) + tuple(range(12, 17))
_EVEN_Q_TILES = tuple(range(0, 4)) + tuple(range(12, 16))


def _append_layer(prev_refs, out_refs, new_values):
    for i, (out_ref, new) in enumerate(zip(out_refs, new_values)):
        nbatch, nlayers, seq, width = out_ref.shape
        if prev_refs:
            out_ref[:, 0:nlayers - 1] = prev_refs[i][...]
        out_ref[:, nlayers - 1] = new.reshape(nbatch, seq, width)


def _state_specs(prev, widths, seq):
    nlayers = 1 if prev is None else prev[0].shape[1] + 1
    per = TM // seq
    ins = [] if prev is None else [pl.BlockSpec((per, nlayers - 1, seq, w), lambda r: (r, 0, 0, 0))
                                   for w in widths]
    outs = [pl.BlockSpec((per, nlayers, seq, w), lambda r: (r, 0, 0, 0)) for w in widths]
    return ins, outs, nlayers


def _proj_even_kernel(*refs, rope, caches, nprev):
    h_ref, w_ref = refs[0], refs[1]
    pos = 2
    if rope:
        c_ref, s1_ref, s2_ref = refs[2:5]
        pos = 5
    prev_refs = refs[pos:pos + nprev]
    pos += nprev
    qkv_ref = refs[pos]
    res = _dot(h_ref[...], w_ref[...])
    scale = HD ** -0.5 * LOG2E
    for t in range(EVEN_IN // LANES):
        x = res[:, t * LANES:(t + 1) * LANES]
        if rope and t in _EVEN_ROPE_TILES:
            x = _rope(x, c_ref[...], s1_ref[...], s2_ref[...], 16)
        if t in _EVEN_Q_TILES:
            x = x * scale
        qkv_ref[:, t * LANES:(t + 1) * LANES] = x.astype(BF16)
    if caches:
        _append_layer(prev_refs, refs[pos + 1:pos + 5],
                      (res[:, 512:1024], res[:, 1024:1536], res[:, 2048:2176], res[:, 2176:2304]))


def _proj_even_call(h, w, j, row0, nrows, tables, dec_seq, prev=None, seq=None):
    rope = tables is not None
    caches = not rope
    t0 = row0 // TM
    in_specs = [pl.BlockSpec((TM, D), lambda r: (t0 + r, 0)),
                pl.BlockSpec((None, D, EVEN_IN), lambda r: (j, 0, 0))]
    args = [h, w]
    if rope:
        per = dec_seq // TM
        for _ in range(3):
            in_specs.append(pl.BlockSpec((TM, LANES), lambda r: (r % per, 0)))
        args += list(tables)
    out_specs = [pl.BlockSpec((TM, EVEN_IN), lambda r: (r, 0))]
    out_shape = [jax.ShapeDtypeStruct((nrows, EVEN_IN), BF16)]
    nprev = 0
    if caches:
        widths = (512, 512, LANES, LANES)
        ins, outs, nlayers = _state_specs(prev, widths, seq)
        nprev = len(ins)
        in_specs += ins
        args += [] if prev is None else list(prev)
        out_specs += outs
        out_shape += [jax.ShapeDtypeStruct((nrows // seq, nlayers, seq, wd), F32) for wd in widths]
    return pl.pallas_call(
        functools.partial(_proj_even_kernel, rope=rope, caches=caches, nprev=nprev),
        grid=(nrows // TM,),
        in_specs=in_specs, out_specs=out_specs, out_shape=out_shape,
        compiler_params=_cparams(("parallel",)),
        name="proj_even_lat" if rope else "proj_even_ctx",
    )(*args)


ODD_IN_PAD = MLA_Q_RANK + MLA_KV_RANK + LANES
MLA_QW = MLA_HEADS * LANES


def _proj_odd_kernel(*refs, rope, nprev):
    h_ref, w_ref, qn_ref, wq_ref, kvn_ref = refs[:5]
    pos = 5
    if rope:
        c_ref, s1_ref, s2_ref = refs[5:8]
        pos = 8
    prev_refs = refs[pos:pos + nprev]
    pos += nprev
    q_ref, ckv_ref, kr_ref = refs[pos:pos + 3]
    res = _dot(h_ref[...], w_ref[...])
    cq = _rms(res[:, :MLA_Q_RANK], qn_ref[...]).astype(BF16)
    ckv = _rms(res[:, MLA_Q_RANK:MLA_Q_RANK + MLA_KV_RANK], kvn_ref[...])
    kr = res[:, MLA_Q_RANK + MLA_KV_RANK:]
    if rope:
        kr = _rope(kr, c_ref[...], s1_ref[...], s2_ref[...], 8)
        ckv_ref[...] = ckv
        kr_ref[...] = kr
    else:
        _append_layer(prev_refs, (ckv_ref, kr_ref), (ckv, kr))
    q = _dot(cq, wq_ref[...])
    scale = (HD + MLA_ROPE) ** -0.5 * LOG2E
    for t in range(MLA_HEADS):
        x = q[:, t * LANES:(t + 1) * LANES]
        if rope:
            x = _rope(x, c_ref[...], s1_ref[...], s2_ref[...], 8)
        q_ref[:, t * LANES:(t + 1) * LANES] = (x * scale).astype(BF16)


def _proj_odd_call(h, w_in, qn, wq, kvn, j, row0, nrows, tables, dec_seq, prev=None, seq=None):
    rope = tables is not None
    t0 = row0 // TM
    in_specs = [pl.BlockSpec((TM, D), lambda r: (t0 + r, 0)),
                pl.BlockSpec((None, D, ODD_IN_PAD), lambda r: (j, 0, 0)),
                pl.BlockSpec((None, 1, MLA_Q_RANK), lambda r: (j, 0, 0)),
                pl.BlockSpec((None, MLA_Q_RANK, MLA_QW), lambda r: (j, 0, 0)),
                pl.BlockSpec((None, 1, MLA_KV_RANK), lambda r: (j, 0, 0))]
    args = [h, w_in, qn, wq, kvn]
    if rope:
        per = dec_seq // TM
        for _ in range(3):
            in_specs.append(pl.BlockSpec((TM, LANES), lambda r: (r % per, 0)))
        args += list(tables)
    out_specs = [pl.BlockSpec((TM, MLA_QW), lambda r: (r, 0))]
    out_shape = [jax.ShapeDtypeStruct((nrows, MLA_QW), BF16)]
    widths = (MLA_KV_RANK, LANES)
    nprev = 0
    if rope:
        out_specs += [pl.BlockSpec((TM, wd), lambda r: (r, 0)) for wd in widths]
        out_shape += [jax.ShapeDtypeStruct((nrows, wd), F32) for wd in widths]
    else:
        ins, outs, nlayers = _state_specs(prev, widths, seq)
        nprev = len(ins)
        in_specs += ins
        args += [] if prev is None else list(prev)
        out_specs += outs
        out_shape += [jax.ShapeDtypeStruct((nrows // seq, nlayers, seq, wd), F32) for wd in widths]
    return pl.pallas_call(
        functools.partial(_proj_odd_kernel, rope=rope, nprev=nprev),
        grid=(nrows // TM,),
        in_specs=in_specs,
        out_specs=out_specs,
        out_shape=out_shape,
        compiler_params=_cparams(("parallel",)),
        name="proj_odd_lat" if rope else "proj_odd_ctx",
    )(*args)


def _query_tile(past):
    return TQ_LATENT if past > 0 else TQ


def _ones_lane(half):
    return HD if half == 0 else 0


def _row_sum(e, o, half, from_matmul):
    if from_matmul:
        one = _ones_lane(half)
        return o[:, one:one + 1]
    return jnp.sum(e, axis=-1, keepdims=True)


def _half_values(v, half):
    lane = lax.broadcasted_iota(I32, (1, LANES), 1)
    keep = (lane < HD) if half == 0 else (lane >= HD)
    return jnp.where(keep, v, jnp.where(lane == _ones_lane(half), 1.0, 0.0)).astype(BF16)


VDT_ROWS = 2 * HD + SEG


def _attn_even_kernel(*refs, seq, past, lam_init, layer):
    latent = past > 0
    tq = _query_tile(past)
    n = past + seq
    if latent:
        (qkv_ref, ck_ref, cv_ref, sk_ref, sv_ref, lam_ref, subln_ref, sink_ref,
         o_ref, kd, vd, ka, vl, vh) = refs
    else:
        qkv_ref, lam_ref, subln_ref, sink_ref, o_ref, kd, vd, ka, vl, vh = refs
    qi = pl.program_id(1)
    lo = lax.broadcasted_iota(I32, (1, LANES), 1) < HD

    @pl.when(qi == 0)
    def _build():
        chunk = 256
        for c0 in range(0, n, chunk):
            rows = slice(c0, c0 + chunk)
            if c0 < past:
                prow = slice(c0, c0 + chunk)
                for h in range(DIFF_HEADS):
                    kd[rows, h * LANES:(h + 1) * LANES] = ck_ref[prow, h, :].astype(BF16)
                    vd[h * VDT_ROWS:h * VDT_ROWS + LANES, rows] = cv_ref[prow, h, :].T.astype(BF16)
                kt = jnp.concatenate([sk_ref[prow, 0, :], sk_ref[prow, 1, :]], axis=1)
                vt = jnp.concatenate([sv_ref[prow, 0, :], sv_ref[prow, 1, :]], axis=1)
            else:
                orow = slice(c0 - past, c0 - past + chunk)
                kd[rows, :] = qkv_ref[orow, 512:1024]
                for h in range(DIFF_HEADS):
                    vown = qkv_ref[orow, 1024 + h * LANES:1024 + (h + 1) * LANES].astype(F32)
                    vd[h * VDT_ROWS:h * VDT_ROWS + LANES, rows] = vown.T.astype(BF16)
                kt = qkv_ref[orow, 2048:2176].astype(F32)
                vt = qkv_ref[orow, 2176:2304].astype(F32)
            kr = pltpu.roll(kt, HD, 1)
            vr = pltpu.roll(vt, HD, 1)
            ka[0, rows, :] = jnp.where(lo, kt, kr).astype(BF16)
            ka[1, rows, :] = jnp.where(lo, kr, kt).astype(BF16)
            vl[0, rows, :] = _half_values(vt, 0)
            vh[0, rows, :] = _half_values(vr, 1)
            vl[1, rows, :] = _half_values(vr, 0)
            vh[1, rows, :] = _half_values(vt, 1)
        for h in range(DIFF_HEADS):
            vd[h * VDT_ROWS + LANES:(h + 1) * VDT_ROWS, :] = jnp.ones((SEG, n), BF16)

    r0 = pl.multiple_of(qi * tq, tq)
    lam = lam_ref[...]
    lam_full = (jnp.exp(jnp.sum(lam[0:1] * lam[1:2], axis=-1, keepdims=True))
                - jnp.exp(jnp.sum(lam[2:3] * lam[3:4], axis=-1, keepdims=True)) + lam_init)
    zero_b = jnp.zeros((), BF16)

    for h in range(DIFF_HEADS):
        cs = slice(h * LANES, (h + 1) * LANES)
        qt = qkv_ref[pl.ds(r0, tq), cs]
        kh = kd[:, cs]
        ots = []
        scores = [_dot_nt(jnp.where(lo, qt, zero_b) if comp == 0 else jnp.where(lo, zero_b, qt), kh)
                  for comp in range(2)]
        for comp in range(2):
            s = scores[comp]
            m = jnp.max(s, axis=-1, keepdims=True)
            e = jnp.exp2(s - m).astype(BF16)
            ots.append(_dot_nt(vd[h * VDT_ROWS:(h + 1) * VDT_ROWS, :], e))
        ot = (ots[0][0:LANES, :] * (1.0 / ots[0][LANES:LANES + 1, :])
              - ots[1][0:LANES, :] * (lam_full / ots[1][LANES:LANES + 1, :]))
        o = _rms(ot.T, subln_ref[...]) * (1.0 - lam_init)
        o_ref[:, cs] = o.astype(BF16)

    nblk = seq // WINDOW
    dense = past if latent else seq
    if latent:
        per = tq // WINDOW
        offsets = tuple(range(-1, per + 1))
        rr = lax.broadcasted_iota(I32, (tq, WINDOW), 0)
        cc = lax.broadcasted_iota(I32, (tq, WINDOW), 1)
        band, starts = {}, {}
        for d in offsets:
            blk = qi * per + d
            inside = jnp.logical_and(blk >= 0, blk < nblk)
            band[d] = jnp.logical_and(jnp.abs(rr - cc - d * WINDOW) <= WINDOW, inside)
            starts[d] = pl.multiple_of(past + jnp.clip(blk, 0, nblk - 1) * WINDOW, WINDOW)
    for i in range(SWA_HEADS // 2):
        hk = i // 2
        cs = slice(1536 + i * LANES, 1536 + (i + 1) * LANES)
        qt = qkv_ref[pl.ds(r0, tq), cs]
        halves = []
        for half in range(2):
            qc = jnp.where(lo, qt, zero_b) if half == 0 else jnp.where(lo, zero_b, qt)
            vsel = vl if half == 0 else vh
            sink = sink_ref[layer, 2 * i + half] * LOG2E
            parts = [_dot_nt(qc, ka[hk, 0:dense, :])]
            if latent:
                for d in offsets:
                    s = _dot_nt(qc, ka[hk, pl.ds(starts[d], WINDOW), :])
                    parts.append(jnp.where(band[d], s, NEG_INF))
            s_all = jnp.concatenate(parts, axis=1) if len(parts) > 1 else parts[0]
            m = jnp.maximum(jnp.max(s_all, axis=-1, keepdims=True), sink)
            e = jnp.exp2(s_all - m)
            eb = e.astype(BF16)
            o = _dot(eb[:, 0:dense], vsel[hk, 0:dense, :])
            if latent:
                for k, d in enumerate(offsets):
                    o += _dot(eb[:, dense + k * WINDOW:dense + (k + 1) * WINDOW],
                              vsel[hk, pl.ds(starts[d], WINDOW), :])
            den = _row_sum(e, o, half, from_matmul=latent) + jnp.exp2(sink - m)
            halves.append(o * (1.0 / den))
        o_ref[:, 512 + i * LANES:512 + (i + 1) * LANES] = jnp.where(lo, halves[0], halves[1]).astype(BF16)


def _attn_even_call(qkv, caches, j, lam, subln, sink, nbatch, seq, past, lam_init):
    n = past + seq
    latent = past > 0
    tq = _query_tile(past)
    in_specs = [pl.BlockSpec((seq, EVEN_IN), lambda b, q: (b, 0))]
    args = [qkv]
    if latent:
        ck, cv, sk, sv = caches
        in_specs += [pl.BlockSpec((None, None, past, DIFF_HEADS, 2 * HD), lambda b, q: (b, j, 0, 0, 0)),
                     pl.BlockSpec((None, None, past, DIFF_HEADS, 2 * HD), lambda b, q: (b, j, 0, 0, 0)),
                     pl.BlockSpec((None, None, past, 2, HD), lambda b, q: (b, j, 0, 0, 0)),
                     pl.BlockSpec((None, None, past, 2, HD), lambda b, q: (b, j, 0, 0, 0))]
        args += [ck, cv, sk, sv]
    in_specs += [pl.BlockSpec((None, 4, HD), lambda b, q: (j, 0, 0)),
                 pl.BlockSpec((None, 1, 2 * HD), lambda b, q: (j, 0, 0)),
                 pl.BlockSpec(memory_space=pltpu.SMEM)]
    args += [lam, subln, sink]
    return pl.pallas_call(
        functools.partial(_attn_even_kernel, seq=seq, past=past, lam_init=lam_init, layer=j),
        grid=(nbatch, seq // tq),
        in_specs=in_specs,
        out_specs=pl.BlockSpec((tq, D), lambda b, q: (b * (seq // tq) + q, 0)),
        out_shape=jax.ShapeDtypeStruct((nbatch * seq, D), BF16),
        scratch_shapes=[pltpu.VMEM((n, 512), BF16),
                        pltpu.VMEM((DIFF_HEADS * VDT_ROWS, n), BF16),
                        pltpu.VMEM((2, n, LANES), BF16), pltpu.VMEM((2, n, LANES), BF16),
                        pltpu.VMEM((2, n, LANES), BF16)],
        compiler_params=_cparams(("arbitrary", "arbitrary")),
        name="attn_even_lat" if latent else "attn_even_ctx",
    )(*args)


VT_ROWS = HD + SEG


def _attn_odd_kernel(*refs, seq, past):
    latent = past > 0
    tq = _query_tile(past)
    n = past + seq
    if latent:
        q_ref, ckv_ref, kr_ref, cckv_ref, ckr_ref, wk_ref, wv_ref, o_ref, kf, vt = refs
    else:
        q_ref, ckv_ref, kr_ref, wk_ref, wv_ref, o_ref, kf, vlo, vhi = refs
    qi = pl.program_id(1)

    @pl.when(qi == 0)
    def _build():
        chunk = 256
        for c0 in range(0, n, chunk):
            rows = slice(c0, c0 + chunk)
            if c0 < past:
                ckv = cckv_ref[c0:c0 + chunk, :].astype(BF16)
                kr = ckr_ref[c0:c0 + chunk, :]
            else:
                ckv = ckv_ref[c0 - past:c0 - past + chunk, :].astype(BF16)
                kr = kr_ref[c0 - past:c0 - past + chunk, :]
            kk = _dot(ckv, wk_ref[...])
            for h in range(MLA_HEADS):
                cs = slice(h * LANES, (h + 1) * LANES)
                kf[rows, cs] = (kk[:, cs] + kr).astype(BF16)
            vv = _dot(ckv, wv_ref[...])
            for i in range(MLA_HEADS // 2):
                cs = slice(i * LANES, (i + 1) * LANES)
                if latent:
                    pair = vv[:, cs].T.astype(BF16)
                    for half in range(2):
                        r = (2 * i + half) * VT_ROWS
                        vt[r:r + HD, rows] = pair[half * HD:(half + 1) * HD, :]
                else:
                    vlo[rows, cs] = _half_values(vv[:, cs], 0)
                    vhi[rows, cs] = _half_values(vv[:, cs], 1)
        if latent:
            for h in range(MLA_HEADS):
                vt[h * VT_ROWS + HD:(h + 1) * VT_ROWS, :] = jnp.ones((SEG, n), BF16)

    r0 = pl.multiple_of(qi * tq, tq)
    lo = lax.broadcasted_iota(I32, (1, LANES), 1) < HD
    for i in range(MLA_HEADS // 2):
        halves = []
        scores = [_dot_nt(q_ref[pl.ds(r0, tq), (2 * i + half) * LANES:(2 * i + half + 1) * LANES],
                          kf[:, (2 * i + half) * LANES:(2 * i + half + 1) * LANES]) for half in range(2)]
        for half in range(2):
            h = 2 * i + half
            s = scores[half]
            m = jnp.max(s, axis=-1, keepdims=True)
            e = jnp.exp2(s - m)
            if latent:
                ot = _dot_nt(vt[h * VT_ROWS:(h + 1) * VT_ROWS, :], e.astype(BF16))
                halves.append(ot[0:HD, :] * (1.0 / ot[HD:HD + 1, :]))
            else:
                vsel = vlo if half == 0 else vhi
                o = _dot(e.astype(BF16), vsel[:, i * LANES:(i + 1) * LANES])
                halves.append(o * (1.0 / jnp.sum(e, axis=-1, keepdims=True)))
        if latent:
            tile = jnp.concatenate(halves, axis=0).T
        else:
            tile = jnp.where(lo, halves[0], halves[1])
        o_ref[:, i * LANES:(i + 1) * LANES] = tile.astype(BF16)


def _attn_odd_call(q, ckv, kr, caches, j, wk, wv, nbatch, seq, past):
    n = past + seq
    latent = past > 0
    tq = _query_tile(past)
    in_specs = [pl.BlockSpec((seq, MLA_QW), lambda b, qq: (b, 0))]
    if latent:
        in_specs += [pl.BlockSpec((seq, MLA_KV_RANK), lambda b, qq: (b, 0)),
                     pl.BlockSpec((seq, LANES), lambda b, qq: (b, 0))]
    else:
        last = ckv.shape[1] - 1
        in_specs += [pl.BlockSpec((None, None, seq, MLA_KV_RANK), lambda b, qq: (b, last, 0, 0)),
                     pl.BlockSpec((None, None, seq, LANES), lambda b, qq: (b, last, 0, 0))]
    args = [q, ckv, kr]
    if latent:
        in_specs += [pl.BlockSpec((None, None, past, MLA_KV_RANK), lambda b, qq: (b, j, 0, 0)),
                     pl.BlockSpec((None, None, past, LANES), lambda b, qq: (b, j, 0, 0))]
        args += list(caches)
    in_specs += [pl.BlockSpec((None, MLA_KV_RANK, MLA_QW), lambda b, qq: (j, 0, 0)),
                 pl.BlockSpec((None, MLA_KV_RANK, D), lambda b, qq: (j, 0, 0))]
    args += [wk, wv]
    return pl.pallas_call(
        functools.partial(_attn_odd_kernel, seq=seq, past=past),
        grid=(nbatch, seq // tq),
        in_specs=in_specs,
        out_specs=pl.BlockSpec((tq, D), lambda b, qq: (b * (seq // tq) + qq, 0)),
        out_shape=jax.ShapeDtypeStruct((nbatch * seq, D), BF16),
        scratch_shapes=([pltpu.VMEM((n, MLA_QW), BF16), pltpu.VMEM((MLA_HEADS * VT_ROWS, n), BF16)]
                        if latent else
                        [pltpu.VMEM((n, MLA_QW), BF16), pltpu.VMEM((n, D), BF16),
                         pltpu.VMEM((n, D), BF16)]),
        compiler_params=_cparams(("arbitrary", "arbitrary")),
        name="attn_odd_lat" if latent else "attn_odd_ctx",
    )(*args)


def _outproj_kernel(xa_ref, xb_ref, oa_ref, ob_ref, w_ref, mod_ref, g_ref, wr_ref,
                    xo_ref, h_ref, aff_ref, *, ntile_a):
    def run(x_ref, o_ref):
        x = x_ref[...] + mod_ref[2] * _dot(o_ref[...], w_ref[...])
        xo_ref[...] = x
        h = _modulate(x, g_ref[...], mod_ref[3], mod_ref[4]).astype(BF16)
        h_ref[...] = h
        logits = _dot(h, wr_ref[...])
        lane = lax.broadcasted_iota(I32, (TM, LANES), 1)
        lg = jnp.where(lane < N_EXPERTS, logits, -jnp.inf)
        e = jnp.exp(lg - jnp.max(lg, axis=-1, keepdims=True))
        aff = e / jnp.sum(e, axis=-1, keepdims=True)
        for c in range(TM // TB):
            aff_ref[c] = aff[c * TB:(c + 1) * TB, :].T[0:N_EXPERTS, :]

    @pl.when(pl.program_id(0) < ntile_a)
    def _():
        run(xa_ref, oa_ref)

    @pl.when(pl.program_id(0) >= ntile_a)
    def _():
        run(xb_ref, ob_ref)


def _outproj_call(xa, xb, xb_off, o_a, o_b, w, widx, mods, layer, gains, w_router, modrow):
    na, nb_ = o_a.shape[0], o_b.shape[0]
    t = na + nb_
    assert TM % TB == 0
    return pl.pallas_call(
        functools.partial(_outproj_kernel, ntile_a=na // TM),
        grid=(t // TM,),
        in_specs=[*_two_part_specs(na, nb_, D, xb_off), *_two_part_specs(na, nb_, D),
                  pl.BlockSpec((None, D, D), lambda r: (widx, 0, 0)),
                  pl.BlockSpec((None, None, 6, 1, D), lambda r: (layer, modrow(r), 0, 0, 0)),
                  pl.BlockSpec((None, 1, D), lambda r: (layer, 0, 0)),
                  pl.BlockSpec((None, D, LANES), lambda r: (layer, 0, 0))],
        out_specs=[pl.BlockSpec((TM, D), lambda r: (r, 0)),
                   pl.BlockSpec((TM, D), lambda r: (r, 0)),
                   pl.BlockSpec((TM // TB, N_EXPERTS, TB), lambda r: (r, 0, 0))],
        out_shape=[jax.ShapeDtypeStruct((t, D), F32), jax.ShapeDtypeStruct((t, D), BF16),
                   jax.ShapeDtypeStruct((t // TB, N_EXPERTS, TB), F32)],
        compiler_params=_cparams(("parallel",)),
        name="outproj",
    )(xa, xb, o_a, o_b, w, mods, gains, w_router)


def _select_kernel(aff_ref, rowid_ref, gate_ref, tab_ref, *, nb, cap):
    ne = N_EXPERTS
    nr = nb * ne
    a = aff_ref[...].reshape(nr, TB)
    ri = lax.broadcasted_iota(I32, (nr, nr), 0)
    ci = lax.broadcasted_iota(I32, (nr, nr), 1)
    same_e = (ri & (ne - 1)) == (ci & (ne - 1))
    same_b = (ri >> 4) == (ci >> 4)
    m_e = jnp.where(same_e, 1.0, 0.0).astype(BF16)
    m_b = jnp.where(same_b, 1.0, 0.0).astype(BF16)
    m_a = jnp.where(jnp.logical_and(same_e, ci < ri), 1.0, 0.0).astype(BF16)
    m_o = jnp.where(jnp.logical_and(same_b, ci < ri), 1.0, 0.0).astype(BF16)
    ui = lax.broadcasted_iota(I32, (TB, TB), 0)
    uj = lax.broadcasted_iota(I32, (TB, TB), 1)
    upper = jnp.where(ui < uj, 1.0, 0.0).astype(BF16)

    def rows_to_lanes(col):
        return jnp.broadcast_to(col, (nr, LANES)).astype(BF16)

    wide = jnp.concatenate([aff_ref[b] for b in range(nb)], axis=1)

    def count_ge(value):
        return jnp.sum(jnp.where(wide >= value, 1.0, 0.0), axis=-1, keepdims=True)

    def bisect(i, v):
        cand = v | jnp.left_shift(jnp.int32(1), 30 - i)
        return jnp.where(count_ge(pltpu.bitcast(cand, F32)) >= cap, cand, v)

    thr = lax.fori_loop(0, 31, bisect, jnp.zeros((ne, 1), I32))

    def refine(i, lohi):
        lo_e, hi_e = lohi
        mid = 0.5 * (lo_e + hi_e)
        take = count_ge(mid) >= cap
        return jnp.where(take, mid, lo_e), jnp.where(take, hi_e, mid)

    lo_e, hi_e = lax.fori_loop(0, TIE_STEPS, refine,
                               (pltpu.bitcast(thr, F32), pltpu.bitcast(thr + 1, F32)))
    lo_v = jnp.concatenate([lo_e] * nb, axis=0)
    hi_v = jnp.concatenate([hi_e] * nb, axis=0)
    gt = jnp.where(a >= hi_v, 1.0, 0.0)
    eq = jnp.where(jnp.logical_and(a >= lo_v, a < hi_v), 1.0, 0.0)
    n_gt = _dot(m_e, rows_to_lanes(jnp.sum(gt, axis=-1, keepdims=True)))[:, 0:1]
    need = cap - n_gt
    eq_before = (_dot(m_a, rows_to_lanes(jnp.sum(eq, axis=-1, keepdims=True)))[:, 0:1]
                 + _dot(eq.astype(BF16), upper))
    sel = jnp.where(jnp.logical_and(eq > 0.0, eq_before < need), 1.0, gt)
    local = _dot(sel.astype(BF16), upper)
    cnt = jnp.sum(sel, axis=-1, keepdims=True)
    seg = jnp.floor((cnt + (SEG - 1)) * (1.0 / SEG)) * SEG
    segb = rows_to_lanes(seg)
    over = jnp.maximum(seg - WINR, 0.0)
    overb = rows_to_lanes(over)
    off_over = FIRST_ROWS + _dot(m_o, overb)[:, 0:1]
    off_buf = _dot(m_a, segb)[:, 0:1]
    over_blk = _dot(m_b, overb)[:, 0:1]
    rows_exp = _dot(m_e, segb)[:, 0:1]
    tiles_exp = jnp.floor((rows_exp + (TF - 1)) / TF)
    expert = (lax.broadcasted_iota(I32, (nr, 1), 0) & (ne - 1)).astype(F32)
    row = jnp.where(local < WINR, expert * WINR + local, off_over + local - WINR)
    rowid_ref[...] = jnp.where(sel > 0.0, row, -1.0).astype(I32).reshape(nb, ne, TB)
    gate_ref[...] = jnp.where(sel > 0.0, a, 0.0).reshape(nb, ne, TB)
    tl = lax.broadcasted_iota(I32, (nr, LANES), 1)
    tab = jnp.where(tl == 0, seg, jnp.where(tl == 1, off_over, jnp.where(
        tl == 2, off_buf, jnp.where(tl == 3, over_blk, jnp.where(tl == 4, rows_exp, tiles_exp)))))
    tab_ref[...] = tab.T[0:8, :].astype(I32)


def _select_call(aff, ngroups, ntok):
    nb = ntok // TB
    cap = EC_FACTOR * ntok // N_EXPERTS
    nr = nb * N_EXPERTS
    return pl.pallas_call(
        functools.partial(_select_kernel, nb=nb, cap=cap),
        grid=(ngroups,),
        in_specs=[pl.BlockSpec((nb, N_EXPERTS, TB), lambda g: (g, 0, 0))],
        out_specs=[pl.BlockSpec((None, nb, N_EXPERTS, TB), lambda g: (g, 0, 0, 0)),
                   pl.BlockSpec((None, nb, N_EXPERTS, TB), lambda g: (g, 0, 0, 0)),
                   pl.BlockSpec((None, 8, nr), lambda g: (g, 0, 0))],
        out_shape=[jax.ShapeDtypeStruct((ngroups, nb, N_EXPERTS, TB), I32),
                   jax.ShapeDtypeStruct((ngroups, nb, N_EXPERTS, TB), F32),
                   jax.ShapeDtypeStruct((ngroups, 8, nr), I32)],
        compiler_params=_cparams(("arbitrary",)),
        name="select",
    )(aff)


STACK_ROWS = N_EXPERTS * TB


FIRST_ROWS = 3 * TB


WINR = 3 * SEG
assert N_EXPERTS * WINR == FIRST_ROWS
TF = 672
TIE_STEPS = 12


def _window_rows(rowid_ref, dst, value_ref=None):
    for e in range(N_EXPERTS):
        rid_e = rowid_ref[e:e + 1, :]
        val_e = 1.0 if value_ref is None else value_ref[e:e + 1, :]
        hit = rid_e == lax.broadcasted_iota(I32, (WINR, TB), 0) + e * WINR
        dst[e * WINR:(e + 1) * WINR, :] = jnp.where(hit, val_e, 0.0).astype(dst.dtype)


class _Table:
    def __init__(self, tab_s, row, nb, per_block=False):
        self.tab_s, self.row, self.per = tab_s, row, nb if per_block else nb * N_EXPERTS
        self.scale = N_EXPERTS if per_block else 1

    def __getitem__(self, k):
        if self.per & (self.per - 1) == 0:
            hi, low = lax.shift_right_logical(k, self.per.bit_length() - 1), k & (self.per - 1)
        else:
            hi, low = lax.div(k, self.per), lax.rem(k, self.per)
        return self.tab_s[hi, self.row, low * self.scale]


def _tables(tab_s, nb):
    return (_Table(tab_s, 0, nb), _Table(tab_s, 1, nb), _Table(tab_s, 2, nb),
            _Table(tab_s, 3, nb, per_block=True))


def _overflow_groups(seg_s, k):
    return lax.div(jnp.maximum(seg_s[k] - WINR, 0), SEG)


def _overflow_rows(seg_s, offo_s, step, over, rowid_ref, dst, value_ref=None):
    def zero(i, carry):
        r0 = pl.multiple_of(FIRST_ROWS + i * SEG, SEG)
        dst[pl.ds(r0, SEG), :] = jnp.zeros((SEG, TB), dst.dtype)
        return carry

    lax.fori_loop(0, ((over + TB - 1) // TB) * (TB // SEG), zero, 0)
    for e in range(N_EXPERTS):
        k = step * N_EXPERTS + e
        rid_e = rowid_ref[e:e + 1, :]
        val_e = 1.0 if value_ref is None else value_ref[e:e + 1, :]

        def group(i, carry, off=offo_s[k], rid_e=rid_e, val_e=val_e):
            r0 = pl.multiple_of(off + i * SEG, SEG)
            hit = rid_e == lax.broadcasted_iota(I32, (SEG, TB), 0) + r0
            dst[pl.ds(r0, SEG), :] = jnp.where(hit, val_e, 0.0).astype(dst.dtype)
            return carry

        lax.fori_loop(0, _overflow_groups(seg_s, k), group, 0)


def _wait_rows(rows, make_copy):
    def big(i, carry):
        make_copy(TB).wait()
        return carry

    def small(i, carry):
        make_copy(SEG).wait()
        return carry

    lax.fori_loop(0, lax.div(rows, TB), big, 0)
    lax.fori_loop(0, lax.div(lax.rem(rows, TB), SEG), small, 0)


def _dispatch_kernel(tab_s, h_ref, rowid_ref, xe_hbm,
                     onehot, stack, zbuf, sem, zsem, *, nb, nsteps, cap):
    g = pl.program_id(0)
    b = pl.program_id(1)
    step = g * nb + b
    slot = lax.rem(step, 2)
    seg_s, offo_s, offb_s, over_s = _tables(tab_s, nb)
    over = over_s[step]
    xrows = xe_hbm.shape[2]

    def wait_slot(nrows, sl):
        _wait_rows(nrows, lambda n: pltpu.make_async_copy(
            stack.at[sl, pl.ds(0, n)], xe_hbm.at[0, 0, pl.ds(0, n)], sem.at[sl]))

    @pl.when(step == 0)
    def _init():
        stack[...] = jnp.zeros_like(stack)

    @pl.when(b == 0)
    def _zero_unused():
        zbuf[...] = jnp.zeros_like(zbuf)
        for e in range(N_EXPERTS):
            pltpu.make_async_copy(zbuf, xe_hbm.at[g, e, pl.ds(cap, xrows - cap)], zsem).start()

    _window_rows(rowid_ref, onehot)
    h = h_ref[...]
    stack[slot, 0:FIRST_ROWS, :] = _dot(onehot[0:FIRST_ROWS, :], h).astype(BF16)

    @pl.when(over > 0)
    def _overflow():
        _overflow_rows(seg_s, offo_s, step, over, rowid_ref, onehot)

        def chunk(c, carry):
            base = pl.multiple_of(FIRST_ROWS + c * TB, TB)
            stack[slot, pl.ds(base, TB), :] = _dot(onehot[pl.ds(base, TB), :], h).astype(BF16)
            return carry

        lax.fori_loop(0, (over + TB - 1) // TB, chunk, 0)

    @pl.when(b == 0)
    def _zero_unused_done():
        for e in range(N_EXPERTS):
            pltpu.make_async_copy(zbuf, xe_hbm.at[g, e, pl.ds(cap, xrows - cap)], zsem).wait()

    @pl.when(step >= 1)
    def _previous_landed():
        wait_slot(FIRST_ROWS + over_s[step - 1], 1 - slot)

    for e in range(N_EXPERTS):
        k = step * N_EXPERTS + e
        pltpu.make_async_copy(
            stack.at[slot, e * WINR:(e + 1) * WINR],
            xe_hbm.at[g, e, pl.ds(pl.multiple_of(offb_s[k], SEG), WINR)], sem.at[slot]).start()

    @pl.when(over > 0)
    def _overflow_copies():
        for e in range(N_EXPERTS):
            k = step * N_EXPERTS + e

            def one(i, carry, e=e, k=k):
                pltpu.make_async_copy(
                    stack.at[slot, pl.ds(pl.multiple_of(offo_s[k] + i * SEG, SEG), SEG)],
                    xe_hbm.at[g, e, pl.ds(pl.multiple_of(offb_s[k] + WINR + i * SEG, SEG), SEG)],
                    sem.at[slot]).start()
                return carry

            lax.fori_loop(0, _overflow_groups(seg_s, k), one, 0)

    @pl.when(step == nsteps - 1)
    def _drain():
        wait_slot(FIRST_ROWS + over, slot)


def _expert_rows(ntok):
    cap = EC_FACTOR * ntok // N_EXPERTS
    worst = cap + (ntok // TB) * (SEG - 1)
    tiles = -(-(worst + WINR) // TF)
    assert cap % SEG == 0 and cap >= WINR and tiles * TF > cap
    return cap, tiles


def _dispatch_call(tab, h, rowid, ngroups, ntok):
    nb = ntok // TB
    cap, tiles = _expert_rows(ntok)
    xrows = tiles * TF
    grid_spec = pltpu.PrefetchScalarGridSpec(
        num_scalar_prefetch=1,
        grid=(ngroups, nb),
        in_specs=[pl.BlockSpec((TB, D), lambda g, b, *_: (g * nb + b, 0)),
                  pl.BlockSpec((None, None, N_EXPERTS, TB), lambda g, b, *_: (g, b, 0, 0))],
        out_specs=pl.BlockSpec(memory_space=pl.ANY),
        scratch_shapes=[pltpu.VMEM((STACK_ROWS, TB), BF16), pltpu.VMEM((2, STACK_ROWS, D), BF16),
                        pltpu.VMEM((xrows - cap, D), BF16),
                        pltpu.SemaphoreType.DMA((2,)), pltpu.SemaphoreType.DMA])
    return pl.pallas_call(
        functools.partial(_dispatch_kernel, nb=nb, nsteps=ngroups * nb, cap=cap),
        grid_spec=grid_spec,
        out_shape=jax.ShapeDtypeStruct((ngroups, N_EXPERTS, xrows, D), BF16),
        compiler_params=_cparams(("arbitrary", "arbitrary")),
        name="dispatch",
    )(tab, h, rowid)


def _ffn_kernel(tab_s, xe_ref, wg_hbm, wu_hbm, wd_hbm, y_ref, wg32, wu32, wd32, wgb, wub, wdb, sem,
                *, layer):
    e = pl.program_id(0)
    g = pl.program_id(1)
    j = pl.program_id(2)

    def weight_copies(ee, sl):
        return (pltpu.make_async_copy(wg_hbm.at[layer, ee], wg32.at[sl], sem.at[sl]),
                pltpu.make_async_copy(wu_hbm.at[layer, ee], wu32.at[sl], sem.at[sl]),
                pltpu.make_async_copy(wd_hbm.at[layer, ee], wd32.at[sl], sem.at[sl]))

    @pl.when(jnp.logical_and(g == 0, j == 0))
    def _weights():
        sl = lax.rem(e, 2)

        @pl.when(e == 0)
        def _():
            for cp in weight_copies(e, sl):
                cp.start()

        @pl.when(e + 1 < N_EXPERTS)
        def _():
            for cp in weight_copies(e + 1, 1 - sl):
                cp.start()

        for cp in weight_copies(e, sl):
            cp.wait()
        wgb[...] = wg32[sl].astype(BF16)
        wub[...] = wu32[sl].astype(BF16)
        wdb[...] = wd32[sl].astype(BF16)

    live = j < tab_s[g, 5, e]

    @pl.when(live)
    def _run():
        x = xe_ref[...]
        hid = (_silu(_dot(x, wgb[...])) * _dot(x, wub[...])).astype(BF16)
        y_ref[...] = _dot(hid, wdb[...]).astype(BF16)

    @pl.when(jnp.logical_not(live))
    def _skip():
        y_ref[...] = jnp.zeros_like(y_ref)


def _ffn_call(tab, xe, wg, wu, wd, layer, ngroups, ntok):
    _, tiles = _expert_rows(ntok)

    def xmap(e, g, j, tab_s):
        return (g, e, jnp.minimum(j, tab_s[g, 5, e] - 1), 0)

    grid_spec = pltpu.PrefetchScalarGridSpec(
        num_scalar_prefetch=1,
        grid=(N_EXPERTS, ngroups, tiles),
        in_specs=[pl.BlockSpec((None, None, TF, D), xmap),
                  pl.BlockSpec(memory_space=pl.ANY), pl.BlockSpec(memory_space=pl.ANY),
                  pl.BlockSpec(memory_space=pl.ANY)],
        out_specs=pl.BlockSpec((None, None, TF, D), lambda e, g, j, nt: (g, e, j, 0)),
        scratch_shapes=[pltpu.VMEM((2, D, EXPERT_FF), F32), pltpu.VMEM((2, D, EXPERT_FF), F32),
                        pltpu.VMEM((2, EXPERT_FF, D), F32),
                        pltpu.VMEM((D, EXPERT_FF), BF16), pltpu.VMEM((D, EXPERT_FF), BF16),
                        pltpu.VMEM((EXPERT_FF, D), BF16), pltpu.SemaphoreType.DMA((2,))])
    return pl.pallas_call(
        functools.partial(_ffn_kernel, layer=layer),
        grid_spec=grid_spec,
        out_shape=jax.ShapeDtypeStruct((ngroups, N_EXPERTS, tiles * TF, D), BF16),
        compiler_params=_cparams(("arbitrary", "arbitrary", "arbitrary")),
        name="ffn",
    )(tab, xe, wg, wu, wd)


def _combine_kernel(tab_s, y_hbm, rowid_ref, gate_ref, x_ref, mod_ref,
                    modn_ref, g_ref, xo_ref, h_ref, weights, stack, acc, sem, *, nb, nsteps, final):
    g = pl.program_id(0)
    b = pl.program_id(1)
    step = g * nb + b
    slot = lax.rem(step, 2)
    seg_s, offo_s, offb_s, over_s = _tables(tab_s, nb)
    over = over_s[step]

    def fetch(st, sl):
        gg = lax.div(st, nb)
        for e in range(N_EXPERTS):
            k = st * N_EXPERTS + e
            pltpu.make_async_copy(
                y_hbm.at[gg, e, pl.ds(pl.multiple_of(offb_s[k], SEG), WINR)],
                stack.at[sl, e * WINR:(e + 1) * WINR], sem.at[sl]).start()

        @pl.when(over_s[st] > 0)
        def _():
            for e in range(N_EXPERTS):
                k = st * N_EXPERTS + e

                def one(i, carry, e=e, k=k):
                    pltpu.make_async_copy(
                        y_hbm.at[gg, e, pl.ds(pl.multiple_of(offb_s[k] + WINR + i * SEG, SEG), SEG)],
                        stack.at[sl, pl.ds(pl.multiple_of(offo_s[k] + i * SEG, SEG), SEG)],
                        sem.at[sl]).start()
                    return carry

                lax.fori_loop(0, _overflow_groups(seg_s, k), one, 0)

    @pl.when(step == 0)
    def _first():
        stack[...] = jnp.zeros_like(stack)
        fetch(step, slot)

    if nsteps > 1:
        @pl.when(step + 1 < nsteps)
        def _prefetch():
            fetch(step + 1, 1 - slot)

    _window_rows(rowid_ref, weights, gate_ref)

    @pl.when(over > 0)
    def _():
        _overflow_rows(seg_s, offo_s, step, over, rowid_ref, weights, gate_ref)

    _wait_rows(FIRST_ROWS + over, lambda n: pltpu.make_async_copy(
        y_hbm.at[0, 0, pl.ds(0, n)], stack.at[slot, pl.ds(0, n)], sem.at[slot]))

    def token_weights(base):
        return weights[pl.ds(base, TB), :].T.astype(BF16)

    w = jnp.concatenate([token_weights(c * TB) for c in range(FIRST_ROWS // TB)], axis=1)
    acc[...] = _dot(w, stack[slot, 0:FIRST_ROWS, :])

    @pl.when(over > 0)
    def _():
        def chunk(c, carry):
            base = pl.multiple_of(FIRST_ROWS + c * TB, TB)
            acc[...] += _dot(token_weights(base), stack[slot, pl.ds(base, TB), :])
            return carry

        lax.fori_loop(0, (over + TB - 1) // TB, chunk, 0)

    x = x_ref[...] + mod_ref[5] * acc[...]
    if final:
        y = _rms(x, g_ref[...])

        @pl.when(g == 0)
        def _():
            xo_ref[...] = y

        @pl.when(g != 0)
        def _():
            h_ref[...] = y
    else:
        xo_ref[...] = x
        h_ref[...] = _modulate(x, g_ref[...], modn_ref[0], modn_ref[1]).astype(BF16)


def _combine_call(tab, y, rowid, gate, x, mods, layer, nxt, gains, gidx, modrow, ngroups, ntok, final):
    nb = ntok // TB
    t = x.shape[0]
    if final:
        assert ngroups == 2
        out_specs = [pl.BlockSpec((TB, D), lambda g, b, *_: (jnp.where(g == 0, b, nb - 1), 0)),
                     pl.BlockSpec((TB, D), lambda g, b, *_: (jnp.where(g == 0, 0, b), 0))]
        out_shape = [jax.ShapeDtypeStruct((ntok, D), F32), jax.ShapeDtypeStruct((ntok, D), F32)]
    else:
        out_specs = [pl.BlockSpec((TB, D), lambda g, b, *_: (g * nb + b, 0)),
                     pl.BlockSpec((TB, D), lambda g, b, *_: (g * nb + b, 0))]
        out_shape = [jax.ShapeDtypeStruct((t, D), F32), jax.ShapeDtypeStruct((t, D), BF16)]
    grid_spec = pltpu.PrefetchScalarGridSpec(
        num_scalar_prefetch=1,
        grid=(ngroups, nb),
        in_specs=[pl.BlockSpec(memory_space=pl.ANY),
                  pl.BlockSpec((None, None, N_EXPERTS, TB), lambda g, b, *_: (g, b, 0, 0)),
                  pl.BlockSpec((None, None, N_EXPERTS, TB), lambda g, b, *_: (g, b, 0, 0)),
                  pl.BlockSpec((TB, D), lambda g, b, *_: (g * nb + b, 0)),
                  pl.BlockSpec((None, None, 6, 1, D),
                               lambda g, b, *_: (layer, modrow(g * nb + b), 0, 0, 0)),
                  pl.BlockSpec((None, None, 6, 1, D),
                               lambda g, b, *_: (nxt, modrow(g * nb + b), 0, 0, 0)),
                  pl.BlockSpec((None, 1, D), lambda g, b, *_: (gidx, 0, 0))],
        out_specs=out_specs,
        scratch_shapes=[pltpu.VMEM((STACK_ROWS, TB), F32), pltpu.VMEM((2, STACK_ROWS, D), BF16),
                        pltpu.VMEM((TB, D), F32), pltpu.SemaphoreType.DMA((2,))])
    return pl.pallas_call(
        functools.partial(_combine_kernel, nb=nb, nsteps=ngroups * nb, final=final),
        grid_spec=grid_spec,
        out_shape=out_shape,
        compiler_params=_cparams(("arbitrary", "arbitrary")),
        name="combine",
    )(tab, y, rowid, gate, x, mods, mods, gains)


def kernel(x_prompt, x_sample, cache_diff_k, cache_diff_v, cache_swa_k, cache_swa_v, cache_mla_ckv, cache_mla_krope, c, c_ctx, w_ada, b_ada, norm_mix, norm_ffn, w_in_even, w_out_even, diff_lambda, diff_subln, swa_sink, w_in_odd, mla_q_norm, w_q_up, mla_kv_norm, w_kv_up, w_out_odd, w_router, w_gate_exp, w_up_exp, w_down_exp, final_norm):
    batch, seq, _ = x_prompt.shape
    dec_batch, dec_seq, _ = x_sample.shape
    past = cache_diff_k.shape[2]
    depth = w_ada.shape[0]
    n_even = w_in_even.shape[0]
    n_odd = w_in_odd.shape[0]
    nc, ns = batch * seq, dec_batch * dec_seq
    assert nc == ns, "the routed-expert kernels take two token groups of equal size"
    assert nc % TM == 0 and dec_seq % TM == 0 and past % 256 == 0 and dec_seq % GRID_W == 0
    assert seq % TQ == 0 and dec_seq % TQ_LATENT == 0
    ntok = nc

    def modrow_of(tile):
        def modrow(r):
            tok = r * tile
            return jnp.where(tok < nc, 0, 1 + jnp.maximum(tok - nc, 0) // dec_seq)
        return modrow

    modrow, modrow_tb = modrow_of(TM), modrow_of(TB)

    rc = -(-(1 + dec_batch) // 16) * 16
    cvec = jnp.zeros((rc, D), F32).at[0].set(c_ctx).at[1:1 + dec_batch].set(c)
    mods = _ada_call(cvec, w_ada, b_ada).reshape(depth, rc, 6, 1, D)

    w_even_b = w_in_even.astype(BF16)
    w_oute_b = w_out_even.astype(BF16)
    w_outo_b = w_out_odd.astype(BF16)
    kr_pad = jnp.zeros((n_odd, D, LANES), F32).at[:, :, 64:96].set(w_in_odd[:, :, 640:672])
    w_odd_b = jnp.concatenate([w_in_odd[:, :, :640], kr_pad], axis=-1).astype(BF16)
    wq = w_q_up.reshape(n_odd, MLA_Q_RANK, MLA_HEADS, HD + MLA_ROPE)
    wq_b = jnp.pad(wq, ((0, 0), (0, 0), (0, 0), (0, LANES - HD - MLA_ROPE))).reshape(
        n_odd, MLA_Q_RANK, MLA_QW).astype(BF16)
    wkv = w_kv_up.reshape(n_odd, MLA_KV_RANK, MLA_HEADS, 2 * HD)
    wk_b = jnp.pad(wkv[..., :HD], ((0, 0), (0, 0), (0, 0), (0, LANES - HD))).reshape(
        n_odd, MLA_KV_RANK, MLA_QW).astype(BF16)
    wv_b = wkv[..., HD:].reshape(n_odd, MLA_KV_RANK, D).astype(BF16)
    w_router_b = jnp.pad(w_router, ((0, 0), (0, 0), (0, LANES - N_EXPERTS))).astype(BF16)
    even_tabs = _rope_tables(dec_seq, 16, _even_lane)
    mla_tabs = _rope_tables(dec_seq, 8, _mla_lane)
    ckr = jnp.zeros((dec_batch, n_odd, past, LANES), F32).at[..., 64:96].set(cache_mla_krope)
    even_caches = (cache_diff_k, cache_diff_v, cache_swa_k, cache_swa_v)

    mix_gains = jnp.concatenate([norm_mix, final_norm[None]], axis=0).reshape(depth + 1, 1, D)
    ffn_gains = norm_ffn.reshape(depth, 1, D)
    q_gains = mla_q_norm.reshape(n_odd, 1, MLA_Q_RANK)
    kv_gains = mla_kv_norm.reshape(n_odd, 1, MLA_KV_RANK)
    sublns = diff_subln.reshape(n_even, 1, 2 * HD)

    xa, xb, xb_off = x_prompt.reshape(nc, D), x_sample.reshape(ns, D), 0
    h = _norm_mod_call(xa, xb, mods, 0, mix_gains, modrow)
    even_state = odd_state = None
    y_prompt = y_sample = None
    for i in range(depth):
        j = i // 2
        if i % 2 == 0:
            qkv_c, *even_state = _proj_even_call(h, w_even_b, j, 0, nc, None, dec_seq,
                                                 prev=even_state, seq=seq)
            (qkv_l,) = _proj_even_call(h, w_even_b, j, nc, ns, even_tabs, dec_seq)
            li = _lambda_init(i)
            o_c = _attn_even_call(qkv_c, None, j, diff_lambda, sublns, swa_sink, batch, seq, 0, li)
            o_l = _attn_even_call(qkv_l, even_caches, j, diff_lambda, sublns, swa_sink,
                                  dec_batch, dec_seq, past, li)
            w_out = w_oute_b
        else:
            q_c, *odd_state = _proj_odd_call(h, w_odd_b, q_gains, wq_b, kv_gains, j, 0, nc, None,
                                             dec_seq, prev=odd_state, seq=seq)
            q_l, ckv_l, kr_l = _proj_odd_call(h, w_odd_b, q_gains, wq_b, kv_gains, j, nc, ns,
                                              mla_tabs, dec_seq)
            o_c = _attn_odd_call(q_c, odd_state[0], odd_state[1], None, j, wk_b, wv_b, batch, seq, 0)
            o_l = _attn_odd_call(q_l, ckv_l, kr_l, (cache_mla_ckv, ckr), j, wk_b, wv_b,
                                 dec_batch, dec_seq, past)
            w_out = w_outo_b
        x, h2, aff = _outproj_call(xa, xb, xb_off, o_c, o_l, w_out, j, mods, i, ffn_gains,
                                   w_router_b, modrow)
        rowid, gate, tab = _select_call(aff, 2, ntok)
        xe = _dispatch_call(tab, h2, rowid, 2, ntok)
        y = _ffn_call(tab, xe, w_gate_exp, w_up_exp, w_down_exp, i, 2, ntok)
        final = i == depth - 1
        nxt = i if final else i + 1
        out_a, out_b = _combine_call(tab, y, rowid, gate, x, mods, i, nxt, mix_gains,
                                     depth if final else nxt, modrow_tb, 2, ntok, final)
        if final:
            y_prompt = out_a.reshape(batch, seq, D)
            y_sample = out_b.reshape(dec_batch, dec_seq, D)
        else:
            x, h = out_a, out_b
            xa, xb, xb_off = x, x, nc // TM

    kd, vd, ks, vs = even_state
    ckv_new, kr_new = odd_state
    return (y_prompt, y_sample,
            kd.reshape(batch, n_even, seq, DIFF_HEADS, 2 * HD),
            vd.reshape(batch, n_even, seq, DIFF_HEADS, 2 * HD),
            ks.reshape(batch, n_even, seq, 2, HD), vs.reshape(batch, n_even, seq, 2, HD),
            ckv_new, kr_new[..., 64:96])
```

```python
import functools
import math

import jax
import jax.numpy as jnp
import numpy as np
from jax import lax
from jax.experimental import pallas as pl
from jax.experimental.pallas import tpu as pltpu

F32 = jnp.float32
BF16 = jnp.bfloat16
I32 = jnp.int32

D = 1024
HD = 64
GRID_W = 64
WINDOW = 128
DIFF_HEADS = 4
SWA_HEADS = 8
MLA_HEADS = 16
MLA_Q_RANK = 384
MLA_KV_RANK = 256
MLA_ROPE = 32
N_EXPERTS = 16
EXPERT_FF = 512
EC_FACTOR = 2
ROPE_BASE = 10000.0
EPS = 1e-6
NEG_INF = -1e30
LOG2E = math.log2(math.e)
EVEN_IN = 2304
LANES = 128
TM = 512
TQ = 256
TQ_LATENT = 256
TB = 256
SEG = 16
VMEM_LIMIT = 56 * 1024 * 1024


def _cparams(sem, vmem=VMEM_LIMIT):
    return pltpu.CompilerParams(dimension_semantics=sem, vmem_limit_bytes=vmem)


def _dot(a, b):
    return jnp.dot(a, b, preferred_element_type=F32)


def _dot_nt(a, b):
    return lax.dot_general(a, b, (((1,), (1,)), ((), ())), preferred_element_type=F32)


def _silu(x):
    return x / (1.0 + jnp.exp(-x))


def _rms(x, g):
    ms = jnp.mean(x * x, axis=-1, keepdims=True)
    return x * lax.rsqrt(ms + EPS) * g


def _modulate(x, g, shift, scale):
    return _rms(x, g) * (1.0 + scale) + shift


def _lambda_init(layer):
    return 0.8 - 0.6 * math.exp(-0.3 * layer)


def _ada_kernel(c_ref, w_ref, b_ref, o_ref):
    s = _silu(c_ref[...]).astype(BF16)
    o_ref[...] = _dot(s, w_ref[...].astype(BF16)) + b_ref[...]


def _ada_call(cvec, w_ada, b_ada):
    depth, _, n6 = w_ada.shape
    rc = cvec.shape[0]
    tn = 512
    return pl.pallas_call(
        _ada_kernel,
        grid=(depth, n6 // tn),
        in_specs=[pl.BlockSpec((rc, D), lambda i, n: (0, 0)),
                  pl.BlockSpec((None, D, tn), lambda i, n: (i, 0, n)),
                  pl.BlockSpec((None, 1, tn), lambda i, n: (i, 0, n))],
        out_specs=pl.BlockSpec((None, rc, tn), lambda i, n: (i, 0, n)),
        out_shape=jax.ShapeDtypeStruct((depth, rc, n6), F32),
        compiler_params=_cparams(("parallel", "parallel")),
        name="ada",
    )(cvec, w_ada, b_ada.reshape(depth, 1, n6))


def _two_part_specs(rows_a, rows_b, width, off_b=0):
    na, nb_ = rows_a // TM, rows_b // TM
    return (pl.BlockSpec((TM, width), lambda r: (jnp.minimum(r, na - 1), 0)),
            pl.BlockSpec((TM, width), lambda r: (off_b + jnp.clip(r - na, 0, nb_ - 1), 0)))


def _norm_mod_kernel(xa_ref, xb_ref, mod_ref, g_ref, h_ref, *, ntile_a):
    def run(x_ref):
        h_ref[...] = _modulate(x_ref[...], g_ref[...], mod_ref[0], mod_ref[1]).astype(BF16)

    @pl.when(pl.program_id(0) < ntile_a)
    def _():
        run(xa_ref)

    @pl.when(pl.program_id(0) >= ntile_a)
    def _():
        run(xb_ref)


def _norm_mod_call(xa, xb, mods, layer, gains, modrow):
    na, nb_ = xa.shape[0], xb.shape[0]
    return pl.pallas_call(
        functools.partial(_norm_mod_kernel, ntile_a=na // TM),
        grid=((na + nb_) // TM,),
        in_specs=[*_two_part_specs(na, nb_, D),
                  pl.BlockSpec((None, None, 6, 1, D), lambda r: (layer, modrow(r), 0, 0, 0)),
                  pl.BlockSpec((None, 1, D), lambda r: (layer, 0, 0))],
        out_specs=pl.BlockSpec((TM, D), lambda r: (r, 0)),
        out_shape=jax.ShapeDtypeStruct((na + nb_, D), BF16),
        compiler_params=_cparams(("parallel",)),
        name="norm_mod",
    )(xa, xb, mods, gains)


def _rope_tables(dec_seq, half, lane_of_dim):
    pos = jnp.arange(dec_seq)
    row = (pos // GRID_W).astype(F32)
    col = (pos % GRID_W).astype(F32)
    inv = ROPE_BASE ** (-(jnp.arange(half, dtype=F32) / half))
    ang = jnp.stack([row[:, None] * inv[None, :], col[:, None] * inv[None, :]])
    info = [lane_of_dim(lane) for lane in range(LANES)]
    axis = np.array([0 if i is None else i[0] for i in info])
    freq = np.array([0 if i is None else i[1] for i in info])
    first = np.array([i is not None and not i[2] for i in info])[None, :]
    second = np.array([i is not None and i[2] for i in info])[None, :]
    lane_ang = ang[axis, :, freq].T
    cos, sin = jnp.cos(lane_ang), jnp.sin(lane_ang)
    return (jnp.where(first | second, cos, 1.0), jnp.where(first, -sin, 0.0),
            jnp.where(second, sin, 0.0))


def _even_lane(lane):
    j = lane % HD
    axis, jj = j // 32, j % 32
    return axis, jj % 16, jj >= 16


def _mla_lane(lane):
    if lane < 64 or lane >= 96:
        return None
    jj = lane - 64
    axis, k = jj // 16, jj % 16
    return axis, k % 8, k >= 8


def _rope(x, c, s1, s2, shift):
    return x * c + pltpu.roll(x, LANES - shift, 1) * s1 + pltpu.roll(x, shift, 1) * s2


_EVEN_ROPE_TILES = tuple(range(0, 8)) + tuple(range(12, 17))
_EVEN_Q_TILES = tuple(range(0, 4)) + tuple(range(12, 16))


def _append_layer(prev_refs, out_refs, new_values):
    for i, (out_ref, new) in enumerate(zip(out_refs, new_values)):
        nbatch, nlayers, seq, width = out_ref.shape
        if prev_refs:
            out_ref[:, 0:nlayers - 1] = prev_refs[i][...]
        out_ref[:, nlayers - 1] = new.reshape(nbatch, seq, width)


def _state_specs(prev, widths, seq):
    nlayers = 1 if prev is None else prev[0].shape[1] + 1
    per = TM // seq
    ins = [] if prev is None else [pl.BlockSpec((per, nlayers - 1, seq, w), lambda r: (r, 0, 0, 0))
                                   for w in widths]
    outs = [pl.BlockSpec((per, nlayers, seq, w), lambda r: (r, 0, 0, 0)) for w in widths]
    return ins, outs, nlayers


def _proj_even_kernel(*refs, rope, caches, nprev):
    h_ref, w_ref = refs[0], refs[1]
    pos = 2
    if rope:
        c_ref, s1_ref, s2_ref = refs[2:5]
        pos = 5
    prev_refs = refs[pos:pos + nprev]
    pos += nprev
    qkv_ref = refs[pos]
    res = _dot(h_ref[...], w_ref[...])
    scale = HD ** -0.5 * LOG2E
    for t in range(EVEN_IN // LANES):
        x = res[:, t * LANES:(t + 1) * LANES]
        if rope and t in _EVEN_ROPE_TILES:
            x = _rope(x, c_ref[...], s1_ref[...], s2_ref[...], 16)
        if t in _EVEN_Q_TILES:
            x = x * scale
        qkv_ref[:, t * LANES:(t + 1) * LANES] = x.astype(BF16)
    if caches:
        _append_layer(prev_refs, refs[pos + 1:pos + 5],
                      (res[:, 512:1024], res[:, 1024:1536], res[:, 2048:2176], res[:, 2176:2304]))


def _proj_even_call(h, w, j, row0, nrows, tables, dec_seq, prev=None, seq=None):
    rope = tables is not None
    caches = not rope
    t0 = row0 // TM
    in_specs = [pl.BlockSpec((TM, D), lambda r: (t0 + r, 0)),
                pl.BlockSpec((None, D, EVEN_IN), lambda r: (j, 0, 0))]
    args = [h, w]
    if rope:
        per = dec_seq // TM
        for _ in range(3):
            in_specs.append(pl.BlockSpec((TM, LANES), lambda r: (r % per, 0)))
        args += list(tables)
    out_specs = [pl.BlockSpec((TM, EVEN_IN), lambda r: (r, 0))]
    out_shape = [jax.ShapeDtypeStruct((nrows, EVEN_IN), BF16)]
    nprev = 0
    if caches:
        widths = (512, 512, LANES, LANES)
        ins, outs, nlayers = _state_specs(prev, widths, seq)
        nprev = len(ins)
        in_specs += ins
        args += [] if prev is None else list(prev)
        out_specs += outs
        out_shape += [jax.ShapeDtypeStruct((nrows // seq, nlayers, seq, wd), F32) for wd in widths]
    return pl.pallas_call(
        functools.partial(_proj_even_kernel, rope=rope, caches=caches, nprev=nprev),
        grid=(nrows // TM,),
        in_specs=in_specs, out_specs=out_specs, out_shape=out_shape,
        compiler_params=_cparams(("parallel",)),
        name="proj_even_lat" if rope else "proj_even_ctx",
    )(*args)


ODD_IN_PAD = MLA_Q_RANK + MLA_KV_RANK + LANES
MLA_QW = MLA_HEADS * LANES


def _proj_odd_kernel(*refs, rope, nprev):
    h_ref, w_ref, qn_ref, wq_ref, kvn_ref = refs[:5]
    pos = 5
    if rope:
        c_ref, s1_ref, s2_ref = refs[5:8]
        pos = 8
    prev_refs = refs[pos:pos + nprev]
    pos += nprev
    q_ref, ckv_ref, kr_ref = refs[pos:pos + 3]
    res = _dot(h_ref[...], w_ref[...])
    cq = _rms(res[:, :MLA_Q_RANK], qn_ref[...]).astype(BF16)
    ckv = _rms(res[:, MLA_Q_RANK:MLA_Q_RANK + MLA_KV_RANK], kvn_ref[...])
    kr = res[:, MLA_Q_RANK + MLA_KV_RANK:]
    if rope:
        kr = _rope(kr, c_ref[...], s1_ref[...], s2_ref[...], 8)
        ckv_ref[...] = ckv
        kr_ref[...] = kr
    else:
        _append_layer(prev_refs, (ckv_ref, kr_ref), (ckv, kr))
    q = _dot(cq, wq_ref[...])
    scale = (HD + MLA_ROPE) ** -0.5 * LOG2E
    for t in range(MLA_HEADS):
        x = q[:, t * LANES:(t + 1) * LANES]
        if rope:
            x = _rope(x, c_ref[...], s1_ref[...], s2_ref[...], 8)
        q_ref[:, t * LANES:(t + 1) * LANES] = (x * scale).astype(BF16)


def _proj_odd_call(h, w_in, qn, wq, kvn, j, row0, nrows, tables, dec_seq, prev=None, seq=None):
    rope = tables is not None
    t0 = row0 // TM
    in_specs = [pl.BlockSpec((TM, D), lambda r: (t0 + r, 0)),
                pl.BlockSpec((None, D, ODD_IN_PAD), lambda r: (j, 0, 0)),
                pl.BlockSpec((None, 1, MLA_Q_RANK), lambda r: (j, 0, 0)),
                pl.BlockSpec((None, MLA_Q_RANK, MLA_QW), lambda r: (j, 0, 0)),
                pl.BlockSpec((None, 1, MLA_KV_RANK), lambda r: (j, 0, 0))]
    args = [h, w_in, qn, wq, kvn]
    if rope:
        per = dec_seq // TM
        for _ in range(3):
            in_specs.append(pl.BlockSpec((TM, LANES), lambda r: (r % per, 0)))
        args += list(tables)
    out_specs = [pl.BlockSpec((TM, MLA_QW), lambda r: (r, 0))]
    out_shape = [jax.ShapeDtypeStruct((nrows, MLA_QW), BF16)]
    widths = (MLA_KV_RANK, LANES)
    nprev = 0
    if rope:
        out_specs += [pl.BlockSpec((TM, wd), lambda r: (r, 0)) for wd in widths]
        out_shape += [jax.ShapeDtypeStruct((nrows, wd), F32) for wd in widths]
    else:
        ins, outs, nlayers = _state_specs(prev, widths, seq)
        nprev = len(ins)
        in_specs += ins
        args += [] if prev is None else list(prev)
        out_specs += outs
        out_shape += [jax.ShapeDtypeStruct((nrows // seq, nlayers, seq, wd), F32) for wd in widths]
    return pl.pallas_call(
        functools.partial(_proj_odd_kernel, rope=rope, nprev=nprev),
        grid=(nrows // TM,),
        in_specs=in_specs,
        out_specs=out_specs,
        out_shape=out_shape,
        compiler_params=_cparams(("parallel",)),
        name="proj_odd_lat" if rope else "proj_odd_ctx",
    )(*args)


def _query_tile(past):
    return TQ_LATENT if past > 0 else TQ


def _ones_lane(half):
    return HD if half == 0 else 0


def _row_sum(e, o, half, from_matmul):
    if from_matmul:
        one = _ones_lane(half)
        return o[:, one:one + 1]
    return jnp.sum(e, axis=-1, keepdims=True)


def _half_values(v, half):
    lane = lax.broadcasted_iota(I32, (1, LANES), 1)
    keep = (lane < HD) if half == 0 else (lane >= HD)
    return jnp.where(keep, v, jnp.where(lane == _ones_lane(half), 1.0, 0.0)).astype(BF16)


VDT_ROWS = 2 * HD + SEG


def _attn_even_kernel(*refs, seq, past, lam_init, layer):
    latent = past > 0
    tq = _query_tile(past)
    n = past + seq
    if latent:
        (qkv_ref, ck_ref, cv_ref, sk_ref, sv_ref, lam_ref, subln_ref, sink_ref,
         o_ref, kd, vd, ka, vl, vh) = refs
    else:
        qkv_ref, lam_ref, subln_ref, sink_ref, o_ref, kd, vd, ka, vl, vh = refs
    qi = pl.program_id(1)
    lo = lax.broadcasted_iota(I32, (1, LANES), 1) < HD

    @pl.when(qi == 0)
    def _build():
        chunk = 256
        for c0 in range(0, n, chunk):
            rows = slice(c0, c0 + chunk)
            if c0 < past:
                prow = slice(c0, c0 + chunk)
                for h in range(DIFF_HEADS):
                    kd[rows, h * LANES:(h + 1) * LANES] = ck_ref[prow, h, :].astype(BF16)
                    vd[h * VDT_ROWS:h * VDT_ROWS + LANES, rows] = cv_ref[prow, h, :].T.astype(BF16)
                kt = jnp.concatenate([sk_ref[prow, 0, :], sk_ref[prow, 1, :]], axis=1)
                vt = jnp.concatenate([sv_ref[prow, 0, :], sv_ref[prow, 1, :]], axis=1)
            else:
                orow = slice(c0 - past, c0 - past + chunk)
                kd[rows, :] = qkv_ref[orow, 512:1024]
                for h in range(DIFF_HEADS):
                    vown = qkv_ref[orow, 1024 + h * LANES:1024 + (h + 1) * LANES].astype(F32)
                    vd[h * VDT_ROWS:h * VDT_ROWS + LANES, rows] = vown.T.astype(BF16)
                kt = qkv_ref[orow, 2048:2176].astype(F32)
                vt = qkv_ref[orow, 2176:2304].astype(F32)
            kr = pltpu.roll(kt, HD, 1)
            vr = pltpu.roll(vt, HD, 1)
            ka[0, rows, :] = jnp.where(lo, kt, kr).astype(BF16)
            ka[1, rows, :] = jnp.where(lo, kr, kt).astype(BF16)
            vl[0, rows, :] = _half_values(vt, 0)
            vh[0, rows, :] = _half_values(vr, 1)
            vl[1, rows, :] = _half_values(vr, 0)
            vh[1, rows, :] = _half_values(vt, 1)
        for h in range(DIFF_HEADS):
            vd[h * VDT_ROWS + LANES:(h + 1) * VDT_ROWS, :] = jnp.ones((SEG, n), BF16)

    r0 = pl.multiple_of(qi * tq, tq)
    lam = lam_ref[...]
    lam_full = (jnp.exp(jnp.sum(lam[0:1] * lam[1:2], axis=-1, keepdims=True))
                - jnp.exp(jnp.sum(lam[2:3] * lam[3:4], axis=-1, keepdims=True)) + lam_init)
    zero_b = jnp.zeros((), BF16)

    for h in range(DIFF_HEADS):
        cs = slice(h * LANES, (h + 1) * LANES)
        qt = qkv_ref[pl.ds(r0, tq), cs]
        kh = kd[:, cs]
        ots = []
        for comp in range(2):
            qc = jnp.where(lo, qt, zero_b) if comp == 0 else jnp.where(lo, zero_b, qt)
            s = _dot_nt(qc, kh)
            m = jnp.max(s, axis=-1, keepdims=True)
            e = jnp.exp2(s - m).astype(BF16)
            ots.append(_dot_nt(vd[h * VDT_ROWS:(h + 1) * VDT_ROWS, :], e))
        ot = (ots[0][0:LANES, :] * (1.0 / ots[0][LANES:LANES + 1, :])
              - ots[1][0:LANES, :] * (lam_full / ots[1][LANES:LANES + 1, :]))
        o = _rms(ot.T, subln_ref[...]) * (1.0 - lam_init)
        o_ref[:, cs] = o.astype(BF16)

    nblk = seq // WINDOW
    dense = past if latent else seq
    if latent:
        per = tq // WINDOW
        offsets = tuple(range(-1, per + 1))
        rr = lax.broadcasted_iota(I32, (tq, WINDOW), 0)
        cc = lax.broadcasted_iota(I32, (tq, WINDOW), 1)
        band, starts = {}, {}
        for d in offsets:
            blk = qi * per + d
            inside = jnp.logical_and(blk >= 0, blk < nblk)
            band[d] = jnp.logical_and(jnp.abs(rr - cc - d * WINDOW) <= WINDOW, inside)
            starts[d] = pl.multiple_of(past + jnp.clip(blk, 0, nblk - 1) * WINDOW, WINDOW)
    for i in range(SWA_HEADS // 2):
        hk = i // 2
        cs = slice(1536 + i * LANES, 1536 + (i + 1) * LANES)
        qt = qkv_ref[pl.ds(r0, tq), cs]
        halves = []
        for half in range(2):
            qc = jnp.where(lo, qt, zero_b) if half == 0 else jnp.where(lo, zero_b, qt)
            vsel = vl if half == 0 else vh
            sink = sink_ref[layer, 2 * i + half] * LOG2E
            parts = [_dot_nt(qc, ka[hk, 0:dense, :])]
            if latent:
                for d in offsets:
                    s = _dot_nt(qc, ka[hk, pl.ds(starts[d], WINDOW), :])
                    parts.append(jnp.where(band[d], s, NEG_INF))
            s_all = jnp.concatenate(parts, axis=1) if len(parts) > 1 else parts[0]
            m = jnp.maximum(jnp.max(s_all, axis=-1, keepdims=True), sink)
            e = jnp.exp2(s_all - m)
            eb = e.astype(BF16)
            o = _dot(eb[:, 0:dense], vsel[hk, 0:dense, :])
            if latent:
                for k, d in enumerate(offsets):
                    o += _dot(eb[:, dense + k * WINDOW:dense + (k + 1) * WINDOW],
                              vsel[hk, pl.ds(starts[d], WINDOW), :])
            den = _row_sum(e, o, half, from_matmul=latent) + jnp.exp2(sink - m)
            halves.append(o * (1.0 / den))
        o_ref[:, 512 + i * LANES:512 + (i + 1) * LANES] = jnp.where(lo, halves[0], halves[1]).astype(BF16)


def _attn_even_call(qkv, caches, j, lam, subln, sink, nbatch, seq, past, lam_init):
    n = past + seq
    latent = past > 0
    tq = _query_tile(past)
    in_specs = [pl.BlockSpec((seq, EVEN_IN), lambda b, q: (b, 0))]
    args = [qkv]
    if latent:
        ck, cv, sk, sv = caches
        in_specs += [pl.BlockSpec((None, None, past, DIFF_HEADS, 2 * HD), lambda b, q: (b, j, 0, 0, 0)),
                     pl.BlockSpec((None, None, past, DIFF_HEADS, 2 * HD), lambda b, q: (b, j, 0, 0, 0)),
                     pl.BlockSpec((None, None, past, 2, HD), lambda b, q: (b, j, 0, 0, 0)),
                     pl.BlockSpec((None, None, past, 2, HD), lambda b, q: (b, j, 0, 0, 0))]
        args += [ck, cv, sk, sv]
    in_specs += [pl.BlockSpec((None, 4, HD), lambda b, q: (j, 0, 0)),
                 pl.BlockSpec((None, 1, 2 * HD), lambda b, q: (j, 0, 0)),
                 pl.BlockSpec(memory_space=pltpu.SMEM)]
    args += [lam, subln, sink]
    return pl.pallas_call(
        functools.partial(_attn_even_kernel, seq=seq, past=past, lam_init=lam_init, layer=j),
        grid=(nbatch, seq // tq),
        in_specs=in_specs,
        out_specs=pl.BlockSpec((tq, D), lambda b, q: (b * (seq // tq) + q, 0)),
        out_shape=jax.ShapeDtypeStruct((nbatch * seq, D), BF16),
        scratch_shapes=[pltpu.VMEM((n, 512), BF16),
                        pltpu.VMEM((DIFF_HEADS * VDT_ROWS, n), BF16),
                        pltpu.VMEM((2, n, LANES), BF16), pltpu.VMEM((2, n, LANES), BF16),
                        pltpu.VMEM((2, n, LANES), BF16)],
        compiler_params=_cparams(("arbitrary", "arbitrary")),
        name="attn_even_lat" if latent else "attn_even_ctx",
    )(*args)


VT_ROWS = HD + SEG


def _attn_odd_kernel(*refs, seq, past):
    latent = past > 0
    tq = _query_tile(past)
    n = past + seq
    if latent:
        q_ref, ckv_ref, kr_ref, cckv_ref, ckr_ref, wk_ref, wv_ref, o_ref, kf, vt = refs
    else:
        q_ref, ckv_ref, kr_ref, wk_ref, wv_ref, o_ref, kf, vlo, vhi = refs
    qi = pl.program_id(1)

    @pl.when(qi == 0)
    def _build():
        chunk = 256
        for c0 in range(0, n, chunk):
            rows = slice(c0, c0 + chunk)
            if c0 < past:
                ckv = cckv_ref[c0:c0 + chunk, :].astype(BF16)
                kr = ckr_ref[c0:c0 + chunk, :]
            else:
                ckv = ckv_ref[c0 - past:c0 - past + chunk, :].astype(BF16)
                kr = kr_ref[c0 - past:c0 - past + chunk, :]
            kk = _dot(ckv, wk_ref[...])
            for h in range(MLA_HEADS):
                cs = slice(h * LANES, (h + 1) * LANES)
                kf[rows, cs] = (kk[:, cs] + kr).astype(BF16)
            vv = _dot(ckv, wv_ref[...])
            for i in range(MLA_HEADS // 2):
                cs = slice(i * LANES, (i + 1) * LANES)
                if latent:
                    pair = vv[:, cs].T.astype(BF16)
                    for half in range(2):
                        r = (2 * i + half) * VT_ROWS
                        vt[r:r + HD, rows] = pair[half * HD:(half + 1) * HD, :]
                else:
                    vlo[rows, cs] = _half_values(vv[:, cs], 0)
                    vhi[rows, cs] = _half_values(vv[:, cs], 1)
        if latent:
            for h in range(MLA_HEADS):
                vt[h * VT_ROWS + HD:(h + 1) * VT_ROWS, :] = jnp.ones((SEG, n), BF16)

    r0 = pl.multiple_of(qi * tq, tq)
    lo = lax.broadcasted_iota(I32, (1, LANES), 1) < HD
    for i in range(MLA_HEADS // 2):
        halves = []
        for half in range(2):
            h = 2 * i + half
            cs = slice(h * LANES, (h + 1) * LANES)
            s = _dot_nt(q_ref[pl.ds(r0, tq), cs], kf[:, cs])
            m = jnp.max(s, axis=-1, keepdims=True)
            e = jnp.exp2(s - m)
            if latent:
                ot = _dot_nt(vt[h * VT_ROWS:(h + 1) * VT_ROWS, :], e.astype(BF16))
                halves.append(ot[0:HD, :] * (1.0 / ot[HD:HD + 1, :]))
            else:
                vsel = vlo if half == 0 else vhi
                o = _dot(e.astype(BF16), vsel[:, i * LANES:(i + 1) * LANES])
                halves.append(o * (1.0 / jnp.sum(e, axis=-1, keepdims=True)))
        if latent:
            tile = jnp.concatenate(halves, axis=0).T
        else:
            tile = jnp.where(lo, halves[0], halves[1])
        o_ref[:, i * LANES:(i + 1) * LANES] = tile.astype(BF16)


def _attn_odd_call(q, ckv, kr, caches, j, wk, wv, nbatch, seq, past):
    n = past + seq
    latent = past > 0
    tq = _query_tile(past)
    in_specs = [pl.BlockSpec((seq, MLA_QW), lambda b, qq: (b, 0))]
    if latent:
        in_specs += [pl.BlockSpec((seq, MLA_KV_RANK), lambda b, qq: (b, 0)),
                     pl.BlockSpec((seq, LANES), lambda b, qq: (b, 0))]
    else:
        last = ckv.shape[1] - 1
        in_specs += [pl.BlockSpec((None, None, seq, MLA_KV_RANK), lambda b, qq: (b, last, 0, 0)),
                     pl.BlockSpec((None, None, seq, LANES), lambda b, qq: (b, last, 0, 0))]
    args = [q, ckv, kr]
    if latent:
        in_specs += [pl.BlockSpec((None, None, past, MLA_KV_RANK), lambda b, qq: (b, j, 0, 0)),
                     pl.BlockSpec((None, None, past, LANES), lambda b, qq: (b, j, 0, 0))]
        args += list(caches)
    in_specs += [pl.BlockSpec((None, MLA_KV_RANK, MLA_QW), lambda b, qq: (j, 0, 0)),
                 pl.BlockSpec((None, MLA_KV_RANK, D), lambda b, qq: (j, 0, 0))]
    args += [wk, wv]
    return pl.pallas_call(
        functools.partial(_attn_odd_kernel, seq=seq, past=past),
        grid=(nbatch, seq // tq),
        in_specs=in_specs,
        out_specs=pl.BlockSpec((tq, D), lambda b, qq: (b * (seq // tq) + qq, 0)),
        out_shape=jax.ShapeDtypeStruct((nbatch * seq, D), BF16),
        scratch_shapes=([pltpu.VMEM((n, MLA_QW), BF16), pltpu.VMEM((MLA_HEADS * VT_ROWS, n), BF16)]
                        if latent else
                        [pltpu.VMEM((n, MLA_QW), BF16), pltpu.VMEM((n, D), BF16),
                         pltpu.VMEM((n, D), BF16)]),
        compiler_params=_cparams(("arbitrary", "arbitrary")),
        name="attn_odd_lat" if latent else "attn_odd_ctx",
    )(*args)


def _outproj_kernel(xa_ref, xb_ref, oa_ref, ob_ref, w_ref, mod_ref, g_ref, wr_ref,
                    xo_ref, h_ref, aff_ref, *, ntile_a):
    def run(x_ref, o_ref):
        x = x_ref[...] + mod_ref[2] * _dot(o_ref[...], w_ref[...])
        xo_ref[...] = x
        h = _modulate(x, g_ref[...], mod_ref[3], mod_ref[4]).astype(BF16)
        h_ref[...] = h
        logits = _dot(h, wr_ref[...])
        lane = lax.broadcasted_iota(I32, (TM, LANES), 1)
        lg = jnp.where(lane < N_EXPERTS, logits, -jnp.inf)
        e = jnp.exp(lg - jnp.max(lg, axis=-1, keepdims=True))
        aff = e / jnp.sum(e, axis=-1, keepdims=True)
        for c in range(TM // TB):
            aff_ref[c] = aff[c * TB:(c + 1) * TB, :].T[0:N_EXPERTS, :]

    @pl.when(pl.program_id(0) < ntile_a)
    def _():
        run(xa_ref, oa_ref)

    @pl.when(pl.program_id(0) >= ntile_a)
    def _():
        run(xb_ref, ob_ref)


def _outproj_call(xa, xb, xb_off, o_a, o_b, w, widx, mods, layer, gains, w_router, modrow):
    na, nb_ = o_a.shape[0], o_b.shape[0]
    t = na + nb_
    assert TM % TB == 0
    return pl.pallas_call(
        functools.partial(_outproj_kernel, ntile_a=na // TM),
        grid=(t // TM,),
        in_specs=[*_two_part_specs(na, nb_, D, xb_off), *_two_part_specs(na, nb_, D),
                  pl.BlockSpec((None, D, D), lambda r: (widx, 0, 0)),
                  pl.BlockSpec((None, None, 6, 1, D), lambda r: (layer, modrow(r), 0, 0, 0)),
                  pl.BlockSpec((None, 1, D), lambda r: (layer, 0, 0)),
                  pl.BlockSpec((None, D, LANES), lambda r: (layer, 0, 0))],
        out_specs=[pl.BlockSpec((TM, D), lambda r: (r, 0)),
                   pl.BlockSpec((TM, D), lambda r: (r, 0)),
                   pl.BlockSpec((TM // TB, N_EXPERTS, TB), lambda r: (r, 0, 0))],
        out_shape=[jax.ShapeDtypeStruct((t, D), F32), jax.ShapeDtypeStruct((t, D), BF16),
                   jax.ShapeDtypeStruct((t // TB, N_EXPERTS, TB), F32)],
        compiler_params=_cparams(("parallel",)),
        name="outproj",
    )(xa, xb, o_a, o_b, w, mods, gains, w_router)


def _select_kernel(aff_ref, rowid_ref, gate_ref, tab_ref, *, nb, cap):
    ne = N_EXPERTS
    nr = nb * ne
    a = aff_ref[...].reshape(nr, TB)
    ri = lax.broadcasted_iota(I32, (nr, nr), 0)
    ci = lax.broadcasted_iota(I32, (nr, nr), 1)
    same_e = (ri & (ne - 1)) == (ci & (ne - 1))
    same_b = (ri >> 4) == (ci >> 4)
    m_e = jnp.where(same_e, 1.0, 0.0).astype(BF16)
    m_b = jnp.where(same_b, 1.0, 0.0).astype(BF16)
    m_a = jnp.where(jnp.logical_and(same_e, ci < ri), 1.0, 0.0).astype(BF16)
    m_o = jnp.where(jnp.logical_and(same_b, ci < ri), 1.0, 0.0).astype(BF16)
    ui = lax.broadcasted_iota(I32, (TB, TB), 0)
    uj = lax.broadcasted_iota(I32, (TB, TB), 1)
    upper = jnp.where(ui < uj, 1.0, 0.0).astype(BF16)

    def rows_to_lanes(col):
        return jnp.broadcast_to(col, (nr, LANES)).astype(BF16)

    wide = jnp.concatenate([aff_ref[b] for b in range(nb)], axis=1)

    def count_ge(value):
        return jnp.sum(jnp.where(wide >= value, 1.0, 0.0), axis=-1, keepdims=True)

    def bisect(i, v):
        cand = v | jnp.left_shift(jnp.int32(1), 30 - i)
        return jnp.where(count_ge(pltpu.bitcast(cand, F32)) >= cap, cand, v)

    thr = lax.fori_loop(0, 31, bisect, jnp.zeros((ne, 1), I32))

    def refine(i, lohi):
        lo_e, hi_e = lohi
        mid = 0.5 * (lo_e + hi_e)
        take = count_ge(mid) >= cap
        return jnp.where(take, mid, lo_e), jnp.where(take, hi_e, mid)

    lo_e, hi_e = lax.fori_loop(0, TIE_STEPS, refine,
                               (pltpu.bitcast(thr, F32), pltpu.bitcast(thr + 1, F32)))
    lo_v = jnp.concatenate([lo_e] * nb, axis=0)
    hi_v = jnp.concatenate([hi_e] * nb, axis=0)
    gt = jnp.where(a >= hi_v, 1.0, 0.0)
    eq = jnp.where(jnp.logical_and(a >= lo_v, a < hi_v), 1.0, 0.0)
    n_gt = _dot(m_e, rows_to_lanes(jnp.sum(gt, axis=-1, keepdims=True)))[:, 0:1]
    need = cap - n_gt
    eq_before = (_dot(m_a, rows_to_lanes(jnp.sum(eq, axis=-1, keepdims=True)))[:, 0:1]
                 + _dot(eq.astype(BF16), upper))
    sel = jnp.where(jnp.logical_and(eq > 0.0, eq_before < need), 1.0, gt)
    local = _dot(sel.astype(BF16), upper)
    cnt = jnp.sum(sel, axis=-1, keepdims=True)
    seg = jnp.floor((cnt + (SEG - 1)) * (1.0 / SEG)) * SEG
    segb = rows_to_lanes(seg)
    over = jnp.maximum(seg - WINR, 0.0)
    overb = rows_to_lanes(over)
    off_over = FIRST_ROWS + _dot(m_o, overb)[:, 0:1]
    off_buf = _dot(m_a, segb)[:, 0:1]
    over_blk = _dot(m_b, overb)[:, 0:1]
    rows_exp = _dot(m_e, segb)[:, 0:1]
    tiles_exp = jnp.floor((rows_exp + (TF - 1)) / TF)
    expert = (lax.broadcasted_iota(I32, (nr, 1), 0) & (ne - 1)).astype(F32)
    row = jnp.where(local < WINR, expert * WINR + local, off_over + local - WINR)
    rowid_ref[...] = jnp.where(sel > 0.0, row, -1.0).astype(I32).reshape(nb, ne, TB)
    gate_ref[...] = jnp.where(sel > 0.0, a, 0.0).reshape(nb, ne, TB)
    tl = lax.broadcasted_iota(I32, (nr, LANES), 1)
    tab = jnp.where(tl == 0, seg, jnp.where(tl == 1, off_over, jnp.where(
        tl == 2, off_buf, jnp.where(tl == 3, over_blk, jnp.where(tl == 4, rows_exp, tiles_exp)))))
    tab_ref[...] = tab.T[0:8, :].astype(I32)


def _select_call(aff, ngroups, ntok):
    nb = ntok // TB
    cap = EC_FACTOR * ntok // N_EXPERTS
    nr = nb * N_EXPERTS
    return pl.pallas_call(
        functools.partial(_select_kernel, nb=nb, cap=cap),
        grid=(ngroups,),
        in_specs=[pl.BlockSpec((nb, N_EXPERTS, TB), lambda g: (g, 0, 0))],
        out_specs=[pl.BlockSpec((None, nb, N_EXPERTS, TB), lambda g: (g, 0, 0, 0)),
                   pl.BlockSpec((None, nb, N_EXPERTS, TB), lambda g: (g, 0, 0, 0)),
                   pl.BlockSpec((None, 8, nr), lambda g: (g, 0, 0))],
        out_shape=[jax.ShapeDtypeStruct((ngroups, nb, N_EXPERTS, TB), I32),
                   jax.ShapeDtypeStruct((ngroups, nb, N_EXPERTS, TB), F32),
                   jax.ShapeDtypeStruct((ngroups, 8, nr), I32)],
        compiler_params=_cparams(("arbitrary",)),
        name="select",
    )(aff)


STACK_ROWS = N_EXPERTS * TB


FIRST_ROWS = 3 * TB


WINR = 3 * SEG
assert N_EXPERTS * WINR == FIRST_ROWS
TF = 672
TIE_STEPS = 12


def _window_rows(rowid_ref, dst, value_ref=None):
    for e in range(N_EXPERTS):
        rid_e = rowid_ref[e:e + 1, :]
        val_e = 1.0 if value_ref is None else value_ref[e:e + 1, :]
        hit = rid_e == lax.broadcasted_iota(I32, (WINR, TB), 0) + e * WINR
        dst[e * WINR:(e + 1) * WINR, :] = jnp.where(hit, val_e, 0.0).astype(dst.dtype)


class _Table:
    def __init__(self, tab_s, row, nb, per_block=False):
        self.tab_s, self.row, self.per = tab_s, row, nb if per_block else nb * N_EXPERTS
        self.scale = N_EXPERTS if per_block else 1

    def __getitem__(self, k):
        if self.per & (self.per - 1) == 0:
            hi, low = lax.shift_right_logical(k, self.per.bit_length() - 1), k & (self.per - 1)
        else:
            hi, low = lax.div(k, self.per), lax.rem(k, self.per)
        return self.tab_s[hi, self.row, low * self.scale]


def _tables(tab_s, nb):
    return (_Table(tab_s, 0, nb), _Table(tab_s, 1, nb), _Table(tab_s, 2, nb),
            _Table(tab_s, 3, nb, per_block=True))


def _overflow_groups(seg_s, k):
    return lax.div(jnp.maximum(seg_s[k] - WINR, 0), SEG)


def _overflow_rows(seg_s, offo_s, step, over, rowid_ref, dst, value_ref=None):
    def zero(i, carry):
        r0 = pl.multiple_of(FIRST_ROWS + i * SEG, SEG)
        dst[pl.ds(r0, SEG), :] = jnp.zeros((SEG, TB), dst.dtype)
        return carry

    lax.fori_loop(0, ((over + TB - 1) // TB) * (TB // SEG), zero, 0)
    for e in range(N_EXPERTS):
        k = step * N_EXPERTS + e
        rid_e = rowid_ref[e:e + 1, :]
        val_e = 1.0 if value_ref is None else value_ref[e:e + 1, :]

        def group(i, carry, off=offo_s[k], rid_e=rid_e, val_e=val_e):
            r0 = pl.multiple_of(off + i * SEG, SEG)
            hit = rid_e == lax.broadcasted_iota(I32, (SEG, TB), 0) + r0
            dst[pl.ds(r0, SEG), :] = jnp.where(hit, val_e, 0.0).astype(dst.dtype)
            return carry

        lax.fori_loop(0, _overflow_groups(seg_s, k), group, 0)


def _wait_rows(rows, make_copy):
    def big(i, carry):
        make_copy(TB).wait()
        return carry

    def small(i, carry):
        make_copy(SEG).wait()
        return carry

    lax.fori_loop(0, lax.div(rows, TB), big, 0)
    lax.fori_loop(0, lax.div(lax.rem(rows, TB), SEG), small, 0)


def _dispatch_kernel(tab_s, h_ref, rowid_ref, xe_hbm,
                     onehot, stack, zbuf, sem, zsem, *, nb, nsteps, cap):
    g = pl.program_id(0)
    b = pl.program_id(1)
    step = g * nb + b
    slot = lax.rem(step, 2)
    seg_s, offo_s, offb_s, over_s = _tables(tab_s, nb)
    over = over_s[step]
    xrows = xe_hbm.shape[2]

    def wait_slot(nrows, sl):
        _wait_rows(nrows, lambda n: pltpu.make_async_copy(
            stack.at[sl, pl.ds(0, n)], xe_hbm.at[0, 0, pl.ds(0, n)], sem.at[sl]))

    @pl.when(step == 0)
    def _init():
        stack[...] = jnp.zeros_like(stack)

    @pl.when(b == 0)
    def _zero_unused():
        zbuf[...] = jnp.zeros_like(zbuf)
        for e in range(N_EXPERTS):
            pltpu.make_async_copy(zbuf, xe_hbm.at[g, e, pl.ds(cap, xrows - cap)], zsem).start()

    _window_rows(rowid_ref, onehot)
    h = h_ref[...]
    stack[slot, 0:FIRST_ROWS, :] = _dot(onehot[0:FIRST_ROWS, :], h).astype(BF16)

    @pl.when(over > 0)
    def _overflow():
        _overflow_rows(seg_s, offo_s, step, over, rowid_ref, onehot)

        def chunk(c, carry):
            base = pl.multiple_of(FIRST_ROWS + c * TB, TB)
            stack[slot, pl.ds(base, TB), :] = _dot(onehot[pl.ds(base, TB), :], h).astype(BF16)
            return carry

        lax.fori_loop(0, (over + TB - 1) // TB, chunk, 0)

    @pl.when(b == 0)
    def _zero_unused_done():
        for e in range(N_EXPERTS):
            pltpu.make_async_copy(zbuf, xe_hbm.at[g, e, pl.ds(cap, xrows - cap)], zsem).wait()

    @pl.when(step >= 1)
    def _previous_landed():
        wait_slot(FIRST_ROWS + over_s[step - 1], 1 - slot)

    for e in range(N_EXPERTS):
        k = step * N_EXPERTS + e
        pltpu.make_async_copy(
            stack.at[slot, e * WINR:(e + 1) * WINR],
            xe_hbm.at[g, e, pl.ds(pl.multiple_of(offb_s[k], SEG), WINR)], sem.at[slot]).start()

    @pl.when(over > 0)
    def _overflow_copies():
        for e in range(N_EXPERTS):
            k = step * N_EXPERTS + e

            def one(i, carry, e=e, k=k):
                pltpu.make_async_copy(
                    stack.at[slot, pl.ds(pl.multiple_of(offo_s[k] + i * SEG, SEG), SEG)],
                    xe_hbm.at[g, e, pl.ds(pl.multiple_of(offb_s[k] + WINR + i * SEG, SEG), SEG)],
                    sem.at[slot]).start()
                return carry

            lax.fori_loop(0, _overflow_groups(seg_s, k), one, 0)

    @pl.when(step == nsteps - 1)
    def _drain():
        wait_slot(FIRST_ROWS + over, slot)


def _expert_rows(ntok):
    cap = EC_FACTOR * ntok // N_EXPERTS
    worst = cap + (ntok // TB) * (SEG - 1)
    tiles = -(-(worst + WINR) // TF)
    assert cap % SEG == 0 and cap >= WINR and tiles * TF > cap
    return cap, tiles


def _dispatch_call(tab, h, rowid, ngroups, ntok):
    nb = ntok // TB
    cap, tiles = _expert_rows(ntok)
    xrows = tiles * TF
    grid_spec = pltpu.PrefetchScalarGridSpec(
        num_scalar_prefetch=1,
        grid=(ngroups, nb),
        in_specs=[pl.BlockSpec((TB, D), lambda g, b, *_: (g * nb + b, 0)),
                  pl.BlockSpec((None, None, N_EXPERTS, TB), lambda g, b, *_: (g, b, 0, 0))],
        out_specs=pl.BlockSpec(memory_space=pl.ANY),
        scratch_shapes=[pltpu.VMEM((STACK_ROWS, TB), BF16), pltpu.VMEM((2, STACK_ROWS, D), BF16),
                        pltpu.VMEM((xrows - cap, D), BF16),
                        pltpu.SemaphoreType.DMA((2,)), pltpu.SemaphoreType.DMA])
    return pl.pallas_call(
        functools.partial(_dispatch_kernel, nb=nb, nsteps=ngroups * nb, cap=cap),
        grid_spec=grid_spec,
        out_shape=jax.ShapeDtypeStruct((ngroups, N_EXPERTS, xrows, D), BF16),
        compiler_params=_cparams(("arbitrary", "arbitrary")),
        name="dispatch",
    )(tab, h, rowid)


def _ffn_kernel(tab_s, xe_ref, wg_hbm, wu_hbm, wd_hbm, y_ref, wg32, wu32, wd32, wgb, wub, wdb, sem,
                *, layer, ngroups, tiles):
    e = pl.program_id(0)
    g = pl.program_id(1)
    j = pl.program_id(2)
    sl = lax.rem(e, 2)

    def weight_copies(ee, slot):
        return (pltpu.make_async_copy(wg_hbm.at[layer, ee], wg32.at[slot], sem.at[slot]),
                pltpu.make_async_copy(wu_hbm.at[layer, ee], wu32.at[slot], sem.at[slot]),
                pltpu.make_async_copy(wd_hbm.at[layer, ee], wd32.at[slot], sem.at[slot]))

    def land_and_cast(ee, slot):
        for cp in weight_copies(ee, slot):
            cp.wait()
        wgb[slot] = wg32[slot].astype(BF16)
        wub[slot] = wu32[slot].astype(BF16)
        wdb[slot] = wd32[slot].astype(BF16)

    @pl.when(jnp.logical_and(g == 0, j == 0))
    def _fetch():
        @pl.when(e == 0)
        def _():
            for cp in weight_copies(e, sl):
                cp.start()
            land_and_cast(e, sl)

        @pl.when(e + 1 < N_EXPERTS)
        def _():
            for cp in weight_copies(e + 1, 1 - sl):
                cp.start()

    live = j < tab_s[g, 5, e]

    @pl.when(live)
    def _run():
        x = xe_ref[...]
        hid = (_silu(_dot(x, wgb[sl])) * _dot(x, wub[sl])).astype(BF16)
        y_ref[...] = _dot(hid, wdb[sl]).astype(BF16)

    @pl.when(jnp.logical_not(live))
    def _skip():
        y_ref[...] = jnp.zeros_like(y_ref)

    @pl.when(jnp.logical_and(jnp.logical_and(g == ngroups - 1, j == tiles - 1), e + 1 < N_EXPERTS))
    def _next_weights():
        land_and_cast(e + 1, 1 - sl)


def _ffn_call(tab, xe, wg, wu, wd, layer, ngroups, ntok):
    _, tiles = _expert_rows(ntok)

    def xmap(e, g, j, tab_s):
        return (g, e, jnp.minimum(j, tab_s[g, 5, e] - 1), 0)

    grid_spec = pltpu.PrefetchScalarGridSpec(
        num_scalar_prefetch=1,
        grid=(N_EXPERTS, ngroups, tiles),
        in_specs=[pl.BlockSpec((None, None, TF, D), xmap),
                  pl.BlockSpec(memory_space=pl.ANY), pl.BlockSpec(memory_space=pl.ANY),
                  pl.BlockSpec(memory_space=pl.ANY)],
        out_specs=pl.BlockSpec((None, None, TF, D), lambda e, g, j, nt: (g, e, j, 0)),
        scratch_shapes=[pltpu.VMEM((2, D, EXPERT_FF), F32), pltpu.VMEM((2, D, EXPERT_FF), F32),
                        pltpu.VMEM((2, EXPERT_FF, D), F32),
                        pltpu.VMEM((2, D, EXPERT_FF), BF16), pltpu.VMEM((2, D, EXPERT_FF), BF16),
                        pltpu.VMEM((2, EXPERT_FF, D), BF16), pltpu.SemaphoreType.DMA((2,))])
    return pl.pallas_call(
        functools.partial(_ffn_kernel, layer=layer, ngroups=ngroups, tiles=tiles),
        grid_spec=grid_spec,
        out_shape=jax.ShapeDtypeStruct((ngroups, N_EXPERTS, tiles * TF, D), BF16),
        compiler_params=_cparams(("arbitrary", "arbitrary", "arbitrary")),
        name="ffn",
    )(tab, xe, wg, wu, wd)


def _combine_kernel(tab_s, y_hbm, rowid_ref, gate_ref, x_ref, mod_ref,
                    modn_ref, g_ref, xo_ref, h_ref, weights, stack, acc, sem, *, nb, nsteps, final):
    g = pl.program_id(0)
    b = pl.program_id(1)
    step = g * nb + b
    slot = lax.rem(step, 2)
    seg_s, offo_s, offb_s, over_s = _tables(tab_s, nb)
    over = over_s[step]

    def fetch(st, sl):
        gg = lax.div(st, nb)
        for e in range(N_EXPERTS):
            k = st * N_EXPERTS + e
            pltpu.make_async_copy(
                y_hbm.at[gg, e, pl.ds(pl.multiple_of(offb_s[k], SEG), WINR)],
                stack.at[sl, e * WINR:(e + 1) * WINR], sem.at[sl]).start()

        @pl.when(over_s[st] > 0)
        def _():
            for e in range(N_EXPERTS):
                k = st * N_EXPERTS + e

                def one(i, carry, e=e, k=k):
                    pltpu.make_async_copy(
                        y_hbm.at[gg, e, pl.ds(pl.multiple_of(offb_s[k] + WINR + i * SEG, SEG), SEG)],
                        stack.at[sl, pl.ds(pl.multiple_of(offo_s[k] + i * SEG, SEG), SEG)],
                        sem.at[sl]).start()
                    return carry

                lax.fori_loop(0, _overflow_groups(seg_s, k), one, 0)

    @pl.when(step == 0)
    def _first():
        stack[...] = jnp.zeros_like(stack)
        fetch(step, slot)

    if nsteps > 1:
        @pl.when(step + 1 < nsteps)
        def _prefetch():
            fetch(step + 1, 1 - slot)

    _window_rows(rowid_ref, weights, gate_ref)

    @pl.when(over > 0)
    def _():
        _overflow_rows(seg_s, offo_s, step, over, rowid_ref, weights, gate_ref)

    _wait_rows(FIRST_ROWS + over, lambda n: pltpu.make_async_copy(
        y_hbm.at[0, 0, pl.ds(0, n)], stack.at[slot, pl.ds(0, n)], sem.at[slot]))

    def token_weights(base):
        return weights[pl.ds(base, TB), :].T.astype(BF16)

    w = jnp.concatenate([token_weights(c * TB) for c in range(FIRST_ROWS // TB)], axis=1)
    acc[...] = _dot(w, stack[slot, 0:FIRST_ROWS, :])

    @pl.when(over > 0)
    def _():
        def chunk(c, carry):
            base = pl.multiple_of(FIRST_ROWS + c * TB, TB)
            acc[...] += _dot(token_weights(base), stack[slot, pl.ds(base, TB), :])
            return carry

        lax.fori_loop(0, (over + TB - 1) // TB, chunk, 0)

    x = x_ref[...] + mod_ref[5] * acc[...]
    if final:
        y = _rms(x, g_ref[...])

        @pl.when(g == 0)
        def _():
            xo_ref[...] = y

        @pl.when(g != 0)
        def _():
            h_ref[...] = y
    else:
        xo_ref[...] = x
        h_ref[...] = _modulate(x, g_ref[...], modn_ref[0], modn_ref[1]).astype(BF16)


def _combine_call(tab, y, rowid, gate, x, mods, layer, nxt, gains, gidx, modrow, ngroups, ntok, final):
    nb = ntok // TB
    t = x.shape[0]
    if final:
        assert ngroups == 2
        out_specs = [pl.BlockSpec((TB, D), lambda g, b, *_: (jnp.where(g == 0, b, nb - 1), 0)),
                     pl.BlockSpec((TB, D), lambda g, b, *_: (jnp.where(g == 0, 0, b), 0))]
        out_shape = [jax.ShapeDtypeStruct((ntok, D), F32), jax.ShapeDtypeStruct((ntok, D), F32)]
    else:
        out_specs = [pl.BlockSpec((TB, D), lambda g, b, *_: (g * nb + b, 0)),
                     pl.BlockSpec((TB, D), lambda g, b, *_: (g * nb + b, 0))]
        out_shape = [jax.ShapeDtypeStruct((t, D), F32), jax.ShapeDtypeStruct((t, D), BF16)]
    grid_spec = pltpu.PrefetchScalarGridSpec(
        num_scalar_prefetch=1,
        grid=(ngroups, nb),
        in_specs=[pl.BlockSpec(memory_space=pl.ANY),
                  pl.BlockSpec((None, None, N_EXPERTS, TB), lambda g, b, *_: (g, b, 0, 0)),
                  pl.BlockSpec((None, None, N_EXPERTS, TB), lambda g, b, *_: (g, b, 0, 0)),
                  pl.BlockSpec((TB, D), lambda g, b, *_: (g * nb + b, 0)),
                  pl.BlockSpec((None, None, 6, 1, D),
                               lambda g, b, *_: (layer, modrow(g * nb + b), 0, 0, 0)),
                  pl.BlockSpec((None, None, 6, 1, D),
                               lambda g, b, *_: (nxt, modrow(g * nb + b), 0, 0, 0)),
                  pl.BlockSpec((None, 1, D), lambda g, b, *_: (gidx, 0, 0))],
        out_specs=out_specs,
        scratch_shapes=[pltpu.VMEM((STACK_ROWS, TB), F32), pltpu.VMEM((2, STACK_ROWS, D), BF16),
                        pltpu.VMEM((TB, D), F32), pltpu.SemaphoreType.DMA((2,))])
    return pl.pallas_call(
        functools.partial(_combine_kernel, nb=nb, nsteps=ngroups * nb, final=final),
        grid_spec=grid_spec,
        out_shape=out_shape,
        compiler_params=_cparams(("arbitrary", "arbitrary")),
        name="combine",
    )(tab, y, rowid, gate, x, mods, mods, gains)


def kernel(x_prompt, x_sample, cache_diff_k, cache_diff_v, cache_swa_k, cache_swa_v, cache_mla_ckv, cache_mla_krope, c, c_ctx, w_ada, b_ada, norm_mix, norm_ffn, w_in_even, w_out_even, diff_lambda, diff_subln, swa_sink, w_in_odd, mla_q_norm, w_q_up, mla_kv_norm, w_kv_up, w_out_odd, w_router, w_gate_exp, w_up_exp, w_down_exp, final_norm):
    batch, seq, _ = x_prompt.shape
    dec_batch, dec_seq, _ = x_sample.shape
    past = cache_diff_k.shape[2]
    depth = w_ada.shape[0]
    n_even = w_in_even.shape[0]
    n_odd = w_in_odd.shape[0]
    nc, ns = batch * seq, dec_batch * dec_seq
    assert nc == ns, "the routed-expert kernels take two token groups of equal size"
    assert nc % TM == 0 and dec_seq % TM == 0 and past % 256 == 0 and dec_seq % GRID_W == 0
    assert seq % TQ == 0 and dec_seq % TQ_LATENT == 0
    ntok = nc

    def modrow_of(tile):
        def modrow(r):
            tok = r * tile
            return jnp.where(tok < nc, 0, 1 + jnp.maximum(tok - nc, 0) // dec_seq)
        return modrow

    modrow, modrow_tb = modrow_of(TM), modrow_of(TB)

    rc = -(-(1 + dec_batch) // 16) * 16
    cvec = jnp.zeros((rc, D), F32).at[0].set(c_ctx).at[1:1 + dec_batch].set(c)
    mods = _ada_call(cvec, w_ada, b_ada).reshape(depth, rc, 6, 1, D)

    w_even_b = w_in_even.astype(BF16)
    w_oute_b = w_out_even.astype(BF16)
    w_outo_b = w_out_odd.astype(BF16)
    kr_pad = jnp.zeros((n_odd, D, LANES), F32).at[:, :, 64:96].set(w_in_odd[:, :, 640:672])
    w_odd_b = jnp.concatenate([w_in_odd[:, :, :640], kr_pad], axis=-1).astype(BF16)
    wq = w_q_up.reshape(n_odd, MLA_Q_RANK, MLA_HEADS, HD + MLA_ROPE)
    wq_b = jnp.pad(wq, ((0, 0), (0, 0), (0, 0), (0, LANES - HD - MLA_ROPE))).reshape(
        n_odd, MLA_Q_RANK, MLA_QW).astype(BF16)
    wkv = w_kv_up.reshape(n_odd, MLA_KV_RANK, MLA_HEADS, 2 * HD)
    wk_b = jnp.pad(wkv[..., :HD], ((0, 0), (0, 0), (0, 0), (0, LANES - HD))).reshape(
        n_odd, MLA_KV_RANK, MLA_QW).astype(BF16)
    wv_b = wkv[..., HD:].reshape(n_odd, MLA_KV_RANK, D).astype(BF16)
    w_router_b = jnp.pad(w_router, ((0, 0), (0, 0), (0, LANES - N_EXPERTS))).astype(BF16)
    even_tabs = _rope_tables(dec_seq, 16, _even_lane)
    mla_tabs = _rope_tables(dec_seq, 8, _mla_lane)
    ckr = jnp.zeros((dec_batch, n_odd, past, LANES), F32).at[..., 64:96].set(cache_mla_krope)
    even_caches = (cache_diff_k, cache_diff_v, cache_swa_k, cache_swa_v)

    mix_gains = jnp.concatenate([norm_mix, final_norm[None]], axis=0).reshape(depth + 1, 1, D)
    ffn_gains = norm_ffn.reshape(depth, 1, D)
    q_gains = mla_q_norm.reshape(n_odd, 1, MLA_Q_RANK)
    kv_gains = mla_kv_norm.reshape(n_odd, 1, MLA_KV_RANK)
    sublns = diff_subln.reshape(n_even, 1, 2 * HD)

    xa, xb, xb_off = x_prompt.reshape(nc, D), x_sample.reshape(ns, D), 0
    h = _norm_mod_call(xa, xb, mods, 0, mix_gains, modrow)
    even_state = odd_state = None
    y_prompt = y_sample = None
    for i in range(depth):
        j = i // 2
        if i % 2 == 0:
            qkv_c, *even_state = _proj_even_call(h, w_even_b, j, 0, nc, None, dec_seq,
                                                 prev=even_state, seq=seq)
            (qkv_l,) = _proj_even_call(h, w_even_b, j, nc, ns, even_tabs, dec_seq)
            li = _lambda_init(i)
            o_c = _attn_even_call(qkv_c, None, j, diff_lambda, sublns, swa_sink, batch, seq, 0, li)
            o_l = _attn_even_call(qkv_l, even_caches, j, diff_lambda, sublns, swa_sink,
                                  dec_batch, dec_seq, past, li)
            w_out = w_oute_b
        else:
            q_c, *odd_state = _proj_odd_call(h, w_odd_b, q_gains, wq_b, kv_gains, j, 0, nc, None,
                                             dec_seq, prev=odd_state, seq=seq)
            q_l, ckv_l, kr_l = _proj_odd_call(h, w_odd_b, q_gains, wq_b, kv_gains, j, nc, ns,
                                              mla_tabs, dec_seq)
            o_c = _attn_odd_call(q_c, odd_state[0], odd_state[1], None, j, wk_b, wv_b, batch, seq, 0)
            o_l = _attn_odd_call(q_l, ckv_l, kr_l, (cache_mla_ckv, ckr), j, wk_b, wv_b,
                                 dec_batch, dec_seq, past)
            w_out = w_outo_b
        x, h2, aff = _outproj_call(xa, xb, xb_off, o_c, o_l, w_out, j, mods, i, ffn_gains,
                                   w_router_b, modrow)
        rowid, gate, tab = _select_call(aff, 2, ntok)
        xe = _dispatch_call(tab, h2, rowid, 2, ntok)
        y = _ffn_call(tab, xe, w_gate_exp, w_up_exp, w_down_exp, i, 2, ntok)
        final = i == depth - 1
        nxt = i if final else i + 1
        out_a, out_b = _combine_call(tab, y, rowid, gate, x, mods, i, nxt, mix_gains,
                                     depth if final else nxt, modrow_tb, 2, ntok, final)
        if final:
            y_prompt = out_a.reshape(batch, seq, D)
            y_sample = out_b.reshape(dec_batch, dec_seq, D)
        else:
            x, h = out_a, out_b
            xa, xb, xb_off = x, x, nc // TM

    kd, vd, ks, vs = even_state
    ckv_new, kr_new = odd_state
    return (y_prompt, y_sample,
            kd.reshape(batch, n_even, seq, DIFF_HEADS, 2 * HD),
            vd.reshape(batch, n_even, seq, DIFF_HEADS, 2 * HD),
            ks.reshape(batch, n_even, seq, 2, HD), vs.reshape(batch, n_even, seq, 2, HD),
            ckv_new, kr_new[..., 64:96])
```

```python
import functools
import math

import jax
import jax.numpy as jnp
import numpy as np
from jax import lax
from jax.experimental import pallas as pl
from jax.experimental.pallas import tpu as pltpu

F32 = jnp.float32
BF16 = jnp.bfloat16
I32 = jnp.int32

D = 1024
HD = 64
GRID_W = 64
WINDOW = 128
DIFF_HEADS = 4
SWA_HEADS = 8
MLA_HEADS = 16
MLA_Q_RANK = 384
MLA_KV_RANK = 256
MLA_ROPE = 32
N_EXPERTS = 16
EXPERT_FF = 512
EC_FACTOR = 2
ROPE_BASE = 10000.0
EPS = 1e-6
NEG_INF = -1e30
LOG2E = math.log2(math.e)
EVEN_IN = 2304
LANES = 128
TM = 512
TQ = 256
TQ_LATENT = 256
TB = 256
SEG = 16
VMEM_LIMIT = 56 * 1024 * 1024


def _cparams(sem, vmem=VMEM_LIMIT):
    return pltpu.CompilerParams(dimension_semantics=sem, vmem_limit_bytes=vmem)


def _dot(a, b):
    return jnp.dot(a, b, preferred_element_type=F32)


def _dot_nt(a, b):
    return lax.dot_general(a, b, (((1,), (1,)), ((), ())), preferred_element_type=F32)


def _silu(x):
    return x / (1.0 + jnp.exp(-x))


def _rms(x, g):
    ms = jnp.mean(x * x, axis=-1, keepdims=True)
    return x * lax.rsqrt(ms + EPS) * g


def _modulate(x, g, shift, scale):
    return _rms(x, g) * (1.0 + scale) + shift


def _lambda_init(layer):
    return 0.8 - 0.6 * math.exp(-0.3 * layer)


def _ada_kernel(c_ref, w_ref, b_ref, o_ref):
    s = _silu(c_ref[...]).astype(BF16)
    o_ref[...] = _dot(s, w_ref[...].astype(BF16)) + b_ref[...]


def _ada_call(cvec, w_ada, b_ada):
    depth, _, n6 = w_ada.shape
    rc = cvec.shape[0]
    tn = 512
    return pl.pallas_call(
        _ada_kernel,
        grid=(depth, n6 // tn),
        in_specs=[pl.BlockSpec((rc, D), lambda i, n: (0, 0)),
                  pl.BlockSpec((None, D, tn), lambda i, n: (i, 0, n)),
                  pl.BlockSpec((None, 1, tn), lambda i, n: (i, 0, n))],
        out_specs=pl.BlockSpec((None, rc, tn), lambda i, n: (i, 0, n)),
        out_shape=jax.ShapeDtypeStruct((depth, rc, n6), F32),
        compiler_params=_cparams(("parallel", "parallel")),
        name="ada",
    )(cvec, w_ada, b_ada.reshape(depth, 1, n6))


def _two_part_specs(rows_a, rows_b, width, off_b=0):
    na, nb_ = rows_a // TM, rows_b // TM
    return (pl.BlockSpec((TM, width), lambda r: (jnp.minimum(r, na - 1), 0)),
            pl.BlockSpec((TM, width), lambda r: (off_b + jnp.clip(r - na, 0, nb_ - 1), 0)))


def _norm_mod_kernel(xa_ref, xb_ref, mod_ref, g_ref, h_ref, *, ntile_a):
    def run(x_ref):
        h_ref[...] = _modulate(x_ref[...], g_ref[...], mod_ref[0], mod_ref[1]).astype(BF16)

    @pl.when(pl.program_id(0) < ntile_a)
    def _():
        run(xa_ref)

    @pl.when(pl.program_id(0) >= ntile_a)
    def _():
        run(xb_ref)


def _norm_mod_call(xa, xb, mods, layer, gains, modrow):
    na, nb_ = xa.shape[0], xb.shape[0]
    return pl.pallas_call(
        functools.partial(_norm_mod_kernel, ntile_a=na // TM),
        grid=((na + nb_) // TM,),
        in_specs=[*_two_part_specs(na, nb_, D),
                  pl.BlockSpec((None, None, 6, 1, D), lambda r: (layer, modrow(r), 0, 0, 0)),
                  pl.BlockSpec((None, 1, D), lambda r: (layer, 0, 0))],
        out_specs=pl.BlockSpec((TM, D), lambda r: (r, 0)),
        out_shape=jax.ShapeDtypeStruct((na + nb_, D), BF16),
        compiler_params=_cparams(("parallel",)),
        name="norm_mod",
    )(xa, xb, mods, gains)


def _rope_tables(dec_seq, half, lane_of_dim):
    pos = jnp.arange(dec_seq)
    row = (pos // GRID_W).astype(F32)
    col = (pos % GRID_W).astype(F32)
    inv = ROPE_BASE ** (-(jnp.arange(half, dtype=F32) / half))
    ang = jnp.stack([row[:, None] * inv[None, :], col[:, None] * inv[None, :]])
    info = [lane_of_dim(lane) for lane in range(LANES)]
    axis = np.array([0 if i is None else i[0] for i in info])
    freq = np.array([0 if i is None else i[1] for i in info])
    first = np.array([i is not None and not i[2] for i in info])[None, :]
    second = np.array([i is not None and i[2] for i in info])[None, :]
    lane_ang = ang[axis, :, freq].T
    cos, sin = jnp.cos(lane_ang), jnp.sin(lane_ang)
    return (jnp.where(first | second, cos, 1.0), jnp.where(first, -sin, 0.0),
            jnp.where(second, sin, 0.0))


def _even_lane(lane):
    j = lane % HD
    axis, jj = j // 32, j % 32
    return axis, jj % 16, jj >= 16


def _mla_lane(lane):
    if lane < 64 or lane >= 96:
        return None
    jj = lane - 64
    axis, k = jj // 16, jj % 16
    return axis, k % 8, k >= 8


def _rope(x, c, s1, s2, shift):
    return x * c + pltpu.roll(x, LANES - shift, 1) * s1 + pltpu.roll(x, shift, 1) * s2


_EVEN_ROPE_TILES = tuple(range(0, 8)) + tuple(range(12, 17))
_EVEN_Q_TILES = tuple(range(0, 4)) + tuple(range(12, 16))


def _append_layer(prev_refs, out_refs, new_values):
    for i, (out_ref, new) in enumerate(zip(out_refs, new_values)):
        nbatch, nlayers, seq, width = out_ref.shape
        if prev_refs:
            out_ref[:, 0:nlayers - 1] = prev_refs[i][...]
        out_ref[:, nlayers - 1] = new.reshape(nbatch, seq, width)


def _state_specs(prev, widths, seq):
    nlayers = 1 if prev is None else prev[0].shape[1] + 1
    per = TM // seq
    ins = [] if prev is None else [pl.BlockSpec((per, nlayers - 1, seq, w), lambda r: (r, 0, 0, 0))
                                   for w in widths]
    outs = [pl.BlockSpec((per, nlayers, seq, w), lambda r: (r, 0, 0, 0)) for w in widths]
    return ins, outs, nlayers


def _proj_even_kernel(*refs, rope, caches, nprev):
    h_ref, w_ref = refs[0], refs[1]
    pos = 2
    if rope:
        c_ref, s1_ref, s2_ref = refs[2:5]
        pos = 5
    prev_refs = refs[pos:pos + nprev]
    pos += nprev
    qkv_ref = refs[pos]
    res = _dot(h_ref[...], w_ref[...])
    scale = HD ** -0.5 * LOG2E
    for t in range(EVEN_IN // LANES):
        x = res[:, t * LANES:(t + 1) * LANES]
        if rope and t in _EVEN_ROPE_TILES:
            x = _rope(x, c_ref[...], s1_ref[...], s2_ref[...], 16)
        if t in _EVEN_Q_TILES:
            x = x * scale
        qkv_ref[:, t * LANES:(t + 1) * LANES] = x.astype(BF16)
    if caches:
        _append_layer(prev_refs, refs[pos + 1:pos + 5],
                      (res[:, 512:1024], res[:, 1024:1536], res[:, 2048:2176], res[:, 2176:2304]))


def _proj_even_call(h, w, j, row0, nrows, tables, dec_seq, prev=None, seq=None):
    rope = tables is not None
    caches = not rope
    t0 = row0 // TM
    in_specs = [pl.BlockSpec((TM, D), lambda r: (t0 + r, 0)),
                pl.BlockSpec((None, D, EVEN_IN), lambda r: (j, 0, 0))]
    args = [h, w]
    if rope:
        per = dec_seq // TM
        for _ in range(3):
            in_specs.append(pl.BlockSpec((TM, LANES), lambda r: (r % per, 0)))
        args += list(tables)
    out_specs = [pl.BlockSpec((TM, EVEN_IN), lambda r: (r, 0))]
    out_shape = [jax.ShapeDtypeStruct((nrows, EVEN_IN), BF16)]
    nprev = 0
    if caches:
        widths = (512, 512, LANES, LANES)
        ins, outs, nlayers = _state_specs(prev, widths, seq)
        nprev = len(ins)
        in_specs += ins
        args += [] if prev is None else list(prev)
        out_specs += outs
        out_shape += [jax.ShapeDtypeStruct((nrows // seq, nlayers, seq, wd), F32) for wd in widths]
    return pl.pallas_call(
        functools.partial(_proj_even_kernel, rope=rope, caches=caches, nprev=nprev),
        grid=(nrows // TM,),
        in_specs=in_specs, out_specs=out_specs, out_shape=out_shape,
        compiler_params=_cparams(("parallel",)),
        name="proj_even_lat" if rope else "proj_even_ctx",
    )(*args)


ODD_IN_PAD = MLA_Q_RANK + MLA_KV_RANK + LANES
MLA_QW = MLA_HEADS * LANES


def _proj_odd_kernel(*refs, rope, nprev):
    h_ref, w_ref, qn_ref, wq_ref, kvn_ref = refs[:5]
    pos = 5
    if rope:
        c_ref, s1_ref, s2_ref = refs[5:8]
        pos = 8
    prev_refs = refs[pos:pos + nprev]
    pos += nprev
    q_ref, ckv_ref, kr_ref = refs[pos:pos + 3]
    res = _dot(h_ref[...], w_ref[...])
    cq = _rms(res[:, :MLA_Q_RANK], qn_ref[...]).astype(BF16)
    ckv = _rms(res[:, MLA_Q_RANK:MLA_Q_RANK + MLA_KV_RANK], kvn_ref[...])
    kr = res[:, MLA_Q_RANK + MLA_KV_RANK:]
    if rope:
        kr = _rope(kr, c_ref[...], s1_ref[...], s2_ref[...], 8)
        ckv_ref[...] = ckv
        kr_ref[...] = kr
    else:
        _append_layer(prev_refs, (ckv_ref, kr_ref), (ckv, kr))
    q = _dot(cq, wq_ref[...])
    scale = (HD + MLA_ROPE) ** -0.5 * LOG2E
    for t in range(MLA_HEADS):
        x = q[:, t * LANES:(t + 1) * LANES]
        if rope:
            x = _rope(x, c_ref[...], s1_ref[...], s2_ref[...], 8)
        q_ref[:, t * LANES:(t + 1) * LANES] = (x * scale).astype(BF16)


def _proj_odd_call(h, w_in, qn, wq, kvn, j, row0, nrows, tables, dec_seq, prev=None, seq=None):
    rope = tables is not None
    t0 = row0 // TM
    in_specs = [pl.BlockSpec((TM, D), lambda r: (t0 + r, 0)),
                pl.BlockSpec((None, D, ODD_IN_PAD), lambda r: (j, 0, 0)),
                pl.BlockSpec((None, 1, MLA_Q_RANK), lambda r: (j, 0, 0)),
                pl.BlockSpec((None, MLA_Q_RANK, MLA_QW), lambda r: (j, 0, 0)),
                pl.BlockSpec((None, 1, MLA_KV_RANK), lambda r: (j, 0, 0))]
    args = [h, w_in, qn, wq, kvn]
    if rope:
        per = dec_seq // TM
        for _ in range(3):
            in_specs.append(pl.BlockSpec((TM, LANES), lambda r: (r % per, 0)))
        args += list(tables)
    out_specs = [pl.BlockSpec((TM, MLA_QW), lambda r: (r, 0))]
    out_shape = [jax.ShapeDtypeStruct((nrows, MLA_QW), BF16)]
    widths = (MLA_KV_RANK, LANES)
    nprev = 0
    if rope:
        out_specs += [pl.BlockSpec((TM, wd), lambda r: (r, 0)) for wd in widths]
        out_shape += [jax.ShapeDtypeStruct((nrows, wd), F32) for wd in widths]
    else:
        ins, outs, nlayers = _state_specs(prev, widths, seq)
        nprev = len(ins)
        in_specs += ins
        args += [] if prev is None else list(prev)
        out_specs += outs
        out_shape += [jax.ShapeDtypeStruct((nrows // seq, nlayers, seq, wd), F32) for wd in widths]
    return pl.pallas_call(
        functools.partial(_proj_odd_kernel, rope=rope, nprev=nprev),
        grid=(nrows // TM,),
        in_specs=in_specs,
        out_specs=out_specs,
        out_shape=out_shape,
        compiler_params=_cparams(("parallel",)),
        name="proj_odd_lat" if rope else "proj_odd_ctx",
    )(*args)


def _query_tile(past):
    return TQ_LATENT if past > 0 else TQ


def _ones_lane(half):
    return HD if half == 0 else 0


def _row_sum(e, o, half, from_matmul):
    if from_matmul:
        one = _ones_lane(half)
        return o[:, one:one + 1]
    return jnp.sum(e, axis=-1, keepdims=True)


def _half_values(v, half):
    lane = lax.broadcasted_iota(I32, (1, LANES), 1)
    keep = (lane < HD) if half == 0 else (lane >= HD)
    return jnp.where(keep, v, jnp.where(lane == _ones_lane(half), 1.0, 0.0)).astype(BF16)


VDT_ROWS = 2 * HD + SEG


def _attn_even_kernel(*refs, seq, past, lam_init, layer):
    latent = past > 0
    tq = _query_tile(past)
    n = past + seq
    if latent:
        (qkv_ref, ck_ref, cv_ref, sk_ref, sv_ref, lam_ref, subln_ref, sink_ref,
         o_ref, kd, vd, ka, vl, vh) = refs
    else:
        qkv_ref, lam_ref, subln_ref, sink_ref, o_ref, kd, vd, ka, vl, vh = refs
    qi = pl.program_id(1)
    lo = lax.broadcasted_iota(I32, (1, LANES), 1) < HD

    @pl.when(qi == 0)
    def _build():
        chunk = 256
        for c0 in range(0, n, chunk):
            rows = slice(c0, c0 + chunk)
            if c0 < past:
                prow = slice(c0, c0 + chunk)
                for h in range(DIFF_HEADS):
                    kd[rows, h * LANES:(h + 1) * LANES] = ck_ref[prow, h, :].astype(BF16)
                    vd[h * VDT_ROWS:h * VDT_ROWS + LANES, rows] = cv_ref[prow, h, :].T.astype(BF16)
                kt = jnp.concatenate([sk_ref[prow, 0, :], sk_ref[prow, 1, :]], axis=1)
                vt = jnp.concatenate([sv_ref[prow, 0, :], sv_ref[prow, 1, :]], axis=1)
            else:
                orow = slice(c0 - past, c0 - past + chunk)
                kd[rows, :] = qkv_ref[orow, 512:1024]
                for h in range(DIFF_HEADS):
                    vown = qkv_ref[orow, 1024 + h * LANES:1024 + (h + 1) * LANES].astype(F32)
                    vd[h * VDT_ROWS:h * VDT_ROWS + LANES, rows] = vown.T.astype(BF16)
                kt = qkv_ref[orow, 2048:2176].astype(F32)
                vt = qkv_ref[orow, 2176:2304].astype(F32)
            kr = pltpu.roll(kt, HD, 1)
            vr = pltpu.roll(vt, HD, 1)
            ka[0, rows, :] = jnp.where(lo, kt, kr).astype(BF16)
            ka[1, rows, :] = jnp.where(lo, kr, kt).astype(BF16)
            vl[0, rows, :] = _half_values(vt, 0)
            vh[0, rows, :] = _half_values(vr, 1)
            vl[1, rows, :] = _half_values(vr, 0)
            vh[1, rows, :] = _half_values(vt, 1)
        for h in range(DIFF_HEADS):
            vd[h * VDT_ROWS + LANES:(h + 1) * VDT_ROWS, :] = jnp.ones((SEG, n), BF16)

    r0 = pl.multiple_of(qi * tq, tq)
    lam = lam_ref[...]
    lam_full = (jnp.exp(jnp.sum(lam[0:1] * lam[1:2], axis=-1, keepdims=True))
                - jnp.exp(jnp.sum(lam[2:3] * lam[3:4], axis=-1, keepdims=True)) + lam_init)
    zero_b = jnp.zeros((), BF16)

    for h in range(DIFF_HEADS):
        cs = slice(h * LANES, (h + 1) * LANES)
        qt = qkv_ref[pl.ds(r0, tq), cs]
        kh = kd[:, cs]
        ots = []
        for comp in range(2):
            qc = jnp.where(lo, qt, zero_b) if comp == 0 else jnp.where(lo, zero_b, qt)
            s = _dot_nt(qc, kh)
            m = jnp.max(s, axis=-1, keepdims=True)
            e = jnp.exp2(s - m).astype(BF16)
            ots.append(_dot_nt(vd[h * VDT_ROWS:(h + 1) * VDT_ROWS, :], e))
        ot = (ots[0][0:LANES, :] * (1.0 / ots[0][LANES:LANES + 1, :])
              - ots[1][0:LANES, :] * (lam_full / ots[1][LANES:LANES + 1, :]))
        o = _rms(ot.T, subln_ref[...]) * (1.0 - lam_init)
        o_ref[:, cs] = o.astype(BF16)

    nblk = seq // WINDOW
    dense = past if latent else seq
    if latent:
        per = tq // WINDOW
        offsets = tuple(range(-1, per + 1))
        rr = lax.broadcasted_iota(I32, (tq, WINDOW), 0)
        cc = lax.broadcasted_iota(I32, (tq, WINDOW), 1)
        band, starts = {}, {}
        for d in offsets:
            blk = qi * per + d
            inside = jnp.logical_and(blk >= 0, blk < nblk)
            band[d] = jnp.logical_and(jnp.abs(rr - cc - d * WINDOW) <= WINDOW, inside)
            starts[d] = pl.multiple_of(past + jnp.clip(blk, 0, nblk - 1) * WINDOW, WINDOW)
    for i in range(SWA_HEADS // 2):
        hk = i // 2
        cs = slice(1536 + i * LANES, 1536 + (i + 1) * LANES)
        qt = qkv_ref[pl.ds(r0, tq), cs]
        halves = []
        for half in range(2):
            qc = jnp.where(lo, qt, zero_b) if half == 0 else jnp.where(lo, zero_b, qt)
            vsel = vl if half == 0 else vh
            sink = sink_ref[layer, 2 * i + half] * LOG2E
            parts = [_dot_nt(qc, ka[hk, 0:dense, :])]
            if latent:
                for d in offsets:
                    s = _dot_nt(qc, ka[hk, pl.ds(starts[d], WINDOW), :])
                    parts.append(jnp.where(band[d], s, NEG_INF))
            s_all = jnp.concatenate(parts, axis=1) if len(parts) > 1 else parts[0]
            m = jnp.maximum(jnp.max(s_all, axis=-1, keepdims=True), sink)
            e = jnp.exp2(s_all - m)
            eb = e.astype(BF16)
            o = _dot(eb[:, 0:dense], vsel[hk, 0:dense, :])
            if latent:
                for k, d in enumerate(offsets):
                    o += _dot(eb[:, dense + k * WINDOW:dense + (k + 1) * WINDOW],
                              vsel[hk, pl.ds(starts[d], WINDOW), :])
            den = _row_sum(e, o, half, from_matmul=latent) + jnp.exp2(sink - m)
            halves.append(o * (1.0 / den))
        o_ref[:, 512 + i * LANES:512 + (i + 1) * LANES] = jnp.where(lo, halves[0], halves[1]).astype(BF16)


def _attn_even_call(qkv, caches, j, lam, subln, sink, nbatch, seq, past, lam_init):
    n = past + seq
    latent = past > 0
    tq = _query_tile(past)
    in_specs = [pl.BlockSpec((seq, EVEN_IN), lambda b, q: (b, 0))]
    args = [qkv]
    if latent:
        ck, cv, sk, sv = caches
        in_specs += [pl.BlockSpec((None, None, past, DIFF_HEADS, 2 * HD), lambda b, q: (b, j, 0, 0, 0)),
                     pl.BlockSpec((None, None, past, DIFF_HEADS, 2 * HD), lambda b, q: (b, j, 0, 0, 0)),
                     pl.BlockSpec((None, None, past, 2, HD), lambda b, q: (b, j, 0, 0, 0)),
                     pl.BlockSpec((None, None, past, 2, HD), lambda b, q: (b, j, 0, 0, 0))]
        args += [ck, cv, sk, sv]
    in_specs += [pl.BlockSpec((None, 4, HD), lambda b, q: (j, 0, 0)),
                 pl.BlockSpec((None, 1, 2 * HD), lambda b, q: (j, 0, 0)),
                 pl.BlockSpec(memory_space=pltpu.SMEM)]
    args += [lam, subln, sink]
    return pl.pallas_call(
        functools.partial(_attn_even_kernel, seq=seq, past=past, lam_init=lam_init, layer=j),
        grid=(nbatch, seq // tq),
        in_specs=in_specs,
        out_specs=pl.BlockSpec((tq, D), lambda b, q: (b * (seq // tq) + q, 0)),
        out_shape=jax.ShapeDtypeStruct((nbatch * seq, D), BF16),
        scratch_shapes=[pltpu.VMEM((n, 512), BF16),
                        pltpu.VMEM((DIFF_HEADS * VDT_ROWS, n), BF16),
                        pltpu.VMEM((2, n, LANES), BF16), pltpu.VMEM((2, n, LANES), BF16),
                        pltpu.VMEM((2, n, LANES), BF16)],
        compiler_params=_cparams(("arbitrary", "arbitrary")),
        name="attn_even_lat" if latent else "attn_even_ctx",
    )(*args)


VT_ROWS = HD + SEG


def _attn_odd_kernel(*refs, seq, past):
    latent = past > 0
    tq = _query_tile(past)
    n = past + seq
    if latent:
        q_ref, ckv_ref, kr_ref, cckv_ref, ckr_ref, wk_ref, wv_ref, o_ref, kf, vt = refs
    else:
        q_ref, ckv_ref, kr_ref, wk_ref, wv_ref, o_ref, kf, vlo, vhi = refs
    qi = pl.program_id(1)

    @pl.when(qi == 0)
    def _build():
        chunk = 256
        for c0 in range(0, n, chunk):
            rows = slice(c0, c0 + chunk)
            if c0 < past:
                ckv = cckv_ref[c0:c0 + chunk, :].astype(BF16)
                kr = ckr_ref[c0:c0 + chunk, :]
            else:
                ckv = ckv_ref[c0 - past:c0 - past + chunk, :].astype(BF16)
                kr = kr_ref[c0 - past:c0 - past + chunk, :]
            kk = _dot(ckv, wk_ref[...])
            for h in range(MLA_HEADS):
                cs = slice(h * LANES, (h + 1) * LANES)
                kf[rows, cs] = (kk[:, cs] + kr).astype(BF16)
            vv = _dot(ckv, wv_ref[...])
            for i in range(MLA_HEADS // 2):
                cs = slice(i * LANES, (i + 1) * LANES)
                if latent:
                    pair = vv[:, cs].T.astype(BF16)
                    for half in range(2):
                        r = (2 * i + half) * VT_ROWS
                        vt[r:r + HD, rows] = pair[half * HD:(half + 1) * HD, :]
                else:
                    vlo[rows, cs] = _half_values(vv[:, cs], 0)
                    vhi[rows, cs] = _half_values(vv[:, cs], 1)
        if latent:
            for h in range(MLA_HEADS):
                vt[h * VT_ROWS + HD:(h + 1) * VT_ROWS, :] = jnp.ones((SEG, n), BF16)

    r0 = pl.multiple_of(qi * tq, tq)
    lo = lax.broadcasted_iota(I32, (1, LANES), 1) < HD
    for i in range(MLA_HEADS // 2):
        halves = []
        for half in range(2):
            h = 2 * i + half
            cs = slice(h * LANES, (h + 1) * LANES)
            s = _dot_nt(q_ref[pl.ds(r0, tq), cs], kf[:, cs])
            m = jnp.max(s, axis=-1, keepdims=True)
            e = jnp.exp2(s - m)
            if latent:
                ot = _dot_nt(vt[h * VT_ROWS:(h + 1) * VT_ROWS, :], e.astype(BF16))
                halves.append(ot[0:HD, :] * (1.0 / ot[HD:HD + 1, :]))
            else:
                vsel = vlo if half == 0 else vhi
                o = _dot(e.astype(BF16), vsel[:, i * LANES:(i + 1) * LANES])
                halves.append(o * (1.0 / jnp.sum(e, axis=-1, keepdims=True)))
        if latent:
            tile = jnp.concatenate(halves, axis=0).T
        else:
            tile = jnp.where(lo, halves[0], halves[1])
        o_ref[:, i * LANES:(i + 1) * LANES] = tile.astype(BF16)


def _attn_odd_call(q, ckv, kr, caches, j, wk, wv, nbatch, seq, past):
    n = past + seq
    latent = past > 0
    tq = _query_tile(past)
    in_specs = [pl.BlockSpec((seq, MLA_QW), lambda b, qq: (b, 0))]
    if latent:
        in_specs += [pl.BlockSpec((seq, MLA_KV_RANK), lambda b, qq: (b, 0)),
                     pl.BlockSpec((seq, LANES), lambda b, qq: (b, 0))]
    else:
        last = ckv.shape[1] - 1
        in_specs += [pl.BlockSpec((None, None, seq, MLA_KV_RANK), lambda b, qq: (b, last, 0, 0)),
                     pl.BlockSpec((None, None, seq, LANES), lambda b, qq: (b, last, 0, 0))]
    args = [q, ckv, kr]
    if latent:
        in_specs += [pl.BlockSpec((None, None, past, MLA_KV_RANK), lambda b, qq: (b, j, 0, 0)),
                     pl.BlockSpec((None, None, past, LANES), lambda b, qq: (b, j, 0, 0))]
        args += list(caches)
    in_specs += [pl.BlockSpec((None, MLA_KV_RANK, MLA_QW), lambda b, qq: (j, 0, 0)),
                 pl.BlockSpec((None, MLA_KV_RANK, D), lambda b, qq: (j, 0, 0))]
    args += [wk, wv]
    return pl.pallas_call(
        functools.partial(_attn_odd_kernel, seq=seq, past=past),
        grid=(nbatch, seq // tq),
        in_specs=in_specs,
        out_specs=pl.BlockSpec((tq, D), lambda b, qq: (b * (seq // tq) + qq, 0)),
        out_shape=jax.ShapeDtypeStruct((nbatch * seq, D), BF16),
        scratch_shapes=([pltpu.VMEM((n, MLA_QW), BF16), pltpu.VMEM((MLA_HEADS * VT_ROWS, n), BF16)]
                        if latent else
                        [pltpu.VMEM((n, MLA_QW), BF16), pltpu.VMEM((n, D), BF16),
                         pltpu.VMEM((n, D), BF16)]),
        compiler_params=_cparams(("arbitrary", "arbitrary")),
        name="attn_odd_lat" if latent else "attn_odd_ctx",
    )(*args)


def _outproj_kernel(xa_ref, xb_ref, oa_ref, ob_ref, w_ref, mod_ref, g_ref, wr_ref,
                    xo_ref, h_ref, aff_ref, *, ntile_a):
    def run(x_ref, o_ref):
        x = x_ref[...] + mod_ref[2] * _dot(o_ref[...], w_ref[...])
        xo_ref[...] = x
        h = _modulate(x, g_ref[...], mod_ref[3], mod_ref[4]).astype(BF16)
        h_ref[...] = h
        logits = _dot(h, wr_ref[...])
        lane = lax.broadcasted_iota(I32, (TM, LANES), 1)
        lg = jnp.where(lane < N_EXPERTS, logits, -jnp.inf)
        e = jnp.exp(lg - jnp.max(lg, axis=-1, keepdims=True))
        aff = e / jnp.sum(e, axis=-1, keepdims=True)
        for c in range(TM // TB):
            aff_ref[c] = aff[c * TB:(c + 1) * TB, :].T[0:N_EXPERTS, :]

    @pl.when(pl.program_id(0) < ntile_a)
    def _():
        run(xa_ref, oa_ref)

    @pl.when(pl.program_id(0) >= ntile_a)
    def _():
        run(xb_ref, ob_ref)


def _outproj_call(xa, xb, xb_off, o_a, o_b, w, widx, mods, layer, gains, w_router, modrow):
    na, nb_ = o_a.shape[0], o_b.shape[0]
    t = na + nb_
    assert TM % TB == 0
    return pl.pallas_call(
        functools.partial(_outproj_kernel, ntile_a=na // TM),
        grid=(t // TM,),
        in_specs=[*_two_part_specs(na, nb_, D, xb_off), *_two_part_specs(na, nb_, D),
                  pl.BlockSpec((None, D, D), lambda r: (widx, 0, 0)),
                  pl.BlockSpec((None, None, 6, 1, D), lambda r: (layer, modrow(r), 0, 0, 0)),
                  pl.BlockSpec((None, 1, D), lambda r: (layer, 0, 0)),
                  pl.BlockSpec((None, D, LANES), lambda r: (layer, 0, 0))],
        out_specs=[pl.BlockSpec((TM, D), lambda r: (r, 0)),
                   pl.BlockSpec((TM, D), lambda r: (r, 0)),
                   pl.BlockSpec((TM // TB, N_EXPERTS, TB), lambda r: (r, 0, 0))],
        out_shape=[jax.ShapeDtypeStruct((t, D), F32), jax.ShapeDtypeStruct((t, D), BF16),
                   jax.ShapeDtypeStruct((t // TB, N_EXPERTS, TB), F32)],
        compiler_params=_cparams(("parallel",)),
        name="outproj",
    )(xa, xb, o_a, o_b, w, mods, gains, w_router)


def _select_kernel(aff_ref, rowid_ref, gate_ref, tab_ref, *, nb, cap):
    ne = N_EXPERTS
    nr = nb * ne
    a = aff_ref[...].reshape(nr, TB)
    ri = lax.broadcasted_iota(I32, (nr, nr), 0)
    ci = lax.broadcasted_iota(I32, (nr, nr), 1)
    same_e = (ri & (ne - 1)) == (ci & (ne - 1))
    same_b = (ri >> 4) == (ci >> 4)
    m_e = jnp.where(same_e, 1.0, 0.0).astype(BF16)
    m_b = jnp.where(same_b, 1.0, 0.0).astype(BF16)
    m_a = jnp.where(jnp.logical_and(same_e, ci < ri), 1.0, 0.0).astype(BF16)
    m_o = jnp.where(jnp.logical_and(same_b, ci < ri), 1.0, 0.0).astype(BF16)
    ui = lax.broadcasted_iota(I32, (TB, TB), 0)
    uj = lax.broadcasted_iota(I32, (TB, TB), 1)
    upper = jnp.where(ui < uj, 1.0, 0.0).astype(BF16)

    def rows_to_lanes(col):
        return jnp.broadcast_to(col, (nr, LANES)).astype(BF16)

    wide = jnp.concatenate([aff_ref[b] for b in range(nb)], axis=1)

    def count_ge(value):
        return jnp.sum(jnp.where(wide >= value, 1.0, 0.0), axis=-1, keepdims=True)

    def bisect(i, v):
        cand = v | jnp.left_shift(jnp.int32(1), 30 - i)
        return jnp.where(count_ge(pltpu.bitcast(cand, F32)) >= cap, cand, v)

    thr = lax.fori_loop(0, 31, bisect, jnp.zeros((ne, 1), I32))

    def refine(i, lohi):
        lo_e, hi_e = lohi
        mid = 0.5 * (lo_e + hi_e)
        take = count_ge(mid) >= cap
        return jnp.where(take, mid, lo_e), jnp.where(take, hi_e, mid)

    lo_e, hi_e = lax.fori_loop(0, TIE_STEPS, refine,
                               (pltpu.bitcast(thr, F32), pltpu.bitcast(thr + 1, F32)))
    lo_v = jnp.concatenate([lo_e] * nb, axis=0)
    hi_v = jnp.concatenate([hi_e] * nb, axis=0)
    gt = jnp.where(a >= hi_v, 1.0, 0.0)
    eq = jnp.where(jnp.logical_and(a >= lo_v, a < hi_v), 1.0, 0.0)
    n_gt = _dot(m_e, rows_to_lanes(jnp.sum(gt, axis=-1, keepdims=True)))[:, 0:1]
    need = cap - n_gt
    eq_before = (_dot(m_a, rows_to_lanes(jnp.sum(eq, axis=-1, keepdims=True)))[:, 0:1]
                 + _dot(eq.astype(BF16), upper))
    sel = jnp.where(jnp.logical_and(eq > 0.0, eq_before < need), 1.0, gt)
    local = _dot(sel.astype(BF16), upper)
    cnt = jnp.sum(sel, axis=-1, keepdims=True)
    seg = jnp.floor((cnt + (SEG - 1)) * (1.0 / SEG)) * SEG
    segb = rows_to_lanes(seg)
    over = jnp.maximum(seg - WINR, 0.0)
    overb = rows_to_lanes(over)
    off_over = FIRST_ROWS + _dot(m_o, overb)[:, 0:1]
    off_buf = _dot(m_a, segb)[:, 0:1]
    over_blk = _dot(m_b, overb)[:, 0:1]
    rows_exp = _dot(m_e, segb)[:, 0:1]
    tiles_exp = jnp.floor((rows_exp + (TF - 1)) / TF)
    expert = (lax.broadcasted_iota(I32, (nr, 1), 0) & (ne - 1)).astype(F32)
    row = jnp.where(local < WINR, expert * WINR + local, off_over + local - WINR)
    rowid_ref[...] = jnp.where(sel > 0.0, row, -1.0).astype(I32).reshape(nb, ne, TB)
    gate_ref[...] = jnp.where(sel > 0.0, a, 0.0).reshape(nb, ne, TB)
    tl = lax.broadcasted_iota(I32, (nr, LANES), 1)
    tab = jnp.where(tl == 0, seg, jnp.where(tl == 1, off_over, jnp.where(
        tl == 2, off_buf, jnp.where(tl == 3, over_blk, jnp.where(tl == 4, rows_exp, tiles_exp)))))
    tab_ref[...] = tab.T[0:8, :].astype(I32)


def _select_call(aff, ngroups, ntok):
    nb = ntok // TB
    cap = EC_FACTOR * ntok // N_EXPERTS
    nr = nb * N_EXPERTS
    return pl.pallas_call(
        functools.partial(_select_kernel, nb=nb, cap=cap),
        grid=(ngroups,),
        in_specs=[pl.BlockSpec((nb, N_EXPERTS, TB), lambda g: (g, 0, 0))],
        out_specs=[pl.BlockSpec((None, nb, N_EXPERTS, TB), lambda g: (g, 0, 0, 0)),
                   pl.BlockSpec((None, nb, N_EXPERTS, TB), lambda g: (g, 0, 0, 0)),
                   pl.BlockSpec((None, 8, nr), lambda g: (g, 0, 0))],
        out_shape=[jax.ShapeDtypeStruct((ngroups, nb, N_EXPERTS, TB), I32),
                   jax.ShapeDtypeStruct((ngroups, nb, N_EXPERTS, TB), F32),
                   jax.ShapeDtypeStruct((ngroups, 8, nr), I32)],
        compiler_params=_cparams(("arbitrary",)),
        name="select",
    )(aff)


STACK_ROWS = N_EXPERTS * TB


FIRST_ROWS = 3 * TB


WINR = 3 * SEG
assert N_EXPERTS * WINR == FIRST_ROWS
TF = 672
TIE_STEPS = 12
SUB = 2


def _window_rows(rowid_ref, dst, value_ref=None):
    for e in range(N_EXPERTS):
        rid_e = rowid_ref[e:e + 1, :]
        val_e = 1.0 if value_ref is None else value_ref[e:e + 1, :]
        hit = rid_e == lax.broadcasted_iota(I32, (WINR, TB), 0) + e * WINR
        dst[e * WINR:(e + 1) * WINR, :] = jnp.where(hit, val_e, 0.0).astype(dst.dtype)


class _Table:
    def __init__(self, tab_s, row, nb, per_block=False):
        self.tab_s, self.row, self.per = tab_s, row, nb if per_block else nb * N_EXPERTS
        self.scale = N_EXPERTS if per_block else 1

    def __getitem__(self, k):
        if self.per & (self.per - 1) == 0:
            hi, low = lax.shift_right_logical(k, self.per.bit_length() - 1), k & (self.per - 1)
        else:
            hi, low = lax.div(k, self.per), lax.rem(k, self.per)
        return self.tab_s[hi, self.row, low * self.scale]


def _tables(tab_s, nb):
    return (_Table(tab_s, 0, nb), _Table(tab_s, 1, nb), _Table(tab_s, 2, nb),
            _Table(tab_s, 3, nb, per_block=True))


def _overflow_groups(seg_s, k):
    return lax.div(jnp.maximum(seg_s[k] - WINR, 0), SEG)


def _overflow_rows(seg_s, offo_s, step, over, rowid_ref, dst, value_ref=None):
    def zero(i, carry):
        r0 = pl.multiple_of(FIRST_ROWS + i * SEG, SEG)
        dst[pl.ds(r0, SEG), :] = jnp.zeros((SEG, TB), dst.dtype)
        return carry

    lax.fori_loop(0, ((over + TB - 1) // TB) * (TB // SEG), zero, 0)
    for e in range(N_EXPERTS):
        k = step * N_EXPERTS + e
        rid_e = rowid_ref[e:e + 1, :]
        val_e = 1.0 if value_ref is None else value_ref[e:e + 1, :]

        def group(i, carry, off=offo_s[k], rid_e=rid_e, val_e=val_e):
            r0 = pl.multiple_of(off + i * SEG, SEG)
            hit = rid_e == lax.broadcasted_iota(I32, (SEG, TB), 0) + r0
            dst[pl.ds(r0, SEG), :] = jnp.where(hit, val_e, 0.0).astype(dst.dtype)
            return carry

        lax.fori_loop(0, _overflow_groups(seg_s, k), group, 0)


def _wait_rows(rows, make_copy):
    def big(i, carry):
        make_copy(TB).wait()
        return carry

    def small(i, carry):
        make_copy(SEG).wait()
        return carry

    lax.fori_loop(0, lax.div(rows, TB), big, 0)
    lax.fori_loop(0, lax.div(lax.rem(rows, TB), SEG), small, 0)


def _dispatch_kernel(tab_s, h_ref, rowid_ref, xe_hbm,
                     onehot, stack, zbuf, sem, zsem, *, nb, nsteps, cap):
    for sub in range(SUB):
        _dispatch_block(tab_s, h_ref.at[sub * TB:(sub + 1) * TB], rowid_ref.at[sub], xe_hbm, onehot,
                        stack, zbuf, sem, zsem, pl.program_id(1) * SUB + sub, nb, nsteps, cap)


def _dispatch_block(tab_s, h_ref, rowid_ref, xe_hbm, onehot, stack, zbuf, sem, zsem, b, nb, nsteps, cap):
    g = pl.program_id(0)
    step = g * nb + b
    slot = lax.rem(step, 2)
    seg_s, offo_s, offb_s, over_s = _tables(tab_s, nb)
    over = over_s[step]
    xrows = xe_hbm.shape[2]

    def wait_slot(nrows, sl):
        _wait_rows(nrows, lambda n: pltpu.make_async_copy(
            stack.at[sl, pl.ds(0, n)], xe_hbm.at[0, 0, pl.ds(0, n)], sem.at[sl]))

    @pl.when(step == 0)
    def _init():
        stack[...] = jnp.zeros_like(stack)

    @pl.when(b == 0)
    def _zero_unused():
        zbuf[...] = jnp.zeros_like(zbuf)
        for e in range(N_EXPERTS):
            pltpu.make_async_copy(zbuf, xe_hbm.at[g, e, pl.ds(cap, xrows - cap)], zsem).start()

    _window_rows(rowid_ref, onehot)
    h = h_ref[...]
    stack[slot, 0:FIRST_ROWS, :] = _dot(onehot[0:FIRST_ROWS, :], h).astype(BF16)

    @pl.when(over > 0)
    def _overflow():
        _overflow_rows(seg_s, offo_s, step, over, rowid_ref, onehot)

        def chunk(c, carry):
            base = pl.multiple_of(FIRST_ROWS + c * TB, TB)
            stack[slot, pl.ds(base, TB), :] = _dot(onehot[pl.ds(base, TB), :], h).astype(BF16)
            return carry

        lax.fori_loop(0, (over + TB - 1) // TB, chunk, 0)

    @pl.when(b == 0)
    def _zero_unused_done():
        for e in range(N_EXPERTS):
            pltpu.make_async_copy(zbuf, xe_hbm.at[g, e, pl.ds(cap, xrows - cap)], zsem).wait()

    @pl.when(step >= 1)
    def _previous_landed():
        wait_slot(FIRST_ROWS + over_s[step - 1], 1 - slot)

    for e in range(N_EXPERTS):
        k = step * N_EXPERTS + e
        pltpu.make_async_copy(
            stack.at[slot, e * WINR:(e + 1) * WINR],
            xe_hbm.at[g, e, pl.ds(pl.multiple_of(offb_s[k], SEG), WINR)], sem.at[slot]).start()

    @pl.when(over > 0)
    def _overflow_copies():
        for e in range(N_EXPERTS):
            k = step * N_EXPERTS + e

            def one(i, carry, e=e, k=k):
                pltpu.make_async_copy(
                    stack.at[slot, pl.ds(pl.multiple_of(offo_s[k] + i * SEG, SEG), SEG)],
                    xe_hbm.at[g, e, pl.ds(pl.multiple_of(offb_s[k] + WINR + i * SEG, SEG), SEG)],
                    sem.at[slot]).start()
                return carry

            lax.fori_loop(0, _overflow_groups(seg_s, k), one, 0)

    @pl.when(step == nsteps - 1)
    def _drain():
        wait_slot(FIRST_ROWS + over, slot)


def _expert_rows(ntok):
    cap = EC_FACTOR * ntok // N_EXPERTS
    worst = cap + (ntok // TB) * (SEG - 1)
    tiles = -(-(worst + WINR) // TF)
    assert cap % SEG == 0 and cap >= WINR and tiles * TF > cap
    return cap, tiles


def _dispatch_call(tab, h, rowid, ngroups, ntok):
    nb = ntok // TB
    cap, tiles = _expert_rows(ntok)
    xrows = tiles * TF
    assert nb % SUB == 0
    nsup = nb // SUB
    grid_spec = pltpu.PrefetchScalarGridSpec(
        num_scalar_prefetch=1,
        grid=(ngroups, nsup),
        in_specs=[pl.BlockSpec((SUB * TB, D), lambda g, b, *_: (g * nsup + b, 0)),
                  pl.BlockSpec((None, SUB, N_EXPERTS, TB), lambda g, b, *_: (g, b, 0, 0))],
        out_specs=pl.BlockSpec(memory_space=pl.ANY),
        scratch_shapes=[pltpu.VMEM((STACK_ROWS, TB), BF16), pltpu.VMEM((2, STACK_ROWS, D), BF16),
                        pltpu.VMEM((xrows - cap, D), BF16),
                        pltpu.SemaphoreType.DMA((2,)), pltpu.SemaphoreType.DMA])
    return pl.pallas_call(
        functools.partial(_dispatch_kernel, nb=nb, nsteps=ngroups * nb, cap=cap),
        grid_spec=grid_spec,
        out_shape=jax.ShapeDtypeStruct((ngroups, N_EXPERTS, xrows, D), BF16),
        compiler_params=_cparams(("arbitrary", "arbitrary")),
        name="dispatch",
    )(tab, h, rowid)


def _ffn_kernel(tab_s, xe_ref, wg_hbm, wu_hbm, wd_hbm, y_ref, wg32, wu32, wd32, wgb, wub, wdb, sem,
                *, layer):
    e = pl.program_id(0)
    g = pl.program_id(1)
    j = pl.program_id(2)

    def weight_copies(ee, sl):
        return (pltpu.make_async_copy(wg_hbm.at[layer, ee], wg32.at[sl], sem.at[sl]),
                pltpu.make_async_copy(wu_hbm.at[layer, ee], wu32.at[sl], sem.at[sl]),
                pltpu.make_async_copy(wd_hbm.at[layer, ee], wd32.at[sl], sem.at[sl]))

    @pl.when(jnp.logical_and(g == 0, j == 0))
    def _weights():
        sl = lax.rem(e, 2)

        @pl.when(e == 0)
        def _():
            for cp in weight_copies(e, sl):
                cp.start()

        @pl.when(e + 1 < N_EXPERTS)
        def _():
            for cp in weight_copies(e + 1, 1 - sl):
                cp.start()

        for cp in weight_copies(e, sl):
            cp.wait()
        wgb[...] = wg32[sl].astype(BF16)
        wub[...] = wu32[sl].astype(BF16)
        wdb[...] = wd32[sl].astype(BF16)

    live = j < tab_s[g, 5, e]

    @pl.when(live)
    def _run():
        x = xe_ref[...]
        hid = (_silu(_dot(x, wgb[...])) * _dot(x, wub[...])).astype(BF16)
        y_ref[...] = _dot(hid, wdb[...]).astype(BF16)

    @pl.when(jnp.logical_not(live))
    def _skip():
        y_ref[...] = jnp.zeros_like(y_ref)


def _ffn_call(tab, xe, wg, wu, wd, layer, ngroups, ntok):
    _, tiles = _expert_rows(ntok)

    def xmap(e, g, j, tab_s):
        return (g, e, jnp.minimum(j, tab_s[g, 5, e] - 1), 0)

    grid_spec = pltpu.PrefetchScalarGridSpec(
        num_scalar_prefetch=1,
        grid=(N_EXPERTS, ngroups, tiles),
        in_specs=[pl.BlockSpec((None, None, TF, D), xmap),
                  pl.BlockSpec(memory_space=pl.ANY), pl.BlockSpec(memory_space=pl.ANY),
                  pl.BlockSpec(memory_space=pl.ANY)],
        out_specs=pl.BlockSpec((None, None, TF, D), lambda e, g, j, nt: (g, e, j, 0)),
        scratch_shapes=[pltpu.VMEM((2, D, EXPERT_FF), F32), pltpu.VMEM((2, D, EXPERT_FF), F32),
                        pltpu.VMEM((2, EXPERT_FF, D), F32),
                        pltpu.VMEM((D, EXPERT_FF), BF16), pltpu.VMEM((D, EXPERT_FF), BF16),
                        pltpu.VMEM((EXPERT_FF, D), BF16), pltpu.SemaphoreType.DMA((2,))])
    return pl.pallas_call(
        functools.partial(_ffn_kernel, layer=layer),
        grid_spec=grid_spec,
        out_shape=jax.ShapeDtypeStruct((ngroups, N_EXPERTS, tiles * TF, D), BF16),
        compiler_params=_cparams(("arbitrary", "arbitrary", "arbitrary")),
        name="ffn",
    )(tab, xe, wg, wu, wd)


def _combine_kernel(tab_s, y_hbm, rowid_ref, gate_ref, x_ref, mod_ref,
                    modn_ref, g_ref, xo_ref, h_ref, weights, stack, acc, sem, *, nb, nsteps, final):
    for sub in range(SUB):
        rows = slice(sub * TB, (sub + 1) * TB)
        _combine_block(tab_s, y_hbm, rowid_ref.at[sub], gate_ref.at[sub], x_ref.at[rows], mod_ref,
                       modn_ref, g_ref, xo_ref.at[rows], h_ref.at[rows], weights, stack, acc, sem,
                       pl.program_id(1) * SUB + sub, nb, nsteps, final)


def _combine_block(tab_s, y_hbm, rowid_ref, gate_ref, x_ref, mod_ref, modn_ref, g_ref, xo_ref, h_ref,
                   weights, stack, acc, sem, b, nb, nsteps, final):
    g = pl.program_id(0)
    step = g * nb + b
    slot = lax.rem(step, 2)
    seg_s, offo_s, offb_s, over_s = _tables(tab_s, nb)
    over = over_s[step]

    def fetch(st, sl):
        gg = lax.div(st, nb)
        for e in range(N_EXPERTS):
            k = st * N_EXPERTS + e
            pltpu.make_async_copy(
                y_hbm.at[gg, e, pl.ds(pl.multiple_of(offb_s[k], SEG), WINR)],
                stack.at[sl, e * WINR:(e + 1) * WINR], sem.at[sl]).start()

        @pl.when(over_s[st] > 0)
        def _():
            for e in range(N_EXPERTS):
                k = st * N_EXPERTS + e

                def one(i, carry, e=e, k=k):
                    pltpu.make_async_copy(
                        y_hbm.at[gg, e, pl.ds(pl.multiple_of(offb_s[k] + WINR + i * SEG, SEG), SEG)],
                        stack.at[sl, pl.ds(pl.multiple_of(offo_s[k] + i * SEG, SEG), SEG)],
                        sem.at[sl]).start()
                    return carry

                lax.fori_loop(0, _overflow_groups(seg_s, k), one, 0)

    @pl.when(step == 0)
    def _first():
        stack[...] = jnp.zeros_like(stack)
        fetch(step, slot)

    if nsteps > 1:
        @pl.when(step + 1 < nsteps)
        def _prefetch():
            fetch(step + 1, 1 - slot)

    _window_rows(rowid_ref, weights, gate_ref)

    @pl.when(over > 0)
    def _():
        _overflow_rows(seg_s, offo_s, step, over, rowid_ref, weights, gate_ref)

    _wait_rows(FIRST_ROWS + over, lambda n: pltpu.make_async_copy(
        y_hbm.at[0, 0, pl.ds(0, n)], stack.at[slot, pl.ds(0, n)], sem.at[slot]))

    def token_weights(base):
        return weights[pl.ds(base, TB), :].T.astype(BF16)

    w = jnp.concatenate([token_weights(c * TB) for c in range(FIRST_ROWS // TB)], axis=1)
    acc[...] = _dot(w, stack[slot, 0:FIRST_ROWS, :])

    @pl.when(over > 0)
    def _():
        def chunk(c, carry):
            base = pl.multiple_of(FIRST_ROWS + c * TB, TB)
            acc[...] += _dot(token_weights(base), stack[slot, pl.ds(base, TB), :])
            return carry

        lax.fori_loop(0, (over + TB - 1) // TB, chunk, 0)

    x = x_ref[...] + mod_ref[5] * acc[...]
    if final:
        y = _rms(x, g_ref[...])

        @pl.when(g == 0)
        def _():
            xo_ref[...] = y

        @pl.when(g != 0)
        def _():
            h_ref[...] = y
    else:
        xo_ref[...] = x
        h_ref[...] = _modulate(x, g_ref[...], modn_ref[0], modn_ref[1]).astype(BF16)


def _combine_call(tab, y, rowid, gate, x, mods, layer, nxt, gains, gidx, modrow, ngroups, ntok, final):
    nb = ntok // TB
    t = x.shape[0]
    assert nb % SUB == 0
    nsup, rows = nb // SUB, SUB * TB
    if final:
        assert ngroups == 2
        out_specs = [pl.BlockSpec((rows, D), lambda g, b, *_: (jnp.where(g == 0, b, nsup - 1), 0)),
                     pl.BlockSpec((rows, D), lambda g, b, *_: (jnp.where(g == 0, 0, b), 0))]
        out_shape = [jax.ShapeDtypeStruct((ntok, D), F32), jax.ShapeDtypeStruct((ntok, D), F32)]
    else:
        out_specs = [pl.BlockSpec((rows, D), lambda g, b, *_: (g * nsup + b, 0)),
                     pl.BlockSpec((rows, D), lambda g, b, *_: (g * nsup + b, 0))]
        out_shape = [jax.ShapeDtypeStruct((t, D), F32), jax.ShapeDtypeStruct((t, D), BF16)]
    grid_spec = pltpu.PrefetchScalarGridSpec(
        num_scalar_prefetch=1,
        grid=(ngroups, nsup),
        in_specs=[pl.BlockSpec(memory_space=pl.ANY),
                  pl.BlockSpec((None, SUB, N_EXPERTS, TB), lambda g, b, *_: (g, b, 0, 0)),
                  pl.BlockSpec((None, SUB, N_EXPERTS, TB), lambda g, b, *_: (g, b, 0, 0)),
                  pl.BlockSpec((rows, D), lambda g, b, *_: (g * nsup + b, 0)),
                  pl.BlockSpec((None, None, 6, 1, D),
                               lambda g, b, *_: (layer, modrow(g * nsup + b), 0, 0, 0)),
                  pl.BlockSpec((None, None, 6, 1, D),
                               lambda g, b, *_: (nxt, modrow(g * nsup + b), 0, 0, 0)),
                  pl.BlockSpec((None, 1, D), lambda g, b, *_: (gidx, 0, 0))],
        out_specs=out_specs,
        scratch_shapes=[pltpu.VMEM((STACK_ROWS, TB), F32), pltpu.VMEM((2, STACK_ROWS, D), BF16),
                        pltpu.VMEM((TB, D), F32), pltpu.SemaphoreType.DMA((2,))])
    return pl.pallas_call(
        functools.partial(_combine_kernel, nb=nb, nsteps=ngroups * nb, final=final),
        grid_spec=grid_spec,
        out_shape=out_shape,
        compiler_params=_cparams(("arbitrary", "arbitrary")),
        name="combine",
    )(tab, y, rowid, gate, x, mods, mods, gains)


def kernel(x_prompt, x_sample, cache_diff_k, cache_diff_v, cache_swa_k, cache_swa_v, cache_mla_ckv, cache_mla_krope, c, c_ctx, w_ada, b_ada, norm_mix, norm_ffn, w_in_even, w_out_even, diff_lambda, diff_subln, swa_sink, w_in_odd, mla_q_norm, w_q_up, mla_kv_norm, w_kv_up, w_out_odd, w_router, w_gate_exp, w_up_exp, w_down_exp, final_norm):
    batch, seq, _ = x_prompt.shape
    dec_batch, dec_seq, _ = x_sample.shape
    past = cache_diff_k.shape[2]
    depth = w_ada.shape[0]
    n_even = w_in_even.shape[0]
    n_odd = w_in_odd.shape[0]
    nc, ns = batch * seq, dec_batch * dec_seq
    assert nc == ns, "the routed-expert kernels take two token groups of equal size"
    assert nc % TM == 0 and dec_seq % TM == 0 and past % 256 == 0 and dec_seq % GRID_W == 0
    assert seq % TQ == 0 and dec_seq % TQ_LATENT == 0
    ntok = nc

    def modrow_of(tile):
        def modrow(r):
            tok = r * tile
            return jnp.where(tok < nc, 0, 1 + jnp.maximum(tok - nc, 0) // dec_seq)
        return modrow

    modrow, modrow_tb = modrow_of(TM), modrow_of(SUB * TB)
    assert dec_seq % (SUB * TB) == 0

    rc = -(-(1 + dec_batch) // 16) * 16
    cvec = jnp.zeros((rc, D), F32).at[0].set(c_ctx).at[1:1 + dec_batch].set(c)
    mods = _ada_call(cvec, w_ada, b_ada).reshape(depth, rc, 6, 1, D)

    w_even_b = w_in_even.astype(BF16)
    w_oute_b = w_out_even.astype(BF16)
    w_outo_b = w_out_odd.astype(BF16)
    kr_pad = jnp.zeros((n_odd, D, LANES), F32).at[:, :, 64:96].set(w_in_odd[:, :, 640:672])
    w_odd_b = jnp.concatenate([w_in_odd[:, :, :640], kr_pad], axis=-1).astype(BF16)
    wq = w_q_up.reshape(n_odd, MLA_Q_RANK, MLA_HEADS, HD + MLA_ROPE)
    wq_b = jnp.pad(wq, ((0, 0), (0, 0), (0, 0), (0, LANES - HD - MLA_ROPE))).reshape(
        n_odd, MLA_Q_RANK, MLA_QW).astype(BF16)
    wkv = w_kv_up.reshape(n_odd, MLA_KV_RANK, MLA_HEADS, 2 * HD)
    wk_b = jnp.pad(wkv[..., :HD], ((0, 0), (0, 0), (0, 0), (0, LANES - HD))).reshape(
        n_odd, MLA_KV_RANK, MLA_QW).astype(BF16)
    wv_b = wkv[..., HD:].reshape(n_odd, MLA_KV_RANK, D).astype(BF16)
    w_router_b = jnp.pad(w_router, ((0, 0), (0, 0), (0, LANES - N_EXPERTS))).astype(BF16)
    even_tabs = _rope_tables(dec_seq, 16, _even_lane)
    mla_tabs = _rope_tables(dec_seq, 8, _mla_lane)
    ckr = jnp.zeros((dec_batch, n_odd, past, LANES), F32).at[..., 64:96].set(cache_mla_krope)
    even_caches = (cache_diff_k, cache_diff_v, cache_swa_k, cache_swa_v)

    mix_gains = jnp.concatenate([norm_mix, final_norm[None]], axis=0).reshape(depth + 1, 1, D)
    ffn_gains = norm_ffn.reshape(depth, 1, D)
    q_gains = mla_q_norm.reshape(n_odd, 1, MLA_Q_RANK)
    kv_gains = mla_kv_norm.reshape(n_odd, 1, MLA_KV_RANK)
    sublns = diff_subln.reshape(n_even, 1, 2 * HD)

    xa, xb, xb_off = x_prompt.reshape(nc, D), x_sample.reshape(ns, D), 0
    h = _norm_mod_call(xa, xb, mods, 0, mix_gains, modrow)
    even_state = odd_state = None
    y_prompt = y_sample = None
    for i in range(depth):
        j = i // 2
        if i % 2 == 0:
            qkv_c, *even_state = _proj_even_call(h, w_even_b, j, 0, nc, None, dec_seq,
                                                 prev=even_state, seq=seq)
            (qkv_l,) = _proj_even_call(h, w_even_b, j, nc, ns, even_tabs, dec_seq)
            li = _lambda_init(i)
            o_c = _attn_even_call(qkv_c, None, j, diff_lambda, sublns, swa_sink, batch, seq, 0, li)
            o_l = _attn_even_call(qkv_l, even_caches, j, diff_lambda, sublns, swa_sink,
                                  dec_batch, dec_seq, past, li)
            w_out = w_oute_b
        else:
            q_c, *odd_state = _proj_odd_call(h, w_odd_b, q_gains, wq_b, kv_gains, j, 0, nc, None,
                                             dec_seq, prev=odd_state, seq=seq)
            q_l, ckv_l, kr_l = _proj_odd_call(h, w_odd_b, q_gains, wq_b, kv_gains, j, nc, ns,
                                              mla_tabs, dec_seq)
            o_c = _attn_odd_call(q_c, odd_state[0], odd_state[1], None, j, wk_b, wv_b, batch, seq, 0)
            o_l = _attn_odd_call(q_l, ckv_l, kr_l, (cache_mla_ckv, ckr), j, wk_b, wv_b,
                                 dec_batch, dec_seq, past)
            w_out = w_outo_b
        x, h2, aff = _outproj_call(xa, xb, xb_off, o_c, o_l, w_out, j, mods, i, ffn_gains,
                                   w_router_b, modrow)
        rowid, gate, tab = _select_call(aff, 2, ntok)
        xe = _dispatch_call(tab, h2, rowid, 2, ntok)
        y = _ffn_call(tab, xe, w_gate_exp, w_up_exp, w_down_exp, i, 2, ntok)
        final = i == depth - 1
        nxt = i if final else i + 1
        out_a, out_b = _combine_call(tab, y, rowid, gate, x, mods, i, nxt, mix_gains,
                                     depth if final else nxt, modrow_tb, 2, ntok, final)
        if final:
            y_prompt = out_a.reshape(batch, seq, D)
            y_sample = out_b.reshape(dec_batch, dec_seq, D)
        else:
            x, h = out_a, out_b
            xa, xb, xb_off = x, x, nc // TM

    kd, vd, ks, vs = even_state
    ckv_new, kr_new = odd_state
    return (y_prompt, y_sample,
            kd.reshape(batch, n_even, seq, DIFF_HEADS, 2 * HD),
            vd.reshape(batch, n_even, seq, DIFF_HEADS, 2 * HD),
            ks.reshape(batch, n_even, seq, 2, HD), vs.reshape(batch, n_even, seq, 2, HD),
            ckv_new, kr_new[..., 64:96])
```

```python
import functools
import math

import jax
import jax.numpy as jnp
import numpy as np
from jax import lax
from jax.experimental import pallas as pl
from jax.experimental.pallas import tpu as pltpu

F32 = jnp.float32
BF16 = jnp.bfloat16
I32 = jnp.int32

D = 1024
HD = 64
GRID_W = 64
WINDOW = 128
DIFF_HEADS = 4
SWA_HEADS = 8
MLA_HEADS = 16
MLA_Q_RANK = 384
MLA_KV_RANK = 256
MLA_ROPE = 32
N_EXPERTS = 16
EXPERT_FF = 512
EC_FACTOR = 2
ROPE_BASE = 10000.0
EPS = 1e-6
NEG_INF = -1e30
LOG2E = math.log2(math.e)
EVEN_IN = 2304
LANES = 128
TM = 512
TQ = 256
TQ_LATENT = 256
TB = 256
SEG = 16
VMEM_LIMIT = 56 * 1024 * 1024


def _cparams(sem, vmem=VMEM_LIMIT):
    return pltpu.CompilerParams(dimension_semantics=sem, vmem_limit_bytes=vmem)


def _dot(a, b):
    return jnp.dot(a, b, preferred_element_type=F32)


def _dot_nt(a, b):
    return lax.dot_general(a, b, (((1,), (1,)), ((), ())), preferred_element_type=F32)


def _silu(x):
    return x / (1.0 + jnp.exp(-x))


def _rms(x, g):
    ms = jnp.mean(x * x, axis=-1, keepdims=True)
    return x * lax.rsqrt(ms + EPS) * g


def _modulate(x, g, shift, scale):
    return _rms(x, g) * (1.0 + scale) + shift


def _lambda_init(layer):
    return 0.8 - 0.6 * math.exp(-0.3 * layer)


def _ada_kernel(c_ref, w_ref, b_ref, o_ref):
    s = _silu(c_ref[...]).astype(BF16)
    o_ref[...] = _dot(s, w_ref[...].astype(BF16)) + b_ref[...]


def _ada_call(cvec, w_ada, b_ada):
    depth, _, n6 = w_ada.shape
    rc = cvec.shape[0]
    tn = 512
    return pl.pallas_call(
        _ada_kernel,
        grid=(depth, n6 // tn),
        in_specs=[pl.BlockSpec((rc, D), lambda i, n: (0, 0)),
                  pl.BlockSpec((None, D, tn), lambda i, n: (i, 0, n)),
                  pl.BlockSpec((None, 1, tn), lambda i, n: (i, 0, n))],
        out_specs=pl.BlockSpec((None, rc, tn), lambda i, n: (i, 0, n)),
        out_shape=jax.ShapeDtypeStruct((depth, rc, n6), F32),
        compiler_params=_cparams(("parallel", "parallel")),
        name="ada",
    )(cvec, w_ada, b_ada.reshape(depth, 1, n6))


def _two_part_specs(rows_a, rows_b, width, off_b=0):
    na, nb_ = rows_a // TM, rows_b // TM
    return (pl.BlockSpec((TM, width), lambda r: (jnp.minimum(r, na - 1), 0)),
            pl.BlockSpec((TM, width), lambda r: (off_b + jnp.clip(r - na, 0, nb_ - 1), 0)))


def _norm_mod_kernel(xa_ref, xb_ref, mod_ref, g_ref, h_ref, *, ntile_a):
    def run(x_ref):
        h_ref[...] = _modulate(x_ref[...], g_ref[...], mod_ref[0], mod_ref[1]).astype(BF16)

    @pl.when(pl.program_id(0) < ntile_a)
    def _():
        run(xa_ref)

    @pl.when(pl.program_id(0) >= ntile_a)
    def _():
        run(xb_ref)


def _norm_mod_call(xa, xb, mods, layer, gains, modrow):
    na, nb_ = xa.shape[0], xb.shape[0]
    return pl.pallas_call(
        functools.partial(_norm_mod_kernel, ntile_a=na // TM),
        grid=((na + nb_) // TM,),
        in_specs=[*_two_part_specs(na, nb_, D),
                  pl.BlockSpec((None, None, 6, 1, D), lambda r: (layer, modrow(r), 0, 0, 0)),
                  pl.BlockSpec((None, 1, D), lambda r: (layer, 0, 0))],
        out_specs=pl.BlockSpec((TM, D), lambda r: (r, 0)),
        out_shape=jax.ShapeDtypeStruct((na + nb_, D), BF16),
        compiler_params=_cparams(("parallel",)),
        name="norm_mod",
    )(xa, xb, mods, gains)


def _rope_tables(dec_seq, half, lane_of_dim):
    pos = jnp.arange(dec_seq)
    row = (pos // GRID_W).astype(F32)
    col = (pos % GRID_W).astype(F32)
    inv = ROPE_BASE ** (-(jnp.arange(half, dtype=F32) / half))
    ang = jnp.stack([row[:, None] * inv[None, :], col[:, None] * inv[None, :]])
    info = [lane_of_dim(lane) for lane in range(LANES)]
    axis = np.array([0 if i is None else i[0] for i in info])
    freq = np.array([0 if i is None else i[1] for i in info])
    first = np.array([i is not None and not i[2] for i in info])[None, :]
    second = np.array([i is not None and i[2] for i in info])[None, :]
    lane_ang = ang[axis, :, freq].T
    cos, sin = jnp.cos(lane_ang), jnp.sin(lane_ang)
    return (jnp.where(first | second, cos, 1.0), jnp.where(first, -sin, 0.0),
            jnp.where(second, sin, 0.0))


def _even_lane(lane):
    j = lane % HD
    axis, jj = j // 32, j % 32
    return axis, jj % 16, jj >= 16


def _mla_lane(lane):
    if lane < 64 or lane >= 96:
        return None
    jj = lane - 64
    axis, k = jj // 16, jj % 16
    return axis, k % 8, k >= 8


def _rope(x, c, s1, s2, shift):
    return x * c + pltpu.roll(x, LANES - shift, 1) * s1 + pltpu.roll(x, shift, 1) * s2


_EVEN_ROPE_TILES = tuple(range(0, 8)) + tuple(range(12, 17))
_EVEN_Q_TILES = tuple(range(0, 4)) + tuple(range(12, 16))


def _append_layer(prev_refs, out_refs, new_values):
    for i, (out_ref, new) in enumerate(zip(out_refs, new_values)):
        nbatch, nlayers, seq, width = out_ref.shape
        if prev_refs:
            out_ref[:, 0:nlayers - 1] = prev_refs[i][...]
        out_ref[:, nlayers - 1] = new.reshape(nbatch, seq, width)


def _state_specs(prev, widths, seq):
    nlayers = 1 if prev is None else prev[0].shape[1] + 1
    per = TM // seq
    ins = [] if prev is None else [pl.BlockSpec((per, nlayers - 1, seq, w), lambda r: (r, 0, 0, 0))
                                   for w in widths]
    outs = [pl.BlockSpec((per, nlayers, seq, w), lambda r: (r, 0, 0, 0)) for w in widths]
    return ins, outs, nlayers


def _proj_even_kernel(*refs, rope, caches, nprev):
    h_ref, w_ref = refs[0], refs[1]
    pos = 2
    if rope:
        c_ref, s1_ref, s2_ref = refs[2:5]
        pos = 5
    prev_refs = refs[pos:pos + nprev]
    pos += nprev
    qkv_ref = refs[pos]
    res = _dot(h_ref[...], w_ref[...])
    scale = HD ** -0.5 * LOG2E
    for t in range(EVEN_IN // LANES):
        x = res[:, t * LANES:(t + 1) * LANES]
        if rope and t in _EVEN_ROPE_TILES:
            x = _rope(x, c_ref[...], s1_ref[...], s2_ref[...], 16)
        if t in _EVEN_Q_TILES:
            x = x * scale
        qkv_ref[:, t * LANES:(t + 1) * LANES] = x.astype(BF16)
    if caches:
        _append_layer(prev_refs, refs[pos + 1:pos + 5],
                      (res[:, 512:1024], res[:, 1024:1536], res[:, 2048:2176], res[:, 2176:2304]))


def _proj_even_call(h, w, j, row0, nrows, tables, dec_seq, prev=None, seq=None):
    rope = tables is not None
    caches = not rope
    t0 = row0 // TM
    in_specs = [pl.BlockSpec((TM, D), lambda r: (t0 + r, 0)),
                pl.BlockSpec((None, D, EVEN_IN), lambda r: (j, 0, 0))]
    args = [h, w]
    if rope:
        per = dec_seq // TM
        for _ in range(3):
            in_specs.append(pl.BlockSpec((TM, LANES), lambda r: (r % per, 0)))
        args += list(tables)
    out_specs = [pl.BlockSpec((TM, EVEN_IN), lambda r: (r, 0))]
    out_shape = [jax.ShapeDtypeStruct((nrows, EVEN_IN), BF16)]
    nprev = 0
    if caches:
        widths = (512, 512, LANES, LANES)
        ins, outs, nlayers = _state_specs(prev, widths, seq)
        nprev = len(ins)
        in_specs += ins
        args += [] if prev is None else list(prev)
        out_specs += outs
        out_shape += [jax.ShapeDtypeStruct((nrows // seq, nlayers, seq, wd), F32) for wd in widths]
    return pl.pallas_call(
        functools.partial(_proj_even_kernel, rope=rope, caches=caches, nprev=nprev),
        grid=(nrows // TM,),
        in_specs=in_specs, out_specs=out_specs, out_shape=out_shape,
        compiler_params=_cparams(("parallel",)),
        name="proj_even_lat" if rope else "proj_even_ctx",
    )(*args)


ODD_IN_PAD = MLA_Q_RANK + MLA_KV_RANK + LANES
MLA_QW = MLA_HEADS * LANES


def _proj_odd_kernel(*refs, rope, nprev):
    h_ref, w_ref, qn_ref, wq_ref, kvn_ref = refs[:5]
    pos = 5
    if rope:
        c_ref, s1_ref, s2_ref = refs[5:8]
        pos = 8
    prev_refs = refs[pos:pos + nprev]
    pos += nprev
    q_ref, ckv_ref, kr_ref = refs[pos:pos + 3]
    res = _dot(h_ref[...], w_ref[...])
    cq = _rms(res[:, :MLA_Q_RANK], qn_ref[...]).astype(BF16)
    ckv = _rms(res[:, MLA_Q_RANK:MLA_Q_RANK + MLA_KV_RANK], kvn_ref[...])
    kr = res[:, MLA_Q_RANK + MLA_KV_RANK:]
    if rope:
        kr = _rope(kr, c_ref[...], s1_ref[...], s2_ref[...], 8)
        ckv_ref[...] = ckv
        kr_ref[...] = kr
    else:
        _append_layer(prev_refs, (ckv_ref, kr_ref), (ckv, kr))
    q = _dot(cq, wq_ref[...])
    scale = (HD + MLA_ROPE) ** -0.5 * LOG2E
    for t in range(MLA_HEADS):
        x = q[:, t * LANES:(t + 1) * LANES]
        if rope:
            x = _rope(x, c_ref[...], s1_ref[...], s2_ref[...], 8)
        q_ref[:, t * LANES:(t + 1) * LANES] = (x * scale).astype(BF16)


def _proj_odd_call(h, w_in, qn, wq, kvn, j, row0, nrows, tables, dec_seq, prev=None, seq=None):
    rope = tables is not None
    t0 = row0 // TM
    in_specs = [pl.BlockSpec((TM, D), lambda r: (t0 + r, 0)),
                pl.BlockSpec((None, D, ODD_IN_PAD), lambda r: (j, 0, 0)),
                pl.BlockSpec((None, 1, MLA_Q_RANK), lambda r: (j, 0, 0)),
                pl.BlockSpec((None, MLA_Q_RANK, MLA_QW), lambda r: (j, 0, 0)),
                pl.BlockSpec((None, 1, MLA_KV_RANK), lambda r: (j, 0, 0))]
    args = [h, w_in, qn, wq, kvn]
    if rope:
        per = dec_seq // TM
        for _ in range(3):
            in_specs.append(pl.BlockSpec((TM, LANES), lambda r: (r % per, 0)))
        args += list(tables)
    out_specs = [pl.BlockSpec((TM, MLA_QW), lambda r: (r, 0))]
    out_shape = [jax.ShapeDtypeStruct((nrows, MLA_QW), BF16)]
    widths = (MLA_KV_RANK, LANES)
    nprev = 0
    if rope:
        out_specs += [pl.BlockSpec((TM, wd), lambda r: (r, 0)) for wd in widths]
        out_shape += [jax.ShapeDtypeStruct((nrows, wd), F32) for wd in widths]
    else:
        ins, outs, nlayers = _state_specs(prev, widths, seq)
        nprev = len(ins)
        in_specs += ins
        args += [] if prev is None else list(prev)
        out_specs += outs
        out_shape += [jax.ShapeDtypeStruct((nrows // seq, nlayers, seq, wd), F32) for wd in widths]
    return pl.pallas_call(
        functools.partial(_proj_odd_kernel, rope=rope, nprev=nprev),
        grid=(nrows // TM,),
        in_specs=in_specs,
        out_specs=out_specs,
        out_shape=out_shape,
        compiler_params=_cparams(("parallel",)),
        name="proj_odd_lat" if rope else "proj_odd_ctx",
    )(*args)


def _query_tile(past):
    return TQ_LATENT if past > 0 else TQ


def _ones_lane(half):
    return HD if half == 0 else 0


def _row_sum(e, o, half, from_matmul):
    if from_matmul:
        one = _ones_lane(half)
        return o[:, one:one + 1]
    return jnp.sum(e, axis=-1, keepdims=True)


def _half_values(v, half):
    lane = lax.broadcasted_iota(I32, (1, LANES), 1)
    keep = (lane < HD) if half == 0 else (lane >= HD)
    return jnp.where(keep, v, jnp.where(lane == _ones_lane(half), 1.0, 0.0)).astype(BF16)


VDT_ROWS = 2 * HD + SEG


def _attn_even_kernel(*refs, seq, past, lam_init, layer):
    latent = past > 0
    tq = _query_tile(past)
    n = past + seq
    if latent:
        (qkv_ref, ck_ref, cv_ref, sk_ref, sv_ref, lam_ref, subln_ref, sink_ref,
         o_ref, kd, vd, ka, vl, vh) = refs
    else:
        qkv_ref, lam_ref, subln_ref, sink_ref, o_ref, kd, vd, ka, vl, vh = refs
    qi = pl.program_id(1)
    lo = lax.broadcasted_iota(I32, (1, LANES), 1) < HD

    @pl.when(qi == 0)
    def _build():
        chunk = 256
        for c0 in range(0, n, chunk):
            rows = slice(c0, c0 + chunk)
            if c0 < past:
                prow = slice(c0, c0 + chunk)
                for h in range(DIFF_HEADS):
                    kd[rows, h * LANES:(h + 1) * LANES] = ck_ref[prow, h, :].astype(BF16)
                    vd[h * VDT_ROWS:h * VDT_ROWS + LANES, rows] = cv_ref[prow, h, :].T.astype(BF16)
                kt = jnp.concatenate([sk_ref[prow, 0, :], sk_ref[prow, 1, :]], axis=1)
                vt = jnp.concatenate([sv_ref[prow, 0, :], sv_ref[prow, 1, :]], axis=1)
            else:
                orow = slice(c0 - past, c0 - past + chunk)
                kd[rows, :] = qkv_ref[orow, 512:1024]
                for h in range(DIFF_HEADS):
                    vown = qkv_ref[orow, 1024 + h * LANES:1024 + (h + 1) * LANES].astype(F32)
                    vd[h * VDT_ROWS:h * VDT_ROWS + LANES, rows] = vown.T.astype(BF16)
                kt = qkv_ref[orow, 2048:2176].astype(F32)
                vt = qkv_ref[orow, 2176:2304].astype(F32)
            kr = pltpu.roll(kt, HD, 1)
            vr = pltpu.roll(vt, HD, 1)
            ka[0, rows, :] = jnp.where(lo, kt, kr).astype(BF16)
            ka[1, rows, :] = jnp.where(lo, kr, kt).astype(BF16)
            vl[0, rows, :] = _half_values(vt, 0)
            vh[0, rows, :] = _half_values(vr, 1)
            vl[1, rows, :] = _half_values(vr, 0)
            vh[1, rows, :] = _half_values(vt, 1)
        for h in range(DIFF_HEADS):
            vd[h * VDT_ROWS + LANES:(h + 1) * VDT_ROWS, :] = jnp.ones((SEG, n), BF16)

    r0 = pl.multiple_of(qi * tq, tq)
    lam = lam_ref[...]
    lam_full = (jnp.exp(jnp.sum(lam[0:1] * lam[1:2], axis=-1, keepdims=True))
                - jnp.exp(jnp.sum(lam[2:3] * lam[3:4], axis=-1, keepdims=True)) + lam_init)
    zero_b = jnp.zeros((), BF16)

    for h in range(DIFF_HEADS):
        cs = slice(h * LANES, (h + 1) * LANES)
        qt = qkv_ref[pl.ds(r0, tq), cs]
        kh = kd[:, cs]
        ots = []
        for comp in range(2):
            qc = jnp.where(lo, qt, zero_b) if comp == 0 else jnp.where(lo, zero_b, qt)
            s = _dot_nt(qc, kh)
            m = jnp.max(s, axis=-1, keepdims=True)
            e = jnp.exp2(s - m).astype(BF16)
            ots.append(_dot_nt(vd[h * VDT_ROWS:(h + 1) * VDT_ROWS, :], e))
        ot = (ots[0][0:LANES, :] * (1.0 / ots[0][LANES:LANES + 1, :])
              - ots[1][0:LANES, :] * (lam_full / ots[1][LANES:LANES + 1, :]))
        o = _rms(ot.T, subln_ref[...]) * (1.0 - lam_init)
        o_ref[:, cs] = o.astype(BF16)

    nblk = seq // WINDOW
    dense = past if latent else seq
    if latent:
        per = tq // WINDOW
        offsets = tuple(range(-1, per + 1))
        rr = lax.broadcasted_iota(I32, (tq, WINDOW), 0)
        cc = lax.broadcasted_iota(I32, (tq, WINDOW), 1)
        band, starts = {}, {}
        for d in offsets:
            blk = qi * per + d
            inside = jnp.logical_and(blk >= 0, blk < nblk)
            band[d] = jnp.logical_and(jnp.abs(rr - cc - d * WINDOW) <= WINDOW, inside)
            starts[d] = pl.multiple_of(past + jnp.clip(blk, 0, nblk - 1) * WINDOW, WINDOW)
    for i in range(SWA_HEADS // 2):
        hk = i // 2
        cs = slice(1536 + i * LANES, 1536 + (i + 1) * LANES)
        qt = qkv_ref[pl.ds(r0, tq), cs]
        halves = []
        for half in range(2):
            qc = jnp.where(lo, qt, zero_b) if half == 0 else jnp.where(lo, zero_b, qt)
            vsel = vl if half == 0 else vh
            sink = sink_ref[layer, 2 * i + half] * LOG2E
            parts = [_dot_nt(qc, ka[hk, 0:dense, :])]
            if latent:
                for d in offsets:
                    s = _dot_nt(qc, ka[hk, pl.ds(starts[d], WINDOW), :])
                    parts.append(jnp.where(band[d], s, NEG_INF))
            s_all = jnp.concatenate(parts, axis=1) if len(parts) > 1 else parts[0]
            m = jnp.maximum(jnp.max(s_all, axis=-1, keepdims=True), sink)
            e = jnp.exp2(s_all - m)
            eb = e.astype(BF16)
            o = _dot(eb[:, 0:dense], vsel[hk, 0:dense, :])
            if latent:
                for k, d in enumerate(offsets):
                    o += _dot(eb[:, dense + k * WINDOW:dense + (k + 1) * WINDOW],
                              vsel[hk, pl.ds(starts[d], WINDOW), :])
            den = _row_sum(e, o, half, from_matmul=latent) + jnp.exp2(sink - m)
            halves.append(o * (1.0 / den))
        o_ref[:, 512 + i * LANES:512 + (i + 1) * LANES] = jnp.where(lo, halves[0], halves[1]).astype(BF16)


def _attn_even_call(qkv, caches, j, lam, subln, sink, nbatch, seq, past, lam_init):
    n = past + seq
    latent = past > 0
    tq = _query_tile(past)
    in_specs = [pl.BlockSpec((seq, EVEN_IN), lambda b, q: (b, 0))]
    args = [qkv]
    if latent:
        ck, cv, sk, sv = caches
        in_specs += [pl.BlockSpec((None, None, past, DIFF_HEADS, 2 * HD), lambda b, q: (b, j, 0, 0, 0)),
                     pl.BlockSpec((None, None, past, DIFF_HEADS, 2 * HD), lambda b, q: (b, j, 0, 0, 0)),
                     pl.BlockSpec((None, None, past, 2, HD), lambda b, q: (b, j, 0, 0, 0)),
                     pl.BlockSpec((None, None, past, 2, HD), lambda b, q: (b, j, 0, 0, 0))]
        args += [ck, cv, sk, sv]
    in_specs += [pl.BlockSpec((None, 4, HD), lambda b, q: (j, 0, 0)),
                 pl.BlockSpec((None, 1, 2 * HD), lambda b, q: (j, 0, 0)),
                 pl.BlockSpec(memory_space=pltpu.SMEM)]
    args += [lam, subln, sink]
    return pl.pallas_call(
        functools.partial(_attn_even_kernel, seq=seq, past=past, lam_init=lam_init, layer=j),
        grid=(nbatch, seq // tq),
        in_specs=in_specs,
        out_specs=pl.BlockSpec((tq, D), lambda b, q: (b * (seq // tq) + q, 0)),
        out_shape=jax.ShapeDtypeStruct((nbatch * seq, D), BF16),
        scratch_shapes=[pltpu.VMEM((n, 512), BF16),
                        pltpu.VMEM((DIFF_HEADS * VDT_ROWS, n), BF16),
                        pltpu.VMEM((2, n, LANES), BF16), pltpu.VMEM((2, n, LANES), BF16),
                        pltpu.VMEM((2, n, LANES), BF16)],
        compiler_params=_cparams(("arbitrary", "arbitrary")),
        name="attn_even_lat" if latent else "attn_even_ctx",
    )(*args)


VT_ROWS = HD + SEG


def _attn_odd_kernel(*refs, seq, past):
    latent = past > 0
    tq = _query_tile(past)
    n = past + seq
    if latent:
        q_ref, ckv_ref, kr_ref, cckv_ref, ckr_ref, wk_ref, wv_ref, o_ref, kf, vt = refs
    else:
        q_ref, ckv_ref, kr_ref, wk_ref, wv_ref, o_ref, kf, vlo, vhi = refs
    qi = pl.program_id(1)

    @pl.when(qi == 0)
    def _build():
        chunk = 256
        for c0 in range(0, n, chunk):
            rows = slice(c0, c0 + chunk)
            if c0 < past:
                ckv = cckv_ref[c0:c0 + chunk, :].astype(BF16)
                kr = ckr_ref[c0:c0 + chunk, :]
            else:
                ckv = ckv_ref[c0 - past:c0 - past + chunk, :].astype(BF16)
                kr = kr_ref[c0 - past:c0 - past + chunk, :]
            kk = _dot(ckv, wk_ref[...])
            for h in range(MLA_HEADS):
                cs = slice(h * LANES, (h + 1) * LANES)
                kf[rows, cs] = (kk[:, cs] + kr).astype(BF16)
            vv = _dot(ckv, wv_ref[...])
            for i in range(MLA_HEADS // 2):
                cs = slice(i * LANES, (i + 1) * LANES)
                if latent:
                    pair = vv[:, cs].T.astype(BF16)
                    for half in range(2):
                        r = (2 * i + half) * VT_ROWS
                        vt[r:r + HD, rows] = pair[half * HD:(half + 1) * HD, :]
                else:
                    vlo[rows, cs] = _half_values(vv[:, cs], 0)
                    vhi[rows, cs] = _half_values(vv[:, cs], 1)
        if latent:
            for h in range(MLA_HEADS):
                vt[h * VT_ROWS + HD:(h + 1) * VT_ROWS, :] = jnp.ones((SEG, n), BF16)

    r0 = pl.multiple_of(qi * tq, tq)
    lo = lax.broadcasted_iota(I32, (1, LANES), 1) < HD
    for i in range(MLA_HEADS // 2):
        halves = []
        for half in range(2):
            h = 2 * i + half
            cs = slice(h * LANES, (h + 1) * LANES)
            s = _dot_nt(q_ref[pl.ds(r0, tq), cs], kf[:, cs])
            m = jnp.max(s, axis=-1, keepdims=True)
            e = jnp.exp2(s - m)
            if latent:
                ot = _dot_nt(vt[h * VT_ROWS:(h + 1) * VT_ROWS, :], e.astype(BF16))
                halves.append(ot[0:HD, :] * (1.0 / ot[HD:HD + 1, :]))
            else:
                vsel = vlo if half == 0 else vhi
                o = _dot(e.astype(BF16), vsel[:, i * LANES:(i + 1) * LANES])
                halves.append(o * (1.0 / jnp.sum(e, axis=-1, keepdims=True)))
        if latent:
            tile = jnp.concatenate(halves, axis=0).T
        else:
            tile = jnp.where(lo, halves[0], halves[1])
        o_ref[:, i * LANES:(i + 1) * LANES] = tile.astype(BF16)


def _attn_odd_call(q, ckv, kr, caches, j, wk, wv, nbatch, seq, past):
    n = past + seq
    latent = past > 0
    tq = _query_tile(past)
    in_specs = [pl.BlockSpec((seq, MLA_QW), lambda b, qq: (b, 0))]
    if latent:
        in_specs += [pl.BlockSpec((seq, MLA_KV_RANK), lambda b, qq: (b, 0)),
                     pl.BlockSpec((seq, LANES), lambda b, qq: (b, 0))]
    else:
        last = ckv.shape[1] - 1
        in_specs += [pl.BlockSpec((None, None, seq, MLA_KV_RANK), lambda b, qq: (b, last, 0, 0)),
                     pl.BlockSpec((None, None, seq, LANES), lambda b, qq: (b, last, 0, 0))]
    args = [q, ckv, kr]
    if latent:
        in_specs += [pl.BlockSpec((None, None, past, MLA_KV_RANK), lambda b, qq: (b, j, 0, 0)),
                     pl.BlockSpec((None, None, past, LANES), lambda b, qq: (b, j, 0, 0))]
        args += list(caches)
    in_specs += [pl.BlockSpec((None, MLA_KV_RANK, MLA_QW), lambda b, qq: (j, 0, 0)),
                 pl.BlockSpec((None, MLA_KV_RANK, D), lambda b, qq: (j, 0, 0))]
    args += [wk, wv]
    return pl.pallas_call(
        functools.partial(_attn_odd_kernel, seq=seq, past=past),
        grid=(nbatch, seq // tq),
        in_specs=in_specs,
        out_specs=pl.BlockSpec((tq, D), lambda b, qq: (b * (seq // tq) + qq, 0)),
        out_shape=jax.ShapeDtypeStruct((nbatch * seq, D), BF16),
        scratch_shapes=([pltpu.VMEM((n, MLA_QW), BF16), pltpu.VMEM((MLA_HEADS * VT_ROWS, n), BF16)]
                        if latent else
                        [pltpu.VMEM((n, MLA_QW), BF16), pltpu.VMEM((n, D), BF16),
                         pltpu.VMEM((n, D), BF16)]),
        compiler_params=_cparams(("arbitrary", "arbitrary")),
        name="attn_odd_lat" if latent else "attn_odd_ctx",
    )(*args)


def _outproj_kernel(xa_ref, xb_ref, oa_ref, ob_ref, w_ref, mod_ref, g_ref, wr_ref,
                    xo_ref, h_ref, aff_ref, *, ntile_a):
    def run(x_ref, o_ref):
        x = x_ref[...] + mod_ref[2] * _dot(o_ref[...], w_ref[...])
        xo_ref[...] = x
        h = _modulate(x, g_ref[...], mod_ref[3], mod_ref[4]).astype(BF16)
        h_ref[...] = h
        logits = _dot(h, wr_ref[...])
        lane = lax.broadcasted_iota(I32, (TM, LANES), 1)
        lg = jnp.where(lane < N_EXPERTS, logits, -jnp.inf)
        e = jnp.exp(lg - jnp.max(lg, axis=-1, keepdims=True))
        aff = e / jnp.sum(e, axis=-1, keepdims=True)
        for c in range(TM // TB):
            aff_ref[c] = aff[c * TB:(c + 1) * TB, :].T[0:N_EXPERTS, :]

    @pl.when(pl.program_id(0) < ntile_a)
    def _():
        run(xa_ref, oa_ref)

    @pl.when(pl.program_id(0) >= ntile_a)
    def _():
        run(xb_ref, ob_ref)


def _outproj_call(xa, xb, xb_off, o_a, o_b, w, widx, mods, layer, gains, w_router, modrow):
    na, nb_ = o_a.shape[0], o_b.shape[0]
    t = na + nb_
    assert TM % TB == 0
    return pl.pallas_call(
        functools.partial(_outproj_kernel, ntile_a=na // TM),
        grid=(t // TM,),
        in_specs=[*_two_part_specs(na, nb_, D, xb_off), *_two_part_specs(na, nb_, D),
                  pl.BlockSpec((None, D, D), lambda r: (widx, 0, 0)),
                  pl.BlockSpec((None, None, 6, 1, D), lambda r: (layer, modrow(r), 0, 0, 0)),
                  pl.BlockSpec((None, 1, D), lambda r: (layer, 0, 0)),
                  pl.BlockSpec((None, D, LANES), lambda r: (layer, 0, 0))],
        out_specs=[pl.BlockSpec((TM, D), lambda r: (r, 0)),
                   pl.BlockSpec((TM, D), lambda r: (r, 0)),
                   pl.BlockSpec((TM // TB, N_EXPERTS, TB), lambda r: (r, 0, 0))],
        out_shape=[jax.ShapeDtypeStruct((t, D), F32), jax.ShapeDtypeStruct((t, D), BF16),
                   jax.ShapeDtypeStruct((t // TB, N_EXPERTS, TB), F32)],
        compiler_params=_cparams(("parallel",)),
        name="outproj",
    )(xa, xb, o_a, o_b, w, mods, gains, w_router)


def _select_kernel(aff_ref, rowid_ref, gate_ref, tab_ref, *, nb, cap):
    ne = N_EXPERTS
    nr = nb * ne
    a = aff_ref[...].reshape(nr, TB)
    ri = lax.broadcasted_iota(I32, (nr, nr), 0)
    ci = lax.broadcasted_iota(I32, (nr, nr), 1)
    same_e = (ri & (ne - 1)) == (ci & (ne - 1))
    same_b = (ri >> 4) == (ci >> 4)
    m_e = jnp.where(same_e, 1.0, 0.0).astype(BF16)
    m_b = jnp.where(same_b, 1.0, 0.0).astype(BF16)
    m_a = jnp.where(jnp.logical_and(same_e, ci < ri), 1.0, 0.0).astype(BF16)
    m_o = jnp.where(jnp.logical_and(same_b, ci < ri), 1.0, 0.0).astype(BF16)
    ui = lax.broadcasted_iota(I32, (TB, TB), 0)
    uj = lax.broadcasted_iota(I32, (TB, TB), 1)
    upper = jnp.where(ui < uj, 1.0, 0.0).astype(BF16)

    def rows_to_lanes(col):
        return jnp.broadcast_to(col, (nr, LANES)).astype(BF16)

    wide = jnp.concatenate([aff_ref[b] for b in range(nb)], axis=1)

    def count_ge(value):
        return jnp.sum(jnp.where(wide >= value, 1.0, 0.0), axis=-1, keepdims=True)

    def bisect(i, v):
        cand = v | jnp.left_shift(jnp.int32(1), 30 - i)
        return jnp.where(count_ge(pltpu.bitcast(cand, F32)) >= cap, cand, v)

    thr = lax.fori_loop(0, 31, bisect, jnp.zeros((ne, 1), I32))

    def refine(i, lohi):
        lo_e, hi_e = lohi
        mid = 0.5 * (lo_e + hi_e)
        take = count_ge(mid) >= cap
        return jnp.where(take, mid, lo_e), jnp.where(take, hi_e, mid)

    lo_e, hi_e = lax.fori_loop(0, TIE_STEPS, refine,
                               (pltpu.bitcast(thr, F32), pltpu.bitcast(thr + 1, F32)))
    lo_v = jnp.concatenate([lo_e] * nb, axis=0)
    hi_v = jnp.concatenate([hi_e] * nb, axis=0)
    gt = jnp.where(a >= hi_v, 1.0, 0.0)
    eq = jnp.where(jnp.logical_and(a >= lo_v, a < hi_v), 1.0, 0.0)
    n_gt = _dot(m_e, rows_to_lanes(jnp.sum(gt, axis=-1, keepdims=True)))[:, 0:1]
    need = cap - n_gt
    eq_before = (_dot(m_a, rows_to_lanes(jnp.sum(eq, axis=-1, keepdims=True)))[:, 0:1]
                 + _dot(eq.astype(BF16), upper))
    sel = jnp.where(jnp.logical_and(eq > 0.0, eq_before < need), 1.0, gt)
    local = _dot(sel.astype(BF16), upper)
    cnt = jnp.sum(sel, axis=-1, keepdims=True)
    seg = jnp.floor((cnt + (SEG - 1)) * (1.0 / SEG)) * SEG
    segb = rows_to_lanes(seg)
    over = jnp.maximum(seg - WINR, 0.0)
    overb = rows_to_lanes(over)
    off_over = FIRST_ROWS + _dot(m_o, overb)[:, 0:1]
    off_buf = _dot(m_a, segb)[:, 0:1]
    over_blk = _dot(m_b, overb)[:, 0:1]
    rows_exp = _dot(m_e, segb)[:, 0:1]
    tiles_exp = jnp.floor((rows_exp + (TF - 1)) / TF)
    expert = (lax.broadcasted_iota(I32, (nr, 1), 0) & (ne - 1)).astype(F32)
    row = jnp.where(local < WINR, expert * WINR + local, off_over + local - WINR)
    rowid_ref[...] = jnp.where(sel > 0.0, row, -1.0).astype(I32).reshape(nb, ne, TB)
    gate_ref[...] = jnp.where(sel > 0.0, a, 0.0).reshape(nb, ne, TB)
    tl = lax.broadcasted_iota(I32, (nr, LANES), 1)
    tab = jnp.where(tl == 0, seg, jnp.where(tl == 1, off_over, jnp.where(
        tl == 2, off_buf, jnp.where(tl == 3, over_blk, jnp.where(tl == 4, rows_exp, tiles_exp)))))
    tab_ref[...] = tab.T[0:8, :].astype(I32)


def _select_call(aff, ngroups, ntok):
    nb = ntok // TB
    cap = EC_FACTOR * ntok // N_EXPERTS
    nr = nb * N_EXPERTS
    return pl.pallas_call(
        functools.partial(_select_kernel, nb=nb, cap=cap),
        grid=(ngroups,),
        in_specs=[pl.BlockSpec((nb, N_EXPERTS, TB), lambda g: (g, 0, 0))],
        out_specs=[pl.BlockSpec((None, nb, N_EXPERTS, TB), lambda g: (g, 0, 0, 0)),
                   pl.BlockSpec((None, nb, N_EXPERTS, TB), lambda g: (g, 0, 0, 0)),
                   pl.BlockSpec((None, 8, nr), lambda g: (g, 0, 0))],
        out_shape=[jax.ShapeDtypeStruct((ngroups, nb, N_EXPERTS, TB), I32),
                   jax.ShapeDtypeStruct((ngroups, nb, N_EXPERTS, TB), F32),
                   jax.ShapeDtypeStruct((ngroups, 8, nr), I32)],
        compiler_params=_cparams(("arbitrary",)),
        name="select",
    )(aff)


STACK_ROWS = N_EXPERTS * TB


FIRST_ROWS = 3 * TB


WINR = 3 * SEG
assert N_EXPERTS * WINR == FIRST_ROWS
TF = 672
TIE_STEPS = 12


def _window_rows(rowid_ref, dst, value_ref=None):
    for e in range(N_EXPERTS):
        rid_e = rowid_ref[e:e + 1, :]
        val_e = 1.0 if value_ref is None else value_ref[e:e + 1, :]
        hit = rid_e == lax.broadcasted_iota(I32, (WINR, TB), 0) + e * WINR
        dst[e * WINR:(e + 1) * WINR, :] = jnp.where(hit, val_e, 0.0).astype(dst.dtype)


class _Table:
    def __init__(self, tab_s, row, nb, per_block=False):
        self.tab_s, self.row, self.per = tab_s, row, nb if per_block else nb * N_EXPERTS
        self.scale = N_EXPERTS if per_block else 1

    def __getitem__(self, k):
        if self.per & (self.per - 1) == 0:
            hi, low = lax.shift_right_logical(k, self.per.bit_length() - 1), k & (self.per - 1)
        else:
            hi, low = lax.div(k, self.per), lax.rem(k, self.per)
        return self.tab_s[hi, self.row, low * self.scale]


def _tables(tab_s, nb):
    return (_Table(tab_s, 0, nb), _Table(tab_s, 1, nb), _Table(tab_s, 2, nb),
            _Table(tab_s, 3, nb, per_block=True))


def _overflow_groups(seg_s, k):
    return lax.div(jnp.maximum(seg_s[k] - WINR, 0), SEG)


def _overflow_rows(seg_s, offo_s, step, over, rowid_ref, dst, value_ref=None):
    def zero(i, carry):
        r0 = pl.multiple_of(FIRST_ROWS + i * SEG, SEG)
        dst[pl.ds(r0, SEG), :] = jnp.zeros((SEG, TB), dst.dtype)
        return carry

    lax.fori_loop(0, ((over + TB - 1) // TB) * (TB // SEG), zero, 0)
    for e in range(N_EXPERTS):
        k = step * N_EXPERTS + e
        rid_e = rowid_ref[e:e + 1, :]
        val_e = 1.0 if value_ref is None else value_ref[e:e + 1, :]

        def group(i, carry, off=offo_s[k], rid_e=rid_e, val_e=val_e):
            r0 = pl.multiple_of(off + i * SEG, SEG)
            hit = rid_e == lax.broadcasted_iota(I32, (SEG, TB), 0) + r0
            dst[pl.ds(r0, SEG), :] = jnp.where(hit, val_e, 0.0).astype(dst.dtype)
            return carry

        lax.fori_loop(0, _overflow_groups(seg_s, k), group, 0)


def _wait_rows(rows, make_copy):
    def big(i, carry):
        make_copy(TB).wait()
        return carry

    def small(i, carry):
        make_copy(SEG).wait()
        return carry

    lax.fori_loop(0, lax.div(rows, TB), big, 0)
    lax.fori_loop(0, lax.div(lax.rem(rows, TB), SEG), small, 0)


def _dispatch_kernel(tab_s, h_ref, rowid_ref, xe_hbm,
                     onehot, stack, zbuf, sem, zsem, *, nb, nsteps, cap):
    g = pl.program_id(0)
    b = pl.program_id(1)
    step = g * nb + b
    slot = lax.rem(step, 2)
    seg_s, offo_s, offb_s, over_s = _tables(tab_s, nb)
    over = over_s[step]
    xrows = xe_hbm.shape[2]

    def wait_slot(nrows, sl):
        _wait_rows(nrows, lambda n: pltpu.make_async_copy(
            stack.at[sl, pl.ds(0, n)], xe_hbm.at[0, 0, pl.ds(0, n)], sem.at[sl]))

    @pl.when(step == 0)
    def _init():
        stack[...] = jnp.zeros_like(stack)

    @pl.when(b == 0)
    def _zero_unused():
        zbuf[...] = jnp.zeros_like(zbuf)
        for e in range(N_EXPERTS):
            pltpu.make_async_copy(zbuf, xe_hbm.at[g, e, pl.ds(cap, xrows - cap)], zsem).start()

    _window_rows(rowid_ref, onehot)
    h = h_ref[...]
    stack[slot, 0:FIRST_ROWS, :] = _dot(onehot[0:FIRST_ROWS, :], h).astype(BF16)

    @pl.when(over > 0)
    def _overflow():
        _overflow_rows(seg_s, offo_s, step, over, rowid_ref, onehot)

        def chunk(c, carry):
            base = pl.multiple_of(FIRST_ROWS + c * TB, TB)
            stack[slot, pl.ds(base, TB), :] = _dot(onehot[pl.ds(base, TB), :], h).astype(BF16)
            return carry

        lax.fori_loop(0, (over + TB - 1) // TB, chunk, 0)

    @pl.when(b == 0)
    def _zero_unused_done():
        for e in range(N_EXPERTS):
            pltpu.make_async_copy(zbuf, xe_hbm.at[g, e, pl.ds(cap, xrows - cap)], zsem).wait()

    @pl.when(step >= 1)
    def _previous_landed():
        wait_slot(FIRST_ROWS + over_s[step - 1], 1 - slot)

    for e in range(N_EXPERTS):
        k = step * N_EXPERTS + e
        pltpu.make_async_copy(
            stack.at[slot, e * WINR:(e + 1) * WINR],
            xe_hbm.at[g, e, pl.ds(pl.multiple_of(offb_s[k], SEG), WINR)], sem.at[slot]).start()

    @pl.when(over > 0)
    def _overflow_copies():
        for e in range(N_EXPERTS):
            k = step * N_EXPERTS + e

            def one(i, carry, e=e, k=k):
                pltpu.make_async_copy(
                    stack.at[slot, pl.ds(pl.multiple_of(offo_s[k] + i * SEG, SEG), SEG)],
                    xe_hbm.at[g, e, pl.ds(pl.multiple_of(offb_s[k] + WINR + i * SEG, SEG), SEG)],
                    sem.at[slot]).start()
                return carry

            lax.fori_loop(0, _overflow_groups(seg_s, k), one, 0)

    @pl.when(step == nsteps - 1)
    def _drain():
        wait_slot(FIRST_ROWS + over, slot)


def _expert_rows(ntok):
    cap = EC_FACTOR * ntok // N_EXPERTS
    worst = cap + (ntok // TB) * (SEG - 1)
    tiles = -(-(worst + WINR) // TF)
    assert cap % SEG == 0 and cap >= WINR and tiles * TF > cap
    return cap, tiles


def _dispatch_call(tab, h, rowid, ngroups, ntok):
    nb = ntok // TB
    cap, tiles = _expert_rows(ntok)
    xrows = tiles * TF
    grid_spec = pltpu.PrefetchScalarGridSpec(
        num_scalar_prefetch=1,
        grid=(ngroups, nb),
        in_specs=[pl.BlockSpec((TB, D), lambda g, b, *_: (g * nb + b, 0)),
                  pl.BlockSpec((None, None, N_EXPERTS, TB), lambda g, b, *_: (g, b, 0, 0))],
        out_specs=pl.BlockSpec(memory_space=pl.ANY),
        scratch_shapes=[pltpu.VMEM((STACK_ROWS, TB), BF16), pltpu.VMEM((2, STACK_ROWS, D), BF16),
                        pltpu.VMEM((xrows - cap, D), BF16),
                        pltpu.SemaphoreType.DMA((2,)), pltpu.SemaphoreType.DMA])
    return pl.pallas_call(
        functools.partial(_dispatch_kernel, nb=nb, nsteps=ngroups * nb, cap=cap),
        grid_spec=grid_spec,
        out_shape=jax.ShapeDtypeStruct((ngroups, N_EXPERTS, xrows, D), BF16),
        compiler_params=_cparams(("arbitrary", "arbitrary")),
        name="dispatch",
    )(tab, h, rowid)


def _ffn_kernel(tab_s, xe_ref, wg_hbm, wu_hbm, wd_hbm, y_ref, wg32, wu32, wd32, wgb, wub, wdb, sem,
                *, layer):
    e = pl.program_id(0)
    g = pl.program_id(1)
    j = pl.program_id(2)

    def weight_copies(ee, sl):
        return (pltpu.make_async_copy(wg_hbm.at[layer, ee], wg32.at[sl], sem.at[sl]),
                pltpu.make_async_copy(wu_hbm.at[layer, ee], wu32.at[sl], sem.at[sl]),
                pltpu.make_async_copy(wd_hbm.at[layer, ee], wd32.at[sl], sem.at[sl]))

    @pl.when(jnp.logical_and(g == 0, j == 0))
    def _weights():
        sl = lax.rem(e, 2)

        @pl.when(e == 0)
        def _():
            for cp in weight_copies(e, sl):
                cp.start()

        @pl.when(e + 1 < N_EXPERTS)
        def _():
            for cp in weight_copies(e + 1, 1 - sl):
                cp.start()

        for cp in weight_copies(e, sl):
            cp.wait()
        wgb[...] = wg32[sl].astype(BF16)
        wub[...] = wu32[sl].astype(BF16)
        wdb[...] = wd32[sl].astype(BF16)

    live = j < tab_s[g, 5, e]

    @pl.when(live)
    def _run():
        x = xe_ref[...]
        hid = (_silu(_dot(x, wgb[...])) * _dot(x, wub[...])).astype(BF16)
        y_ref[...] = _dot(hid, wdb[...]).astype(BF16)

    @pl.when(jnp.logical_not(live))
    def _skip():
        y_ref[...] = jnp.zeros_like(y_ref)


def _ffn_call(tab, xe, wg, wu, wd, layer, ngroups, ntok):
    _, tiles = _expert_rows(ntok)

    def xmap(e, g, j, tab_s):
        return (g, e, jnp.minimum(j, tab_s[g, 5, e] - 1), 0)

    grid_spec = pltpu.PrefetchScalarGridSpec(
        num_scalar_prefetch=1,
        grid=(N_EXPERTS, ngroups, tiles),
        in_specs=[pl.BlockSpec((None, None, TF, D), xmap),
                  pl.BlockSpec(memory_space=pl.ANY), pl.BlockSpec(memory_space=pl.ANY),
                  pl.BlockSpec(memory_space=pl.ANY)],
        out_specs=pl.BlockSpec((None, None, TF, D), lambda e, g, j, nt: (g, e, j, 0)),
        scratch_shapes=[pltpu.VMEM((2, D, EXPERT_FF), F32), pltpu.VMEM((2, D, EXPERT_FF), F32),
                        pltpu.VMEM((2, EXPERT_FF, D), F32),
                        pltpu.VMEM((D, EXPERT_FF), BF16), pltpu.VMEM((D, EXPERT_FF), BF16),
                        pltpu.VMEM((EXPERT_FF, D), BF16), pltpu.SemaphoreType.DMA((2,))])
    return pl.pallas_call(
        functools.partial(_ffn_kernel, layer=layer),
        grid_spec=grid_spec,
        out_shape=jax.ShapeDtypeStruct((ngroups, N_EXPERTS, tiles * TF, D), BF16),
        compiler_params=_cparams(("arbitrary", "arbitrary", "arbitrary")),
        name="ffn",
    )(tab, xe, wg, wu, wd)


def _combine_kernel(tab_s, y_hbm, rowid_ref, gate_ref, x_ref, mod_ref,
                    modn_ref, g_ref, xo_ref, h_ref, weights, stack, acc, sem, *, nb, nsteps, final):
    g = pl.program_id(0)
    b = pl.program_id(1)
    step = g * nb + b
    slot = lax.rem(step, 2)
    seg_s, offo_s, offb_s, over_s = _tables(tab_s, nb)
    over = over_s[step]

    def fetch(st, sl):
        gg = lax.div(st, nb)
        for e in range(N_EXPERTS):
            k = st * N_EXPERTS + e
            pltpu.make_async_copy(
                y_hbm.at[gg, e, pl.ds(pl.multiple_of(offb_s[k], SEG), WINR)],
                stack.at[sl, e * WINR:(e + 1) * WINR], sem.at[sl]).start()

        @pl.when(over_s[st] > 0)
        def _():
            for e in range(N_EXPERTS):
                k = st * N_EXPERTS + e

                def one(i, carry, e=e, k=k):
                    pltpu.make_async_copy(
                        y_hbm.at[gg, e, pl.ds(pl.multiple_of(offb_s[k] + WINR + i * SEG, SEG), SEG)],
                        stack.at[sl, pl.ds(pl.multiple_of(offo_s[k] + i * SEG, SEG), SEG)],
                        sem.at[sl]).start()
                    return carry

                lax.fori_loop(0, _overflow_groups(seg_s, k), one, 0)

    @pl.when(step == 0)
    def _first():
        stack[...] = jnp.zeros_like(stack)
        fetch(step, slot)

    _window_rows(rowid_ref, weights, gate_ref)

    @pl.when(over > 0)
    def _():
        _overflow_rows(seg_s, offo_s, step, over, rowid_ref, weights, gate_ref)

    if nsteps > 1:
        @pl.when(step + 1 < nsteps)
        def _prefetch():
            fetch(step + 1, 1 - slot)

    _wait_rows(FIRST_ROWS + over, lambda n: pltpu.make_async_copy(
        y_hbm.at[0, 0, pl.ds(0, n)], stack.at[slot, pl.ds(0, n)], sem.at[slot]))

    def token_weights(base):
        return weights[pl.ds(base, TB), :].T.astype(BF16)

    w = jnp.concatenate([token_weights(c * TB) for c in range(FIRST_ROWS // TB)], axis=1)
    acc[...] = _dot(w, stack[slot, 0:FIRST_ROWS, :])

    @pl.when(over > 0)
    def _():
        def chunk(c, carry):
            base = pl.multiple_of(FIRST_ROWS + c * TB, TB)
            acc[...] += _dot(token_weights(base), stack[slot, pl.ds(base, TB), :])
            return carry

        lax.fori_loop(0, (over + TB - 1) // TB, chunk, 0)

    x = x_ref[...] + mod_ref[5] * acc[...]
    if final:
        y = _rms(x, g_ref[...])

        @pl.when(g == 0)
        def _():
            xo_ref[...] = y

        @pl.when(g != 0)
        def _():
            h_ref[...] = y
    else:
        xo_ref[...] = x
        h_ref[...] = _modulate(x, g_ref[...], modn_ref[0], modn_ref[1]).astype(BF16)


def _combine_call(tab, y, rowid, gate, x, mods, layer, nxt, gains, gidx, modrow, ngroups, ntok, final):
    nb = ntok // TB
    t = x.shape[0]
    if final:
        assert ngroups == 2
        out_specs = [pl.BlockSpec((TB, D), lambda g, b, *_: (jnp.where(g == 0, b, nb - 1), 0)),
                     pl.BlockSpec((TB, D), lambda g, b, *_: (jnp.where(g == 0, 0, b), 0))]
        out_shape = [jax.ShapeDtypeStruct((ntok, D), F32), jax.ShapeDtypeStruct((ntok, D), F32)]
    else:
        out_specs = [pl.BlockSpec((TB, D), lambda g, b, *_: (g * nb + b, 0)),
                     pl.BlockSpec((TB, D), lambda g, b, *_: (g * nb + b, 0))]
        out_shape = [jax.ShapeDtypeStruct((t, D), F32), jax.ShapeDtypeStruct((t, D), BF16)]
    grid_spec = pltpu.PrefetchScalarGridSpec(
        num_scalar_prefetch=1,
        grid=(ngroups, nb),
        in_specs=[pl.BlockSpec(memory_space=pl.ANY),
                  pl.BlockSpec((None, None, N_EXPERTS, TB), lambda g, b, *_: (g, b, 0, 0)),
                  pl.BlockSpec((None, None, N_EXPERTS, TB), lambda g, b, *_: (g, b, 0, 0)),
                  pl.BlockSpec((TB, D), lambda g, b, *_: (g * nb + b, 0)),
                  pl.BlockSpec((None, None, 6, 1, D),
                               lambda g, b, *_: (layer, modrow(g * nb + b), 0, 0, 0)),
                  pl.BlockSpec((None, None, 6, 1, D),
                               lambda g, b, *_: (nxt, modrow(g * nb + b), 0, 0, 0)),
                  pl.BlockSpec((None, 1, D), lambda g, b, *_: (gidx, 0, 0))],
        out_specs=out_specs,
        scratch_shapes=[pltpu.VMEM((STACK_ROWS, TB), F32), pltpu.VMEM((2, STACK_ROWS, D), BF16),
                        pltpu.VMEM((TB, D), F32), pltpu.SemaphoreType.DMA((2,))])
    return pl.pallas_call(
        functools.partial(_combine_kernel, nb=nb, nsteps=ngroups * nb, final=final),
        grid_spec=grid_spec,
        out_shape=out_shape,
        compiler_params=_cparams(("arbitrary", "arbitrary")),
        name="combine",
    )(tab, y, rowid, gate, x, mods, mods, gains)


def kernel(x_prompt, x_sample, cache_diff_k, cache_diff_v, cache_swa_k, cache_swa_v, cache_mla_ckv, cache_mla_krope, c, c_ctx, w_ada, b_ada, norm_mix, norm_ffn, w_in_even, w_out_even, diff_lambda, diff_subln, swa_sink, w_in_odd, mla_q_norm, w_q_up, mla_kv_norm, w_kv_up, w_out_odd, w_router, w_gate_exp, w_up_exp, w_down_exp, final_norm):
    batch, seq, _ = x_prompt.shape
    dec_batch, dec_seq, _ = x_sample.shape
    past = cache_diff_k.shape[2]
    depth = w_ada.shape[0]
    n_even = w_in_even.shape[0]
    n_odd = w_in_odd.shape[0]
    nc, ns = batch * seq, dec_batch * dec_seq
    assert nc == ns, "the routed-expert kernels take two token groups of equal size"
    assert nc % TM == 0 and dec_seq % TM == 0 and past % 256 == 0 and dec_seq % GRID_W == 0
    assert seq % TQ == 0 and dec_seq % TQ_LATENT == 0
    ntok = nc

    def modrow_of(tile):
        def modrow(r):
            tok = r * tile
            return jnp.where(tok < nc, 0, 1 + jnp.maximum(tok - nc, 0) // dec_seq)
        return modrow

    modrow, modrow_tb = modrow_of(TM), modrow_of(TB)

    rc = -(-(1 + dec_batch) // 16) * 16
    cvec = jnp.zeros((rc, D), F32).at[0].set(c_ctx).at[1:1 + dec_batch].set(c)
    mods = _ada_call(cvec, w_ada, b_ada).reshape(depth, rc, 6, 1, D)

    w_even_b = w_in_even.astype(BF16)
    w_oute_b = w_out_even.astype(BF16)
    w_outo_b = w_out_odd.astype(BF16)
    kr_pad = jnp.zeros((n_odd, D, LANES), F32).at[:, :, 64:96].set(w_in_odd[:, :, 640:672])
    w_odd_b = jnp.concatenate([w_in_odd[:, :, :640], kr_pad], axis=-1).astype(BF16)
    wq = w_q_up.reshape(n_odd, MLA_Q_RANK, MLA_HEADS, HD + MLA_ROPE)
    wq_b = jnp.pad(wq, ((0, 0), (0, 0), (0, 0), (0, LANES - HD - MLA_ROPE))).reshape(
        n_odd, MLA_Q_RANK, MLA_QW).astype(BF16)
    wkv = w_kv_up.reshape(n_odd, MLA_KV_RANK, MLA_HEADS, 2 * HD)
    wk_b = jnp.pad(wkv[..., :HD], ((0, 0), (0, 0), (0, 0), (0, LANES - HD))).reshape(
        n_odd, MLA_KV_RANK, MLA_QW).astype(BF16)
    wv_b = wkv[..., HD:].reshape(n_odd, MLA_KV_RANK, D).astype(BF16)
    w_router_b = jnp.pad(w_router, ((0, 0), (0, 0), (0, LANES - N_EXPERTS))).astype(BF16)
    even_tabs = _rope_tables(dec_seq, 16, _even_lane)
    mla_tabs = _rope_tables(dec_seq, 8, _mla_lane)
    ckr = jnp.zeros((dec_batch, n_odd, past, LANES), F32).at[..., 64:96].set(cache_mla_krope)
    even_caches = (cache_diff_k, cache_diff_v, cache_swa_k, cache_swa_v)

    mix_gains = jnp.concatenate([norm_mix, final_norm[None]], axis=0).reshape(depth + 1, 1, D)
    ffn_gains = norm_ffn.reshape(depth, 1, D)
    q_gains = mla_q_norm.reshape(n_odd, 1, MLA_Q_RANK)
    kv_gains = mla_kv_norm.reshape(n_odd, 1, MLA_KV_RANK)
    sublns = diff_subln.reshape(n_even, 1, 2 * HD)

    xa, xb, xb_off = x_prompt.reshape(nc, D), x_sample.reshape(ns, D), 0
    h = _norm_mod_call(xa, xb, mods, 0, mix_gains, modrow)
    even_state = odd_state = None
    y_prompt = y_sample = None
    for i in range(depth):
        j = i // 2
        if i % 2 == 0:
            qkv_c, *even_state = _proj_even_call(h, w_even_b, j, 0, nc, None, dec_seq,
                                                 prev=even_state, seq=seq)
            (qkv_l,) = _proj_even_call(h, w_even_b, j, nc, ns, even_tabs, dec_seq)
            li = _lambda_init(i)
            o_c = _attn_even_call(qkv_c, None, j, diff_lambda, sublns, swa_sink, batch, seq, 0, li)
            o_l = _attn_even_call(qkv_l, even_caches, j, diff_lambda, sublns, swa_sink,
                                  dec_batch, dec_seq, past, li)
            w_out = w_oute_b
        else:
            q_c, *odd_state = _proj_odd_call(h, w_odd_b, q_gains, wq_b, kv_gains, j, 0, nc, None,
                                             dec_seq, prev=odd_state, seq=seq)
            q_l, ckv_l, kr_l = _proj_odd_call(h, w_odd_b, q_gains, wq_b, kv_gains, j, nc, ns,
                                              mla_tabs, dec_seq)
            o_c = _attn_odd_call(q_c, odd_state[0], odd_state[1], None, j, wk_b, wv_b, batch, seq, 0)
            o_l = _attn_odd_call(q_l, ckv_l, kr_l, (cache_mla_ckv, ckr), j, wk_b, wv_b,
                                 dec_batch, dec_seq, past)
            w_out = w_outo_b
        x, h2, aff = _outproj_call(xa, xb, xb_off, o_c, o_l, w_out, j, mods, i, ffn_gains,
                                   w_router_b, modrow)
        rowid, gate, tab = _select_call(aff, 2, ntok)
        xe = _dispatch_call(tab, h2, rowid, 2, ntok)
        y = _ffn_call(tab, xe, w_gate_exp, w_up_exp, w_down_exp, i, 2, ntok)
        final = i == depth - 1
        nxt = i if final else i + 1
        out_a, out_b = _combine_call(tab, y, rowid, gate, x, mods, i, nxt, mix_gains,
                                     depth if final else nxt, modrow_tb, 2, ntok, final)
        if final:
            y_prompt = out_a.reshape(batch, seq, D)
            y_sample = out_b.reshape(dec_batch, dec_seq, D)
        else:
            x, h = out_a, out_b
            xa, xb, xb_off = x, x, nc // TM

    kd, vd, ks, vs = even_state
    ckv_new, kr_new = odd_state
    return (y_prompt, y_sample,
            kd.reshape(batch, n_even, seq, DIFF_HEADS, 2 * HD),
            vd.reshape(batch, n_even, seq, DIFF_HEADS, 2 * HD),
            ks.reshape(batch, n_even, seq, 2, HD), vs.reshape(batch, n_even, seq, 2, HD),
            ckv_new, kr_new[..., 64:96])
```
